```python
import math
import jax, jax.numpy as jnp
from jax import lax
import numpy as np

D_MODEL = 2048
BATCH = 4
SEQ = 2048
DEPTH = 1
DEC_BATCH = 128
DEC_SEQ = 8
PAST_LEN = 8192
PAGE_SIZE = 128

D_CONV = D_MODEL // 2
CONV_WIDTH = 3
HEAD_DIM = 64
N_HEADS = (D_MODEL // 2) // HEAD_DIM
N_KV_HEADS = N_HEADS // 4
GROUP = N_HEADS // N_KV_HEADS
D_ATTN = N_HEADS * HEAD_DIM
D_KV = N_KV_HEADS * HEAD_DIM
WINDOW = 128
NUM_BUCKETS = 32
MAX_DISTANCE = 128
D_FF = -(-8 * D_MODEL // (3 * 256)) * 256
EPS = 1e-6
SPLIT_SIZES = (D_CONV, D_CONV, D_CONV, D_ATTN, D_KV, D_KV, D_MODEL, D_MODEL)
D_IN_PROJ = 3 * D_CONV + D_ATTN + 2 * D_KV + 2 * D_MODEL

kernel_name = 'hybrid_shortconv_swa_sink_decoder_step'


def rms_norm(x, g):
    xf = x.astype(jnp.float32)
    y = xf * lax.rsqrt(jnp.mean(xf * xf, axis=-1, keepdims=True) + EPS)
    return (y * g.astype(jnp.float32)).astype(x.dtype)


def t5_bucket(dist):
    dist = jnp.maximum(dist, 0)
    max_exact = NUM_BUCKETS // 2
    ratio = jnp.log(jnp.maximum(dist, 1).astype(jnp.float32) / max_exact) / math.log(MAX_DISTANCE / max_exact)
    large = max_exact + (ratio * (NUM_BUCKETS - max_exact)).astype(jnp.int32)
    large = jnp.minimum(large, NUM_BUCKETS - 1)
    return jnp.where(dist < max_exact, dist, large)


def window_attend(q, k, v, dist, valid, rel_bias, sinks):
    s = jnp.einsum('...qhgd,...shd->...hgqs', q, k).astype(jnp.float32) * (HEAD_DIM ** -0.5)
    bias = rel_bias.astype(jnp.float32)[t5_bucket(dist)]
    bias = jnp.moveaxis(bias, -1, 0).reshape(N_KV_HEADS, GROUP, dist.shape[0], dist.shape[1])
    s = jnp.where(valid, s + bias, -jnp.inf)
    sink = sinks.astype(jnp.float32).reshape(N_KV_HEADS, GROUP, 1, 1)
    m = jnp.maximum(jnp.max(s, axis=-1, keepdims=True), sink)
    p = jnp.exp(s - m)
    p = p / (jnp.sum(p, axis=-1, keepdims=True) + jnp.exp(sink - m))
    return jnp.einsum('...hgqs,...shd->...qhgd', p.astype(v.dtype), v)


def prompt_attention(q, k, v, rel_bias, sinks):
    b, t = q.shape[0], q.shape[1]
    nb = t // WINDOW
    qb = q.reshape(b, nb, WINDOW, N_KV_HEADS, GROUP, HEAD_DIM)
    kc = k.reshape(b, nb, WINDOW, N_KV_HEADS, HEAD_DIM)
    vc = v.reshape(b, nb, WINDOW, N_KV_HEADS, HEAD_DIM)
    kb = jnp.concatenate([jnp.concatenate([jnp.zeros_like(kc[:, :1]), kc[:, :-1]], axis=1), kc], axis=2)
    vb = jnp.concatenate([jnp.concatenate([jnp.zeros_like(vc[:, :1]), vc[:, :-1]], axis=1), vc], axis=2)
    qi = jnp.arange(WINDOW)[:, None]
    kj = jnp.arange(2 * WINDOW)[None, :]
    dist = qi + WINDOW - kj
    in_win = (dist >= 0) & (dist < WINDOW)
    key_pos = jnp.arange(nb)[:, None, None] * WINDOW - WINDOW + kj[None]
    valid = (in_win[None] & (key_pos >= 0))[:, None, None]
    o = window_attend(qb, kb, vb, dist, valid, rel_bias, sinks)
    return o.reshape(b, t, D_ATTN)


def sample_attention(q, k, v, k_win, v_win, rel_bias, sinks):
    b, t = q.shape[0], q.shape[1]
    wc = k_win.shape[1]
    kf = jnp.concatenate([k_win.astype(k.dtype), k], axis=1)
    vf = jnp.concatenate([v_win.astype(v.dtype), v], axis=1)
    dist = jnp.arange(t)[:, None] + wc - jnp.arange(wc + t)[None, :]
    valid = (dist >= 0) & (dist < WINDOW)
    o = window_attend(q.reshape(b, t, N_KV_HEADS, GROUP, HEAD_DIM), kf, vf, dist, valid, rel_bias, sinks)
    return o.reshape(b, t, D_ATTN), kf[:, -wc:], vf[:, -wc:]


def short_conv(u, prev, w):
    t = u.shape[1]
    full = jnp.concatenate([prev.astype(u.dtype), u], axis=1)
    out = w[0] * full[:, 0:t]
    for j in range(1, CONV_WIDTH):
        out = out + w[j] * full[:, j:j + t]
    return out, full[:, -(CONV_WIDTH - 1):]


def decoder_layer(x, conv_prev, k_win, v_win, rel_bias, w_in, w_conv, w_conv_out, sinks,
                  w_attn_out, w_o, g_mix, g_ffn, w_gate, w_up, w_down):
    b, t = x.shape[0], x.shape[1]
    xn = rms_norm(x, g_mix)
    z = xn @ w_in
    parts = []
    start = 0
    for size in SPLIT_SIZES:
        parts.append(z[..., start:start + size])
        start += size
    b_gate, c_gate, h_conv, q, k, v, gate_c, gate_a = parts
    u = c_gate * h_conv
    if conv_prev is None:
        conv_prev = jnp.zeros((b, CONV_WIDTH - 1, D_CONV), x.dtype)
    conv_out, conv_state = short_conv(u, conv_prev, w_conv)
    y_conv = (b_gate * conv_out) @ w_conv_out
    k = k.reshape(b, t, N_KV_HEADS, HEAD_DIM)
    v = v.reshape(b, t, N_KV_HEADS, HEAD_DIM)
    if k_win is None:
        attn = prompt_attention(q, k, v, rel_bias, sinks)
        wc = min(WINDOW, t)
        k_state, v_state = k[:, t - wc:], v[:, t - wc:]
    else:
        attn, k_state, v_state = sample_attention(q, k, v, k_win, v_win, rel_bias, sinks)
    y_attn = attn @ w_attn_out
    merged = jax.nn.sigmoid(gate_c) * y_conv + jax.nn.sigmoid(gate_a) * y_attn
    h = x + merged @ w_o
    hn = rms_norm(h, g_ffn)
    h = h + (jax.nn.silu(hn @ w_gate) * (hn @ w_up)) @ w_down
    return h, conv_state, k_state, v_state


def setup_inputs(seed: int = 0) -> dict:
    key = jax.random.key(seed)
    ks = jax.random.split(key, 20)
    f32 = jnp.float32
    wc = min(WINDOW, PAST_LEN)

    def nrm(k, shape, scale):
        return jax.random.normal(k, shape, f32) * scale

    return {
        'x_prompt': nrm(ks[0], (BATCH, SEQ, D_MODEL), 1.0),
        'x_sample': nrm(ks[1], (DEC_BATCH, DEC_SEQ, D_MODEL), 1.0),
        'cache_k': nrm(ks[2], (DEPTH, DEC_BATCH, wc, N_KV_HEADS, HEAD_DIM), 1.0),
        'cache_v': nrm(ks[3], (DEPTH, DEC_BATCH, wc, N_KV_HEADS, HEAD_DIM), 1.0),
        'state_conv': nrm(ks[4], (DEPTH, DEC_BATCH, CONV_WIDTH - 1, D_CONV), 1.0),
        'rel_bias': nrm(ks[5], (NUM_BUCKETS, N_HEADS), 0.5),
        'w_in': nrm(ks[6], (DEPTH, D_MODEL, D_IN_PROJ), D_MODEL ** -0.5),
        'w_conv': nrm(ks[7], (DEPTH, CONV_WIDTH, D_CONV), CONV_WIDTH ** -0.5),
        'w_conv_out': nrm(ks[8], (DEPTH, D_CONV, D_MODEL), D_CONV ** -0.5),
        'sinks': nrm(ks[9], (DEPTH, N_HEADS), 0.5),
        'w_attn_out': nrm(ks[10], (DEPTH, D_ATTN, D_MODEL), D_ATTN ** -0.5),
        'w_o': nrm(ks[11], (DEPTH, D_MODEL, D_MODEL), D_MODEL ** -0.5),
        'g_mix': 1.0 + nrm(ks[12], (DEPTH, D_MODEL), 0.02),
        'g_ffn': 1.0 + nrm(ks[13], (DEPTH, D_MODEL), 0.02),
        'w_gate': nrm(ks[14], (DEPTH, D_MODEL, D_FF), D_MODEL ** -0.5),
        'w_up': nrm(ks[15], (DEPTH, D_MODEL, D_FF), D_MODEL ** -0.5),
        'w_down': nrm(ks[16], (DEPTH, D_FF, D_MODEL), D_FF ** -0.5),
        'g_final': 1.0 + nrm(ks[17], (D_MODEL,), 0.02),
    }


def reference(x_prompt, x_sample, cache_k, cache_v, state_conv, rel_bias, w_in, w_conv, w_conv_out,
              sinks, w_attn_out, w_o, g_mix, g_ffn, w_gate, w_up, w_down, g_final):
    hp, hs = x_prompt, x_sample
    kp_l, vp_l, cp_l, ks_l, vs_l, cs_l = [], [], [], [], [], []
    for l in range(DEPTH):
        lw = (w_in[l], w_conv[l], w_conv_out[l], sinks[l], w_attn_out[l], w_o[l],
              g_mix[l], g_ffn[l], w_gate[l], w_up[l], w_down[l])
        hp, cp, kp, vp = decoder_layer(hp, None, None, None, rel_bias, *lw)
        hs, cs, ksm, vsm = decoder_layer(hs, state_conv[l], cache_k[l], cache_v[l], rel_bias, *lw)
        kp_l.append(kp); vp_l.append(vp); cp_l.append(cp)
        ks_l.append(ksm); vs_l.append(vsm); cs_l.append(cs)
    y_prompt = rms_norm(hp, g_final)
    y_sample = rms_norm(hs, g_final)
    return (y_prompt, y_sample, jnp.stack(kp_l), jnp.stack(vp_l), jnp.stack(cp_l),
            jnp.stack(ks_l), jnp.stack(vs_l), jnp.stack(cs_l))
```

```python
import functools
import math

import numpy as np
import jax
import jax.numpy as jnp
from jax import lax
from jax.experimental import pallas as pl
from jax.experimental.pallas import tpu as pltpu

D_MODEL = 2048
D_CONV = D_MODEL // 2
CONV_WIDTH = 3
HEAD_DIM = 64
N_HEADS = (D_MODEL // 2) // HEAD_DIM
N_KV_HEADS = N_HEADS // 4
GROUP = N_HEADS // N_KV_HEADS
D_ATTN = N_HEADS * HEAD_DIM
D_KV = N_KV_HEADS * HEAD_DIM
WINDOW = 128
NUM_BUCKETS = 32
MAX_DISTANCE = 128
D_FF = -(-8 * D_MODEL // (3 * 256)) * 256
EPS = 1e-6
D_IN_PROJ = 3 * D_CONV + D_ATTN + 2 * D_KV + 2 * D_MODEL

OFF_GC = 0
OFF_GA = D_MODEL
OFF_B = 2 * D_MODEL
OFF_C = OFF_B + D_CONV
OFF_H = OFF_C + D_CONV
OFF_Q = OFF_H + D_CONV
OFF_K = OFF_Q + D_ATTN
OFF_V = OFF_K + D_KV

NEG = -1e30
F32 = jnp.float32
BF16 = jnp.bfloat16
VMEM_LIMIT = 56 * 1024 * 1024


def _bucket_thresholds():
    max_exact = NUM_BUCKETS // 2
    d = np.arange(MAX_DISTANCE)
    ratio = np.log(np.maximum(d, 1).astype(np.float32) / np.float32(max_exact)) / np.float32(
        math.log(MAX_DISTANCE / max_exact))
    large = max_exact + (ratio * np.float32(NUM_BUCKETS - max_exact)).astype(np.int32)
    large = np.minimum(large, NUM_BUCKETS - 1)
    return [int(np.min(d[(d >= max_exact) & (large >= b)])) for b in range(max_exact + 1, NUM_BUCKETS)]


BUCKET_THRESHOLDS = _bucket_thresholds()


def _params(*semantics):
    return pltpu.CompilerParams(dimension_semantics=semantics, vmem_limit_bytes=VMEM_LIMIT)


def _rms_rows(x, g):
    ms = jnp.mean(x * x, axis=-1, keepdims=True)
    return x * lax.rsqrt(ms + EPS) * g


NORM_ROWS = 16


def _norm_to_scratch(x_ref, g_ref, xn_ref):
    rows_total = x_ref.shape[0]

    def body(r, carry):
        rows = pl.ds(pl.multiple_of(r * NORM_ROWS, NORM_ROWS), NORM_ROWS)
        xn_ref[rows, :] = _rms_rows(x_ref[rows, :], g_ref[...]).astype(xn_ref.dtype)
        return carry

    lax.fori_loop(0, rows_total // NORM_ROWS, body, 0)


def _inproj_kernel(x_ref, g_ref, w_ref, z_ref, xn_ref):
    @pl.when(pl.program_id(1) == 0)
    def _():
        _norm_to_scratch(x_ref, g_ref, xn_ref)

    z_ref[...] = jnp.dot(xn_ref[...], w_ref[...], preferred_element_type=F32).astype(z_ref.dtype)


def _in_proj(x, g, w, tm, tn):
    m = x.shape[0]
    n = w.shape[1]
    return pl.pallas_call(
        _inproj_kernel,
        grid=(m // tm, n // tn),
        in_specs=[
            pl.BlockSpec((tm, D_MODEL), lambda i, j: (i, 0)),
            pl.BlockSpec((1, D_MODEL), lambda i, j: (0, 0)),
            pl.BlockSpec((D_MODEL, tn), lambda i, j: (0, j)),
        ],
        out_specs=pl.BlockSpec((tm, tn), lambda i, j: (i, j)),
        out_shape=jax.ShapeDtypeStruct((m, n), BF16),
        scratch_shapes=[pltpu.VMEM((tm, D_MODEL), BF16)],
        compiler_params=_params("arbitrary", "arbitrary"),
        name="in_proj",
    )(x, g, w)


CONV_COLS = 256


def _conv_prompt_kernel(b_ref, c_ref, h_ref, w_ref, y_ref, st_ref):
    u = c_ref[...].astype(F32) * h_ref[...].astype(F32)
    t = u.shape[0]
    row = lax.broadcasted_iota(jnp.int32, u.shape, 0)
    u1 = jnp.where(row >= 1, pltpu.roll(u, 1, 0), 0.0)
    u2 = jnp.where(row >= 2, pltpu.roll(u, 2, 0), 0.0)
    conv = w_ref[0:1, :] * u2 + w_ref[1:2, :] * u1 + w_ref[2:3, :] * u
    y_ref[...] = (b_ref[...].astype(F32) * conv).astype(y_ref.dtype)
    st_ref[...] = u[t - (CONV_WIDTH - 1):, :]


def _conv_prompt(z, w_conv, batch, seq):
    nct = D_CONV // CONV_COLS

    def col(off):
        return lambda b, j: (b, off // CONV_COLS + j)

    return pl.pallas_call(
        _conv_prompt_kernel,
        grid=(batch, nct),
        in_specs=[
            pl.BlockSpec((seq, CONV_COLS), col(OFF_B)),
            pl.BlockSpec((seq, CONV_COLS), col(OFF_C)),
            pl.BlockSpec((seq, CONV_COLS), col(OFF_H)),
            pl.BlockSpec((CONV_WIDTH, CONV_COLS), lambda b, j: (0, j)),
        ],
        out_specs=[
            pl.BlockSpec((seq, CONV_COLS), lambda b, j: (b, j)),
            pl.BlockSpec((None, CONV_WIDTH - 1, CONV_COLS), lambda b, j: (b, 0, j)),
        ],
        out_shape=[
            jax.ShapeDtypeStruct((batch * seq, D_CONV), BF16),
            jax.ShapeDtypeStruct((batch, CONV_WIDTH - 1, D_CONV), F32),
        ],
        compiler_params=_params("arbitrary", "arbitrary"),
        name="conv_prompt",
    )(z, z, z, w_conv)


def _conv_sample_kernel(b_ref, c_ref, h_ref, p1_ref, p2_ref, w_ref, y_ref, u_ref, *, steps):
    u = c_ref[...].astype(F32) * h_ref[...].astype(F32)
    t = lax.broadcasted_iota(jnp.int32, u.shape, 0) % steps
    u1 = jnp.where(t >= 1, pltpu.roll(u, 1, 0), p1_ref[...])
    u2 = jnp.where(t >= 2, pltpu.roll(u, 2, 0), p2_ref[...])
    conv = w_ref[0:1, :] * u2 + w_ref[1:2, :] * u1 + w_ref[2:3, :] * u
    y_ref[...] = (b_ref[...].astype(F32) * conv).astype(y_ref.dtype)
    u_ref[...] = u


def _conv_sample(z, prev1, prev2, w_conv, steps):
    rows = z.shape[0]
    nct = D_CONV // CONV_COLS

    def col(off):
        return lambda j: (0, off // CONV_COLS + j)

    plain = pl.BlockSpec((rows, CONV_COLS), lambda j: (0, j))
    return pl.pallas_call(
        functools.partial(_conv_sample_kernel, steps=steps),
        grid=(nct,),
        in_specs=[
            pl.BlockSpec((rows, CONV_COLS), col(OFF_B)),
            pl.BlockSpec((rows, CONV_COLS), col(OFF_C)),
            pl.BlockSpec((rows, CONV_COLS), col(OFF_H)),
            plain,
            plain,
            pl.BlockSpec((CONV_WIDTH, CONV_COLS), lambda j: (0, j)),
        ],
        out_specs=[plain, plain],
        out_shape=[
            jax.ShapeDtypeStruct((rows, D_CONV), BF16),
            jax.ShapeDtypeStruct((rows, D_CONV), F32),
        ],
        compiler_params=_params("arbitrary"),
        name="conv_sample",
    )(z, z, z, prev1, prev2, w_conv)


def _bias_from_distance(dist, head, relb_ref):
    valid = (dist >= 0) & (dist < WINDOW)
    d = jnp.clip(dist, 0, MAX_DISTANCE - 1)
    bucket = jnp.minimum(d, NUM_BUCKETS // 2)
    for thr in BUCKET_THRESHOLDS:
        bucket = bucket + (d >= thr).astype(jnp.int32)
    bias = jnp.zeros(dist.shape, F32)
    for b in range(NUM_BUCKETS):
        bias = jnp.where(bucket == b, relb_ref[b, head], bias)
    return jnp.where(valid, bias, NEG)


def _softmax_with_sink(s, sink):
    m = jnp.maximum(jnp.max(s, axis=-1, keepdims=True), sink)
    p = jnp.exp(s - m)
    denom = jnp.sum(p, axis=-1, keepdims=True) + jnp.exp(sink - m)
    return p * (1.0 / denom)


def _attn_prompt_kernel(relb_ref, sink_ref, q_ref, kp_ref, kc_ref, vp_ref, vc_ref, o_ref, bias_ref):
    b = pl.program_id(0)
    n = pl.program_id(1)

    @pl.when((b == 0) & (n == 0))
    def _():
        qi = lax.broadcasted_iota(jnp.int32, (WINDOW, 2 * WINDOW), 0)
        kj = lax.broadcasted_iota(jnp.int32, (WINDOW, 2 * WINDOW), 1)
        dist = qi + WINDOW - kj
        for h in range(N_HEADS):
            bias_ref[h] = _bias_from_distance(dist, h, relb_ref)

    col = lax.broadcasted_iota(jnp.int32, (WINDOW, 2 * WINDOW), 1)
    keep = (col >= WINDOW) | (n > 0)
    q = q_ref[...]
    kk = jnp.concatenate([kp_ref[...], kc_ref[...]], axis=0)
    vv = jnp.concatenate([vp_ref[...], vc_ref[...]], axis=0)
    outs = []
    for g in range(N_KV_HEADS):
        kg = kk[:, g * HEAD_DIM:(g + 1) * HEAD_DIM]
        vg = vv[:, g * HEAD_DIM:(g + 1) * HEAD_DIM]
        for r in range(GROUP):
            h = g * GROUP + r
            qh = q[:, h * HEAD_DIM:(h + 1) * HEAD_DIM]
            s = lax.dot_general(qh, kg, (((1,), (1,)), ((), ())), preferred_element_type=F32)
            s = s * (HEAD_DIM ** -0.5) + bias_ref[h]
            s = jnp.where(keep, s, NEG)
            p = _softmax_with_sink(s, sink_ref[h])
            outs.append(jnp.dot(p.astype(BF16), vg, preferred_element_type=F32))
    o_ref[...] = jnp.concatenate(outs, axis=-1).astype(o_ref.dtype)


def _attn_prompt(z, rel_bias, sinks, batch, seq):
    nb = seq // WINDOW

    def cur(off, width):
        return lambda b, n: (b * nb + n, off // width)

    def prev(off, width):
        return lambda b, n: (b * nb + jnp.maximum(n - 1, 0), off // width)

    smem = pl.BlockSpec(memory_space=pltpu.SMEM)
    return pl.pallas_call(
        _attn_prompt_kernel,
        grid=(batch, nb),
        in_specs=[
            smem,
            smem,
            pl.BlockSpec((WINDOW, D_ATTN), cur(OFF_Q, D_ATTN)),
            pl.BlockSpec((WINDOW, D_KV), prev(OFF_K, D_KV)),
            pl.BlockSpec((WINDOW, D_KV), cur(OFF_K, D_KV)),
            pl.BlockSpec((WINDOW, D_KV), prev(OFF_V, D_KV)),
            pl.BlockSpec((WINDOW, D_KV), cur(OFF_V, D_KV)),
        ],
        out_specs=pl.BlockSpec((WINDOW, D_ATTN), lambda b, n: (b * nb + n, 0)),
        out_shape=jax.ShapeDtypeStruct((batch * seq, D_ATTN), BF16),
        scratch_shapes=[pltpu.VMEM((N_HEADS, WINDOW, 2 * WINDOW), F32)],
        compiler_params=_params("arbitrary", "arbitrary"),
        name="attn_prompt",
    )(rel_bias, sinks, z, z, z, z, z)


SAMPLE_BATCH_TILE = 16
KEY_PAD = 2 * WINDOW


def _attn_sample_kernel(relb_ref, sink_ref, q_ref, kn_ref, vn_ref, ck_ref, cv_ref,
                        o_ref, nk_ref, nv_ref,
                        bias_ref, qf_ref, kf_ref, vf_ref, of_ref, *, steps):
    wc = ck_ref.shape[1]
    rows = GROUP * steps

    @pl.when(pl.program_id(0) == 0)
    def _():
        t = lax.broadcasted_iota(jnp.int32, (rows, KEY_PAD), 0) % steps
        kj = lax.broadcasted_iota(jnp.int32, (rows, KEY_PAD), 1)
        dist = t + wc - kj
        for g in range(N_KV_HEADS):
            for r in range(GROUP):
                sl = slice(r * steps, (r + 1) * steps)
                bias_ref[g, sl, :] = _bias_from_distance(dist[sl], g * GROUP + r, relb_ref)
        kf_ref[...] = jnp.zeros(kf_ref.shape, kf_ref.dtype)
        vf_ref[...] = jnp.zeros(vf_ref.shape, vf_ref.dtype)

    qf_ref[...] = q_ref[...].astype(F32)
    knf = kn_ref[...].astype(F32)
    vnf = vn_ref[...].astype(F32)

    for bi in range(SAMPLE_BATCH_TILE):
        new = slice(bi * steps, (bi + 1) * steps)
        ck = ck_ref[bi]
        cv = cv_ref[bi]
        kn = knf[new]
        vn = vnf[new]
        nk_ref[bi, 0:wc - steps, :] = ck[steps:, :]
        nk_ref[bi, wc - steps:wc, :] = kn
        nv_ref[bi, 0:wc - steps, :] = cv[steps:, :]
        nv_ref[bi, wc - steps:wc, :] = vn
        kf_ref[0:wc, :] = ck
        kf_ref[wc:wc + steps, :] = kn
        vf_ref[0:wc, :] = cv
        vf_ref[wc:wc + steps, :] = vn
        kf = kf_ref[...].astype(BF16)
        vf = vf_ref[...].astype(BF16)
        qb = qf_ref[new, :]
        outs = []
        for g in range(N_KV_HEADS):
            kg = kf[:, g * HEAD_DIM:(g + 1) * HEAD_DIM]
            vg = vf[:, g * HEAD_DIM:(g + 1) * HEAD_DIM]
            qg = jnp.concatenate(
                [qb[:, (g * GROUP + r) * HEAD_DIM:(g * GROUP + r + 1) * HEAD_DIM] for r in range(GROUP)], axis=0)
            s = lax.dot_general(qg.astype(BF16), kg, (((1,), (1,)), ((), ())), preferred_element_type=F32)
            s = s * (HEAD_DIM ** -0.5) + bias_ref[g]
            sink = jnp.concatenate(
                [jnp.full((steps, 1), sink_ref[g * GROUP + r], F32) for r in range(GROUP)], axis=0)
            p = _softmax_with_sink(s, sink)
            og = jnp.dot(p.astype(BF16), vg, preferred_element_type=F32)
            outs.extend(og[r * steps:(r + 1) * steps, :] for r in range(GROUP))
        of_ref[new, :] = jnp.concatenate(outs, axis=-1)
    o_ref[...] = of_ref[...].astype(o_ref.dtype)


def _attn_sample(z, cache_k, cache_v, rel_bias, sinks, batch, steps):
    wc = cache_k.shape[1]
    bt = SAMPLE_BATCH_TILE
    rows = bt * steps
    smem = pl.BlockSpec(memory_space=pltpu.SMEM)
    cache_spec = pl.BlockSpec((bt, wc, D_KV), lambda i: (i, 0, 0))
    return pl.pallas_call(
        functools.partial(_attn_sample_kernel, steps=steps),
        grid=(batch // bt,),
        in_specs=[
            smem,
            smem,
            pl.BlockSpec((rows, D_ATTN), lambda i: (i, OFF_Q // D_ATTN)),
            pl.BlockSpec((rows, D_KV), lambda i: (i, OFF_K // D_KV)),
            pl.BlockSpec((rows, D_KV), lambda i: (i, OFF_V // D_KV)),
            cache_spec,
            cache_spec,
        ],
        out_specs=[
            pl.BlockSpec((rows, D_ATTN), lambda i: (i, 0)),
            cache_spec,
            cache_spec,
        ],
        out_shape=[
            jax.ShapeDtypeStruct((batch * steps, D_ATTN), BF16),
            jax.ShapeDtypeStruct(cache_k.shape, F32),
            jax.ShapeDtypeStruct(cache_v.shape, F32),
        ],
        scratch_shapes=[
            pltpu.VMEM((N_KV_HEADS, GROUP * steps, KEY_PAD), F32),
            pltpu.VMEM((rows, D_ATTN), F32),
            pltpu.VMEM((KEY_PAD, D_KV), F32),
            pltpu.VMEM((KEY_PAD, D_KV), F32),
            pltpu.VMEM((rows, D_ATTN), F32),
        ],
        compiler_params=_params("arbitrary"),
        name="attn_sample",
    )(rel_bias, sinks, z, z, z, cache_k, cache_v)


def _mix_kernel(yc_ref, at_ref, gc_ref, ga_ref, x_ref, wco_ref, wao_ref, wo_ref, h_ref):
    y_conv = jnp.dot(yc_ref[...], wco_ref[...], preferred_element_type=F32)
    y_attn = jnp.dot(at_ref[...], wao_ref[...], preferred_element_type=F32)
    merged = (jax.nn.sigmoid(gc_ref[...].astype(F32)) * y_conv
              + jax.nn.sigmoid(ga_ref[...].astype(F32)) * y_attn)
    h_ref[...] = x_ref[...] + jnp.dot(merged.astype(BF16), wo_ref[...], preferred_element_type=F32)


def _mix(yc, attn, z, x, wco, wao, wo, tm):
    m = x.shape[0]
    resident = functools.partial(pl.BlockSpec, index_map=lambda i: (0, 0), pipeline_mode=pl.Buffered(1))
    return pl.pallas_call(
        _mix_kernel,
        grid=(m // tm,),
        in_specs=[
            pl.BlockSpec((tm, D_CONV), lambda i: (i, 0)),
            pl.BlockSpec((tm, D_ATTN), lambda i: (i, 0)),
            pl.BlockSpec((tm, D_MODEL), lambda i: (i, OFF_GC // D_MODEL)),
            pl.BlockSpec((tm, D_MODEL), lambda i: (i, OFF_GA // D_MODEL)),
            pl.BlockSpec((tm, D_MODEL), lambda i: (i, 0)),
            resident((D_CONV, D_MODEL)),
            resident((D_ATTN, D_MODEL)),
            resident((D_MODEL, D_MODEL)),
        ],
        out_specs=pl.BlockSpec((tm, D_MODEL), lambda i: (i, 0)),
        out_shape=jax.ShapeDtypeStruct((m, D_MODEL), F32),
        compiler_params=_params("arbitrary"),
        name="mix",
    )(yc, attn, z, z, x, wco, wao, wo)


def _ffn_kernel(h_ref, g_ref, gf_ref, wg_ref, wu_ref, wd_ref, y_ref, hn_ref, acc_ref, *, final_norm):
    j = pl.program_id(1)

    @pl.when(j == 0)
    def _():
        _norm_to_scratch(h_ref, g_ref, hn_ref)
        acc_ref[...] = jnp.zeros(acc_ref.shape, acc_ref.dtype)

    hn = hn_ref[...]
    a = (jax.nn.silu(jnp.dot(hn, wg_ref[...], preferred_element_type=F32))
         * jnp.dot(hn, wu_ref[...], preferred_element_type=F32))
    acc_ref[...] += jnp.dot(a.astype(BF16), wd_ref[...], preferred_element_type=F32)

    @pl.when(j == pl.num_programs(1) - 1)
    def _():
        def body(r, carry):
            rows = pl.ds(pl.multiple_of(r * NORM_ROWS, NORM_ROWS), NORM_ROWS)
            out = h_ref[rows, :] + acc_ref[rows, :]
            if final_norm:
                out = _rms_rows(out, gf_ref[...])
            y_ref[rows, :] = out
            return carry

        lax.fori_loop(0, h_ref.shape[0] // NORM_ROWS, body, 0)


def _ffn(h, g_ffn, g_final, wg, wu, wd, tm, tf, final_norm):
    m = h.shape[0]
    row_spec = pl.BlockSpec((tm, D_MODEL), lambda i, j: (i, 0))
    gain_spec = pl.BlockSpec((1, D_MODEL), lambda i, j: (0, 0))
    return pl.pallas_call(
        functools.partial(_ffn_kernel, final_norm=final_norm),
        grid=(m // tm, D_FF // tf),
        in_specs=[
            row_spec,
            gain_spec,
            gain_spec,
            pl.BlockSpec((D_MODEL, tf), lambda i, j: (0, j)),
            pl.BlockSpec((D_MODEL, tf), lambda i, j: (0, j)),
            pl.BlockSpec((tf, D_MODEL), lambda i, j: (j, 0)),
        ],
        out_specs=row_spec,
        out_shape=jax.ShapeDtypeStruct((m, D_MODEL), F32),
        scratch_shapes=[pltpu.VMEM((tm, D_MODEL), BF16), pltpu.VMEM((tm, D_MODEL), F32)],
        compiler_params=_params("arbitrary", "arbitrary"),
        name="ffn",
    )(h, g_ffn, g_final, wg, wu, wd)


IN_PROJ_TM = 512
IN_PROJ_TN = 2176
MIX_TM = 256
FFN_TM = 512
FFN_TF = 512


def _dense_tail(x, z, yc, attn, lw, g_final, final_norm):
    h = _mix(yc, attn, z, x, lw["wco"], lw["wao"], lw["wo"], MIX_TM)
    return _ffn(h, lw["g_ffn"], g_final, lw["wg"], lw["wu"], lw["wd"], FFN_TM, FFN_TF, final_norm)


def kernel(x_prompt, x_sample, cache_k, cache_v, state_conv, rel_bias, w_in, w_conv, w_conv_out, sinks,
           w_attn_out, w_o, g_mix, g_ffn, w_gate, w_up, w_down, g_final):
    depth = w_in.shape[0]
    batch, seq, _ = x_prompt.shape
    dec_batch, steps, _ = x_sample.shape
    wc = cache_k.shape[2]
    assert seq % WINDOW == 0 and wc == WINDOW and dec_batch % SAMPLE_BATCH_TILE == 0

    hp = x_prompt.reshape(batch * seq, D_MODEL)
    hs = x_sample.reshape(dec_batch * steps, D_MODEL)
    g_final2 = g_final.reshape(1, D_MODEL)
    split = 3 * D_CONV + D_ATTN + 2 * D_KV
    kp_l, vp_l, cp_l, ks_l, vs_l, cs_l = [], [], [], [], [], []
    for l in range(depth):
        final = l == depth - 1
        lw = {
            "w_in": jnp.concatenate([w_in[l][:, split:], w_in[l][:, :split]], axis=1).astype(BF16),
            "wco": w_conv_out[l].astype(BF16),
            "wao": w_attn_out[l].astype(BF16),
            "wo": w_o[l].astype(BF16),
            "wg": w_gate[l].astype(BF16),
            "wu": w_up[l].astype(BF16),
            "wd": w_down[l].astype(BF16),
            "g_ffn": g_ffn[l].reshape(1, D_MODEL),
        }
        g_mix_l = g_mix[l].reshape(1, D_MODEL)

        zp = _in_proj(hp, g_mix_l, lw["w_in"], IN_PROJ_TM, IN_PROJ_TN)
        ycp, conv_p = _conv_prompt(zp, w_conv[l], batch, seq)
        attn_p = _attn_prompt(zp, rel_bias, sinks[l], batch, seq)
        hp = _dense_tail(hp, zp, ycp, attn_p, lw, g_final2, final)
        kv_p = zp.reshape(batch, seq, D_IN_PROJ)[:, seq - WINDOW:, OFF_K:OFF_V + D_KV].astype(F32)
        kp_l.append(kv_p[..., :D_KV].reshape(batch, WINDOW, N_KV_HEADS, HEAD_DIM))
        vp_l.append(kv_p[..., D_KV:].reshape(batch, WINDOW, N_KV_HEADS, HEAD_DIM))
        cp_l.append(conv_p)

        zs = _in_proj(hs, g_mix_l, lw["w_in"], IN_PROJ_TM, IN_PROJ_TN)
        st = state_conv[l]
        zero = jnp.zeros((dec_batch, steps - 2, D_CONV), F32)
        prev1 = jnp.concatenate([st[:, 1:2], zero, zero[:, :1]], axis=1).reshape(dec_batch * steps, D_CONV)
        prev2 = jnp.concatenate([st, zero], axis=1).reshape(dec_batch * steps, D_CONV)
        ycs, u_s = _conv_sample(zs, prev1, prev2, w_conv[l], steps)
        attn_s, nk, nv = _attn_sample(zs, cache_k[l].reshape(dec_batch, wc, D_KV),
                                      cache_v[l].reshape(dec_batch, wc, D_KV), rel_bias, sinks[l],
                                      dec_batch, steps)
        hs = _dense_tail(hs, zs, ycs, attn_s, lw, g_final2, final)
        ks_l.append(nk.reshape(dec_batch, wc, N_KV_HEADS, HEAD_DIM))
        vs_l.append(nv.reshape(dec_batch, wc, N_KV_HEADS, HEAD_DIM))
        cs_l.append(u_s.reshape(dec_batch, steps, D_CONV)[:, steps - (CONV_WIDTH - 1):])

    return (hp.reshape(batch, seq, D_MODEL), hs.reshape(dec_batch, steps, D_MODEL),
            jnp.stack(kp_l), jnp.stack(vp_l), jnp.stack(cp_l),
            jnp.stack(ks_l), jnp.stack(vs_l), jnp.stack(cs_l))
```

```python
import functools
import math

import numpy as np
import jax
import jax.numpy as jnp
from jax import lax
from jax.experimental import pallas as pl
from jax.experimental.pallas import tpu as pltpu

D_MODEL = 2048
D_CONV = D_MODEL // 2
CONV_WIDTH = 3
HEAD_DIM = 64
N_HEADS = (D_MODEL // 2) // HEAD_DIM
N_KV_HEADS = N_HEADS // 4
GROUP = N_HEADS // N_KV_HEADS
D_ATTN = N_HEADS * HEAD_DIM
D_KV = N_KV_HEADS * HEAD_DIM
WINDOW = 128
NUM_BUCKETS = 32
MAX_DISTANCE = 128
D_FF = -(-8 * D_MODEL // (3 * 256)) * 256
EPS = 1e-6
D_IN_PROJ = 3 * D_CONV + D_ATTN + 2 * D_KV + 2 * D_MODEL

OFF_GC = 0
OFF_GA = D_MODEL
OFF_B = 2 * D_MODEL
OFF_C = OFF_B + D_CONV
OFF_H = OFF_C + D_CONV
OFF_Q = OFF_H + D_CONV
OFF_K = OFF_Q + D_ATTN
OFF_V = OFF_K + D_KV

NEG = -1e30
F32 = jnp.float32
BF16 = jnp.bfloat16
VMEM_LIMIT = 56 * 1024 * 1024


def _bucket_thresholds():
    max_exact = NUM_BUCKETS // 2
    d = np.arange(MAX_DISTANCE)
    ratio = np.log(np.maximum(d, 1).astype(np.float32) / np.float32(max_exact)) / np.float32(
        math.log(MAX_DISTANCE / max_exact))
    large = max_exact + (ratio * np.float32(NUM_BUCKETS - max_exact)).astype(np.int32)
    large = np.minimum(large, NUM_BUCKETS - 1)
    return [int(np.min(d[(d >= max_exact) & (large >= b)])) for b in range(max_exact + 1, NUM_BUCKETS)]


BUCKET_THRESHOLDS = _bucket_thresholds()


def _params(*semantics):
    return pltpu.CompilerParams(dimension_semantics=semantics, vmem_limit_bytes=VMEM_LIMIT)


def _rms_rows(x, g):
    ms = jnp.mean(x * x, axis=-1, keepdims=True)
    return x * lax.rsqrt(ms + EPS) * g


NORM_ROWS = 16


def _inproj_kernel(x_ref, g_ref, w_ref, z_ref, xn_ref):
    @pl.when(pl.program_id(1) == 0)
    def _():
        def body(r, carry):
            rows = pl.ds(pl.multiple_of(r * NORM_ROWS, NORM_ROWS), NORM_ROWS)
            xn_ref[rows, :] = _rms_rows(x_ref[rows, :], g_ref[...]).astype(xn_ref.dtype)
            return carry

        lax.fori_loop(0, x_ref.shape[0] // NORM_ROWS, body, 0)

    z_ref[...] = jnp.dot(xn_ref[...], w_ref[...], preferred_element_type=F32).astype(z_ref.dtype)


def _in_proj(x, g, w, tm, tn):
    m = x.shape[0]
    n = w.shape[1]
    return pl.pallas_call(
        _inproj_kernel,
        grid=(m // tm, n // tn),
        in_specs=[
            pl.BlockSpec((tm, D_MODEL), lambda i, j: (i, 0), pipeline_mode=pl.Buffered(1)),
            pl.BlockSpec((1, D_MODEL), lambda i, j: (0, 0)),
            pl.BlockSpec((D_MODEL, tn), lambda i, j: (0, j)),
        ],
        out_specs=pl.BlockSpec((tm, tn), lambda i, j: (i, j)),
        out_shape=jax.ShapeDtypeStruct((m, n), BF16),
        scratch_shapes=[pltpu.VMEM((tm, D_MODEL), BF16)],
        compiler_params=_params("arbitrary", "arbitrary"),
        name="in_proj",
    )(x, g, w)


CONV_COLS = 256


def _conv_prompt_kernel(b_ref, c_ref, h_ref, w_ref, y_ref, st_ref):
    u = c_ref[...].astype(F32) * h_ref[...].astype(F32)
    t = u.shape[0]
    row = lax.broadcasted_iota(jnp.int32, u.shape, 0)
    u1 = jnp.where(row >= 1, pltpu.roll(u, 1, 0), 0.0)
    u2 = jnp.where(row >= 2, pltpu.roll(u, 2, 0), 0.0)
    conv = w_ref[0:1, :] * u2 + w_ref[1:2, :] * u1 + w_ref[2:3, :] * u
    y_ref[...] = (b_ref[...].astype(F32) * conv).astype(y_ref.dtype)
    st_ref[...] = u[t - (CONV_WIDTH - 1):, :]


def _conv_prompt(z, w_conv, batch, seq):
    nct = D_CONV // CONV_COLS

    def col(off):
        return lambda b, j: (b, off // CONV_COLS + j)

    return pl.pallas_call(
        _conv_prompt_kernel,
        grid=(batch, nct),
        in_specs=[
            pl.BlockSpec((seq, CONV_COLS), col(OFF_B)),
            pl.BlockSpec((seq, CONV_COLS), col(OFF_C)),
            pl.BlockSpec((seq, CONV_COLS), col(OFF_H)),
            pl.BlockSpec((CONV_WIDTH, CONV_COLS), lambda b, j: (0, j)),
        ],
        out_specs=[
            pl.BlockSpec((seq, CONV_COLS), lambda b, j: (b, j)),
            pl.BlockSpec((None, CONV_WIDTH - 1, CONV_COLS), lambda b, j: (b, 0, j)),
        ],
        out_shape=[
            jax.ShapeDtypeStruct((batch * seq, D_CONV), BF16),
            jax.ShapeDtypeStruct((batch, CONV_WIDTH - 1, D_CONV), F32),
        ],
        compiler_params=_params("arbitrary", "arbitrary"),
        name="conv_prompt",
    )(z, z, z, w_conv)


def _conv_sample_kernel(b_ref, c_ref, h_ref, p1_ref, p2_ref, w_ref, y_ref, u_ref, *, steps):
    u = c_ref[...].astype(F32) * h_ref[...].astype(F32)
    t = lax.broadcasted_iota(jnp.int32, u.shape, 0) % steps
    u1 = jnp.where(t >= 1, pltpu.roll(u, 1, 0), p1_ref[...])
    u2 = jnp.where(t >= 2, pltpu.roll(u, 2, 0), p2_ref[...])
    conv = w_ref[0:1, :] * u2 + w_ref[1:2, :] * u1 + w_ref[2:3, :] * u
    y_ref[...] = (b_ref[...].astype(F32) * conv).astype(y_ref.dtype)
    u_ref[...] = u


def _conv_sample(z, prev1, prev2, w_conv, steps):
    rows = z.shape[0]
    nct = D_CONV // CONV_COLS

    def col(off):
        return lambda j: (0, off // CONV_COLS + j)

    plain = pl.BlockSpec((rows, CONV_COLS), lambda j: (0, j))
    return pl.pallas_call(
        functools.partial(_conv_sample_kernel, steps=steps),
        grid=(nct,),
        in_specs=[
            pl.BlockSpec((rows, CONV_COLS), col(OFF_B)),
            pl.BlockSpec((rows, CONV_COLS), col(OFF_C)),
            pl.BlockSpec((rows, CONV_COLS), col(OFF_H)),
            plain,
            plain,
            pl.BlockSpec((CONV_WIDTH, CONV_COLS), lambda j: (0, j)),
        ],
        out_specs=[plain, plain],
        out_shape=[
            jax.ShapeDtypeStruct((rows, D_CONV), BF16),
            jax.ShapeDtypeStruct((rows, D_CONV), F32),
        ],
        compiler_params=_params("arbitrary"),
        name="conv_sample",
    )(z, z, z, prev1, prev2, w_conv)


def _bias_from_distance(dist, head, relb_ref):
    valid = (dist >= 0) & (dist < WINDOW)
    d = jnp.clip(dist, 0, MAX_DISTANCE - 1)
    bucket = jnp.minimum(d, NUM_BUCKETS // 2)
    for thr in BUCKET_THRESHOLDS:
        bucket = bucket + (d >= thr).astype(jnp.int32)
    bias = jnp.zeros(dist.shape, F32)
    for b in range(NUM_BUCKETS):
        bias = jnp.where(bucket == b, relb_ref[b, head], bias)
    return jnp.where(valid, bias, NEG)


def _softmax_with_sink(s, sink):
    m = jnp.maximum(jnp.max(s, axis=-1, keepdims=True), sink)
    p = jnp.exp(s - m)
    denom = jnp.sum(p, axis=-1, keepdims=True) + jnp.exp(sink - m)
    return p * (1.0 / denom)


def _attn_prompt_kernel(relb_ref, sink_ref, q_ref, kp_ref, kc_ref, vp_ref, vc_ref, o_ref, bias_ref):
    b = pl.program_id(0)
    n = pl.program_id(1)

    @pl.when((b == 0) & (n == 0))
    def _():
        qi = lax.broadcasted_iota(jnp.int32, (WINDOW, 2 * WINDOW), 0)
        kj = lax.broadcasted_iota(jnp.int32, (WINDOW, 2 * WINDOW), 1)
        dist = qi + WINDOW - kj
        for h in range(N_HEADS):
            bias_ref[h] = _bias_from_distance(dist, h, relb_ref)

    col = lax.broadcasted_iota(jnp.int32, (WINDOW, 2 * WINDOW), 1)
    keep = (col >= WINDOW) | (n > 0)
    q = q_ref[...]
    kk = jnp.concatenate([kp_ref[...], kc_ref[...]], axis=0)
    vv = jnp.concatenate([vp_ref[...], vc_ref[...]], axis=0)
    outs = []
    for g in range(N_KV_HEADS):
        kg = kk[:, g * HEAD_DIM:(g + 1) * HEAD_DIM]
        vg = vv[:, g * HEAD_DIM:(g + 1) * HEAD_DIM]
        for r in range(GROUP):
            h = g * GROUP + r
            qh = q[:, h * HEAD_DIM:(h + 1) * HEAD_DIM]
            s = lax.dot_general(qh, kg, (((1,), (1,)), ((), ())), preferred_element_type=F32)
            s = s * (HEAD_DIM ** -0.5) + bias_ref[h]
            s = jnp.where(keep, s, NEG)
            p = _softmax_with_sink(s, sink_ref[h])
            outs.append(jnp.dot(p.astype(BF16), vg, preferred_element_type=F32))
    o_ref[...] = jnp.concatenate(outs, axis=-1).astype(o_ref.dtype)


def _attn_prompt(z, rel_bias, sinks, batch, seq):
    nb = seq // WINDOW

    def cur(off, width):
        return lambda b, n: (b * nb + n, off // width)

    def prev(off, width):
        return lambda b, n: (b * nb + jnp.maximum(n - 1, 0), off // width)

    smem = pl.BlockSpec(memory_space=pltpu.SMEM)
    return pl.pallas_call(
        _attn_prompt_kernel,
        grid=(batch, nb),
        in_specs=[
            smem,
            smem,
            pl.BlockSpec((WINDOW, D_ATTN), cur(OFF_Q, D_ATTN)),
            pl.BlockSpec((WINDOW, D_KV), prev(OFF_K, D_KV)),
            pl.BlockSpec((WINDOW, D_KV), cur(OFF_K, D_KV)),
            pl.BlockSpec((WINDOW, D_KV), prev(OFF_V, D_KV)),
            pl.BlockSpec((WINDOW, D_KV), cur(OFF_V, D_KV)),
        ],
        out_specs=pl.BlockSpec((WINDOW, D_ATTN), lambda b, n: (b * nb + n, 0)),
        out_shape=jax.ShapeDtypeStruct((batch * seq, D_ATTN), BF16),
        scratch_shapes=[pltpu.VMEM((N_HEADS, WINDOW, 2 * WINDOW), F32)],
        compiler_params=_params("arbitrary", "arbitrary"),
        name="attn_prompt",
    )(rel_bias, sinks, z, z, z, z, z)


SAMPLE_BATCH_TILE = 16


def _attn_sample_kernel(relb_ref, sink_ref, q_ref, kn_ref, vn_ref, ck_ref, cv_ref,
                        o_ref, nk_ref, nv_ref,
                        bias1_ref, bias2_ref, z_ref, s_ref, p_ref, oh_ref, *, steps):
    bt, wc, _ = ck_ref.shape
    tile_rows = bt * steps
    half = D_KV // 2
    assert 2 * HEAD_DIM == half and GROUP * HEAD_DIM == D_KV

    @pl.when(pl.program_id(0) == 0)
    def _():
        t = lax.broadcasted_iota(jnp.int32, (steps, wc), 0)
        j = lax.broadcasted_iota(jnp.int32, (steps, wc), 1)
        t2 = lax.broadcasted_iota(jnp.int32, (steps, tile_rows), 0)
        j2 = lax.broadcasted_iota(jnp.int32, (steps, tile_rows), 1)
        own = jnp.where(j2 < steps, t2 - j2, -1)
        for h in range(N_HEADS):
            sl = slice(h * steps, (h + 1) * steps)
            bias1_ref[sl, :] = _bias_from_distance(t + wc - j, h, relb_ref)
            bias2_ref[0, sl, :] = _bias_from_distance(own, h, relb_ref)
        for b in range(1, bt):
            bias2_ref[b] = pltpu.roll(bias2_ref[0], b * steps, 1)

    kn = kn_ref[...]
    vn = vn_ref[...]
    nk_ref[:, 0:wc - steps, :] = ck_ref[:, steps:, :]
    nk_ref[:, wc - steps:wc, :] = kn.astype(F32).reshape(bt, steps, D_KV)
    nv_ref[:, 0:wc - steps, :] = cv_ref[:, steps:, :]
    nv_ref[:, wc - steps:wc, :] = vn.astype(F32).reshape(bt, steps, D_KV)

    q = q_ref[...].astype(F32)
    lane_half = lax.broadcasted_iota(jnp.int32, (tile_rows, half), 1) // HEAD_DIM
    for h in range(N_HEADS):
        g = h // GROUP
        src = q[:, (h // 2) * half:(h // 2 + 1) * half]
        if h % 2 != g % 2:
            src = pltpu.roll(src, HEAD_DIM, 1)
        placed = jnp.where(lane_half == g % 2, src, 0.0)
        zero = jnp.zeros_like(placed)
        zh = jnp.concatenate([placed, zero] if g // 2 == 0 else [zero, placed], axis=1)
        z_ref[:, h * steps:(h + 1) * steps, :] = zh.reshape(bt, steps, D_KV)

    for b in range(bt):
        keys = jnp.concatenate([ck_ref[b].astype(BF16), kn], axis=0)
        s_ref[b] = lax.dot_general(z_ref[b].astype(BF16), keys, (((1,), (1,)), ((), ())),
                                   preferred_element_type=F32)

    sink = jnp.concatenate([jnp.full((steps, 1), sink_ref[h], F32) for h in range(N_HEADS)], axis=0)
    s = s_ref[...] * (HEAD_DIM ** -0.5)
    s = s + jnp.concatenate([jnp.broadcast_to(bias1_ref[...], (bt,) + bias1_ref.shape), bias2_ref[...]], axis=2)
    p_ref[...] = _softmax_with_sink(s, sink[None]).astype(BF16)

    for b in range(bt):
        vals = jnp.concatenate([cv_ref[b].astype(BF16), vn], axis=0)
        oh_ref[b] = jnp.dot(p_ref[b], vals, preferred_element_type=F32)

    o_ref[...] = jnp.concatenate(
        [oh_ref[:, h * steps:(h + 1) * steps, (h // GROUP) * HEAD_DIM:(h // GROUP + 1) * HEAD_DIM]
         .reshape(tile_rows, HEAD_DIM) for h in range(N_HEADS)], axis=1).astype(o_ref.dtype)


def _attn_sample(z, cache_k, cache_v, rel_bias, sinks, batch, steps):
    wc = cache_k.shape[1]
    bt = SAMPLE_BATCH_TILE
    rows = bt * steps
    smem = pl.BlockSpec(memory_space=pltpu.SMEM)
    cache_spec = pl.BlockSpec((bt, wc, D_KV), lambda i: (i, 0, 0))
    return pl.pallas_call(
        functools.partial(_attn_sample_kernel, steps=steps),
        grid=(batch // bt,),
        in_specs=[
            smem,
            smem,
            pl.BlockSpec((rows, D_ATTN), lambda i: (i, OFF_Q // D_ATTN)),
            pl.BlockSpec((rows, D_KV), lambda i: (i, OFF_K // D_KV)),
            pl.BlockSpec((rows, D_KV), lambda i: (i, OFF_V // D_KV)),
            cache_spec,
            cache_spec,
        ],
        out_specs=[
            pl.BlockSpec((rows, D_ATTN), lambda i: (i, 0)),
            cache_spec,
            cache_spec,
        ],
        out_shape=[
            jax.ShapeDtypeStruct((batch * steps, D_ATTN), BF16),
            jax.ShapeDtypeStruct(cache_k.shape, F32),
            jax.ShapeDtypeStruct(cache_v.shape, F32),
        ],
        scratch_shapes=[
            pltpu.VMEM((N_HEADS * steps, wc), F32),
            pltpu.VMEM((bt, N_HEADS * steps, rows), F32),
            pltpu.VMEM((bt, N_HEADS * steps, D_KV), F32),
            pltpu.VMEM((bt, N_HEADS * steps, wc + rows), F32),
            pltpu.VMEM((bt, N_HEADS * steps, wc + rows), BF16),
            pltpu.VMEM((bt, N_HEADS * steps, D_KV), F32),
        ],
        compiler_params=_params("arbitrary"),
        name="attn_sample",
    )(rel_bias, sinks, z, z, z, cache_k, cache_v)


def _mix_kernel(yc_ref, at_ref, gc_ref, ga_ref, x_ref, g_ref, wco_ref, wao_ref, wo_ref, h_ref, hn_ref):
    y_conv = jnp.dot(yc_ref[...], wco_ref[...], preferred_element_type=F32)
    y_attn = jnp.dot(at_ref[...], wao_ref[...], preferred_element_type=F32)
    merged = (jax.nn.sigmoid(gc_ref[...].astype(F32)) * y_conv
              + jax.nn.sigmoid(ga_ref[...].astype(F32)) * y_attn)
    h = x_ref[...] + jnp.dot(merged.astype(BF16), wo_ref[...], preferred_element_type=F32)
    h_ref[...] = h
    hn_ref[...] = _rms_rows(h, g_ref[...]).astype(hn_ref.dtype)


def _mix(yc, attn, z, x, g_ffn, wco, wao, wo, tm):
    m = x.shape[0]
    resident = functools.partial(pl.BlockSpec, index_map=lambda i: (0, 0), pipeline_mode=pl.Buffered(1))
    row_spec = pl.BlockSpec((tm, D_MODEL), lambda i: (i, 0))
    return pl.pallas_call(
        _mix_kernel,
        grid=(m // tm,),
        in_specs=[
            pl.BlockSpec((tm, D_CONV), lambda i: (i, 0)),
            pl.BlockSpec((tm, D_ATTN), lambda i: (i, 0)),
            pl.BlockSpec((tm, D_MODEL), lambda i: (i, OFF_GC // D_MODEL)),
            pl.BlockSpec((tm, D_MODEL), lambda i: (i, OFF_GA // D_MODEL)),
            row_spec,
            pl.BlockSpec((1, D_MODEL), lambda i: (0, 0)),
            resident((D_CONV, D_MODEL)),
            resident((D_ATTN, D_MODEL)),
            resident((D_MODEL, D_MODEL)),
        ],
        out_specs=[row_spec, row_spec],
        out_shape=[jax.ShapeDtypeStruct((m, D_MODEL), F32), jax.ShapeDtypeStruct((m, D_MODEL), BF16)],
        compiler_params=_params("arbitrary"),
        name="mix",
    )(yc, attn, z, z, x, g_ffn, wco, wao, wo)


def _ffn_kernel(hn_ref, h_ref, gf_ref, wg_ref, wu_ref, wd_ref, y_ref, *, final_norm):
    j = pl.program_id(1)

    @pl.when(j == 0)
    def _():
        y_ref[...] = h_ref[...]

    hn = hn_ref[...]
    a = (jax.nn.silu(jnp.dot(hn, wg_ref[...], preferred_element_type=F32))
         * jnp.dot(hn, wu_ref[...], preferred_element_type=F32))
    y_ref[...] += jnp.dot(a.astype(BF16), wd_ref[...], preferred_element_type=F32)

    if final_norm:
        @pl.when(j == pl.num_programs(1) - 1)
        def _():
            def body(r, carry):
                rows = pl.ds(pl.multiple_of(r * NORM_ROWS, NORM_ROWS), NORM_ROWS)
                y_ref[rows, :] = _rms_rows(y_ref[rows, :], gf_ref[...])
                return carry

            lax.fori_loop(0, y_ref.shape[0] // NORM_ROWS, body, 0)


def _ffn(hn, h, g_final, wg, wu, wd, tm, tf, final_norm):
    m = h.shape[0]
    return pl.pallas_call(
        functools.partial(_ffn_kernel, final_norm=final_norm),
        grid=(m // tm, D_FF // tf),
        in_specs=[
            pl.BlockSpec((tm, D_MODEL), lambda i, j: (i, 0)),
            pl.BlockSpec((tm, D_MODEL), lambda i, j: (i, 0), pipeline_mode=pl.Buffered(1)),
            pl.BlockSpec((1, D_MODEL), lambda i, j: (0, 0)),
            pl.BlockSpec((D_MODEL, tf), lambda i, j: (0, j)),
            pl.BlockSpec((D_MODEL, tf), lambda i, j: (0, j)),
            pl.BlockSpec((tf, D_MODEL), lambda i, j: (j, 0)),
        ],
        out_specs=pl.BlockSpec((tm, D_MODEL), lambda i, j: (i, 0)),
        out_shape=jax.ShapeDtypeStruct((m, D_MODEL), F32),
        compiler_params=_params("arbitrary", "arbitrary"),
        name="ffn",
    )(hn, h, g_final, wg, wu, wd)


IN_PROJ_TM = 1024
IN_PROJ_TN = 2176
MIX_TM = 256
FFN_TM = 1024
FFN_TF = 512


def _dense_tail(x, z, yc, attn, lw, g_final, final_norm):
    h, hn = _mix(yc, attn, z, x, lw["g_ffn"], lw["wco"], lw["wao"], lw["wo"], MIX_TM)
    return _ffn(hn, h, g_final, lw["wg"], lw["wu"], lw["wd"], FFN_TM, FFN_TF, final_norm)


def kernel(x_prompt, x_sample, cache_k, cache_v, state_conv, rel_bias, w_in, w_conv, w_conv_out, sinks,
           w_attn_out, w_o, g_mix, g_ffn, w_gate, w_up, w_down, g_final):
    depth = w_in.shape[0]
    batch, seq, _ = x_prompt.shape
    dec_batch, steps, _ = x_sample.shape
    wc = cache_k.shape[2]
    assert seq % WINDOW == 0 and wc == WINDOW and dec_batch % SAMPLE_BATCH_TILE == 0

    hp = x_prompt.reshape(batch * seq, D_MODEL)
    hs = x_sample.reshape(dec_batch * steps, D_MODEL)
    g_final2 = g_final.reshape(1, D_MODEL)
    split = 3 * D_CONV + D_ATTN + 2 * D_KV
    kp_l, vp_l, cp_l, ks_l, vs_l, cs_l = [], [], [], [], [], []
    for l in range(depth):
        final = l == depth - 1
        lw = {
            "w_in": jnp.concatenate([w_in[l][:, split:], w_in[l][:, :split]], axis=1).astype(BF16),
            "wco": w_conv_out[l].astype(BF16),
            "wao": w_attn_out[l].astype(BF16),
            "wo": w_o[l].astype(BF16),
            "wg": w_gate[l].astype(BF16),
            "wu": w_up[l].astype(BF16),
            "wd": w_down[l].astype(BF16),
            "g_ffn": g_ffn[l].reshape(1, D_MODEL),
        }
        g_mix_l = g_mix[l].reshape(1, D_MODEL)

        zp = _in_proj(hp, g_mix_l, lw["w_in"], IN_PROJ_TM, IN_PROJ_TN)
        ycp, conv_p = _conv_prompt(zp, w_conv[l], batch, seq)
        attn_p = _attn_prompt(zp, rel_bias, sinks[l], batch, seq)
        hp = _dense_tail(hp, zp, ycp, attn_p, lw, g_final2, final)
        kv_p = zp.reshape(batch, seq, D_IN_PROJ)[:, seq - WINDOW:, OFF_K:OFF_V + D_KV].astype(F32)
        kp_l.append(kv_p[..., :D_KV].reshape(batch, WINDOW, N_KV_HEADS, HEAD_DIM))
        vp_l.append(kv_p[..., D_KV:].reshape(batch, WINDOW, N_KV_HEADS, HEAD_DIM))
        cp_l.append(conv_p)

        zs = _in_proj(hs, g_mix_l, lw["w_in"], IN_PROJ_TM, IN_PROJ_TN)
        st = state_conv[l]
        zero = jnp.zeros((dec_batch, steps - 2, D_CONV), F32)
        prev1 = jnp.concatenate([st[:, 1:2], zero, zero[:, :1]], axis=1).reshape(dec_batch * steps, D_CONV)
        prev2 = jnp.concatenate([st, zero], axis=1).reshape(dec_batch * steps, D_CONV)
        ycs, u_s = _conv_sample(zs, prev1, prev2, w_conv[l], steps)
        attn_s, nk, nv = _attn_sample(zs, cache_k[l].reshape(dec_batch, wc, D_KV),
                                      cache_v[l].reshape(dec_batch, wc, D_KV), rel_bias, sinks[l],
                                      dec_batch, steps)
        hs = _dense_tail(hs, zs, ycs, attn_s, lw, g_final2, final)
        ks_l.append(nk.reshape(dec_batch, wc, N_KV_HEADS, HEAD_DIM))
        vs_l.append(nv.reshape(dec_batch, wc, N_KV_HEADS, HEAD_DIM))
        cs_l.append(u_s.reshape(dec_batch, steps, D_CONV)[:, steps - (CONV_WIDTH - 1):])

    return (hp.reshape(batch, seq, D_MODEL), hs.reshape(dec_batch, steps, D_MODEL),
            jnp.stack(kp_l), jnp.stack(vp_l), jnp.stack(cp_l),
            jnp.stack(ks_l), jnp.stack(vs_l), jnp.stack(cs_l))
```

```python
import functools
import math

import numpy as np
import jax
import jax.numpy as jnp
from jax import lax
from jax.experimental import pallas as pl
from jax.experimental.pallas import tpu as pltpu

D_MODEL = 2048
D_CONV = D_MODEL // 2
CONV_WIDTH = 3
HEAD_DIM = 64
N_HEADS = (D_MODEL // 2) // HEAD_DIM
N_KV_HEADS = N_HEADS // 4
GROUP = N_HEADS // N_KV_HEADS
D_ATTN = N_HEADS * HEAD_DIM
D_KV = N_KV_HEADS * HEAD_DIM
WINDOW = 128
NUM_BUCKETS = 32
MAX_DISTANCE = 128
D_FF = -(-8 * D_MODEL // (3 * 256)) * 256
EPS = 1e-6
D_IN_PROJ = 3 * D_CONV + D_ATTN + 2 * D_KV + 2 * D_MODEL

OFF_GC = 0
OFF_GA = D_MODEL
OFF_B = 2 * D_MODEL
OFF_C = OFF_B + D_CONV
OFF_H = OFF_C + D_CONV
OFF_Q = OFF_H + D_CONV
OFF_K = OFF_Q + D_ATTN
OFF_V = OFF_K + D_KV

LANES = 128
NEG = -1e30
F32 = jnp.float32
BF16 = jnp.bfloat16
VMEM_LIMIT = 56 * 1024 * 1024


def _bucket_thresholds():
    max_exact = NUM_BUCKETS // 2
    d = np.arange(MAX_DISTANCE)
    ratio = np.log(np.maximum(d, 1).astype(np.float32) / np.float32(max_exact)) / np.float32(
        math.log(MAX_DISTANCE / max_exact))
    large = max_exact + (ratio * np.float32(NUM_BUCKETS - max_exact)).astype(np.int32)
    large = np.minimum(large, NUM_BUCKETS - 1)
    return [int(np.min(d[(d >= max_exact) & (large >= b)])) for b in range(max_exact + 1, NUM_BUCKETS)]


BUCKET_THRESHOLDS = _bucket_thresholds()


def _params(*semantics):
    return pltpu.CompilerParams(dimension_semantics=semantics, vmem_limit_bytes=VMEM_LIMIT)


def _rms_rows(x, g):
    ms = jnp.mean(x * x, axis=-1, keepdims=True)
    return x * lax.rsqrt(ms + EPS) * g


NORM_ROWS = 16


def _inproj_kernel(x_ref, g_ref, w_ref, z_ref, xn_ref):
    @pl.when(pl.program_id(1) == 0)
    def _():
        def body(r, carry):
            rows = pl.ds(pl.multiple_of(r * NORM_ROWS, NORM_ROWS), NORM_ROWS)
            xn_ref[rows, :] = _rms_rows(x_ref[rows, :], g_ref[...]).astype(xn_ref.dtype)
            return carry

        lax.fori_loop(0, x_ref.shape[0] // NORM_ROWS, body, 0)

    z_ref[...] = jnp.dot(xn_ref[...], w_ref[...], preferred_element_type=F32).astype(z_ref.dtype)


def _in_proj(x, g, w, tm, tn):
    m = x.shape[0]
    n = w.shape[1]
    gate_start = n - 2 * D_MODEL
    assert gate_start % tn == 0 and n % tn == 0
    gate_block, nblocks = gate_start // tn, n // tn
    return pl.pallas_call(
        _inproj_kernel,
        grid=(m // tm, nblocks),
        in_specs=[
            pl.BlockSpec((tm, D_MODEL), lambda i, j: (i, 0)),
            pl.BlockSpec((1, D_MODEL), lambda i, j: (0, 0)),
            pl.BlockSpec((D_MODEL, tn), lambda i, j: (0, (j + gate_block) % nblocks)),
        ],
        out_specs=pl.BlockSpec((tm, tn), lambda i, j: (i, j)),
        out_shape=jax.ShapeDtypeStruct((m, n), BF16),
        scratch_shapes=[pltpu.VMEM((tm, D_MODEL), BF16)],
        compiler_params=_params("arbitrary", "arbitrary"),
        name="in_proj",
    )(x, g, w)


CONV_COLS = 256


def _conv_prompt_kernel(b_ref, c_ref, h_ref, w_ref, y_ref, st_ref):
    u = c_ref[...].astype(F32) * h_ref[...].astype(F32)
    t = u.shape[0]
    row = lax.broadcasted_iota(jnp.int32, u.shape, 0)
    u1 = jnp.where(row >= 1, pltpu.roll(u, 1, 0), 0.0)
    u2 = jnp.where(row >= 2, pltpu.roll(u, 2, 0), 0.0)
    conv = w_ref[0:1, :] * u2 + w_ref[1:2, :] * u1 + w_ref[2:3, :] * u
    y_ref[...] = (b_ref[...].astype(F32) * conv).astype(y_ref.dtype)
    st_ref[...] = u[t - (CONV_WIDTH - 1):, :]


def _conv_prompt(z, w_conv, batch, seq):
    nct = D_CONV // CONV_COLS

    def col(off):
        return lambda b, j: (b, off // CONV_COLS + j)

    return pl.pallas_call(
        _conv_prompt_kernel,
        grid=(batch, nct),
        in_specs=[
            pl.BlockSpec((seq, CONV_COLS), col(OFF_B)),
            pl.BlockSpec((seq, CONV_COLS), col(OFF_C)),
            pl.BlockSpec((seq, CONV_COLS), col(OFF_H)),
            pl.BlockSpec((CONV_WIDTH, CONV_COLS), lambda b, j: (0, j)),
        ],
        out_specs=[
            pl.BlockSpec((seq, CONV_COLS), lambda b, j: (b, j)),
            pl.BlockSpec((None, CONV_WIDTH - 1, CONV_COLS), lambda b, j: (b, 0, j)),
        ],
        out_shape=[
            jax.ShapeDtypeStruct((batch * seq, D_CONV), BF16),
            jax.ShapeDtypeStruct((batch, CONV_WIDTH - 1, D_CONV), F32),
        ],
        compiler_params=_params("arbitrary", "arbitrary"),
        name="conv_prompt",
    )(z, z, z, w_conv)


def _conv_sample_kernel(b_ref, c_ref, h_ref, p1_ref, p2_ref, w_ref, y_ref, u_ref, *, steps):
    u = c_ref[...].astype(F32) * h_ref[...].astype(F32)
    t = lax.broadcasted_iota(jnp.int32, u.shape, 0) % steps
    u1 = jnp.where(t >= 1, pltpu.roll(u, 1, 0), p1_ref[...])
    u2 = jnp.where(t >= 2, pltpu.roll(u, 2, 0), p2_ref[...])
    conv = w_ref[0:1, :] * u2 + w_ref[1:2, :] * u1 + w_ref[2:3, :] * u
    y_ref[...] = (b_ref[...].astype(F32) * conv).astype(y_ref.dtype)
    u_ref[...] = u


def _conv_sample(z, prev1, prev2, w_conv, steps):
    rows = z.shape[0]
    nct = D_CONV // CONV_COLS

    def col(off):
        return lambda j: (0, off // CONV_COLS + j)

    plain = pl.BlockSpec((rows, CONV_COLS), lambda j: (0, j))
    return pl.pallas_call(
        functools.partial(_conv_sample_kernel, steps=steps),
        grid=(nct,),
        in_specs=[
            pl.BlockSpec((rows, CONV_COLS), col(OFF_B)),
            pl.BlockSpec((rows, CONV_COLS), col(OFF_C)),
            pl.BlockSpec((rows, CONV_COLS), col(OFF_H)),
            plain,
            plain,
            pl.BlockSpec((CONV_WIDTH, CONV_COLS), lambda j: (0, j)),
        ],
        out_specs=[plain, plain],
        out_shape=[
            jax.ShapeDtypeStruct((rows, D_CONV), BF16),
            jax.ShapeDtypeStruct((rows, D_CONV), F32),
        ],
        compiler_params=_params("arbitrary"),
        name="conv_sample",
    )(z, z, z, prev1, prev2, w_conv)


def _bias_from_distance(dist, head, relb_ref):
    valid = (dist >= 0) & (dist < WINDOW)
    d = jnp.clip(dist, 0, MAX_DISTANCE - 1)
    bucket = jnp.minimum(d, NUM_BUCKETS // 2)
    for thr in BUCKET_THRESHOLDS:
        bucket = bucket + (d >= thr).astype(jnp.int32)
    bias = jnp.zeros(dist.shape, F32)
    for b in range(NUM_BUCKETS):
        bias = jnp.where(bucket == b, relb_ref[b, head], bias)
    return jnp.where(valid, bias, NEG)


def _softmax_with_sink(s, sink):
    m = jnp.maximum(jnp.max(s, axis=-1, keepdims=True), sink)
    p = jnp.exp(s - m)
    denom = jnp.sum(p, axis=-1, keepdims=True) + jnp.exp(sink - m)
    return p * (1.0 / denom)


def _head_in_group_columns(q, h):
    half = D_KV // 2
    assert 2 * HEAD_DIM == half and GROUP * HEAD_DIM == D_KV
    g = h // GROUP
    src = q[:, (h // 2) * half:(h // 2 + 1) * half]
    if h % 2 != g % 2:
        src = pltpu.roll(src, HEAD_DIM, 1)
    lane_half = lax.broadcasted_iota(jnp.int32, src.shape, 1) // HEAD_DIM
    placed = jnp.where(lane_half == g % 2, src, 0.0)
    zero = jnp.zeros_like(placed)
    return jnp.concatenate([placed, zero] if g // 2 == 0 else [zero, placed], axis=1)


def _group_columns(h):
    g = h // GROUP
    return slice(g * HEAD_DIM, (g + 1) * HEAD_DIM)


def _attn_prompt_kernel(relb_ref, sink_ref, q_ref, kp_ref, kc_ref, vp_ref, vc_ref, o_ref, bias_ref, lhs_ref):
    b = pl.program_id(0)
    n = pl.program_id(1)
    qi = lax.broadcasted_iota(jnp.int32, (WINDOW, WINDOW), 0)
    kj = lax.broadcasted_iota(jnp.int32, (WINDOW, WINDOW), 1)
    from_prev = kj > qi

    @pl.when((b == 0) & (n == 0))
    def _():
        dist = jnp.where(from_prev, qi + WINDOW - kj, qi - kj)
        for h in range(N_HEADS):
            rows = slice(h * WINDOW, (h + 1) * WINDOW)
            bias = _bias_from_distance(dist, h, relb_ref)
            bias_ref[1, rows, :] = bias
            bias_ref[0, rows, :] = jnp.where(from_prev, NEG, bias)

    q = q_ref[...].astype(F32) * (HEAD_DIM ** -0.5)
    for h in range(N_HEADS):
        lhs_ref[h * WINDOW:(h + 1) * WINDOW, :] = _head_in_group_columns(q, h).astype(BF16)
    kk = jnp.concatenate([kp_ref[...], kc_ref[...]], axis=0)
    vv = jnp.concatenate([vp_ref[...], vc_ref[...]], axis=0)
    s = lax.dot_general(lhs_ref[...], kk, (((1,), (1,)), ((), ())), preferred_element_type=F32)
    has_prev = jnp.minimum(n, 1)
    probs, sink_terms = [], []
    for h in range(N_HEADS):
        rows = slice(h * WINDOW, (h + 1) * WINDOW)
        sh = jnp.where(from_prev, s[rows, :WINDOW], s[rows, WINDOW:]) + bias_ref[has_prev, rows, :]
        m = jnp.maximum(jnp.max(sh, axis=-1, keepdims=True), sink_ref[h])
        p = jnp.exp(sh - m)
        probs.append(jnp.concatenate([jnp.where(from_prev, p, 0.0), jnp.where(from_prev, 0.0, p)],
                                     axis=1).astype(BF16))
        sink_terms.append(jnp.exp(sink_ref[h] - m))
    pm = jnp.concatenate(probs, axis=0)
    rowsum = jnp.dot(pm, jnp.ones((2 * WINDOW, LANES), BF16), preferred_element_type=F32)
    outs = []
    for g in range(N_KV_HEADS):
        og = jnp.dot(pm[g * GROUP * WINDOW:(g + 1) * GROUP * WINDOW], vv[:, g * HEAD_DIM:(g + 1) * HEAD_DIM],
                     preferred_element_type=F32)
        for r in range(GROUP):
            h = g * GROUP + r
            inv = 1.0 / (rowsum[h * WINDOW:(h + 1) * WINDOW] + sink_terms[h])
            outs.append(og[r * WINDOW:(r + 1) * WINDOW] * inv[:, :HEAD_DIM])
    o_ref[...] = jnp.concatenate(outs, axis=1).astype(o_ref.dtype)


def _attn_prompt(z, rel_bias, sinks, batch, seq):
    nb = seq // WINDOW

    def cur(off, width):
        return lambda b, n: (b * nb + n, off // width)

    def prev(off, width):
        return lambda b, n: (b * nb + jnp.maximum(n - 1, 0), off // width)

    smem = pl.BlockSpec(memory_space=pltpu.SMEM)
    return pl.pallas_call(
        _attn_prompt_kernel,
        grid=(batch, nb),
        in_specs=[
            smem,
            smem,
            pl.BlockSpec((WINDOW, D_ATTN), cur(OFF_Q, D_ATTN)),
            pl.BlockSpec((WINDOW, D_KV), prev(OFF_K, D_KV)),
            pl.BlockSpec((WINDOW, D_KV), cur(OFF_K, D_KV)),
            pl.BlockSpec((WINDOW, D_KV), prev(OFF_V, D_KV)),
            pl.BlockSpec((WINDOW, D_KV), cur(OFF_V, D_KV)),
        ],
        out_specs=pl.BlockSpec((WINDOW, D_ATTN), lambda b, n: (b * nb + n, 0)),
        out_shape=jax.ShapeDtypeStruct((batch * seq, D_ATTN), BF16),
        scratch_shapes=[pltpu.VMEM((2, N_HEADS * WINDOW, WINDOW), F32),
                        pltpu.VMEM((N_HEADS * WINDOW, D_KV), BF16)],
        compiler_params=_params("arbitrary", "arbitrary"),
        name="attn_prompt",
    )(rel_bias, sinks, z, z, z, z, z)


SAMPLE_BATCH_TILE = 16


def _attn_sample_kernel(relb_ref, sink_ref, q_ref, kn_ref, vn_ref, ck_ref, cv_ref,
                        o_ref, nk_ref, nv_ref,
                        bias1_ref, bias2_ref, sinkv_ref, z_ref, s_ref, p_ref, oh_ref, *, steps):
    bt, wc, _ = ck_ref.shape
    tile_rows = bt * steps

    @pl.when(pl.program_id(0) == 0)
    def _():
        t = lax.broadcasted_iota(jnp.int32, (steps, wc), 0)
        j = lax.broadcasted_iota(jnp.int32, (steps, wc), 1)
        t2 = lax.broadcasted_iota(jnp.int32, (steps, tile_rows), 0)
        j2 = lax.broadcasted_iota(jnp.int32, (steps, tile_rows), 1)
        own = jnp.where(j2 < steps, t2 - j2, -1)
        for h in range(N_HEADS):
            sl = slice(h * steps, (h + 1) * steps)
            bias1_ref[sl, :] = _bias_from_distance(t + wc - j, h, relb_ref)
            bias2_ref[0, sl, :] = _bias_from_distance(own, h, relb_ref)
            sinkv_ref[sl, :] = jnp.full((steps, LANES), sink_ref[h], F32)
        for b in range(1, bt):
            bias2_ref[b] = pltpu.roll(bias2_ref[0], b * steps, 1)

    kn = kn_ref[...]
    vn = vn_ref[...]
    nk_ref[:, 0:wc - steps, :] = ck_ref[:, steps:, :]
    nk_ref[:, wc - steps:wc, :] = kn.astype(F32).reshape(bt, steps, D_KV)
    nv_ref[:, 0:wc - steps, :] = cv_ref[:, steps:, :]
    nv_ref[:, wc - steps:wc, :] = vn.astype(F32).reshape(bt, steps, D_KV)

    q = q_ref[...].astype(F32) * (HEAD_DIM ** -0.5)
    for h in range(N_HEADS):
        z_ref[:, h * steps:(h + 1) * steps, :] = _head_in_group_columns(q, h).reshape(bt, steps, D_KV)

    for b in range(bt):
        keys = jnp.concatenate([ck_ref[b].astype(BF16), kn], axis=0)
        s_ref[b] = lax.dot_general(z_ref[b].astype(BF16), keys, (((1,), (1,)), ((), ())),
                                   preferred_element_type=F32)

    bias = jnp.concatenate([jnp.broadcast_to(bias1_ref[...], (bt,) + bias1_ref.shape), bias2_ref[...]], axis=2)
    s = s_ref[...] + bias
    sink = sinkv_ref[...][None]
    m = jnp.maximum(jnp.max(s, axis=-1, keepdims=True), sink)
    p_ref[...] = jnp.exp(s - jnp.concatenate([m] * (s.shape[-1] // LANES), axis=-1)).astype(BF16)
    sink_term = jnp.exp(sink - m)

    vn_ones = jnp.concatenate([vn, jnp.ones((tile_rows, LANES), BF16)], axis=1)
    for b in range(bt):
        cv_ones = jnp.concatenate([cv_ref[b].astype(BF16), jnp.ones((wc, LANES), BF16)], axis=1)
        oh_ref[b] = jnp.dot(p_ref[b], jnp.concatenate([cv_ones, vn_ones], axis=0), preferred_element_type=F32)

    inv = 1.0 / (oh_ref[:, :, D_KV:] + sink_term)
    outs = []
    for h in range(N_HEADS):
        rows = slice(h * steps, (h + 1) * steps)
        outs.append((oh_ref[:, rows, _group_columns(h)] * inv[:, rows, :HEAD_DIM]).reshape(tile_rows, HEAD_DIM))
    o_ref[...] = jnp.concatenate(outs, axis=1).astype(o_ref.dtype)


def _attn_sample(z, cache_k, cache_v, rel_bias, sinks, batch, steps):
    wc = cache_k.shape[1]
    bt = SAMPLE_BATCH_TILE
    rows = bt * steps
    smem = pl.BlockSpec(memory_space=pltpu.SMEM)
    cache_spec = pl.BlockSpec((bt, wc, D_KV), lambda i: (i, 0, 0))
    return pl.pallas_call(
        functools.partial(_attn_sample_kernel, steps=steps),
        grid=(batch // bt,),
        in_specs=[
            smem,
            smem,
            pl.BlockSpec((rows, D_ATTN), lambda i: (i, OFF_Q // D_ATTN)),
            pl.BlockSpec((rows, D_KV), lambda i: (i, OFF_K // D_KV)),
            pl.BlockSpec((rows, D_KV), lambda i: (i, OFF_V // D_KV)),
            cache_spec,
            cache_spec,
        ],
        out_specs=[
            pl.BlockSpec((rows, D_ATTN), lambda i: (i, 0)),
            cache_spec,
            cache_spec,
        ],
        out_shape=[
            jax.ShapeDtypeStruct((batch * steps, D_ATTN), BF16),
            jax.ShapeDtypeStruct(cache_k.shape, F32),
            jax.ShapeDtypeStruct(cache_v.shape, F32),
        ],
        scratch_shapes=[
            pltpu.VMEM((N_HEADS * steps, wc), F32),
            pltpu.VMEM((bt, N_HEADS * steps, rows), F32),
            pltpu.VMEM((N_HEADS * steps, LANES), F32),
            pltpu.VMEM((bt, N_HEADS * steps, D_KV), F32),
            pltpu.VMEM((bt, N_HEADS * steps, wc + rows), F32),
            pltpu.VMEM((bt, N_HEADS * steps, wc + rows), BF16),
            pltpu.VMEM((bt, N_HEADS * steps, D_KV + LANES), F32),
        ],
        compiler_params=_params("arbitrary"),
        name="attn_sample",
    )(rel_bias, sinks, z, z, z, cache_k, cache_v)


def _mix_kernel(yc_ref, at_ref, gc_ref, ga_ref, x_ref, g_ref, wco_ref, wao_ref, wo_ref, h_ref, hn_ref):
    y_conv = jnp.dot(yc_ref[...], wco_ref[...], preferred_element_type=F32)
    y_attn = jnp.dot(at_ref[...], wao_ref[...], preferred_element_type=F32)
    merged = (jax.nn.sigmoid(gc_ref[...].astype(F32)) * y_conv
              + jax.nn.sigmoid(ga_ref[...].astype(F32)) * y_attn)
    h = x_ref[...] + jnp.dot(merged.astype(BF16), wo_ref[...], preferred_element_type=F32)
    h_ref[...] = h
    hn_ref[...] = _rms_rows(h, g_ref[...]).astype(hn_ref.dtype)


def _mix(yc, attn, z, x, g_ffn, wco, wao, wo, tm):
    m = x.shape[0]
    resident = functools.partial(pl.BlockSpec, index_map=lambda i: (0, 0), pipeline_mode=pl.Buffered(1))
    row_spec = pl.BlockSpec((tm, D_MODEL), lambda i: (i, 0))
    return pl.pallas_call(
        _mix_kernel,
        grid=(m // tm,),
        in_specs=[
            pl.BlockSpec((tm, D_CONV), lambda i: (i, 0)),
            pl.BlockSpec((tm, D_ATTN), lambda i: (i, 0)),
            pl.BlockSpec((tm, D_MODEL), lambda i: (i, OFF_GC // D_MODEL)),
            pl.BlockSpec((tm, D_MODEL), lambda i: (i, OFF_GA // D_MODEL)),
            row_spec,
            pl.BlockSpec((1, D_MODEL), lambda i: (0, 0)),
            resident((D_CONV, D_MODEL)),
            resident((D_ATTN, D_MODEL)),
            resident((D_MODEL, D_MODEL)),
        ],
        out_specs=[row_spec, row_spec],
        out_shape=[jax.ShapeDtypeStruct((m, D_MODEL), F32), jax.ShapeDtypeStruct((m, D_MODEL), BF16)],
        compiler_params=_params("arbitrary"),
        name="mix",
    )(yc, attn, z, z, x, g_ffn, wco, wao, wo)


def _ffn_kernel(hn_ref, h_ref, gf_ref, wg_ref, wu_ref, wd_ref, y_ref, *, final_norm):
    j = pl.program_id(1)

    @pl.when(j == 0)
    def _():
        y_ref[...] = h_ref[...]

    hn = hn_ref[...]
    a = (jax.nn.silu(jnp.dot(hn, wg_ref[...], preferred_element_type=F32))
         * jnp.dot(hn, wu_ref[...], preferred_element_type=F32))
    y_ref[...] += jnp.dot(a.astype(BF16), wd_ref[...], preferred_element_type=F32)

    if final_norm:
        @pl.when(j == pl.num_programs(1) - 1)
        def _():
            def body(r, carry):
                rows = pl.ds(pl.multiple_of(r * NORM_ROWS, NORM_ROWS), NORM_ROWS)
                y_ref[rows, :] = _rms_rows(y_ref[rows, :], gf_ref[...])
                return carry

            lax.fori_loop(0, y_ref.shape[0] // NORM_ROWS, body, 0)


def _ffn(hn, h, g_final, wg, wu, wd, tm, tf, final_norm):
    m = h.shape[0]
    return pl.pallas_call(
        functools.partial(_ffn_kernel, final_norm=final_norm),
        grid=(m // tm, D_FF // tf),
        in_specs=[
            pl.BlockSpec((tm, D_MODEL), lambda i, j: (i, 0)),
            pl.BlockSpec((tm, D_MODEL), lambda i, j: (i, 0), pipeline_mode=pl.Buffered(1)),
            pl.BlockSpec((1, D_MODEL), lambda i, j: (0, 0)),
            pl.BlockSpec((D_MODEL, tf), lambda i, j: (0, j)),
            pl.BlockSpec((D_MODEL, tf), lambda i, j: (0, j)),
            pl.BlockSpec((tf, D_MODEL), lambda i, j: (j, 0)),
        ],
        out_specs=pl.BlockSpec((tm, D_MODEL), lambda i, j: (i, 0)),
        out_shape=jax.ShapeDtypeStruct((m, D_MODEL), F32),
        compiler_params=_params("arbitrary", "arbitrary"),
        name="ffn",
    )(hn, h, g_final, wg, wu, wd)


IN_PROJ_TM_PROMPT = 2048
IN_PROJ_TM_SAMPLE = 1024
IN_PROJ_TN = 512
MIX_TM = 256
FFN_TM = 1024
FFN_TF = 512


def _dense_tail(x, z, yc, attn, lw, g_final, final_norm):
    h, hn = _mix(yc, attn, z, x, lw["g_ffn"], lw["wco"], lw["wao"], lw["wo"], MIX_TM)
    return _ffn(hn, h, g_final, lw["wg"], lw["wu"], lw["wd"], FFN_TM, FFN_TF, final_norm)


def kernel(x_prompt, x_sample, cache_k, cache_v, state_conv, rel_bias, w_in, w_conv, w_conv_out, sinks,
           w_attn_out, w_o, g_mix, g_ffn, w_gate, w_up, w_down, g_final):
    depth = w_in.shape[0]
    batch, seq, _ = x_prompt.shape
    dec_batch, steps, _ = x_sample.shape
    wc = cache_k.shape[2]
    assert seq % WINDOW == 0 and wc == WINDOW and dec_batch % SAMPLE_BATCH_TILE == 0

    hp = x_prompt.reshape(batch * seq, D_MODEL)
    hs = x_sample.reshape(dec_batch * steps, D_MODEL)
    g_final2 = g_final.reshape(1, D_MODEL)
    kp_l, vp_l, cp_l, ks_l, vs_l, cs_l = [], [], [], [], [], []
    for l in range(depth):
        final = l == depth - 1
        lw = {
            "w_in": w_in[l].astype(BF16),
            "wco": w_conv_out[l].astype(BF16),
            "wao": w_attn_out[l].astype(BF16),
            "wo": w_o[l].astype(BF16),
            "wg": w_gate[l].astype(BF16),
            "wu": w_up[l].astype(BF16),
            "wd": w_down[l].astype(BF16),
            "g_ffn": g_ffn[l].reshape(1, D_MODEL),
        }
        g_mix_l = g_mix[l].reshape(1, D_MODEL)

        zp = _in_proj(hp, g_mix_l, lw["w_in"], IN_PROJ_TM_PROMPT, IN_PROJ_TN)
        ycp, conv_p = _conv_prompt(zp, w_conv[l], batch, seq)
        attn_p = _attn_prompt(zp, rel_bias, sinks[l], batch, seq)
        hp = _dense_tail(hp, zp, ycp, attn_p, lw, g_final2, final)
        kv_p = zp.reshape(batch, seq, D_IN_PROJ)[:, seq - WINDOW:, OFF_K:OFF_V + D_KV].astype(F32)
        kp_l.append(kv_p[..., :D_KV].reshape(batch, WINDOW, N_KV_HEADS, HEAD_DIM))
        vp_l.append(kv_p[..., D_KV:].reshape(batch, WINDOW, N_KV_HEADS, HEAD_DIM))
        cp_l.append(conv_p)

        zs = _in_proj(hs, g_mix_l, lw["w_in"], IN_PROJ_TM_SAMPLE, IN_PROJ_TN)
        st = state_conv[l]
        zero = jnp.zeros((dec_batch, steps - 2, D_CONV), F32)
        prev1 = jnp.concatenate([st[:, 1:2], zero, zero[:, :1]], axis=1).reshape(dec_batch * steps, D_CONV)
        prev2 = jnp.concatenate([st, zero], axis=1).reshape(dec_batch * steps, D_CONV)
        ycs, u_s = _conv_sample(zs, prev1, prev2, w_conv[l], steps)
        attn_s, nk, nv = _attn_sample(zs, cache_k[l].reshape(dec_batch, wc, D_KV),
                                      cache_v[l].reshape(dec_batch, wc, D_KV), rel_bias, sinks[l],
                                      dec_batch, steps)
        hs = _dense_tail(hs, zs, ycs, attn_s, lw, g_final2, final)
        ks_l.append(nk.reshape(dec_batch, wc, N_KV_HEADS, HEAD_DIM))
        vs_l.append(nv.reshape(dec_batch, wc, N_KV_HEADS, HEAD_DIM))
        cs_l.append(u_s.reshape(dec_batch, steps, D_CONV)[:, steps - (CONV_WIDTH - 1):])

    return (hp.reshape(batch, seq, D_MODEL), hs.reshape(dec_batch, steps, D_MODEL),
            jnp.stack(kp_l), jnp.stack(vp_l), jnp.stack(cp_l),
            jnp.stack(ks_l), jnp.stack(vs_l), jnp.stack(cs_l))
```

```python
import functools
import math

import numpy as np
import jax
import jax.numpy as jnp
from jax import lax
from jax.experimental import pallas as pl
from jax.experimental.pallas import tpu as pltpu

D_MODEL = 2048
D_CONV = D_MODEL // 2
CONV_WIDTH = 3
HEAD_DIM = 64
N_HEADS = (D_MODEL // 2) // HEAD_DIM
N_KV_HEADS = N_HEADS // 4
GROUP = N_HEADS // N_KV_HEADS
D_ATTN = N_HEADS * HEAD_DIM
D_KV = N_KV_HEADS * HEAD_DIM
WINDOW = 128
NUM_BUCKETS = 32
MAX_DISTANCE = 128
D_FF = -(-8 * D_MODEL // (3 * 256)) * 256
EPS = 1e-6
D_IN_PROJ = 3 * D_CONV + D_ATTN + 2 * D_KV + 2 * D_MODEL

OFF_GC = 0
OFF_GA = D_MODEL
OFF_B = 2 * D_MODEL
OFF_C = OFF_B + D_CONV
OFF_H = OFF_C + D_CONV
OFF_Q = OFF_H + D_CONV
OFF_K = OFF_Q + D_ATTN
OFF_V = OFF_K + D_KV

LANES = 128
NEG = -1e30
F32 = jnp.float32
BF16 = jnp.bfloat16
VMEM_LIMIT = 60 * 1024 * 1024


def _bucket_thresholds():
    max_exact = NUM_BUCKETS // 2
    d = np.arange(MAX_DISTANCE)
    ratio = np.log(np.maximum(d, 1).astype(np.float32) / np.float32(max_exact)) / np.float32(
        math.log(MAX_DISTANCE / max_exact))
    large = max_exact + (ratio * np.float32(NUM_BUCKETS - max_exact)).astype(np.int32)
    large = np.minimum(large, NUM_BUCKETS - 1)
    return [int(np.min(d[(d >= max_exact) & (large >= b)])) for b in range(max_exact + 1, NUM_BUCKETS)]


BUCKET_THRESHOLDS = _bucket_thresholds()


def _params(*semantics):
    return pltpu.CompilerParams(dimension_semantics=semantics, vmem_limit_bytes=VMEM_LIMIT)


def _rms_rows(x, g):
    ms = jnp.mean(x * x, axis=-1, keepdims=True)
    return x * lax.rsqrt(ms + EPS) * g


NORM_ROWS = 256


def _inproj_kernel(x_ref, g_ref, w_ref, z_ref, xn_ref):
    @pl.when(pl.program_id(1) == 0)
    def _():
        def body(r, carry):
            rows = pl.ds(pl.multiple_of(r * NORM_ROWS, NORM_ROWS), NORM_ROWS)
            xn_ref[rows, :] = _rms_rows(x_ref[rows, :], g_ref[...]).astype(xn_ref.dtype)
            return carry

        lax.fori_loop(0, x_ref.shape[0] // NORM_ROWS, body, 0)

    z_ref[...] = jnp.dot(xn_ref[...], w_ref[...].astype(BF16), preferred_element_type=F32).astype(z_ref.dtype)


def _in_proj(x, g, w, tm, tn):
    m = x.shape[0]
    n = w.shape[1]
    gate_start = n - 2 * D_MODEL
    assert gate_start % tn == 0 and n % tn == 0
    gate_block, nblocks = gate_start // tn, n // tn
    return pl.pallas_call(
        _inproj_kernel,
        grid=(m // tm, nblocks),
        in_specs=[
            pl.BlockSpec((tm, D_MODEL), lambda i, j: (i, 0)),
            pl.BlockSpec((1, D_MODEL), lambda i, j: (0, 0)),
            pl.BlockSpec((D_MODEL, tn), lambda i, j: (0, (j + gate_block) % nblocks)),
        ],
        out_specs=pl.BlockSpec((tm, tn), lambda i, j: (i, j)),
        out_shape=jax.ShapeDtypeStruct((m, n), BF16),
        scratch_shapes=[pltpu.VMEM((tm, D_MODEL), BF16)],
        compiler_params=_params("arbitrary", "arbitrary"),
        name="in_proj",
    )(x, g, w)


CONV_COLS = 256


def _conv_prompt_kernel(b_ref, c_ref, h_ref, w_ref, y_ref, st_ref):
    u = c_ref[...].astype(F32) * h_ref[...].astype(F32)
    t = u.shape[0]
    row = lax.broadcasted_iota(jnp.int32, u.shape, 0)
    u1 = jnp.where(row >= 1, pltpu.roll(u, 1, 0), 0.0)
    u2 = jnp.where(row >= 2, pltpu.roll(u, 2, 0), 0.0)
    conv = w_ref[0:1, :] * u2 + w_ref[1:2, :] * u1 + w_ref[2:3, :] * u
    y_ref[...] = (b_ref[...].astype(F32) * conv).astype(y_ref.dtype)
    st_ref[...] = u[t - (CONV_WIDTH - 1):, :]


def _conv_prompt(z, w_conv, batch, seq):
    nct = D_CONV // CONV_COLS

    def col(off):
        return lambda b, j: (b, off // CONV_COLS + j)

    return pl.pallas_call(
        _conv_prompt_kernel,
        grid=(batch, nct),
        in_specs=[
            pl.BlockSpec((seq, CONV_COLS), col(OFF_B)),
            pl.BlockSpec((seq, CONV_COLS), col(OFF_C)),
            pl.BlockSpec((seq, CONV_COLS), col(OFF_H)),
            pl.BlockSpec((CONV_WIDTH, CONV_COLS), lambda b, j: (0, j)),
        ],
        out_specs=[
            pl.BlockSpec((seq, CONV_COLS), lambda b, j: (b, j)),
            pl.BlockSpec((None, CONV_WIDTH - 1, CONV_COLS), lambda b, j: (b, 0, j)),
        ],
        out_shape=[
            jax.ShapeDtypeStruct((batch * seq, D_CONV), BF16),
            jax.ShapeDtypeStruct((batch, CONV_WIDTH - 1, D_CONV), F32),
        ],
        compiler_params=_params("arbitrary", "arbitrary"),
        name="conv_prompt",
    )(z, z, z, w_conv)


def _conv_sample_kernel(b_ref, c_ref, h_ref, p1_ref, p2_ref, w_ref, y_ref, u_ref, *, steps):
    u = c_ref[...].astype(F32) * h_ref[...].astype(F32)
    t = lax.broadcasted_iota(jnp.int32, u.shape, 0) % steps
    u1 = jnp.where(t >= 1, pltpu.roll(u, 1, 0), p1_ref[...])
    u2 = jnp.where(t >= 2, pltpu.roll(u, 2, 0), p2_ref[...])
    conv = w_ref[0:1, :] * u2 + w_ref[1:2, :] * u1 + w_ref[2:3, :] * u
    y_ref[...] = (b_ref[...].astype(F32) * conv).astype(y_ref.dtype)
    u_ref[...] = u


def _conv_sample(z, prev1, prev2, w_conv, steps):
    rows = z.shape[0]
    nct = D_CONV // CONV_COLS

    def col(off):
        return lambda j: (0, off // CONV_COLS + j)

    plain = pl.BlockSpec((rows, CONV_COLS), lambda j: (0, j))
    return pl.pallas_call(
        functools.partial(_conv_sample_kernel, steps=steps),
        grid=(nct,),
        in_specs=[
            pl.BlockSpec((rows, CONV_COLS), col(OFF_B)),
            pl.BlockSpec((rows, CONV_COLS), col(OFF_C)),
            pl.BlockSpec((rows, CONV_COLS), col(OFF_H)),
            plain,
            plain,
            pl.BlockSpec((CONV_WIDTH, CONV_COLS), lambda j: (0, j)),
        ],
        out_specs=[plain, plain],
        out_shape=[
            jax.ShapeDtypeStruct((rows, D_CONV), BF16),
            jax.ShapeDtypeStruct((rows, D_CONV), F32),
        ],
        compiler_params=_params("arbitrary"),
        name="conv_sample",
    )(z, z, z, prev1, prev2, w_conv)


def _bias_from_distance(dist, head, relb_ref):
    valid = (dist >= 0) & (dist < WINDOW)
    d = jnp.clip(dist, 0, MAX_DISTANCE - 1)
    bucket = jnp.minimum(d, NUM_BUCKETS // 2)
    for thr in BUCKET_THRESHOLDS:
        bucket = bucket + (d >= thr).astype(jnp.int32)
    bias = jnp.zeros(dist.shape, F32)
    for b in range(NUM_BUCKETS):
        bias = jnp.where(bucket == b, relb_ref[b, head], bias)
    return jnp.where(valid, bias, NEG)


def _softmax_with_sink(s, sink):
    m = jnp.maximum(jnp.max(s, axis=-1, keepdims=True), sink)
    p = jnp.exp(s - m)
    denom = jnp.sum(p, axis=-1, keepdims=True) + jnp.exp(sink - m)
    return p * (1.0 / denom)


def _head_in_group_columns(q, h):
    half = D_KV // 2
    assert 2 * HEAD_DIM == half and GROUP * HEAD_DIM == D_KV
    g = h // GROUP
    src = q[:, (h // 2) * half:(h // 2 + 1) * half]
    if h % 2 != g % 2:
        src = pltpu.roll(src, HEAD_DIM, 1)
    lane_half = lax.broadcasted_iota(jnp.int32, src.shape, 1) // HEAD_DIM
    placed = jnp.where(lane_half == g % 2, src, 0.0)
    zero = jnp.zeros_like(placed)
    return jnp.concatenate([placed, zero] if g // 2 == 0 else [zero, placed], axis=1)


def _group_columns(h):
    g = h // GROUP
    return slice(g * HEAD_DIM, (g + 1) * HEAD_DIM)


def _attn_prompt_kernel(relb_ref, sink_ref, q_ref, kp_ref, kc_ref, vp_ref, vc_ref, o_ref, bias_ref, lhs_ref):
    b = pl.program_id(0)
    n = pl.program_id(1)
    qi = lax.broadcasted_iota(jnp.int32, (WINDOW, WINDOW), 0)
    kj = lax.broadcasted_iota(jnp.int32, (WINDOW, WINDOW), 1)
    from_prev = kj > qi

    @pl.when((b == 0) & (n == 0))
    def _():
        dist = jnp.where(from_prev, qi + WINDOW - kj, qi - kj)
        for h in range(N_HEADS):
            rows = slice(h * WINDOW, (h + 1) * WINDOW)
            bias = _bias_from_distance(dist, h, relb_ref)
            bias_ref[1, rows, :] = bias
            bias_ref[0, rows, :] = jnp.where(from_prev, NEG, bias)

    q = q_ref[...].astype(F32) * (HEAD_DIM ** -0.5)
    for h in range(N_HEADS):
        lhs_ref[h * WINDOW:(h + 1) * WINDOW, :] = _head_in_group_columns(q, h).astype(BF16)
    kk = jnp.concatenate([kp_ref[...], kc_ref[...]], axis=0)
    vv = jnp.concatenate([vp_ref[...], vc_ref[...]], axis=0)
    s = lax.dot_general(lhs_ref[...], kk, (((1,), (1,)), ((), ())), preferred_element_type=F32)
    has_prev = jnp.minimum(n, 1)
    probs, sink_terms = [], []
    for h in range(N_HEADS):
        rows = slice(h * WINDOW, (h + 1) * WINDOW)
        sh = jnp.where(from_prev, s[rows, :WINDOW], s[rows, WINDOW:]) + bias_ref[has_prev, rows, :]
        m = jnp.maximum(jnp.max(sh, axis=-1, keepdims=True), sink_ref[h])
        p = jnp.exp(sh - m)
        probs.append(jnp.concatenate([jnp.where(from_prev, p, 0.0), jnp.where(from_prev, 0.0, p)],
                                     axis=1).astype(BF16))
        sink_terms.append(jnp.exp(sink_ref[h] - m))
    pm = jnp.concatenate(probs, axis=0)
    rowsum = jnp.dot(pm, jnp.ones((2 * WINDOW, LANES), BF16), preferred_element_type=F32)
    outs = []
    for g in range(N_KV_HEADS):
        og = jnp.dot(pm[g * GROUP * WINDOW:(g + 1) * GROUP * WINDOW], vv[:, g * HEAD_DIM:(g + 1) * HEAD_DIM],
                     preferred_element_type=F32)
        for r in range(GROUP):
            h = g * GROUP + r
            inv = 1.0 / (rowsum[h * WINDOW:(h + 1) * WINDOW] + sink_terms[h])
            outs.append(og[r * WINDOW:(r + 1) * WINDOW] * inv[:, :HEAD_DIM])
    o_ref[...] = jnp.concatenate(outs, axis=1).astype(o_ref.dtype)


def _attn_prompt(z, rel_bias, sinks, batch, seq):
    nb = seq // WINDOW

    def cur(off, width):
        return lambda b, n: (b * nb + n, off // width)

    def prev(off, width):
        return lambda b, n: (b * nb + jnp.maximum(n - 1, 0), off // width)

    smem = pl.BlockSpec(memory_space=pltpu.SMEM)
    return pl.pallas_call(
        _attn_prompt_kernel,
        grid=(batch, nb),
        in_specs=[
            smem,
            smem,
            pl.BlockSpec((WINDOW, D_ATTN), cur(OFF_Q, D_ATTN)),
            pl.BlockSpec((WINDOW, D_KV), prev(OFF_K, D_KV)),
            pl.BlockSpec((WINDOW, D_KV), cur(OFF_K, D_KV)),
            pl.BlockSpec((WINDOW, D_KV), prev(OFF_V, D_KV)),
            pl.BlockSpec((WINDOW, D_KV), cur(OFF_V, D_KV)),
        ],
        out_specs=pl.BlockSpec((WINDOW, D_ATTN), lambda b, n: (b * nb + n, 0)),
        out_shape=jax.ShapeDtypeStruct((batch * seq, D_ATTN), BF16),
        scratch_shapes=[pltpu.VMEM((2, N_HEADS * WINDOW, WINDOW), F32),
                        pltpu.VMEM((N_HEADS * WINDOW, D_KV), BF16)],
        compiler_params=_params("arbitrary", "arbitrary"),
        name="attn_prompt",
    )(rel_bias, sinks, z, z, z, z, z)


SAMPLE_BATCH_TILE = 16


def _attn_sample_kernel(relb_ref, sink_ref, q_ref, kn_ref, vn_ref, ck_ref, cv_ref,
                        o_ref, nk_ref, nv_ref,
                        bias1_ref, bias2_ref, sinkv_ref, z_ref, s_ref, p_ref, oh_ref, *, steps):
    bt, wc, _ = ck_ref.shape
    tile_rows = bt * steps

    @pl.when(pl.program_id(0) == 0)
    def _():
        t = lax.broadcasted_iota(jnp.int32, (steps, wc), 0)
        j = lax.broadcasted_iota(jnp.int32, (steps, wc), 1)
        t2 = lax.broadcasted_iota(jnp.int32, (steps, tile_rows), 0)
        j2 = lax.broadcasted_iota(jnp.int32, (steps, tile_rows), 1)
        own = jnp.where(j2 < steps, t2 - j2, -1)
        for h in range(N_HEADS):
            sl = slice(h * steps, (h + 1) * steps)
            bias1_ref[sl, :] = _bias_from_distance(t + wc - j, h, relb_ref)
            bias2_ref[0, sl, :] = _bias_from_distance(own, h, relb_ref)
            sinkv_ref[sl, :] = jnp.full((steps, LANES), sink_ref[h], F32)
        for b in range(1, bt):
            bias2_ref[b] = pltpu.roll(bias2_ref[0], b * steps, 1)

    kn = kn_ref[...]
    vn = vn_ref[...]
    nk_ref[:, 0:wc - steps, :] = ck_ref[:, steps:, :]
    nk_ref[:, wc - steps:wc, :] = kn.astype(F32).reshape(bt, steps, D_KV)
    nv_ref[:, 0:wc - steps, :] = cv_ref[:, steps:, :]
    nv_ref[:, wc - steps:wc, :] = vn.astype(F32).reshape(bt, steps, D_KV)

    q = q_ref[...].astype(F32) * (HEAD_DIM ** -0.5)
    for h in range(N_HEADS):
        z_ref[:, h * steps:(h + 1) * steps, :] = _head_in_group_columns(q, h).reshape(bt, steps, D_KV)

    for b in range(bt):
        keys = jnp.concatenate([ck_ref[b].astype(BF16), kn], axis=0)
        s_ref[b] = lax.dot_general(z_ref[b].astype(BF16), keys, (((1,), (1,)), ((), ())),
                                   preferred_element_type=F32)

    bias = jnp.concatenate([jnp.broadcast_to(bias1_ref[...], (bt,) + bias1_ref.shape), bias2_ref[...]], axis=2)
    s = s_ref[...] + bias
    sink = sinkv_ref[...][None]
    m = jnp.maximum(jnp.max(s, axis=-1, keepdims=True), sink)
    p_ref[...] = jnp.exp(s - jnp.concatenate([m] * (s.shape[-1] // LANES), axis=-1)).astype(BF16)
    sink_term = jnp.exp(sink - m)

    vn_ones = jnp.concatenate([vn, jnp.ones((tile_rows, LANES), BF16)], axis=1)
    for b in range(bt):
        cv_ones = jnp.concatenate([cv_ref[b].astype(BF16), jnp.ones((wc, LANES), BF16)], axis=1)
        oh_ref[b] = jnp.dot(p_ref[b], jnp.concatenate([cv_ones, vn_ones], axis=0), preferred_element_type=F32)

    inv = 1.0 / (oh_ref[:, :, D_KV:] + sink_term)
    outs = []
    for h in range(N_HEADS):
        rows = slice(h * steps, (h + 1) * steps)
        outs.append((oh_ref[:, rows, _group_columns(h)] * inv[:, rows, :HEAD_DIM]).reshape(tile_rows, HEAD_DIM))
    o_ref[...] = jnp.concatenate(outs, axis=1).astype(o_ref.dtype)


def _attn_sample(z, cache_k, cache_v, rel_bias, sinks, batch, steps):
    wc = cache_k.shape[1]
    bt = SAMPLE_BATCH_TILE
    rows = bt * steps
    smem = pl.BlockSpec(memory_space=pltpu.SMEM)
    cache_spec = pl.BlockSpec((bt, wc, D_KV), lambda i: (i, 0, 0))
    return pl.pallas_call(
        functools.partial(_attn_sample_kernel, steps=steps),
        grid=(batch // bt,),
        in_specs=[
            smem,
            smem,
            pl.BlockSpec((rows, D_ATTN), lambda i: (i, OFF_Q // D_ATTN)),
            pl.BlockSpec((rows, D_KV), lambda i: (i, OFF_K // D_KV)),
            pl.BlockSpec((rows, D_KV), lambda i: (i, OFF_V // D_KV)),
            cache_spec,
            cache_spec,
        ],
        out_specs=[
            pl.BlockSpec((rows, D_ATTN), lambda i: (i, 0)),
            cache_spec,
            cache_spec,
        ],
        out_shape=[
            jax.ShapeDtypeStruct((batch * steps, D_ATTN), BF16),
            jax.ShapeDtypeStruct(cache_k.shape, F32),
            jax.ShapeDtypeStruct(cache_v.shape, F32),
        ],
        scratch_shapes=[
            pltpu.VMEM((N_HEADS * steps, wc), F32),
            pltpu.VMEM((bt, N_HEADS * steps, rows), F32),
            pltpu.VMEM((N_HEADS * steps, LANES), F32),
            pltpu.VMEM((bt, N_HEADS * steps, D_KV), F32),
            pltpu.VMEM((bt, N_HEADS * steps, wc + rows), F32),
            pltpu.VMEM((bt, N_HEADS * steps, wc + rows), BF16),
            pltpu.VMEM((bt, N_HEADS * steps, D_KV + LANES), F32),
        ],
        compiler_params=_params("arbitrary"),
        name="attn_sample",
    )(rel_bias, sinks, z, z, z, cache_k, cache_v)


def _mix_kernel(yc_ref, at_ref, gc_ref, ga_ref, x_ref, g_ref, wco_ref, wao_ref, wo_ref, h_ref, hn_ref):
    y_conv = jnp.dot(yc_ref[...], wco_ref[...], preferred_element_type=F32)
    y_attn = jnp.dot(at_ref[...], wao_ref[...], preferred_element_type=F32)
    merged = (jax.nn.sigmoid(gc_ref[...].astype(F32)) * y_conv
              + jax.nn.sigmoid(ga_ref[...].astype(F32)) * y_attn)
    h = x_ref[...] + jnp.dot(merged.astype(BF16), wo_ref[...], preferred_element_type=F32)
    h_ref[...] = h
    hn_ref[...] = _rms_rows(h, g_ref[...]).astype(hn_ref.dtype)


def _mix(yc, attn, z, x, g_ffn, wco, wao, wo, tm):
    m = x.shape[0]
    resident = functools.partial(pl.BlockSpec, index_map=lambda i: (0, 0), pipeline_mode=pl.Buffered(1))
    row_spec = pl.BlockSpec((tm, D_MODEL), lambda i: (i, 0))
    return pl.pallas_call(
        _mix_kernel,
        grid=(m // tm,),
        in_specs=[
            pl.BlockSpec((tm, D_CONV), lambda i: (i, 0)),
            pl.BlockSpec((tm, D_ATTN), lambda i: (i, 0)),
            pl.BlockSpec((tm, D_MODEL), lambda i: (i, OFF_GC // D_MODEL)),
            pl.BlockSpec((tm, D_MODEL), lambda i: (i, OFF_GA // D_MODEL)),
            row_spec,
            pl.BlockSpec((1, D_MODEL), lambda i: (0, 0)),
            resident((D_CONV, D_MODEL)),
            resident((D_ATTN, D_MODEL)),
            resident((D_MODEL, D_MODEL)),
        ],
        out_specs=[row_spec, row_spec],
        out_shape=[jax.ShapeDtypeStruct((m, D_MODEL), F32), jax.ShapeDtypeStruct((m, D_MODEL), BF16)],
        compiler_params=_params("arbitrary"),
        name="mix",
    )(yc, attn, z, z, x, g_ffn, wco, wao, wo)


def _ffn_kernel(hn_ref, h_ref, gf_ref, wg_ref, wu_ref, wd_ref, y_ref, *, final_norm):
    j = pl.program_id(1)

    @pl.when(j == 0)
    def _():
        y_ref[...] = h_ref[...]

    hn = hn_ref[...]
    a = (jax.nn.silu(jnp.dot(hn, wg_ref[...].astype(BF16), preferred_element_type=F32))
         * jnp.dot(hn, wu_ref[...].astype(BF16), preferred_element_type=F32))
    y_ref[...] += jnp.dot(a.astype(BF16), wd_ref[...].astype(BF16), preferred_element_type=F32)

    if final_norm:
        @pl.when(j == pl.num_programs(1) - 1)
        def _():
            def body(r, carry):
                rows = pl.ds(pl.multiple_of(r * NORM_ROWS, NORM_ROWS), NORM_ROWS)
                y_ref[rows, :] = _rms_rows(y_ref[rows, :], gf_ref[...])
                return carry

            lax.fori_loop(0, y_ref.shape[0] // NORM_ROWS, body, 0)


def _ffn(hn, h, g_final, wg, wu, wd, tm, tf, final_norm):
    m = h.shape[0]
    return pl.pallas_call(
        functools.partial(_ffn_kernel, final_norm=final_norm),
        grid=(m // tm, D_FF // tf),
        in_specs=[
            pl.BlockSpec((tm, D_MODEL), lambda i, j: (i, 0)),
            pl.BlockSpec((tm, D_MODEL), lambda i, j: (i, 0), pipeline_mode=pl.Buffered(1)),
            pl.BlockSpec((1, D_MODEL), lambda i, j: (0, 0)),
            pl.BlockSpec((D_MODEL, tf), lambda i, j: (0, j)),
            pl.BlockSpec((D_MODEL, tf), lambda i, j: (0, j)),
            pl.BlockSpec((tf, D_MODEL), lambda i, j: (j, 0)),
        ],
        out_specs=pl.BlockSpec((tm, D_MODEL), lambda i, j: (i, 0)),
        out_shape=jax.ShapeDtypeStruct((m, D_MODEL), F32),
        compiler_params=_params("arbitrary", "arbitrary"),
        name="ffn",
    )(hn, h, g_final, wg, wu, wd)


IN_PROJ_TM_PROMPT = 2048
IN_PROJ_TM_SAMPLE = 1024
IN_PROJ_TN = 512
MIX_TM = 256
FFN_TM = 1024
FFN_TF = 256


def _dense_tail(x, z, yc, attn, lw, g_final, final_norm):
    h, hn = _mix(yc, attn, z, x, lw["g_ffn"], lw["wco"], lw["wao"], lw["wo"], MIX_TM)
    return _ffn(hn, h, g_final, lw["wg"], lw["wu"], lw["wd"], FFN_TM, FFN_TF, final_norm)


def kernel(x_prompt, x_sample, cache_k, cache_v, state_conv, rel_bias, w_in, w_conv, w_conv_out, sinks,
           w_attn_out, w_o, g_mix, g_ffn, w_gate, w_up, w_down, g_final):
    depth = w_in.shape[0]
    batch, seq, _ = x_prompt.shape
    dec_batch, steps, _ = x_sample.shape
    wc = cache_k.shape[2]
    assert seq % WINDOW == 0 and wc == WINDOW and dec_batch % SAMPLE_BATCH_TILE == 0

    hp = x_prompt.reshape(batch * seq, D_MODEL)
    hs = x_sample.reshape(dec_batch * steps, D_MODEL)
    g_final2 = g_final.reshape(1, D_MODEL)
    kp_l, vp_l, cp_l, ks_l, vs_l, cs_l = [], [], [], [], [], []
    for l in range(depth):
        final = l == depth - 1
        lw = {
            "w_in": w_in[l],
            "wco": w_conv_out[l].astype(BF16),
            "wao": w_attn_out[l].astype(BF16),
            "wo": w_o[l].astype(BF16),
            "wg": w_gate[l],
            "wu": w_up[l],
            "wd": w_down[l],
            "g_ffn": g_ffn[l].reshape(1, D_MODEL),
        }
        g_mix_l = g_mix[l].reshape(1, D_MODEL)

        zp = _in_proj(hp, g_mix_l, lw["w_in"], IN_PROJ_TM_PROMPT, IN_PROJ_TN)
        ycp, conv_p = _conv_prompt(zp, w_conv[l], batch, seq)
        attn_p = _attn_prompt(zp, rel_bias, sinks[l], batch, seq)
        hp = _dense_tail(hp, zp, ycp, attn_p, lw, g_final2, final)
        kv_p = zp.reshape(batch, seq, D_IN_PROJ)[:, seq - WINDOW:, OFF_K:OFF_V + D_KV].astype(F32)
        kp_l.append(kv_p[..., :D_KV].reshape(batch, WINDOW, N_KV_HEADS, HEAD_DIM))
        vp_l.append(kv_p[..., D_KV:].reshape(batch, WINDOW, N_KV_HEADS, HEAD_DIM))
        cp_l.append(conv_p)

        zs = _in_proj(hs, g_mix_l, lw["w_in"], IN_PROJ_TM_SAMPLE, IN_PROJ_TN)
        st = state_conv[l]
        zero = jnp.zeros((dec_batch, steps - 2, D_CONV), F32)
        prev1 = jnp.concatenate([st[:, 1:2], zero, zero[:, :1]], axis=1).reshape(dec_batch * steps, D_CONV)
        prev2 = jnp.concatenate([st, zero], axis=1).reshape(dec_batch * steps, D_CONV)
        ycs, u_s = _conv_sample(zs, prev1, prev2, w_conv[l], steps)
        attn_s, nk, nv = _attn_sample(zs, cache_k[l].reshape(dec_batch, wc, D_KV),
                                      cache_v[l].reshape(dec_batch, wc, D_KV), rel_bias, sinks[l],
                                      dec_batch, steps)
        hs = _dense_tail(hs, zs, ycs, attn_s, lw, g_final2, final)
        ks_l.append(nk.reshape(dec_batch, wc, N_KV_HEADS, HEAD_DIM))
        vs_l.append(nv.reshape(dec_batch, wc, N_KV_HEADS, HEAD_DIM))
        cs_l.append(u_s.reshape(dec_batch, steps, D_CONV)[:, steps - (CONV_WIDTH - 1):])

    return (hp.reshape(batch, seq, D_MODEL), hs.reshape(dec_batch, steps, D_MODEL),
            jnp.stack(kp_l), jnp.stack(vp_l), jnp.stack(cp_l),
            jnp.stack(ks_l), jnp.stack(vs_l), jnp.stack(cs_l))
```

```python
import functools
import math

import numpy as np
import jax
import jax.numpy as jnp
from jax import lax
from jax.experimental import pallas as pl
from jax.experimental.pallas import tpu as pltpu

D_MODEL = 2048
D_CONV = D_MODEL // 2
CONV_WIDTH = 3
HEAD_DIM = 64
N_HEADS = (D_MODEL // 2) // HEAD_DIM
N_KV_HEADS = N_HEADS // 4
GROUP = N_HEADS // N_KV_HEADS
D_ATTN = N_HEADS * HEAD_DIM
D_KV = N_KV_HEADS * HEAD_DIM
WINDOW = 128
NUM_BUCKETS = 32
MAX_DISTANCE = 128
D_FF = -(-8 * D_MODEL // (3 * 256)) * 256
EPS = 1e-6
D_IN_PROJ = 3 * D_CONV + D_ATTN + 2 * D_KV + 2 * D_MODEL

OFF_GC = 0
OFF_GA = D_MODEL
OFF_B = 2 * D_MODEL
OFF_C = OFF_B + D_CONV
OFF_H = OFF_C + D_CONV
OFF_Q = OFF_H + D_CONV
OFF_K = OFF_Q + D_ATTN
OFF_V = OFF_K + D_KV

LANES = 128
NEG = -1e30
F32 = jnp.float32
BF16 = jnp.bfloat16
VMEM_LIMIT = 60 * 1024 * 1024


def _bucket_thresholds():
    max_exact = NUM_BUCKETS // 2
    d = np.arange(MAX_DISTANCE)
    ratio = np.log(np.maximum(d, 1).astype(np.float32) / np.float32(max_exact)) / np.float32(
        math.log(MAX_DISTANCE / max_exact))
    large = max_exact + (ratio * np.float32(NUM_BUCKETS - max_exact)).astype(np.int32)
    large = np.minimum(large, NUM_BUCKETS - 1)
    return [int(np.min(d[(d >= max_exact) & (large >= b)])) for b in range(max_exact + 1, NUM_BUCKETS)]


BUCKET_THRESHOLDS = _bucket_thresholds()


def _params(*semantics):
    return pltpu.CompilerParams(dimension_semantics=semantics, vmem_limit_bytes=VMEM_LIMIT)


def _rms_rows(x, g):
    ms = jnp.mean(x * x, axis=-1, keepdims=True)
    return x * lax.rsqrt(ms + EPS) * g


NORM_ROWS = 256


def _inproj_kernel(x_ref, g_ref, w_ref, z_ref, xn_ref):
    @pl.when(pl.program_id(1) == 0)
    def _():
        def body(r, carry):
            rows = pl.ds(pl.multiple_of(r * NORM_ROWS, NORM_ROWS), NORM_ROWS)
            xn_ref[rows, :] = _rms_rows(x_ref[rows, :], g_ref[...]).astype(xn_ref.dtype)
            return carry

        lax.fori_loop(0, x_ref.shape[0] // NORM_ROWS, body, 0)

    z_ref[...] = jnp.dot(xn_ref[...], w_ref[...].astype(BF16), preferred_element_type=F32).astype(z_ref.dtype)


def _in_proj(x, g, w, tm, tn):
    m = x.shape[0]
    n = w.shape[1]
    gate_start = n - 2 * D_MODEL
    assert gate_start % tn == 0 and n % tn == 0
    gate_block, nblocks = gate_start // tn, n // tn
    return pl.pallas_call(
        _inproj_kernel,
        grid=(m // tm, nblocks),
        in_specs=[
            pl.BlockSpec((tm, D_MODEL), lambda i, j: (i, 0)),
            pl.BlockSpec((1, D_MODEL), lambda i, j: (0, 0)),
            pl.BlockSpec((D_MODEL, tn), lambda i, j: (0, (j + gate_block) % nblocks)),
        ],
        out_specs=pl.BlockSpec((tm, tn), lambda i, j: (i, j)),
        out_shape=jax.ShapeDtypeStruct((m, n), BF16),
        scratch_shapes=[pltpu.VMEM((tm, D_MODEL), BF16)],
        compiler_params=_params("arbitrary", "arbitrary"),
        name="in_proj",
    )(x, g, w)


def _bias_from_distance(dist, head, relb_ref):
    valid = (dist >= 0) & (dist < WINDOW)
    d = jnp.clip(dist, 0, MAX_DISTANCE - 1)
    bucket = jnp.minimum(d, NUM_BUCKETS // 2)
    for thr in BUCKET_THRESHOLDS:
        bucket = bucket + (d >= thr).astype(jnp.int32)
    bias = jnp.zeros(dist.shape, F32)
    for b in range(NUM_BUCKETS):
        bias = jnp.where(bucket == b, relb_ref[b, head], bias)
    return jnp.where(valid, bias, NEG)


def _softmax_with_sink(s, sink):
    m = jnp.maximum(jnp.max(s, axis=-1, keepdims=True), sink)
    p = jnp.exp(s - m)
    denom = jnp.sum(p, axis=-1, keepdims=True) + jnp.exp(sink - m)
    return p * (1.0 / denom)


def _head_in_group_columns(q, h):
    half = D_KV // 2
    assert 2 * HEAD_DIM == half and GROUP * HEAD_DIM == D_KV
    g = h // GROUP
    src = q[:, (h // 2) * half:(h // 2 + 1) * half]
    if h % 2 != g % 2:
        src = pltpu.roll(src, HEAD_DIM, 1)
    lane_half = lax.broadcasted_iota(jnp.int32, src.shape, 1) // HEAD_DIM
    placed = jnp.where(lane_half == g % 2, src, 0.0)
    zero = jnp.zeros_like(placed)
    return jnp.concatenate([placed, zero] if g // 2 == 0 else [zero, placed], axis=1)


def _group_columns(h):
    g = h // GROUP
    return slice(g * HEAD_DIM, (g + 1) * HEAD_DIM)


PROMPT_BLOCKS_PER_STEP = 4


def _attn_prompt_kernel(relb_ref, sink_ref, q_ref, kp_ref, kc_ref, vp_ref, vc_ref, o_ref, bias_ref, lhs_ref):
    b = pl.program_id(0)
    n = pl.program_id(1)
    qi = lax.broadcasted_iota(jnp.int32, (WINDOW, WINDOW), 0)
    kj = lax.broadcasted_iota(jnp.int32, (WINDOW, WINDOW), 1)
    from_prev = kj > qi

    @pl.when((b == 0) & (n == 0))
    def _():
        dist = jnp.where(from_prev, qi + WINDOW - kj, qi - kj)
        for h in range(N_HEADS):
            rows = slice(h * WINDOW, (h + 1) * WINDOW)
            bias = _bias_from_distance(dist, h, relb_ref)
            bias_ref[1, rows, :] = bias
            bias_ref[0, rows, :] = jnp.where(from_prev, NEG, bias)

    k_all = jnp.concatenate([kp_ref[...], kc_ref[...]], axis=0)
    v_all = jnp.concatenate([vp_ref[...], vc_ref[...]], axis=0)
    for blk in range(PROMPT_BLOCKS_PER_STEP):
        rows_q = slice(blk * WINDOW, (blk + 1) * WINDOW)
        keys = slice(blk * WINDOW, (blk + 2) * WINDOW)
        has_prev = jnp.minimum(n, 1) if blk == 0 else 1
        o_ref[rows_q, :] = _attend_block(q_ref[rows_q, :], k_all[keys], v_all[keys], has_prev, from_prev,
                                         sink_ref, bias_ref, lhs_ref.at[blk]).astype(o_ref.dtype)


def _attend_block(q, kk, vv, has_prev, from_prev, sink_ref, bias_ref, lhs_ref):
    q = q.astype(F32) * (HEAD_DIM ** -0.5)
    for h in range(N_HEADS):
        lhs_ref[h * WINDOW:(h + 1) * WINDOW, :] = _head_in_group_columns(q, h).astype(BF16)
    s = lax.dot_general(lhs_ref[...], kk, (((1,), (1,)), ((), ())), preferred_element_type=F32)
    probs, sink_terms = [], []
    for h in range(N_HEADS):
        rows = slice(h * WINDOW, (h + 1) * WINDOW)
        sh = jnp.where(from_prev, s[rows, :WINDOW], s[rows, WINDOW:]) + bias_ref[has_prev, rows, :]
        m = jnp.maximum(jnp.max(sh, axis=-1, keepdims=True), sink_ref[h])
        p = jnp.exp(sh - m)
        probs.append(jnp.concatenate([jnp.where(from_prev, p, 0.0), jnp.where(from_prev, 0.0, p)],
                                     axis=1).astype(BF16))
        sink_terms.append(jnp.exp(sink_ref[h] - m))
    pm = jnp.concatenate(probs, axis=0)
    rowsum = jnp.dot(pm, jnp.ones((2 * WINDOW, LANES), BF16), preferred_element_type=F32)
    outs = []
    for g in range(N_KV_HEADS):
        og = jnp.dot(pm[g * GROUP * WINDOW:(g + 1) * GROUP * WINDOW], vv[:, g * HEAD_DIM:(g + 1) * HEAD_DIM],
                     preferred_element_type=F32)
        for r in range(GROUP):
            h = g * GROUP + r
            inv = 1.0 / (rowsum[h * WINDOW:(h + 1) * WINDOW] + sink_terms[h])
            outs.append(og[r * WINDOW:(r + 1) * WINDOW] * inv[:, :HEAD_DIM])
    return jnp.concatenate(outs, axis=1)


def _attn_prompt(z, rel_bias, sinks, batch, seq):
    bps = PROMPT_BLOCKS_PER_STEP
    assert seq % (bps * WINDOW) == 0
    ns = seq // (bps * WINDOW)

    def cur(off, width):
        return lambda b, n: (b * ns + n, off // width)

    def prev(off, width):
        return lambda b, n: ((b * ns + n) * bps - jnp.minimum(n, 1), off // width)

    smem = pl.BlockSpec(memory_space=pltpu.SMEM)
    return pl.pallas_call(
        _attn_prompt_kernel,
        grid=(batch, ns),
        in_specs=[
            smem,
            smem,
            pl.BlockSpec((bps * WINDOW, D_ATTN), cur(OFF_Q, D_ATTN)),
            pl.BlockSpec((WINDOW, D_KV), prev(OFF_K, D_KV)),
            pl.BlockSpec((bps * WINDOW, D_KV), cur(OFF_K, D_KV)),
            pl.BlockSpec((WINDOW, D_KV), prev(OFF_V, D_KV)),
            pl.BlockSpec((bps * WINDOW, D_KV), cur(OFF_V, D_KV)),
        ],
        out_specs=pl.BlockSpec((bps * WINDOW, D_ATTN), lambda b, n: (b * ns + n, 0)),
        out_shape=jax.ShapeDtypeStruct((batch * seq, D_ATTN), BF16),
        scratch_shapes=[pltpu.VMEM((2, N_HEADS * WINDOW, WINDOW), F32),
                        pltpu.VMEM((bps, N_HEADS * WINDOW, D_KV), BF16)],
        compiler_params=_params("arbitrary", "arbitrary"),
        name="attn_prompt",
    )(rel_bias, sinks, z, z, z, z, z)


SAMPLE_BATCH_TILE = 16


def _attn_sample_kernel(relb_ref, sink_ref, q_ref, kn_ref, vn_ref, ck_ref, cv_ref,
                        o_ref, nk_ref, nv_ref,
                        bias1_ref, bias2_ref, sinkv_ref, z_ref, s_ref, p_ref, oh_ref, *, steps):
    bt, wc, _ = ck_ref.shape
    tile_rows = bt * steps

    @pl.when(pl.program_id(0) == 0)
    def _():
        t = lax.broadcasted_iota(jnp.int32, (steps, wc), 0)
        j = lax.broadcasted_iota(jnp.int32, (steps, wc), 1)
        t2 = lax.broadcasted_iota(jnp.int32, (steps, tile_rows), 0)
        j2 = lax.broadcasted_iota(jnp.int32, (steps, tile_rows), 1)
        own = jnp.where(j2 < steps, t2 - j2, -1)
        for h in range(N_HEADS):
            sl = slice(h * steps, (h + 1) * steps)
            bias1_ref[sl, :] = _bias_from_distance(t + wc - j, h, relb_ref)
            bias2_ref[0, sl, :] = _bias_from_distance(own, h, relb_ref)
            sinkv_ref[sl, :] = jnp.full((steps, LANES), sink_ref[h], F32)
        for b in range(1, bt):
            bias2_ref[b] = pltpu.roll(bias2_ref[0], b * steps, 1)

    kn = kn_ref[...]
    vn = vn_ref[...]
    nk_ref[:, 0:wc - steps, :] = ck_ref[:, steps:, :]
    nk_ref[:, wc - steps:wc, :] = kn.astype(F32).reshape(bt, steps, D_KV)
    nv_ref[:, 0:wc - steps, :] = cv_ref[:, steps:, :]
    nv_ref[:, wc - steps:wc, :] = vn.astype(F32).reshape(bt, steps, D_KV)

    q = q_ref[...].astype(F32) * (HEAD_DIM ** -0.5)
    for h in range(N_HEADS):
        z_ref[:, h * steps:(h + 1) * steps, :] = _head_in_group_columns(q, h).reshape(bt, steps, D_KV)

    for b in range(bt):
        keys = jnp.concatenate([ck_ref[b].astype(BF16), kn], axis=0)
        s_ref[b] = lax.dot_general(z_ref[b].astype(BF16), keys, (((1,), (1,)), ((), ())),
                                   preferred_element_type=F32)

    bias = jnp.concatenate([jnp.broadcast_to(bias1_ref[...], (bt,) + bias1_ref.shape), bias2_ref[...]], axis=2)
    s = s_ref[...] + bias
    sink = sinkv_ref[...][None]
    m = jnp.maximum(jnp.max(s, axis=-1, keepdims=True), sink)
    p_ref[...] = jnp.exp(s - jnp.concatenate([m] * (s.shape[-1] // LANES), axis=-1)).astype(BF16)
    sink_term = jnp.exp(sink - m)

    vn_ones = jnp.concatenate([vn, jnp.ones((tile_rows, LANES), BF16)], axis=1)
    for b in range(bt):
        cv_ones = jnp.concatenate([cv_ref[b].astype(BF16), jnp.ones((wc, LANES), BF16)], axis=1)
        oh_ref[b] = jnp.dot(p_ref[b], jnp.concatenate([cv_ones, vn_ones], axis=0), preferred_element_type=F32)

    inv = 1.0 / (oh_ref[:, :, D_KV:] + sink_term)
    outs = []
    for h in range(N_HEADS):
        rows = slice(h * steps, (h + 1) * steps)
        outs.append((oh_ref[:, rows, _group_columns(h)] * inv[:, rows, :HEAD_DIM]).reshape(tile_rows, HEAD_DIM))
    o_ref[...] = jnp.concatenate(outs, axis=1).astype(o_ref.dtype)


def _attn_sample(z, cache_k, cache_v, rel_bias, sinks, batch, steps):
    wc = cache_k.shape[1]
    bt = SAMPLE_BATCH_TILE
    rows = bt * steps
    smem = pl.BlockSpec(memory_space=pltpu.SMEM)
    cache_spec = pl.BlockSpec((bt, wc, D_KV), lambda i: (i, 0, 0))
    return pl.pallas_call(
        functools.partial(_attn_sample_kernel, steps=steps),
        grid=(batch // bt,),
        in_specs=[
            smem,
            smem,
            pl.BlockSpec((rows, D_ATTN), lambda i: (i, OFF_Q // D_ATTN)),
            pl.BlockSpec((rows, D_KV), lambda i: (i, OFF_K // D_KV)),
            pl.BlockSpec((rows, D_KV), lambda i: (i, OFF_V // D_KV)),
            cache_spec,
            cache_spec,
        ],
        out_specs=[
            pl.BlockSpec((rows, D_ATTN), lambda i: (i, 0)),
            cache_spec,
            cache_spec,
        ],
        out_shape=[
            jax.ShapeDtypeStruct((batch * steps, D_ATTN), BF16),
            jax.ShapeDtypeStruct(cache_k.shape, F32),
            jax.ShapeDtypeStruct(cache_v.shape, F32),
        ],
        scratch_shapes=[
            pltpu.VMEM((N_HEADS * steps, wc), F32),
            pltpu.VMEM((bt, N_HEADS * steps, rows), F32),
            pltpu.VMEM((N_HEADS * steps, LANES), F32),
            pltpu.VMEM((bt, N_HEADS * steps, D_KV), F32),
            pltpu.VMEM((bt, N_HEADS * steps, wc + rows), F32),
            pltpu.VMEM((bt, N_HEADS * steps, wc + rows), BF16),
            pltpu.VMEM((bt, N_HEADS * steps, D_KV + LANES), F32),
        ],
        compiler_params=_params("arbitrary"),
        name="attn_sample",
    )(rel_bias, sinks, z, z, z, cache_k, cache_v)


CARRY_ROWS = 8


def _shifted_u(u, us_ref):
    tm = u.shape[0]
    us_ref[CARRY_ROWS:, :] = u
    return us_ref[CARRY_ROWS - 1:CARRY_ROWS - 1 + tm, :], us_ref[CARRY_ROWS - 2:CARRY_ROWS - 2 + tm, :]


def _mix_tail(b_ref, wc_ref, u, u1, u2, at_ref, gc_ref, ga_ref, x_ref, g_ref, wco_ref, wao_ref, wo_ref,
              h_ref, hn_ref):
    conv = wc_ref[0:1, :] * u2 + wc_ref[1:2, :] * u1 + wc_ref[2:3, :] * u
    yc = (b_ref[...].astype(F32) * conv).astype(BF16)
    y_conv = jnp.dot(yc, wco_ref[...], preferred_element_type=F32)
    y_attn = jnp.dot(at_ref[...], wao_ref[...], preferred_element_type=F32)
    merged = (jax.nn.sigmoid(gc_ref[...].astype(F32)) * y_conv
              + jax.nn.sigmoid(ga_ref[...].astype(F32)) * y_attn)
    h = x_ref[...] + jnp.dot(merged.astype(BF16), wo_ref[...], preferred_element_type=F32)
    h_ref[...] = h
    hn_ref[...] = _rms_rows(h, g_ref[...]).astype(hn_ref.dtype)


def _mix_prompt_kernel(b_ref, c_ref, hc_ref, wc_ref, at_ref, gc_ref, ga_ref, x_ref, g_ref,
                       wco_ref, wao_ref, wo_ref, h_ref, hn_ref, st_ref, us_ref, *, tiles_per_seq):
    tm = x_ref.shape[0]

    @pl.when(pl.program_id(0) % tiles_per_seq == 0)
    def _():
        us_ref[0:CARRY_ROWS, :] = jnp.zeros((CARRY_ROWS, D_CONV), F32)

    u = c_ref[...].astype(F32) * hc_ref[...].astype(F32)
    u1, u2 = _shifted_u(u, us_ref)
    us_ref[0:CARRY_ROWS, :] = u[tm - CARRY_ROWS:, :]
    st_ref[...] = u[tm - (CONV_WIDTH - 1):, :]
    _mix_tail(b_ref, wc_ref, u, u1, u2, at_ref, gc_ref, ga_ref, x_ref, g_ref, wco_ref, wao_ref, wo_ref,
              h_ref, hn_ref)


def _mix_sample_kernel(b_ref, c_ref, hc_ref, p1_ref, p2_ref, wc_ref, at_ref, gc_ref, ga_ref, x_ref, g_ref,
                       wco_ref, wao_ref, wo_ref, h_ref, hn_ref, u_ref, us_ref, *, steps):
    us_ref[0:CARRY_ROWS, :] = jnp.zeros((CARRY_ROWS, D_CONV), F32)
    u = c_ref[...].astype(F32) * hc_ref[...].astype(F32)
    u_ref[...] = u
    u1, u2 = _shifted_u(u, us_ref)
    t = lax.broadcasted_iota(jnp.int32, u.shape, 0) % steps
    u1 = jnp.where(t >= 1, u1, p1_ref[...])
    u2 = jnp.where(t >= 2, u2, p2_ref[...])
    _mix_tail(b_ref, wc_ref, u, u1, u2, at_ref, gc_ref, ga_ref, x_ref, g_ref, wco_ref, wao_ref, wo_ref,
              h_ref, hn_ref)


def _mix_specs(tm):
    resident = functools.partial(pl.BlockSpec, index_map=lambda i: (0, 0), pipeline_mode=pl.Buffered(1))
    row_spec = pl.BlockSpec((tm, D_MODEL), lambda i: (i, 0))

    def zcol(off, width):
        return pl.BlockSpec((tm, width), lambda i: (i, off // width))

    conv_in = [zcol(OFF_B, D_CONV), zcol(OFF_C, D_CONV), zcol(OFF_H, D_CONV)]
    rest_in = [
        pl.BlockSpec((CONV_WIDTH, D_CONV), lambda i: (0, 0)),
        pl.BlockSpec((tm, D_ATTN), lambda i: (i, 0)),
        zcol(OFF_GC, D_MODEL),
        zcol(OFF_GA, D_MODEL),
        row_spec,
        pl.BlockSpec((1, D_MODEL), lambda i: (0, 0)),
        resident((D_CONV, D_MODEL)),
        resident((D_ATTN, D_MODEL)),
        resident((D_MODEL, D_MODEL)),
    ]
    return conv_in, rest_in, row_spec


def _mix_prompt(z, attn, x, w_conv, g_ffn, wco, wao, wo, tm, batch, seq):
    m = x.shape[0]
    assert seq % tm == 0 and tm >= CARRY_ROWS
    tiles_per_seq = seq // tm
    conv_in, rest_in, row_spec = _mix_specs(tm)
    return pl.pallas_call(
        functools.partial(_mix_prompt_kernel, tiles_per_seq=tiles_per_seq),
        grid=(m // tm,),
        in_specs=conv_in + rest_in,
        out_specs=[row_spec, row_spec,
                   pl.BlockSpec((None, CONV_WIDTH - 1, D_CONV), lambda i: (i // tiles_per_seq, 0, 0))],
        out_shape=[jax.ShapeDtypeStruct((m, D_MODEL), F32), jax.ShapeDtypeStruct((m, D_MODEL), BF16),
                   jax.ShapeDtypeStruct((batch, CONV_WIDTH - 1, D_CONV), F32)],
        scratch_shapes=[pltpu.VMEM((tm + CARRY_ROWS, D_CONV), F32)],
        compiler_params=_params("arbitrary"),
        name="mix_prompt",
    )(z, z, z, w_conv, attn, z, z, x, g_ffn, wco, wao, wo)


def _mix_sample(z, prev1, prev2, attn, x, w_conv, g_ffn, wco, wao, wo, tm, steps):
    m = x.shape[0]
    assert tm % steps == 0 and steps >= CONV_WIDTH - 1
    conv_in, rest_in, row_spec = _mix_specs(tm)
    state_spec = pl.BlockSpec((tm, D_CONV), lambda i: (i, 0))
    return pl.pallas_call(
        functools.partial(_mix_sample_kernel, steps=steps),
        grid=(m // tm,),
        in_specs=conv_in + [state_spec, state_spec] + rest_in,
        out_specs=[row_spec, row_spec, state_spec],
        out_shape=[jax.ShapeDtypeStruct((m, D_MODEL), F32), jax.ShapeDtypeStruct((m, D_MODEL), BF16),
                   jax.ShapeDtypeStruct((m, D_CONV), F32)],
        scratch_shapes=[pltpu.VMEM((tm + CARRY_ROWS, D_CONV), F32)],
        compiler_params=_params("arbitrary"),
        name="mix_sample",
    )(z, z, z, prev1, prev2, w_conv, attn, z, z, x, g_ffn, wco, wao, wo)


def _ffn_kernel(hn_ref, h_ref, gf_ref, wg_ref, wu_ref, wd_ref, y_ref, *, final_norm):
    j = pl.program_id(1)

    @pl.when(j == 0)
    def _():
        y_ref[...] = h_ref[...]

    hn = hn_ref[...]
    a = (jax.nn.silu(jnp.dot(hn, wg_ref[...].astype(BF16), preferred_element_type=F32))
         * jnp.dot(hn, wu_ref[...].astype(BF16), preferred_element_type=F32))
    y_ref[...] += jnp.dot(a.astype(BF16), wd_ref[...].astype(BF16), preferred_element_type=F32)

    if final_norm:
        @pl.when(j == pl.num_programs(1) - 1)
        def _():
            def body(r, carry):
                rows = pl.ds(pl.multiple_of(r * NORM_ROWS, NORM_ROWS), NORM_ROWS)
                y_ref[rows, :] = _rms_rows(y_ref[rows, :], gf_ref[...])
                return carry

            lax.fori_loop(0, y_ref.shape[0] // NORM_ROWS, body, 0)


def _ffn(hn, h, g_final, wg, wu, wd, tm, tf, final_norm):
    m = h.shape[0]
    return pl.pallas_call(
        functools.partial(_ffn_kernel, final_norm=final_norm),
        grid=(m // tm, D_FF // tf),
        in_specs=[
            pl.BlockSpec((tm, D_MODEL), lambda i, j: (i, 0)),
            pl.BlockSpec((tm, D_MODEL), lambda i, j: (i, 0), pipeline_mode=pl.Buffered(1)),
            pl.BlockSpec((1, D_MODEL), lambda i, j: (0, 0)),
            pl.BlockSpec((D_MODEL, tf), lambda i, j: (0, j)),
            pl.BlockSpec((D_MODEL, tf), lambda i, j: (0, j)),
            pl.BlockSpec((tf, D_MODEL), lambda i, j: (j, 0)),
        ],
        out_specs=pl.BlockSpec((tm, D_MODEL), lambda i, j: (i, 0)),
        out_shape=jax.ShapeDtypeStruct((m, D_MODEL), F32),
        compiler_params=_params("arbitrary", "arbitrary"),
        name="ffn",
    )(hn, h, g_final, wg, wu, wd)


IN_PROJ_TM_PROMPT = 2048
IN_PROJ_TM_SAMPLE = 1024
IN_PROJ_TN = 512
MIX_TM = 256
FFN_TM = 1024
FFN_TF = 256


def kernel(x_prompt, x_sample, cache_k, cache_v, state_conv, rel_bias, w_in, w_conv, w_conv_out, sinks,
           w_attn_out, w_o, g_mix, g_ffn, w_gate, w_up, w_down, g_final):
    depth = w_in.shape[0]
    batch, seq, _ = x_prompt.shape
    dec_batch, steps, _ = x_sample.shape
    wc = cache_k.shape[2]
    assert seq % WINDOW == 0 and wc == WINDOW and dec_batch % SAMPLE_BATCH_TILE == 0

    hp = x_prompt.reshape(batch * seq, D_MODEL)
    hs = x_sample.reshape(dec_batch * steps, D_MODEL)
    g_final2 = g_final.reshape(1, D_MODEL)
    kp_l, vp_l, cp_l, ks_l, vs_l, cs_l = [], [], [], [], [], []
    for l in range(depth):
        final = l == depth - 1
        lw = {
            "w_in": w_in[l],
            "wco": w_conv_out[l].astype(BF16),
            "wao": w_attn_out[l].astype(BF16),
            "wo": w_o[l].astype(BF16),
            "wg": w_gate[l],
            "wu": w_up[l],
            "wd": w_down[l],
            "g_ffn": g_ffn[l].reshape(1, D_MODEL),
        }
        g_mix_l = g_mix[l].reshape(1, D_MODEL)

        zp = _in_proj(hp, g_mix_l, lw["w_in"], IN_PROJ_TM_PROMPT, IN_PROJ_TN)
        attn_p = _attn_prompt(zp, rel_bias, sinks[l], batch, seq)
        hp, hnp, conv_p = _mix_prompt(zp, attn_p, hp, w_conv[l], lw["g_ffn"], lw["wco"], lw["wao"], lw["wo"],
                                      MIX_TM, batch, seq)
        hp = _ffn(hnp, hp, g_final2, lw["wg"], lw["wu"], lw["wd"], FFN_TM, FFN_TF, final)
        kv_p = zp.reshape(batch, seq, D_IN_PROJ)[:, seq - WINDOW:, OFF_K:OFF_V + D_KV].astype(F32)
        kp_l.append(kv_p[..., :D_KV].reshape(batch, WINDOW, N_KV_HEADS, HEAD_DIM))
        vp_l.append(kv_p[..., D_KV:].reshape(batch, WINDOW, N_KV_HEADS, HEAD_DIM))
        cp_l.append(conv_p)

        zs = _in_proj(hs, g_mix_l, lw["w_in"], IN_PROJ_TM_SAMPLE, IN_PROJ_TN)
        st = state_conv[l]
        zero = jnp.zeros((dec_batch, steps - 2, D_CONV), F32)
        prev1 = jnp.concatenate([st[:, 1:2], zero, zero[:, :1]], axis=1).reshape(dec_batch * steps, D_CONV)
        prev2 = jnp.concatenate([st, zero], axis=1).reshape(dec_batch * steps, D_CONV)
        attn_s, nk, nv = _attn_sample(zs, cache_k[l].reshape(dec_batch, wc, D_KV),
                                      cache_v[l].reshape(dec_batch, wc, D_KV), rel_bias, sinks[l],
                                      dec_batch, steps)
        hs, hns, u_s = _mix_sample(zs, prev1, prev2, attn_s, hs, w_conv[l], lw["g_ffn"], lw["wco"], lw["wao"],
                                   lw["wo"], MIX_TM, steps)
        hs = _ffn(hns, hs, g_final2, lw["wg"], lw["wu"], lw["wd"], FFN_TM, FFN_TF, final)
        ks_l.append(nk.reshape(dec_batch, wc, N_KV_HEADS, HEAD_DIM))
        vs_l.append(nv.reshape(dec_batch, wc, N_KV_HEADS, HEAD_DIM))
        cs_l.append(u_s.reshape(dec_batch, steps, D_CONV)[:, steps - (CONV_WIDTH - 1):])

    return (hp.reshape(batch, seq, D_MODEL), hs.reshape(dec_batch, steps, D_MODEL),
            jnp.stack(kp_l), jnp.stack(vp_l), jnp.stack(cp_l),
            jnp.stack(ks_l), jnp.stack(vs_l), jnp.stack(cs_l))
```

```python
import functools
import math

import numpy as np
import jax
import jax.numpy as jnp
from jax import lax
from jax.experimental import pallas as pl
from jax.experimental.pallas import tpu as pltpu

D_MODEL = 2048
D_CONV = D_MODEL // 2
CONV_WIDTH = 3
HEAD_DIM = 64
N_HEADS = (D_MODEL // 2) // HEAD_DIM
N_KV_HEADS = N_HEADS // 4
GROUP = N_HEADS // N_KV_HEADS
D_ATTN = N_HEADS * HEAD_DIM
D_KV = N_KV_HEADS * HEAD_DIM
WINDOW = 128
NUM_BUCKETS = 32
MAX_DISTANCE = 128
D_FF = -(-8 * D_MODEL // (3 * 256)) * 256
EPS = 1e-6
D_IN_PROJ = 3 * D_CONV + D_ATTN + 2 * D_KV + 2 * D_MODEL

OFF_GC = 0
OFF_GA = D_MODEL
OFF_B = 2 * D_MODEL
OFF_C = OFF_B + D_CONV
OFF_H = OFF_C + D_CONV
OFF_Q = OFF_H + D_CONV
OFF_K = OFF_Q + D_ATTN
OFF_V = OFF_K + D_KV

LANES = 128
NEG = -1e30
F32 = jnp.float32
BF16 = jnp.bfloat16
VMEM_LIMIT = 60 * 1024 * 1024


def _bucket_thresholds():
    max_exact = NUM_BUCKETS // 2
    d = np.arange(MAX_DISTANCE)
    ratio = np.log(np.maximum(d, 1).astype(np.float32) / np.float32(max_exact)) / np.float32(
        math.log(MAX_DISTANCE / max_exact))
    large = max_exact + (ratio * np.float32(NUM_BUCKETS - max_exact)).astype(np.int32)
    large = np.minimum(large, NUM_BUCKETS - 1)
    return [int(np.min(d[(d >= max_exact) & (large >= b)])) for b in range(max_exact + 1, NUM_BUCKETS)]


BUCKET_THRESHOLDS = _bucket_thresholds()


def _params(*semantics):
    return pltpu.CompilerParams(dimension_semantics=semantics, vmem_limit_bytes=VMEM_LIMIT)


def _rms_rows(x, g):
    ms = jnp.mean(x * x, axis=-1, keepdims=True)
    return x * lax.rsqrt(ms + EPS) * g


NORM_ROWS = 256


def _inproj_kernel(x_ref, g_ref, w_ref, z_ref, xn_ref):
    @pl.when(pl.program_id(1) == 0)
    def _():
        def body(r, carry):
            rows = pl.ds(pl.multiple_of(r * NORM_ROWS, NORM_ROWS), NORM_ROWS)
            xn_ref[rows, :] = _rms_rows(x_ref[rows, :], g_ref[...]).astype(xn_ref.dtype)
            return carry

        lax.fori_loop(0, x_ref.shape[0] // NORM_ROWS, body, 0)

    z_ref[...] = jnp.dot(xn_ref[...], w_ref[...].astype(BF16), preferred_element_type=F32).astype(z_ref.dtype)


def _in_proj(x, g, w, tm, tn):
    m = x.shape[0]
    n = w.shape[1]
    gate_start = n - 2 * D_MODEL
    assert gate_start % tn == 0 and n % tn == 0
    gate_block, nblocks = gate_start // tn, n // tn
    return pl.pallas_call(
        _inproj_kernel,
        grid=(m // tm, nblocks),
        in_specs=[
            pl.BlockSpec((tm, D_MODEL), lambda i, j: (i, 0)),
            pl.BlockSpec((1, D_MODEL), lambda i, j: (0, 0)),
            pl.BlockSpec((D_MODEL, tn), lambda i, j: (0, (j + gate_block) % nblocks)),
        ],
        out_specs=pl.BlockSpec((tm, tn), lambda i, j: (i, j)),
        out_shape=jax.ShapeDtypeStruct((m, n), BF16),
        scratch_shapes=[pltpu.VMEM((tm, D_MODEL), BF16)],
        compiler_params=_params("arbitrary", "arbitrary"),
        name="in_proj",
    )(x, g, w)


def _bias_from_distance(dist, head, relb_ref):
    valid = (dist >= 0) & (dist < WINDOW)
    d = jnp.clip(dist, 0, MAX_DISTANCE - 1)
    bucket = jnp.minimum(d, NUM_BUCKETS // 2)
    for thr in BUCKET_THRESHOLDS:
        bucket = bucket + (d >= thr).astype(jnp.int32)
    bias = jnp.zeros(dist.shape, F32)
    for b in range(NUM_BUCKETS):
        bias = jnp.where(bucket == b, relb_ref[b, head], bias)
    return jnp.where(valid, bias, NEG)


def _softmax_with_sink(s, sink):
    m = jnp.maximum(jnp.max(s, axis=-1, keepdims=True), sink)
    p = jnp.exp(s - m)
    denom = jnp.sum(p, axis=-1, keepdims=True) + jnp.exp(sink - m)
    return p * (1.0 / denom)


def _head_in_group_columns(q, h):
    half = D_KV // 2
    assert 2 * HEAD_DIM == half and GROUP * HEAD_DIM == D_KV
    g = h // GROUP
    src = q[:, (h // 2) * half:(h // 2 + 1) * half]
    if h % 2 != g % 2:
        src = pltpu.roll(src, HEAD_DIM, 1)
    lane_half = lax.broadcasted_iota(jnp.int32, src.shape, 1) // HEAD_DIM
    placed = jnp.where(lane_half == g % 2, src, 0.0)
    zero = jnp.zeros_like(placed)
    return jnp.concatenate([placed, zero] if g // 2 == 0 else [zero, placed], axis=1)


def _group_columns(h):
    g = h // GROUP
    return slice(g * HEAD_DIM, (g + 1) * HEAD_DIM)


PROMPT_BLOCKS_PER_STEP = 4


def _attn_prompt_kernel(relb_ref, sink_ref, q_ref, kp_ref, kc_ref, vp_ref, vc_ref, o_ref, bias_ref, lhs_ref):
    b = pl.program_id(0)
    n = pl.program_id(1)
    qi = lax.broadcasted_iota(jnp.int32, (WINDOW, WINDOW), 0)
    kj = lax.broadcasted_iota(jnp.int32, (WINDOW, WINDOW), 1)
    from_prev = kj > qi

    @pl.when((b == 0) & (n == 0))
    def _():
        dist = jnp.where(from_prev, qi + WINDOW - kj, qi - kj)
        for h in range(N_HEADS):
            rows = slice(h * WINDOW, (h + 1) * WINDOW)
            bias = _bias_from_distance(dist, h, relb_ref)
            bias_ref[1, rows, :] = bias
            bias_ref[0, rows, :] = jnp.where(from_prev, NEG, bias)

    k_all = jnp.concatenate([kp_ref[...], kc_ref[...]], axis=0)
    v_all = jnp.concatenate([vp_ref[...], vc_ref[...]], axis=0)
    for blk in range(PROMPT_BLOCKS_PER_STEP):
        rows_q = slice(blk * WINDOW, (blk + 1) * WINDOW)
        keys = slice(blk * WINDOW, (blk + 2) * WINDOW)
        has_prev = jnp.minimum(n, 1) if blk == 0 else 1
        o_ref[rows_q, :] = _attend_block(q_ref[rows_q, :], k_all[keys], v_all[keys], has_prev, from_prev,
                                         sink_ref, bias_ref, lhs_ref.at[blk]).astype(o_ref.dtype)


def _attend_block(q, kk, vv, has_prev, from_prev, sink_ref, bias_ref, lhs_ref):
    q = q.astype(F32) * (HEAD_DIM ** -0.5)
    for h in range(N_HEADS):
        lhs_ref[h * WINDOW:(h + 1) * WINDOW, :] = _head_in_group_columns(q, h).astype(BF16)
    s = lax.dot_general(lhs_ref[...], kk, (((1,), (1,)), ((), ())), preferred_element_type=F32)
    probs, sink_terms = [], []
    for h in range(N_HEADS):
        rows = slice(h * WINDOW, (h + 1) * WINDOW)
        sh = jnp.where(from_prev, s[rows, :WINDOW], s[rows, WINDOW:]) + bias_ref[has_prev, rows, :]
        m = jnp.maximum(jnp.max(sh, axis=-1, keepdims=True), sink_ref[h])
        p = jnp.exp(sh - m)
        probs.append(jnp.concatenate([jnp.where(from_prev, p, 0.0), jnp.where(from_prev, 0.0, p)],
                                     axis=1).astype(BF16))
        sink_terms.append(jnp.exp(sink_ref[h] - m))
    pm = jnp.concatenate(probs, axis=0)
    rowsum = jnp.dot(pm, jnp.ones((2 * WINDOW, LANES), BF16), preferred_element_type=F32)
    outs = []
    for g in range(N_KV_HEADS):
        og = jnp.dot(pm[g * GROUP * WINDOW:(g + 1) * GROUP * WINDOW], vv[:, g * HEAD_DIM:(g + 1) * HEAD_DIM],
                     preferred_element_type=F32)
        for r in range(GROUP):
            h = g * GROUP + r
            inv = 1.0 / (rowsum[h * WINDOW:(h + 1) * WINDOW] + sink_terms[h])
            outs.append(og[r * WINDOW:(r + 1) * WINDOW] * inv[:, :HEAD_DIM])
    return jnp.concatenate(outs, axis=1)


def _attn_prompt(z, rel_bias, sinks, batch, seq):
    bps = PROMPT_BLOCKS_PER_STEP
    assert seq % (bps * WINDOW) == 0
    ns = seq // (bps * WINDOW)

    def cur(off, width):
        return lambda b, n: (b * ns + n, off // width)

    def prev(off, width):
        return lambda b, n: ((b * ns + n) * bps - jnp.minimum(n, 1), off // width)

    smem = pl.BlockSpec(memory_space=pltpu.SMEM)
    return pl.pallas_call(
        _attn_prompt_kernel,
        grid=(batch, ns),
        in_specs=[
            smem,
            smem,
            pl.BlockSpec((bps * WINDOW, D_ATTN), cur(OFF_Q, D_ATTN)),
            pl.BlockSpec((WINDOW, D_KV), prev(OFF_K, D_KV)),
            pl.BlockSpec((bps * WINDOW, D_KV), cur(OFF_K, D_KV)),
            pl.BlockSpec((WINDOW, D_KV), prev(OFF_V, D_KV)),
            pl.BlockSpec((bps * WINDOW, D_KV), cur(OFF_V, D_KV)),
        ],
        out_specs=pl.BlockSpec((bps * WINDOW, D_ATTN), lambda b, n: (b * ns + n, 0)),
        out_shape=jax.ShapeDtypeStruct((batch * seq, D_ATTN), BF16),
        scratch_shapes=[pltpu.VMEM((2, N_HEADS * WINDOW, WINDOW), F32),
                        pltpu.VMEM((bps, N_HEADS * WINDOW, D_KV), BF16)],
        compiler_params=_params("arbitrary", "arbitrary"),
        name="attn_prompt",
    )(rel_bias, sinks, z, z, z, z, z)


SAMPLE_BATCH_TILE = 16


def _attn_sample_kernel(relb_ref, sink_ref, q_ref, kn_ref, vn_ref, ck_ref, cv_ref,
                        o_ref, nk_ref, nv_ref,
                        bias1_ref, bias2_ref, sinkv_ref, z_ref, s_ref, p_ref, oh_ref, *, steps):
    bt, wc, _ = ck_ref.shape
    tile_rows = bt * steps

    @pl.when(pl.program_id(0) == 0)
    def _():
        t = lax.broadcasted_iota(jnp.int32, (steps, wc), 0)
        j = lax.broadcasted_iota(jnp.int32, (steps, wc), 1)
        t2 = lax.broadcasted_iota(jnp.int32, (steps, tile_rows), 0)
        j2 = lax.broadcasted_iota(jnp.int32, (steps, tile_rows), 1)
        own = jnp.where(j2 < steps, t2 - j2, -1)
        for h in range(N_HEADS):
            sl = slice(h * steps, (h + 1) * steps)
            bias1_ref[sl, :] = _bias_from_distance(t + wc - j, h, relb_ref)
            bias2_ref[0, sl, :] = _bias_from_distance(own, h, relb_ref)
            sinkv_ref[sl, :] = jnp.full((steps, LANES), sink_ref[h], F32)
        for b in range(1, bt):
            bias2_ref[b] = pltpu.roll(bias2_ref[0], b * steps, 1)

    kn = kn_ref[...]
    vn = vn_ref[...]
    nk_ref[:, 0:wc - steps, :] = ck_ref[:, steps:, :]
    nk_ref[:, wc - steps:wc, :] = kn.astype(F32).reshape(bt, steps, D_KV)
    nv_ref[:, 0:wc - steps, :] = cv_ref[:, steps:, :]
    nv_ref[:, wc - steps:wc, :] = vn.astype(F32).reshape(bt, steps, D_KV)

    q = q_ref[...].astype(F32) * (HEAD_DIM ** -0.5)
    for h in range(N_HEADS):
        z_ref[:, h * steps:(h + 1) * steps, :] = _head_in_group_columns(q, h).reshape(bt, steps, D_KV)

    for b in range(bt):
        keys = jnp.concatenate([ck_ref[b].astype(BF16), kn], axis=0)
        s_ref[b] = lax.dot_general(z_ref[b].astype(BF16), keys, (((1,), (1,)), ((), ())),
                                   preferred_element_type=F32)

    bias = jnp.concatenate([jnp.broadcast_to(bias1_ref[...], (bt,) + bias1_ref.shape), bias2_ref[...]], axis=2)
    s = s_ref[...] + bias
    sink = sinkv_ref[...][None]
    m = jnp.maximum(jnp.max(s, axis=-1, keepdims=True), sink)
    p_ref[...] = jnp.exp(s - jnp.concatenate([m] * (s.shape[-1] // LANES), axis=-1)).astype(BF16)
    sink_term = jnp.exp(sink - m)

    vn_ones = jnp.concatenate([vn, jnp.ones((tile_rows, LANES), BF16)], axis=1)
    for b in range(bt):
        cv_ones = jnp.concatenate([cv_ref[b].astype(BF16), jnp.ones((wc, LANES), BF16)], axis=1)
        oh_ref[b] = jnp.dot(p_ref[b], jnp.concatenate([cv_ones, vn_ones], axis=0), preferred_element_type=F32)

    inv = 1.0 / (oh_ref[:, :, D_KV:] + sink_term)
    outs = []
    for h in range(N_HEADS):
        rows = slice(h * steps, (h + 1) * steps)
        outs.append((oh_ref[:, rows, _group_columns(h)] * inv[:, rows, :HEAD_DIM]).reshape(tile_rows, HEAD_DIM))
    o_ref[...] = jnp.concatenate(outs, axis=1).astype(o_ref.dtype)


def _attn_sample(z, cache_k, cache_v, rel_bias, sinks, batch, steps):
    wc = cache_k.shape[1]
    bt = SAMPLE_BATCH_TILE
    rows = bt * steps
    smem = pl.BlockSpec(memory_space=pltpu.SMEM)
    cache_spec = pl.BlockSpec((bt, wc, D_KV), lambda i: (i, 0, 0))
    return pl.pallas_call(
        functools.partial(_attn_sample_kernel, steps=steps),
        grid=(batch // bt,),
        in_specs=[
            smem,
            smem,
            pl.BlockSpec((rows, D_ATTN), lambda i: (i, OFF_Q // D_ATTN)),
            pl.BlockSpec((rows, D_KV), lambda i: (i, OFF_K // D_KV)),
            pl.BlockSpec((rows, D_KV), lambda i: (i, OFF_V // D_KV)),
            cache_spec,
            cache_spec,
        ],
        out_specs=[
            pl.BlockSpec((rows, D_ATTN), lambda i: (i, 0)),
            cache_spec,
            cache_spec,
        ],
        out_shape=[
            jax.ShapeDtypeStruct((batch * steps, D_ATTN), BF16),
            jax.ShapeDtypeStruct(cache_k.shape, F32),
            jax.ShapeDtypeStruct(cache_v.shape, F32),
        ],
        scratch_shapes=[
            pltpu.VMEM((N_HEADS * steps, wc), F32),
            pltpu.VMEM((bt, N_HEADS * steps, rows), F32),
            pltpu.VMEM((N_HEADS * steps, LANES), F32),
            pltpu.VMEM((bt, N_HEADS * steps, D_KV), F32),
            pltpu.VMEM((bt, N_HEADS * steps, wc + rows), F32),
            pltpu.VMEM((bt, N_HEADS * steps, wc + rows), BF16),
            pltpu.VMEM((bt, N_HEADS * steps, D_KV + LANES), F32),
        ],
        compiler_params=_params("arbitrary"),
        name="attn_sample",
    )(rel_bias, sinks, z, z, z, cache_k, cache_v)


CARRY_ROWS = 8


def _shifted_u(u, us_ref):
    tm = u.shape[0]
    us_ref[CARRY_ROWS:, :] = u
    return us_ref[CARRY_ROWS - 1:CARRY_ROWS - 1 + tm, :], us_ref[CARRY_ROWS - 2:CARRY_ROWS - 2 + tm, :]


def _mix_tail(b_ref, wc_ref, u, u1, u2, at_ref, gc_ref, ga_ref, x_ref, g_ref, wco_ref, wao_ref, wo_ref,
              h_ref, hn_ref):
    conv = wc_ref[0:1, :] * u2 + wc_ref[1:2, :] * u1 + wc_ref[2:3, :] * u
    yc = (b_ref[...].astype(F32) * conv).astype(BF16)
    y_conv = jnp.dot(yc, wco_ref[...], preferred_element_type=F32)
    y_attn = jnp.dot(at_ref[...], wao_ref[...], preferred_element_type=F32)
    merged = (jax.nn.sigmoid(gc_ref[...].astype(F32)) * y_conv
              + jax.nn.sigmoid(ga_ref[...].astype(F32)) * y_attn)
    h = x_ref[...] + jnp.dot(merged.astype(BF16), wo_ref[...], preferred_element_type=F32)
    h_ref[...] = h
    hn_ref[...] = _rms_rows(h, g_ref[...]).astype(hn_ref.dtype)


def _mix_prompt_kernel(b_ref, c_ref, hc_ref, wc_ref, at_ref, gc_ref, ga_ref, x_ref, g_ref,
                       wco_ref, wao_ref, wo_ref, h_ref, hn_ref, st_ref, us_ref, *, tiles_per_seq):
    tm = x_ref.shape[0]

    @pl.when(pl.program_id(0) % tiles_per_seq == 0)
    def _():
        us_ref[0:CARRY_ROWS, :] = jnp.zeros((CARRY_ROWS, D_CONV), F32)

    u = c_ref[...].astype(F32) * hc_ref[...].astype(F32)
    u1, u2 = _shifted_u(u, us_ref)
    us_ref[0:CARRY_ROWS, :] = u[tm - CARRY_ROWS:, :]
    st_ref[...] = u[tm - (CONV_WIDTH - 1):, :]
    _mix_tail(b_ref, wc_ref, u, u1, u2, at_ref, gc_ref, ga_ref, x_ref, g_ref, wco_ref, wao_ref, wo_ref,
              h_ref, hn_ref)


def _mix_sample_kernel(b_ref, c_ref, hc_ref, p1_ref, p2_ref, wc_ref, at_ref, gc_ref, ga_ref, x_ref, g_ref,
                       wco_ref, wao_ref, wo_ref, h_ref, hn_ref, u_ref, us_ref, *, steps):
    us_ref[0:CARRY_ROWS, :] = jnp.zeros((CARRY_ROWS, D_CONV), F32)
    u = c_ref[...].astype(F32) * hc_ref[...].astype(F32)
    u_ref[...] = u
    u1, u2 = _shifted_u(u, us_ref)
    t = lax.broadcasted_iota(jnp.int32, u.shape, 0) % steps
    u1 = jnp.where(t >= 1, u1, p1_ref[...])
    u2 = jnp.where(t >= 2, u2, p2_ref[...])
    _mix_tail(b_ref, wc_ref, u, u1, u2, at_ref, gc_ref, ga_ref, x_ref, g_ref, wco_ref, wao_ref, wo_ref,
              h_ref, hn_ref)


def _mix_specs(tm):
    resident = functools.partial(pl.BlockSpec, index_map=lambda i: (0, 0), pipeline_mode=pl.Buffered(1))
    row_spec = pl.BlockSpec((tm, D_MODEL), lambda i: (i, 0))

    def zcol(off, width):
        return pl.BlockSpec((tm, width), lambda i: (i, off // width))

    conv_in = [zcol(OFF_B, D_CONV), zcol(OFF_C, D_CONV), zcol(OFF_H, D_CONV)]
    rest_in = [
        pl.BlockSpec((CONV_WIDTH, D_CONV), lambda i: (0, 0)),
        pl.BlockSpec((tm, D_ATTN), lambda i: (i, 0)),
        zcol(OFF_GC, D_MODEL),
        zcol(OFF_GA, D_MODEL),
        row_spec,
        pl.BlockSpec((1, D_MODEL), lambda i: (0, 0)),
        resident((D_CONV, D_MODEL)),
        resident((D_ATTN, D_MODEL)),
        resident((D_MODEL, D_MODEL)),
    ]
    return conv_in, rest_in, row_spec


def _mix_prompt(z, attn, x, w_conv, g_ffn, wco, wao, wo, tm, batch, seq):
    m = x.shape[0]
    assert seq % tm == 0 and tm >= CARRY_ROWS
    tiles_per_seq = seq // tm
    conv_in, rest_in, row_spec = _mix_specs(tm)
    return pl.pallas_call(
        functools.partial(_mix_prompt_kernel, tiles_per_seq=tiles_per_seq),
        grid=(m // tm,),
        in_specs=conv_in + rest_in,
        out_specs=[row_spec, row_spec,
                   pl.BlockSpec((None, CONV_WIDTH - 1, D_CONV), lambda i: (i // tiles_per_seq, 0, 0))],
        out_shape=[jax.ShapeDtypeStruct((m, D_MODEL), F32), jax.ShapeDtypeStruct((m, D_MODEL), BF16),
                   jax.ShapeDtypeStruct((batch, CONV_WIDTH - 1, D_CONV), F32)],
        scratch_shapes=[pltpu.VMEM((tm + CARRY_ROWS, D_CONV), F32)],
        compiler_params=_params("arbitrary"),
        name="mix_prompt",
    )(z, z, z, w_conv, attn, z, z, x, g_ffn, wco, wao, wo)


def _mix_sample(z, prev1, prev2, attn, x, w_conv, g_ffn, wco, wao, wo, tm, steps):
    m = x.shape[0]
    assert tm % steps == 0 and steps >= CONV_WIDTH - 1
    conv_in, rest_in, row_spec = _mix_specs(tm)
    state_spec = pl.BlockSpec((tm, D_CONV), lambda i: (i, 0))
    return pl.pallas_call(
        functools.partial(_mix_sample_kernel, steps=steps),
        grid=(m // tm,),
        in_specs=conv_in + [state_spec, state_spec] + rest_in,
        out_specs=[row_spec, row_spec, state_spec],
        out_shape=[jax.ShapeDtypeStruct((m, D_MODEL), F32), jax.ShapeDtypeStruct((m, D_MODEL), BF16),
                   jax.ShapeDtypeStruct((m, D_CONV), F32)],
        scratch_shapes=[pltpu.VMEM((tm + CARRY_ROWS, D_CONV), F32)],
        compiler_params=_params("arbitrary"),
        name="mix_sample",
    )(z, z, z, prev1, prev2, w_conv, attn, z, z, x, g_ffn, wco, wao, wo)


def _ffn_kernel(hn_ref, h_hbm_ref, gf_ref, wg_ref, wu_ref, wd_ref, y_ref, h_sem, *, final_norm):
    i = pl.program_id(0)
    j = pl.program_id(1)
    tm = y_ref.shape[0]

    @pl.when(j == 0)
    def _():
        h_copy = pltpu.make_async_copy(h_hbm_ref.at[pl.ds(pl.multiple_of(i * tm, tm), tm), :], y_ref, h_sem)
        h_copy.start()
        h_copy.wait()

    hn = hn_ref[...]
    a = (jax.nn.silu(jnp.dot(hn, wg_ref[...].astype(BF16), preferred_element_type=F32))
         * jnp.dot(hn, wu_ref[...].astype(BF16), preferred_element_type=F32))
    y_ref[...] += jnp.dot(a.astype(BF16), wd_ref[...].astype(BF16), preferred_element_type=F32)

    if final_norm:
        @pl.when(j == pl.num_programs(1) - 1)
        def _():
            def body(r, carry):
                rows = pl.ds(pl.multiple_of(r * NORM_ROWS, NORM_ROWS), NORM_ROWS)
                y_ref[rows, :] = _rms_rows(y_ref[rows, :], gf_ref[...])
                return carry

            lax.fori_loop(0, y_ref.shape[0] // NORM_ROWS, body, 0)


def _ffn(hn, h, g_final, wg, wu, wd, tm, tf, final_norm):
    m = h.shape[0]
    return pl.pallas_call(
        functools.partial(_ffn_kernel, final_norm=final_norm),
        grid=(m // tm, D_FF // tf),
        in_specs=[
            pl.BlockSpec((tm, D_MODEL), lambda i, j: (i, 0)),
            pl.BlockSpec(memory_space=pl.ANY),
            pl.BlockSpec((1, D_MODEL), lambda i, j: (0, 0)),
            pl.BlockSpec((D_MODEL, tf), lambda i, j: (0, j)),
            pl.BlockSpec((D_MODEL, tf), lambda i, j: (0, j)),
            pl.BlockSpec((tf, D_MODEL), lambda i, j: (j, 0)),
        ],
        out_specs=pl.BlockSpec((tm, D_MODEL), lambda i, j: (i, 0)),
        out_shape=jax.ShapeDtypeStruct((m, D_MODEL), F32),
        scratch_shapes=[pltpu.SemaphoreType.DMA(())],
        compiler_params=_params("arbitrary", "arbitrary"),
        name="ffn",
    )(hn, h, g_final, wg, wu, wd)


IN_PROJ_TM_PROMPT = 2048
IN_PROJ_TM_SAMPLE = 1024
IN_PROJ_TN = 512
MIX_TM = 256
FFN_TM = 1024
FFN_TF = 512


def kernel(x_prompt, x_sample, cache_k, cache_v, state_conv, rel_bias, w_in, w_conv, w_conv_out, sinks,
           w_attn_out, w_o, g_mix, g_ffn, w_gate, w_up, w_down, g_final):
    depth = w_in.shape[0]
    batch, seq, _ = x_prompt.shape
    dec_batch, steps, _ = x_sample.shape
    wc = cache_k.shape[2]
    assert seq % WINDOW == 0 and wc == WINDOW and dec_batch % SAMPLE_BATCH_TILE == 0

    hp = x_prompt.reshape(batch * seq, D_MODEL)
    hs = x_sample.reshape(dec_batch * steps, D_MODEL)
    g_final2 = g_final.reshape(1, D_MODEL)
    kp_l, vp_l, cp_l, ks_l, vs_l, cs_l = [], [], [], [], [], []
    for l in range(depth):
        final = l == depth - 1
        lw = {
            "w_in": w_in[l],
            "wco": w_conv_out[l].astype(BF16),
            "wao": w_attn_out[l].astype(BF16),
            "wo": w_o[l].astype(BF16),
            "wg": w_gate[l],
            "wu": w_up[l],
            "wd": w_down[l],
            "g_ffn": g_ffn[l].reshape(1, D_MODEL),
        }
        g_mix_l = g_mix[l].reshape(1, D_MODEL)

        zp = _in_proj(hp, g_mix_l, lw["w_in"], IN_PROJ_TM_PROMPT, IN_PROJ_TN)
        attn_p = _attn_prompt(zp, rel_bias, sinks[l], batch, seq)
        hp, hnp, conv_p = _mix_prompt(zp, attn_p, hp, w_conv[l], lw["g_ffn"], lw["wco"], lw["wao"], lw["wo"],
                                      MIX_TM, batch, seq)
        hp = _ffn(hnp, hp, g_final2, lw["wg"], lw["wu"], lw["wd"], FFN_TM, FFN_TF, final)
        kv_p = zp.reshape(batch, seq, D_IN_PROJ)[:, seq - WINDOW:, OFF_K:OFF_V + D_KV].astype(F32)
        kp_l.append(kv_p[..., :D_KV].reshape(batch, WINDOW, N_KV_HEADS, HEAD_DIM))
        vp_l.append(kv_p[..., D_KV:].reshape(batch, WINDOW, N_KV_HEADS, HEAD_DIM))
        cp_l.append(conv_p)

        zs = _in_proj(hs, g_mix_l, lw["w_in"], IN_PROJ_TM_SAMPLE, IN_PROJ_TN)
        st = state_conv[l]
        zero = jnp.zeros((dec_batch, steps - 2, D_CONV), F32)
        prev1 = jnp.concatenate([st[:, 1:2], zero, zero[:, :1]], axis=1).reshape(dec_batch * steps, D_CONV)
        prev2 = jnp.concatenate([st, zero], axis=1).reshape(dec_batch * steps, D_CONV)
        attn_s, nk, nv = _attn_sample(zs, cache_k[l].reshape(dec_batch, wc, D_KV),
                                      cache_v[l].reshape(dec_batch, wc, D_KV), rel_bias, sinks[l],
                                      dec_batch, steps)
        hs, hns, u_s = _mix_sample(zs, prev1, prev2, attn_s, hs, w_conv[l], lw["g_ffn"], lw["wco"], lw["wao"],
                                   lw["wo"], MIX_TM, steps)
        hs = _ffn(hns, hs, g_final2, lw["wg"], lw["wu"], lw["wd"], FFN_TM, FFN_TF, final)
        ks_l.append(nk.reshape(dec_batch, wc, N_KV_HEADS, HEAD_DIM))
        vs_l.append(nv.reshape(dec_batch, wc, N_KV_HEADS, HEAD_DIM))
        cs_l.append(u_s.reshape(dec_batch, steps, D_CONV)[:, steps - (CONV_WIDTH - 1):])

    return (hp.reshape(batch, seq, D_MODEL), hs.reshape(dec_batch, steps, D_MODEL),
            jnp.stack(kp_l), jnp.stack(vp_l), jnp.stack(cp_l),
            jnp.stack(ks_l), jnp.stack(vs_l), jnp.stack(cs_l))
```

```python
import functools
import math

import numpy as np
import jax
import jax.numpy as jnp
from jax import lax
from jax.experimental import pallas as pl
from jax.experimental.pallas import tpu as pltpu

D_MODEL = 2048
D_CONV = D_MODEL // 2
CONV_WIDTH = 3
HEAD_DIM = 64
N_HEADS = (D_MODEL // 2) // HEAD_DIM
N_KV_HEADS = N_HEADS // 4
GROUP = N_HEADS // N_KV_HEADS
D_ATTN = N_HEADS * HEAD_DIM
D_KV = N_KV_HEADS * HEAD_DIM
WINDOW = 128
NUM_BUCKETS = 32
MAX_DISTANCE = 128
D_FF = -(-8 * D_MODEL // (3 * 256)) * 256
EPS = 1e-6
D_IN_PROJ = 3 * D_CONV + D_ATTN + 2 * D_KV + 2 * D_MODEL

OFF_GC = 0
OFF_GA = D_MODEL
OFF_B = 2 * D_MODEL
OFF_C = OFF_B + D_CONV
OFF_H = OFF_C + D_CONV
OFF_Q = OFF_H + D_CONV
OFF_K = OFF_Q + D_ATTN
OFF_V = OFF_K + D_KV

LANES = 128
NEG = -1e30
F32 = jnp.float32
BF16 = jnp.bfloat16
VMEM_LIMIT = 60 * 1024 * 1024


def _bucket_thresholds():
    max_exact = NUM_BUCKETS // 2
    d = np.arange(MAX_DISTANCE)
    ratio = np.log(np.maximum(d, 1).astype(np.float32) / np.float32(max_exact)) / np.float32(
        math.log(MAX_DISTANCE / max_exact))
    large = max_exact + (ratio * np.float32(NUM_BUCKETS - max_exact)).astype(np.int32)
    large = np.minimum(large, NUM_BUCKETS - 1)
    return [int(np.min(d[(d >= max_exact) & (large >= b)])) for b in range(max_exact + 1, NUM_BUCKETS)]


BUCKET_THRESHOLDS = _bucket_thresholds()


def _params(*semantics):
    return pltpu.CompilerParams(dimension_semantics=semantics, vmem_limit_bytes=VMEM_LIMIT)


def _rms_rows(x, g):
    ms = jnp.mean(x * x, axis=-1, keepdims=True)
    return x * lax.rsqrt(ms + EPS) * g


NORM_ROWS = 256


def _inproj_kernel(x_ref, g_ref, w_ref, z_ref, xn_ref):
    @pl.when(pl.program_id(1) == 0)
    def _():
        def body(r, carry):
            rows = pl.ds(pl.multiple_of(r * NORM_ROWS, NORM_ROWS), NORM_ROWS)
            xn_ref[rows, :] = _rms_rows(x_ref[rows, :], g_ref[...]).astype(xn_ref.dtype)
            return carry

        lax.fori_loop(0, x_ref.shape[0] // NORM_ROWS, body, 0)

    z_ref[...] = jnp.dot(xn_ref[...], w_ref[...].astype(BF16), preferred_element_type=F32).astype(z_ref.dtype)


def _in_proj(x, g, w, tm, tn):
    m = x.shape[0]
    n = w.shape[1]
    gate_start = n - 2 * D_MODEL
    assert gate_start % tn == 0 and n % tn == 0
    gate_block, nblocks = gate_start // tn, n // tn
    return pl.pallas_call(
        _inproj_kernel,
        grid=(m // tm, nblocks),
        in_specs=[
            pl.BlockSpec((tm, D_MODEL), lambda i, j: (i, 0)),
            pl.BlockSpec((1, D_MODEL), lambda i, j: (0, 0)),
            pl.BlockSpec((D_MODEL, tn), lambda i, j: (0, (j + gate_block) % nblocks)),
        ],
        out_specs=pl.BlockSpec((tm, tn), lambda i, j: (i, j)),
        out_shape=jax.ShapeDtypeStruct((m, n), BF16),
        scratch_shapes=[pltpu.VMEM((tm, D_MODEL), BF16)],
        compiler_params=_params("arbitrary", "arbitrary"),
        name="in_proj",
    )(x, g, w)


def _bias_from_distance(dist, head, relb_ref):
    valid = (dist >= 0) & (dist < WINDOW)
    d = jnp.clip(dist, 0, MAX_DISTANCE - 1)
    bucket = jnp.minimum(d, NUM_BUCKETS // 2)
    for thr in BUCKET_THRESHOLDS:
        bucket = bucket + (d >= thr).astype(jnp.int32)
    bias = jnp.zeros(dist.shape, F32)
    for b in range(NUM_BUCKETS):
        bias = jnp.where(bucket == b, relb_ref[b, head], bias)
    return jnp.where(valid, bias, NEG)


def _softmax_with_sink(s, sink):
    m = jnp.maximum(jnp.max(s, axis=-1, keepdims=True), sink)
    p = jnp.exp(s - m)
    denom = jnp.sum(p, axis=-1, keepdims=True) + jnp.exp(sink - m)
    return p * (1.0 / denom)


def _head_in_group_columns(q, h):
    half = D_KV // 2
    assert 2 * HEAD_DIM == half and GROUP * HEAD_DIM == D_KV
    g = h // GROUP
    src = q[:, (h // 2) * half:(h // 2 + 1) * half]
    if h % 2 != g % 2:
        src = pltpu.roll(src, HEAD_DIM, 1)
    lane_half = lax.broadcasted_iota(jnp.int32, src.shape, 1) // HEAD_DIM
    placed = jnp.where(lane_half == g % 2, src, 0.0)
    zero = jnp.zeros_like(placed)
    return jnp.concatenate([placed, zero] if g // 2 == 0 else [zero, placed], axis=1)


def _group_columns(h):
    g = h // GROUP
    return slice(g * HEAD_DIM, (g + 1) * HEAD_DIM)


PROMPT_BLOCKS_PER_STEP = 4


def _attn_prompt_kernel(relb_ref, sink_ref, q_ref, kp_ref, kc_ref, vp_ref, vc_ref, o_ref, bias_ref, lhs_ref):
    b = pl.program_id(0)
    n = pl.program_id(1)
    qi = lax.broadcasted_iota(jnp.int32, (WINDOW, WINDOW), 0)
    kj = lax.broadcasted_iota(jnp.int32, (WINDOW, WINDOW), 1)
    from_prev = kj > qi

    @pl.when((b == 0) & (n == 0))
    def _():
        dist = jnp.where(from_prev, qi + WINDOW - kj, qi - kj)
        for h in range(N_HEADS):
            rows = slice(h * WINDOW, (h + 1) * WINDOW)
            bias = _bias_from_distance(dist, h, relb_ref)
            bias_ref[1, rows, :] = bias
            bias_ref[0, rows, :] = jnp.where(from_prev, NEG, bias)

    k_all = jnp.concatenate([kp_ref[...], kc_ref[...]], axis=0)
    v_all = jnp.concatenate([vp_ref[...], vc_ref[...]], axis=0)

    def keys_of(blk):
        return slice(blk * WINDOW, (blk + 2) * WINDOW)

    nblk = PROMPT_BLOCKS_PER_STEP
    scores, probs = {}, {}
    for stage in range(nblk + 2):
        if stage < nblk:
            q = q_ref[stage * WINDOW:(stage + 1) * WINDOW, :]
            scores[stage] = _block_scores(q, k_all[keys_of(stage)], lhs_ref.at[stage])
        if 0 <= stage - 1 < nblk:
            blk = stage - 1
            has_prev = jnp.minimum(n, 1) if blk == 0 else 1
            probs[blk] = _block_softmax(scores.pop(blk), has_prev, from_prev, sink_ref, bias_ref)
        if 0 <= stage - 2 < nblk:
            blk = stage - 2
            o_ref[blk * WINDOW:(blk + 1) * WINDOW, :] = _block_output(
                *probs.pop(blk), v_all[keys_of(blk)]).astype(o_ref.dtype)


def _block_scores(q, kk, lhs_ref):
    q = q.astype(F32) * (HEAD_DIM ** -0.5)
    for h in range(N_HEADS):
        lhs_ref[h * WINDOW:(h + 1) * WINDOW, :] = _head_in_group_columns(q, h).astype(BF16)
    return lax.dot_general(lhs_ref[...], kk, (((1,), (1,)), ((), ())), preferred_element_type=F32)


def _block_softmax(s, has_prev, from_prev, sink_ref, bias_ref):
    probs, sink_terms = [], []
    for h in range(N_HEADS):
        rows = slice(h * WINDOW, (h + 1) * WINDOW)
        sh = jnp.where(from_prev, s[rows, :WINDOW], s[rows, WINDOW:]) + bias_ref[has_prev, rows, :]
        m = jnp.maximum(jnp.max(sh, axis=-1, keepdims=True), sink_ref[h])
        p = jnp.exp(sh - m)
        probs.append(jnp.concatenate([jnp.where(from_prev, p, 0.0), jnp.where(from_prev, 0.0, p)],
                                     axis=1).astype(BF16))
        sink_terms.append(jnp.exp(sink_ref[h] - m))
    return probs, sink_terms


def _block_output(probs, sink_terms, vv):
    assert 2 * HEAD_DIM == LANES and GROUP % 2 == 0
    ones = jnp.ones((vv.shape[0], HEAD_DIM), BF16)
    low_half = lax.broadcasted_iota(jnp.int32, (WINDOW, LANES), 1) < HEAD_DIM
    outs = []
    for g in range(N_KV_HEADS):
        vg = vv[:, g * HEAD_DIM:(g + 1) * HEAD_DIM]
        even_heads = [g * GROUP + r for r in range(0, GROUP, 2)]
        pe = jnp.concatenate([probs[h] for h in even_heads], axis=0)
        po = jnp.concatenate([probs[h + 1] for h in even_heads], axis=0)
        oe = jnp.dot(pe, jnp.concatenate([vg, ones], axis=1), preferred_element_type=F32)
        oo = jnp.dot(po, jnp.concatenate([ones, vg], axis=1), preferred_element_type=F32)
        for k, h in enumerate(even_heads):
            a = oe[k * WINDOW:(k + 1) * WINDOW]
            b = oo[k * WINDOW:(k + 1) * WINDOW]
            out_pair = jnp.where(low_half, a, b)
            sum_pair = pltpu.roll(jnp.where(low_half, b, a), HEAD_DIM, 1)
            sink_pair = jnp.where(low_half, sink_terms[h], sink_terms[h + 1])
            outs.append(out_pair * (1.0 / (sum_pair + sink_pair)))
    return jnp.concatenate(outs, axis=1)


def _attn_prompt(z, rel_bias, sinks, batch, seq):
    bps = PROMPT_BLOCKS_PER_STEP
    assert seq % (bps * WINDOW) == 0
    ns = seq // (bps * WINDOW)

    def cur(off, width):
        return lambda b, n: (b * ns + n, off // width)

    def prev(off, width):
        return lambda b, n: ((b * ns + n) * bps - jnp.minimum(n, 1), off // width)

    smem = pl.BlockSpec(memory_space=pltpu.SMEM)
    return pl.pallas_call(
        _attn_prompt_kernel,
        grid=(batch, ns),
        in_specs=[
            smem,
            smem,
            pl.BlockSpec((bps * WINDOW, D_ATTN), cur(OFF_Q, D_ATTN)),
            pl.BlockSpec((WINDOW, D_KV), prev(OFF_K, D_KV)),
            pl.BlockSpec((bps * WINDOW, D_KV), cur(OFF_K, D_KV)),
            pl.BlockSpec((WINDOW, D_KV), prev(OFF_V, D_KV)),
            pl.BlockSpec((bps * WINDOW, D_KV), cur(OFF_V, D_KV)),
        ],
        out_specs=pl.BlockSpec((bps * WINDOW, D_ATTN), lambda b, n: (b * ns + n, 0)),
        out_shape=jax.ShapeDtypeStruct((batch * seq, D_ATTN), BF16),
        scratch_shapes=[pltpu.VMEM((2, N_HEADS * WINDOW, WINDOW), F32),
                        pltpu.VMEM((bps, N_HEADS * WINDOW, D_KV), BF16)],
        compiler_params=_params("arbitrary", "arbitrary"),
        name="attn_prompt",
    )(rel_bias, sinks, z, z, z, z, z)


SAMPLE_BATCH_TILE = 16


def _attn_sample_kernel(relb_ref, sink_ref, q_ref, kn_ref, vn_ref, ck_ref, cv_ref,
                        o_ref, nk_ref, nv_ref,
                        bias1_ref, bias2_ref, sinkv_ref, z_ref, s_ref, p_ref, oh_ref, *, steps):
    bt, wc, _ = ck_ref.shape
    tile_rows = bt * steps

    @pl.when(pl.program_id(0) == 0)
    def _():
        t = lax.broadcasted_iota(jnp.int32, (steps, wc), 0)
        j = lax.broadcasted_iota(jnp.int32, (steps, wc), 1)
        t2 = lax.broadcasted_iota(jnp.int32, (steps, tile_rows), 0)
        j2 = lax.broadcasted_iota(jnp.int32, (steps, tile_rows), 1)
        own = jnp.where(j2 < steps, t2 - j2, -1)
        for h in range(N_HEADS):
            sl = slice(h * steps, (h + 1) * steps)
            bias1_ref[sl, :] = _bias_from_distance(t + wc - j, h, relb_ref)
            bias2_ref[0, sl, :] = _bias_from_distance(own, h, relb_ref)
            sinkv_ref[sl, :] = jnp.full((steps, LANES), sink_ref[h], F32)
        for b in range(1, bt):
            bias2_ref[b] = pltpu.roll(bias2_ref[0], b * steps, 1)

    kn = kn_ref[...]
    vn = vn_ref[...]
    nk_ref[:, 0:wc - steps, :] = ck_ref[:, steps:, :]
    nk_ref[:, wc - steps:wc, :] = kn.astype(F32).reshape(bt, steps, D_KV)
    nv_ref[:, 0:wc - steps, :] = cv_ref[:, steps:, :]
    nv_ref[:, wc - steps:wc, :] = vn.astype(F32).reshape(bt, steps, D_KV)

    q = q_ref[...].astype(F32) * (HEAD_DIM ** -0.5)
    for h in range(N_HEADS):
        z_ref[:, h * steps:(h + 1) * steps, :] = _head_in_group_columns(q, h).reshape(bt, steps, D_KV)

    for b in range(bt):
        keys = jnp.concatenate([ck_ref[b].astype(BF16), kn], axis=0)
        s_ref[b] = lax.dot_general(z_ref[b].astype(BF16), keys, (((1,), (1,)), ((), ())),
                                   preferred_element_type=F32)

    bias = jnp.concatenate([jnp.broadcast_to(bias1_ref[...], (bt,) + bias1_ref.shape), bias2_ref[...]], axis=2)
    s = s_ref[...] + bias
    sink = sinkv_ref[...][None]
    m = jnp.maximum(jnp.max(s, axis=-1, keepdims=True), sink)
    p_ref[...] = jnp.exp(s - jnp.concatenate([m] * (s.shape[-1] // LANES), axis=-1)).astype(BF16)
    sink_term = jnp.exp(sink - m)

    vn_ones = jnp.concatenate([vn, jnp.ones((tile_rows, LANES), BF16)], axis=1)
    for b in range(bt):
        cv_ones = jnp.concatenate([cv_ref[b].astype(BF16), jnp.ones((wc, LANES), BF16)], axis=1)
        oh_ref[b] = jnp.dot(p_ref[b], jnp.concatenate([cv_ones, vn_ones], axis=0), preferred_element_type=F32)

    inv = 1.0 / (oh_ref[:, :, D_KV:] + sink_term)
    outs = []
    for h in range(N_HEADS):
        rows = slice(h * steps, (h + 1) * steps)
        outs.append((oh_ref[:, rows, _group_columns(h)] * inv[:, rows, :HEAD_DIM]).reshape(tile_rows, HEAD_DIM))
    o_ref[...] = jnp.concatenate(outs, axis=1).astype(o_ref.dtype)


def _attn_sample(z, cache_k, cache_v, rel_bias, sinks, batch, steps):
    wc = cache_k.shape[1]
    bt = SAMPLE_BATCH_TILE
    rows = bt * steps
    smem = pl.BlockSpec(memory_space=pltpu.SMEM)
    cache_spec = pl.BlockSpec((bt, wc, D_KV), lambda i: (i, 0, 0))
    return pl.pallas_call(
        functools.partial(_attn_sample_kernel, steps=steps),
        grid=(batch // bt,),
        in_specs=[
            smem,
            smem,
            pl.BlockSpec((rows, D_ATTN), lambda i: (i, OFF_Q // D_ATTN)),
            pl.BlockSpec((rows, D_KV), lambda i: (i, OFF_K // D_KV)),
            pl.BlockSpec((rows, D_KV), lambda i: (i, OFF_V // D_KV)),
            cache_spec,
            cache_spec,
        ],
        out_specs=[
            pl.BlockSpec((rows, D_ATTN), lambda i: (i, 0)),
            cache_spec,
            cache_spec,
        ],
        out_shape=[
            jax.ShapeDtypeStruct((batch * steps, D_ATTN), BF16),
            jax.ShapeDtypeStruct(cache_k.shape, F32),
            jax.ShapeDtypeStruct(cache_v.shape, F32),
        ],
        scratch_shapes=[
            pltpu.VMEM((N_HEADS * steps, wc), F32),
            pltpu.VMEM((bt, N_HEADS * steps, rows), F32),
            pltpu.VMEM((N_HEADS * steps, LANES), F32),
            pltpu.VMEM((bt, N_HEADS * steps, D_KV), F32),
            pltpu.VMEM((bt, N_HEADS * steps, wc + rows), F32),
            pltpu.VMEM((bt, N_HEADS * steps, wc + rows), BF16),
            pltpu.VMEM((bt, N_HEADS * steps, D_KV + LANES), F32),
        ],
        compiler_params=_params("arbitrary"),
        name="attn_sample",
    )(rel_bias, sinks, z, z, z, cache_k, cache_v)


CARRY_ROWS = 8


def _shifted_u(u, us_ref):
    tm = u.shape[0]
    us_ref[CARRY_ROWS:, :] = u
    return us_ref[CARRY_ROWS - 1:CARRY_ROWS - 1 + tm, :], us_ref[CARRY_ROWS - 2:CARRY_ROWS - 2 + tm, :]


def _mix_tail(b_ref, wc_ref, u, u1, u2, at_ref, gc_ref, ga_ref, x_ref, g_ref, wco_ref, wao_ref, wo_ref,
              h_ref, hn_ref):
    conv = wc_ref[0:1, :] * u2 + wc_ref[1:2, :] * u1 + wc_ref[2:3, :] * u
    yc = (b_ref[...].astype(F32) * conv).astype(BF16)
    y_conv = jnp.dot(yc, wco_ref[...], preferred_element_type=F32)
    y_attn = jnp.dot(at_ref[...], wao_ref[...], preferred_element_type=F32)
    merged = (jax.nn.sigmoid(gc_ref[...].astype(F32)) * y_conv
              + jax.nn.sigmoid(ga_ref[...].astype(F32)) * y_attn)
    h = x_ref[...] + jnp.dot(merged.astype(BF16), wo_ref[...], preferred_element_type=F32)
    h_ref[...] = h
    hn_ref[...] = _rms_rows(h, g_ref[...]).astype(hn_ref.dtype)


def _mix_prompt_kernel(b_ref, c_ref, hc_ref, wc_ref, at_ref, gc_ref, ga_ref, x_ref, g_ref,
                       wco_ref, wao_ref, wo_ref, h_ref, hn_ref, st_ref, us_ref, *, tiles_per_seq):
    tm = x_ref.shape[0]

    @pl.when(pl.program_id(0) % tiles_per_seq == 0)
    def _():
        us_ref[0:CARRY_ROWS, :] = jnp.zeros((CARRY_ROWS, D_CONV), F32)

    u = c_ref[...].astype(F32) * hc_ref[...].astype(F32)
    u1, u2 = _shifted_u(u, us_ref)
    us_ref[0:CARRY_ROWS, :] = u[tm - CARRY_ROWS:, :]
    st_ref[...] = u[tm - (CONV_WIDTH - 1):, :]
    _mix_tail(b_ref, wc_ref, u, u1, u2, at_ref, gc_ref, ga_ref, x_ref, g_ref, wco_ref, wao_ref, wo_ref,
              h_ref, hn_ref)


def _mix_sample_kernel(b_ref, c_ref, hc_ref, p1_ref, p2_ref, wc_ref, at_ref, gc_ref, ga_ref, x_ref, g_ref,
                       wco_ref, wao_ref, wo_ref, h_ref, hn_ref, u_ref, us_ref, *, steps):
    us_ref[0:CARRY_ROWS, :] = jnp.zeros((CARRY_ROWS, D_CONV), F32)
    u = c_ref[...].astype(F32) * hc_ref[...].astype(F32)
    u_ref[...] = u
    u1, u2 = _shifted_u(u, us_ref)
    t = lax.broadcasted_iota(jnp.int32, u.shape, 0) % steps
    u1 = jnp.where(t >= 1, u1, p1_ref[...])
    u2 = jnp.where(t >= 2, u2, p2_ref[...])
    _mix_tail(b_ref, wc_ref, u, u1, u2, at_ref, gc_ref, ga_ref, x_ref, g_ref, wco_ref, wao_ref, wo_ref,
              h_ref, hn_ref)


def _mix_specs(tm):
    resident = functools.partial(pl.BlockSpec, index_map=lambda i: (0, 0), pipeline_mode=pl.Buffered(1))
    row_spec = pl.BlockSpec((tm, D_MODEL), lambda i: (i, 0))

    def zcol(off, width):
        return pl.BlockSpec((tm, width), lambda i: (i, off // width))

    conv_in = [zcol(OFF_B, D_CONV), zcol(OFF_C, D_CONV), zcol(OFF_H, D_CONV)]
    rest_in = [
        pl.BlockSpec((CONV_WIDTH, D_CONV), lambda i: (0, 0)),
        pl.BlockSpec((tm, D_ATTN), lambda i: (i, 0)),
        zcol(OFF_GC, D_MODEL),
        zcol(OFF_GA, D_MODEL),
        row_spec,
        pl.BlockSpec((1, D_MODEL), lambda i: (0, 0)),
        resident((D_CONV, D_MODEL)),
        resident((D_ATTN, D_MODEL)),
        resident((D_MODEL, D_MODEL)),
    ]
    return conv_in, rest_in, row_spec


def _mix_prompt(z, attn, x, w_conv, g_ffn, wco, wao, wo, tm, batch, seq):
    m = x.shape[0]
    assert seq % tm == 0 and tm >= CARRY_ROWS
    tiles_per_seq = seq // tm
    conv_in, rest_in, row_spec = _mix_specs(tm)
    return pl.pallas_call(
        functools.partial(_mix_prompt_kernel, tiles_per_seq=tiles_per_seq),
        grid=(m // tm,),
        in_specs=conv_in + rest_in,
        out_specs=[row_spec, row_spec,
                   pl.BlockSpec((None, CONV_WIDTH - 1, D_CONV), lambda i: (i // tiles_per_seq, 0, 0))],
        out_shape=[jax.ShapeDtypeStruct((m, D_MODEL), F32), jax.ShapeDtypeStruct((m, D_MODEL), BF16),
                   jax.ShapeDtypeStruct((batch, CONV_WIDTH - 1, D_CONV), F32)],
        scratch_shapes=[pltpu.VMEM((tm + CARRY_ROWS, D_CONV), F32)],
        compiler_params=_params("arbitrary"),
        name="mix_prompt",
    )(z, z, z, w_conv, attn, z, z, x, g_ffn, wco, wao, wo)


def _mix_sample(z, prev1, prev2, attn, x, w_conv, g_ffn, wco, wao, wo, tm, steps):
    m = x.shape[0]
    assert tm % steps == 0 and steps >= CONV_WIDTH - 1
    conv_in, rest_in, row_spec = _mix_specs(tm)
    state_spec = pl.BlockSpec((tm, D_CONV), lambda i: (i, 0))
    return pl.pallas_call(
        functools.partial(_mix_sample_kernel, steps=steps),
        grid=(m // tm,),
        in_specs=conv_in + [state_spec, state_spec] + rest_in,
        out_specs=[row_spec, row_spec, state_spec],
        out_shape=[jax.ShapeDtypeStruct((m, D_MODEL), F32), jax.ShapeDtypeStruct((m, D_MODEL), BF16),
                   jax.ShapeDtypeStruct((m, D_CONV), F32)],
        scratch_shapes=[pltpu.VMEM((tm + CARRY_ROWS, D_CONV), F32)],
        compiler_params=_params("arbitrary"),
        name="mix_sample",
    )(z, z, z, prev1, prev2, w_conv, attn, z, z, x, g_ffn, wco, wao, wo)


def _ffn_kernel(hn_ref, h_ref, gf_ref, wg_ref, wu_ref, wd_ref, y_ref, *, final_norm):
    j = pl.program_id(1)

    @pl.when(j == 0)
    def _():
        y_ref[...] = h_ref[...]

    hn = hn_ref[...]
    a = (jax.nn.silu(jnp.dot(hn, wg_ref[...].astype(BF16), preferred_element_type=F32))
         * jnp.dot(hn, wu_ref[...].astype(BF16), preferred_element_type=F32))
    y_ref[...] += jnp.dot(a.astype(BF16), wd_ref[...].astype(BF16), preferred_element_type=F32)

    if final_norm:
        @pl.when(j == pl.num_programs(1) - 1)
        def _():
            def body(r, carry):
                rows = pl.ds(pl.multiple_of(r * NORM_ROWS, NORM_ROWS), NORM_ROWS)
                y_ref[rows, :] = _rms_rows(y_ref[rows, :], gf_ref[...])
                return carry

            lax.fori_loop(0, y_ref.shape[0] // NORM_ROWS, body, 0)


def _ffn(hn, h, g_final, wg, wu, wd, tm, tf, final_norm):
    m = h.shape[0]
    return pl.pallas_call(
        functools.partial(_ffn_kernel, final_norm=final_norm),
        grid=(m // tm, D_FF // tf),
        in_specs=[
            pl.BlockSpec((tm, D_MODEL), lambda i, j: (i, 0)),
            pl.BlockSpec((tm, D_MODEL), lambda i, j: (i, 0)),
            pl.BlockSpec((1, D_MODEL), lambda i, j: (0, 0)),
            pl.BlockSpec((D_MODEL, tf), lambda i, j: (0, j)),
            pl.BlockSpec((D_MODEL, tf), lambda i, j: (0, j)),
            pl.BlockSpec((tf, D_MODEL), lambda i, j: (j, 0)),
        ],
        out_specs=pl.BlockSpec((tm, D_MODEL), lambda i, j: (i, 0)),
        out_shape=jax.ShapeDtypeStruct((m, D_MODEL), F32),
        compiler_params=_params("arbitrary", "arbitrary"),
        name="ffn",
    )(hn, h, g_final, wg, wu, wd)


IN_PROJ_TM_PROMPT = 2048
IN_PROJ_TM_SAMPLE = 1024
IN_PROJ_TN = 512
MIX_TM = 256
FFN_TM = 1024
FFN_TF = 256


def kernel(x_prompt, x_sample, cache_k, cache_v, state_conv, rel_bias, w_in, w_conv, w_conv_out, sinks,
           w_attn_out, w_o, g_mix, g_ffn, w_gate, w_up, w_down, g_final):
    depth = w_in.shape[0]
    batch, seq, _ = x_prompt.shape
    dec_batch, steps, _ = x_sample.shape
    wc = cache_k.shape[2]
    assert seq % WINDOW == 0 and wc == WINDOW and dec_batch % SAMPLE_BATCH_TILE == 0

    hp = x_prompt.reshape(batch * seq, D_MODEL)
    hs = x_sample.reshape(dec_batch * steps, D_MODEL)
    g_final2 = g_final.reshape(1, D_MODEL)
    kp_l, vp_l, cp_l, ks_l, vs_l, cs_l = [], [], [], [], [], []
    for l in range(depth):
        final = l == depth - 1
        lw = {
            "w_in": w_in[l],
            "wco": w_conv_out[l].astype(BF16),
            "wao": w_attn_out[l].astype(BF16),
            "wo": w_o[l].astype(BF16),
            "wg": w_gate[l],
            "wu": w_up[l],
            "wd": w_down[l],
            "g_ffn": g_ffn[l].reshape(1, D_MODEL),
        }
        g_mix_l = g_mix[l].reshape(1, D_MODEL)

        zp = _in_proj(hp, g_mix_l, lw["w_in"], IN_PROJ_TM_PROMPT, IN_PROJ_TN)
        attn_p = _attn_prompt(zp, rel_bias, sinks[l], batch, seq)
        hp, hnp, conv_p = _mix_prompt(zp, attn_p, hp, w_conv[l], lw["g_ffn"], lw["wco"], lw["wao"], lw["wo"],
                                      MIX_TM, batch, seq)
        hp = _ffn(hnp, hp, g_final2, lw["wg"], lw["wu"], lw["wd"], FFN_TM, FFN_TF, final)
        kv_p = zp.reshape(batch, seq, D_IN_PROJ)[:, seq - WINDOW:, OFF_K:OFF_V + D_KV].astype(F32)
        kp_l.append(kv_p[..., :D_KV].reshape(batch, WINDOW, N_KV_HEADS, HEAD_DIM))
        vp_l.append(kv_p[..., D_KV:].reshape(batch, WINDOW, N_KV_HEADS, HEAD_DIM))
        cp_l.append(conv_p)

        zs = _in_proj(hs, g_mix_l, lw["w_in"], IN_PROJ_TM_SAMPLE, IN_PROJ_TN)
        st = state_conv[l]
        zero = jnp.zeros((dec_batch, steps - 2, D_CONV), F32)
        prev1 = jnp.concatenate([st[:, 1:2], zero, zero[:, :1]], axis=1).reshape(dec_batch * steps, D_CONV)
        prev2 = jnp.concatenate([st, zero], axis=1).reshape(dec_batch * steps, D_CONV)
        attn_s, nk, nv = _attn_sample(zs, cache_k[l].reshape(dec_batch, wc, D_KV),
                                      cache_v[l].reshape(dec_batch, wc, D_KV), rel_bias, sinks[l],
                                      dec_batch, steps)
        hs, hns, u_s = _mix_sample(zs, prev1, prev2, attn_s, hs, w_conv[l], lw["g_ffn"], lw["wco"], lw["wao"],
                                   lw["wo"], MIX_TM, steps)
        hs = _ffn(hns, hs, g_final2, lw["wg"], lw["wu"], lw["wd"], FFN_TM, FFN_TF, final)
        ks_l.append(nk.reshape(dec_batch, wc, N_KV_HEADS, HEAD_DIM))
        vs_l.append(nv.reshape(dec_batch, wc, N_KV_HEADS, HEAD_DIM))
        cs_l.append(u_s.reshape(dec_batch, steps, D_CONV)[:, steps - (CONV_WIDTH - 1):])

    return (hp.reshape(batch, seq, D_MODEL), hs.reshape(dec_batch, steps, D_MODEL),
            jnp.stack(kp_l), jnp.stack(vp_l), jnp.stack(cp_l),
            jnp.stack(ks_l), jnp.stack(vs_l), jnp.stack(cs_l))
```

```python
import functools
import math

import numpy as np
import jax
import jax.numpy as jnp
from jax import lax
from jax.experimental import pallas as pl
from jax.experimental.pallas import tpu as pltpu

D_MODEL = 2048
D_CONV = D_MODEL // 2
CONV_WIDTH = 3
HEAD_DIM = 64
N_HEADS = (D_MODEL // 2) // HEAD_DIM
N_KV_HEADS = N_HEADS // 4
GROUP = N_HEADS // N_KV_HEADS
D_ATTN = N_HEADS * HEAD_DIM
D_KV = N_KV_HEADS * HEAD_DIM
WINDOW = 128
NUM_BUCKETS = 32
MAX_DISTANCE = 128
D_FF = -(-8 * D_MODEL // (3 * 256)) * 256
EPS = 1e-6
D_IN_PROJ = 3 * D_CONV + D_ATTN + 2 * D_KV + 2 * D_MODEL

OFF_GC = 0
OFF_GA = D_MODEL
OFF_B = 2 * D_MODEL
OFF_C = OFF_B + D_CONV
OFF_H = OFF_C + D_CONV
OFF_Q = OFF_H + D_CONV
OFF_K = OFF_Q + D_ATTN
OFF_V = OFF_K + D_KV

LANES = 128
NEG = -1e30
F32 = jnp.float32
BF16 = jnp.bfloat16
VMEM_LIMIT = 60 * 1024 * 1024


def _bucket_thresholds():
    max_exact = NUM_BUCKETS // 2
    d = np.arange(MAX_DISTANCE)
    ratio = np.log(np.maximum(d, 1).astype(np.float32) / np.float32(max_exact)) / np.float32(
        math.log(MAX_DISTANCE / max_exact))
    large = max_exact + (ratio * np.float32(NUM_BUCKETS - max_exact)).astype(np.int32)
    large = np.minimum(large, NUM_BUCKETS - 1)
    return [int(np.min(d[(d >= max_exact) & (large >= b)])) for b in range(max_exact + 1, NUM_BUCKETS)]


BUCKET_THRESHOLDS = _bucket_thresholds()


def _params(*semantics):
    return pltpu.CompilerParams(dimension_semantics=semantics, vmem_limit_bytes=VMEM_LIMIT)


def _rms_rows(x, g):
    ms = jnp.mean(x * x, axis=-1, keepdims=True)
    return x * lax.rsqrt(ms + EPS) * g


NORM_ROWS = 256


def _inproj_kernel(x_ref, g_ref, w_ref, z_ref, xn_ref):
    _inproj_norm(pl.program_id(1) == 0, x_ref, g_ref, xn_ref)
    _inproj_columns(w_ref, z_ref, xn_ref)


def _inproj_norm(first_column_step, x_ref, g_ref, xn_ref):
    @pl.when(first_column_step)
    def _():
        def body(r, carry):
            rows = pl.ds(pl.multiple_of(r * NORM_ROWS, NORM_ROWS), NORM_ROWS)
            xn_ref[rows, :] = _rms_rows(x_ref[rows, :], g_ref[...]).astype(xn_ref.dtype)
            return carry

        lax.fori_loop(0, x_ref.shape[0] // NORM_ROWS, body, 0)


def _inproj_columns(w_ref, z_ref, xn_ref):
    z_ref[...] = jnp.dot(xn_ref[...], w_ref[...].astype(BF16), preferred_element_type=F32).astype(z_ref.dtype)


def _in_proj(x, g, w, tm, tn):
    m = x.shape[0]
    n = w.shape[1]
    gate_start = n - 2 * D_MODEL
    assert gate_start % tn == 0 and n % tn == 0
    gate_block, nblocks = gate_start // tn, n // tn
    return pl.pallas_call(
        _inproj_kernel,
        grid=(m // tm, nblocks),
        in_specs=[
            pl.BlockSpec((tm, D_MODEL), lambda i, j: (i, 0)),
            pl.BlockSpec((1, D_MODEL), lambda i, j: (0, 0)),
            pl.BlockSpec((D_MODEL, tn), lambda i, j: (0, (j + gate_block) % nblocks)),
        ],
        out_specs=pl.BlockSpec((tm, tn), lambda i, j: (i, j)),
        out_shape=jax.ShapeDtypeStruct((m, n), BF16),
        scratch_shapes=[pltpu.VMEM((tm, D_MODEL), BF16)],
        compiler_params=_params("arbitrary", "arbitrary"),
        name="in_proj",
    )(x, g, w)


def _bias_from_distance(dist, head, relb_ref):
    valid = (dist >= 0) & (dist < WINDOW)
    d = jnp.clip(dist, 0, MAX_DISTANCE - 1)
    bucket = jnp.minimum(d, NUM_BUCKETS // 2)
    for thr in BUCKET_THRESHOLDS:
        bucket = bucket + (d >= thr).astype(jnp.int32)
    bias = jnp.zeros(dist.shape, F32)
    for b in range(NUM_BUCKETS):
        bias = jnp.where(bucket == b, relb_ref[b, head], bias)
    return jnp.where(valid, bias, NEG)


def _softmax_with_sink(s, sink):
    m = jnp.maximum(jnp.max(s, axis=-1, keepdims=True), sink)
    p = jnp.exp(s - m)
    denom = jnp.sum(p, axis=-1, keepdims=True) + jnp.exp(sink - m)
    return p * (1.0 / denom)


def _head_in_group_columns(q, h):
    half = D_KV // 2
    assert 2 * HEAD_DIM == half and GROUP * HEAD_DIM == D_KV
    g = h // GROUP
    src = q[:, (h // 2) * half:(h // 2 + 1) * half]
    if h % 2 != g % 2:
        src = pltpu.roll(src, HEAD_DIM, 1)
    lane_half = lax.broadcasted_iota(jnp.int32, src.shape, 1) // HEAD_DIM
    placed = jnp.where(lane_half == g % 2, src, 0.0)
    zero = jnp.zeros_like(placed)
    return jnp.concatenate([placed, zero] if g // 2 == 0 else [zero, placed], axis=1)


def _group_columns(h):
    g = h // GROUP
    return slice(g * HEAD_DIM, (g + 1) * HEAD_DIM)


INPROJ_FILL_CHUNKS = 4
PROMPT_BLOCKS_PER_STEP = 4


def _inproj_attn_kernel(x_ref, g_ref, w_ref, relb_ref, sink_ref, q_ref, kp_ref, kc_ref, vp_ref, vc_ref,
                        z_ref, o_ref, xn_ref, wb_ref, bias_ref, lhs_ref, *, steps_per_seq):
    j = pl.program_id(0)
    _attn_prompt_bias(j == 0, relb_ref, bias_ref)
    _inproj_norm(j == 0, x_ref, g_ref, xn_ref)
    wb_ref[...] = w_ref[...].astype(BF16)
    chunk = xn_ref.shape[0] // INPROJ_FILL_CHUNKS

    def column_chunk(c):
        rows = slice(c * chunk, (c + 1) * chunk)

        def emit():
            z_ref[rows, :] = jnp.dot(xn_ref[rows, :], wb_ref[...], preferred_element_type=F32).astype(z_ref.dtype)
        return emit

    _attn_prompt_blocks(j % steps_per_seq != 0, sink_ref, q_ref, kp_ref, kc_ref, vp_ref, vc_ref,
                        o_ref, bias_ref, lhs_ref, fillers=[column_chunk(c) for c in range(INPROJ_FILL_CHUNKS)])


def _from_prev_mask():
    qi = lax.broadcasted_iota(jnp.int32, (WINDOW, WINDOW), 0)
    kj = lax.broadcasted_iota(jnp.int32, (WINDOW, WINDOW), 1)
    return qi, kj, kj > qi


def _attn_prompt_bias(first_step, relb_ref, bias_ref):
    @pl.when(first_step)
    def _():
        qi, kj, from_prev = _from_prev_mask()
        dist = jnp.where(from_prev, qi + WINDOW - kj, qi - kj)
        for h in range(N_HEADS):
            rows = slice(h * WINDOW, (h + 1) * WINDOW)
            bias = _bias_from_distance(dist, h, relb_ref)
            bias_ref[1, rows, :] = bias
            bias_ref[0, rows, :] = jnp.where(from_prev, NEG, bias)


def _attn_prompt_blocks(has_prev_block, sink_ref, q_ref, kp_ref, kc_ref, vp_ref, vc_ref, o_ref, bias_ref, lhs_ref,
                        fillers=()):
    _, _, from_prev = _from_prev_mask()
    k_all = jnp.concatenate([kp_ref[...], kc_ref[...]], axis=0)
    v_all = jnp.concatenate([vp_ref[...], vc_ref[...]], axis=0)

    def keys_of(blk):
        return slice(blk * WINDOW, (blk + 2) * WINDOW)

    nblk = PROMPT_BLOCKS_PER_STEP
    fillers = list(fillers)
    scores, probs = {}, {}
    for stage in range(nblk + 2):
        if stage < nblk:
            q = q_ref[stage * WINDOW:(stage + 1) * WINDOW, :]
            scores[stage] = _block_scores(q, k_all[keys_of(stage)], lhs_ref.at[stage])
        if fillers:
            fillers.pop(0)()
        if 0 <= stage - 1 < nblk:
            blk = stage - 1
            has_prev = has_prev_block.astype(jnp.int32) if blk == 0 else 1
            probs[blk] = _block_softmax(scores.pop(blk), has_prev, from_prev, sink_ref, bias_ref)
        if 0 <= stage - 2 < nblk:
            blk = stage - 2
            o_ref[blk * WINDOW:(blk + 1) * WINDOW, :] = _block_output(
                *probs.pop(blk), v_all[keys_of(blk)]).astype(o_ref.dtype)


def _block_scores(q, kk, lhs_ref):
    q = q.astype(F32) * (HEAD_DIM ** -0.5)
    for h in range(N_HEADS):
        lhs_ref[h * WINDOW:(h + 1) * WINDOW, :] = _head_in_group_columns(q, h).astype(BF16)
    return lax.dot_general(lhs_ref[...], kk, (((1,), (1,)), ((), ())), preferred_element_type=F32)


def _block_softmax(s, has_prev, from_prev, sink_ref, bias_ref):
    probs, sink_terms = [], []
    for h in range(N_HEADS):
        rows = slice(h * WINDOW, (h + 1) * WINDOW)
        sh = jnp.where(from_prev, s[rows, :WINDOW], s[rows, WINDOW:]) + bias_ref[has_prev, rows, :]
        m = jnp.maximum(jnp.max(sh, axis=-1, keepdims=True), sink_ref[h])
        p = jnp.exp(sh - m)
        probs.append(jnp.concatenate([jnp.where(from_prev, p, 0.0), jnp.where(from_prev, 0.0, p)],
                                     axis=1).astype(BF16))
        sink_terms.append(jnp.exp(sink_ref[h] - m))
    return probs, sink_terms


def _block_output(probs, sink_terms, vv):
    assert 2 * HEAD_DIM == LANES and GROUP % 2 == 0
    ones = jnp.ones((vv.shape[0], HEAD_DIM), BF16)
    low_half = lax.broadcasted_iota(jnp.int32, (WINDOW, LANES), 1) < HEAD_DIM
    outs = []
    for g in range(N_KV_HEADS):
        vg = vv[:, g * HEAD_DIM:(g + 1) * HEAD_DIM]
        even_heads = [g * GROUP + r for r in range(0, GROUP, 2)]
        pe = jnp.concatenate([probs[h] for h in even_heads], axis=0)
        po = jnp.concatenate([probs[h + 1] for h in even_heads], axis=0)
        oe = jnp.dot(pe, jnp.concatenate([vg, ones], axis=1), preferred_element_type=F32)
        oo = jnp.dot(po, jnp.concatenate([ones, vg], axis=1), preferred_element_type=F32)
        for k, h in enumerate(even_heads):
            a = oe[k * WINDOW:(k + 1) * WINDOW]
            b = oo[k * WINDOW:(k + 1) * WINDOW]
            out_pair = jnp.where(low_half, a, b)
            sum_pair = pltpu.roll(jnp.where(low_half, b, a), HEAD_DIM, 1)
            sink_pair = jnp.where(low_half, sink_terms[h], sink_terms[h + 1])
            outs.append(out_pair * (1.0 / (sum_pair + sink_pair)))
    return jnp.concatenate(outs, axis=1)


def _in_proj_sample_attn_prompt(x, g, w, zp, rel_bias, sinks, tn, batch, seq):
    tm = x.shape[0]
    n = w.shape[1]
    gate_start = n - 2 * D_MODEL
    assert gate_start % tn == 0 and n % tn == 0
    gate_block, nsteps = gate_start // tn, n // tn
    bps = PROMPT_BLOCKS_PER_STEP
    assert seq % (bps * WINDOW) == 0
    steps_per_seq = seq // (bps * WINDOW)
    attn_steps = batch * steps_per_seq
    assert attn_steps <= nsteps

    def astep(j):
        return jnp.minimum(j, attn_steps - 1)

    def cur(off, width):
        return lambda j: (astep(j), off // width)

    def prev(off, width):
        return lambda j: (astep(j) * bps - jnp.minimum(astep(j) % steps_per_seq, 1), off // width)

    smem = pl.BlockSpec(memory_space=pltpu.SMEM)
    return pl.pallas_call(
        functools.partial(_inproj_attn_kernel, steps_per_seq=steps_per_seq),
        grid=(nsteps,),
        in_specs=[
            pl.BlockSpec((tm, D_MODEL), lambda j: (0, 0)),
            pl.BlockSpec((1, D_MODEL), lambda j: (0, 0)),
            pl.BlockSpec((D_MODEL, tn), lambda j: (0, (j + gate_block) % nsteps)),
            smem,
            smem,
            pl.BlockSpec((bps * WINDOW, D_ATTN), cur(OFF_Q, D_ATTN)),
            pl.BlockSpec((WINDOW, D_KV), prev(OFF_K, D_KV)),
            pl.BlockSpec((bps * WINDOW, D_KV), cur(OFF_K, D_KV)),
            pl.BlockSpec((WINDOW, D_KV), prev(OFF_V, D_KV)),
            pl.BlockSpec((bps * WINDOW, D_KV), cur(OFF_V, D_KV)),
        ],
        out_specs=[
            pl.BlockSpec((tm, tn), lambda j: (0, j)),
            pl.BlockSpec((bps * WINDOW, D_ATTN), lambda j: (j, 0)),
        ],
        out_shape=[
            jax.ShapeDtypeStruct((tm, n), BF16),
            jax.ShapeDtypeStruct((nsteps * bps * WINDOW, D_ATTN), BF16),
        ],
        scratch_shapes=[pltpu.VMEM((tm, D_MODEL), BF16),
                        pltpu.VMEM((D_MODEL, tn), BF16),
                        pltpu.VMEM((2, N_HEADS * WINDOW, WINDOW), F32),
                        pltpu.VMEM((bps, N_HEADS * WINDOW, D_KV), BF16)],
        compiler_params=_params("arbitrary"),
        name="in_proj_sample_attn_prompt",
    )(x, g, w, rel_bias, sinks, zp, zp, zp, zp, zp)


SAMPLE_BATCH_TILE = 16


def _attn_sample_kernel(relb_ref, sink_ref, q_ref, kn_ref, vn_ref, ck_ref, cv_ref,
                        o_ref, nk_ref, nv_ref,
                        bias1_ref, bias2_ref, sinkv_ref, z_ref, s_ref, p_ref, oh_ref, *, steps):
    bt, wc, _ = ck_ref.shape
    tile_rows = bt * steps

    @pl.when(pl.program_id(0) == 0)
    def _():
        t = lax.broadcasted_iota(jnp.int32, (steps, wc), 0)
        j = lax.broadcasted_iota(jnp.int32, (steps, wc), 1)
        t2 = lax.broadcasted_iota(jnp.int32, (steps, tile_rows), 0)
        j2 = lax.broadcasted_iota(jnp.int32, (steps, tile_rows), 1)
        own = jnp.where(j2 < steps, t2 - j2, -1)
        for h in range(N_HEADS):
            sl = slice(h * steps, (h + 1) * steps)
            bias1_ref[sl, :] = _bias_from_distance(t + wc - j, h, relb_ref)
            bias2_ref[0, sl, :] = _bias_from_distance(own, h, relb_ref)
            sinkv_ref[sl, :] = jnp.full((steps, LANES), sink_ref[h], F32)
        for b in range(1, bt):
            bias2_ref[b] = pltpu.roll(bias2_ref[0], b * steps, 1)

    kn = kn_ref[...]
    vn = vn_ref[...]
    nk_ref[:, 0:wc - steps, :] = ck_ref[:, steps:, :]
    nk_ref[:, wc - steps:wc, :] = kn.astype(F32).reshape(bt, steps, D_KV)
    nv_ref[:, 0:wc - steps, :] = cv_ref[:, steps:, :]
    nv_ref[:, wc - steps:wc, :] = vn.astype(F32).reshape(bt, steps, D_KV)

    q = q_ref[...].astype(F32) * (HEAD_DIM ** -0.5)
    for h in range(N_HEADS):
        z_ref[:, h * steps:(h + 1) * steps, :] = _head_in_group_columns(q, h).reshape(bt, steps, D_KV)

    for b in range(bt):
        keys = jnp.concatenate([ck_ref[b].astype(BF16), kn], axis=0)
        s_ref[b] = lax.dot_general(z_ref[b].astype(BF16), keys, (((1,), (1,)), ((), ())),
                                   preferred_element_type=F32)

    bias = jnp.concatenate([jnp.broadcast_to(bias1_ref[...], (bt,) + bias1_ref.shape), bias2_ref[...]], axis=2)
    s = s_ref[...] + bias
    sink = sinkv_ref[...][None]
    m = jnp.maximum(jnp.max(s, axis=-1, keepdims=True), sink)
    p_ref[...] = jnp.exp(s - jnp.concatenate([m] * (s.shape[-1] // LANES), axis=-1)).astype(BF16)
    sink_term = jnp.exp(sink - m)

    vn_ones = jnp.concatenate([vn, jnp.ones((tile_rows, LANES), BF16)], axis=1)
    for b in range(bt):
        cv_ones = jnp.concatenate([cv_ref[b].astype(BF16), jnp.ones((wc, LANES), BF16)], axis=1)
        oh_ref[b] = jnp.dot(p_ref[b], jnp.concatenate([cv_ones, vn_ones], axis=0), preferred_element_type=F32)

    inv = 1.0 / (oh_ref[:, :, D_KV:] + sink_term)
    outs = []
    for h in range(N_HEADS):
        rows = slice(h * steps, (h + 1) * steps)
        outs.append((oh_ref[:, rows, _group_columns(h)] * inv[:, rows, :HEAD_DIM]).reshape(tile_rows, HEAD_DIM))
    o_ref[...] = jnp.concatenate(outs, axis=1).astype(o_ref.dtype)


def _attn_sample(z, cache_k, cache_v, rel_bias, sinks, batch, steps):
    wc = cache_k.shape[1]
    bt = SAMPLE_BATCH_TILE
    rows = bt * steps
    smem = pl.BlockSpec(memory_space=pltpu.SMEM)
    cache_spec = pl.BlockSpec((bt, wc, D_KV), lambda i: (i, 0, 0))
    return pl.pallas_call(
        functools.partial(_attn_sample_kernel, steps=steps),
        grid=(batch // bt,),
        in_specs=[
            smem,
            smem,
            pl.BlockSpec((rows, D_ATTN), lambda i: (i, OFF_Q // D_ATTN)),
            pl.BlockSpec((rows, D_KV), lambda i: (i, OFF_K // D_KV)),
            pl.BlockSpec((rows, D_KV), lambda i: (i, OFF_V // D_KV)),
            cache_spec,
            cache_spec,
        ],
        out_specs=[
            pl.BlockSpec((rows, D_ATTN), lambda i: (i, 0)),
            cache_spec,
            cache_spec,
        ],
        out_shape=[
            jax.ShapeDtypeStruct((batch * steps, D_ATTN), BF16),
            jax.ShapeDtypeStruct(cache_k.shape, F32),
            jax.ShapeDtypeStruct(cache_v.shape, F32),
        ],
        scratch_shapes=[
            pltpu.VMEM((N_HEADS * steps, wc), F32),
            pltpu.VMEM((bt, N_HEADS * steps, rows), F32),
            pltpu.VMEM((N_HEADS * steps, LANES), F32),
            pltpu.VMEM((bt, N_HEADS * steps, D_KV), F32),
            pltpu.VMEM((bt, N_HEADS * steps, wc + rows), F32),
            pltpu.VMEM((bt, N_HEADS * steps, wc + rows), BF16),
            pltpu.VMEM((bt, N_HEADS * steps, D_KV + LANES), F32),
        ],
        compiler_params=_params("arbitrary"),
        name="attn_sample",
    )(rel_bias, sinks, z, z, z, cache_k, cache_v)


CARRY_ROWS = 8


def _shifted_u(u, us_ref):
    tm = u.shape[0]
    us_ref[CARRY_ROWS:, :] = u
    return us_ref[CARRY_ROWS - 1:CARRY_ROWS - 1 + tm, :], us_ref[CARRY_ROWS - 2:CARRY_ROWS - 2 + tm, :]


def _mix_tail(b_ref, wc_ref, u, u1, u2, at_ref, gc_ref, ga_ref, x_ref, g_ref, wco_ref, wao_ref, wo_ref,
              h_ref, hn_ref):
    conv = wc_ref[0:1, :] * u2 + wc_ref[1:2, :] * u1 + wc_ref[2:3, :] * u
    yc = (b_ref[...].astype(F32) * conv).astype(BF16)
    y_conv = jnp.dot(yc, wco_ref[...], preferred_element_type=F32)
    y_attn = jnp.dot(at_ref[...], wao_ref[...], preferred_element_type=F32)
    merged = (jax.nn.sigmoid(gc_ref[...].astype(F32)) * y_conv
              + jax.nn.sigmoid(ga_ref[...].astype(F32)) * y_attn)
    h = x_ref[...] + jnp.dot(merged.astype(BF16), wo_ref[...], preferred_element_type=F32)
    h_ref[...] = h
    hn_ref[...] = _rms_rows(h, g_ref[...]).astype(hn_ref.dtype)


def _mix_prompt_kernel(b_ref, c_ref, hc_ref, wc_ref, at_ref, gc_ref, ga_ref, x_ref, g_ref,
                       wco_ref, wao_ref, wo_ref, h_ref, hn_ref, st_ref, us_ref, *, tiles_per_seq):
    tm = x_ref.shape[0]

    @pl.when(pl.program_id(0) % tiles_per_seq == 0)
    def _():
        us_ref[0:CARRY_ROWS, :] = jnp.zeros((CARRY_ROWS, D_CONV), F32)

    u = c_ref[...].astype(F32) * hc_ref[...].astype(F32)
    u1, u2 = _shifted_u(u, us_ref)
    us_ref[0:CARRY_ROWS, :] = u[tm - CARRY_ROWS:, :]
    st_ref[...] = u[tm - (CONV_WIDTH - 1):, :]
    _mix_tail(b_ref, wc_ref, u, u1, u2, at_ref, gc_ref, ga_ref, x_ref, g_ref, wco_ref, wao_ref, wo_ref,
              h_ref, hn_ref)


def _mix_sample_kernel(b_ref, c_ref, hc_ref, p1_ref, p2_ref, wc_ref, at_ref, gc_ref, ga_ref, x_ref, g_ref,
                       wco_ref, wao_ref, wo_ref, h_ref, hn_ref, u_ref, us_ref, *, steps):
    us_ref[0:CARRY_ROWS, :] = jnp.zeros((CARRY_ROWS, D_CONV), F32)
    u = c_ref[...].astype(F32) * hc_ref[...].astype(F32)
    u_ref[...] = u
    u1, u2 = _shifted_u(u, us_ref)
    t = lax.broadcasted_iota(jnp.int32, u.shape, 0) % steps
    u1 = jnp.where(t >= 1, u1, p1_ref[...])
    u2 = jnp.where(t >= 2, u2, p2_ref[...])
    _mix_tail(b_ref, wc_ref, u, u1, u2, at_ref, gc_ref, ga_ref, x_ref, g_ref, wco_ref, wao_ref, wo_ref,
              h_ref, hn_ref)


def _mix_specs(tm):
    resident = functools.partial(pl.BlockSpec, index_map=lambda i: (0, 0), pipeline_mode=pl.Buffered(1))
    row_spec = pl.BlockSpec((tm, D_MODEL), lambda i: (i, 0))

    def zcol(off, width):
        return pl.BlockSpec((tm, width), lambda i: (i, off // width))

    conv_in = [zcol(OFF_B, D_CONV), zcol(OFF_C, D_CONV), zcol(OFF_H, D_CONV)]
    rest_in = [
        pl.BlockSpec((CONV_WIDTH, D_CONV), lambda i: (0, 0)),
        pl.BlockSpec((tm, D_ATTN), lambda i: (i, 0)),
        zcol(OFF_GC, D_MODEL),
        zcol(OFF_GA, D_MODEL),
        row_spec,
        pl.BlockSpec((1, D_MODEL), lambda i: (0, 0)),
        resident((D_CONV, D_MODEL)),
        resident((D_ATTN, D_MODEL)),
        resident((D_MODEL, D_MODEL)),
    ]
    return conv_in, rest_in, row_spec


def _mix_prompt(z, attn, x, w_conv, g_ffn, wco, wao, wo, tm, batch, seq):
    m = x.shape[0]
    assert seq % tm == 0 and tm >= CARRY_ROWS
    tiles_per_seq = seq // tm
    conv_in, rest_in, row_spec = _mix_specs(tm)
    return pl.pallas_call(
        functools.partial(_mix_prompt_kernel, tiles_per_seq=tiles_per_seq),
        grid=(m // tm,),
        in_specs=conv_in + rest_in,
        out_specs=[row_spec, row_spec,
                   pl.BlockSpec((None, CONV_WIDTH - 1, D_CONV), lambda i: (i // tiles_per_seq, 0, 0))],
        out_shape=[jax.ShapeDtypeStruct((m, D_MODEL), F32), jax.ShapeDtypeStruct((m, D_MODEL), BF16),
                   jax.ShapeDtypeStruct((batch, CONV_WIDTH - 1, D_CONV), F32)],
        scratch_shapes=[pltpu.VMEM((tm + CARRY_ROWS, D_CONV), F32)],
        compiler_params=_params("arbitrary"),
        name="mix_prompt",
    )(z, z, z, w_conv, attn, z, z, x, g_ffn, wco, wao, wo)


def _mix_sample(z, prev1, prev2, attn, x, w_conv, g_ffn, wco, wao, wo, tm, steps):
    m = x.shape[0]
    assert tm % steps == 0 and steps >= CONV_WIDTH - 1
    conv_in, rest_in, row_spec = _mix_specs(tm)
    state_spec = pl.BlockSpec((tm, D_CONV), lambda i: (i, 0))
    return pl.pallas_call(
        functools.partial(_mix_sample_kernel, steps=steps),
        grid=(m // tm,),
        in_specs=conv_in + [state_spec, state_spec] + rest_in,
        out_specs=[row_spec, row_spec, state_spec],
        out_shape=[jax.ShapeDtypeStruct((m, D_MODEL), F32), jax.ShapeDtypeStruct((m, D_MODEL), BF16),
                   jax.ShapeDtypeStruct((m, D_CONV), F32)],
        scratch_shapes=[pltpu.VMEM((tm + CARRY_ROWS, D_CONV), F32)],
        compiler_params=_params("arbitrary"),
        name="mix_sample",
    )(z, z, z, prev1, prev2, w_conv, attn, z, z, x, g_ffn, wco, wao, wo)


def _ffn_kernel(hn_ref, h_ref, gf_ref, wg_ref, wu_ref, wd_ref, y_ref, *, final_norm):
    j = pl.program_id(1)

    @pl.when(j == 0)
    def _():
        y_ref[...] = h_ref[...]

    hn = hn_ref[...]
    a = (jax.nn.silu(jnp.dot(hn, wg_ref[...].astype(BF16), preferred_element_type=F32))
         * jnp.dot(hn, wu_ref[...].astype(BF16), preferred_element_type=F32))
    y_ref[...] += jnp.dot(a.astype(BF16), wd_ref[...].astype(BF16), preferred_element_type=F32)

    if final_norm:
        @pl.when(j == pl.num_programs(1) - 1)
        def _():
            def body(r, carry):
                rows = pl.ds(pl.multiple_of(r * NORM_ROWS, NORM_ROWS), NORM_ROWS)
                y_ref[rows, :] = _rms_rows(y_ref[rows, :], gf_ref[...])
                return carry

            lax.fori_loop(0, y_ref.shape[0] // NORM_ROWS, body, 0)


def _ffn(hn, h, g_final, wg, wu, wd, tm, tf, final_norm):
    m = h.shape[0]
    return pl.pallas_call(
        functools.partial(_ffn_kernel, final_norm=final_norm),
        grid=(m // tm, D_FF // tf),
        in_specs=[
            pl.BlockSpec((tm, D_MODEL), lambda i, j: (i, 0)),
            pl.BlockSpec((tm, D_MODEL), lambda i, j: (i, 0)),
            pl.BlockSpec((1, D_MODEL), lambda i, j: (0, 0)),
            pl.BlockSpec((D_MODEL, tf), lambda i, j: (0, j)),
            pl.BlockSpec((D_MODEL, tf), lambda i, j: (0, j)),
            pl.BlockSpec((tf, D_MODEL), lambda i, j: (j, 0)),
        ],
        out_specs=pl.BlockSpec((tm, D_MODEL), lambda i, j: (i, 0)),
        out_shape=jax.ShapeDtypeStruct((m, D_MODEL), F32),
        compiler_params=_params("arbitrary", "arbitrary"),
        name="ffn",
    )(hn, h, g_final, wg, wu, wd)


IN_PROJ_TM_PROMPT = 2048
IN_PROJ_TN = 512
MIX_TM = 256
FFN_TM = 1024
FFN_TF = 256


def kernel(x_prompt, x_sample, cache_k, cache_v, state_conv, rel_bias, w_in, w_conv, w_conv_out, sinks,
           w_attn_out, w_o, g_mix, g_ffn, w_gate, w_up, w_down, g_final):
    depth = w_in.shape[0]
    batch, seq, _ = x_prompt.shape
    dec_batch, steps, _ = x_sample.shape
    wc = cache_k.shape[2]
    assert seq % WINDOW == 0 and wc == WINDOW and dec_batch % SAMPLE_BATCH_TILE == 0

    hp = x_prompt.reshape(batch * seq, D_MODEL)
    hs = x_sample.reshape(dec_batch * steps, D_MODEL)
    g_final2 = g_final.reshape(1, D_MODEL)
    kp_l, vp_l, cp_l, ks_l, vs_l, cs_l = [], [], [], [], [], []
    for l in range(depth):
        final = l == depth - 1
        lw = {
            "w_in": w_in[l],
            "wco": w_conv_out[l].astype(BF16),
            "wao": w_attn_out[l].astype(BF16),
            "wo": w_o[l].astype(BF16),
            "wg": w_gate[l],
            "wu": w_up[l],
            "wd": w_down[l],
            "g_ffn": g_ffn[l].reshape(1, D_MODEL),
        }
        g_mix_l = g_mix[l].reshape(1, D_MODEL)

        zp = _in_proj(hp, g_mix_l, lw["w_in"], IN_PROJ_TM_PROMPT, IN_PROJ_TN)
        zs, attn_p = _in_proj_sample_attn_prompt(hs, g_mix_l, lw["w_in"], zp, rel_bias, sinks[l], IN_PROJ_TN,
                                                 batch, seq)
        hp, hnp, conv_p = _mix_prompt(zp, attn_p, hp, w_conv[l], lw["g_ffn"], lw["wco"], lw["wao"], lw["wo"],
                                      MIX_TM, batch, seq)
        hp = _ffn(hnp, hp, g_final2, lw["wg"], lw["wu"], lw["wd"], FFN_TM, FFN_TF, final)
        kv_p = zp.reshape(batch, seq, D_IN_PROJ)[:, seq - WINDOW:, OFF_K:OFF_V + D_KV].astype(F32)
        kp_l.append(kv_p[..., :D_KV].reshape(batch, WINDOW, N_KV_HEADS, HEAD_DIM))
        vp_l.append(kv_p[..., D_KV:].reshape(batch, WINDOW, N_KV_HEADS, HEAD_DIM))
        cp_l.append(conv_p)

        st = state_conv[l]
        zero = jnp.zeros((dec_batch, steps - 2, D_CONV), F32)
        prev1 = jnp.concatenate([st[:, 1:2], zero, zero[:, :1]], axis=1).reshape(dec_batch * steps, D_CONV)
        prev2 = jnp.concatenate([st, zero], axis=1).reshape(dec_batch * steps, D_CONV)
        attn_s, nk, nv = _attn_sample(zs, cache_k[l].reshape(dec_batch, wc, D_KV),
                                      cache_v[l].reshape(dec_batch, wc, D_KV), rel_bias, sinks[l],
                                      dec_batch, steps)
        hs, hns, u_s = _mix_sample(zs, prev1, prev2, attn_s, hs, w_conv[l], lw["g_ffn"], lw["wco"], lw["wao"],
                                   lw["wo"], MIX_TM, steps)
        hs = _ffn(hns, hs, g_final2, lw["wg"], lw["wu"], lw["wd"], FFN_TM, FFN_TF, final)
        ks_l.append(nk.reshape(dec_batch, wc, N_KV_HEADS, HEAD_DIM))
        vs_l.append(nv.reshape(dec_batch, wc, N_KV_HEADS, HEAD_DIM))
        cs_l.append(u_s.reshape(dec_batch, steps, D_CONV)[:, steps - (CONV_WIDTH - 1):])

    return (hp.reshape(batch, seq, D_MODEL), hs.reshape(dec_batch, steps, D_MODEL),
            jnp.stack(kp_l), jnp.stack(vp_l), jnp.stack(cp_l),
            jnp.stack(ks_l), jnp.stack(vs_l), jnp.stack(cs_l))
```

```python
import functools
import math

import numpy as np
import jax
import jax.numpy as jnp
from jax import lax
from jax.experimental import pallas as pl
from jax.experimental.pallas import tpu as pltpu

D_MODEL = 2048
D_CONV = D_MODEL // 2
CONV_WIDTH = 3
HEAD_DIM = 64
N_HEADS = (D_MODEL // 2) // HEAD_DIM
N_KV_HEADS = N_HEADS // 4
GROUP = N_HEADS // N_KV_HEADS
D_ATTN = N_HEADS * HEAD_DIM
D_KV = N_KV_HEADS * HEAD_DIM
WINDOW = 128
NUM_BUCKETS = 32
MAX_DISTANCE = 128
D_FF = -(-8 * D_MODEL // (3 * 256)) * 256
EPS = 1e-6
D_IN_PROJ = 3 * D_CONV + D_ATTN + 2 * D_KV + 2 * D_MODEL

OFF_GC = 0
OFF_GA = D_MODEL
OFF_B = 2 * D_MODEL
OFF_C = OFF_B + D_CONV
OFF_H = OFF_C + D_CONV
OFF_Q = OFF_H + D_CONV
OFF_K = OFF_Q + D_ATTN
OFF_V = OFF_K + D_KV

LANES = 128
NEG = -1e30
F32 = jnp.float32
BF16 = jnp.bfloat16
VMEM_LIMIT = 60 * 1024 * 1024


def _bucket_thresholds():
    max_exact = NUM_BUCKETS // 2
    d = np.arange(MAX_DISTANCE)
    ratio = np.log(np.maximum(d, 1).astype(np.float32) / np.float32(max_exact)) / np.float32(
        math.log(MAX_DISTANCE / max_exact))
    large = max_exact + (ratio * np.float32(NUM_BUCKETS - max_exact)).astype(np.int32)
    large = np.minimum(large, NUM_BUCKETS - 1)
    return [int(np.min(d[(d >= max_exact) & (large >= b)])) for b in range(max_exact + 1, NUM_BUCKETS)]


BUCKET_THRESHOLDS = _bucket_thresholds()


def _params(*semantics):
    return pltpu.CompilerParams(dimension_semantics=semantics, vmem_limit_bytes=VMEM_LIMIT)


GAIN_ROWS = 8


def _gain_tile(g):
    return jnp.broadcast_to(g.reshape(1, D_MODEL), (GAIN_ROWS, D_MODEL))


def _rms_rows(x, g):
    ms = jnp.mean(x * x, axis=-1, keepdims=True)
    return x * lax.rsqrt(ms + EPS) * g


NORM_ROWS = 256


def _inproj_kernel(x_ref, g_ref, w_ref, z_ref, xn_ref):
    _inproj_norm(pl.program_id(1) == 0, x_ref, g_ref, xn_ref)
    _inproj_columns(w_ref, z_ref, xn_ref)


def _inproj_norm(first_column_step, x_ref, g_ref, xn_ref):
    @pl.when(first_column_step)
    def _():
        def body(r, carry):
            rows = pl.ds(pl.multiple_of(r * NORM_ROWS, NORM_ROWS), NORM_ROWS)
            xn_ref[rows, :] = _rms_rows(x_ref[rows, :], g_ref[0:1, :]).astype(xn_ref.dtype)
            return carry

        lax.fori_loop(0, x_ref.shape[0] // NORM_ROWS, body, 0)


def _inproj_columns(w_ref, z_ref, xn_ref):
    z_ref[...] = jnp.dot(xn_ref[...], w_ref[...].astype(BF16), preferred_element_type=F32).astype(z_ref.dtype)


def _in_proj(x, g, w, tm, tn):
    m = x.shape[0]
    n = w.shape[1]
    gate_start = n - 2 * D_MODEL
    assert gate_start % tn == 0 and n % tn == 0
    gate_block, nblocks = gate_start // tn, n // tn
    return pl.pallas_call(
        _inproj_kernel,
        grid=(m // tm, nblocks),
        in_specs=[
            pl.BlockSpec((tm, D_MODEL), lambda i, j: (i, 0)),
            pl.BlockSpec((GAIN_ROWS, D_MODEL), lambda i, j: (0, 0)),
            pl.BlockSpec((D_MODEL, tn), lambda i, j: (0, (j + gate_block) % nblocks)),
        ],
        out_specs=pl.BlockSpec((tm, tn), lambda i, j: (i, j)),
        out_shape=jax.ShapeDtypeStruct((m, n), BF16),
        scratch_shapes=[pltpu.VMEM((tm, D_MODEL), BF16)],
        compiler_params=_params("arbitrary", "arbitrary"),
        name="in_proj",
    )(x, g, w)


def _bias_from_distance(dist, head, relb_ref):
    valid = (dist >= 0) & (dist < WINDOW)
    d = jnp.clip(dist, 0, MAX_DISTANCE - 1)
    bucket = jnp.minimum(d, NUM_BUCKETS // 2)
    for thr in BUCKET_THRESHOLDS:
        bucket = bucket + (d >= thr).astype(jnp.int32)
    bias = jnp.zeros(dist.shape, F32)
    for b in range(NUM_BUCKETS):
        bias = jnp.where(bucket == b, relb_ref[b, head], bias)
    return jnp.where(valid, bias, NEG)


def _softmax_with_sink(s, sink):
    m = jnp.maximum(jnp.max(s, axis=-1, keepdims=True), sink)
    p = jnp.exp(s - m)
    denom = jnp.sum(p, axis=-1, keepdims=True) + jnp.exp(sink - m)
    return p * (1.0 / denom)


def _head_in_group_columns(q, h):
    half = D_KV // 2
    assert 2 * HEAD_DIM == half and GROUP * HEAD_DIM == D_KV
    g = h // GROUP
    src = q[:, (h // 2) * half:(h // 2 + 1) * half]
    if h % 2 != g % 2:
        src = pltpu.roll(src, HEAD_DIM, 1)
    lane_half = lax.broadcasted_iota(jnp.int32, src.shape, 1) // HEAD_DIM
    placed = jnp.where(lane_half == g % 2, src, 0.0)
    zero = jnp.zeros_like(placed)
    return jnp.concatenate([placed, zero] if g // 2 == 0 else [zero, placed], axis=1)


def _group_columns(h):
    g = h // GROUP
    return slice(g * HEAD_DIM, (g + 1) * HEAD_DIM)


INPROJ_FILL_CHUNKS = 4
PROMPT_BLOCKS_PER_STEP = 4


def _inproj_attn_kernel(x_ref, g_ref, w_ref, relb_ref, sink_ref, q_ref, kp_ref, kc_ref, vp_ref, vc_ref,
                        z_ref, o_ref, xn_ref, wb_ref, bias_ref, lhs_ref, *, steps_per_seq):
    j = pl.program_id(0)
    _attn_prompt_bias(j == 0, relb_ref, bias_ref)
    _inproj_norm(j == 0, x_ref, g_ref, xn_ref)
    wb_ref[...] = w_ref[...].astype(BF16)
    chunk = xn_ref.shape[0] // INPROJ_FILL_CHUNKS

    def column_chunk(c):
        rows = slice(c * chunk, (c + 1) * chunk)

        def emit():
            z_ref[rows, :] = jnp.dot(xn_ref[rows, :], wb_ref[...], preferred_element_type=F32).astype(z_ref.dtype)
        return emit

    _attn_prompt_blocks(j % steps_per_seq != 0, sink_ref, q_ref, kp_ref, kc_ref, vp_ref, vc_ref,
                        o_ref, bias_ref, lhs_ref, fillers=[column_chunk(c) for c in range(INPROJ_FILL_CHUNKS)])


def _from_prev_mask():
    qi = lax.broadcasted_iota(jnp.int32, (WINDOW, WINDOW), 0)
    kj = lax.broadcasted_iota(jnp.int32, (WINDOW, WINDOW), 1)
    return qi, kj, kj > qi


def _attn_prompt_bias(first_step, relb_ref, bias_ref):
    @pl.when(first_step)
    def _():
        qi, kj, from_prev = _from_prev_mask()
        dist = jnp.where(from_prev, qi + WINDOW - kj, qi - kj)
        for h in range(N_HEADS):
            rows = slice(h * WINDOW, (h + 1) * WINDOW)
            bias = _bias_from_distance(dist, h, relb_ref)
            bias_ref[1, rows, :] = bias
            bias_ref[0, rows, :] = jnp.where(from_prev, NEG, bias)


def _attn_prompt_blocks(has_prev_block, sink_ref, q_ref, kp_ref, kc_ref, vp_ref, vc_ref, o_ref, bias_ref, lhs_ref,
                        fillers=()):
    _, _, from_prev = _from_prev_mask()
    k_all = jnp.concatenate([kp_ref[...], kc_ref[...]], axis=0)
    v_all = jnp.concatenate([vp_ref[...], vc_ref[...]], axis=0)

    def keys_of(blk):
        return slice(blk * WINDOW, (blk + 2) * WINDOW)

    nblk = PROMPT_BLOCKS_PER_STEP
    fillers = list(fillers)
    scores, probs = {}, {}
    for stage in range(nblk + 2):
        if stage < nblk:
            q = q_ref[stage * WINDOW:(stage + 1) * WINDOW, :]
            scores[stage] = _block_scores(q, k_all[keys_of(stage)], lhs_ref.at[stage])
        if fillers:
            fillers.pop(0)()
        if 0 <= stage - 1 < nblk:
            blk = stage - 1
            has_prev = has_prev_block.astype(jnp.int32) if blk == 0 else 1
            probs[blk] = _block_softmax(scores.pop(blk), has_prev, from_prev, sink_ref, bias_ref)
        if 0 <= stage - 2 < nblk:
            blk = stage - 2
            o_ref[blk * WINDOW:(blk + 1) * WINDOW, :] = _block_output(
                *probs.pop(blk), v_all[keys_of(blk)]).astype(o_ref.dtype)


def _block_scores(q, kk, lhs_ref):
    q = q.astype(F32) * (HEAD_DIM ** -0.5)
    for h in range(N_HEADS):
        lhs_ref[h * WINDOW:(h + 1) * WINDOW, :] = _head_in_group_columns(q, h).astype(BF16)
    return lax.dot_general(lhs_ref[...], kk, (((1,), (1,)), ((), ())), preferred_element_type=F32)


def _block_softmax(s, has_prev, from_prev, sink_ref, bias_ref):
    probs, sink_terms = [], []
    for h in range(N_HEADS):
        rows = slice(h * WINDOW, (h + 1) * WINDOW)
        sh = jnp.where(from_prev, s[rows, :WINDOW], s[rows, WINDOW:]) + bias_ref[has_prev, rows, :]
        m = jnp.maximum(jnp.max(sh, axis=-1, keepdims=True), sink_ref[h])
        p = jnp.exp(sh - m)
        probs.append(jnp.concatenate([jnp.where(from_prev, p, 0.0), jnp.where(from_prev, 0.0, p)],
                                     axis=1).astype(BF16))
        sink_terms.append(jnp.exp(sink_ref[h] - m))
    return probs, sink_terms


def _block_output(probs, sink_terms, vv):
    assert 2 * HEAD_DIM == LANES and GROUP % 2 == 0
    ones = jnp.ones((vv.shape[0], HEAD_DIM), BF16)
    low_half = lax.broadcasted_iota(jnp.int32, (WINDOW, LANES), 1) < HEAD_DIM
    outs = []
    for g in range(N_KV_HEADS):
        vg = vv[:, g * HEAD_DIM:(g + 1) * HEAD_DIM]
        even_heads = [g * GROUP + r for r in range(0, GROUP, 2)]
        pe = jnp.concatenate([probs[h] for h in even_heads], axis=0)
        po = jnp.concatenate([probs[h + 1] for h in even_heads], axis=0)
        oe = jnp.dot(pe, jnp.concatenate([vg, ones], axis=1), preferred_element_type=F32)
        oo = jnp.dot(po, jnp.concatenate([ones, vg], axis=1), preferred_element_type=F32)
        for k, h in enumerate(even_heads):
            a = oe[k * WINDOW:(k + 1) * WINDOW]
            b = oo[k * WINDOW:(k + 1) * WINDOW]
            out_pair = jnp.where(low_half, a, b)
            sum_pair = pltpu.roll(jnp.where(low_half, b, a), HEAD_DIM, 1)
            sink_pair = jnp.where(low_half, sink_terms[h], sink_terms[h + 1])
            outs.append(out_pair * (1.0 / (sum_pair + sink_pair)))
    return jnp.concatenate(outs, axis=1)


def _in_proj_sample_attn_prompt(x, g, w, zp, rel_bias, sinks, tn, batch, seq):
    tm = x.shape[0]
    n = w.shape[1]
    gate_start = n - 2 * D_MODEL
    assert gate_start % tn == 0 and n % tn == 0
    gate_block, nsteps = gate_start // tn, n // tn
    bps = PROMPT_BLOCKS_PER_STEP
    assert seq % (bps * WINDOW) == 0
    steps_per_seq = seq // (bps * WINDOW)
    attn_steps = batch * steps_per_seq
    assert attn_steps <= nsteps

    def astep(j):
        return jnp.minimum(j, attn_steps - 1)

    def cur(off, width):
        return lambda j: (astep(j), off // width)

    def prev(off, width):
        return lambda j: (astep(j) * bps - jnp.minimum(astep(j) % steps_per_seq, 1), off // width)

    smem = pl.BlockSpec(memory_space=pltpu.SMEM)
    return pl.pallas_call(
        functools.partial(_inproj_attn_kernel, steps_per_seq=steps_per_seq),
        grid=(nsteps,),
        in_specs=[
            pl.BlockSpec((tm, D_MODEL), lambda j: (0, 0)),
            pl.BlockSpec((GAIN_ROWS, D_MODEL), lambda j: (0, 0)),
            pl.BlockSpec((D_MODEL, tn), lambda j: (0, (j + gate_block) % nsteps)),
            smem,
            smem,
            pl.BlockSpec((bps * WINDOW, D_ATTN), cur(OFF_Q, D_ATTN)),
            pl.BlockSpec((WINDOW, D_KV), prev(OFF_K, D_KV)),
            pl.BlockSpec((bps * WINDOW, D_KV), cur(OFF_K, D_KV)),
            pl.BlockSpec((WINDOW, D_KV), prev(OFF_V, D_KV)),
            pl.BlockSpec((bps * WINDOW, D_KV), cur(OFF_V, D_KV)),
        ],
        out_specs=[
            pl.BlockSpec((tm, tn), lambda j: (0, j)),
            pl.BlockSpec((bps * WINDOW, D_ATTN), lambda j: (j, 0)),
        ],
        out_shape=[
            jax.ShapeDtypeStruct((tm, n), BF16),
            jax.ShapeDtypeStruct((nsteps * bps * WINDOW, D_ATTN), BF16),
        ],
        scratch_shapes=[pltpu.VMEM((tm, D_MODEL), BF16),
                        pltpu.VMEM((D_MODEL, tn), BF16),
                        pltpu.VMEM((2, N_HEADS * WINDOW, WINDOW), F32),
                        pltpu.VMEM((bps, N_HEADS * WINDOW, D_KV), BF16)],
        compiler_params=_params("arbitrary"),
        name="in_proj_sample_attn_prompt",
    )(x, g, w, rel_bias, sinks, zp, zp, zp, zp, zp)


SAMPLE_BATCH_TILE = 16


def _attn_sample_kernel(relb_ref, sink_ref, q_ref, kn_ref, vn_ref, ck_ref, cv_ref,
                        o_ref, nk_ref, nv_ref,
                        bias1_ref, bias2_ref, sinkv_ref, z_ref, s_ref, p_ref, oh_ref, *, steps):
    bt, wc, _ = ck_ref.shape
    tile_rows = bt * steps

    @pl.when(pl.program_id(0) == 0)
    def _():
        t = lax.broadcasted_iota(jnp.int32, (steps, wc), 0)
        j = lax.broadcasted_iota(jnp.int32, (steps, wc), 1)
        t2 = lax.broadcasted_iota(jnp.int32, (steps, tile_rows), 0)
        j2 = lax.broadcasted_iota(jnp.int32, (steps, tile_rows), 1)
        own = jnp.where(j2 < steps, t2 - j2, -1)
        for h in range(N_HEADS):
            sl = slice(h * steps, (h + 1) * steps)
            bias1_ref[sl, :] = _bias_from_distance(t + wc - j, h, relb_ref)
            bias2_ref[0, sl, :] = _bias_from_distance(own, h, relb_ref)
            sinkv_ref[sl, :] = jnp.full((steps, LANES), sink_ref[h], F32)
        for b in range(1, bt):
            bias2_ref[b] = pltpu.roll(bias2_ref[0], b * steps, 1)

    kn = kn_ref[...]
    vn = vn_ref[...]
    nk_ref[:, 0:wc - steps, :] = ck_ref[:, steps:, :]
    nk_ref[:, wc - steps:wc, :] = kn.astype(F32).reshape(bt, steps, D_KV)
    nv_ref[:, 0:wc - steps, :] = cv_ref[:, steps:, :]
    nv_ref[:, wc - steps:wc, :] = vn.astype(F32).reshape(bt, steps, D_KV)

    q = q_ref[...].astype(F32) * (HEAD_DIM ** -0.5)
    for h in range(N_HEADS):
        z_ref[:, h * steps:(h + 1) * steps, :] = _head_in_group_columns(q, h).reshape(bt, steps, D_KV)

    for b in range(bt):
        keys = jnp.concatenate([ck_ref[b].astype(BF16), kn], axis=0)
        s_ref[b] = lax.dot_general(z_ref[b].astype(BF16), keys, (((1,), (1,)), ((), ())),
                                   preferred_element_type=F32)

    bias = jnp.concatenate([jnp.broadcast_to(bias1_ref[...], (bt,) + bias1_ref.shape), bias2_ref[...]], axis=2)
    s = s_ref[...] + bias
    sink = sinkv_ref[...][None]
    m = jnp.maximum(jnp.max(s, axis=-1, keepdims=True), sink)
    p_ref[...] = jnp.exp(s - jnp.concatenate([m] * (s.shape[-1] // LANES), axis=-1)).astype(BF16)
    sink_term = jnp.exp(sink - m)

    vn_ones = jnp.concatenate([vn, jnp.ones((tile_rows, LANES), BF16)], axis=1)
    for b in range(bt):
        cv_ones = jnp.concatenate([cv_ref[b].astype(BF16), jnp.ones((wc, LANES), BF16)], axis=1)
        oh_ref[b] = jnp.dot(p_ref[b], jnp.concatenate([cv_ones, vn_ones], axis=0), preferred_element_type=F32)

    inv = 1.0 / (oh_ref[:, :, D_KV:] + sink_term)
    outs = []
    for h in range(N_HEADS):
        rows = slice(h * steps, (h + 1) * steps)
        outs.append((oh_ref[:, rows, _group_columns(h)] * inv[:, rows, :HEAD_DIM]).reshape(tile_rows, HEAD_DIM))
    o_ref[...] = jnp.concatenate(outs, axis=1).astype(o_ref.dtype)


def _attn_sample(z, cache_k, cache_v, rel_bias, sinks, batch, steps):
    wc = cache_k.shape[1]
    bt = SAMPLE_BATCH_TILE
    rows = bt * steps
    smem = pl.BlockSpec(memory_space=pltpu.SMEM)
    cache_spec = pl.BlockSpec((bt, wc, D_KV), lambda i: (i, 0, 0))
    return pl.pallas_call(
        functools.partial(_attn_sample_kernel, steps=steps),
        grid=(batch // bt,),
        in_specs=[
            smem,
            smem,
            pl.BlockSpec((rows, D_ATTN), lambda i: (i, OFF_Q // D_ATTN)),
            pl.BlockSpec((rows, D_KV), lambda i: (i, OFF_K // D_KV)),
            pl.BlockSpec((rows, D_KV), lambda i: (i, OFF_V // D_KV)),
            cache_spec,
            cache_spec,
        ],
        out_specs=[
            pl.BlockSpec((rows, D_ATTN), lambda i: (i, 0)),
            cache_spec,
            cache_spec,
        ],
        out_shape=[
            jax.ShapeDtypeStruct((batch * steps, D_ATTN), BF16),
            jax.ShapeDtypeStruct(cache_k.shape, F32),
            jax.ShapeDtypeStruct(cache_v.shape, F32),
        ],
        scratch_shapes=[
            pltpu.VMEM((N_HEADS * steps, wc), F32),
            pltpu.VMEM((bt, N_HEADS * steps, rows), F32),
            pltpu.VMEM((N_HEADS * steps, LANES), F32),
            pltpu.VMEM((bt, N_HEADS * steps, D_KV), F32),
            pltpu.VMEM((bt, N_HEADS * steps, wc + rows), F32),
            pltpu.VMEM((bt, N_HEADS * steps, wc + rows), BF16),
            pltpu.VMEM((bt, N_HEADS * steps, D_KV + LANES), F32),
        ],
        compiler_params=_params("arbitrary"),
        name="attn_sample",
    )(rel_bias, sinks, z, z, z, cache_k, cache_v)


CARRY_ROWS = 8


def _mix_tail(delayed_u, b_ref, wc_ref, at_ref, gc_ref, ga_ref, x_ref, g_ref, wco_ref, wao_ref, wo_ref,
              h_ref, hn_ref):
    u, u1, u2 = delayed_u(slice(0, x_ref.shape[0]))
    conv = wc_ref[0:1, :] * u2 + wc_ref[1:2, :] * u1 + wc_ref[2:3, :] * u
    yc = (b_ref[...].astype(F32) * conv).astype(BF16)
    y_conv = jnp.dot(yc, wco_ref[...], preferred_element_type=F32)
    y_attn = jnp.dot(at_ref[...], wao_ref[...], preferred_element_type=F32)
    merged = (jax.nn.sigmoid(gc_ref[...].astype(F32)) * y_conv
              + jax.nn.sigmoid(ga_ref[...].astype(F32)) * y_attn)
    h = x_ref[...] + jnp.dot(merged.astype(BF16), wo_ref[...], preferred_element_type=F32)
    h_ref[...] = h
    hn_ref[...] = _rms_rows(h, g_ref[0:1, :]).astype(hn_ref.dtype)


def _delayed_from_scratch(us_ref, rows):
    def at(delay):
        return us_ref[CARRY_ROWS - delay + rows.start:CARRY_ROWS - delay + rows.stop, :]
    return at(0), at(1), at(2)


def _mix_prompt_kernel(b_ref, c_ref, hc_ref, wc_ref, at_ref, gc_ref, ga_ref, x_ref, g_ref,
                       wco_ref, wao_ref, wo_ref, h_ref, hn_ref, st_ref, us_ref, *, tiles_per_seq):
    tm = x_ref.shape[0]

    @pl.when(pl.program_id(0) % tiles_per_seq == 0)
    def _():
        us_ref[0:CARRY_ROWS, :] = jnp.zeros((CARRY_ROWS, D_CONV), F32)

    u = c_ref[...].astype(F32) * hc_ref[...].astype(F32)
    us_ref[CARRY_ROWS:, :] = u
    st_ref[...] = u[tm - (CONV_WIDTH - 1):, :]
    _mix_tail(functools.partial(_delayed_from_scratch, us_ref), b_ref, wc_ref, at_ref, gc_ref, ga_ref, x_ref,
              g_ref, wco_ref, wao_ref, wo_ref, h_ref, hn_ref)
    us_ref[0:CARRY_ROWS, :] = us_ref[tm:tm + CARRY_ROWS, :]


def _mix_sample_kernel(b_ref, c_ref, hc_ref, p1_ref, p2_ref, wc_ref, at_ref, gc_ref, ga_ref, x_ref, g_ref,
                       wco_ref, wao_ref, wo_ref, h_ref, hn_ref, u_ref, us_ref, *, steps):
    us_ref[0:CARRY_ROWS, :] = jnp.zeros((CARRY_ROWS, D_CONV), F32)
    u = c_ref[...].astype(F32) * hc_ref[...].astype(F32)
    u_ref[...] = u
    us_ref[CARRY_ROWS:, :] = u

    def delayed_u(rows):
        u0, u1, u2 = _delayed_from_scratch(us_ref, rows)
        t = (rows.start + lax.broadcasted_iota(jnp.int32, u0.shape, 0)) % steps
        return u0, jnp.where(t >= 1, u1, p1_ref[rows, :]), jnp.where(t >= 2, u2, p2_ref[rows, :])

    _mix_tail(delayed_u, b_ref, wc_ref, at_ref, gc_ref, ga_ref, x_ref, g_ref, wco_ref, wao_ref, wo_ref,
              h_ref, hn_ref)


def _mix_specs(tm):
    resident = functools.partial(pl.BlockSpec, index_map=lambda i: (0, 0), pipeline_mode=pl.Buffered(1))
    row_spec = pl.BlockSpec((tm, D_MODEL), lambda i: (i, 0))

    def zcol(off, width):
        return pl.BlockSpec((tm, width), lambda i: (i, off // width))

    conv_in = [zcol(OFF_B, D_CONV), zcol(OFF_C, D_CONV), zcol(OFF_H, D_CONV)]
    rest_in = [
        pl.BlockSpec((CONV_WIDTH, D_CONV), lambda i: (0, 0)),
        pl.BlockSpec((tm, D_ATTN), lambda i: (i, 0)),
        zcol(OFF_GC, D_MODEL),
        zcol(OFF_GA, D_MODEL),
        row_spec,
        pl.BlockSpec((GAIN_ROWS, D_MODEL), lambda i: (0, 0)),
        resident((D_CONV, D_MODEL)),
        resident((D_ATTN, D_MODEL)),
        resident((D_MODEL, D_MODEL)),
    ]
    return conv_in, rest_in, row_spec


def _mix_prompt(z, attn, x, w_conv, g_ffn, wco, wao, wo, tm, batch, seq):
    m = x.shape[0]
    assert seq % tm == 0 and tm >= CARRY_ROWS
    tiles_per_seq = seq // tm
    conv_in, rest_in, row_spec = _mix_specs(tm)
    return pl.pallas_call(
        functools.partial(_mix_prompt_kernel, tiles_per_seq=tiles_per_seq),
        grid=(m // tm,),
        in_specs=conv_in + rest_in,
        out_specs=[row_spec, row_spec,
                   pl.BlockSpec((None, CONV_WIDTH - 1, D_CONV), lambda i: (i // tiles_per_seq, 0, 0))],
        out_shape=[jax.ShapeDtypeStruct((m, D_MODEL), F32), jax.ShapeDtypeStruct((m, D_MODEL), BF16),
                   jax.ShapeDtypeStruct((batch, CONV_WIDTH - 1, D_CONV), F32)],
        scratch_shapes=[pltpu.VMEM((tm + CARRY_ROWS, D_CONV), F32)],
        compiler_params=_params("arbitrary"),
        name="mix_prompt",
    )(z, z, z, w_conv, attn, z, z, x, g_ffn, wco, wao, wo)


def _mix_sample(z, prev1, prev2, attn, x, w_conv, g_ffn, wco, wao, wo, tm, steps):
    m = x.shape[0]
    assert tm % steps == 0 and steps >= CONV_WIDTH - 1
    conv_in, rest_in, row_spec = _mix_specs(tm)
    state_spec = pl.BlockSpec((tm, D_CONV), lambda i: (i, 0))
    return pl.pallas_call(
        functools.partial(_mix_sample_kernel, steps=steps),
        grid=(m // tm,),
        in_specs=conv_in + [state_spec, state_spec] + rest_in,
        out_specs=[row_spec, row_spec, state_spec],
        out_shape=[jax.ShapeDtypeStruct((m, D_MODEL), F32), jax.ShapeDtypeStruct((m, D_MODEL), BF16),
                   jax.ShapeDtypeStruct((m, D_CONV), F32)],
        scratch_shapes=[pltpu.VMEM((tm + CARRY_ROWS, D_CONV), F32)],
        compiler_params=_params("arbitrary"),
        name="mix_sample",
    )(z, z, z, prev1, prev2, w_conv, attn, z, z, x, g_ffn, wco, wao, wo)


def _ffn_kernel(hn_ref, h_ref, gf_ref, wg_ref, wu_ref, wd_ref, y_ref, *, final_norm):
    j = pl.program_id(1)

    @pl.when(j == 0)
    def _():
        y_ref[...] = h_ref[...]

    hn = hn_ref[...]
    a = (jax.nn.silu(jnp.dot(hn, wg_ref[...].astype(BF16), preferred_element_type=F32))
         * jnp.dot(hn, wu_ref[...].astype(BF16), preferred_element_type=F32))
    y_ref[...] += jnp.dot(a.astype(BF16), wd_ref[...].astype(BF16), preferred_element_type=F32)

    if final_norm:
        @pl.when(j == pl.num_programs(1) - 1)
        def _():
            def body(r, carry):
                rows = pl.ds(pl.multiple_of(r * NORM_ROWS, NORM_ROWS), NORM_ROWS)
                y_ref[rows, :] = _rms_rows(y_ref[rows, :], gf_ref[0:1, :])
                return carry

            lax.fori_loop(0, y_ref.shape[0] // NORM_ROWS, body, 0)


def _ffn(hn, h, g_final, wg, wu, wd, tm, tf, final_norm):
    m = h.shape[0]
    return pl.pallas_call(
        functools.partial(_ffn_kernel, final_norm=final_norm),
        grid=(m // tm, D_FF // tf),
        in_specs=[
            pl.BlockSpec((tm, D_MODEL), lambda i, j: (i, 0)),
            pl.BlockSpec((tm, D_MODEL), lambda i, j: (i, 0)),
            pl.BlockSpec((GAIN_ROWS, D_MODEL), lambda i, j: (0, 0)),
            pl.BlockSpec((D_MODEL, tf), lambda i, j: (0, j)),
            pl.BlockSpec((D_MODEL, tf), lambda i, j: (0, j)),
            pl.BlockSpec((tf, D_MODEL), lambda i, j: (j, 0)),
        ],
        out_specs=pl.BlockSpec((tm, D_MODEL), lambda i, j: (i, 0)),
        out_shape=jax.ShapeDtypeStruct((m, D_MODEL), F32),
        compiler_params=_params("arbitrary", "arbitrary"),
        name="ffn",
    )(hn, h, g_final, wg, wu, wd)


IN_PROJ_TM_PROMPT = 2048
IN_PROJ_TN = 512
MIX_TM = 256
FFN_TM = 1024
FFN_TF = 256


def kernel(x_prompt, x_sample, cache_k, cache_v, state_conv, rel_bias, w_in, w_conv, w_conv_out, sinks,
           w_attn_out, w_o, g_mix, g_ffn, w_gate, w_up, w_down, g_final):
    depth = w_in.shape[0]
    batch, seq, _ = x_prompt.shape
    dec_batch, steps, _ = x_sample.shape
    wc = cache_k.shape[2]
    assert seq % WINDOW == 0 and wc == WINDOW and dec_batch % SAMPLE_BATCH_TILE == 0

    hp = x_prompt.reshape(batch * seq, D_MODEL)
    hs = x_sample.reshape(dec_batch * steps, D_MODEL)
    g_final2 = _gain_tile(g_final)
    kp_l, vp_l, cp_l, ks_l, vs_l, cs_l = [], [], [], [], [], []
    for l in range(depth):
        final = l == depth - 1
        lw = {
            "w_in": w_in[l],
            "wco": w_conv_out[l].astype(BF16),
            "wao": w_attn_out[l].astype(BF16),
            "wo": w_o[l].astype(BF16),
            "wg": w_gate[l],
            "wu": w_up[l],
            "wd": w_down[l],
            "g_ffn": _gain_tile(g_ffn[l]),
        }
        g_mix_l = _gain_tile(g_mix[l])

        zp = _in_proj(hp, g_mix_l, lw["w_in"], IN_PROJ_TM_PROMPT, IN_PROJ_TN)
        zs, attn_p = _in_proj_sample_attn_prompt(hs, g_mix_l, lw["w_in"], zp, rel_bias, sinks[l], IN_PROJ_TN,
                                                 batch, seq)
        hp, hnp, conv_p = _mix_prompt(zp, attn_p, hp, w_conv[l], lw["g_ffn"], lw["wco"], lw["wao"], lw["wo"],
                                      MIX_TM, batch, seq)
        hp = _ffn(hnp, hp, g_final2, lw["wg"], lw["wu"], lw["wd"], FFN_TM, FFN_TF, final)
        kv_p = zp.reshape(batch, seq, D_IN_PROJ)[:, seq - WINDOW:, OFF_K:OFF_V + D_KV].astype(F32)
        kp_l.append(kv_p[..., :D_KV].reshape(batch, WINDOW, N_KV_HEADS, HEAD_DIM))
        vp_l.append(kv_p[..., D_KV:].reshape(batch, WINDOW, N_KV_HEADS, HEAD_DIM))
        cp_l.append(conv_p)

        st = state_conv[l]
        zero = jnp.zeros((dec_batch, steps - 2, D_CONV), F32)
        prev1 = jnp.concatenate([st[:, 1:2], zero, zero[:, :1]], axis=1).reshape(dec_batch * steps, D_CONV)
        prev2 = jnp.concatenate([st, zero], axis=1).reshape(dec_batch * steps, D_CONV)
        attn_s, nk, nv = _attn_sample(zs, cache_k[l].reshape(dec_batch, wc, D_KV),
                                      cache_v[l].reshape(dec_batch, wc, D_KV), rel_bias, sinks[l],
                                      dec_batch, steps)
        hs, hns, u_s = _mix_sample(zs, prev1, prev2, attn_s, hs, w_conv[l], lw["g_ffn"], lw["wco"], lw["wao"],
                                   lw["wo"], MIX_TM, steps)
        hs = _ffn(hns, hs, g_final2, lw["wg"], lw["wu"], lw["wd"], FFN_TM, FFN_TF, final)
        ks_l.append(nk.reshape(dec_batch, wc, N_KV_HEADS, HEAD_DIM))
        vs_l.append(nv.reshape(dec_batch, wc, N_KV_HEADS, HEAD_DIM))
        cs_l.append(u_s.reshape(dec_batch, steps, D_CONV)[:, steps - (CONV_WIDTH - 1):])

    return (hp.reshape(batch, seq, D_MODEL), hs.reshape(dec_batch, steps, D_MODEL),
            jnp.stack(kp_l), jnp.stack(vp_l), jnp.stack(cp_l),
            jnp.stack(ks_l), jnp.stack(vs_l), jnp.stack(cs_l))
```

```python
import functools
import math

import numpy as np
import jax
import jax.numpy as jnp
from jax import lax
from jax.experimental import pallas as pl
from jax.experimental.pallas import tpu as pltpu

D_MODEL = 2048
D_CONV = D_MODEL // 2
CONV_WIDTH = 3
HEAD_DIM = 64
N_HEADS = (D_MODEL // 2) // HEAD_DIM
N_KV_HEADS = N_HEADS // 4
GROUP = N_HEADS // N_KV_HEADS
D_ATTN = N_HEADS * HEAD_DIM
D_KV = N_KV_HEADS * HEAD_DIM
WINDOW = 128
NUM_BUCKETS = 32
MAX_DISTANCE = 128
D_FF = -(-8 * D_MODEL // (3 * 256)) * 256
EPS = 1e-6
D_IN_PROJ = 3 * D_CONV + D_ATTN + 2 * D_KV + 2 * D_MODEL

OFF_GC = 0
OFF_GA = D_MODEL
OFF_B = 2 * D_MODEL
OFF_C = OFF_B + D_CONV
OFF_H = OFF_C + D_CONV
OFF_Q = OFF_H + D_CONV
OFF_K = OFF_Q + D_ATTN
OFF_V = OFF_K + D_KV

LANES = 128
NEG = -1e30
F32 = jnp.float32
BF16 = jnp.bfloat16
VMEM_LIMIT = 60 * 1024 * 1024


def _bucket_thresholds():
    max_exact = NUM_BUCKETS // 2
    d = np.arange(MAX_DISTANCE)
    ratio = np.log(np.maximum(d, 1).astype(np.float32) / np.float32(max_exact)) / np.float32(
        math.log(MAX_DISTANCE / max_exact))
    large = max_exact + (ratio * np.float32(NUM_BUCKETS - max_exact)).astype(np.int32)
    large = np.minimum(large, NUM_BUCKETS - 1)
    return [int(np.min(d[(d >= max_exact) & (large >= b)])) for b in range(max_exact + 1, NUM_BUCKETS)]


BUCKET_THRESHOLDS = _bucket_thresholds()


def _params(*semantics):
    return pltpu.CompilerParams(dimension_semantics=semantics, vmem_limit_bytes=VMEM_LIMIT)


GAIN_ROWS = 8


def _gain_tile(g):
    return jnp.broadcast_to(g.reshape(1, D_MODEL), (GAIN_ROWS, D_MODEL))


def _rms_rows(x, g):
    ms = jnp.mean(x * x, axis=-1, keepdims=True)
    return x * lax.rsqrt(ms + EPS) * g


NORM_ROWS = 256


def _inproj_kernel(x_ref, g_ref, w_ref, side_ref, z_ref, side_bf16_ref, xn_ref):
    _inproj_norm(pl.program_id(1) == 0, x_ref, g_ref, xn_ref)
    _inproj_columns(w_ref, z_ref, xn_ref)
    side_bf16_ref[...] = side_ref[...].astype(BF16)


BF16_ROWS = 16


def _side_cast_rows(rows, nsteps):
    per_step = -(-rows // nsteps)
    per_step = -(-per_step // BF16_ROWS) * BF16_ROWS
    assert rows % per_step == 0
    return per_step


def _inproj_norm(first_column_step, x_ref, g_ref, xn_ref):
    @pl.when(first_column_step)
    def _():
        def body(r, carry):
            rows = pl.ds(pl.multiple_of(r * NORM_ROWS, NORM_ROWS), NORM_ROWS)
            xn_ref[rows, :] = _rms_rows(x_ref[rows, :], g_ref[0:1, :]).astype(xn_ref.dtype)
            return carry

        lax.fori_loop(0, x_ref.shape[0] // NORM_ROWS, body, 0)


def _inproj_columns(w_ref, z_ref, xn_ref):
    z_ref[...] = jnp.dot(xn_ref[...], w_ref[...].astype(BF16), preferred_element_type=F32).astype(z_ref.dtype)


def _in_proj(x, g, w, side, tm, tn):
    m = x.shape[0]
    n = w.shape[1]
    gate_start = n - 2 * D_MODEL
    assert gate_start % tn == 0 and n % tn == 0
    gate_block, nblocks = gate_start // tn, n // tn
    side_rows = _side_cast_rows(side.shape[0], (m // tm) * nblocks)
    side_spec = pl.BlockSpec((side_rows, side.shape[1]),
                             lambda i, j: (jnp.minimum(i * nblocks + j, side.shape[0] // side_rows - 1), 0))
    return pl.pallas_call(
        _inproj_kernel,
        grid=(m // tm, nblocks),
        in_specs=[
            pl.BlockSpec((tm, D_MODEL), lambda i, j: (i, 0)),
            pl.BlockSpec((GAIN_ROWS, D_MODEL), lambda i, j: (0, 0)),
            pl.BlockSpec((D_MODEL, tn), lambda i, j: (0, (j + gate_block) % nblocks)),
            side_spec,
        ],
        out_specs=[pl.BlockSpec((tm, tn), lambda i, j: (i, j)), side_spec],
        out_shape=[jax.ShapeDtypeStruct((m, n), BF16), jax.ShapeDtypeStruct(side.shape, BF16)],
        scratch_shapes=[pltpu.VMEM((tm, D_MODEL), BF16)],
        compiler_params=_params("arbitrary", "arbitrary"),
        name="in_proj",
    )(x, g, w, side)


def _bias_from_distance(dist, head, relb_ref):
    valid = (dist >= 0) & (dist < WINDOW)
    d = jnp.clip(dist, 0, MAX_DISTANCE - 1)
    bucket = jnp.minimum(d, NUM_BUCKETS // 2)
    for thr in BUCKET_THRESHOLDS:
        bucket = bucket + (d >= thr).astype(jnp.int32)
    bias = jnp.zeros(dist.shape, F32)
    for b in range(NUM_BUCKETS):
        bias = jnp.where(bucket == b, relb_ref[b, head], bias)
    return jnp.where(valid, bias, NEG)


def _softmax_with_sink(s, sink):
    m = jnp.maximum(jnp.max(s, axis=-1, keepdims=True), sink)
    p = jnp.exp(s - m)
    denom = jnp.sum(p, axis=-1, keepdims=True) + jnp.exp(sink - m)
    return p * (1.0 / denom)


def _head_in_group_columns(q, h):
    half = D_KV // 2
    assert 2 * HEAD_DIM == half and GROUP * HEAD_DIM == D_KV
    g = h // GROUP
    src = q[:, (h // 2) * half:(h // 2 + 1) * half]
    if h % 2 != g % 2:
        src = pltpu.roll(src, HEAD_DIM, 1)
    lane_half = lax.broadcasted_iota(jnp.int32, src.shape, 1) // HEAD_DIM
    placed = jnp.where(lane_half == g % 2, src, 0.0)
    zero = jnp.zeros_like(placed)
    return jnp.concatenate([placed, zero] if g // 2 == 0 else [zero, placed], axis=1)


def _group_columns(h):
    g = h // GROUP
    return slice(g * HEAD_DIM, (g + 1) * HEAD_DIM)


INPROJ_FILL_CHUNKS = 4
PROMPT_BLOCKS_PER_STEP = 4


def _inproj_attn_kernel(x_ref, g_ref, w_ref, side0_ref, side1_ref, relb_ref, sink_ref, q_ref, kp_ref, kc_ref,
                        vp_ref, vc_ref, z_ref, o_ref, side0_bf16_ref, side1_bf16_ref,
                        xn_ref, wb_ref, bias_ref, lhs_ref, *, steps_per_seq):
    j = pl.program_id(0)
    _attn_prompt_bias(j == 0, relb_ref, bias_ref)
    _inproj_norm(j == 0, x_ref, g_ref, xn_ref)
    wb_ref[...] = w_ref[...].astype(BF16)
    side0_bf16_ref[...] = side0_ref[...].astype(BF16)
    side1_bf16_ref[...] = side1_ref[...].astype(BF16)
    chunk = xn_ref.shape[0] // INPROJ_FILL_CHUNKS

    def column_chunk(c):
        rows = slice(c * chunk, (c + 1) * chunk)

        def emit():
            z_ref[rows, :] = jnp.dot(xn_ref[rows, :], wb_ref[...], preferred_element_type=F32).astype(z_ref.dtype)
        return emit

    _attn_prompt_blocks(j % steps_per_seq != 0, sink_ref, q_ref, kp_ref, kc_ref, vp_ref, vc_ref,
                        o_ref, bias_ref, lhs_ref, fillers=[column_chunk(c) for c in range(INPROJ_FILL_CHUNKS)])


def _from_prev_mask():
    qi = lax.broadcasted_iota(jnp.int32, (WINDOW, WINDOW), 0)
    kj = lax.broadcasted_iota(jnp.int32, (WINDOW, WINDOW), 1)
    return qi, kj, kj > qi


def _attn_prompt_bias(first_step, relb_ref, bias_ref):
    @pl.when(first_step)
    def _():
        qi, kj, from_prev = _from_prev_mask()
        dist = jnp.where(from_prev, qi + WINDOW - kj, qi - kj)
        for h in range(N_HEADS):
            rows = slice(h * WINDOW, (h + 1) * WINDOW)
            bias = _bias_from_distance(dist, h, relb_ref)
            bias_ref[1, rows, :] = bias
            bias_ref[0, rows, :] = jnp.where(from_prev, NEG, bias)


def _attn_prompt_blocks(has_prev_block, sink_ref, q_ref, kp_ref, kc_ref, vp_ref, vc_ref, o_ref, bias_ref, lhs_ref,
                        fillers=()):
    _, _, from_prev = _from_prev_mask()
    k_all = jnp.concatenate([kp_ref[...], kc_ref[...]], axis=0)
    v_all = jnp.concatenate([vp_ref[...], vc_ref[...]], axis=0)

    def keys_of(blk):
        return slice(blk * WINDOW, (blk + 2) * WINDOW)

    nblk = PROMPT_BLOCKS_PER_STEP
    fillers = list(fillers)
    per_stage = -(-len(fillers) // (nblk + 2))
    scores, probs = {}, {}
    for stage in range(nblk + 2):
        if stage < nblk:
            q = q_ref[stage * WINDOW:(stage + 1) * WINDOW, :]
            scores[stage] = _block_scores(q, k_all[keys_of(stage)], lhs_ref.at[stage])
        for _ in range(min(per_stage, len(fillers))):
            fillers.pop(0)()
        if 0 <= stage - 1 < nblk:
            blk = stage - 1
            has_prev = has_prev_block.astype(jnp.int32) if blk == 0 else 1
            probs[blk] = _block_softmax(scores.pop(blk), has_prev, from_prev, sink_ref, bias_ref)
        if 0 <= stage - 2 < nblk:
            blk = stage - 2
            o_ref[blk * WINDOW:(blk + 1) * WINDOW, :] = _block_output(
                *probs.pop(blk), v_all[keys_of(blk)]).astype(o_ref.dtype)


def _block_scores(q, kk, lhs_ref):
    q = q.astype(F32) * (HEAD_DIM ** -0.5)
    for h in range(N_HEADS):
        lhs_ref[h * WINDOW:(h + 1) * WINDOW, :] = _head_in_group_columns(q, h).astype(BF16)
    return lax.dot_general(lhs_ref[...], kk, (((1,), (1,)), ((), ())), preferred_element_type=F32)


def _block_softmax(s, has_prev, from_prev, sink_ref, bias_ref):
    probs, sink_terms = [], []
    for h in range(N_HEADS):
        rows = slice(h * WINDOW, (h + 1) * WINDOW)
        sh = jnp.where(from_prev, s[rows, :WINDOW], s[rows, WINDOW:]) + bias_ref[has_prev, rows, :]
        m = jnp.maximum(jnp.max(sh, axis=-1, keepdims=True), sink_ref[h])
        p = jnp.exp(sh - m)
        probs.append(jnp.concatenate([jnp.where(from_prev, p, 0.0), jnp.where(from_prev, 0.0, p)],
                                     axis=1).astype(BF16))
        sink_terms.append(jnp.exp(sink_ref[h] - m))
    return probs, sink_terms


def _block_output(probs, sink_terms, vv):
    assert 2 * HEAD_DIM == LANES and GROUP % 2 == 0
    ones = jnp.ones((vv.shape[0], HEAD_DIM), BF16)
    low_half = lax.broadcasted_iota(jnp.int32, (WINDOW, LANES), 1) < HEAD_DIM
    outs = []
    for g in range(N_KV_HEADS):
        vg = vv[:, g * HEAD_DIM:(g + 1) * HEAD_DIM]
        even_heads = [g * GROUP + r for r in range(0, GROUP, 2)]
        pe = jnp.concatenate([probs[h] for h in even_heads], axis=0)
        po = jnp.concatenate([probs[h + 1] for h in even_heads], axis=0)
        oe = jnp.dot(pe, jnp.concatenate([vg, ones], axis=1), preferred_element_type=F32)
        oo = jnp.dot(po, jnp.concatenate([ones, vg], axis=1), preferred_element_type=F32)
        for k, h in enumerate(even_heads):
            a = oe[k * WINDOW:(k + 1) * WINDOW]
            b = oo[k * WINDOW:(k + 1) * WINDOW]
            out_pair = jnp.where(low_half, a, b)
            sum_pair = pltpu.roll(jnp.where(low_half, b, a), HEAD_DIM, 1)
            sink_pair = jnp.where(low_half, sink_terms[h], sink_terms[h + 1])
            outs.append(out_pair * (1.0 / (sum_pair + sink_pair)))
    return jnp.concatenate(outs, axis=1)


def _in_proj_sample_attn_prompt(x, g, w, side0, side1, zp, rel_bias, sinks, tn, batch, seq):
    tm = x.shape[0]
    n = w.shape[1]
    gate_start = n - 2 * D_MODEL
    assert gate_start % tn == 0 and n % tn == 0
    gate_block, nsteps = gate_start // tn, n // tn
    bps = PROMPT_BLOCKS_PER_STEP
    assert seq % (bps * WINDOW) == 0
    steps_per_seq = seq // (bps * WINDOW)
    attn_steps = batch * steps_per_seq
    assert attn_steps <= nsteps

    def astep(j):
        return jnp.minimum(j, attn_steps - 1)

    def cur(off, width):
        return lambda j: (astep(j), off // width)

    def prev(off, width):
        return lambda j: (astep(j) * bps - jnp.minimum(astep(j) % steps_per_seq, 1), off // width)

    def side_spec(side):
        rows = _side_cast_rows(side.shape[0], nsteps)
        return pl.BlockSpec((rows, side.shape[1]), lambda j: (jnp.minimum(j, side.shape[0] // rows - 1), 0))

    smem = pl.BlockSpec(memory_space=pltpu.SMEM)
    return pl.pallas_call(
        functools.partial(_inproj_attn_kernel, steps_per_seq=steps_per_seq),
        grid=(nsteps,),
        in_specs=[
            pl.BlockSpec((tm, D_MODEL), lambda j: (0, 0)),
            pl.BlockSpec((GAIN_ROWS, D_MODEL), lambda j: (0, 0)),
            pl.BlockSpec((D_MODEL, tn), lambda j: (0, (j + gate_block) % nsteps)),
            side_spec(side0),
            side_spec(side1),
            smem,
            smem,
            pl.BlockSpec((bps * WINDOW, D_ATTN), cur(OFF_Q, D_ATTN)),
            pl.BlockSpec((WINDOW, D_KV), prev(OFF_K, D_KV)),
            pl.BlockSpec((bps * WINDOW, D_KV), cur(OFF_K, D_KV)),
            pl.BlockSpec((WINDOW, D_KV), prev(OFF_V, D_KV)),
            pl.BlockSpec((bps * WINDOW, D_KV), cur(OFF_V, D_KV)),
        ],
        out_specs=[
            pl.BlockSpec((tm, tn), lambda j: (0, j)),
            pl.BlockSpec((bps * WINDOW, D_ATTN), lambda j: (j, 0)),
            side_spec(side0),
            side_spec(side1),
        ],
        out_shape=[
            jax.ShapeDtypeStruct((tm, n), BF16),
            jax.ShapeDtypeStruct((nsteps * bps * WINDOW, D_ATTN), BF16),
            jax.ShapeDtypeStruct(side0.shape, BF16),
            jax.ShapeDtypeStruct(side1.shape, BF16),
        ],
        scratch_shapes=[pltpu.VMEM((tm, D_MODEL), BF16),
                        pltpu.VMEM((D_MODEL, tn), BF16),
                        pltpu.VMEM((2, N_HEADS * WINDOW, WINDOW), F32),
                        pltpu.VMEM((bps, N_HEADS * WINDOW, D_KV), BF16)],
        compiler_params=_params("arbitrary"),
        name="in_proj_sample_attn_prompt",
    )(x, g, w, side0, side1, rel_bias, sinks, zp, zp, zp, zp, zp)


SAMPLE_BATCH_TILE = 16


def _attn_sample_kernel(relb_ref, sink_ref, q_ref, kn_ref, vn_ref, ck_ref, cv_ref,
                        o_ref, nk_ref, nv_ref,
                        bias1_ref, bias2_ref, sinkv_ref, z_ref, s_ref, p_ref, oh_ref, *, steps):
    bt, wc, _ = ck_ref.shape
    tile_rows = bt * steps

    @pl.when(pl.program_id(0) == 0)
    def _():
        t = lax.broadcasted_iota(jnp.int32, (steps, wc), 0)
        j = lax.broadcasted_iota(jnp.int32, (steps, wc), 1)
        t2 = lax.broadcasted_iota(jnp.int32, (steps, tile_rows), 0)
        j2 = lax.broadcasted_iota(jnp.int32, (steps, tile_rows), 1)
        own = jnp.where(j2 < steps, t2 - j2, -1)
        for h in range(N_HEADS):
            sl = slice(h * steps, (h + 1) * steps)
            bias1_ref[sl, :] = _bias_from_distance(t + wc - j, h, relb_ref)
            bias2_ref[0, sl, :] = _bias_from_distance(own, h, relb_ref)
            sinkv_ref[sl, :] = jnp.full((steps, LANES), sink_ref[h], F32)
        for b in range(1, bt):
            bias2_ref[b] = pltpu.roll(bias2_ref[0], b * steps, 1)

    kn = kn_ref[...]
    vn = vn_ref[...]
    nk_ref[:, 0:wc - steps, :] = ck_ref[:, steps:, :]
    nk_ref[:, wc - steps:wc, :] = kn.astype(F32).reshape(bt, steps, D_KV)
    nv_ref[:, 0:wc - steps, :] = cv_ref[:, steps:, :]
    nv_ref[:, wc - steps:wc, :] = vn.astype(F32).reshape(bt, steps, D_KV)

    q = q_ref[...].astype(F32) * (HEAD_DIM ** -0.5)
    for h in range(N_HEADS):
        z_ref[:, h * steps:(h + 1) * steps, :] = _head_in_group_columns(q, h).reshape(bt, steps, D_KV)

    for b in range(bt):
        keys = jnp.concatenate([ck_ref[b].astype(BF16), kn], axis=0)
        s_ref[b] = lax.dot_general(z_ref[b].astype(BF16), keys, (((1,), (1,)), ((), ())),
                                   preferred_element_type=F32)

    bias = jnp.concatenate([jnp.broadcast_to(bias1_ref[...], (bt,) + bias1_ref.shape), bias2_ref[...]], axis=2)
    s = s_ref[...] + bias
    sink = sinkv_ref[...][None]
    m = jnp.maximum(jnp.max(s, axis=-1, keepdims=True), sink)
    p_ref[...] = jnp.exp(s - jnp.concatenate([m] * (s.shape[-1] // LANES), axis=-1)).astype(BF16)
    sink_term = jnp.exp(sink - m)

    vn_ones = jnp.concatenate([vn, jnp.ones((tile_rows, LANES), BF16)], axis=1)
    for b in range(bt):
        cv_ones = jnp.concatenate([cv_ref[b].astype(BF16), jnp.ones((wc, LANES), BF16)], axis=1)
        oh_ref[b] = jnp.dot(p_ref[b], jnp.concatenate([cv_ones, vn_ones], axis=0), preferred_element_type=F32)

    inv = 1.0 / (oh_ref[:, :, D_KV:] + sink_term)
    outs = []
    for h in range(N_HEADS):
        rows = slice(h * steps, (h + 1) * steps)
        outs.append((oh_ref[:, rows, _group_columns(h)] * inv[:, rows, :HEAD_DIM]).reshape(tile_rows, HEAD_DIM))
    o_ref[...] = jnp.concatenate(outs, axis=1).astype(o_ref.dtype)


def _attn_sample(z, cache_k, cache_v, rel_bias, sinks, batch, steps):
    wc = cache_k.shape[1]
    bt = SAMPLE_BATCH_TILE
    rows = bt * steps
    smem = pl.BlockSpec(memory_space=pltpu.SMEM)
    cache_spec = pl.BlockSpec((bt, wc, D_KV), lambda i: (i, 0, 0))
    return pl.pallas_call(
        functools.partial(_attn_sample_kernel, steps=steps),
        grid=(batch // bt,),
        in_specs=[
            smem,
            smem,
            pl.BlockSpec((rows, D_ATTN), lambda i: (i, OFF_Q // D_ATTN)),
            pl.BlockSpec((rows, D_KV), lambda i: (i, OFF_K // D_KV)),
            pl.BlockSpec((rows, D_KV), lambda i: (i, OFF_V // D_KV)),
            cache_spec,
            cache_spec,
        ],
        out_specs=[
            pl.BlockSpec((rows, D_ATTN), lambda i: (i, 0)),
            cache_spec,
            cache_spec,
        ],
        out_shape=[
            jax.ShapeDtypeStruct((batch * steps, D_ATTN), BF16),
            jax.ShapeDtypeStruct(cache_k.shape, F32),
            jax.ShapeDtypeStruct(cache_v.shape, F32),
        ],
        scratch_shapes=[
            pltpu.VMEM((N_HEADS * steps, wc), F32),
            pltpu.VMEM((bt, N_HEADS * steps, rows), F32),
            pltpu.VMEM((N_HEADS * steps, LANES), F32),
            pltpu.VMEM((bt, N_HEADS * steps, D_KV), F32),
            pltpu.VMEM((bt, N_HEADS * steps, wc + rows), F32),
            pltpu.VMEM((bt, N_HEADS * steps, wc + rows), BF16),
            pltpu.VMEM((bt, N_HEADS * steps, D_KV + LANES), F32),
        ],
        compiler_params=_params("arbitrary"),
        name="attn_sample",
    )(rel_bias, sinks, z, z, z, cache_k, cache_v)


CARRY_ROWS = 8


def _gated_conv(b, wc_ref, u, u1, u2):
    conv = wc_ref[0:1, :] * u2 + wc_ref[1:2, :] * u1 + wc_ref[2:3, :] * u
    return (b.astype(F32) * conv).astype(BF16)


def _mix_tail(yc, at_ref, gc_ref, ga_ref, x_ref, g_ref, wco_ref, wao_ref, wo_ref, h_ref, hn_ref):
    y_conv = jnp.dot(yc, wco_ref[...], preferred_element_type=F32)
    y_attn = jnp.dot(at_ref[...], wao_ref[...], preferred_element_type=F32)
    merged = (jax.nn.sigmoid(gc_ref[...].astype(F32)) * y_conv
              + jax.nn.sigmoid(ga_ref[...].astype(F32)) * y_attn)
    h = x_ref[...] + jnp.dot(merged.astype(BF16), wo_ref[...], preferred_element_type=F32)
    h_ref[...] = h
    hn_ref[...] = _rms_rows(h, g_ref[0:1, :]).astype(hn_ref.dtype)


def _delayed_from_scratch(us_ref, rows):
    def at(delay):
        return us_ref[CARRY_ROWS - delay + rows.start:CARRY_ROWS - delay + rows.stop, :]
    return at(0), at(1), at(2)


def _mix_prompt_kernel(b_ref, c_ref, hc_ref, wc_ref, at_ref, gc_ref, ga_ref, x_ref, g_ref,
                       wco_ref, wao_ref, wo_ref, h_ref, hn_ref, st_ref, us_ref, *, tiles_per_seq):
    tm = x_ref.shape[0]

    @pl.when(pl.program_id(0) % tiles_per_seq == 0)
    def _():
        us_ref[0:CARRY_ROWS, :] = jnp.zeros((CARRY_ROWS, D_CONV), F32)

    u = c_ref[...].astype(F32) * hc_ref[...].astype(F32)
    us_ref[CARRY_ROWS:, :] = u
    st_ref[...] = u[tm - (CONV_WIDTH - 1):, :]
    yc = _gated_conv(b_ref[...], wc_ref, *_delayed_from_scratch(us_ref, slice(0, tm)))
    us_ref[0:CARRY_ROWS, :] = us_ref[tm:tm + CARRY_ROWS, :]
    _mix_tail(yc, at_ref, gc_ref, ga_ref, x_ref, g_ref, wco_ref, wao_ref, wo_ref, h_ref, hn_ref)


def _mix_sample_kernel(b_ref, c_ref, hc_ref, p1_ref, p2_ref, wc_ref, at_ref, gc_ref, ga_ref, x_ref, g_ref,
                       wco_ref, wao_ref, wo_ref, h_ref, hn_ref, u_ref, us_ref, *, steps):
    tm = x_ref.shape[0]
    us_ref[0:CARRY_ROWS, :] = jnp.zeros((CARRY_ROWS, D_CONV), F32)
    u = c_ref[...].astype(F32) * hc_ref[...].astype(F32)
    u_ref[...] = u
    us_ref[CARRY_ROWS:, :] = u
    u0, u1, u2 = _delayed_from_scratch(us_ref, slice(0, tm))
    t = lax.broadcasted_iota(jnp.int32, u0.shape, 0) % steps
    yc = _gated_conv(b_ref[...], wc_ref, u0, jnp.where(t >= 1, u1, p1_ref[...]), jnp.where(t >= 2, u2, p2_ref[...]))
    _mix_tail(yc, at_ref, gc_ref, ga_ref, x_ref, g_ref, wco_ref, wao_ref, wo_ref, h_ref, hn_ref)


def _mix_specs(tm):
    resident = functools.partial(pl.BlockSpec, index_map=lambda i: (0, 0), pipeline_mode=pl.Buffered(1))
    row_spec = pl.BlockSpec((tm, D_MODEL), lambda i: (i, 0))

    def zcol(off, width):
        return pl.BlockSpec((tm, width), lambda i: (i, off // width))

    conv_in = [zcol(OFF_B, D_CONV), zcol(OFF_C, D_CONV), zcol(OFF_H, D_CONV)]
    rest_in = [
        pl.BlockSpec((CONV_WIDTH, D_CONV), lambda i: (0, 0)),
        pl.BlockSpec((tm, D_ATTN), lambda i: (i, 0)),
        zcol(OFF_GC, D_MODEL),
        zcol(OFF_GA, D_MODEL),
        row_spec,
        pl.BlockSpec((GAIN_ROWS, D_MODEL), lambda i: (0, 0)),
        resident((D_CONV, D_MODEL)),
        resident((D_ATTN, D_MODEL)),
        resident((D_MODEL, D_MODEL)),
    ]
    return conv_in, rest_in, row_spec


def _mix_prompt(z, attn, x, w_conv, g_ffn, wco, wao, wo, tm, batch, seq):
    m = x.shape[0]
    assert seq % tm == 0 and tm >= CARRY_ROWS
    tiles_per_seq = seq // tm
    conv_in, rest_in, row_spec = _mix_specs(tm)
    return pl.pallas_call(
        functools.partial(_mix_prompt_kernel, tiles_per_seq=tiles_per_seq),
        grid=(m // tm,),
        in_specs=conv_in + rest_in,
        out_specs=[row_spec, row_spec,
                   pl.BlockSpec((None, CONV_WIDTH - 1, D_CONV), lambda i: (i // tiles_per_seq, 0, 0))],
        out_shape=[jax.ShapeDtypeStruct((m, D_MODEL), F32), jax.ShapeDtypeStruct((m, D_MODEL), BF16),
                   jax.ShapeDtypeStruct((batch, CONV_WIDTH - 1, D_CONV), F32)],
        scratch_shapes=[pltpu.VMEM((tm + CARRY_ROWS, D_CONV), F32)],
        compiler_params=_params("arbitrary"),
        name="mix_prompt",
    )(z, z, z, w_conv, attn, z, z, x, g_ffn, wco, wao, wo)


def _mix_sample(z, prev1, prev2, attn, x, w_conv, g_ffn, wco, wao, wo, tm, steps):
    m = x.shape[0]
    assert tm % steps == 0 and steps >= CONV_WIDTH - 1
    conv_in, rest_in, row_spec = _mix_specs(tm)
    state_spec = pl.BlockSpec((tm, D_CONV), lambda i: (i, 0))
    return pl.pallas_call(
        functools.partial(_mix_sample_kernel, steps=steps),
        grid=(m // tm,),
        in_specs=conv_in + [state_spec, state_spec] + rest_in,
        out_specs=[row_spec, row_spec, state_spec],
        out_shape=[jax.ShapeDtypeStruct((m, D_MODEL), F32), jax.ShapeDtypeStruct((m, D_MODEL), BF16),
                   jax.ShapeDtypeStruct((m, D_CONV), F32)],
        scratch_shapes=[pltpu.VMEM((tm + CARRY_ROWS, D_CONV), F32)],
        compiler_params=_params("arbitrary"),
        name="mix_sample",
    )(z, z, z, prev1, prev2, w_conv, attn, z, z, x, g_ffn, wco, wao, wo)


def _ffn_kernel(hn_ref, h_ref, gf_ref, wg_ref, wu_ref, wd_ref, y_ref, *, final_norm):
    j = pl.program_id(1)

    @pl.when(j == 0)
    def _():
        y_ref[...] = h_ref[...]

    hn = hn_ref[...]
    a = (jax.nn.silu(jnp.dot(hn, wg_ref[...].astype(BF16), preferred_element_type=F32))
         * jnp.dot(hn, wu_ref[...].astype(BF16), preferred_element_type=F32))
    y_ref[...] += jnp.dot(a.astype(BF16), wd_ref[...].astype(BF16), preferred_element_type=F32)

    if final_norm:
        @pl.when(j == pl.num_programs(1) - 1)
        def _():
            def body(r, carry):
                rows = pl.ds(pl.multiple_of(r * NORM_ROWS, NORM_ROWS), NORM_ROWS)
                y_ref[rows, :] = _rms_rows(y_ref[rows, :], gf_ref[0:1, :])
                return carry

            lax.fori_loop(0, y_ref.shape[0] // NORM_ROWS, body, 0)


def _ffn(hn, h, g_final, wg, wu, wd, tm, tf, final_norm):
    m = h.shape[0]
    return pl.pallas_call(
        functools.partial(_ffn_kernel, final_norm=final_norm),
        grid=(m // tm, D_FF // tf),
        in_specs=[
            pl.BlockSpec((tm, D_MODEL), lambda i, j: (i, 0)),
            pl.BlockSpec((tm, D_MODEL), lambda i, j: (i, 0)),
            pl.BlockSpec((GAIN_ROWS, D_MODEL), lambda i, j: (0, 0)),
            pl.BlockSpec((D_MODEL, tf), lambda i, j: (0, j)),
            pl.BlockSpec((D_MODEL, tf), lambda i, j: (0, j)),
            pl.BlockSpec((tf, D_MODEL), lambda i, j: (j, 0)),
        ],
        out_specs=pl.BlockSpec((tm, D_MODEL), lambda i, j: (i, 0)),
        out_shape=jax.ShapeDtypeStruct((m, D_MODEL), F32),
        compiler_params=_params("arbitrary", "arbitrary"),
        name="ffn",
    )(hn, h, g_final, wg, wu, wd)


IN_PROJ_TM_PROMPT = 2048
IN_PROJ_TN = 512
MIX_TM = 256
FFN_TM = 1024
FFN_TF = 256


def kernel(x_prompt, x_sample, cache_k, cache_v, state_conv, rel_bias, w_in, w_conv, w_conv_out, sinks,
           w_attn_out, w_o, g_mix, g_ffn, w_gate, w_up, w_down, g_final):
    depth = w_in.shape[0]
    batch, seq, _ = x_prompt.shape
    dec_batch, steps, _ = x_sample.shape
    wc = cache_k.shape[2]
    assert seq % WINDOW == 0 and wc == WINDOW and dec_batch % SAMPLE_BATCH_TILE == 0

    hp = x_prompt.reshape(batch * seq, D_MODEL)
    hs = x_sample.reshape(dec_batch * steps, D_MODEL)
    g_final2 = _gain_tile(g_final)
    kp_l, vp_l, cp_l, ks_l, vs_l, cs_l = [], [], [], [], [], []
    for l in range(depth):
        final = l == depth - 1
        lw = {
            "w_in": w_in[l],
            "wg": w_gate[l],
            "wu": w_up[l],
            "wd": w_down[l],
            "g_ffn": _gain_tile(g_ffn[l]),
        }
        g_mix_l = _gain_tile(g_mix[l])

        zp, lw["wo"] = _in_proj(hp, g_mix_l, lw["w_in"], w_o[l], IN_PROJ_TM_PROMPT, IN_PROJ_TN)
        zs, attn_p, lw["wco"], lw["wao"] = _in_proj_sample_attn_prompt(
            hs, g_mix_l, lw["w_in"], w_conv_out[l], w_attn_out[l], zp, rel_bias, sinks[l], IN_PROJ_TN, batch, seq)
        hp, hnp, conv_p = _mix_prompt(zp, attn_p, hp, w_conv[l], lw["g_ffn"], lw["wco"], lw["wao"], lw["wo"],
                                      MIX_TM, batch, seq)
        hp = _ffn(hnp, hp, g_final2, lw["wg"], lw["wu"], lw["wd"], FFN_TM, FFN_TF, final)
        kv_p = zp.reshape(batch, seq, D_IN_PROJ)[:, seq - WINDOW:, OFF_K:OFF_V + D_KV].astype(F32)
        kp_l.append(kv_p[..., :D_KV].reshape(batch, WINDOW, N_KV_HEADS, HEAD_DIM))
        vp_l.append(kv_p[..., D_KV:].reshape(batch, WINDOW, N_KV_HEADS, HEAD_DIM))
        cp_l.append(conv_p)

        st = state_conv[l]
        zero = jnp.zeros((dec_batch, steps - 2, D_CONV), F32)
        prev1 = jnp.concatenate([st[:, 1:2], zero, zero[:, :1]], axis=1).reshape(dec_batch * steps, D_CONV)
        prev2 = jnp.concatenate([st, zero], axis=1).reshape(dec_batch * steps, D_CONV)
        attn_s, nk, nv = _attn_sample(zs, cache_k[l].reshape(dec_batch, wc, D_KV),
                                      cache_v[l].reshape(dec_batch, wc, D_KV), rel_bias, sinks[l],
                                      dec_batch, steps)
        hs, hns, u_s = _mix_sample(zs, prev1, prev2, attn_s, hs, w_conv[l], lw["g_ffn"], lw["wco"], lw["wao"],
                                   lw["wo"], MIX_TM, steps)
        hs = _ffn(hns, hs, g_final2, lw["wg"], lw["wu"], lw["wd"], FFN_TM, FFN_TF, final)
        ks_l.append(nk.reshape(dec_batch, wc, N_KV_HEADS, HEAD_DIM))
        vs_l.append(nv.reshape(dec_batch, wc, N_KV_HEADS, HEAD_DIM))
        cs_l.append(u_s.reshape(dec_batch, steps, D_CONV)[:, steps - (CONV_WIDTH - 1):])

    return (hp.reshape(batch, seq, D_MODEL), hs.reshape(dec_batch, steps, D_MODEL),
            jnp.stack(kp_l), jnp.stack(vp_l), jnp.stack(cp_l),
            jnp.stack(ks_l), jnp.stack(vs_l), jnp.stack(cs_l))
```

```python
import functools
import math

import numpy as np
import jax
import jax.numpy as jnp
from jax import lax
from jax.experimental import pallas as pl
from jax.experimental.pallas import tpu as pltpu

D_MODEL = 2048
D_CONV = D_MODEL // 2
CONV_WIDTH = 3
HEAD_DIM = 64
N_HEADS = (D_MODEL // 2) // HEAD_DIM
N_KV_HEADS = N_HEADS // 4
GROUP = N_HEADS // N_KV_HEADS
D_ATTN = N_HEADS * HEAD_DIM
D_KV = N_KV_HEADS * HEAD_DIM
WINDOW = 128
NUM_BUCKETS = 32
MAX_DISTANCE = 128
D_FF = -(-8 * D_MODEL // (3 * 256)) * 256
EPS = 1e-6
D_IN_PROJ = 3 * D_CONV + D_ATTN + 2 * D_KV + 2 * D_MODEL

OFF_GC = 0
OFF_GA = D_MODEL
OFF_B = 2 * D_MODEL
OFF_C = OFF_B + D_CONV
OFF_H = OFF_C + D_CONV
OFF_Q = OFF_H + D_CONV
OFF_K = OFF_Q + D_ATTN
OFF_V = OFF_K + D_KV

LANES = 128
NEG = -1e30
F32 = jnp.float32
BF16 = jnp.bfloat16
VMEM_LIMIT = 60 * 1024 * 1024


def _bucket_thresholds():
    max_exact = NUM_BUCKETS // 2
    d = np.arange(MAX_DISTANCE)
    ratio = np.log(np.maximum(d, 1).astype(np.float32) / np.float32(max_exact)) / np.float32(
        math.log(MAX_DISTANCE / max_exact))
    large = max_exact + (ratio * np.float32(NUM_BUCKETS - max_exact)).astype(np.int32)
    large = np.minimum(large, NUM_BUCKETS - 1)
    return [int(np.min(d[(d >= max_exact) & (large >= b)])) for b in range(max_exact + 1, NUM_BUCKETS)]


BUCKET_THRESHOLDS = _bucket_thresholds()


def _params(*semantics):
    return pltpu.CompilerParams(dimension_semantics=semantics, vmem_limit_bytes=VMEM_LIMIT)


GAIN_ROWS = 8


def _gain_tile(g):
    return jnp.broadcast_to(g.reshape(1, D_MODEL), (GAIN_ROWS, D_MODEL))


def _rms_rows(x, g):
    ms = jnp.mean(x * x, axis=-1, keepdims=True)
    return x * lax.rsqrt(ms + EPS) * g


NORM_ROWS = 256


def _inproj_kernel(x_ref, g_ref, w_ref, side_ref, z_ref, side_bf16_ref, xn_ref):
    _inproj_norm(pl.program_id(1) == 0, x_ref, g_ref, xn_ref)
    _inproj_columns(w_ref, z_ref, xn_ref)
    side_bf16_ref[...] = side_ref[...].astype(BF16)


BF16_ROWS = 16


def _side_cast_rows(rows, nsteps):
    per_step = -(-rows // nsteps)
    per_step = -(-per_step // BF16_ROWS) * BF16_ROWS
    assert rows % per_step == 0
    return per_step


def _inproj_norm(first_column_step, x_ref, g_ref, xn_ref):
    @pl.when(first_column_step)
    def _():
        def body(r, carry):
            rows = pl.ds(pl.multiple_of(r * NORM_ROWS, NORM_ROWS), NORM_ROWS)
            xn_ref[rows, :] = _rms_rows(x_ref[rows, :], g_ref[0:1, :]).astype(xn_ref.dtype)
            return carry

        lax.fori_loop(0, x_ref.shape[0] // NORM_ROWS, body, 0)


def _inproj_columns(w_ref, z_ref, xn_ref):
    z_ref[...] = jnp.dot(xn_ref[...], w_ref[...].astype(BF16), preferred_element_type=F32).astype(z_ref.dtype)


def _in_proj(x, g, w, side, tm, tn):
    m = x.shape[0]
    n = w.shape[1]
    gate_start = n - 2 * D_MODEL
    assert gate_start % tn == 0 and n % tn == 0
    gate_block, nblocks = gate_start // tn, n // tn
    side_rows = _side_cast_rows(side.shape[0], (m // tm) * nblocks)
    side_spec = pl.BlockSpec((side_rows, side.shape[1]),
                             lambda i, j: (jnp.minimum(i * nblocks + j, side.shape[0] // side_rows - 1), 0))
    return pl.pallas_call(
        _inproj_kernel,
        grid=(m // tm, nblocks),
        in_specs=[
            pl.BlockSpec((tm, D_MODEL), lambda i, j: (i, 0)),
            pl.BlockSpec((GAIN_ROWS, D_MODEL), lambda i, j: (0, 0)),
            pl.BlockSpec((D_MODEL, tn), lambda i, j: (0, (j + gate_block) % nblocks)),
            side_spec,
        ],
        out_specs=[pl.BlockSpec((tm, tn), lambda i, j: (i, j)), side_spec],
        out_shape=[jax.ShapeDtypeStruct((m, n), BF16), jax.ShapeDtypeStruct(side.shape, BF16)],
        scratch_shapes=[pltpu.VMEM((tm, D_MODEL), BF16)],
        compiler_params=_params("arbitrary", "arbitrary"),
        name="in_proj",
    )(x, g, w, side)


def _bias_from_distance(dist, head, relb_ref):
    valid = (dist >= 0) & (dist < WINDOW)
    d = jnp.clip(dist, 0, MAX_DISTANCE - 1)
    bucket = jnp.minimum(d, NUM_BUCKETS // 2)
    for thr in BUCKET_THRESHOLDS:
        bucket = bucket + (d >= thr).astype(jnp.int32)
    bias = jnp.zeros(dist.shape, F32)
    for b in range(NUM_BUCKETS):
        bias = jnp.where(bucket == b, relb_ref[b, head], bias)
    return jnp.where(valid, bias, NEG)


def _softmax_with_sink(s, sink):
    m = jnp.maximum(jnp.max(s, axis=-1, keepdims=True), sink)
    p = jnp.exp(s - m)
    denom = jnp.sum(p, axis=-1, keepdims=True) + jnp.exp(sink - m)
    return p * (1.0 / denom)


def _head_in_group_columns(q, h):
    half = D_KV // 2
    assert 2 * HEAD_DIM == half and GROUP * HEAD_DIM == D_KV
    g = h // GROUP
    src = q[:, (h // 2) * half:(h // 2 + 1) * half]
    if h % 2 != g % 2:
        src = pltpu.roll(src, HEAD_DIM, 1)
    lane_half = lax.broadcasted_iota(jnp.int32, src.shape, 1) // HEAD_DIM
    placed = jnp.where(lane_half == g % 2, src, 0.0)
    zero = jnp.zeros_like(placed)
    return jnp.concatenate([placed, zero] if g // 2 == 0 else [zero, placed], axis=1)


def _group_columns(h):
    g = h // GROUP
    return slice(g * HEAD_DIM, (g + 1) * HEAD_DIM)


INPROJ_FILL_CHUNKS = 4
PROMPT_BLOCKS_PER_STEP = 4


def _inproj_attn_kernel(x_ref, g_ref, w_ref, side0_ref, side1_ref, relb_ref, sink_ref, q_ref, kp_ref, kc_ref,
                        vp_ref, vc_ref, z_ref, o_ref, side0_bf16_ref, side1_bf16_ref,
                        xn_ref, wb_ref, bias_ref, lhs_ref, *, steps_per_seq):
    j = pl.program_id(0)
    _attn_prompt_bias(j == 0, relb_ref, bias_ref)
    _inproj_norm(j == 0, x_ref, g_ref, xn_ref)
    wb_ref[...] = w_ref[...].astype(BF16)
    side0_bf16_ref[...] = side0_ref[...].astype(BF16)
    side1_bf16_ref[...] = side1_ref[...].astype(BF16)
    chunk = xn_ref.shape[0] // INPROJ_FILL_CHUNKS

    def column_chunk(c):
        rows = slice(c * chunk, (c + 1) * chunk)

        def emit():
            z_ref[rows, :] = jnp.dot(xn_ref[rows, :], wb_ref[...], preferred_element_type=F32).astype(z_ref.dtype)
        return emit

    _attn_prompt_blocks(j % steps_per_seq != 0, sink_ref, q_ref, kp_ref, kc_ref, vp_ref, vc_ref,
                        o_ref, bias_ref, lhs_ref, fillers=[column_chunk(c) for c in range(INPROJ_FILL_CHUNKS)])


def _from_prev_mask():
    qi = lax.broadcasted_iota(jnp.int32, (WINDOW, WINDOW), 0)
    kj = lax.broadcasted_iota(jnp.int32, (WINDOW, WINDOW), 1)
    return qi, kj, kj > qi


def _attn_prompt_bias(first_step, relb_ref, bias_ref):
    @pl.when(first_step)
    def _():
        qi, kj, from_prev = _from_prev_mask()
        dist = jnp.where(from_prev, qi + WINDOW - kj, qi - kj)
        for h in range(N_HEADS):
            rows = slice(h * WINDOW, (h + 1) * WINDOW)
            bias = _bias_from_distance(dist, h, relb_ref)
            bias_ref[1, rows, :] = bias
            bias_ref[0, rows, :] = jnp.where(from_prev, NEG, bias)


def _attn_prompt_blocks(has_prev_block, sink_ref, q_ref, kp_ref, kc_ref, vp_ref, vc_ref, o_ref, bias_ref, lhs_ref,
                        fillers=()):
    _, _, from_prev = _from_prev_mask()
    k_all = jnp.concatenate([kp_ref[...], kc_ref[...]], axis=0)
    v_all = jnp.concatenate([vp_ref[...], vc_ref[...]], axis=0)

    def keys_of(blk):
        return slice(blk * WINDOW, (blk + 2) * WINDOW)

    nblk = PROMPT_BLOCKS_PER_STEP
    fillers = list(fillers)
    per_stage = -(-len(fillers) // (nblk + 2))
    scores, probs = {}, {}
    for stage in range(nblk + 2):
        if stage < nblk:
            q = q_ref[stage * WINDOW:(stage + 1) * WINDOW, :]
            scores[stage] = _block_scores(q, k_all[keys_of(stage)], lhs_ref.at[stage])
        for _ in range(min(per_stage, len(fillers))):
            fillers.pop(0)()
        if 0 <= stage - 1 < nblk:
            blk = stage - 1
            has_prev = has_prev_block.astype(jnp.int32) if blk == 0 else 1
            probs[blk] = _block_softmax(scores.pop(blk), has_prev, from_prev, sink_ref, bias_ref)
        if 0 <= stage - 2 < nblk:
            blk = stage - 2
            o_ref[blk * WINDOW:(blk + 1) * WINDOW, :] = _block_output(
                *probs.pop(blk), v_all[keys_of(blk)]).astype(o_ref.dtype)


def _block_scores(q, kk, lhs_ref):
    q = q.astype(F32) * (HEAD_DIM ** -0.5)
    for h in range(N_HEADS):
        lhs_ref[h * WINDOW:(h + 1) * WINDOW, :] = _head_in_group_columns(q, h).astype(BF16)
    return lax.dot_general(lhs_ref[...], kk, (((1,), (1,)), ((), ())), preferred_element_type=F32)


def _block_softmax(s, has_prev, from_prev, sink_ref, bias_ref):
    probs, sink_terms = [], []
    for h in range(N_HEADS):
        rows = slice(h * WINDOW, (h + 1) * WINDOW)
        sh = jnp.where(from_prev, s[rows, :WINDOW], s[rows, WINDOW:]) + bias_ref[has_prev, rows, :]
        m = jnp.maximum(jnp.max(sh, axis=-1, keepdims=True), sink_ref[h])
        p = jnp.exp(sh - m)
        probs.append(jnp.concatenate([jnp.where(from_prev, p, 0.0), jnp.where(from_prev, 0.0, p)],
                                     axis=1).astype(BF16))
        sink_terms.append(jnp.exp(sink_ref[h] - m))
    return probs, sink_terms


def _block_output(probs, sink_terms, vv):
    assert 2 * HEAD_DIM == LANES and GROUP % 2 == 0
    ones = jnp.ones((vv.shape[0], HEAD_DIM), BF16)
    low_half = lax.broadcasted_iota(jnp.int32, (WINDOW, LANES), 1) < HEAD_DIM
    outs = []
    for g in range(N_KV_HEADS):
        vg = vv[:, g * HEAD_DIM:(g + 1) * HEAD_DIM]
        even_heads = [g * GROUP + r for r in range(0, GROUP, 2)]
        pe = jnp.concatenate([probs[h] for h in even_heads], axis=0)
        po = jnp.concatenate([probs[h + 1] for h in even_heads], axis=0)
        oe = jnp.dot(pe, jnp.concatenate([vg, ones], axis=1), preferred_element_type=F32)
        oo = jnp.dot(po, jnp.concatenate([ones, vg], axis=1), preferred_element_type=F32)
        for k, h in enumerate(even_heads):
            a = oe[k * WINDOW:(k + 1) * WINDOW]
            b = oo[k * WINDOW:(k + 1) * WINDOW]
            out_pair = jnp.where(low_half, a, b)
            sum_pair = pltpu.roll(jnp.where(low_half, b, a), HEAD_DIM, 1)
            sink_pair = jnp.where(low_half, sink_terms[h], sink_terms[h + 1])
            outs.append(out_pair * (1.0 / (sum_pair + sink_pair)))
    return jnp.concatenate(outs, axis=1)


def _in_proj_sample_attn_prompt(x, g, w, side0, side1, zp, rel_bias, sinks, tn, batch, seq):
    tm = x.shape[0]
    n = w.shape[1]
    gate_start = n - 2 * D_MODEL
    assert gate_start % tn == 0 and n % tn == 0
    gate_block, nsteps = gate_start // tn, n // tn
    bps = PROMPT_BLOCKS_PER_STEP
    assert seq % (bps * WINDOW) == 0
    steps_per_seq = seq // (bps * WINDOW)
    attn_steps = batch * steps_per_seq
    assert attn_steps <= nsteps

    def astep(j):
        return jnp.minimum(j, attn_steps - 1)

    def cur(off, width):
        return lambda j: (astep(j), off // width)

    def prev(off, width):
        return lambda j: (astep(j) * bps - jnp.minimum(astep(j) % steps_per_seq, 1), off // width)

    def side_spec(side):
        rows = _side_cast_rows(side.shape[0], nsteps)
        return pl.BlockSpec((rows, side.shape[1]), lambda j: (jnp.minimum(j, side.shape[0] // rows - 1), 0))

    smem = pl.BlockSpec(memory_space=pltpu.SMEM)
    return pl.pallas_call(
        functools.partial(_inproj_attn_kernel, steps_per_seq=steps_per_seq),
        grid=(nsteps,),
        in_specs=[
            pl.BlockSpec((tm, D_MODEL), lambda j: (0, 0)),
            pl.BlockSpec((GAIN_ROWS, D_MODEL), lambda j: (0, 0)),
            pl.BlockSpec((D_MODEL, tn), lambda j: (0, (j + gate_block) % nsteps)),
            side_spec(side0),
            side_spec(side1),
            smem,
            smem,
            pl.BlockSpec((bps * WINDOW, D_ATTN), cur(OFF_Q, D_ATTN)),
            pl.BlockSpec((WINDOW, D_KV), prev(OFF_K, D_KV)),
            pl.BlockSpec((bps * WINDOW, D_KV), cur(OFF_K, D_KV)),
            pl.BlockSpec((WINDOW, D_KV), prev(OFF_V, D_KV)),
            pl.BlockSpec((bps * WINDOW, D_KV), cur(OFF_V, D_KV)),
        ],
        out_specs=[
            pl.BlockSpec((tm, tn), lambda j: (0, j)),
            pl.BlockSpec((bps * WINDOW, D_ATTN), lambda j: (j, 0)),
            side_spec(side0),
            side_spec(side1),
        ],
        out_shape=[
            jax.ShapeDtypeStruct((tm, n), BF16),
            jax.ShapeDtypeStruct((nsteps * bps * WINDOW, D_ATTN), BF16),
            jax.ShapeDtypeStruct(side0.shape, BF16),
            jax.ShapeDtypeStruct(side1.shape, BF16),
        ],
        scratch_shapes=[pltpu.VMEM((tm, D_MODEL), BF16),
                        pltpu.VMEM((D_MODEL, tn), BF16),
                        pltpu.VMEM((2, N_HEADS * WINDOW, WINDOW), F32),
                        pltpu.VMEM((bps, N_HEADS * WINDOW, D_KV), BF16)],
        compiler_params=_params("arbitrary"),
        name="in_proj_sample_attn_prompt",
    )(x, g, w, side0, side1, rel_bias, sinks, zp, zp, zp, zp, zp)


SAMPLE_BATCH_TILE = 16


def _attn_sample_kernel(relb_ref, sink_ref, q_ref, kn_ref, vn_ref, ckt_ref, cvt_ref,
                        o_ref, nk_ref, nv_ref,
                        bias1_ref, bias2_ref, sinkv_ref, z_ref, s_ref, p_ref, oh_ref, *, steps):
    bt, wc, _ = nk_ref.shape
    tile_rows = bt * steps

    @pl.when(pl.program_id(0) == 0)
    def _():
        t = lax.broadcasted_iota(jnp.int32, (steps, wc), 0)
        j = lax.broadcasted_iota(jnp.int32, (steps, wc), 1)
        t2 = lax.broadcasted_iota(jnp.int32, (steps, tile_rows), 0)
        j2 = lax.broadcasted_iota(jnp.int32, (steps, tile_rows), 1)
        own = jnp.where(j2 < steps, t2 - j2, -1)
        for h in range(N_HEADS):
            sl = slice(h * steps, (h + 1) * steps)
            bias1_ref[sl, :] = _bias_from_distance(t + wc - j, h, relb_ref)
            bias2_ref[0, sl, :] = _bias_from_distance(own, h, relb_ref)
            sinkv_ref[sl, :] = jnp.full((steps, LANES), sink_ref[h], F32)
        for b in range(1, bt):
            bias2_ref[b] = pltpu.roll(bias2_ref[0], b * steps, 1)

    kn = kn_ref[...]
    vn = vn_ref[...]
    nk_ref[:, wc - steps:wc, :] = kn.astype(F32).reshape(bt, steps, D_KV)
    nv_ref[:, wc - steps:wc, :] = vn.astype(F32).reshape(bt, steps, D_KV)

    q = q_ref[...].astype(F32) * (HEAD_DIM ** -0.5)
    for h in range(N_HEADS):
        z_ref[:, h * steps:(h + 1) * steps, :] = _head_in_group_columns(q, h).reshape(bt, steps, D_KV)

    for b in range(bt):
        ck = ckt_ref[b].T
        nk_ref[b, 0:wc - steps, :] = ck[steps:, :]
        keys = jnp.concatenate([ck.astype(BF16), kn], axis=0)
        s_ref[b] = lax.dot_general(z_ref[b].astype(BF16), keys, (((1,), (1,)), ((), ())),
                                   preferred_element_type=F32)

    bias = jnp.concatenate([jnp.broadcast_to(bias1_ref[...], (bt,) + bias1_ref.shape), bias2_ref[...]], axis=2)
    s = s_ref[...] + bias
    sink = sinkv_ref[...][None]
    m = jnp.maximum(jnp.max(s, axis=-1, keepdims=True), sink)
    p_ref[...] = jnp.exp(s - jnp.concatenate([m] * (s.shape[-1] // LANES), axis=-1)).astype(BF16)
    sink_term = jnp.exp(sink - m)

    vn_ones = jnp.concatenate([vn, jnp.ones((tile_rows, LANES), BF16)], axis=1)
    for b in range(bt):
        cv = cvt_ref[b].T
        nv_ref[b, 0:wc - steps, :] = cv[steps:, :]
        cv_ones = jnp.concatenate([cv.astype(BF16), jnp.ones((wc, LANES), BF16)], axis=1)
        oh_ref[b] = jnp.dot(p_ref[b], jnp.concatenate([cv_ones, vn_ones], axis=0), preferred_element_type=F32)

    inv = 1.0 / (oh_ref[:, :, D_KV:] + sink_term)
    outs = []
    for h in range(N_HEADS):
        rows = slice(h * steps, (h + 1) * steps)
        outs.append((oh_ref[:, rows, _group_columns(h)] * inv[:, rows, :HEAD_DIM]).reshape(tile_rows, HEAD_DIM))
    o_ref[...] = jnp.concatenate(outs, axis=1).astype(o_ref.dtype)


def _attn_sample(z, cache_kt, cache_vt, rel_bias, sinks, batch, steps):
    wc = cache_kt.shape[2]
    bt = SAMPLE_BATCH_TILE
    rows = bt * steps
    smem = pl.BlockSpec(memory_space=pltpu.SMEM)
    cache_t_spec = pl.BlockSpec((bt, D_KV, wc), lambda i: (i, 0, 0))
    cache_spec = pl.BlockSpec((bt, wc, D_KV), lambda i: (i, 0, 0))
    return pl.pallas_call(
        functools.partial(_attn_sample_kernel, steps=steps),
        grid=(batch // bt,),
        in_specs=[
            smem,
            smem,
            pl.BlockSpec((rows, D_ATTN), lambda i: (i, OFF_Q // D_ATTN)),
            pl.BlockSpec((rows, D_KV), lambda i: (i, OFF_K // D_KV)),
            pl.BlockSpec((rows, D_KV), lambda i: (i, OFF_V // D_KV)),
            cache_t_spec,
            cache_t_spec,
        ],
        out_specs=[
            pl.BlockSpec((rows, D_ATTN), lambda i: (i, 0)),
            cache_spec,
            cache_spec,
        ],
        out_shape=[
            jax.ShapeDtypeStruct((batch * steps, D_ATTN), BF16),
            jax.ShapeDtypeStruct((batch, wc, D_KV), F32),
            jax.ShapeDtypeStruct((batch, wc, D_KV), F32),
        ],
        scratch_shapes=[
            pltpu.VMEM((N_HEADS * steps, wc), F32),
            pltpu.VMEM((bt, N_HEADS * steps, rows), F32),
            pltpu.VMEM((N_HEADS * steps, LANES), F32),
            pltpu.VMEM((bt, N_HEADS * steps, D_KV), F32),
            pltpu.VMEM((bt, N_HEADS * steps, wc + rows), F32),
            pltpu.VMEM((bt, N_HEADS * steps, wc + rows), BF16),
            pltpu.VMEM((bt, N_HEADS * steps, D_KV + LANES), F32),
        ],
        compiler_params=_params("arbitrary"),
        name="attn_sample",
    )(rel_bias, sinks, z, z, z, cache_kt, cache_vt)


CARRY_ROWS = 8


def _gated_conv(b, wc_ref, u, u1, u2):
    conv = wc_ref[0:1, :] * u2 + wc_ref[1:2, :] * u1 + wc_ref[2:3, :] * u
    return (b.astype(F32) * conv).astype(BF16)


def _mix_tail(conv_branch_input, at_ref, gc_ref, ga_ref, x_ref, g_ref, wco_ref, wao_ref, wo_ref, h_ref, hn_ref):
    y_conv = jnp.dot(conv_branch_input(), wco_ref[...], preferred_element_type=F32)
    y_attn = jnp.dot(at_ref[...], wao_ref[...], preferred_element_type=F32)
    merged = (jax.nn.sigmoid(gc_ref[...].astype(F32)) * y_conv
              + jax.nn.sigmoid(ga_ref[...].astype(F32)) * y_attn)
    h = x_ref[...] + jnp.dot(merged.astype(BF16), wo_ref[...], preferred_element_type=F32)
    h_ref[...] = h
    hn_ref[...] = _rms_rows(h, g_ref[0:1, :]).astype(hn_ref.dtype)


def _delayed_from_scratch(us_ref, rows):
    def at(delay):
        return us_ref[CARRY_ROWS - delay + rows.start:CARRY_ROWS - delay + rows.stop, :]
    return at(0), at(1), at(2)


def _mix_prompt_kernel(b_ref, c_ref, hc_ref, wc_ref, at_ref, gc_ref, ga_ref, x_ref, g_ref,
                       wco_ref, wao_ref, wo_ref, h_ref, hn_ref, st_ref, us_ref, *, tiles_per_seq):
    tm = x_ref.shape[0]

    @pl.when(pl.program_id(0) % tiles_per_seq == 0)
    def _():
        us_ref[0:CARRY_ROWS, :] = jnp.zeros((CARRY_ROWS, D_CONV), F32)

    def conv_branch_input():
        u = c_ref[...].astype(F32) * hc_ref[...].astype(F32)
        us_ref[CARRY_ROWS:, :] = u
        st_ref[...] = u[tm - (CONV_WIDTH - 1):, :]
        yc = _gated_conv(b_ref[...], wc_ref, *_delayed_from_scratch(us_ref, slice(0, tm)))
        us_ref[0:CARRY_ROWS, :] = us_ref[tm:tm + CARRY_ROWS, :]
        return yc

    _mix_tail(conv_branch_input, at_ref, gc_ref, ga_ref, x_ref, g_ref, wco_ref, wao_ref, wo_ref, h_ref, hn_ref)


def _mix_sample_kernel(b_ref, c_ref, hc_ref, p1_ref, p2_ref, wc_ref, at_ref, gc_ref, ga_ref, x_ref, g_ref,
                       wco_ref, wao_ref, wo_ref, h_ref, hn_ref, u_ref, us_ref, *, steps):
    tm = x_ref.shape[0]

    def conv_branch_input():
        us_ref[0:CARRY_ROWS, :] = jnp.zeros((CARRY_ROWS, D_CONV), F32)
        u = c_ref[...].astype(F32) * hc_ref[...].astype(F32)
        u_ref[...] = u
        us_ref[CARRY_ROWS:, :] = u
        u0, u1, u2 = _delayed_from_scratch(us_ref, slice(0, tm))
        t = lax.broadcasted_iota(jnp.int32, u0.shape, 0) % steps
        return _gated_conv(b_ref[...], wc_ref, u0, jnp.where(t >= 1, u1, p1_ref[...]),
                           jnp.where(t >= 2, u2, p2_ref[...]))

    _mix_tail(conv_branch_input, at_ref, gc_ref, ga_ref, x_ref, g_ref, wco_ref, wao_ref, wo_ref, h_ref, hn_ref)


def _mix_specs(tm):
    resident = functools.partial(pl.BlockSpec, index_map=lambda i: (0, 0), pipeline_mode=pl.Buffered(1))
    row_spec = pl.BlockSpec((tm, D_MODEL), lambda i: (i, 0))

    def zcol(off, width):
        return pl.BlockSpec((tm, width), lambda i: (i, off // width))

    conv_in = [zcol(OFF_B, D_CONV), zcol(OFF_C, D_CONV), zcol(OFF_H, D_CONV)]
    rest_in = [
        pl.BlockSpec((CONV_WIDTH, D_CONV), lambda i: (0, 0)),
        pl.BlockSpec((tm, D_ATTN), lambda i: (i, 0)),
        zcol(OFF_GC, D_MODEL),
        zcol(OFF_GA, D_MODEL),
        row_spec,
        pl.BlockSpec((GAIN_ROWS, D_MODEL), lambda i: (0, 0)),
        resident((D_CONV, D_MODEL)),
        resident((D_ATTN, D_MODEL)),
        resident((D_MODEL, D_MODEL)),
    ]
    return conv_in, rest_in, row_spec


def _mix_prompt(z, attn, x, w_conv, g_ffn, wco, wao, wo, tm, batch, seq):
    m = x.shape[0]
    assert seq % tm == 0 and tm >= CARRY_ROWS
    tiles_per_seq = seq // tm
    conv_in, rest_in, row_spec = _mix_specs(tm)
    return pl.pallas_call(
        functools.partial(_mix_prompt_kernel, tiles_per_seq=tiles_per_seq),
        grid=(m // tm,),
        in_specs=conv_in + rest_in,
        out_specs=[row_spec, row_spec,
                   pl.BlockSpec((None, CONV_WIDTH - 1, D_CONV), lambda i: (i // tiles_per_seq, 0, 0))],
        out_shape=[jax.ShapeDtypeStruct((m, D_MODEL), F32), jax.ShapeDtypeStruct((m, D_MODEL), BF16),
                   jax.ShapeDtypeStruct((batch, CONV_WIDTH - 1, D_CONV), F32)],
        scratch_shapes=[pltpu.VMEM((tm + CARRY_ROWS, D_CONV), F32)],
        compiler_params=_params("arbitrary"),
        name="mix_prompt",
    )(z, z, z, w_conv, attn, z, z, x, g_ffn, wco, wao, wo)


def _mix_sample(z, prev1, prev2, attn, x, w_conv, g_ffn, wco, wao, wo, tm, steps):
    m = x.shape[0]
    assert tm % steps == 0 and steps >= CONV_WIDTH - 1
    conv_in, rest_in, row_spec = _mix_specs(tm)
    state_spec = pl.BlockSpec((tm, D_CONV), lambda i: (i, 0))
    return pl.pallas_call(
        functools.partial(_mix_sample_kernel, steps=steps),
        grid=(m // tm,),
        in_specs=conv_in + [state_spec, state_spec] + rest_in,
        out_specs=[row_spec, row_spec, state_spec],
        out_shape=[jax.ShapeDtypeStruct((m, D_MODEL), F32), jax.ShapeDtypeStruct((m, D_MODEL), BF16),
                   jax.ShapeDtypeStruct((m, D_CONV), F32)],
        scratch_shapes=[pltpu.VMEM((tm + CARRY_ROWS, D_CONV), F32)],
        compiler_params=_params("arbitrary"),
        name="mix_sample",
    )(z, z, z, prev1, prev2, w_conv, attn, z, z, x, g_ffn, wco, wao, wo)


def _ffn_kernel(hn_ref, h_ref, gf_ref, wg_ref, wu_ref, wd_ref, y_ref, *, final_norm):
    j = pl.program_id(1)

    @pl.when(j == 0)
    def _():
        y_ref[...] = h_ref[...]

    hn = hn_ref[...]
    a = (jax.nn.silu(jnp.dot(hn, wg_ref[...].astype(BF16), preferred_element_type=F32))
         * jnp.dot(hn, wu_ref[...].astype(BF16), preferred_element_type=F32))
    y_ref[...] += jnp.dot(a.astype(BF16), wd_ref[...].astype(BF16), preferred_element_type=F32)

    if final_norm:
        @pl.when(j == pl.num_programs(1) - 1)
        def _():
            def body(r, carry):
                rows = pl.ds(pl.multiple_of(r * NORM_ROWS, NORM_ROWS), NORM_ROWS)
                y_ref[rows, :] = _rms_rows(y_ref[rows, :], gf_ref[0:1, :])
                return carry

            lax.fori_loop(0, y_ref.shape[0] // NORM_ROWS, body, 0)


def _ffn(hn, h, g_final, wg, wu, wd, tm, tf, final_norm):
    m = h.shape[0]
    return pl.pallas_call(
        functools.partial(_ffn_kernel, final_norm=final_norm),
        grid=(m // tm, D_FF // tf),
        in_specs=[
            pl.BlockSpec((tm, D_MODEL), lambda i, j: (i, 0)),
            pl.BlockSpec((tm, D_MODEL), lambda i, j: (i, 0)),
            pl.BlockSpec((GAIN_ROWS, D_MODEL), lambda i, j: (0, 0)),
            pl.BlockSpec((D_MODEL, tf), lambda i, j: (0, j)),
            pl.BlockSpec((D_MODEL, tf), lambda i, j: (0, j)),
            pl.BlockSpec((tf, D_MODEL), lambda i, j: (j, 0)),
        ],
        out_specs=pl.BlockSpec((tm, D_MODEL), lambda i, j: (i, 0)),
        out_shape=jax.ShapeDtypeStruct((m, D_MODEL), F32),
        compiler_params=_params("arbitrary", "arbitrary"),
        name="ffn",
    )(hn, h, g_final, wg, wu, wd)


IN_PROJ_TM_PROMPT = 2048
IN_PROJ_TN = 512
MIX_TM = 256
FFN_TM = 1024
FFN_TF = 256


def kernel(x_prompt, x_sample, cache_k, cache_v, state_conv, rel_bias, w_in, w_conv, w_conv_out, sinks,
           w_attn_out, w_o, g_mix, g_ffn, w_gate, w_up, w_down, g_final):
    depth = w_in.shape[0]
    batch, seq, _ = x_prompt.shape
    dec_batch, steps, _ = x_sample.shape
    wc = cache_k.shape[2]
    assert seq % WINDOW == 0 and wc == WINDOW and dec_batch % SAMPLE_BATCH_TILE == 0

    hp = x_prompt.reshape(batch * seq, D_MODEL)
    hs = x_sample.reshape(dec_batch * steps, D_MODEL)
    g_final2 = _gain_tile(g_final)
    kp_l, vp_l, cp_l, ks_l, vs_l, cs_l = [], [], [], [], [], []
    for l in range(depth):
        final = l == depth - 1
        lw = {
            "w_in": w_in[l],
            "wg": w_gate[l],
            "wu": w_up[l],
            "wd": w_down[l],
            "g_ffn": _gain_tile(g_ffn[l]),
        }
        g_mix_l = _gain_tile(g_mix[l])

        zp, lw["wo"] = _in_proj(hp, g_mix_l, lw["w_in"], w_o[l], IN_PROJ_TM_PROMPT, IN_PROJ_TN)
        zs, attn_p, lw["wco"], lw["wao"] = _in_proj_sample_attn_prompt(
            hs, g_mix_l, lw["w_in"], w_conv_out[l], w_attn_out[l], zp, rel_bias, sinks[l], IN_PROJ_TN, batch, seq)
        hp, hnp, conv_p = _mix_prompt(zp, attn_p, hp, w_conv[l], lw["g_ffn"], lw["wco"], lw["wao"], lw["wo"],
                                      MIX_TM, batch, seq)
        hp = _ffn(hnp, hp, g_final2, lw["wg"], lw["wu"], lw["wd"], FFN_TM, FFN_TF, final)
        kv_p = zp.reshape(batch, seq, D_IN_PROJ)[:, seq - WINDOW:, OFF_K:OFF_V + D_KV].astype(F32)
        kp_l.append(kv_p[..., :D_KV].reshape(batch, WINDOW, N_KV_HEADS, HEAD_DIM))
        vp_l.append(kv_p[..., D_KV:].reshape(batch, WINDOW, N_KV_HEADS, HEAD_DIM))
        cp_l.append(conv_p)

        st = state_conv[l]
        zero = jnp.zeros((dec_batch, steps - 2, D_CONV), F32)
        prev1 = jnp.concatenate([st[:, 1:2], zero, zero[:, :1]], axis=1).reshape(dec_batch * steps, D_CONV)
        prev2 = jnp.concatenate([st, zero], axis=1).reshape(dec_batch * steps, D_CONV)
        cache_kt = jnp.transpose(cache_k[l], (0, 2, 3, 1)).reshape(dec_batch, D_KV, wc)
        cache_vt = jnp.transpose(cache_v[l], (0, 2, 3, 1)).reshape(dec_batch, D_KV, wc)
        attn_s, nk, nv = _attn_sample(zs, cache_kt, cache_vt, rel_bias, sinks[l], dec_batch, steps)
        hs, hns, u_s = _mix_sample(zs, prev1, prev2, attn_s, hs, w_conv[l], lw["g_ffn"], lw["wco"], lw["wao"],
                                   lw["wo"], MIX_TM, steps)
        hs = _ffn(hns, hs, g_final2, lw["wg"], lw["wu"], lw["wd"], FFN_TM, FFN_TF, final)
        ks_l.append(nk.reshape(dec_batch, wc, N_KV_HEADS, HEAD_DIM))
        vs_l.append(nv.reshape(dec_batch, wc, N_KV_HEADS, HEAD_DIM))
        cs_l.append(u_s.reshape(dec_batch, steps, D_CONV)[:, steps - (CONV_WIDTH - 1):])

    return (hp.reshape(batch, seq, D_MODEL), hs.reshape(dec_batch, steps, D_MODEL),
            jnp.stack(kp_l), jnp.stack(vp_l), jnp.stack(cp_l),
            jnp.stack(ks_l), jnp.stack(vs_l), jnp.stack(cs_l))
```

```python
import functools
import math

import numpy as np
import jax
import jax.numpy as jnp
from jax import lax
from jax.experimental import pallas as pl
from jax.experimental.pallas import tpu as pltpu

D_MODEL = 2048
D_CONV = D_MODEL // 2
CONV_WIDTH = 3
HEAD_DIM = 64
N_HEADS = (D_MODEL // 2) // HEAD_DIM
N_KV_HEADS = N_HEADS // 4
GROUP = N_HEADS // N_KV_HEADS
D_ATTN = N_HEADS * HEAD_DIM
D_KV = N_KV_HEADS * HEAD_DIM
WINDOW = 128
NUM_BUCKETS = 32
MAX_DISTANCE = 128
D_FF = -(-8 * D_MODEL // (3 * 256)) * 256
EPS = 1e-6
D_IN_PROJ = 3 * D_CONV + D_ATTN + 2 * D_KV + 2 * D_MODEL

OFF_GC = 0
OFF_GA = D_MODEL
OFF_B = 2 * D_MODEL
OFF_C = OFF_B + D_CONV
OFF_H = OFF_C + D_CONV
OFF_Q = OFF_H + D_CONV
OFF_K = OFF_Q + D_ATTN
OFF_V = OFF_K + D_KV

LANES = 128
NEG = -1e30
F32 = jnp.float32
BF16 = jnp.bfloat16
VMEM_LIMIT = 60 * 1024 * 1024


def _bucket_thresholds():
    max_exact = NUM_BUCKETS // 2
    d = np.arange(MAX_DISTANCE)
    ratio = np.log(np.maximum(d, 1).astype(np.float32) / np.float32(max_exact)) / np.float32(
        math.log(MAX_DISTANCE / max_exact))
    large = max_exact + (ratio * np.float32(NUM_BUCKETS - max_exact)).astype(np.int32)
    large = np.minimum(large, NUM_BUCKETS - 1)
    return [int(np.min(d[(d >= max_exact) & (large >= b)])) for b in range(max_exact + 1, NUM_BUCKETS)]


BUCKET_THRESHOLDS = _bucket_thresholds()


def _params(*semantics):
    return pltpu.CompilerParams(dimension_semantics=semantics, vmem_limit_bytes=VMEM_LIMIT)


GAIN_ROWS = 8


def _gain_tile(g):
    return jnp.broadcast_to(g.reshape(1, D_MODEL), (GAIN_ROWS, D_MODEL))


def _rms_rows(x, g):
    ms = jnp.mean(x * x, axis=-1, keepdims=True)
    return x * lax.rsqrt(ms + EPS) * g


NORM_ROWS = 256


def _inproj_kernel(x_ref, g_ref, w_ref, side_ref, z_ref, side_bf16_ref, xn_ref):
    _inproj_norm(pl.program_id(1) == 0, x_ref, g_ref, xn_ref)
    _inproj_columns(w_ref, z_ref, xn_ref)
    side_bf16_ref[...] = side_ref[...].astype(BF16)


BF16_ROWS = 16


def _side_cast_rows(rows, nsteps):
    per_step = -(-rows // nsteps)
    per_step = -(-per_step // BF16_ROWS) * BF16_ROWS
    assert rows % per_step == 0
    return per_step


def _inproj_norm(first_column_step, x_ref, g_ref, xn_ref):
    @pl.when(first_column_step)
    def _():
        def body(r, carry):
            rows = pl.ds(pl.multiple_of(r * NORM_ROWS, NORM_ROWS), NORM_ROWS)
            xn_ref[rows, :] = _rms_rows(x_ref[rows, :], g_ref[0:1, :]).astype(xn_ref.dtype)
            return carry

        lax.fori_loop(0, x_ref.shape[0] // NORM_ROWS, body, 0)


def _inproj_columns(w_ref, z_ref, xn_ref):
    z_ref[...] = jnp.dot(xn_ref[...], w_ref[...].astype(BF16), preferred_element_type=F32).astype(z_ref.dtype)


def _in_proj(x, g, w, side, tm, tn):
    m = x.shape[0]
    n = w.shape[1]
    gate_start = n - 2 * D_MODEL
    assert gate_start % tn == 0 and n % tn == 0
    gate_block, nblocks = gate_start // tn, n // tn
    side_rows = _side_cast_rows(side.shape[0], (m // tm) * nblocks)
    side_spec = pl.BlockSpec((side_rows, side.shape[1]),
                             lambda i, j: (jnp.minimum(i * nblocks + j, side.shape[0] // side_rows - 1), 0))
    return pl.pallas_call(
        _inproj_kernel,
        grid=(m // tm, nblocks),
        in_specs=[
            pl.BlockSpec((tm, D_MODEL), lambda i, j: (i, 0)),
            pl.BlockSpec((GAIN_ROWS, D_MODEL), lambda i, j: (0, 0)),
            pl.BlockSpec((D_MODEL, tn), lambda i, j: (0, (j + gate_block) % nblocks)),
            side_spec,
        ],
        out_specs=[pl.BlockSpec((tm, tn), lambda i, j: (i, j)), side_spec],
        out_shape=[jax.ShapeDtypeStruct((m, n), BF16), jax.ShapeDtypeStruct(side.shape, BF16)],
        scratch_shapes=[pltpu.VMEM((tm, D_MODEL), BF16)],
        compiler_params=_params("arbitrary", "arbitrary"),
        name="in_proj",
    )(x, g, w, side)


def _bias_from_distance(dist, head, relb_ref):
    valid = (dist >= 0) & (dist < WINDOW)
    d = jnp.clip(dist, 0, MAX_DISTANCE - 1)
    bucket = jnp.minimum(d, NUM_BUCKETS // 2)
    for thr in BUCKET_THRESHOLDS:
        bucket = bucket + (d >= thr).astype(jnp.int32)
    bias = jnp.zeros(dist.shape, F32)
    for b in range(NUM_BUCKETS):
        bias = jnp.where(bucket == b, relb_ref[b, head], bias)
    return jnp.where(valid, bias, NEG)


def _softmax_with_sink(s, sink):
    m = jnp.maximum(jnp.max(s, axis=-1, keepdims=True), sink)
    p = jnp.exp(s - m)
    denom = jnp.sum(p, axis=-1, keepdims=True) + jnp.exp(sink - m)
    return p * (1.0 / denom)


def _head_in_group_columns(q, h):
    half = D_KV // 2
    assert 2 * HEAD_DIM == half and GROUP * HEAD_DIM == D_KV
    g = h // GROUP
    src = q[:, (h // 2) * half:(h // 2 + 1) * half]
    if h % 2 != g % 2:
        src = pltpu.roll(src, HEAD_DIM, 1)
    lane_half = lax.broadcasted_iota(jnp.int32, src.shape, 1) // HEAD_DIM
    placed = jnp.where(lane_half == g % 2, src, 0.0)
    zero = jnp.zeros_like(placed)
    return jnp.concatenate([placed, zero] if g // 2 == 0 else [zero, placed], axis=1)


def _group_columns(h):
    g = h // GROUP
    return slice(g * HEAD_DIM, (g + 1) * HEAD_DIM)


INPROJ_FILL_CHUNKS = 4
PROMPT_BLOCKS_PER_STEP = 4


def _inproj_attn_kernel(x_ref, g_ref, w_ref, side0_ref, side1_ref, relb_ref, sink_ref, q_ref, kp_ref, kc_ref,
                        vp_ref, vc_ref, z_ref, o_ref, side0_bf16_ref, side1_bf16_ref,
                        xn_ref, wb_ref, bias_ref, lhs_ref, *, steps_per_seq):
    j = pl.program_id(0)
    _attn_prompt_bias(j == 0, relb_ref, bias_ref)
    _inproj_norm(j == 0, x_ref, g_ref, xn_ref)
    wb_ref[...] = w_ref[...].astype(BF16)
    side0_bf16_ref[...] = side0_ref[...].astype(BF16)
    side1_bf16_ref[...] = side1_ref[...].astype(BF16)
    chunk = xn_ref.shape[0] // INPROJ_FILL_CHUNKS

    def column_chunk(c):
        rows = slice(c * chunk, (c + 1) * chunk)

        def emit():
            z_ref[rows, :] = jnp.dot(xn_ref[rows, :], wb_ref[...], preferred_element_type=F32).astype(z_ref.dtype)
        return emit

    _attn_prompt_blocks(j % steps_per_seq != 0, sink_ref, q_ref, kp_ref, kc_ref, vp_ref, vc_ref,
                        o_ref, bias_ref, lhs_ref, fillers=[column_chunk(c) for c in range(INPROJ_FILL_CHUNKS)])


def _from_prev_mask():
    qi = lax.broadcasted_iota(jnp.int32, (WINDOW, WINDOW), 0)
    kj = lax.broadcasted_iota(jnp.int32, (WINDOW, WINDOW), 1)
    return qi, kj, kj > qi


def _attn_prompt_bias(first_step, relb_ref, bias_ref):
    @pl.when(first_step)
    def _():
        qi, kj, from_prev = _from_prev_mask()
        dist = jnp.where(from_prev, qi + WINDOW - kj, qi - kj)
        for h in range(N_HEADS):
            rows = slice(h * WINDOW, (h + 1) * WINDOW)
            bias = _bias_from_distance(dist, h, relb_ref)
            bias_ref[1, rows, :] = bias
            bias_ref[0, rows, :] = jnp.where(from_prev, NEG, bias)


def _attn_prompt_blocks(has_prev_block, sink_ref, q_ref, kp_ref, kc_ref, vp_ref, vc_ref, o_ref, bias_ref, lhs_ref,
                        fillers=()):
    _, _, from_prev = _from_prev_mask()
    k_all = jnp.concatenate([kp_ref[...], kc_ref[...]], axis=0)
    v_all = jnp.concatenate([vp_ref[...], vc_ref[...]], axis=0)

    def keys_of(blk):
        return slice(blk * WINDOW, (blk + 2) * WINDOW)

    nblk = PROMPT_BLOCKS_PER_STEP
    fillers = list(fillers)
    per_stage = -(-len(fillers) // (nblk + 2))
    scores, probs = {}, {}
    for stage in range(nblk + 2):
        if stage < nblk:
            q = q_ref[stage * WINDOW:(stage + 1) * WINDOW, :]
            scores[stage] = _block_scores(q, k_all[keys_of(stage)], lhs_ref.at[stage])
        for _ in range(min(per_stage, len(fillers))):
            fillers.pop(0)()
        if 0 <= stage - 1 < nblk:
            blk = stage - 1
            has_prev = has_prev_block.astype(jnp.int32) if blk == 0 else 1
            probs[blk] = _block_softmax(scores.pop(blk), has_prev, from_prev, sink_ref, bias_ref)
        if 0 <= stage - 2 < nblk:
            blk = stage - 2
            o_ref[blk * WINDOW:(blk + 1) * WINDOW, :] = _block_output(
                *probs.pop(blk), v_all[keys_of(blk)]).astype(o_ref.dtype)


def _block_scores(q, kk, lhs_ref):
    q = q.astype(F32) * (HEAD_DIM ** -0.5)
    for h in range(N_HEADS):
        lhs_ref[h * WINDOW:(h + 1) * WINDOW, :] = _head_in_group_columns(q, h).astype(BF16)
    return lax.dot_general(lhs_ref[...], kk, (((1,), (1,)), ((), ())), preferred_element_type=F32)


def _block_softmax(s, has_prev, from_prev, sink_ref, bias_ref):
    probs, sink_terms = [], []
    for h in range(N_HEADS):
        rows = slice(h * WINDOW, (h + 1) * WINDOW)
        sh = jnp.where(from_prev, s[rows, :WINDOW], s[rows, WINDOW:]) + bias_ref[has_prev, rows, :]
        m = jnp.maximum(jnp.max(sh, axis=-1, keepdims=True), sink_ref[h])
        p = jnp.exp(sh - m)
        probs.append(jnp.concatenate([jnp.where(from_prev, p, 0.0), jnp.where(from_prev, 0.0, p)],
                                     axis=1).astype(BF16))
        sink_terms.append(jnp.exp(sink_ref[h] - m))
    return probs, sink_terms


def _block_output(probs, sink_terms, vv):
    assert 2 * HEAD_DIM == LANES and GROUP % 2 == 0
    ones = jnp.ones((vv.shape[0], HEAD_DIM), BF16)
    low_half = lax.broadcasted_iota(jnp.int32, (WINDOW, LANES), 1) < HEAD_DIM
    outs = []
    for g in range(N_KV_HEADS):
        vg = vv[:, g * HEAD_DIM:(g + 1) * HEAD_DIM]
        even_heads = [g * GROUP + r for r in range(0, GROUP, 2)]
        pe = jnp.concatenate([probs[h] for h in even_heads], axis=0)
        po = jnp.concatenate([probs[h + 1] for h in even_heads], axis=0)
        oe = jnp.dot(pe, jnp.concatenate([vg, ones], axis=1), preferred_element_type=F32)
        oo = jnp.dot(po, jnp.concatenate([ones, vg], axis=1), preferred_element_type=F32)
        for k, h in enumerate(even_heads):
            a = oe[k * WINDOW:(k + 1) * WINDOW]
            b = oo[k * WINDOW:(k + 1) * WINDOW]
            out_pair = jnp.where(low_half, a, b)
            sum_pair = pltpu.roll(jnp.where(low_half, b, a), HEAD_DIM, 1)
            sink_pair = jnp.where(low_half, sink_terms[h], sink_terms[h + 1])
            outs.append(out_pair * (1.0 / (sum_pair + sink_pair)))
    return jnp.concatenate(outs, axis=1)


def _in_proj_sample_attn_prompt(x, g, w, side0, side1, zp, rel_bias, sinks, tn, batch, seq):
    tm = x.shape[0]
    n = w.shape[1]
    gate_start = n - 2 * D_MODEL
    assert gate_start % tn == 0 and n % tn == 0
    gate_block, nsteps = gate_start // tn, n // tn
    bps = PROMPT_BLOCKS_PER_STEP
    assert seq % (bps * WINDOW) == 0
    steps_per_seq = seq // (bps * WINDOW)
    attn_steps = batch * steps_per_seq
    assert attn_steps <= nsteps

    def astep(j):
        return jnp.minimum(j, attn_steps - 1)

    def cur(off, width):
        return lambda j: (astep(j), off // width)

    def prev(off, width):
        return lambda j: (astep(j) * bps - jnp.minimum(astep(j) % steps_per_seq, 1), off // width)

    def side_spec(side):
        rows = _side_cast_rows(side.shape[0], nsteps)
        return pl.BlockSpec((rows, side.shape[1]), lambda j: (jnp.minimum(j, side.shape[0] // rows - 1), 0))

    smem = pl.BlockSpec(memory_space=pltpu.SMEM)
    return pl.pallas_call(
        functools.partial(_inproj_attn_kernel, steps_per_seq=steps_per_seq),
        grid=(nsteps,),
        in_specs=[
            pl.BlockSpec((tm, D_MODEL), lambda j: (0, 0)),
            pl.BlockSpec((GAIN_ROWS, D_MODEL), lambda j: (0, 0)),
            pl.BlockSpec((D_MODEL, tn), lambda j: (0, (j + gate_block) % nsteps)),
            side_spec(side0),
            side_spec(side1),
            smem,
            smem,
            pl.BlockSpec((bps * WINDOW, D_ATTN), cur(OFF_Q, D_ATTN)),
            pl.BlockSpec((WINDOW, D_KV), prev(OFF_K, D_KV)),
            pl.BlockSpec((bps * WINDOW, D_KV), cur(OFF_K, D_KV)),
            pl.BlockSpec((WINDOW, D_KV), prev(OFF_V, D_KV)),
            pl.BlockSpec((bps * WINDOW, D_KV), cur(OFF_V, D_KV)),
        ],
        out_specs=[
            pl.BlockSpec((tm, tn), lambda j: (0, j)),
            pl.BlockSpec((bps * WINDOW, D_ATTN), lambda j: (j, 0)),
            side_spec(side0),
            side_spec(side1),
        ],
        out_shape=[
            jax.ShapeDtypeStruct((tm, n), BF16),
            jax.ShapeDtypeStruct((nsteps * bps * WINDOW, D_ATTN), BF16),
            jax.ShapeDtypeStruct(side0.shape, BF16),
            jax.ShapeDtypeStruct(side1.shape, BF16),
        ],
        scratch_shapes=[pltpu.VMEM((tm, D_MODEL), BF16),
                        pltpu.VMEM((D_MODEL, tn), BF16),
                        pltpu.VMEM((2, N_HEADS * WINDOW, WINDOW), F32),
                        pltpu.VMEM((bps, N_HEADS * WINDOW, D_KV), BF16)],
        compiler_params=_params("arbitrary"),
        name="in_proj_sample_attn_prompt",
    )(x, g, w, side0, side1, rel_bias, sinks, zp, zp, zp, zp, zp)


SAMPLE_BATCH_TILE = 16


def _attn_sample_kernel(relb_ref, sink_ref, q_ref, kn_ref, vn_ref, ckt_ref, cvt_ref,
                        o_ref, nk_ref, nv_ref,
                        bias1_ref, bias2_ref, sinkv_ref, z_ref, s_ref, p_ref, oh_ref, *, steps):
    bt, wc, _ = nk_ref.shape
    tile_rows = bt * steps

    @pl.when(pl.program_id(0) == 0)
    def _():
        t = lax.broadcasted_iota(jnp.int32, (steps, wc), 0)
        j = lax.broadcasted_iota(jnp.int32, (steps, wc), 1)
        t2 = lax.broadcasted_iota(jnp.int32, (steps, tile_rows), 0)
        j2 = lax.broadcasted_iota(jnp.int32, (steps, tile_rows), 1)
        own = jnp.where(j2 < steps, t2 - j2, -1)
        for h in range(N_HEADS):
            sl = slice(h * steps, (h + 1) * steps)
            bias1_ref[sl, :] = _bias_from_distance(t + wc - j, h, relb_ref)
            bias2_ref[0, sl, :] = _bias_from_distance(own, h, relb_ref)
            sinkv_ref[sl, :] = jnp.full((steps, LANES), sink_ref[h], F32)
        for b in range(1, bt):
            bias2_ref[b] = pltpu.roll(bias2_ref[0], b * steps, 1)

    kn = kn_ref[...]
    vn = vn_ref[...]
    nk_ref[:, wc - steps:wc, :] = kn.astype(F32).reshape(bt, steps, D_KV)
    nv_ref[:, wc - steps:wc, :] = vn.astype(F32).reshape(bt, steps, D_KV)

    q = q_ref[...].astype(F32) * (HEAD_DIM ** -0.5)
    for h in range(N_HEADS):
        z_ref[:, h * steps:(h + 1) * steps, :] = _head_in_group_columns(q, h).reshape(bt, steps, D_KV)

    for b in range(bt):
        ck = ckt_ref[b].T
        nk_ref[b, 0:wc - steps, :] = ck[steps:, :]
        keys = jnp.concatenate([ck.astype(BF16), kn], axis=0)
        s_ref[b] = lax.dot_general(z_ref[b].astype(BF16), keys, (((1,), (1,)), ((), ())),
                                   preferred_element_type=F32)

    bias = jnp.concatenate([jnp.broadcast_to(bias1_ref[...], (bt,) + bias1_ref.shape), bias2_ref[...]], axis=2)
    s = s_ref[...] + bias
    sink = sinkv_ref[...][None]
    m = jnp.maximum(jnp.max(s, axis=-1, keepdims=True), sink)
    p_ref[...] = jnp.exp(s - jnp.concatenate([m] * (s.shape[-1] // LANES), axis=-1)).astype(BF16)
    sink_term = jnp.exp(sink - m)

    vn_ones = jnp.concatenate([vn, jnp.ones((tile_rows, LANES), BF16)], axis=1)
    for b in range(bt):
        cv = cvt_ref[b].T
        nv_ref[b, 0:wc - steps, :] = cv[steps:, :]
        cv_ones = jnp.concatenate([cv.astype(BF16), jnp.ones((wc, LANES), BF16)], axis=1)
        oh_ref[b] = jnp.dot(p_ref[b], jnp.concatenate([cv_ones, vn_ones], axis=0), preferred_element_type=F32)

    inv = 1.0 / (oh_ref[:, :, D_KV:] + sink_term)
    outs = []
    for h in range(N_HEADS):
        rows = slice(h * steps, (h + 1) * steps)
        outs.append((oh_ref[:, rows, _group_columns(h)] * inv[:, rows, :HEAD_DIM]).reshape(tile_rows, HEAD_DIM))
    o_ref[...] = jnp.concatenate(outs, axis=1).astype(o_ref.dtype)


def _attn_sample(z, cache_kt, cache_vt, rel_bias, sinks, batch, steps):
    wc = cache_kt.shape[2]
    bt = SAMPLE_BATCH_TILE
    rows = bt * steps
    smem = pl.BlockSpec(memory_space=pltpu.SMEM)
    cache_t_spec = pl.BlockSpec((bt, D_KV, wc), lambda i: (i, 0, 0))
    cache_spec = pl.BlockSpec((bt, wc, D_KV), lambda i: (i, 0, 0))
    return pl.pallas_call(
        functools.partial(_attn_sample_kernel, steps=steps),
        grid=(batch // bt,),
        in_specs=[
            smem,
            smem,
            pl.BlockSpec((rows, D_ATTN), lambda i: (i, OFF_Q // D_ATTN)),
            pl.BlockSpec((rows, D_KV), lambda i: (i, OFF_K // D_KV)),
            pl.BlockSpec((rows, D_KV), lambda i: (i, OFF_V // D_KV)),
            cache_t_spec,
            cache_t_spec,
        ],
        out_specs=[
            pl.BlockSpec((rows, D_ATTN), lambda i: (i, 0)),
            cache_spec,
            cache_spec,
        ],
        out_shape=[
            jax.ShapeDtypeStruct((batch * steps, D_ATTN), BF16),
            jax.ShapeDtypeStruct((batch, wc, D_KV), F32),
            jax.ShapeDtypeStruct((batch, wc, D_KV), F32),
        ],
        scratch_shapes=[
            pltpu.VMEM((N_HEADS * steps, wc), F32),
            pltpu.VMEM((bt, N_HEADS * steps, rows), F32),
            pltpu.VMEM((N_HEADS * steps, LANES), F32),
            pltpu.VMEM((bt, N_HEADS * steps, D_KV), F32),
            pltpu.VMEM((bt, N_HEADS * steps, wc + rows), F32),
            pltpu.VMEM((bt, N_HEADS * steps, wc + rows), BF16),
            pltpu.VMEM((bt, N_HEADS * steps, D_KV + LANES), F32),
        ],
        compiler_params=_params("arbitrary"),
        name="attn_sample",
    )(rel_bias, sinks, z, z, z, cache_kt, cache_vt)


CARRY_ROWS = 8


def _gated_conv(b, wc_ref, u, u1, u2):
    conv = wc_ref[0:1, :] * u2 + wc_ref[1:2, :] * u1 + wc_ref[2:3, :] * u
    return (b.astype(F32) * conv).astype(BF16)


def _mix_tail(conv_branch_input, at_ref, gc_ref, ga_ref, x_ref, g_ref, wco_ref, wao_ref, wo_ref, h_ref, hn_ref):
    y_conv = jnp.dot(conv_branch_input(), wco_ref[...], preferred_element_type=F32)
    y_attn = jnp.dot(at_ref[...], wao_ref[...], preferred_element_type=F32)
    merged = (jax.nn.sigmoid(gc_ref[...].astype(F32)) * y_conv
              + jax.nn.sigmoid(ga_ref[...].astype(F32)) * y_attn)
    h = x_ref[...] + jnp.dot(merged.astype(BF16), wo_ref[...], preferred_element_type=F32)
    h_ref[...] = h
    hn_ref[...] = _rms_rows(h, g_ref[0:1, :]).astype(hn_ref.dtype)


def _delayed_from_scratch(us_ref, rows):
    def at(delay):
        return us_ref[CARRY_ROWS - delay + rows.start:CARRY_ROWS - delay + rows.stop, :]
    return at(0), at(1), at(2)


def _mix_prompt_kernel(b_ref, c_ref, hc_ref, wc_ref, at_ref, gc_ref, ga_ref, x_ref, g_ref,
                       wco_ref, wao_ref, wo_ref, h_ref, hn_ref, st_ref, us_ref, *, tiles_per_seq):
    tm = x_ref.shape[0]

    @pl.when(pl.program_id(0) % tiles_per_seq == 0)
    def _():
        us_ref[0:CARRY_ROWS, :] = jnp.zeros((CARRY_ROWS, D_CONV), F32)

    def conv_branch_input():
        u = c_ref[...].astype(F32) * hc_ref[...].astype(F32)
        us_ref[CARRY_ROWS:, :] = u
        st_ref[...] = u[tm - (CONV_WIDTH - 1):, :]
        yc = _gated_conv(b_ref[...], wc_ref, *_delayed_from_scratch(us_ref, slice(0, tm)))
        us_ref[0:CARRY_ROWS, :] = us_ref[tm:tm + CARRY_ROWS, :]
        return yc

    _mix_tail(conv_branch_input, at_ref, gc_ref, ga_ref, x_ref, g_ref, wco_ref, wao_ref, wo_ref, h_ref, hn_ref)


def _mix_sample_kernel(b_ref, c_ref, hc_ref, st_ref, wc_ref, at_ref, gc_ref, ga_ref, x_ref, g_ref,
                       wco_ref, wao_ref, wo_ref, h_ref, hn_ref, new_st_ref, us_ref, *, steps):
    tm = x_ref.shape[0]
    nseq = tm // steps
    keep = CONV_WIDTH - 1

    def conv_branch_input():
        u = c_ref[...].astype(F32) * hc_ref[...].astype(F32)
        us_ref[:, CARRY_ROWS - keep:CARRY_ROWS, :] = st_ref[...]
        us_ref[:, CARRY_ROWS:, :] = u.reshape(nseq, steps, D_CONV)
        new_st_ref[...] = us_ref[:, CARRY_ROWS + steps - keep:, :]

        def delayed(d):
            return us_ref[:, CARRY_ROWS - d:CARRY_ROWS - d + steps, :].reshape(tm, D_CONV)

        return _gated_conv(b_ref[...], wc_ref, u, delayed(1), delayed(2))

    _mix_tail(conv_branch_input, at_ref, gc_ref, ga_ref, x_ref, g_ref, wco_ref, wao_ref, wo_ref, h_ref, hn_ref)


def _mix_specs(tm):
    resident = functools.partial(pl.BlockSpec, index_map=lambda i: (0, 0), pipeline_mode=pl.Buffered(1))
    row_spec = pl.BlockSpec((tm, D_MODEL), lambda i: (i, 0))

    def zcol(off, width):
        return pl.BlockSpec((tm, width), lambda i: (i, off // width))

    conv_in = [zcol(OFF_B, D_CONV), zcol(OFF_C, D_CONV), zcol(OFF_H, D_CONV)]
    rest_in = [
        pl.BlockSpec((CONV_WIDTH, D_CONV), lambda i: (0, 0)),
        pl.BlockSpec((tm, D_ATTN), lambda i: (i, 0)),
        zcol(OFF_GC, D_MODEL),
        zcol(OFF_GA, D_MODEL),
        row_spec,
        pl.BlockSpec((GAIN_ROWS, D_MODEL), lambda i: (0, 0)),
        resident((D_CONV, D_MODEL)),
        resident((D_ATTN, D_MODEL)),
        resident((D_MODEL, D_MODEL)),
    ]
    return conv_in, rest_in, row_spec


def _mix_prompt(z, attn, x, w_conv, g_ffn, wco, wao, wo, tm, batch, seq):
    m = x.shape[0]
    assert seq % tm == 0 and tm >= CARRY_ROWS
    tiles_per_seq = seq // tm
    conv_in, rest_in, row_spec = _mix_specs(tm)
    return pl.pallas_call(
        functools.partial(_mix_prompt_kernel, tiles_per_seq=tiles_per_seq),
        grid=(m // tm,),
        in_specs=conv_in + rest_in,
        out_specs=[row_spec, row_spec,
                   pl.BlockSpec((None, CONV_WIDTH - 1, D_CONV), lambda i: (i // tiles_per_seq, 0, 0))],
        out_shape=[jax.ShapeDtypeStruct((m, D_MODEL), F32), jax.ShapeDtypeStruct((m, D_MODEL), BF16),
                   jax.ShapeDtypeStruct((batch, CONV_WIDTH - 1, D_CONV), F32)],
        scratch_shapes=[pltpu.VMEM((tm + CARRY_ROWS, D_CONV), F32)],
        compiler_params=_params("arbitrary"),
        name="mix_prompt",
    )(z, z, z, w_conv, attn, z, z, x, g_ffn, wco, wao, wo)


def _mix_sample(z, state, attn, x, w_conv, g_ffn, wco, wao, wo, tm, steps):
    m = x.shape[0]
    assert tm % steps == 0 and CONV_WIDTH - 1 <= min(steps, CARRY_ROWS)
    conv_in, rest_in, row_spec = _mix_specs(tm)
    state_spec = pl.BlockSpec((tm // steps, CONV_WIDTH - 1, D_CONV), lambda i: (i, 0, 0))
    return pl.pallas_call(
        functools.partial(_mix_sample_kernel, steps=steps),
        grid=(m // tm,),
        in_specs=conv_in + [state_spec] + rest_in,
        out_specs=[row_spec, row_spec, state_spec],
        out_shape=[jax.ShapeDtypeStruct((m, D_MODEL), F32), jax.ShapeDtypeStruct((m, D_MODEL), BF16),
                   jax.ShapeDtypeStruct(state.shape, F32)],
        scratch_shapes=[pltpu.VMEM((tm // steps, CARRY_ROWS + steps, D_CONV), F32)],
        compiler_params=_params("arbitrary"),
        name="mix_sample",
    )(z, z, z, state, w_conv, attn, z, z, x, g_ffn, wco, wao, wo)


def _ffn_kernel(hn_ref, h_ref, gf_ref, wg_ref, wu_ref, wd_ref, y_ref, *, final_norm):
    j = pl.program_id(1)

    @pl.when(j == 0)
    def _():
        y_ref[...] = h_ref[...]

    hn = hn_ref[...]
    a = (jax.nn.silu(jnp.dot(hn, wg_ref[...].astype(BF16), preferred_element_type=F32))
         * jnp.dot(hn, wu_ref[...].astype(BF16), preferred_element_type=F32))
    y_ref[...] += jnp.dot(a.astype(BF16), wd_ref[...].astype(BF16), preferred_element_type=F32)

    if final_norm:
        @pl.when(j == pl.num_programs(1) - 1)
        def _():
            def body(r, carry):
                rows = pl.ds(pl.multiple_of(r * NORM_ROWS, NORM_ROWS), NORM_ROWS)
                y_ref[rows, :] = _rms_rows(y_ref[rows, :], gf_ref[0:1, :])
                return carry

            lax.fori_loop(0, y_ref.shape[0] // NORM_ROWS, body, 0)


def _ffn(hn, h, g_final, wg, wu, wd, tm, tf, final_norm):
    m = h.shape[0]
    return pl.pallas_call(
        functools.partial(_ffn_kernel, final_norm=final_norm),
        grid=(m // tm, D_FF // tf),
        in_specs=[
            pl.BlockSpec((tm, D_MODEL), lambda i, j: (i, 0)),
            pl.BlockSpec((tm, D_MODEL), lambda i, j: (i, 0)),
            pl.BlockSpec((GAIN_ROWS, D_MODEL), lambda i, j: (0, 0)),
            pl.BlockSpec((D_MODEL, tf), lambda i, j: (0, j)),
            pl.BlockSpec((D_MODEL, tf), lambda i, j: (0, j)),
            pl.BlockSpec((tf, D_MODEL), lambda i, j: (j, 0)),
        ],
        out_specs=pl.BlockSpec((tm, D_MODEL), lambda i, j: (i, 0)),
        out_shape=jax.ShapeDtypeStruct((m, D_MODEL), F32),
        compiler_params=_params("arbitrary", "arbitrary"),
        name="ffn",
    )(hn, h, g_final, wg, wu, wd)


IN_PROJ_TM_PROMPT = 2048
IN_PROJ_TN = 512
MIX_TM = 256
FFN_TM = 1024
FFN_TF = 256


def kernel(x_prompt, x_sample, cache_k, cache_v, state_conv, rel_bias, w_in, w_conv, w_conv_out, sinks,
           w_attn_out, w_o, g_mix, g_ffn, w_gate, w_up, w_down, g_final):
    depth = w_in.shape[0]
    batch, seq, _ = x_prompt.shape
    dec_batch, steps, _ = x_sample.shape
    wc = cache_k.shape[2]
    assert seq % WINDOW == 0 and wc == WINDOW and dec_batch % SAMPLE_BATCH_TILE == 0

    hp = x_prompt.reshape(batch * seq, D_MODEL)
    hs = x_sample.reshape(dec_batch * steps, D_MODEL)
    g_final2 = _gain_tile(g_final)
    kp_l, vp_l, cp_l, ks_l, vs_l, cs_l = [], [], [], [], [], []
    for l in range(depth):
        final = l == depth - 1
        lw = {
            "w_in": w_in[l],
            "wg": w_gate[l],
            "wu": w_up[l],
            "wd": w_down[l],
            "g_ffn": _gain_tile(g_ffn[l]),
        }
        g_mix_l = _gain_tile(g_mix[l])

        zp, lw["wo"] = _in_proj(hp, g_mix_l, lw["w_in"], w_o[l], IN_PROJ_TM_PROMPT, IN_PROJ_TN)
        zs, attn_p, lw["wco"], lw["wao"] = _in_proj_sample_attn_prompt(
            hs, g_mix_l, lw["w_in"], w_conv_out[l], w_attn_out[l], zp, rel_bias, sinks[l], IN_PROJ_TN, batch, seq)
        hp, hnp, conv_p = _mix_prompt(zp, attn_p, hp, w_conv[l], lw["g_ffn"], lw["wco"], lw["wao"], lw["wo"],
                                      MIX_TM, batch, seq)
        hp = _ffn(hnp, hp, g_final2, lw["wg"], lw["wu"], lw["wd"], FFN_TM, FFN_TF, final)
        kv_p = zp.reshape(batch, seq, D_IN_PROJ)[:, seq - WINDOW:, OFF_K:OFF_V + D_KV].astype(F32)
        kp_l.append(kv_p[..., :D_KV].reshape(batch, WINDOW, N_KV_HEADS, HEAD_DIM))
        vp_l.append(kv_p[..., D_KV:].reshape(batch, WINDOW, N_KV_HEADS, HEAD_DIM))
        cp_l.append(conv_p)

        cache_kt = jnp.transpose(cache_k[l], (0, 2, 3, 1)).reshape(dec_batch, D_KV, wc)
        cache_vt = jnp.transpose(cache_v[l], (0, 2, 3, 1)).reshape(dec_batch, D_KV, wc)
        attn_s, nk, nv = _attn_sample(zs, cache_kt, cache_vt, rel_bias, sinks[l], dec_batch, steps)
        hs, hns, conv_s = _mix_sample(zs, state_conv[l], attn_s, hs, w_conv[l], lw["g_ffn"], lw["wco"], lw["wao"],
                                      lw["wo"], MIX_TM, steps)
        hs = _ffn(hns, hs, g_final2, lw["wg"], lw["wu"], lw["wd"], FFN_TM, FFN_TF, final)
        ks_l.append(nk.reshape(dec_batch, wc, N_KV_HEADS, HEAD_DIM))
        vs_l.append(nv.reshape(dec_batch, wc, N_KV_HEADS, HEAD_DIM))
        cs_l.append(conv_s)

    return (hp.reshape(batch, seq, D_MODEL), hs.reshape(dec_batch, steps, D_MODEL),
            jnp.stack(kp_l), jnp.stack(vp_l), jnp.stack(cp_l),
            jnp.stack(ks_l), jnp.stack(vs_l), jnp.stack(cs_l))
```

```python
import functools
import math

import numpy as np
import jax
import jax.numpy as jnp
from jax import lax
from jax.experimental import pallas as pl
from jax.experimental.pallas import tpu as pltpu

D_MODEL = 2048
D_CONV = D_MODEL // 2
CONV_WIDTH = 3
HEAD_DIM = 64
N_HEADS = (D_MODEL // 2) // HEAD_DIM
N_KV_HEADS = N_HEADS // 4
GROUP = N_HEADS // N_KV_HEADS
D_ATTN = N_HEADS * HEAD_DIM
D_KV = N_KV_HEADS * HEAD_DIM
WINDOW = 128
NUM_BUCKETS = 32
MAX_DISTANCE = 128
D_FF = -(-8 * D_MODEL // (3 * 256)) * 256
EPS = 1e-6
D_IN_PROJ = 3 * D_CONV + D_ATTN + 2 * D_KV + 2 * D_MODEL

OFF_GC = 0
OFF_GA = D_MODEL
OFF_B = 2 * D_MODEL
OFF_C = OFF_B + D_CONV
OFF_H = OFF_C + D_CONV
OFF_Q = OFF_H + D_CONV
OFF_K = OFF_Q + D_ATTN
OFF_V = OFF_K + D_KV

LANES = 128
NEG = -1e30
F32 = jnp.float32
BF16 = jnp.bfloat16
VMEM_LIMIT = 60 * 1024 * 1024


def _bucket_thresholds():
    max_exact = NUM_BUCKETS // 2
    d = np.arange(MAX_DISTANCE)
    ratio = np.log(np.maximum(d, 1).astype(np.float32) / np.float32(max_exact)) / np.float32(
        math.log(MAX_DISTANCE / max_exact))
    large = max_exact + (ratio * np.float32(NUM_BUCKETS - max_exact)).astype(np.int32)
    large = np.minimum(large, NUM_BUCKETS - 1)
    return [int(np.min(d[(d >= max_exact) & (large >= b)])) for b in range(max_exact + 1, NUM_BUCKETS)]


BUCKET_THRESHOLDS = _bucket_thresholds()


def _params(*semantics):
    return pltpu.CompilerParams(dimension_semantics=semantics, vmem_limit_bytes=VMEM_LIMIT)


def _rms_rows(x, g):
    ms = jnp.mean(x * x, axis=-1, keepdims=True)
    return x * lax.rsqrt(ms + EPS) * g


NORM_ROWS = 256


def _inproj_kernel(x_ref, g_ref, w_ref, side_ref, z_ref, side_bf16_ref, xn_ref):
    _inproj_norm(pl.program_id(1) == 0, x_ref, g_ref, xn_ref)
    _inproj_columns(w_ref, z_ref, xn_ref)
    side_bf16_ref[...] = side_ref[...].astype(BF16)


BF16_ROWS = 16


def _side_cast_rows(rows, nsteps):
    per_step = -(-rows // nsteps)
    per_step = -(-per_step // BF16_ROWS) * BF16_ROWS
    assert rows % per_step == 0
    return per_step


def _inproj_norm(first_column_step, x_ref, g_ref, xn_ref):
    @pl.when(first_column_step)
    def _():
        def body(r, carry):
            rows = pl.ds(pl.multiple_of(r * NORM_ROWS, NORM_ROWS), NORM_ROWS)
            xn_ref[rows, :] = _rms_rows(x_ref[rows, :], g_ref[0:1, :]).astype(xn_ref.dtype)
            return carry

        lax.fori_loop(0, x_ref.shape[0] // NORM_ROWS, body, 0)


def _inproj_columns(w_ref, z_ref, xn_ref):
    z_ref[...] = jnp.dot(xn_ref[...], w_ref[...].astype(BF16), preferred_element_type=F32).astype(z_ref.dtype)


def _in_proj(x, g, w, side, tm, tn):
    m = x.shape[0]
    n = w.shape[1]
    gate_start = n - 2 * D_MODEL
    assert gate_start % tn == 0 and n % tn == 0
    gate_block, nblocks = gate_start // tn, n // tn
    side_rows = _side_cast_rows(side.shape[0], (m // tm) * nblocks)
    side_spec = pl.BlockSpec((side_rows, side.shape[1]),
                             lambda i, j: (jnp.minimum(i * nblocks + j, side.shape[0] // side_rows - 1), 0))
    return pl.pallas_call(
        _inproj_kernel,
        grid=(m // tm, nblocks),
        in_specs=[
            pl.BlockSpec((tm, D_MODEL), lambda i, j: (i, 0)),
            pl.BlockSpec((1, D_MODEL),lambda i, j: (0, 0)),
            pl.BlockSpec((D_MODEL, tn), lambda i, j: (0, (j + gate_block) % nblocks)),
            side_spec,
        ],
        out_specs=[pl.BlockSpec((tm, tn), lambda i, j: (i, j)), side_spec],
        out_shape=[jax.ShapeDtypeStruct((m, n), BF16), jax.ShapeDtypeStruct(side.shape, BF16)],
        scratch_shapes=[pltpu.VMEM((tm, D_MODEL), BF16)],
        compiler_params=_params("arbitrary", "arbitrary"),
        name="in_proj",
    )(x, g, w, side)


def _bias_from_distance(dist, head, relb_ref):
    valid = (dist >= 0) & (dist < WINDOW)
    d = jnp.clip(dist, 0, MAX_DISTANCE - 1)
    bucket = jnp.minimum(d, NUM_BUCKETS // 2)
    for thr in BUCKET_THRESHOLDS:
        bucket = bucket + (d >= thr).astype(jnp.int32)
    bias = jnp.zeros(dist.shape, F32)
    for b in range(NUM_BUCKETS):
        bias = jnp.where(bucket == b, relb_ref[b, head], bias)
    return jnp.where(valid, bias, NEG)


def _head_in_group_columns(q, h):
    half = D_KV // 2
    assert 2 * HEAD_DIM == half and GROUP * HEAD_DIM == D_KV
    g = h // GROUP
    src = q[:, (h // 2) * half:(h // 2 + 1) * half]
    if h % 2 != g % 2:
        src = pltpu.roll(src, HEAD_DIM, 1)
    lane_half = lax.broadcasted_iota(jnp.int32, src.shape, 1) // HEAD_DIM
    placed = jnp.where(lane_half == g % 2, src, 0.0)
    zero = jnp.zeros_like(placed)
    return jnp.concatenate([placed, zero] if g // 2 == 0 else [zero, placed], axis=1)


def _group_columns(h):
    g = h // GROUP
    return slice(g * HEAD_DIM, (g + 1) * HEAD_DIM)


INPROJ_FILL_CHUNKS = 4
PROMPT_BLOCKS_PER_STEP = 4


def _inproj_attn_kernel(x_ref, g_ref, w_ref, side0_ref, side1_ref, relb_ref, sink_ref, q_ref, kp_ref, kc_ref,
                        vp_ref, vc_ref, z_ref, o_ref, side0_bf16_ref, side1_bf16_ref,
                        xn_ref, wb_ref, bias_ref, lhs_ref, *, steps_per_seq):
    j = pl.program_id(0)
    _attn_prompt_bias(j == 0, relb_ref, bias_ref)
    _inproj_norm(j == 0, x_ref, g_ref, xn_ref)
    wb_ref[...] = w_ref[...].astype(BF16)
    side0_bf16_ref[...] = side0_ref[...].astype(BF16)
    side1_bf16_ref[...] = side1_ref[...].astype(BF16)
    chunk = xn_ref.shape[0] // INPROJ_FILL_CHUNKS

    def column_chunk(c):
        rows = slice(c * chunk, (c + 1) * chunk)

        def emit():
            z_ref[rows, :] = jnp.dot(xn_ref[rows, :], wb_ref[...], preferred_element_type=F32).astype(z_ref.dtype)
        return emit

    _attn_prompt_blocks(j % steps_per_seq != 0, sink_ref, q_ref, kp_ref, kc_ref, vp_ref, vc_ref,
                        o_ref, bias_ref, lhs_ref, fillers=[column_chunk(c) for c in range(INPROJ_FILL_CHUNKS)])


def _from_prev_mask():
    qi = lax.broadcasted_iota(jnp.int32, (WINDOW, WINDOW), 0)
    kj = lax.broadcasted_iota(jnp.int32, (WINDOW, WINDOW), 1)
    return qi, kj, kj > qi


def _attn_prompt_bias(first_step, relb_ref, bias_ref):
    @pl.when(first_step)
    def _():
        qi, kj, from_prev = _from_prev_mask()
        dist = jnp.where(from_prev, qi + WINDOW - kj, qi - kj)
        for h in range(N_HEADS):
            rows = slice(h * WINDOW, (h + 1) * WINDOW)
            bias = _bias_from_distance(dist, h, relb_ref)
            bias_ref[1, rows, :] = bias
            bias_ref[0, rows, :] = jnp.where(from_prev, NEG, bias)


def _attn_prompt_blocks(has_prev_block, sink_ref, q_ref, kp_ref, kc_ref, vp_ref, vc_ref, o_ref, bias_ref, lhs_ref,
                        fillers=()):
    _, _, from_prev = _from_prev_mask()
    k_all = jnp.concatenate([kp_ref[...], kc_ref[...]], axis=0)
    v_all = jnp.concatenate([vp_ref[...], vc_ref[...]], axis=0)

    def keys_of(blk):
        return slice(blk * WINDOW, (blk + 2) * WINDOW)

    nblk = PROMPT_BLOCKS_PER_STEP
    fillers = list(fillers)
    per_stage = -(-len(fillers) // (nblk + 2))
    scores, probs = {}, {}
    for stage in range(nblk + 2):
        if stage < nblk:
            q = q_ref[stage * WINDOW:(stage + 1) * WINDOW, :]
            scores[stage] = _block_scores(q, k_all[keys_of(stage)], lhs_ref.at[stage])
        for _ in range(min(per_stage, len(fillers))):
            fillers.pop(0)()
        if 0 <= stage - 1 < nblk:
            blk = stage - 1
            has_prev = has_prev_block.astype(jnp.int32) if blk == 0 else 1
            probs[blk] = _block_softmax(scores.pop(blk), has_prev, from_prev, sink_ref, bias_ref)
        if 0 <= stage - 2 < nblk:
            blk = stage - 2
            o_ref[blk * WINDOW:(blk + 1) * WINDOW, :] = _block_output(
                *probs.pop(blk), v_all[keys_of(blk)]).astype(o_ref.dtype)


def _block_scores(q, kk, lhs_ref):
    q = q.astype(F32) * (HEAD_DIM ** -0.5)
    for h in range(N_HEADS):
        lhs_ref[h * WINDOW:(h + 1) * WINDOW, :] = _head_in_group_columns(q, h).astype(BF16)
    return lax.dot_general(lhs_ref[...], kk, (((1,), (1,)), ((), ())), preferred_element_type=F32)


def _block_softmax(s, has_prev, from_prev, sink_ref, bias_ref):
    probs, sink_terms = [], []
    for h in range(N_HEADS):
        rows = slice(h * WINDOW, (h + 1) * WINDOW)
        sh = jnp.where(from_prev, s[rows, :WINDOW], s[rows, WINDOW:]) + bias_ref[has_prev, rows, :]
        m = jnp.maximum(jnp.max(sh, axis=-1, keepdims=True), sink_ref[h])
        p = jnp.exp(sh - m)
        probs.append(jnp.concatenate([jnp.where(from_prev, p, 0.0), jnp.where(from_prev, 0.0, p)],
                                     axis=1).astype(BF16))
        sink_terms.append(jnp.exp(sink_ref[h] - m))
    return probs, sink_terms


def _block_output(probs, sink_terms, vv):
    assert 2 * HEAD_DIM == LANES and GROUP % 2 == 0
    ones = jnp.ones((vv.shape[0], HEAD_DIM), BF16)
    low_half = lax.broadcasted_iota(jnp.int32, (WINDOW, LANES), 1) < HEAD_DIM
    outs = []
    for g in range(N_KV_HEADS):
        vg = vv[:, g * HEAD_DIM:(g + 1) * HEAD_DIM]
        even_heads = [g * GROUP + r for r in range(0, GROUP, 2)]
        pe = jnp.concatenate([probs[h] for h in even_heads], axis=0)
        po = jnp.concatenate([probs[h + 1] for h in even_heads], axis=0)
        oe = jnp.dot(pe, jnp.concatenate([vg, ones], axis=1), preferred_element_type=F32)
        oo = jnp.dot(po, jnp.concatenate([ones, vg], axis=1), preferred_element_type=F32)
        for k, h in enumerate(even_heads):
            a = oe[k * WINDOW:(k + 1) * WINDOW]
            b = oo[k * WINDOW:(k + 1) * WINDOW]
            out_pair = jnp.where(low_half, a, b)
            sum_pair = pltpu.roll(jnp.where(low_half, b, a), HEAD_DIM, 1)
            sink_pair = jnp.where(low_half, sink_terms[h], sink_terms[h + 1])
            outs.append(out_pair * (1.0 / (sum_pair + sink_pair)))
    return jnp.concatenate(outs, axis=1)


def _in_proj_sample_attn_prompt(x, g, w, side0, side1, zp, rel_bias, sinks, tn, batch, seq):
    tm = x.shape[0]
    n = w.shape[1]
    gate_start = n - 2 * D_MODEL
    assert gate_start % tn == 0 and n % tn == 0
    gate_block, nsteps = gate_start // tn, n // tn
    bps = PROMPT_BLOCKS_PER_STEP
    assert seq % (bps * WINDOW) == 0
    steps_per_seq = seq // (bps * WINDOW)
    attn_steps = batch * steps_per_seq
    assert attn_steps <= nsteps

    def astep(j):
        return jnp.minimum(j, attn_steps - 1)

    def cur(off, width):
        return lambda j: (astep(j), off // width)

    def prev(off, width):
        return lambda j: (astep(j) * bps - jnp.minimum(astep(j) % steps_per_seq, 1), off // width)

    def side_spec(side):
        rows = _side_cast_rows(side.shape[0], nsteps)
        return pl.BlockSpec((rows, side.shape[1]), lambda j: (jnp.minimum(j, side.shape[0] // rows - 1), 0))

    smem = pl.BlockSpec(memory_space=pltpu.SMEM)
    return pl.pallas_call(
        functools.partial(_inproj_attn_kernel, steps_per_seq=steps_per_seq),
        grid=(nsteps,),
        in_specs=[
            pl.BlockSpec((tm, D_MODEL), lambda j: (0, 0)),
            pl.BlockSpec((1, D_MODEL),lambda j: (0, 0)),
            pl.BlockSpec((D_MODEL, tn), lambda j: (0, (j + gate_block) % nsteps)),
            side_spec(side0),
            side_spec(side1),
            smem,
            smem,
            pl.BlockSpec((bps * WINDOW, D_ATTN), cur(OFF_Q, D_ATTN)),
            pl.BlockSpec((WINDOW, D_KV), prev(OFF_K, D_KV)),
            pl.BlockSpec((bps * WINDOW, D_KV), cur(OFF_K, D_KV)),
            pl.BlockSpec((WINDOW, D_KV), prev(OFF_V, D_KV)),
            pl.BlockSpec((bps * WINDOW, D_KV), cur(OFF_V, D_KV)),
        ],
        out_specs=[
            pl.BlockSpec((tm, tn), lambda j: (0, j)),
            pl.BlockSpec((bps * WINDOW, D_ATTN), lambda j: (j, 0)),
            side_spec(side0),
            side_spec(side1),
        ],
        out_shape=[
            jax.ShapeDtypeStruct((tm, n), BF16),
            jax.ShapeDtypeStruct((nsteps * bps * WINDOW, D_ATTN), BF16),
            jax.ShapeDtypeStruct(side0.shape, BF16),
            jax.ShapeDtypeStruct(side1.shape, BF16),
        ],
        scratch_shapes=[pltpu.VMEM((tm, D_MODEL), BF16),
                        pltpu.VMEM((D_MODEL, tn), BF16),
                        pltpu.VMEM((2, N_HEADS * WINDOW, WINDOW), F32),
                        pltpu.VMEM((bps, N_HEADS * WINDOW, D_KV), BF16)],
        compiler_params=_params("arbitrary"),
        name="in_proj_sample_attn_prompt",
    )(x, g, w, side0, side1, rel_bias, sinks, zp, zp, zp, zp, zp)


SAMPLE_BATCH_TILE = 16


def _attn_sample_kernel(relb_ref, sink_ref, q_ref, kn_ref, vn_ref, ckt_ref, cvt_ref,
                        o_ref, nk_ref, nv_ref,
                        bias1_ref, bias2_ref, sinkv_ref, z_ref, s_ref, p_ref, oh_ref, *, steps):
    bt, wc, _ = nk_ref.shape
    tile_rows = bt * steps

    @pl.when(pl.program_id(0) == 0)
    def _():
        t = lax.broadcasted_iota(jnp.int32, (steps, wc), 0)
        j = lax.broadcasted_iota(jnp.int32, (steps, wc), 1)
        t2 = lax.broadcasted_iota(jnp.int32, (steps, tile_rows), 0)
        j2 = lax.broadcasted_iota(jnp.int32, (steps, tile_rows), 1)
        own = jnp.where(j2 < steps, t2 - j2, -1)
        for h in range(N_HEADS):
            sl = slice(h * steps, (h + 1) * steps)
            bias1_ref[sl, :] = _bias_from_distance(t + wc - j, h, relb_ref)
            bias2_ref[0, sl, :] = _bias_from_distance(own, h, relb_ref)
            sinkv_ref[sl, :] = jnp.full((steps, LANES), sink_ref[h], F32)
        for b in range(1, bt):
            bias2_ref[b] = pltpu.roll(bias2_ref[0], b * steps, 1)

    kn = kn_ref[...]
    vn = vn_ref[...]
    nk_ref[:, wc - steps:wc, :] = kn.astype(F32).reshape(bt, steps, D_KV)
    nv_ref[:, wc - steps:wc, :] = vn.astype(F32).reshape(bt, steps, D_KV)

    q = q_ref[...].astype(F32) * (HEAD_DIM ** -0.5)
    for h in range(N_HEADS):
        z_ref[:, h * steps:(h + 1) * steps, :] = _head_in_group_columns(q, h).reshape(bt, steps, D_KV)

    for b in range(bt):
        ck = ckt_ref[b].T
        nk_ref[b, 0:wc - steps, :] = ck[steps:, :]
        keys = jnp.concatenate([ck.astype(BF16), kn], axis=0)
        s_ref[b] = lax.dot_general(z_ref[b].astype(BF16), keys, (((1,), (1,)), ((), ())),
                                   preferred_element_type=F32)

    bias = jnp.concatenate([jnp.broadcast_to(bias1_ref[...], (bt,) + bias1_ref.shape), bias2_ref[...]], axis=2)
    s = s_ref[...] + bias
    sink = sinkv_ref[...][None]
    m = jnp.maximum(jnp.max(s, axis=-1, keepdims=True), sink)
    p_ref[...] = jnp.exp(s - jnp.concatenate([m] * (s.shape[-1] // LANES), axis=-1)).astype(BF16)
    sink_term = jnp.exp(sink - m)

    vn_ones = jnp.concatenate([vn, jnp.ones((tile_rows, LANES), BF16)], axis=1)
    for b in range(bt):
        cv = cvt_ref[b].T
        nv_ref[b, 0:wc - steps, :] = cv[steps:, :]
        cv_ones = jnp.concatenate([cv.astype(BF16), jnp.ones((wc, LANES), BF16)], axis=1)
        oh_ref[b] = jnp.dot(p_ref[b], jnp.concatenate([cv_ones, vn_ones], axis=0), preferred_element_type=F32)

    inv = 1.0 / (oh_ref[:, :, D_KV:] + sink_term)
    outs = []
    for h in range(N_HEADS):
        rows = slice(h * steps, (h + 1) * steps)
        outs.append((oh_ref[:, rows, _group_columns(h)] * inv[:, rows, :HEAD_DIM]).reshape(tile_rows, HEAD_DIM))
    o_ref[...] = jnp.concatenate(outs, axis=1).astype(o_ref.dtype)


def _attn_sample(z, cache_kt, cache_vt, rel_bias, sinks, batch, steps):
    wc = cache_kt.shape[2]
    bt = SAMPLE_BATCH_TILE
    rows = bt * steps
    smem = pl.BlockSpec(memory_space=pltpu.SMEM)
    cache_t_spec = pl.BlockSpec((bt, D_KV, wc), lambda i: (i, 0, 0))
    cache_spec = pl.BlockSpec((bt, wc, D_KV), lambda i: (i, 0, 0))
    return pl.pallas_call(
        functools.partial(_attn_sample_kernel, steps=steps),
        grid=(batch // bt,),
        in_specs=[
            smem,
            smem,
            pl.BlockSpec((rows, D_ATTN), lambda i: (i, OFF_Q // D_ATTN)),
            pl.BlockSpec((rows, D_KV), lambda i: (i, OFF_K // D_KV)),
            pl.BlockSpec((rows, D_KV), lambda i: (i, OFF_V // D_KV)),
            cache_t_spec,
            cache_t_spec,
        ],
        out_specs=[
            pl.BlockSpec((rows, D_ATTN), lambda i: (i, 0)),
            cache_spec,
            cache_spec,
        ],
        out_shape=[
            jax.ShapeDtypeStruct((batch * steps, D_ATTN), BF16),
            jax.ShapeDtypeStruct((batch, wc, D_KV), F32),
            jax.ShapeDtypeStruct((batch, wc, D_KV), F32),
        ],
        scratch_shapes=[
            pltpu.VMEM((N_HEADS * steps, wc), F32),
            pltpu.VMEM((bt, N_HEADS * steps, rows), F32),
            pltpu.VMEM((N_HEADS * steps, LANES), F32),
            pltpu.VMEM((bt, N_HEADS * steps, D_KV), F32),
            pltpu.VMEM((bt, N_HEADS * steps, wc + rows), F32),
            pltpu.VMEM((bt, N_HEADS * steps, wc + rows), BF16),
            pltpu.VMEM((bt, N_HEADS * steps, D_KV + LANES), F32),
        ],
        compiler_params=_params("arbitrary"),
        name="attn_sample",
    )(rel_bias, sinks, z, z, z, cache_kt, cache_vt)


CARRY_ROWS = 8


def _gated_conv(b, wc_ref, u, u1, u2):
    conv = wc_ref[0:1, :] * u2 + wc_ref[1:2, :] * u1 + wc_ref[2:3, :] * u
    return (b.astype(F32) * conv).astype(BF16)


def _mix_tail(conv_branch_input, at_ref, gc_ref, ga_ref, x_ref, g_ref, wco_ref, wao_ref, wo_ref, h_ref, hn_ref):
    y_conv = jnp.dot(conv_branch_input(), wco_ref[...], preferred_element_type=F32)
    y_attn = jnp.dot(at_ref[...], wao_ref[...], preferred_element_type=F32)
    merged = (jax.nn.sigmoid(gc_ref[...].astype(F32)) * y_conv
              + jax.nn.sigmoid(ga_ref[...].astype(F32)) * y_attn)
    h = x_ref[...] + jnp.dot(merged.astype(BF16), wo_ref[...], preferred_element_type=F32)
    h_ref[...] = h
    hn_ref[...] = _rms_rows(h, g_ref[0:1, :]).astype(hn_ref.dtype)


def _delayed_from_scratch(us_ref, rows):
    def at(delay):
        return us_ref[CARRY_ROWS - delay + rows.start:CARRY_ROWS - delay + rows.stop, :]
    return at(0), at(1), at(2)


def _mix_prompt_kernel(b_ref, c_ref, hc_ref, wc_ref, at_ref, gc_ref, ga_ref, x_ref, g_ref,
                       wco_ref, wao_ref, wo_ref, h_ref, hn_ref, st_ref, us_ref, *, tiles_per_seq):
    tm = x_ref.shape[0]

    @pl.when(pl.program_id(0) % tiles_per_seq == 0)
    def _():
        us_ref[0:CARRY_ROWS, :] = jnp.zeros((CARRY_ROWS, D_CONV), F32)

    def conv_branch_input():
        u = c_ref[...].astype(F32) * hc_ref[...].astype(F32)
        us_ref[CARRY_ROWS:, :] = u
        st_ref[...] = u[tm - (CONV_WIDTH - 1):, :]
        yc = _gated_conv(b_ref[...], wc_ref, *_delayed_from_scratch(us_ref, slice(0, tm)))
        us_ref[0:CARRY_ROWS, :] = us_ref[tm:tm + CARRY_ROWS, :]
        return yc

    _mix_tail(conv_branch_input, at_ref, gc_ref, ga_ref, x_ref, g_ref, wco_ref, wao_ref, wo_ref, h_ref, hn_ref)


def _mix_sample_kernel(b_ref, c_ref, hc_ref, st_ref, wc_ref, at_ref, gc_ref, ga_ref, x_ref, g_ref,
                       wco_ref, wao_ref, wo_ref, h_ref, hn_ref, new_st_ref, us_ref, *, steps):
    tm = x_ref.shape[0]
    nseq = tm // steps
    keep = CONV_WIDTH - 1

    def conv_branch_input():
        u = c_ref[...].astype(F32) * hc_ref[...].astype(F32)
        us_ref[:, CARRY_ROWS - keep:CARRY_ROWS, :] = st_ref[...]
        us_ref[:, CARRY_ROWS:, :] = u.reshape(nseq, steps, D_CONV)
        new_st_ref[...] = us_ref[:, CARRY_ROWS + steps - keep:, :]

        def delayed(d):
            return us_ref[:, CARRY_ROWS - d:CARRY_ROWS - d + steps, :].reshape(tm, D_CONV)

        return _gated_conv(b_ref[...], wc_ref, u, delayed(1), delayed(2))

    _mix_tail(conv_branch_input, at_ref, gc_ref, ga_ref, x_ref, g_ref, wco_ref, wao_ref, wo_ref, h_ref, hn_ref)


def _mix_specs(tm):
    resident = functools.partial(pl.BlockSpec, index_map=lambda i: (0, 0), pipeline_mode=pl.Buffered(1))
    row_spec = pl.BlockSpec((tm, D_MODEL), lambda i: (i, 0))

    def zcol(off, width):
        return pl.BlockSpec((tm, width), lambda i: (i, off // width))

    conv_in = [zcol(OFF_B, D_CONV), zcol(OFF_C, D_CONV), zcol(OFF_H, D_CONV)]
    rest_in = [
        pl.BlockSpec((CONV_WIDTH, D_CONV), lambda i: (0, 0)),
        pl.BlockSpec((tm, D_ATTN), lambda i: (i, 0)),
        zcol(OFF_GC, D_MODEL),
        zcol(OFF_GA, D_MODEL),
        row_spec,
        pl.BlockSpec((1, D_MODEL),lambda i: (0, 0)),
        resident((D_CONV, D_MODEL)),
        resident((D_ATTN, D_MODEL)),
        resident((D_MODEL, D_MODEL)),
    ]
    return conv_in, rest_in, row_spec


def _mix_prompt(z, attn, x, w_conv, g_ffn, wco, wao, wo, tm, batch, seq):
    m = x.shape[0]
    assert seq % tm == 0 and tm >= CARRY_ROWS
    tiles_per_seq = seq // tm
    conv_in, rest_in, row_spec = _mix_specs(tm)
    return pl.pallas_call(
        functools.partial(_mix_prompt_kernel, tiles_per_seq=tiles_per_seq),
        grid=(m // tm,),
        in_specs=conv_in + rest_in,
        out_specs=[row_spec, row_spec,
                   pl.BlockSpec((None, CONV_WIDTH - 1, D_CONV), lambda i: (i // tiles_per_seq, 0, 0))],
        out_shape=[jax.ShapeDtypeStruct((m, D_MODEL), F32), jax.ShapeDtypeStruct((m, D_MODEL), BF16),
                   jax.ShapeDtypeStruct((batch, CONV_WIDTH - 1, D_CONV), F32)],
        scratch_shapes=[pltpu.VMEM((tm + CARRY_ROWS, D_CONV), F32)],
        compiler_params=_params("arbitrary"),
        name="mix_prompt",
    )(z, z, z, w_conv, attn, z, z, x, g_ffn, wco, wao, wo)


def _mix_sample(z, state, attn, x, w_conv, g_ffn, wco, wao, wo, tm, steps):
    m = x.shape[0]
    assert tm % steps == 0 and CONV_WIDTH - 1 <= min(steps, CARRY_ROWS)
    conv_in, rest_in, row_spec = _mix_specs(tm)
    state_spec = pl.BlockSpec((tm // steps, CONV_WIDTH - 1, D_CONV), lambda i: (i, 0, 0))
    return pl.pallas_call(
        functools.partial(_mix_sample_kernel, steps=steps),
        grid=(m // tm,),
        in_specs=conv_in + [state_spec] + rest_in,
        out_specs=[row_spec, row_spec, state_spec],
        out_shape=[jax.ShapeDtypeStruct((m, D_MODEL), F32), jax.ShapeDtypeStruct((m, D_MODEL), BF16),
                   jax.ShapeDtypeStruct(state.shape, F32)],
        scratch_shapes=[pltpu.VMEM((tm // steps, CARRY_ROWS + steps, D_CONV), F32)],
        compiler_params=_params("arbitrary"),
        name="mix_sample",
    )(z, z, z, state, w_conv, attn, z, z, x, g_ffn, wco, wao, wo)


def _ffn_kernel(hn_ref, h_ref, gf_ref, wg_ref, wu_ref, wd_ref, y_ref, *, final_norm):
    j = pl.program_id(1)

    @pl.when(j == 0)
    def _():
        y_ref[...] = h_ref[...]

    hn = hn_ref[...]
    a = (jax.nn.silu(jnp.dot(hn, wg_ref[...].astype(BF16), preferred_element_type=F32))
         * jnp.dot(hn, wu_ref[...].astype(BF16), preferred_element_type=F32))
    y_ref[...] += jnp.dot(a.astype(BF16), wd_ref[...].astype(BF16), preferred_element_type=F32)

    if final_norm:
        @pl.when(j == pl.num_programs(1) - 1)
        def _():
            def body(r, carry):
                rows = pl.ds(pl.multiple_of(r * NORM_ROWS, NORM_ROWS), NORM_ROWS)
                y_ref[rows, :] = _rms_rows(y_ref[rows, :], gf_ref[0:1, :])
                return carry

            lax.fori_loop(0, y_ref.shape[0] // NORM_ROWS, body, 0)


def _ffn(hn, h, g_final, wg, wu, wd, tm, tf, final_norm):
    m = h.shape[0]
    return pl.pallas_call(
        functools.partial(_ffn_kernel, final_norm=final_norm),
        grid=(m // tm, D_FF // tf),
        in_specs=[
            pl.BlockSpec((tm, D_MODEL), lambda i, j: (i, 0)),
            pl.BlockSpec((tm, D_MODEL), lambda i, j: (i, 0)),
            pl.BlockSpec((1, D_MODEL),lambda i, j: (0, 0)),
            pl.BlockSpec((D_MODEL, tf), lambda i, j: (0, j)),
            pl.BlockSpec((D_MODEL, tf), lambda i, j: (0, j)),
            pl.BlockSpec((tf, D_MODEL), lambda i, j: (j, 0)),
        ],
        out_specs=pl.BlockSpec((tm, D_MODEL), lambda i, j: (i, 0)),
        out_shape=jax.ShapeDtypeStruct((m, D_MODEL), F32),
        compiler_params=_params("arbitrary", "arbitrary"),
        name="ffn",
    )(hn, h, g_final, wg, wu, wd)


IN_PROJ_TM_PROMPT = 2048
IN_PROJ_TN = 512
MIX_TM = 256
FFN_TM = 1024
FFN_TF = 256


def kernel(x_prompt, x_sample, cache_k, cache_v, state_conv, rel_bias, w_in, w_conv, w_conv_out, sinks,
           w_attn_out, w_o, g_mix, g_ffn, w_gate, w_up, w_down, g_final):
    depth = w_in.shape[0]
    batch, seq, _ = x_prompt.shape
    dec_batch, steps, _ = x_sample.shape
    wc = cache_k.shape[2]
    assert seq % WINDOW == 0 and wc == WINDOW and dec_batch % SAMPLE_BATCH_TILE == 0

    hp = x_prompt.reshape(batch * seq, D_MODEL)
    hs = x_sample.reshape(dec_batch * steps, D_MODEL)
    g_final2 = g_final.reshape(1, D_MODEL)
    kp_l, vp_l, cp_l, ks_l, vs_l, cs_l = [], [], [], [], [], []
    for l in range(depth):
        final = l == depth - 1
        lw = {
            "w_in": w_in[l],
            "wg": w_gate[l],
            "wu": w_up[l],
            "wd": w_down[l],
            "g_ffn": g_ffn[l].reshape(1, D_MODEL),
        }
        g_mix_l = g_mix[l].reshape(1, D_MODEL)

        zp, lw["wo"] = _in_proj(hp, g_mix_l, lw["w_in"], w_o[l], IN_PROJ_TM_PROMPT, IN_PROJ_TN)
        zs, attn_p, lw["wco"], lw["wao"] = _in_proj_sample_attn_prompt(
            hs, g_mix_l, lw["w_in"], w_conv_out[l], w_attn_out[l], zp, rel_bias, sinks[l], IN_PROJ_TN, batch, seq)
        hp, hnp, conv_p = _mix_prompt(zp, attn_p, hp, w_conv[l], lw["g_ffn"], lw["wco"], lw["wao"], lw["wo"],
                                      MIX_TM, batch, seq)
        hp = _ffn(hnp, hp, g_final2, lw["wg"], lw["wu"], lw["wd"], FFN_TM, FFN_TF, final)
        kv_p = zp.reshape(batch, seq, D_IN_PROJ)[:, seq - WINDOW:, OFF_K:OFF_V + D_KV].astype(F32)
        kp_l.append(kv_p[..., :D_KV].reshape(batch, WINDOW, N_KV_HEADS, HEAD_DIM))
        vp_l.append(kv_p[..., D_KV:].reshape(batch, WINDOW, N_KV_HEADS, HEAD_DIM))
        cp_l.append(conv_p)

        cache_kt = jnp.transpose(cache_k[l], (0, 2, 3, 1)).reshape(dec_batch, D_KV, wc)
        cache_vt = jnp.transpose(cache_v[l], (0, 2, 3, 1)).reshape(dec_batch, D_KV, wc)
        attn_s, nk, nv = _attn_sample(zs, cache_kt, cache_vt, rel_bias, sinks[l], dec_batch, steps)
        hs, hns, conv_s = _mix_sample(zs, state_conv[l], attn_s, hs, w_conv[l], lw["g_ffn"], lw["wco"], lw["wao"],
                                      lw["wo"], MIX_TM, steps)
        hs = _ffn(hns, hs, g_final2, lw["wg"], lw["wu"], lw["wd"], FFN_TM, FFN_TF, final)
        ks_l.append(nk.reshape(dec_batch, wc, N_KV_HEADS, HEAD_DIM))
        vs_l.append(nv.reshape(dec_batch, wc, N_KV_HEADS, HEAD_DIM))
        cs_l.append(conv_s)

    return (hp.reshape(batch, seq, D_MODEL), hs.reshape(dec_batch, steps, D_MODEL),
            jnp.stack(kp_l), jnp.stack(vp_l), jnp.stack(cp_l),
            jnp.stack(ks_l), jnp.stack(vs_l), jnp.stack(cs_l))
```

```python
import functools
import math

import numpy as np
import jax
import jax.numpy as jnp
from jax import lax
from jax.experimental import pallas as pl
from jax.experimental.pallas import tpu as pltpu

D_MODEL = 2048
D_CONV = D_MODEL // 2
CONV_WIDTH = 3
HEAD_DIM = 64
N_HEADS = (D_MODEL // 2) // HEAD_DIM
N_KV_HEADS = N_HEADS // 4
GROUP = N_HEADS // N_KV_HEADS
D_ATTN = N_HEADS * HEAD_DIM
D_KV = N_KV_HEADS * HEAD_DIM
WINDOW = 128
NUM_BUCKETS = 32
MAX_DISTANCE = 128
D_FF = -(-8 * D_MODEL // (3 * 256)) * 256
EPS = 1e-6
D_IN_PROJ = 3 * D_CONV + D_ATTN + 2 * D_KV + 2 * D_MODEL

OFF_GC = 0
OFF_GA = D_MODEL
OFF_B = 2 * D_MODEL
OFF_C = OFF_B + D_CONV
OFF_H = OFF_C + D_CONV
OFF_Q = OFF_H + D_CONV
OFF_K = OFF_Q + D_ATTN
OFF_V = OFF_K + D_KV

LANES = 128
NEG = -1e30
F32 = jnp.float32
BF16 = jnp.bfloat16
VMEM_LIMIT = 60 * 1024 * 1024


def _bucket_thresholds():
    max_exact = NUM_BUCKETS // 2
    d = np.arange(MAX_DISTANCE)
    ratio = np.log(np.maximum(d, 1).astype(np.float32) / np.float32(max_exact)) / np.float32(
        math.log(MAX_DISTANCE / max_exact))
    large = max_exact + (ratio * np.float32(NUM_BUCKETS - max_exact)).astype(np.int32)
    large = np.minimum(large, NUM_BUCKETS - 1)
    return [int(np.min(d[(d >= max_exact) & (large >= b)])) for b in range(max_exact + 1, NUM_BUCKETS)]


BUCKET_THRESHOLDS = _bucket_thresholds()


def _params(*semantics):
    return pltpu.CompilerParams(dimension_semantics=semantics, vmem_limit_bytes=VMEM_LIMIT)


def _rms_rows(x, g):
    ms = jnp.mean(x * x, axis=-1, keepdims=True)
    return x * lax.rsqrt(ms + EPS) * g


NORM_ROWS = 256


def _inproj_kernel(x_ref, g_ref, w_ref, side_ref, z_ref, side_bf16_ref, xn_ref):
    _inproj_norm(pl.program_id(1) == 0, x_ref, g_ref, xn_ref)
    _inproj_columns(w_ref, z_ref, xn_ref)
    side_bf16_ref[...] = side_ref[...].astype(BF16)


BF16_ROWS = 16


def _side_cast_rows(rows, nsteps):
    per_step = -(-rows // nsteps)
    per_step = -(-per_step // BF16_ROWS) * BF16_ROWS
    assert rows % per_step == 0
    return per_step


def _inproj_norm(first_column_step, x_ref, g_ref, xn_ref):
    @pl.when(first_column_step)
    def _():
        def body(r, carry):
            rows = pl.ds(pl.multiple_of(r * NORM_ROWS, NORM_ROWS), NORM_ROWS)
            xn_ref[rows, :] = _rms_rows(x_ref[rows, :], g_ref[0:1, :]).astype(xn_ref.dtype)
            return carry

        lax.fori_loop(0, x_ref.shape[0] // NORM_ROWS, body, 0)


def _inproj_columns(w_ref, z_ref, xn_ref):
    z_ref[...] = jnp.dot(xn_ref[...], w_ref[...].astype(BF16), preferred_element_type=F32).astype(z_ref.dtype)


def _in_proj(x, g, w, side, tm, tn):
    m = x.shape[0]
    n = w.shape[1]
    gate_start = n - 2 * D_MODEL
    assert gate_start % tn == 0 and n % tn == 0
    gate_block, nblocks = gate_start // tn, n // tn
    side_rows = _side_cast_rows(side.shape[0], (m // tm) * nblocks)
    side_spec = pl.BlockSpec((side_rows, side.shape[1]),
                             lambda i, j: (jnp.minimum(i * nblocks + j, side.shape[0] // side_rows - 1), 0))
    return pl.pallas_call(
        _inproj_kernel,
        grid=(m // tm, nblocks),
        in_specs=[
            pl.BlockSpec((tm, D_MODEL), lambda i, j: (i, 0)),
            pl.BlockSpec((1, D_MODEL),lambda i, j: (0, 0)),
            pl.BlockSpec((D_MODEL, tn), lambda i, j: (0, (j + gate_block) % nblocks)),
            side_spec,
        ],
        out_specs=[pl.BlockSpec((tm, tn), lambda i, j: (i, j)), side_spec],
        out_shape=[jax.ShapeDtypeStruct((m, n), BF16), jax.ShapeDtypeStruct(side.shape, BF16)],
        scratch_shapes=[pltpu.VMEM((tm, D_MODEL), BF16)],
        compiler_params=_params("arbitrary", "arbitrary"),
        name="in_proj",
    )(x, g, w, side)


def _bias_from_distance(dist, head, relb_ref):
    valid = (dist >= 0) & (dist < WINDOW)
    d = jnp.clip(dist, 0, MAX_DISTANCE - 1)
    bucket = jnp.minimum(d, NUM_BUCKETS // 2)
    for thr in BUCKET_THRESHOLDS:
        bucket = bucket + (d >= thr).astype(jnp.int32)
    bias = jnp.zeros(dist.shape, F32)
    for b in range(NUM_BUCKETS):
        bias = jnp.where(bucket == b, relb_ref[b, head], bias)
    return jnp.where(valid, bias, NEG)


def _biased_scores(s, bias):
    return jnp.where(bias > 0.5 * NEG, s + bias, NEG)


def _head_in_group_columns(q, h):
    half = D_KV // 2
    assert 2 * HEAD_DIM == half and GROUP * HEAD_DIM == D_KV
    g = h // GROUP
    src = q[:, (h // 2) * half:(h // 2 + 1) * half]
    if h % 2 != g % 2:
        src = pltpu.roll(src, HEAD_DIM, 1)
    lane_half = lax.broadcasted_iota(jnp.int32, src.shape, 1) // HEAD_DIM
    placed = jnp.where(lane_half == g % 2, src, 0.0)
    zero = jnp.zeros_like(placed)
    return jnp.concatenate([placed, zero] if g // 2 == 0 else [zero, placed], axis=1)


def _group_columns(h):
    g = h // GROUP
    return slice(g * HEAD_DIM, (g + 1) * HEAD_DIM)


INPROJ_FILL_CHUNKS = 4
PROMPT_BLOCKS_PER_STEP = 4


def _inproj_attn_kernel(x_ref, g_ref, w_ref, side0_ref, side1_ref, relb_ref, sink_ref, q_ref, kp_ref, kc_ref,
                        vp_ref, vc_ref, z_ref, o_ref, side0_bf16_ref, side1_bf16_ref,
                        xn_ref, wb_ref, bias_ref, lhs_ref, *, steps_per_seq):
    j = pl.program_id(0)
    _attn_prompt_bias(j == 0, relb_ref, bias_ref)
    _inproj_norm(j == 0, x_ref, g_ref, xn_ref)
    wb_ref[...] = w_ref[...].astype(BF16)
    side0_bf16_ref[...] = side0_ref[...].astype(BF16)
    side1_bf16_ref[...] = side1_ref[...].astype(BF16)
    chunk = xn_ref.shape[0] // INPROJ_FILL_CHUNKS

    def column_chunk(c):
        rows = slice(c * chunk, (c + 1) * chunk)

        def emit():
            z_ref[rows, :] = jnp.dot(xn_ref[rows, :], wb_ref[...], preferred_element_type=F32).astype(z_ref.dtype)
        return emit

    _attn_prompt_blocks(j % steps_per_seq != 0, sink_ref, q_ref, kp_ref, kc_ref, vp_ref, vc_ref,
                        o_ref, bias_ref, lhs_ref, fillers=[column_chunk(c) for c in range(INPROJ_FILL_CHUNKS)])


def _from_prev_mask():
    qi = lax.broadcasted_iota(jnp.int32, (WINDOW, WINDOW), 0)
    kj = lax.broadcasted_iota(jnp.int32, (WINDOW, WINDOW), 1)
    return qi, kj, kj > qi


def _attn_prompt_bias(first_step, relb_ref, bias_ref):
    @pl.when(first_step)
    def _():
        qi, kj, from_prev = _from_prev_mask()
        dist = jnp.where(from_prev, qi + WINDOW - kj, qi - kj)
        for h in range(N_HEADS):
            rows = slice(h * WINDOW, (h + 1) * WINDOW)
            bias = _bias_from_distance(dist, h, relb_ref)
            bias_ref[1, rows, :] = bias
            bias_ref[0, rows, :] = jnp.where(from_prev, NEG, bias)


def _attn_prompt_blocks(has_prev_block, sink_ref, q_ref, kp_ref, kc_ref, vp_ref, vc_ref, o_ref, bias_ref, lhs_ref,
                        fillers=()):
    _, _, from_prev = _from_prev_mask()
    k_all = jnp.concatenate([kp_ref[...], kc_ref[...]], axis=0)
    v_all = jnp.concatenate([vp_ref[...], vc_ref[...]], axis=0)

    def keys_of(blk):
        return slice(blk * WINDOW, (blk + 2) * WINDOW)

    nblk = PROMPT_BLOCKS_PER_STEP
    fillers = list(fillers)
    per_stage = -(-len(fillers) // (nblk + 2))
    scores, probs = {}, {}
    for stage in range(nblk + 2):
        if stage < nblk:
            q = q_ref[stage * WINDOW:(stage + 1) * WINDOW, :]
            scores[stage] = _block_scores(q, k_all[keys_of(stage)], lhs_ref.at[stage])
        for _ in range(min(per_stage, len(fillers))):
            fillers.pop(0)()
        if 0 <= stage - 1 < nblk:
            blk = stage - 1
            has_prev = has_prev_block.astype(jnp.int32) if blk == 0 else 1
            probs[blk] = _block_softmax(scores.pop(blk), has_prev, from_prev, sink_ref, bias_ref)
        if 0 <= stage - 2 < nblk:
            blk = stage - 2
            o_ref[blk * WINDOW:(blk + 1) * WINDOW, :] = _block_output(
                *probs.pop(blk), v_all[keys_of(blk)]).astype(o_ref.dtype)


def _block_scores(q, kk, lhs_ref):
    q = q.astype(F32) * (HEAD_DIM ** -0.5)
    for h in range(N_HEADS):
        lhs_ref[h * WINDOW:(h + 1) * WINDOW, :] = _head_in_group_columns(q, h).astype(BF16)
    return lax.dot_general(lhs_ref[...], kk, (((1,), (1,)), ((), ())), preferred_element_type=F32)


def _block_softmax(s, has_prev, from_prev, sink_ref, bias_ref):
    probs, sink_terms = [], []
    for h in range(N_HEADS):
        rows = slice(h * WINDOW, (h + 1) * WINDOW)
        sh = _biased_scores(jnp.where(from_prev, s[rows, :WINDOW], s[rows, WINDOW:]), bias_ref[has_prev, rows, :])
        m = jnp.maximum(jnp.max(sh, axis=-1, keepdims=True), sink_ref[h])
        p = jnp.exp(sh - m)
        probs.append(jnp.concatenate([jnp.where(from_prev, p, 0.0), jnp.where(from_prev, 0.0, p)],
                                     axis=1).astype(BF16))
        sink_terms.append(jnp.exp(sink_ref[h] - m))
    return probs, sink_terms


def _block_output(probs, sink_terms, vv):
    assert 2 * HEAD_DIM == LANES and GROUP % 2 == 0
    ones = jnp.ones((vv.shape[0], HEAD_DIM), BF16)
    low_half = lax.broadcasted_iota(jnp.int32, (WINDOW, LANES), 1) < HEAD_DIM
    outs = []
    for g in range(N_KV_HEADS):
        vg = vv[:, g * HEAD_DIM:(g + 1) * HEAD_DIM]
        even_heads = [g * GROUP + r for r in range(0, GROUP, 2)]
        pe = jnp.concatenate([probs[h] for h in even_heads], axis=0)
        po = jnp.concatenate([probs[h + 1] for h in even_heads], axis=0)
        oe = jnp.dot(pe, jnp.concatenate([vg, ones], axis=1), preferred_element_type=F32)
        oo = jnp.dot(po, jnp.concatenate([ones, vg], axis=1), preferred_element_type=F32)
        for k, h in enumerate(even_heads):
            a = oe[k * WINDOW:(k + 1) * WINDOW]
            b = oo[k * WINDOW:(k + 1) * WINDOW]
            out_pair = jnp.where(low_half, a, b)
            sum_pair = pltpu.roll(jnp.where(low_half, b, a), HEAD_DIM, 1)
            sink_pair = jnp.where(low_half, sink_terms[h], sink_terms[h + 1])
            outs.append(out_pair * (1.0 / (sum_pair + sink_pair)))
    return jnp.concatenate(outs, axis=1)


def _in_proj_sample_attn_prompt(x, g, w, side0, side1, zp, rel_bias, sinks, tn, batch, seq):
    tm = x.shape[0]
    n = w.shape[1]
    gate_start = n - 2 * D_MODEL
    assert gate_start % tn == 0 and n % tn == 0
    gate_block, nsteps = gate_start // tn, n // tn
    bps = PROMPT_BLOCKS_PER_STEP
    assert seq % (bps * WINDOW) == 0
    steps_per_seq = seq // (bps * WINDOW)
    attn_steps = batch * steps_per_seq
    assert attn_steps <= nsteps

    def astep(j):
        return jnp.minimum(j, attn_steps - 1)

    def cur(off, width):
        return lambda j: (astep(j), off // width)

    def prev(off, width):
        return lambda j: (astep(j) * bps - jnp.minimum(astep(j) % steps_per_seq, 1), off // width)

    def side_spec(side):
        rows = _side_cast_rows(side.shape[0], nsteps)
        return pl.BlockSpec((rows, side.shape[1]), lambda j: (jnp.minimum(j, side.shape[0] // rows - 1), 0))

    smem = pl.BlockSpec(memory_space=pltpu.SMEM)
    return pl.pallas_call(
        functools.partial(_inproj_attn_kernel, steps_per_seq=steps_per_seq),
        grid=(nsteps,),
        in_specs=[
            pl.BlockSpec((tm, D_MODEL), lambda j: (0, 0)),
            pl.BlockSpec((1, D_MODEL),lambda j: (0, 0)),
            pl.BlockSpec((D_MODEL, tn), lambda j: (0, (j + gate_block) % nsteps)),
            side_spec(side0),
            side_spec(side1),
            smem,
            smem,
            pl.BlockSpec((bps * WINDOW, D_ATTN), cur(OFF_Q, D_ATTN)),
            pl.BlockSpec((WINDOW, D_KV), prev(OFF_K, D_KV)),
            pl.BlockSpec((bps * WINDOW, D_KV), cur(OFF_K, D_KV)),
            pl.BlockSpec((WINDOW, D_KV), prev(OFF_V, D_KV)),
            pl.BlockSpec((bps * WINDOW, D_KV), cur(OFF_V, D_KV)),
        ],
        out_specs=[
            pl.BlockSpec((tm, tn), lambda j: (0, j)),
            pl.BlockSpec((bps * WINDOW, D_ATTN), lambda j: (j, 0)),
            side_spec(side0),
            side_spec(side1),
        ],
        out_shape=[
            jax.ShapeDtypeStruct((tm, n), BF16),
            jax.ShapeDtypeStruct((nsteps * bps * WINDOW, D_ATTN), BF16),
            jax.ShapeDtypeStruct(side0.shape, BF16),
            jax.ShapeDtypeStruct(side1.shape, BF16),
        ],
        scratch_shapes=[pltpu.VMEM((tm, D_MODEL), BF16),
                        pltpu.VMEM((D_MODEL, tn), BF16),
                        pltpu.VMEM((2, N_HEADS * WINDOW, WINDOW), F32),
                        pltpu.VMEM((bps, N_HEADS * WINDOW, D_KV), BF16)],
        compiler_params=_params("arbitrary"),
        name="in_proj_sample_attn_prompt",
    )(x, g, w, side0, side1, rel_bias, sinks, zp, zp, zp, zp, zp)


SAMPLE_BATCH_TILE = 16


def _attn_sample_kernel(relb_ref, sink_ref, q_ref, kn_ref, vn_ref, ckt_ref, cvt_ref,
                        o_ref, nk_ref, nv_ref,
                        bias1_ref, bias2_ref, sinkv_ref, z_ref, s_ref, p_ref, oh_ref, *, steps):
    bt, wc, _ = nk_ref.shape
    tile_rows = bt * steps

    @pl.when(pl.program_id(0) == 0)
    def _():
        t = lax.broadcasted_iota(jnp.int32, (steps, wc), 0)
        j = lax.broadcasted_iota(jnp.int32, (steps, wc), 1)
        t2 = lax.broadcasted_iota(jnp.int32, (steps, tile_rows), 0)
        j2 = lax.broadcasted_iota(jnp.int32, (steps, tile_rows), 1)
        own = jnp.where(j2 < steps, t2 - j2, -1)
        for h in range(N_HEADS):
            sl = slice(h * steps, (h + 1) * steps)
            bias1_ref[sl, :] = _bias_from_distance(t + wc - j, h, relb_ref)
            bias2_ref[0, sl, :] = _bias_from_distance(own, h, relb_ref)
            sinkv_ref[sl, :] = jnp.full((steps, LANES), sink_ref[h], F32)
        for b in range(1, bt):
            bias2_ref[b] = pltpu.roll(bias2_ref[0], b * steps, 1)

    kn = kn_ref[...]
    vn = vn_ref[...]
    nk_ref[:, wc - steps:wc, :] = kn.astype(F32).reshape(bt, steps, D_KV)
    nv_ref[:, wc - steps:wc, :] = vn.astype(F32).reshape(bt, steps, D_KV)

    q = q_ref[...].astype(F32) * (HEAD_DIM ** -0.5)
    for h in range(N_HEADS):
        z_ref[:, h * steps:(h + 1) * steps, :] = _head_in_group_columns(q, h).reshape(bt, steps, D_KV)

    for b in range(bt):
        ck = ckt_ref[b].T
        nk_ref[b, 0:wc - steps, :] = ck[steps:, :]
        keys = jnp.concatenate([ck.astype(BF16), kn], axis=0)
        s_ref[b] = lax.dot_general(z_ref[b].astype(BF16), keys, (((1,), (1,)), ((), ())),
                                   preferred_element_type=F32)

    bias = jnp.concatenate([jnp.broadcast_to(bias1_ref[...], (bt,) + bias1_ref.shape), bias2_ref[...]], axis=2)
    s = _biased_scores(s_ref[...], bias)
    sink = sinkv_ref[...][None]
    m = jnp.maximum(jnp.max(s, axis=-1, keepdims=True), sink)
    p_ref[...] = jnp.exp(s - jnp.concatenate([m] * (s.shape[-1] // LANES), axis=-1)).astype(BF16)
    sink_term = jnp.exp(sink - m)

    vn_ones = jnp.concatenate([vn, jnp.ones((tile_rows, LANES), BF16)], axis=1)
    for b in range(bt):
        cv = cvt_ref[b].T
        nv_ref[b, 0:wc - steps, :] = cv[steps:, :]
        cv_ones = jnp.concatenate([cv.astype(BF16), jnp.ones((wc, LANES), BF16)], axis=1)
        oh_ref[b] = jnp.dot(p_ref[b], jnp.concatenate([cv_ones, vn_ones], axis=0), preferred_element_type=F32)

    inv = 1.0 / (oh_ref[:, :, D_KV:] + sink_term)
    outs = []
    for h in range(N_HEADS):
        rows = slice(h * steps, (h + 1) * steps)
        outs.append((oh_ref[:, rows, _group_columns(h)] * inv[:, rows, :HEAD_DIM]).reshape(tile_rows, HEAD_DIM))
    o_ref[...] = jnp.concatenate(outs, axis=1).astype(o_ref.dtype)


def _attn_sample(z, cache_kt, cache_vt, rel_bias, sinks, batch, steps):
    wc = cache_kt.shape[2]
    bt = SAMPLE_BATCH_TILE
    rows = bt * steps
    smem = pl.BlockSpec(memory_space=pltpu.SMEM)
    cache_t_spec = pl.BlockSpec((bt, D_KV, wc), lambda i: (i, 0, 0))
    cache_spec = pl.BlockSpec((bt, wc, D_KV), lambda i: (i, 0, 0))
    return pl.pallas_call(
        functools.partial(_attn_sample_kernel, steps=steps),
        grid=(batch // bt,),
        in_specs=[
            smem,
            smem,
            pl.BlockSpec((rows, D_ATTN), lambda i: (i, OFF_Q // D_ATTN)),
            pl.BlockSpec((rows, D_KV), lambda i: (i, OFF_K // D_KV)),
            pl.BlockSpec((rows, D_KV), lambda i: (i, OFF_V // D_KV)),
            cache_t_spec,
            cache_t_spec,
        ],
        out_specs=[
            pl.BlockSpec((rows, D_ATTN), lambda i: (i, 0)),
            cache_spec,
            cache_spec,
        ],
        out_shape=[
            jax.ShapeDtypeStruct((batch * steps, D_ATTN), BF16),
            jax.ShapeDtypeStruct((batch, wc, D_KV), F32),
            jax.ShapeDtypeStruct((batch, wc, D_KV), F32),
        ],
        scratch_shapes=[
            pltpu.VMEM((N_HEADS * steps, wc), F32),
            pltpu.VMEM((bt, N_HEADS * steps, rows), F32),
            pltpu.VMEM((N_HEADS * steps, LANES), F32),
            pltpu.VMEM((bt, N_HEADS * steps, D_KV), F32),
            pltpu.VMEM((bt, N_HEADS * steps, wc + rows), F32),
            pltpu.VMEM((bt, N_HEADS * steps, wc + rows), BF16),
            pltpu.VMEM((bt, N_HEADS * steps, D_KV + LANES), F32),
        ],
        compiler_params=_params("arbitrary"),
        name="attn_sample",
    )(rel_bias, sinks, z, z, z, cache_kt, cache_vt)


CARRY_ROWS = 8


def _gated_conv(b, wc_ref, u, u1, u2):
    conv = wc_ref[0:1, :] * u2 + wc_ref[1:2, :] * u1 + wc_ref[2:3, :] * u
    return (b.astype(F32) * conv).astype(BF16)


def _mix_tail(conv_branch_input, at_ref, gc_ref, ga_ref, x_ref, g_ref, wco_ref, wao_ref, wo_ref, h_ref, hn_ref):
    y_conv = jnp.dot(conv_branch_input(), wco_ref[...], preferred_element_type=F32)
    y_attn = jnp.dot(at_ref[...], wao_ref[...], preferred_element_type=F32)
    merged = (jax.nn.sigmoid(gc_ref[...].astype(F32)) * y_conv
              + jax.nn.sigmoid(ga_ref[...].astype(F32)) * y_attn)
    h = x_ref[...] + jnp.dot(merged.astype(BF16), wo_ref[...], preferred_element_type=F32)
    h_ref[...] = h
    hn_ref[...] = _rms_rows(h, g_ref[0:1, :]).astype(hn_ref.dtype)


def _delayed_from_scratch(us_ref, rows):
    def at(delay):
        return us_ref[CARRY_ROWS - delay + rows.start:CARRY_ROWS - delay + rows.stop, :]
    return at(0), at(1), at(2)


def _mix_prompt_kernel(b_ref, c_ref, hc_ref, wc_ref, at_ref, gc_ref, ga_ref, x_ref, g_ref,
                       wco_ref, wao_ref, wo_ref, h_ref, hn_ref, st_ref, us_ref, *, tiles_per_seq):
    tm = x_ref.shape[0]

    @pl.when(pl.program_id(0) % tiles_per_seq == 0)
    def _():
        us_ref[0:CARRY_ROWS, :] = jnp.zeros((CARRY_ROWS, D_CONV), F32)

    def conv_branch_input():
        u = c_ref[...].astype(F32) * hc_ref[...].astype(F32)
        us_ref[CARRY_ROWS:, :] = u
        st_ref[...] = u[tm - (CONV_WIDTH - 1):, :]
        yc = _gated_conv(b_ref[...], wc_ref, *_delayed_from_scratch(us_ref, slice(0, tm)))
        us_ref[0:CARRY_ROWS, :] = us_ref[tm:tm + CARRY_ROWS, :]
        return yc

    _mix_tail(conv_branch_input, at_ref, gc_ref, ga_ref, x_ref, g_ref, wco_ref, wao_ref, wo_ref, h_ref, hn_ref)


def _mix_sample_kernel(b_ref, c_ref, hc_ref, st_ref, wc_ref, at_ref, gc_ref, ga_ref, x_ref, g_ref,
                       wco_ref, wao_ref, wo_ref, h_ref, hn_ref, new_st_ref, us_ref, *, steps):
    tm = x_ref.shape[0]
    nseq = tm // steps
    keep = CONV_WIDTH - 1

    def conv_branch_input():
        u = c_ref[...].astype(F32) * hc_ref[...].astype(F32)
        us_ref[:, CARRY_ROWS - keep:CARRY_ROWS, :] = st_ref[...]
        us_ref[:, CARRY_ROWS:, :] = u.reshape(nseq, steps, D_CONV)
        new_st_ref[...] = us_ref[:, CARRY_ROWS + steps - keep:, :]

        def delayed(d):
            return us_ref[:, CARRY_ROWS - d:CARRY_ROWS - d + steps, :].reshape(tm, D_CONV)

        return _gated_conv(b_ref[...], wc_ref, u, delayed(1), delayed(2))

    _mix_tail(conv_branch_input, at_ref, gc_ref, ga_ref, x_ref, g_ref, wco_ref, wao_ref, wo_ref, h_ref, hn_ref)


def _mix_specs(tm):
    resident = functools.partial(pl.BlockSpec, index_map=lambda i: (0, 0), pipeline_mode=pl.Buffered(1))
    row_spec = pl.BlockSpec((tm, D_MODEL), lambda i: (i, 0))

    def zcol(off, width):
        return pl.BlockSpec((tm, width), lambda i: (i, off // width))

    conv_in = [zcol(OFF_B, D_CONV), zcol(OFF_C, D_CONV), zcol(OFF_H, D_CONV)]
    rest_in = [
        pl.BlockSpec((CONV_WIDTH, D_CONV), lambda i: (0, 0)),
        pl.BlockSpec((tm, D_ATTN), lambda i: (i, 0)),
        zcol(OFF_GC, D_MODEL),
        zcol(OFF_GA, D_MODEL),
        row_spec,
        pl.BlockSpec((1, D_MODEL),lambda i: (0, 0)),
        resident((D_CONV, D_MODEL)),
        resident((D_ATTN, D_MODEL)),
        resident((D_MODEL, D_MODEL)),
    ]
    return conv_in, rest_in, row_spec


def _mix_prompt(z, attn, x, w_conv, g_ffn, wco, wao, wo, tm, batch, seq):
    m = x.shape[0]
    assert seq % tm == 0 and tm >= CARRY_ROWS
    tiles_per_seq = seq // tm
    conv_in, rest_in, row_spec = _mix_specs(tm)
    return pl.pallas_call(
        functools.partial(_mix_prompt_kernel, tiles_per_seq=tiles_per_seq),
        grid=(m // tm,),
        in_specs=conv_in + rest_in,
        out_specs=[row_spec, row_spec,
                   pl.BlockSpec((None, CONV_WIDTH - 1, D_CONV), lambda i: (i // tiles_per_seq, 0, 0))],
        out_shape=[jax.ShapeDtypeStruct((m, D_MODEL), F32), jax.ShapeDtypeStruct((m, D_MODEL), BF16),
                   jax.ShapeDtypeStruct((batch, CONV_WIDTH - 1, D_CONV), F32)],
        scratch_shapes=[pltpu.VMEM((tm + CARRY_ROWS, D_CONV), F32)],
        compiler_params=_params("arbitrary"),
        name="mix_prompt",
    )(z, z, z, w_conv, attn, z, z, x, g_ffn, wco, wao, wo)


def _mix_sample(z, state, attn, x, w_conv, g_ffn, wco, wao, wo, tm, steps):
    m = x.shape[0]
    assert tm % steps == 0 and CONV_WIDTH - 1 <= min(steps, CARRY_ROWS)
    conv_in, rest_in, row_spec = _mix_specs(tm)
    state_spec = pl.BlockSpec((tm // steps, CONV_WIDTH - 1, D_CONV), lambda i: (i, 0, 0))
    return pl.pallas_call(
        functools.partial(_mix_sample_kernel, steps=steps),
        grid=(m // tm,),
        in_specs=conv_in + [state_spec] + rest_in,
        out_specs=[row_spec, row_spec, state_spec],
        out_shape=[jax.ShapeDtypeStruct((m, D_MODEL), F32), jax.ShapeDtypeStruct((m, D_MODEL), BF16),
                   jax.ShapeDtypeStruct(state.shape, F32)],
        scratch_shapes=[pltpu.VMEM((tm // steps, CARRY_ROWS + steps, D_CONV), F32)],
        compiler_params=_params("arbitrary"),
        name="mix_sample",
    )(z, z, z, state, w_conv, attn, z, z, x, g_ffn, wco, wao, wo)


def _ffn_kernel(hn_ref, h_ref, gf_ref, wg_ref, wu_ref, wd_ref, y_ref, *, final_norm):
    j = pl.program_id(1)

    @pl.when(j == 0)
    def _():
        y_ref[...] = h_ref[...]

    hn = hn_ref[...]
    a = (jax.nn.silu(jnp.dot(hn, wg_ref[...].astype(BF16), preferred_element_type=F32))
         * jnp.dot(hn, wu_ref[...].astype(BF16), preferred_element_type=F32))
    y_ref[...] += jnp.dot(a.astype(BF16), wd_ref[...].astype(BF16), preferred_element_type=F32)

    if final_norm:
        @pl.when(j == pl.num_programs(1) - 1)
        def _():
            def body(r, carry):
                rows = pl.ds(pl.multiple_of(r * NORM_ROWS, NORM_ROWS), NORM_ROWS)
                y_ref[rows, :] = _rms_rows(y_ref[rows, :], gf_ref[0:1, :])
                return carry

            lax.fori_loop(0, y_ref.shape[0] // NORM_ROWS, body, 0)


def _ffn(hn, h, g_final, wg, wu, wd, tm, tf, final_norm):
    m = h.shape[0]
    return pl.pallas_call(
        functools.partial(_ffn_kernel, final_norm=final_norm),
        grid=(m // tm, D_FF // tf),
        in_specs=[
            pl.BlockSpec((tm, D_MODEL), lambda i, j: (i, 0)),
            pl.BlockSpec((tm, D_MODEL), lambda i, j: (i, 0)),
            pl.BlockSpec((1, D_MODEL),lambda i, j: (0, 0)),
            pl.BlockSpec((D_MODEL, tf), lambda i, j: (0, j)),
            pl.BlockSpec((D_MODEL, tf), lambda i, j: (0, j)),
            pl.BlockSpec((tf, D_MODEL), lambda i, j: (j, 0)),
        ],
        out_specs=pl.BlockSpec((tm, D_MODEL), lambda i, j: (i, 0)),
        out_shape=jax.ShapeDtypeStruct((m, D_MODEL), F32),
        compiler_params=_params("arbitrary", "arbitrary"),
        name="ffn",
    )(hn, h, g_final, wg, wu, wd)


IN_PROJ_TM_PROMPT = 2048
IN_PROJ_TN = 512
MIX_TM = 256
FFN_TM = 1024
FFN_TF = 256


def kernel(x_prompt, x_sample, cache_k, cache_v, state_conv, rel_bias, w_in, w_conv, w_conv_out, sinks,
           w_attn_out, w_o, g_mix, g_ffn, w_gate, w_up, w_down, g_final):
    depth = w_in.shape[0]
    batch, seq, _ = x_prompt.shape
    dec_batch, steps, _ = x_sample.shape
    wc = cache_k.shape[2]
    assert seq % WINDOW == 0 and wc == WINDOW and dec_batch % SAMPLE_BATCH_TILE == 0

    hp = x_prompt.reshape(batch * seq, D_MODEL)
    hs = x_sample.reshape(dec_batch * steps, D_MODEL)
    g_final2 = g_final.reshape(1, D_MODEL)
    kp_l, vp_l, cp_l, ks_l, vs_l, cs_l = [], [], [], [], [], []
    for l in range(depth):
        final = l == depth - 1
        lw = {
            "w_in": w_in[l],
            "wg": w_gate[l],
            "wu": w_up[l],
            "wd": w_down[l],
            "g_ffn": g_ffn[l].reshape(1, D_MODEL),
        }
        g_mix_l = g_mix[l].reshape(1, D_MODEL)

        zp, lw["wo"] = _in_proj(hp, g_mix_l, lw["w_in"], w_o[l], IN_PROJ_TM_PROMPT, IN_PROJ_TN)
        zs, attn_p, lw["wco"], lw["wao"] = _in_proj_sample_attn_prompt(
            hs, g_mix_l, lw["w_in"], w_conv_out[l], w_attn_out[l], zp, rel_bias, sinks[l], IN_PROJ_TN, batch, seq)
        hp, hnp, conv_p = _mix_prompt(zp, attn_p, hp, w_conv[l], lw["g_ffn"], lw["wco"], lw["wao"], lw["wo"],
                                      MIX_TM, batch, seq)
        hp = _ffn(hnp, hp, g_final2, lw["wg"], lw["wu"], lw["wd"], FFN_TM, FFN_TF, final)
        kv_p = zp.reshape(batch, seq, D_IN_PROJ)[:, seq - WINDOW:, OFF_K:OFF_V + D_KV].astype(F32)
        kp_l.append(kv_p[..., :D_KV].reshape(batch, WINDOW, N_KV_HEADS, HEAD_DIM))
        vp_l.append(kv_p[..., D_KV:].reshape(batch, WINDOW, N_KV_HEADS, HEAD_DIM))
        cp_l.append(conv_p)

        cache_kt = jnp.transpose(cache_k[l], (0, 2, 3, 1)).reshape(dec_batch, D_KV, wc)
        cache_vt = jnp.transpose(cache_v[l], (0, 2, 3, 1)).reshape(dec_batch, D_KV, wc)
        attn_s, nk, nv = _attn_sample(zs, cache_kt, cache_vt, rel_bias, sinks[l], dec_batch, steps)
        hs, hns, conv_s = _mix_sample(zs, state_conv[l], attn_s, hs, w_conv[l], lw["g_ffn"], lw["wco"], lw["wao"],
                                      lw["wo"], MIX_TM, steps)
        hs = _ffn(hns, hs, g_final2, lw["wg"], lw["wu"], lw["wd"], FFN_TM, FFN_TF, final)
        ks_l.append(nk.reshape(dec_batch, wc, N_KV_HEADS, HEAD_DIM))
        vs_l.append(nv.reshape(dec_batch, wc, N_KV_HEADS, HEAD_DIM))
        cs_l.append(conv_s)

    return (hp.reshape(batch, seq, D_MODEL), hs.reshape(dec_batch, steps, D_MODEL),
            jnp.stack(kp_l), jnp.stack(vp_l), jnp.stack(cp_l),
            jnp.stack(ks_l), jnp.stack(vs_l), jnp.stack(cs_l))
```

```python
import functools
import math

import numpy as np
import jax
import jax.numpy as jnp
from jax import lax
from jax.experimental import pallas as pl
from jax.experimental.pallas import tpu as pltpu

D_MODEL = 2048
D_CONV = D_MODEL // 2
CONV_WIDTH = 3
HEAD_DIM = 64
N_HEADS = (D_MODEL // 2) // HEAD_DIM
N_KV_HEADS = N_HEADS // 4
GROUP = N_HEADS // N_KV_HEADS
D_ATTN = N_HEADS * HEAD_DIM
D_KV = N_KV_HEADS * HEAD_DIM
WINDOW = 128
NUM_BUCKETS = 32
MAX_DISTANCE = 128
D_FF = -(-8 * D_MODEL // (3 * 256)) * 256
EPS = 1e-6
D_IN_PROJ = 3 * D_CONV + D_ATTN + 2 * D_KV + 2 * D_MODEL

OFF_GC = 0
OFF_GA = D_MODEL
OFF_B = 2 * D_MODEL
OFF_C = OFF_B + D_CONV
OFF_H = OFF_C + D_CONV
OFF_Q = OFF_H + D_CONV
OFF_K = OFF_Q + D_ATTN
OFF_V = OFF_K + D_KV

LANES = 128
NEG = -1e30
F32 = jnp.float32
BF16 = jnp.bfloat16
VMEM_LIMIT = 60 * 1024 * 1024


def _bucket_thresholds():
    max_exact = NUM_BUCKETS // 2
    d = np.arange(MAX_DISTANCE)
    ratio = np.log(np.maximum(d, 1).astype(np.float32) / np.float32(max_exact)) / np.float32(
        math.log(MAX_DISTANCE / max_exact))
    large = max_exact + (ratio * np.float32(NUM_BUCKETS - max_exact)).astype(np.int32)
    large = np.minimum(large, NUM_BUCKETS - 1)
    return [int(np.min(d[(d >= max_exact) & (large >= b)])) for b in range(max_exact + 1, NUM_BUCKETS)]


BUCKET_THRESHOLDS = _bucket_thresholds()


def _params(*semantics):
    return pltpu.CompilerParams(dimension_semantics=semantics, vmem_limit_bytes=VMEM_LIMIT)


def _rms_rows(x, g):
    ms = jnp.mean(x * x, axis=-1, keepdims=True)
    return x * lax.rsqrt(ms + EPS) * g


NORM_ROWS = 256


def _inproj_kernel(x_ref, g_ref, w_ref, side_ref, z_ref, side_bf16_ref, xn_ref):
    _inproj_norm(pl.program_id(1) == 0, x_ref, g_ref, xn_ref)
    _inproj_columns(w_ref, z_ref, xn_ref)
    side_bf16_ref[...] = side_ref[...].astype(BF16)


BF16_ROWS = 16


def _side_cast_rows(rows, nsteps):
    per_step = -(-rows // nsteps)
    per_step = -(-per_step // BF16_ROWS) * BF16_ROWS
    assert rows % per_step == 0
    return per_step


def _inproj_norm(first_column_step, x_ref, g_ref, xn_ref):
    @pl.when(first_column_step)
    def _():
        def body(r, carry):
            rows = pl.ds(pl.multiple_of(r * NORM_ROWS, NORM_ROWS), NORM_ROWS)
            xn_ref[rows, :] = _rms_rows(x_ref[rows, :], g_ref[0:1, :]).astype(xn_ref.dtype)
            return carry

        lax.fori_loop(0, x_ref.shape[0] // NORM_ROWS, body, 0)


def _inproj_columns(w_ref, z_ref, xn_ref):
    z_ref[...] = jnp.dot(xn_ref[...], w_ref[...].astype(BF16), preferred_element_type=F32).astype(z_ref.dtype)


def _in_proj(x, g, w, side, tm, tn):
    m = x.shape[0]
    n = w.shape[1]
    gate_start = n - 2 * D_MODEL
    assert gate_start % tn == 0 and n % tn == 0
    gate_block, nblocks = gate_start // tn, n // tn
    side_rows = _side_cast_rows(side.shape[0], (m // tm) * nblocks)
    side_spec = pl.BlockSpec((side_rows, side.shape[1]),
                             lambda i, j: (jnp.minimum(i * nblocks + j, side.shape[0] // side_rows - 1), 0))
    return pl.pallas_call(
        _inproj_kernel,
        grid=(m // tm, nblocks),
        in_specs=[
            pl.BlockSpec((tm, D_MODEL), lambda i, j: (i, 0)),
            pl.BlockSpec((1, D_MODEL),lambda i, j: (0, 0)),
            pl.BlockSpec((D_MODEL, tn), lambda i, j: (0, (j + gate_block) % nblocks)),
            side_spec,
        ],
        out_specs=[pl.BlockSpec((tm, tn), lambda i, j: (i, j)), side_spec],
        out_shape=[jax.ShapeDtypeStruct((m, n), BF16), jax.ShapeDtypeStruct(side.shape, BF16)],
        scratch_shapes=[pltpu.VMEM((tm, D_MODEL), BF16)],
        compiler_params=_params("arbitrary", "arbitrary"),
        name="in_proj",
    )(x, g, w, side)


def _bias_from_distance(dist, head, relb_ref):
    valid = (dist >= 0) & (dist < WINDOW)
    d = jnp.clip(dist, 0, MAX_DISTANCE - 1)
    bucket = jnp.minimum(d, NUM_BUCKETS // 2)
    for thr in BUCKET_THRESHOLDS:
        bucket = bucket + (d >= thr).astype(jnp.int32)
    bias = jnp.zeros(dist.shape, F32)
    for b in range(NUM_BUCKETS):
        bias = jnp.where(bucket == b, relb_ref[b, head], bias)
    return jnp.where(valid, bias, NEG)


def _biased_scores(s, bias):
    return jnp.where(bias > 0.5 * NEG, s + bias, NEG)


def _head_in_group_columns(q, h):
    half = D_KV // 2
    assert 2 * HEAD_DIM == half and GROUP * HEAD_DIM == D_KV
    g = h // GROUP
    src = q[:, (h // 2) * half:(h // 2 + 1) * half]
    if h % 2 != g % 2:
        src = pltpu.roll(src, HEAD_DIM, 1)
    lane_half = lax.broadcasted_iota(jnp.int32, src.shape, 1) // HEAD_DIM
    placed = jnp.where(lane_half == g % 2, src, 0.0)
    zero = jnp.zeros_like(placed)
    return jnp.concatenate([placed, zero] if g // 2 == 0 else [zero, placed], axis=1)


def _group_columns(h):
    g = h // GROUP
    return slice(g * HEAD_DIM, (g + 1) * HEAD_DIM)


INPROJ_FILL_CHUNKS = 4
PROMPT_BLOCKS_PER_STEP = 4


def _inproj_attn_kernel(x_ref, g_ref, w_ref, side0_ref, side1_ref, relb_ref, sink_ref, q_ref, kp_ref, kc_ref,
                        vp_ref, vc_ref, z_ref, o_ref, side0_bf16_ref, side1_bf16_ref,
                        xn_ref, wb_ref, bias_ref, lhs_ref, *, steps_per_seq):
    j = pl.program_id(0)
    _attn_prompt_bias(j == 0, relb_ref, bias_ref)
    _inproj_norm(j == 0, x_ref, g_ref, xn_ref)
    wb_ref[...] = w_ref[...].astype(BF16)
    side0_bf16_ref[...] = side0_ref[...].astype(BF16)
    side1_bf16_ref[...] = side1_ref[...].astype(BF16)
    chunk = xn_ref.shape[0] // INPROJ_FILL_CHUNKS

    def column_chunk(c):
        rows = slice(c * chunk, (c + 1) * chunk)

        def emit():
            z_ref[rows, :] = jnp.dot(xn_ref[rows, :], wb_ref[...], preferred_element_type=F32).astype(z_ref.dtype)
        return emit

    _attn_prompt_blocks(j % steps_per_seq != 0, sink_ref, q_ref, kp_ref, kc_ref, vp_ref, vc_ref,
                        o_ref, bias_ref, lhs_ref, fillers=[column_chunk(c) for c in range(INPROJ_FILL_CHUNKS)])


def _from_prev_mask():
    qi = lax.broadcasted_iota(jnp.int32, (WINDOW, WINDOW), 0)
    kj = lax.broadcasted_iota(jnp.int32, (WINDOW, WINDOW), 1)
    return qi, kj, kj > qi


def _attn_prompt_bias(first_step, relb_ref, bias_ref):
    @pl.when(first_step)
    def _():
        qi, kj, from_prev = _from_prev_mask()
        dist = jnp.where(from_prev, qi + WINDOW - kj, qi - kj)
        for h in range(N_HEADS):
            rows = slice(h * WINDOW, (h + 1) * WINDOW)
            bias = _bias_from_distance(dist, h, relb_ref)
            bias_ref[1, rows, :] = bias
            bias_ref[0, rows, :] = jnp.where(from_prev, NEG, bias)


def _attn_prompt_blocks(has_prev_block, sink_ref, q_ref, kp_ref, kc_ref, vp_ref, vc_ref, o_ref, bias_ref, lhs_ref,
                        fillers=()):
    _, _, from_prev = _from_prev_mask()
    k_all = jnp.concatenate([kp_ref[...], kc_ref[...]], axis=0)
    v_all = jnp.concatenate([vp_ref[...], vc_ref[...]], axis=0)

    def keys_of(blk):
        return slice(blk * WINDOW, (blk + 2) * WINDOW)

    nblk = PROMPT_BLOCKS_PER_STEP
    fillers = list(fillers)
    per_stage = -(-len(fillers) // (nblk + 2))
    scores, probs = {}, {}
    for stage in range(nblk + 2):
        if stage < nblk:
            q = q_ref[stage * WINDOW:(stage + 1) * WINDOW, :]
            scores[stage] = _block_scores(q, k_all[keys_of(stage)], lhs_ref.at[stage])
        for _ in range(min(per_stage, len(fillers))):
            fillers.pop(0)()
        if 0 <= stage - 1 < nblk:
            blk = stage - 1
            has_prev = has_prev_block.astype(jnp.int32) if blk == 0 else 1
            probs[blk] = _block_softmax(scores.pop(blk), has_prev, from_prev, sink_ref, bias_ref)
        if 0 <= stage - 2 < nblk:
            blk = stage - 2
            o_ref[blk * WINDOW:(blk + 1) * WINDOW, :] = _block_output(
                *probs.pop(blk), v_all[keys_of(blk)]).astype(o_ref.dtype)


def _block_scores(q, kk, lhs_ref):
    q = q.astype(F32) * (HEAD_DIM ** -0.5)
    for h in range(N_HEADS):
        lhs_ref[h * WINDOW:(h + 1) * WINDOW, :] = _head_in_group_columns(q, h).astype(BF16)
    return lax.dot_general(lhs_ref[...], kk, (((1,), (1,)), ((), ())), preferred_element_type=F32)


def _block_softmax(s, has_prev, from_prev, sink_ref, bias_ref):
    probs, sink_terms = [], []
    for h in range(N_HEADS):
        rows = slice(h * WINDOW, (h + 1) * WINDOW)
        sh = _biased_scores(jnp.where(from_prev, s[rows, :WINDOW], s[rows, WINDOW:]), bias_ref[has_prev, rows, :])
        m = jnp.maximum(jnp.max(sh, axis=-1, keepdims=True), sink_ref[h])
        p = jnp.exp(sh - m)
        probs.append(jnp.concatenate([jnp.where(from_prev, p, 0.0), jnp.where(from_prev, 0.0, p)],
                                     axis=1).astype(BF16))
        sink_terms.append(jnp.exp(sink_ref[h] - m))
    return probs, sink_terms


def _block_output(probs, sink_terms, vv):
    assert 2 * HEAD_DIM == LANES and GROUP % 2 == 0
    ones = jnp.ones((vv.shape[0], HEAD_DIM), BF16)
    low_half = lax.broadcasted_iota(jnp.int32, (WINDOW, LANES), 1) < HEAD_DIM
    outs = []
    for g in range(N_KV_HEADS):
        vg = vv[:, g * HEAD_DIM:(g + 1) * HEAD_DIM]
        even_heads = [g * GROUP + r for r in range(0, GROUP, 2)]
        pe = jnp.concatenate([probs[h] for h in even_heads], axis=0)
        po = jnp.concatenate([probs[h + 1] for h in even_heads], axis=0)
        oe = jnp.dot(pe, jnp.concatenate([vg, ones], axis=1), preferred_element_type=F32)
        oo = jnp.dot(po, jnp.concatenate([ones, vg], axis=1), preferred_element_type=F32)
        for k, h in enumerate(even_heads):
            a = oe[k * WINDOW:(k + 1) * WINDOW]
            b = oo[k * WINDOW:(k + 1) * WINDOW]
            out_pair = jnp.where(low_half, a, b)
            sum_pair = pltpu.roll(jnp.where(low_half, b, a), HEAD_DIM, 1)
            sink_pair = jnp.where(low_half, sink_terms[h], sink_terms[h + 1])
            outs.append(out_pair * (1.0 / (sum_pair + sink_pair)))
    return jnp.concatenate(outs, axis=1)


def _in_proj_sample_attn_prompt(x, g, w, side0, side1, zp, rel_bias, sinks, tn, batch, seq):
    tm = x.shape[0]
    n = w.shape[1]
    gate_start = n - 2 * D_MODEL
    assert gate_start % tn == 0 and n % tn == 0
    gate_block, nsteps = gate_start // tn, n // tn
    bps = PROMPT_BLOCKS_PER_STEP
    assert seq % (bps * WINDOW) == 0
    steps_per_seq = seq // (bps * WINDOW)
    attn_steps = batch * steps_per_seq
    assert attn_steps <= nsteps

    def astep(j):
        return jnp.minimum(j, attn_steps - 1)

    def cur(off, width):
        return lambda j: (astep(j), off // width)

    def prev(off, width):
        return lambda j: (astep(j) * bps - jnp.minimum(astep(j) % steps_per_seq, 1), off // width)

    def side_spec(side):
        rows = _side_cast_rows(side.shape[0], nsteps)
        return pl.BlockSpec((rows, side.shape[1]), lambda j: (jnp.minimum(j, side.shape[0] // rows - 1), 0))

    smem = pl.BlockSpec(memory_space=pltpu.SMEM)
    return pl.pallas_call(
        functools.partial(_inproj_attn_kernel, steps_per_seq=steps_per_seq),
        grid=(nsteps,),
        in_specs=[
            pl.BlockSpec((tm, D_MODEL), lambda j: (0, 0)),
            pl.BlockSpec((1, D_MODEL),lambda j: (0, 0)),
            pl.BlockSpec((D_MODEL, tn), lambda j: (0, (j + gate_block) % nsteps)),
            side_spec(side0),
            side_spec(side1),
            smem,
            smem,
            pl.BlockSpec((bps * WINDOW, D_ATTN), cur(OFF_Q, D_ATTN)),
            pl.BlockSpec((WINDOW, D_KV), prev(OFF_K, D_KV)),
            pl.BlockSpec((bps * WINDOW, D_KV), cur(OFF_K, D_KV)),
            pl.BlockSpec((WINDOW, D_KV), prev(OFF_V, D_KV)),
            pl.BlockSpec((bps * WINDOW, D_KV), cur(OFF_V, D_KV)),
        ],
        out_specs=[
            pl.BlockSpec((tm, tn), lambda j: (0, j)),
            pl.BlockSpec((bps * WINDOW, D_ATTN), lambda j: (j, 0)),
            side_spec(side0),
            side_spec(side1),
        ],
        out_shape=[
            jax.ShapeDtypeStruct((tm, n), BF16),
            jax.ShapeDtypeStruct((nsteps * bps * WINDOW, D_ATTN), BF16),
            jax.ShapeDtypeStruct(side0.shape, BF16),
            jax.ShapeDtypeStruct(side1.shape, BF16),
        ],
        scratch_shapes=[pltpu.VMEM((tm, D_MODEL), BF16),
                        pltpu.VMEM((D_MODEL, tn), BF16),
                        pltpu.VMEM((2, N_HEADS * WINDOW, WINDOW), F32),
                        pltpu.VMEM((bps, N_HEADS * WINDOW, D_KV), BF16)],
        compiler_params=_params("arbitrary"),
        name="in_proj_sample_attn_prompt",
    )(x, g, w, side0, side1, rel_bias, sinks, zp, zp, zp, zp, zp)


SAMPLE_BATCH_TILE = 16


def _attn_sample_kernel(relb_ref, sink_ref, q_ref, kn_ref, vn_ref, ckt_ref, cvt_ref,
                        o_ref, nk_ref, nv_ref,
                        bias1_ref, bias2_ref, sinkv_ref, z_ref, s_ref, p_ref, oh_ref, *, steps):
    bt, wc, _ = nk_ref.shape
    tile_rows = bt * steps

    @pl.when(pl.program_id(0) == 0)
    def _():
        t = lax.broadcasted_iota(jnp.int32, (steps, wc), 0)
        j = lax.broadcasted_iota(jnp.int32, (steps, wc), 1)
        t2 = lax.broadcasted_iota(jnp.int32, (steps, tile_rows), 0)
        j2 = lax.broadcasted_iota(jnp.int32, (steps, tile_rows), 1)
        own = jnp.where(j2 < steps, t2 - j2, -1)
        for h in range(N_HEADS):
            sl = slice(h * steps, (h + 1) * steps)
            bias1_ref[sl, :] = _bias_from_distance(t + wc - j, h, relb_ref)
            bias2_ref[0, sl, :] = _bias_from_distance(own, h, relb_ref)
            sinkv_ref[sl, :] = jnp.full((steps, LANES), sink_ref[h], F32)
        for b in range(1, bt):
            bias2_ref[b] = pltpu.roll(bias2_ref[0], b * steps, 1)

    kn = kn_ref[...]
    vn = vn_ref[...]
    nk_ref[:, wc - steps:wc, :] = kn.astype(F32).reshape(bt, steps, D_KV)
    nv_ref[:, wc - steps:wc, :] = vn.astype(F32).reshape(bt, steps, D_KV)

    q = q_ref[...].astype(F32) * (HEAD_DIM ** -0.5)
    for h in range(N_HEADS):
        z_ref[:, h * steps:(h + 1) * steps, :] = _head_in_group_columns(q, h).reshape(bt, steps, D_KV)

    for b in range(bt):
        ck = ckt_ref[b].T
        nk_ref[b, 0:wc - steps, :] = ck[steps:, :]
        keys = jnp.concatenate([ck.astype(BF16), kn], axis=0)
        s_ref[b] = lax.dot_general(z_ref[b].astype(BF16), keys, (((1,), (1,)), ((), ())),
                                   preferred_element_type=F32)

    bias = jnp.concatenate([jnp.broadcast_to(bias1_ref[...], (bt,) + bias1_ref.shape), bias2_ref[...]], axis=2)
    s = _biased_scores(s_ref[...], bias)
    sink = sinkv_ref[...][None]
    m = jnp.maximum(jnp.max(s, axis=-1, keepdims=True), sink)
    p_ref[...] = jnp.exp(s - jnp.concatenate([m] * (s.shape[-1] // LANES), axis=-1)).astype(BF16)
    sink_term = jnp.exp(sink - m)

    vn_ones = jnp.concatenate([vn, jnp.ones((tile_rows, LANES), BF16)], axis=1)
    for b in range(bt):
        cv = cvt_ref[b].T
        nv_ref[b, 0:wc - steps, :] = cv[steps:, :]
        cv_ones = jnp.concatenate([cv.astype(BF16), jnp.ones((wc, LANES), BF16)], axis=1)
        oh_ref[b] = jnp.dot(p_ref[b], jnp.concatenate([cv_ones, vn_ones], axis=0), preferred_element_type=F32)

    inv = 1.0 / (oh_ref[:, :, D_KV:] + sink_term)
    outs = []
    for h in range(N_HEADS):
        rows = slice(h * steps, (h + 1) * steps)
        outs.append((oh_ref[:, rows, _group_columns(h)] * inv[:, rows, :HEAD_DIM]).reshape(tile_rows, HEAD_DIM))
    o_ref[...] = jnp.concatenate(outs, axis=1).astype(o_ref.dtype)


def _attn_sample(z, cache_kt, cache_vt, rel_bias, sinks, batch, steps):
    wc = cache_kt.shape[2]
    bt = SAMPLE_BATCH_TILE
    rows = bt * steps
    smem = pl.BlockSpec(memory_space=pltpu.SMEM)
    cache_t_spec = pl.BlockSpec((bt, D_KV, wc), lambda i: (i, 0, 0))
    cache_spec = pl.BlockSpec((bt, wc, D_KV), lambda i: (i, 0, 0))
    return pl.pallas_call(
        functools.partial(_attn_sample_kernel, steps=steps),
        grid=(batch // bt,),
        in_specs=[
            smem,
            smem,
            pl.BlockSpec((rows, D_ATTN), lambda i: (i, OFF_Q // D_ATTN)),
            pl.BlockSpec((rows, D_KV), lambda i: (i, OFF_K // D_KV)),
            pl.BlockSpec((rows, D_KV), lambda i: (i, OFF_V // D_KV)),
            cache_t_spec,
            cache_t_spec,
        ],
        out_specs=[
            pl.BlockSpec((rows, D_ATTN), lambda i: (i, 0)),
            cache_spec,
            cache_spec,
        ],
        out_shape=[
            jax.ShapeDtypeStruct((batch * steps, D_ATTN), BF16),
            jax.ShapeDtypeStruct((batch, wc, D_KV), F32),
            jax.ShapeDtypeStruct((batch, wc, D_KV), F32),
        ],
        scratch_shapes=[
            pltpu.VMEM((N_HEADS * steps, wc), F32),
            pltpu.VMEM((bt, N_HEADS * steps, rows), F32),
            pltpu.VMEM((N_HEADS * steps, LANES), F32),
            pltpu.VMEM((bt, N_HEADS * steps, D_KV), F32),
            pltpu.VMEM((bt, N_HEADS * steps, wc + rows), F32),
            pltpu.VMEM((bt, N_HEADS * steps, wc + rows), BF16),
            pltpu.VMEM((bt, N_HEADS * steps, D_KV + LANES), F32),
        ],
        compiler_params=_params("arbitrary"),
        name="attn_sample",
    )(rel_bias, sinks, z, z, z, cache_kt, cache_vt)


CARRY_ROWS = 8


def _gated_conv(b, wc_ref, u, u1, u2):
    conv = wc_ref[0:1, :] * u2 + wc_ref[1:2, :] * u1 + wc_ref[2:3, :] * u
    return (b.astype(F32) * conv).astype(BF16)


def _mix_tail(conv_branch_input, at_ref, gc_ref, ga_ref, x_ref, g_ref, wco_ref, wao_ref, wo_ref, h_ref, hn_ref):
    y_conv = jnp.dot(conv_branch_input(), wco_ref[...], preferred_element_type=F32)
    y_attn = jnp.dot(at_ref[...], wao_ref[...], preferred_element_type=F32)
    merged = (jax.nn.sigmoid(gc_ref[...].astype(F32)) * y_conv
              + jax.nn.sigmoid(ga_ref[...].astype(F32)) * y_attn)
    h = x_ref[...] + jnp.dot(merged.astype(BF16), wo_ref[...], preferred_element_type=F32)
    h_ref[...] = h
    hn_ref[...] = _rms_rows(h, g_ref[0:1, :]).astype(hn_ref.dtype)


def _delayed_from_scratch(us_ref, rows):
    def at(delay):
        return us_ref[CARRY_ROWS - delay + rows.start:CARRY_ROWS - delay + rows.stop, :]
    return at(0), at(1), at(2)


def _mix_prompt_kernel(b_ref, c_ref, hc_ref, wc_ref, at_ref, gc_ref, ga_ref, x_ref, g_ref,
                       wco_ref, wao_ref, wo_ref, h_ref, hn_ref, st_ref, us_ref, *, tiles_per_seq):
    tm = x_ref.shape[0]

    @pl.when(pl.program_id(0) % tiles_per_seq == 0)
    def _():
        us_ref[0:CARRY_ROWS, :] = jnp.zeros((CARRY_ROWS, D_CONV), F32)

    def conv_branch_input():
        u = c_ref[...].astype(F32) * hc_ref[...].astype(F32)
        us_ref[CARRY_ROWS:, :] = u
        st_ref[...] = u[tm - (CONV_WIDTH - 1):, :]
        yc = _gated_conv(b_ref[...], wc_ref, *_delayed_from_scratch(us_ref, slice(0, tm)))
        us_ref[0:CARRY_ROWS, :] = us_ref[tm:tm + CARRY_ROWS, :]
        return yc

    _mix_tail(conv_branch_input, at_ref, gc_ref, ga_ref, x_ref, g_ref, wco_ref, wao_ref, wo_ref, h_ref, hn_ref)


def _mix_sample_kernel(b_ref, c_ref, hc_ref, st_ref, wc_ref, at_ref, gc_ref, ga_ref, x_ref, g_ref,
                       wco_ref, wao_ref, wo_ref, h_ref, hn_ref, new_st_ref, us_ref, *, steps):
    tm = x_ref.shape[0]
    nseq = tm // steps
    keep = CONV_WIDTH - 1

    def conv_branch_input():
        u = c_ref[...].astype(F32) * hc_ref[...].astype(F32)
        us_ref[:, CARRY_ROWS - keep:CARRY_ROWS, :] = st_ref[...]
        us_ref[:, CARRY_ROWS:, :] = u.reshape(nseq, steps, D_CONV)
        new_st_ref[...] = us_ref[:, CARRY_ROWS + steps - keep:, :]

        def delayed(d):
            return us_ref[:, CARRY_ROWS - d:CARRY_ROWS - d + steps, :].reshape(tm, D_CONV)

        return _gated_conv(b_ref[...], wc_ref, u, delayed(1), delayed(2))

    _mix_tail(conv_branch_input, at_ref, gc_ref, ga_ref, x_ref, g_ref, wco_ref, wao_ref, wo_ref, h_ref, hn_ref)


def _mix_specs(tm):
    resident = functools.partial(pl.BlockSpec, index_map=lambda i: (0, 0), pipeline_mode=pl.Buffered(1))
    row_spec = pl.BlockSpec((tm, D_MODEL), lambda i: (i, 0))

    def zcol(off, width):
        return pl.BlockSpec((tm, width), lambda i: (i, off // width))

    conv_in = [zcol(OFF_B, D_CONV), zcol(OFF_C, D_CONV), zcol(OFF_H, D_CONV)]
    rest_in = [
        pl.BlockSpec((CONV_WIDTH, D_CONV), lambda i: (0, 0)),
        pl.BlockSpec((tm, D_ATTN), lambda i: (i, 0)),
        zcol(OFF_GC, D_MODEL),
        zcol(OFF_GA, D_MODEL),
        row_spec,
        pl.BlockSpec((1, D_MODEL),lambda i: (0, 0)),
        resident((D_CONV, D_MODEL)),
        resident((D_ATTN, D_MODEL)),
        resident((D_MODEL, D_MODEL)),
    ]
    return conv_in, rest_in, row_spec


def _mix_prompt(z, attn, x, w_conv, g_ffn, wco, wao, wo, tm, batch, seq):
    m = x.shape[0]
    assert seq % tm == 0 and tm >= CARRY_ROWS
    tiles_per_seq = seq // tm
    conv_in, rest_in, row_spec = _mix_specs(tm)
    return pl.pallas_call(
        functools.partial(_mix_prompt_kernel, tiles_per_seq=tiles_per_seq),
        grid=(m // tm,),
        in_specs=conv_in + rest_in,
        out_specs=[row_spec, row_spec,
                   pl.BlockSpec((None, CONV_WIDTH - 1, D_CONV), lambda i: (i // tiles_per_seq, 0, 0))],
        out_shape=[jax.ShapeDtypeStruct((m, D_MODEL), F32), jax.ShapeDtypeStruct((m, D_MODEL), BF16),
                   jax.ShapeDtypeStruct((batch, CONV_WIDTH - 1, D_CONV), F32)],
        scratch_shapes=[pltpu.VMEM((tm + CARRY_ROWS, D_CONV), F32)],
        compiler_params=_params("arbitrary"),
        name="mix_prompt",
    )(z, z, z, w_conv, attn, z, z, x, g_ffn, wco, wao, wo)


def _mix_sample(z, state, attn, x, w_conv, g_ffn, wco, wao, wo, tm, steps):
    m = x.shape[0]
    assert tm % steps == 0 and CONV_WIDTH - 1 <= min(steps, CARRY_ROWS)
    conv_in, rest_in, row_spec = _mix_specs(tm)
    state_spec = pl.BlockSpec((tm // steps, CONV_WIDTH - 1, D_CONV), lambda i: (i, 0, 0))
    return pl.pallas_call(
        functools.partial(_mix_sample_kernel, steps=steps),
        grid=(m // tm,),
        in_specs=conv_in + [state_spec] + rest_in,
        out_specs=[row_spec, row_spec, state_spec],
        out_shape=[jax.ShapeDtypeStruct((m, D_MODEL), F32), jax.ShapeDtypeStruct((m, D_MODEL), BF16),
                   jax.ShapeDtypeStruct(state.shape, F32)],
        scratch_shapes=[pltpu.VMEM((tm // steps, CARRY_ROWS + steps, D_CONV), F32)],
        compiler_params=_params("arbitrary"),
        name="mix_sample",
    )(z, z, z, state, w_conv, attn, z, z, x, g_ffn, wco, wao, wo)


FFN_ROW_CHUNKS = 4


def _ffn_kernel(hn_ref, h_ref, gf_ref, wg_ref, wu_ref, wd_ref, y_ref, wgb_ref, wub_ref, wdb_ref, *, final_norm):
    j = pl.program_id(1)

    @pl.when(j == 0)
    def _():
        y_ref[...] = h_ref[...]

    wgb_ref[...] = wg_ref[...].astype(BF16)
    wub_ref[...] = wu_ref[...].astype(BF16)
    wdb_ref[...] = wd_ref[...].astype(BF16)
    chunk = y_ref.shape[0] // FFN_ROW_CHUNKS
    gate_up = {}
    for t in range(FFN_ROW_CHUNKS + 1):
        if t < FFN_ROW_CHUNKS:
            hn = hn_ref[t * chunk:(t + 1) * chunk, :]
            gate_up[t] = (jnp.dot(hn, wgb_ref[...], preferred_element_type=F32),
                          jnp.dot(hn, wub_ref[...], preferred_element_type=F32))
        if t >= 1:
            rows = slice((t - 1) * chunk, t * chunk)
            g, u = gate_up.pop(t - 1)
            a = (jax.nn.silu(g) * u).astype(BF16)
            y_ref[rows, :] += jnp.dot(a, wdb_ref[...], preferred_element_type=F32)

    if final_norm:
        @pl.when(j == pl.num_programs(1) - 1)
        def _():
            def body(r, carry):
                rows = pl.ds(pl.multiple_of(r * NORM_ROWS, NORM_ROWS), NORM_ROWS)
                y_ref[rows, :] = _rms_rows(y_ref[rows, :], gf_ref[0:1, :])
                return carry

            lax.fori_loop(0, y_ref.shape[0] // NORM_ROWS, body, 0)


def _ffn(hn, h, g_final, wg, wu, wd, tm, tf, final_norm):
    m = h.shape[0]
    return pl.pallas_call(
        functools.partial(_ffn_kernel, final_norm=final_norm),
        grid=(m // tm, D_FF // tf),
        in_specs=[
            pl.BlockSpec((tm, D_MODEL), lambda i, j: (i, 0)),
            pl.BlockSpec((tm, D_MODEL), lambda i, j: (i, 0)),
            pl.BlockSpec((1, D_MODEL),lambda i, j: (0, 0)),
            pl.BlockSpec((D_MODEL, tf), lambda i, j: (0, j)),
            pl.BlockSpec((D_MODEL, tf), lambda i, j: (0, j)),
            pl.BlockSpec((tf, D_MODEL), lambda i, j: (j, 0)),
        ],
        out_specs=pl.BlockSpec((tm, D_MODEL), lambda i, j: (i, 0)),
        out_shape=jax.ShapeDtypeStruct((m, D_MODEL), F32),
        scratch_shapes=[pltpu.VMEM((D_MODEL, tf), BF16), pltpu.VMEM((D_MODEL, tf), BF16),
                        pltpu.VMEM((tf, D_MODEL), BF16)],
        compiler_params=_params("arbitrary", "arbitrary"),
        name="ffn",
    )(hn, h, g_final, wg, wu, wd)


IN_PROJ_TM_PROMPT = 2048
IN_PROJ_TN = 512
MIX_TM = 256
FFN_TM = 1024
FFN_TF = 256


def kernel(x_prompt, x_sample, cache_k, cache_v, state_conv, rel_bias, w_in, w_conv, w_conv_out, sinks,
           w_attn_out, w_o, g_mix, g_ffn, w_gate, w_up, w_down, g_final):
    depth = w_in.shape[0]
    batch, seq, _ = x_prompt.shape
    dec_batch, steps, _ = x_sample.shape
    wc = cache_k.shape[2]
    assert seq % WINDOW == 0 and wc == WINDOW and dec_batch % SAMPLE_BATCH_TILE == 0

    hp = x_prompt.reshape(batch * seq, D_MODEL)
    hs = x_sample.reshape(dec_batch * steps, D_MODEL)
    g_final2 = g_final.reshape(1, D_MODEL)
    kp_l, vp_l, cp_l, ks_l, vs_l, cs_l = [], [], [], [], [], []
    for l in range(depth):
        final = l == depth - 1
        lw = {
            "w_in": w_in[l],
            "wg": w_gate[l],
            "wu": w_up[l],
            "wd": w_down[l],
            "g_ffn": g_ffn[l].reshape(1, D_MODEL),
        }
        g_mix_l = g_mix[l].reshape(1, D_MODEL)

        zp, lw["wo"] = _in_proj(hp, g_mix_l, lw["w_in"], w_o[l], IN_PROJ_TM_PROMPT, IN_PROJ_TN)
        zs, attn_p, lw["wco"], lw["wao"] = _in_proj_sample_attn_prompt(
            hs, g_mix_l, lw["w_in"], w_conv_out[l], w_attn_out[l], zp, rel_bias, sinks[l], IN_PROJ_TN, batch, seq)
        hp, hnp, conv_p = _mix_prompt(zp, attn_p, hp, w_conv[l], lw["g_ffn"], lw["wco"], lw["wao"], lw["wo"],
                                      MIX_TM, batch, seq)
        hp = _ffn(hnp, hp, g_final2, lw["wg"], lw["wu"], lw["wd"], FFN_TM, FFN_TF, final)
        kv_p = zp.reshape(batch, seq, D_IN_PROJ)[:, seq - WINDOW:, OFF_K:OFF_V + D_KV].astype(F32)
        kp_l.append(kv_p[..., :D_KV].reshape(batch, WINDOW, N_KV_HEADS, HEAD_DIM))
        vp_l.append(kv_p[..., D_KV:].reshape(batch, WINDOW, N_KV_HEADS, HEAD_DIM))
        cp_l.append(conv_p)

        cache_kt = jnp.transpose(cache_k[l], (0, 2, 3, 1)).reshape(dec_batch, D_KV, wc)
        cache_vt = jnp.transpose(cache_v[l], (0, 2, 3, 1)).reshape(dec_batch, D_KV, wc)
        attn_s, nk, nv = _attn_sample(zs, cache_kt, cache_vt, rel_bias, sinks[l], dec_batch, steps)
        hs, hns, conv_s = _mix_sample(zs, state_conv[l], attn_s, hs, w_conv[l], lw["g_ffn"], lw["wco"], lw["wao"],
                                      lw["wo"], MIX_TM, steps)
        hs = _ffn(hns, hs, g_final2, lw["wg"], lw["wu"], lw["wd"], FFN_TM, FFN_TF, final)
        ks_l.append(nk.reshape(dec_batch, wc, N_KV_HEADS, HEAD_DIM))
        vs_l.append(nv.reshape(dec_batch, wc, N_KV_HEADS, HEAD_DIM))
        cs_l.append(conv_s)

    return (hp.reshape(batch, seq, D_MODEL), hs.reshape(dec_batch, steps, D_MODEL),
            jnp.stack(kp_l), jnp.stack(vp_l), jnp.stack(cp_l),
            jnp.stack(ks_l), jnp.stack(vs_l), jnp.stack(cs_l))
```

```python
import functools
import math

import numpy as np
import jax
import jax.numpy as jnp
from jax import lax
from jax.experimental import pallas as pl
from jax.experimental.pallas import tpu as pltpu

D_MODEL = 2048
D_CONV = D_MODEL // 2
CONV_WIDTH = 3
HEAD_DIM = 64
N_HEADS = (D_MODEL // 2) // HEAD_DIM
N_KV_HEADS = N_HEADS // 4
GROUP = N_HEADS // N_KV_HEADS
D_ATTN = N_HEADS * HEAD_DIM
D_KV = N_KV_HEADS * HEAD_DIM
WINDOW = 128
NUM_BUCKETS = 32
MAX_DISTANCE = 128
D_FF = -(-8 * D_MODEL // (3 * 256)) * 256
EPS = 1e-6
D_IN_PROJ = 3 * D_CONV + D_ATTN + 2 * D_KV + 2 * D_MODEL

OFF_GC = 0
OFF_GA = D_MODEL
OFF_B = 2 * D_MODEL
OFF_C = OFF_B + D_CONV
OFF_H = OFF_C + D_CONV
OFF_Q = OFF_H + D_CONV
OFF_K = OFF_Q + D_ATTN
OFF_V = OFF_K + D_KV

LANES = 128
NEG = -1e30
F32 = jnp.float32
BF16 = jnp.bfloat16
VMEM_LIMIT = 60 * 1024 * 1024


def _bucket_thresholds():
    max_exact = NUM_BUCKETS // 2
    d = np.arange(MAX_DISTANCE)
    ratio = np.log(np.maximum(d, 1).astype(np.float32) / np.float32(max_exact)) / np.float32(
        math.log(MAX_DISTANCE / max_exact))
    large = max_exact + (ratio * np.float32(NUM_BUCKETS - max_exact)).astype(np.int32)
    large = np.minimum(large, NUM_BUCKETS - 1)
    return [int(np.min(d[(d >= max_exact) & (large >= b)])) for b in range(max_exact + 1, NUM_BUCKETS)]


BUCKET_THRESHOLDS = _bucket_thresholds()


def _params(*semantics):
    return pltpu.CompilerParams(dimension_semantics=semantics, vmem_limit_bytes=VMEM_LIMIT)


def _rms_rows(x, g):
    ms = jnp.mean(x * x, axis=-1, keepdims=True)
    return x * lax.rsqrt(ms + EPS) * g


NORM_ROWS = 256


def _inproj_kernel(x_ref, g_ref, w_ref, side_ref, z_ref, side_bf16_ref, xn_ref):
    _inproj_norm(pl.program_id(1) == 0, x_ref, g_ref, xn_ref)
    _inproj_columns(w_ref, z_ref, xn_ref)
    side_bf16_ref[...] = side_ref[...].astype(BF16)


BF16_ROWS = 16


def _side_cast_rows(rows, nsteps):
    per_step = -(-rows // nsteps)
    per_step = -(-per_step // BF16_ROWS) * BF16_ROWS
    assert rows % per_step == 0
    return per_step


def _inproj_norm(first_column_step, x_ref, g_ref, xn_ref):
    @pl.when(first_column_step)
    def _():
        def body(r, carry):
            rows = pl.ds(pl.multiple_of(r * NORM_ROWS, NORM_ROWS), NORM_ROWS)
            xn_ref[rows, :] = _rms_rows(x_ref[rows, :], g_ref[0:1, :]).astype(xn_ref.dtype)
            return carry

        lax.fori_loop(0, x_ref.shape[0] // NORM_ROWS, body, 0)


def _inproj_columns(w_ref, z_ref, xn_ref):
    z_ref[...] = jnp.dot(xn_ref[...], w_ref[...].astype(BF16), preferred_element_type=F32).astype(z_ref.dtype)


def _in_proj(x, g, w, side, tm, tn):
    m = x.shape[0]
    n = w.shape[1]
    gate_start = n - 2 * D_MODEL
    assert gate_start % tn == 0 and n % tn == 0
    gate_block, nblocks = gate_start // tn, n // tn
    side_rows = _side_cast_rows(side.shape[0], (m // tm) * nblocks)
    side_spec = pl.BlockSpec((side_rows, side.shape[1]),
                             lambda i, j: (jnp.minimum(i * nblocks + j, side.shape[0] // side_rows - 1), 0))
    return pl.pallas_call(
        _inproj_kernel,
        grid=(m // tm, nblocks),
        in_specs=[
            pl.BlockSpec((tm, D_MODEL), lambda i, j: (i, 0)),
            pl.BlockSpec((1, D_MODEL),lambda i, j: (0, 0)),
            pl.BlockSpec((D_MODEL, tn), lambda i, j: (0, (j + gate_block) % nblocks)),
            side_spec,
        ],
        out_specs=[pl.BlockSpec((tm, tn), lambda i, j: (i, j)), side_spec],
        out_shape=[jax.ShapeDtypeStruct((m, n), BF16), jax.ShapeDtypeStruct(side.shape, BF16)],
        scratch_shapes=[pltpu.VMEM((tm, D_MODEL), BF16)],
        compiler_params=_params("arbitrary", "arbitrary"),
        name="in_proj",
    )(x, g, w, side)


def _bias_from_distance(dist, head, relb_ref):
    valid = (dist >= 0) & (dist < WINDOW)
    d = jnp.clip(dist, 0, MAX_DISTANCE - 1)
    bucket = jnp.minimum(d, NUM_BUCKETS // 2)
    for thr in BUCKET_THRESHOLDS:
        bucket = bucket + (d >= thr).astype(jnp.int32)
    bias = jnp.zeros(dist.shape, F32)
    for b in range(NUM_BUCKETS):
        bias = jnp.where(bucket == b, relb_ref[b, head], bias)
    return jnp.where(valid, bias, NEG)


def _biased_scores(s, bias):
    return jnp.where(bias > 0.5 * NEG, s + bias, NEG)


def _head_in_group_columns(q, h):
    half = D_KV // 2
    assert 2 * HEAD_DIM == half and GROUP * HEAD_DIM == D_KV
    g = h // GROUP
    src = q[:, (h // 2) * half:(h // 2 + 1) * half]
    if h % 2 != g % 2:
        src = pltpu.roll(src, HEAD_DIM, 1)
    lane_half = lax.broadcasted_iota(jnp.int32, src.shape, 1) // HEAD_DIM
    placed = jnp.where(lane_half == g % 2, src, 0.0)
    zero = jnp.zeros_like(placed)
    return jnp.concatenate([placed, zero] if g // 2 == 0 else [zero, placed], axis=1)


def _group_columns(h):
    g = h // GROUP
    return slice(g * HEAD_DIM, (g + 1) * HEAD_DIM)


INPROJ_FILL_CHUNKS = 4
PROMPT_BLOCKS_PER_STEP = 4


def _inproj_attn_kernel(x_ref, g_ref, w_ref, side0_ref, side1_ref, relb_ref, sink_ref, q_ref, kp_ref, kc_ref,
                        vp_ref, vc_ref, z_ref, o_ref, side0_bf16_ref, side1_bf16_ref,
                        xn_ref, wb_ref, bias_ref, lhs_ref, *, steps_per_seq):
    j = pl.program_id(0)
    _attn_prompt_bias(j == 0, relb_ref, bias_ref)
    _inproj_norm(j == 0, x_ref, g_ref, xn_ref)
    wb_ref[...] = w_ref[...].astype(BF16)
    side0_bf16_ref[...] = side0_ref[...].astype(BF16)
    side1_bf16_ref[...] = side1_ref[...].astype(BF16)
    chunk = xn_ref.shape[0] // INPROJ_FILL_CHUNKS

    def column_chunk(c):
        rows = slice(c * chunk, (c + 1) * chunk)

        def emit():
            z_ref[rows, :] = jnp.dot(xn_ref[rows, :], wb_ref[...], preferred_element_type=F32).astype(z_ref.dtype)
        return emit

    _attn_prompt_blocks(j % steps_per_seq != 0, sink_ref, q_ref, kp_ref, kc_ref, vp_ref, vc_ref,
                        o_ref, bias_ref, lhs_ref, fillers=[column_chunk(c) for c in range(INPROJ_FILL_CHUNKS)])


def _from_prev_mask():
    qi = lax.broadcasted_iota(jnp.int32, (WINDOW, WINDOW), 0)
    kj = lax.broadcasted_iota(jnp.int32, (WINDOW, WINDOW), 1)
    return qi, kj, kj > qi


def _attn_prompt_bias(first_step, relb_ref, bias_ref):
    @pl.when(first_step)
    def _():
        qi, kj, from_prev = _from_prev_mask()
        dist = jnp.where(from_prev, qi + WINDOW - kj, qi - kj)
        for h in range(N_HEADS):
            rows = slice(h * WINDOW, (h + 1) * WINDOW)
            bias = _bias_from_distance(dist, h, relb_ref)
            bias_ref[1, rows, :] = bias
            bias_ref[0, rows, :] = jnp.where(from_prev, NEG, bias)


def _attn_prompt_blocks(has_prev_block, sink_ref, q_ref, kp_ref, kc_ref, vp_ref, vc_ref, o_ref, bias_ref, lhs_ref,
                        fillers=()):
    _, _, from_prev = _from_prev_mask()
    k_all = jnp.concatenate([kp_ref[...], kc_ref[...]], axis=0)
    v_all = jnp.concatenate([vp_ref[...], vc_ref[...]], axis=0)

    def keys_of(blk):
        return slice(blk * WINDOW, (blk + 2) * WINDOW)

    nblk = PROMPT_BLOCKS_PER_STEP
    fillers = list(fillers)
    per_stage = -(-len(fillers) // (nblk + 2))
    scores, probs = {}, {}
    for stage in range(nblk + 2):
        if stage < nblk:
            q = q_ref[stage * WINDOW:(stage + 1) * WINDOW, :]
            scores[stage] = _block_scores(q, k_all[keys_of(stage)], lhs_ref.at[stage])
        for _ in range(min(per_stage, len(fillers))):
            fillers.pop(0)()
        if 0 <= stage - 1 < nblk:
            blk = stage - 1
            has_prev = has_prev_block.astype(jnp.int32) if blk == 0 else 1
            probs[blk] = _block_softmax(scores.pop(blk), has_prev, from_prev, sink_ref, bias_ref)
        if 0 <= stage - 2 < nblk:
            blk = stage - 2
            o_ref[blk * WINDOW:(blk + 1) * WINDOW, :] = _block_output(
                *probs.pop(blk), v_all[keys_of(blk)]).astype(o_ref.dtype)


def _block_scores(q, kk, lhs_ref):
    q = q.astype(F32) * (HEAD_DIM ** -0.5)
    for h in range(N_HEADS):
        lhs_ref[h * WINDOW:(h + 1) * WINDOW, :] = _head_in_group_columns(q, h).astype(BF16)
    return lax.dot_general(lhs_ref[...], kk, (((1,), (1,)), ((), ())), preferred_element_type=F32)


def _block_softmax(s, has_prev, from_prev, sink_ref, bias_ref):
    probs, sink_terms = [], []
    for h in range(N_HEADS):
        rows = slice(h * WINDOW, (h + 1) * WINDOW)
        sh = _biased_scores(jnp.where(from_prev, s[rows, :WINDOW], s[rows, WINDOW:]), bias_ref[has_prev, rows, :])
        m = jnp.maximum(jnp.max(sh, axis=-1, keepdims=True), sink_ref[h])
        p = jnp.exp(sh - m)
        probs.append(jnp.concatenate([jnp.where(from_prev, p, 0.0), jnp.where(from_prev, 0.0, p)],
                                     axis=1).astype(BF16))
        sink_terms.append(jnp.exp(sink_ref[h] - m))
    return probs, sink_terms


def _block_output(probs, sink_terms, vv):
    assert 2 * HEAD_DIM == LANES and GROUP % 2 == 0
    ones = jnp.ones((vv.shape[0], HEAD_DIM), BF16)
    low_half = lax.broadcasted_iota(jnp.int32, (WINDOW, LANES), 1) < HEAD_DIM
    outs = []
    for g in range(N_KV_HEADS):
        vg = vv[:, g * HEAD_DIM:(g + 1) * HEAD_DIM]
        even_heads = [g * GROUP + r for r in range(0, GROUP, 2)]
        pe = jnp.concatenate([probs[h] for h in even_heads], axis=0)
        po = jnp.concatenate([probs[h + 1] for h in even_heads], axis=0)
        oe = jnp.dot(pe, jnp.concatenate([vg, ones], axis=1), preferred_element_type=F32)
        oo = jnp.dot(po, jnp.concatenate([ones, vg], axis=1), preferred_element_type=F32)
        for k, h in enumerate(even_heads):
            a = oe[k * WINDOW:(k + 1) * WINDOW]
            b = oo[k * WINDOW:(k + 1) * WINDOW]
            out_pair = jnp.where(low_half, a, b)
            sum_pair = pltpu.roll(jnp.where(low_half, b, a), HEAD_DIM, 1)
            sink_pair = jnp.where(low_half, sink_terms[h], sink_terms[h + 1])
            outs.append(out_pair * (1.0 / (sum_pair + sink_pair)))
    return jnp.concatenate(outs, axis=1)


def _in_proj_sample_attn_prompt(x, g, w, side0, side1, zp, rel_bias, sinks, tn, batch, seq):
    tm = x.shape[0]
    n = w.shape[1]
    gate_start = n - 2 * D_MODEL
    assert gate_start % tn == 0 and n % tn == 0
    gate_block, nsteps = gate_start // tn, n // tn
    bps = PROMPT_BLOCKS_PER_STEP
    assert seq % (bps * WINDOW) == 0
    steps_per_seq = seq // (bps * WINDOW)
    attn_steps = batch * steps_per_seq
    assert attn_steps <= nsteps

    def astep(j):
        return jnp.minimum(j, attn_steps - 1)

    def cur(off, width):
        return lambda j: (astep(j), off // width)

    def prev(off, width):
        return lambda j: (astep(j) * bps - jnp.minimum(astep(j) % steps_per_seq, 1), off // width)

    def side_spec(side):
        rows = _side_cast_rows(side.shape[0], nsteps)
        return pl.BlockSpec((rows, side.shape[1]), lambda j: (jnp.minimum(j, side.shape[0] // rows - 1), 0))

    smem = pl.BlockSpec(memory_space=pltpu.SMEM)
    return pl.pallas_call(
        functools.partial(_inproj_attn_kernel, steps_per_seq=steps_per_seq),
        grid=(nsteps,),
        in_specs=[
            pl.BlockSpec((tm, D_MODEL), lambda j: (0, 0)),
            pl.BlockSpec((1, D_MODEL),lambda j: (0, 0)),
            pl.BlockSpec((D_MODEL, tn), lambda j: (0, (j + gate_block) % nsteps)),
            side_spec(side0),
            side_spec(side1),
            smem,
            smem,
            pl.BlockSpec((bps * WINDOW, D_ATTN), cur(OFF_Q, D_ATTN)),
            pl.BlockSpec((WINDOW, D_KV), prev(OFF_K, D_KV)),
            pl.BlockSpec((bps * WINDOW, D_KV), cur(OFF_K, D_KV)),
            pl.BlockSpec((WINDOW, D_KV), prev(OFF_V, D_KV)),
            pl.BlockSpec((bps * WINDOW, D_KV), cur(OFF_V, D_KV)),
        ],
        out_specs=[
            pl.BlockSpec((tm, tn), lambda j: (0, j)),
            pl.BlockSpec((bps * WINDOW, D_ATTN), lambda j: (j, 0)),
            side_spec(side0),
            side_spec(side1),
        ],
        out_shape=[
            jax.ShapeDtypeStruct((tm, n), BF16),
            jax.ShapeDtypeStruct((nsteps * bps * WINDOW, D_ATTN), BF16),
            jax.ShapeDtypeStruct(side0.shape, BF16),
            jax.ShapeDtypeStruct(side1.shape, BF16),
        ],
        scratch_shapes=[pltpu.VMEM((tm, D_MODEL), BF16),
                        pltpu.VMEM((D_MODEL, tn), BF16),
                        pltpu.VMEM((2, N_HEADS * WINDOW, WINDOW), F32),
                        pltpu.VMEM((bps, N_HEADS * WINDOW, D_KV), BF16)],
        compiler_params=_params("arbitrary"),
        name="in_proj_sample_attn_prompt",
    )(x, g, w, side0, side1, rel_bias, sinks, zp, zp, zp, zp, zp)


SAMPLE_BATCH_TILE = 16


def _attn_sample_kernel(relb_ref, sink_ref, q_ref, kn_ref, vn_ref, ckt_ref, cvt_ref,
                        o_ref, nk_ref, nv_ref,
                        bias1_ref, bias2_ref, sinkv_ref, z_ref, s_ref, p_ref, oh_ref, *, steps):
    bt, wc, _ = nk_ref.shape
    tile_rows = bt * steps

    @pl.when(pl.program_id(0) == 0)
    def _():
        t = lax.broadcasted_iota(jnp.int32, (steps, wc), 0)
        j = lax.broadcasted_iota(jnp.int32, (steps, wc), 1)
        t2 = lax.broadcasted_iota(jnp.int32, (steps, tile_rows), 0)
        j2 = lax.broadcasted_iota(jnp.int32, (steps, tile_rows), 1)
        own = jnp.where(j2 < steps, t2 - j2, -1)
        for h in range(N_HEADS):
            sl = slice(h * steps, (h + 1) * steps)
            bias1_ref[sl, :] = _bias_from_distance(t + wc - j, h, relb_ref)
            bias2_ref[0, sl, :] = _bias_from_distance(own, h, relb_ref)
            sinkv_ref[sl, :] = jnp.full((steps, LANES), sink_ref[h], F32)
        for b in range(1, bt):
            bias2_ref[b] = pltpu.roll(bias2_ref[0], b * steps, 1)

    kn = kn_ref[...]
    vn = vn_ref[...]
    nk_ref[:, wc - steps:wc, :] = kn.astype(F32).reshape(bt, steps, D_KV)
    nv_ref[:, wc - steps:wc, :] = vn.astype(F32).reshape(bt, steps, D_KV)

    q = q_ref[...].astype(F32) * (HEAD_DIM ** -0.5)
    for h in range(N_HEADS):
        z_ref[:, h * steps:(h + 1) * steps, :] = _head_in_group_columns(q, h).reshape(bt, steps, D_KV)

    for b in range(bt):
        ck = ckt_ref[b].T
        nk_ref[b, 0:wc - steps, :] = ck[steps:, :]
        keys = jnp.concatenate([ck.astype(BF16), kn], axis=0)
        s_ref[b] = lax.dot_general(z_ref[b].astype(BF16), keys, (((1,), (1,)), ((), ())),
                                   preferred_element_type=F32)

    bias = jnp.concatenate([jnp.broadcast_to(bias1_ref[...], (bt,) + bias1_ref.shape), bias2_ref[...]], axis=2)
    s = _biased_scores(s_ref[...], bias)
    sink = sinkv_ref[...][None]
    m = jnp.maximum(jnp.max(s, axis=-1, keepdims=True), sink)
    p_ref[...] = jnp.exp(s - jnp.concatenate([m] * (s.shape[-1] // LANES), axis=-1)).astype(BF16)
    sink_term = jnp.exp(sink - m)

    vn_ones = jnp.concatenate([vn, jnp.ones((tile_rows, LANES), BF16)], axis=1)
    for b in range(bt):
        cv = cvt_ref[b].T
        nv_ref[b, 0:wc - steps, :] = cv[steps:, :]
        cv_ones = jnp.concatenate([cv.astype(BF16), jnp.ones((wc, LANES), BF16)], axis=1)
        oh_ref[b] = jnp.dot(p_ref[b], jnp.concatenate([cv_ones, vn_ones], axis=0), preferred_element_type=F32)

    inv = 1.0 / (oh_ref[:, :, D_KV:] + sink_term)
    outs = []
    for h in range(N_HEADS):
        rows = slice(h * steps, (h + 1) * steps)
        outs.append((oh_ref[:, rows, _group_columns(h)] * inv[:, rows, :HEAD_DIM]).reshape(tile_rows, HEAD_DIM))
    o_ref[...] = jnp.concatenate(outs, axis=1).astype(o_ref.dtype)


def _attn_sample(z, cache_kt, cache_vt, rel_bias, sinks, batch, steps):
    wc = cache_kt.shape[2]
    bt = SAMPLE_BATCH_TILE
    rows = bt * steps
    smem = pl.BlockSpec(memory_space=pltpu.SMEM)
    cache_t_spec = pl.BlockSpec((bt, D_KV, wc), lambda i: (i, 0, 0))
    cache_spec = pl.BlockSpec((bt, wc, D_KV), lambda i: (i, 0, 0))
    return pl.pallas_call(
        functools.partial(_attn_sample_kernel, steps=steps),
        grid=(batch // bt,),
        in_specs=[
            smem,
            smem,
            pl.BlockSpec((rows, D_ATTN), lambda i: (i, OFF_Q // D_ATTN)),
            pl.BlockSpec((rows, D_KV), lambda i: (i, OFF_K // D_KV)),
            pl.BlockSpec((rows, D_KV), lambda i: (i, OFF_V // D_KV)),
            cache_t_spec,
            cache_t_spec,
        ],
        out_specs=[
            pl.BlockSpec((rows, D_ATTN), lambda i: (i, 0)),
            cache_spec,
            cache_spec,
        ],
        out_shape=[
            jax.ShapeDtypeStruct((batch * steps, D_ATTN), BF16),
            jax.ShapeDtypeStruct((batch, wc, D_KV), F32),
            jax.ShapeDtypeStruct((batch, wc, D_KV), F32),
        ],
        scratch_shapes=[
            pltpu.VMEM((N_HEADS * steps, wc), F32),
            pltpu.VMEM((bt, N_HEADS * steps, rows), F32),
            pltpu.VMEM((N_HEADS * steps, LANES), F32),
            pltpu.VMEM((bt, N_HEADS * steps, D_KV), F32),
            pltpu.VMEM((bt, N_HEADS * steps, wc + rows), F32),
            pltpu.VMEM((bt, N_HEADS * steps, wc + rows), BF16),
            pltpu.VMEM((bt, N_HEADS * steps, D_KV + LANES), F32),
        ],
        compiler_params=_params("arbitrary"),
        name="attn_sample",
    )(rel_bias, sinks, z, z, z, cache_kt, cache_vt)


CARRY_ROWS = 8


def _gated_conv(b, wc_ref, u, u1, u2):
    conv = wc_ref[0:1, :] * u2 + wc_ref[1:2, :] * u1 + wc_ref[2:3, :] * u
    return (b.astype(F32) * conv).astype(BF16)


def _mix_tail(conv_branch_input, at_ref, gc_ref, ga_ref, x_ref, g_ref, wco_ref, wao_ref, wo_ref, h_ref, hn_ref):
    y_conv = jnp.dot(conv_branch_input(), wco_ref[...], preferred_element_type=F32)
    y_attn = jnp.dot(at_ref[...], wao_ref[...], preferred_element_type=F32)
    merged = (jax.nn.sigmoid(gc_ref[...].astype(F32)) * y_conv
              + jax.nn.sigmoid(ga_ref[...].astype(F32)) * y_attn)
    h = x_ref[...] + jnp.dot(merged.astype(BF16), wo_ref[...], preferred_element_type=F32)
    h_ref[...] = h
    hn_ref[...] = _rms_rows(h, g_ref[0:1, :]).astype(hn_ref.dtype)


def _delayed_from_scratch(us_ref, rows):
    def at(delay):
        return us_ref[CARRY_ROWS - delay + rows.start:CARRY_ROWS - delay + rows.stop, :]
    return at(0), at(1), at(2)


def _mix_prompt_kernel(b_ref, c_ref, hc_ref, wc_ref, at_ref, gc_ref, ga_ref, x_ref, g_ref,
                       wco_ref, wao_ref, wo_ref, side0_ref, side1_ref, side2_ref,
                       h_ref, hn_ref, st_ref, side0_bf16_ref, side1_bf16_ref, side2_bf16_ref,
                       us_ref, *, tiles_per_seq):
    tm = x_ref.shape[0]
    side0_bf16_ref[...] = side0_ref[...].astype(BF16)
    side1_bf16_ref[...] = side1_ref[...].astype(BF16)
    side2_bf16_ref[...] = side2_ref[...].astype(BF16)

    @pl.when(pl.program_id(0) % tiles_per_seq == 0)
    def _():
        us_ref[0:CARRY_ROWS, :] = jnp.zeros((CARRY_ROWS, D_CONV), F32)

    def conv_branch_input():
        u = c_ref[...].astype(F32) * hc_ref[...].astype(F32)
        us_ref[CARRY_ROWS:, :] = u
        st_ref[...] = u[tm - (CONV_WIDTH - 1):, :]
        yc = _gated_conv(b_ref[...], wc_ref, *_delayed_from_scratch(us_ref, slice(0, tm)))
        us_ref[0:CARRY_ROWS, :] = us_ref[tm:tm + CARRY_ROWS, :]
        return yc

    _mix_tail(conv_branch_input, at_ref, gc_ref, ga_ref, x_ref, g_ref, wco_ref, wao_ref, wo_ref, h_ref, hn_ref)


def _mix_sample_kernel(b_ref, c_ref, hc_ref, st_ref, wc_ref, at_ref, gc_ref, ga_ref, x_ref, g_ref,
                       wco_ref, wao_ref, wo_ref, h_ref, hn_ref, new_st_ref, us_ref, *, steps):
    tm = x_ref.shape[0]
    nseq = tm // steps
    keep = CONV_WIDTH - 1

    def conv_branch_input():
        u = c_ref[...].astype(F32) * hc_ref[...].astype(F32)
        us_ref[:, CARRY_ROWS - keep:CARRY_ROWS, :] = st_ref[...]
        us_ref[:, CARRY_ROWS:, :] = u.reshape(nseq, steps, D_CONV)
        new_st_ref[...] = us_ref[:, CARRY_ROWS + steps - keep:, :]

        def delayed(d):
            return us_ref[:, CARRY_ROWS - d:CARRY_ROWS - d + steps, :].reshape(tm, D_CONV)

        return _gated_conv(b_ref[...], wc_ref, u, delayed(1), delayed(2))

    _mix_tail(conv_branch_input, at_ref, gc_ref, ga_ref, x_ref, g_ref, wco_ref, wao_ref, wo_ref, h_ref, hn_ref)


def _mix_specs(tm):
    resident = functools.partial(pl.BlockSpec, index_map=lambda i: (0, 0), pipeline_mode=pl.Buffered(1))
    row_spec = pl.BlockSpec((tm, D_MODEL), lambda i: (i, 0))

    def zcol(off, width):
        return pl.BlockSpec((tm, width), lambda i: (i, off // width))

    conv_in = [zcol(OFF_B, D_CONV), zcol(OFF_C, D_CONV), zcol(OFF_H, D_CONV)]
    rest_in = [
        pl.BlockSpec((CONV_WIDTH, D_CONV), lambda i: (0, 0)),
        pl.BlockSpec((tm, D_ATTN), lambda i: (i, 0)),
        zcol(OFF_GC, D_MODEL),
        zcol(OFF_GA, D_MODEL),
        row_spec,
        pl.BlockSpec((1, D_MODEL),lambda i: (0, 0)),
        resident((D_CONV, D_MODEL)),
        resident((D_ATTN, D_MODEL)),
        resident((D_MODEL, D_MODEL)),
    ]
    return conv_in, rest_in, row_spec


def _mix_prompt(z, attn, x, w_conv, g_ffn, wco, wao, wo, sides, tm, batch, seq):
    m = x.shape[0]
    assert seq % tm == 0 and tm >= CARRY_ROWS
    tiles_per_seq = seq // tm
    nsteps = m // tm
    conv_in, rest_in, row_spec = _mix_specs(tm)

    def side_spec(side):
        rows = _side_cast_rows(side.shape[0], nsteps)
        return pl.BlockSpec((rows, side.shape[1]), lambda i: (jnp.minimum(i, side.shape[0] // rows - 1), 0))

    side_specs = [side_spec(s) for s in sides]
    return pl.pallas_call(
        functools.partial(_mix_prompt_kernel, tiles_per_seq=tiles_per_seq),
        grid=(nsteps,),
        in_specs=conv_in + rest_in + side_specs,
        out_specs=[row_spec, row_spec,
                   pl.BlockSpec((None, CONV_WIDTH - 1, D_CONV), lambda i: (i // tiles_per_seq, 0, 0))] + side_specs,
        out_shape=[jax.ShapeDtypeStruct((m, D_MODEL), F32), jax.ShapeDtypeStruct((m, D_MODEL), BF16),
                   jax.ShapeDtypeStruct((batch, CONV_WIDTH - 1, D_CONV), F32)]
        + [jax.ShapeDtypeStruct(s.shape, BF16) for s in sides],
        scratch_shapes=[pltpu.VMEM((tm + CARRY_ROWS, D_CONV), F32)],
        compiler_params=_params("arbitrary"),
        name="mix_prompt",
    )(z, z, z, w_conv, attn, z, z, x, g_ffn, wco, wao, wo, *sides)


def _mix_sample(z, state, attn, x, w_conv, g_ffn, wco, wao, wo, tm, steps):
    m = x.shape[0]
    assert tm % steps == 0 and CONV_WIDTH - 1 <= min(steps, CARRY_ROWS)
    conv_in, rest_in, row_spec = _mix_specs(tm)
    state_spec = pl.BlockSpec((tm // steps, CONV_WIDTH - 1, D_CONV), lambda i: (i, 0, 0))
    return pl.pallas_call(
        functools.partial(_mix_sample_kernel, steps=steps),
        grid=(m // tm,),
        in_specs=conv_in + [state_spec] + rest_in,
        out_specs=[row_spec, row_spec, state_spec],
        out_shape=[jax.ShapeDtypeStruct((m, D_MODEL), F32), jax.ShapeDtypeStruct((m, D_MODEL), BF16),
                   jax.ShapeDtypeStruct(state.shape, F32)],
        scratch_shapes=[pltpu.VMEM((tm // steps, CARRY_ROWS + steps, D_CONV), F32)],
        compiler_params=_params("arbitrary"),
        name="mix_sample",
    )(z, z, z, state, w_conv, attn, z, z, x, g_ffn, wco, wao, wo)


def _ffn_kernel(hn_ref, h_ref, gf_ref, wg_ref, wu_ref, wd_ref, y_ref, *, final_norm):
    j = pl.program_id(1)

    @pl.when(j == 0)
    def _():
        y_ref[...] = h_ref[...]

    hn = hn_ref[...]
    a = (jax.nn.silu(jnp.dot(hn, wg_ref[...], preferred_element_type=F32))
         * jnp.dot(hn, wu_ref[...], preferred_element_type=F32))
    y_ref[...] += jnp.dot(a.astype(BF16), wd_ref[...], preferred_element_type=F32)

    if final_norm:
        @pl.when(j == pl.num_programs(1) - 1)
        def _():
            def body(r, carry):
                rows = pl.ds(pl.multiple_of(r * NORM_ROWS, NORM_ROWS), NORM_ROWS)
                y_ref[rows, :] = _rms_rows(y_ref[rows, :], gf_ref[0:1, :])
                return carry

            lax.fori_loop(0, y_ref.shape[0] // NORM_ROWS, body, 0)


def _ffn(hn, h, g_final, wg, wu, wd, tm, tf, final_norm):
    m = h.shape[0]
    return pl.pallas_call(
        functools.partial(_ffn_kernel, final_norm=final_norm),
        grid=(m // tm, D_FF // tf),
        in_specs=[
            pl.BlockSpec((tm, D_MODEL), lambda i, j: (i, 0)),
            pl.BlockSpec((tm, D_MODEL), lambda i, j: (i, 0)),
            pl.BlockSpec((1, D_MODEL),lambda i, j: (0, 0)),
            pl.BlockSpec((D_MODEL, tf), lambda i, j: (0, j)),
            pl.BlockSpec((D_MODEL, tf), lambda i, j: (0, j)),
            pl.BlockSpec((tf, D_MODEL), lambda i, j: (j, 0)),
        ],
        out_specs=pl.BlockSpec((tm, D_MODEL), lambda i, j: (i, 0)),
        out_shape=jax.ShapeDtypeStruct((m, D_MODEL), F32),
        compiler_params=_params("arbitrary", "arbitrary"),
        name="ffn",
    )(hn, h, g_final, wg, wu, wd)


IN_PROJ_TM_PROMPT = 2048
IN_PROJ_TN = 512
MIX_TM = 256
FFN_TM = 1024
FFN_TF = 512


def kernel(x_prompt, x_sample, cache_k, cache_v, state_conv, rel_bias, w_in, w_conv, w_conv_out, sinks,
           w_attn_out, w_o, g_mix, g_ffn, w_gate, w_up, w_down, g_final):
    depth = w_in.shape[0]
    batch, seq, _ = x_prompt.shape
    dec_batch, steps, _ = x_sample.shape
    wc = cache_k.shape[2]
    assert seq % WINDOW == 0 and wc == WINDOW and dec_batch % SAMPLE_BATCH_TILE == 0

    hp = x_prompt.reshape(batch * seq, D_MODEL)
    hs = x_sample.reshape(dec_batch * steps, D_MODEL)
    g_final2 = g_final.reshape(1, D_MODEL)
    kp_l, vp_l, cp_l, ks_l, vs_l, cs_l = [], [], [], [], [], []
    for l in range(depth):
        final = l == depth - 1
        lw = {
            "w_in": w_in[l],
            "g_ffn": g_ffn[l].reshape(1, D_MODEL),
        }
        g_mix_l = g_mix[l].reshape(1, D_MODEL)

        zp, lw["wo"] = _in_proj(hp, g_mix_l, lw["w_in"], w_o[l], IN_PROJ_TM_PROMPT, IN_PROJ_TN)
        zs, attn_p, lw["wco"], lw["wao"] = _in_proj_sample_attn_prompt(
            hs, g_mix_l, lw["w_in"], w_conv_out[l], w_attn_out[l], zp, rel_bias, sinks[l], IN_PROJ_TN, batch, seq)
        hp, hnp, conv_p, lw["wg"], lw["wu"], lw["wd"] = _mix_prompt(
            zp, attn_p, hp, w_conv[l], lw["g_ffn"], lw["wco"], lw["wao"], lw["wo"],
            (w_gate[l], w_up[l], w_down[l]), MIX_TM, batch, seq)
        hp = _ffn(hnp, hp, g_final2, lw["wg"], lw["wu"], lw["wd"], FFN_TM, FFN_TF, final)
        kv_p = zp.reshape(batch, seq, D_IN_PROJ)[:, seq - WINDOW:, OFF_K:OFF_V + D_KV].astype(F32)
        kp_l.append(kv_p[..., :D_KV].reshape(batch, WINDOW, N_KV_HEADS, HEAD_DIM))
        vp_l.append(kv_p[..., D_KV:].reshape(batch, WINDOW, N_KV_HEADS, HEAD_DIM))
        cp_l.append(conv_p)

        cache_kt = jnp.transpose(cache_k[l], (0, 2, 3, 1)).reshape(dec_batch, D_KV, wc)
        cache_vt = jnp.transpose(cache_v[l], (0, 2, 3, 1)).reshape(dec_batch, D_KV, wc)
        attn_s, nk, nv = _attn_sample(zs, cache_kt, cache_vt, rel_bias, sinks[l], dec_batch, steps)
        hs, hns, conv_s = _mix_sample(zs, state_conv[l], attn_s, hs, w_conv[l], lw["g_ffn"], lw["wco"], lw["wao"],
                                      lw["wo"], MIX_TM, steps)
        hs = _ffn(hns, hs, g_final2, lw["wg"], lw["wu"], lw["wd"], FFN_TM, FFN_TF, final)
        ks_l.append(nk.reshape(dec_batch, wc, N_KV_HEADS, HEAD_DIM))
        vs_l.append(nv.reshape(dec_batch, wc, N_KV_HEADS, HEAD_DIM))
        cs_l.append(conv_s)

    return (hp.reshape(batch, seq, D_MODEL), hs.reshape(dec_batch, steps, D_MODEL),
            jnp.stack(kp_l), jnp.stack(vp_l), jnp.stack(cp_l),
            jnp.stack(ks_l), jnp.stack(vs_l), jnp.stack(cs_l))
```

```python
import functools
import math

import numpy as np
import jax
import jax.numpy as jnp
from jax import lax
from jax.experimental import pallas as pl
from jax.experimental.pallas import tpu as pltpu

D_MODEL = 2048
D_CONV = D_MODEL // 2
CONV_WIDTH = 3
HEAD_DIM = 64
N_HEADS = (D_MODEL // 2) // HEAD_DIM
N_KV_HEADS = N_HEADS // 4
GROUP = N_HEADS // N_KV_HEADS
D_ATTN = N_HEADS * HEAD_DIM
D_KV = N_KV_HEADS * HEAD_DIM
WINDOW = 128
NUM_BUCKETS = 32
MAX_DISTANCE = 128
D_FF = -(-8 * D_MODEL // (3 * 256)) * 256
EPS = 1e-6
D_IN_PROJ = 3 * D_CONV + D_ATTN + 2 * D_KV + 2 * D_MODEL

OFF_GC = 0
OFF_GA = D_MODEL
OFF_B = 2 * D_MODEL
OFF_C = OFF_B + D_CONV
OFF_H = OFF_C + D_CONV
OFF_Q = OFF_H + D_CONV
OFF_K = OFF_Q + D_ATTN
OFF_V = OFF_K + D_KV

LANES = 128
NEG = -1e30
F32 = jnp.float32
BF16 = jnp.bfloat16
VMEM_LIMIT = 60 * 1024 * 1024


def _bucket_thresholds():
    max_exact = NUM_BUCKETS // 2
    d = np.arange(MAX_DISTANCE)
    ratio = np.log(np.maximum(d, 1).astype(np.float32) / np.float32(max_exact)) / np.float32(
        math.log(MAX_DISTANCE / max_exact))
    large = max_exact + (ratio * np.float32(NUM_BUCKETS - max_exact)).astype(np.int32)
    large = np.minimum(large, NUM_BUCKETS - 1)
    return [int(np.min(d[(d >= max_exact) & (large >= b)])) for b in range(max_exact + 1, NUM_BUCKETS)]


BUCKET_THRESHOLDS = _bucket_thresholds()


def _params(*semantics):
    return pltpu.CompilerParams(dimension_semantics=semantics, vmem_limit_bytes=VMEM_LIMIT)


def _rms_rows(x, g):
    ms = jnp.mean(x * x, axis=-1, keepdims=True)
    return x * lax.rsqrt(ms + EPS) * g


NORM_ROWS = 256


def _inproj_kernel(x_ref, g_ref, w_ref, side_ref, z_ref, side_bf16_ref, xn_ref):
    _inproj_norm(pl.program_id(1) == 0, x_ref, g_ref, xn_ref)
    _inproj_columns(w_ref, z_ref, xn_ref)
    side_bf16_ref[...] = side_ref[...].astype(BF16)


BF16_ROWS = 16


def _side_cast_rows(rows, nsteps):
    per_step = -(-rows // nsteps)
    per_step = -(-per_step // BF16_ROWS) * BF16_ROWS
    assert rows % per_step == 0
    return per_step


def _inproj_norm(first_column_step, x_ref, g_ref, xn_ref):
    @pl.when(first_column_step)
    def _():
        def body(r, carry):
            rows = pl.ds(pl.multiple_of(r * NORM_ROWS, NORM_ROWS), NORM_ROWS)
            xn_ref[rows, :] = _rms_rows(x_ref[rows, :], g_ref[0:1, :]).astype(xn_ref.dtype)
            return carry

        lax.fori_loop(0, x_ref.shape[0] // NORM_ROWS, body, 0)


def _inproj_columns(w_ref, z_ref, xn_ref):
    z_ref[...] = jnp.dot(xn_ref[...], w_ref[...].astype(BF16), preferred_element_type=F32).astype(z_ref.dtype)


def _in_proj(x, g, w, side, tm, tn):
    m = x.shape[0]
    n = w.shape[1]
    gate_start = n - 2 * D_MODEL
    assert gate_start % tn == 0 and n % tn == 0
    gate_block, nblocks = gate_start // tn, n // tn
    side_rows = _side_cast_rows(side.shape[0], (m // tm) * nblocks)
    side_spec = pl.BlockSpec((side_rows, side.shape[1]),
                             lambda i, j: (jnp.minimum(i * nblocks + j, side.shape[0] // side_rows - 1), 0))
    return pl.pallas_call(
        _inproj_kernel,
        grid=(m // tm, nblocks),
        in_specs=[
            pl.BlockSpec((tm, D_MODEL), lambda i, j: (i, 0)),
            pl.BlockSpec((1, D_MODEL),lambda i, j: (0, 0)),
            pl.BlockSpec((D_MODEL, tn), lambda i, j: (0, (j + gate_block) % nblocks)),
            side_spec,
        ],
        out_specs=[pl.BlockSpec((tm, tn), lambda i, j: (i, j)), side_spec],
        out_shape=[jax.ShapeDtypeStruct((m, n), BF16), jax.ShapeDtypeStruct(side.shape, BF16)],
        scratch_shapes=[pltpu.VMEM((tm, D_MODEL), BF16)],
        compiler_params=_params("arbitrary", "arbitrary"),
        name="in_proj",
    )(x, g, w, side)


def _bias_from_distance(dist, head, relb_ref):
    valid = (dist >= 0) & (dist < WINDOW)
    d = jnp.clip(dist, 0, MAX_DISTANCE - 1)
    bucket = jnp.minimum(d, NUM_BUCKETS // 2)
    for thr in BUCKET_THRESHOLDS:
        bucket = bucket + (d >= thr).astype(jnp.int32)
    bias = jnp.zeros(dist.shape, F32)
    for b in range(NUM_BUCKETS):
        bias = jnp.where(bucket == b, relb_ref[b, head], bias)
    return jnp.where(valid, bias, NEG)


def _biased_scores(s, bias):
    return jnp.where(bias > 0.5 * NEG, s + bias, NEG)


def _head_in_group_columns(q, h):
    half = D_KV // 2
    assert 2 * HEAD_DIM == half and GROUP * HEAD_DIM == D_KV
    g = h // GROUP
    src = q[:, (h // 2) * half:(h // 2 + 1) * half]
    if h % 2 != g % 2:
        src = pltpu.roll(src, HEAD_DIM, 1)
    lane_half = lax.broadcasted_iota(jnp.int32, src.shape, 1) // HEAD_DIM
    placed = jnp.where(lane_half == g % 2, src, 0.0)
    zero = jnp.zeros_like(placed)
    return jnp.concatenate([placed, zero] if g // 2 == 0 else [zero, placed], axis=1)


def _group_columns(h):
    g = h // GROUP
    return slice(g * HEAD_DIM, (g + 1) * HEAD_DIM)


INPROJ_FILL_CHUNKS = 4
PROMPT_BLOCKS_PER_STEP = 4


def _inproj_attn_kernel(x_ref, g_ref, w_ref, side0_ref, side1_ref, relb_ref, sink_ref, q_ref, kp_ref, kc_ref,
                        vp_ref, vc_ref, z_ref, o_ref, side0_bf16_ref, side1_bf16_ref,
                        xn_ref, wb_ref, bias_ref, lhs_ref, *, steps_per_seq):
    j = pl.program_id(0)
    _attn_prompt_bias(j == 0, relb_ref, bias_ref)
    _inproj_norm(j == 0, x_ref, g_ref, xn_ref)
    wb_ref[...] = w_ref[...].astype(BF16)
    side0_bf16_ref[...] = side0_ref[...].astype(BF16)
    side1_bf16_ref[...] = side1_ref[...].astype(BF16)
    chunk = xn_ref.shape[0] // INPROJ_FILL_CHUNKS

    def column_chunk(c):
        rows = slice(c * chunk, (c + 1) * chunk)

        def emit():
            z_ref[rows, :] = jnp.dot(xn_ref[rows, :], wb_ref[...], preferred_element_type=F32).astype(z_ref.dtype)
        return emit

    _attn_prompt_blocks(j % steps_per_seq != 0, sink_ref, q_ref, kp_ref, kc_ref, vp_ref, vc_ref,
                        o_ref, bias_ref, lhs_ref, fillers=[column_chunk(c) for c in range(INPROJ_FILL_CHUNKS)])


def _from_prev_mask():
    qi = lax.broadcasted_iota(jnp.int32, (WINDOW, WINDOW), 0)
    kj = lax.broadcasted_iota(jnp.int32, (WINDOW, WINDOW), 1)
    return qi, kj, kj > qi


def _attn_prompt_bias(first_step, relb_ref, bias_ref):
    @pl.when(first_step)
    def _():
        qi, kj, from_prev = _from_prev_mask()
        dist = jnp.where(from_prev, qi + WINDOW - kj, qi - kj)
        for h in range(N_HEADS):
            rows = slice(h * WINDOW, (h + 1) * WINDOW)
            bias = _bias_from_distance(dist, h, relb_ref)
            bias_ref[1, rows, :] = bias
            bias_ref[0, rows, :] = jnp.where(from_prev, NEG, bias)


def _attn_prompt_blocks(has_prev_block, sink_ref, q_ref, kp_ref, kc_ref, vp_ref, vc_ref, o_ref, bias_ref, lhs_ref,
                        fillers=()):
    _, _, from_prev = _from_prev_mask()
    k_all = jnp.concatenate([kp_ref[...], kc_ref[...]], axis=0)
    v_all = jnp.concatenate([vp_ref[...], vc_ref[...]], axis=0)

    def keys_of(blk):
        return slice(blk * WINDOW, (blk + 2) * WINDOW)

    nblk = PROMPT_BLOCKS_PER_STEP
    fillers = list(fillers)
    per_stage = -(-len(fillers) // (nblk + 2))
    scores, probs = {}, {}
    for stage in range(nblk + 2):
        if stage < nblk:
            q = q_ref[stage * WINDOW:(stage + 1) * WINDOW, :]
            scores[stage] = _block_scores(q, k_all[keys_of(stage)], lhs_ref.at[stage])
        for _ in range(min(per_stage, len(fillers))):
            fillers.pop(0)()
        if 0 <= stage - 1 < nblk:
            blk = stage - 1
            has_prev = has_prev_block.astype(jnp.int32) if blk == 0 else 1
            probs[blk] = _block_softmax(scores.pop(blk), has_prev, from_prev, sink_ref, bias_ref)
        if 0 <= stage - 2 < nblk:
            blk = stage - 2
            o_ref[blk * WINDOW:(blk + 1) * WINDOW, :] = _block_output(
                *probs.pop(blk), v_all[keys_of(blk)]).astype(o_ref.dtype)


def _block_scores(q, kk, lhs_ref):
    q = q.astype(F32) * (HEAD_DIM ** -0.5)
    for h in range(N_HEADS):
        lhs_ref[h * WINDOW:(h + 1) * WINDOW, :] = _head_in_group_columns(q, h).astype(BF16)
    return lax.dot_general(lhs_ref[...], kk, (((1,), (1,)), ((), ())), preferred_element_type=F32)


def _block_softmax(s, has_prev, from_prev, sink_ref, bias_ref):
    probs, sink_terms = [], []
    for h in range(N_HEADS):
        rows = slice(h * WINDOW, (h + 1) * WINDOW)
        sh = _biased_scores(jnp.where(from_prev, s[rows, :WINDOW], s[rows, WINDOW:]), bias_ref[has_prev, rows, :])
        m = jnp.maximum(jnp.max(sh, axis=-1, keepdims=True), sink_ref[h])
        p = jnp.exp(sh - m)
        probs.append(jnp.concatenate([jnp.where(from_prev, p, 0.0), jnp.where(from_prev, 0.0, p)],
                                     axis=1).astype(BF16))
        sink_terms.append(jnp.exp(sink_ref[h] - m))
    return probs, sink_terms


def _block_output(probs, sink_terms, vv):
    assert 2 * HEAD_DIM == LANES and GROUP % 2 == 0
    ones = jnp.ones((vv.shape[0], HEAD_DIM), BF16)
    low_half = lax.broadcasted_iota(jnp.int32, (WINDOW, LANES), 1) < HEAD_DIM
    outs = []
    for g in range(N_KV_HEADS):
        vg = vv[:, g * HEAD_DIM:(g + 1) * HEAD_DIM]
        even_heads = [g * GROUP + r for r in range(0, GROUP, 2)]
        pe = jnp.concatenate([probs[h] for h in even_heads], axis=0)
        po = jnp.concatenate([probs[h + 1] for h in even_heads], axis=0)
        oe = jnp.dot(pe, jnp.concatenate([vg, ones], axis=1), preferred_element_type=F32)
        oo = jnp.dot(po, jnp.concatenate([ones, vg], axis=1), preferred_element_type=F32)
        for k, h in enumerate(even_heads):
            a = oe[k * WINDOW:(k + 1) * WINDOW]
            b = oo[k * WINDOW:(k + 1) * WINDOW]
            out_pair = jnp.where(low_half, a, b)
            sum_pair = pltpu.roll(jnp.where(low_half, b, a), HEAD_DIM, 1)
            sink_pair = jnp.where(low_half, sink_terms[h], sink_terms[h + 1])
            outs.append(out_pair * (1.0 / (sum_pair + sink_pair)))
    return jnp.concatenate(outs, axis=1)


def _in_proj_sample_attn_prompt(x, g, w, side0, side1, zp, rel_bias, sinks, tn, batch, seq):
    tm = x.shape[0]
    n = w.shape[1]
    gate_start = n - 2 * D_MODEL
    assert gate_start % tn == 0 and n % tn == 0
    gate_block, nsteps = gate_start // tn, n // tn
    bps = PROMPT_BLOCKS_PER_STEP
    assert seq % (bps * WINDOW) == 0
    steps_per_seq = seq // (bps * WINDOW)
    attn_steps = batch * steps_per_seq
    assert attn_steps <= nsteps

    def astep(j):
        return jnp.minimum(j, attn_steps - 1)

    def cur(off, width):
        return lambda j: (astep(j), off // width)

    def prev(off, width):
        return lambda j: (astep(j) * bps - jnp.minimum(astep(j) % steps_per_seq, 1), off // width)

    def side_spec(side):
        rows = _side_cast_rows(side.shape[0], nsteps)
        return pl.BlockSpec((rows, side.shape[1]), lambda j: (jnp.minimum(j, side.shape[0] // rows - 1), 0))

    smem = pl.BlockSpec(memory_space=pltpu.SMEM)
    return pl.pallas_call(
        functools.partial(_inproj_attn_kernel, steps_per_seq=steps_per_seq),
        grid=(nsteps,),
        in_specs=[
            pl.BlockSpec((tm, D_MODEL), lambda j: (0, 0)),
            pl.BlockSpec((1, D_MODEL),lambda j: (0, 0)),
            pl.BlockSpec((D_MODEL, tn), lambda j: (0, (j + gate_block) % nsteps)),
            side_spec(side0),
            side_spec(side1),
            smem,
            smem,
            pl.BlockSpec((bps * WINDOW, D_ATTN), cur(OFF_Q, D_ATTN)),
            pl.BlockSpec((WINDOW, D_KV), prev(OFF_K, D_KV)),
            pl.BlockSpec((bps * WINDOW, D_KV), cur(OFF_K, D_KV)),
            pl.BlockSpec((WINDOW, D_KV), prev(OFF_V, D_KV)),
            pl.BlockSpec((bps * WINDOW, D_KV), cur(OFF_V, D_KV)),
        ],
        out_specs=[
            pl.BlockSpec((tm, tn), lambda j: (0, j)),
            pl.BlockSpec((bps * WINDOW, D_ATTN), lambda j: (j, 0)),
            side_spec(side0),
            side_spec(side1),
        ],
        out_shape=[
            jax.ShapeDtypeStruct((tm, n), BF16),
            jax.ShapeDtypeStruct((nsteps * bps * WINDOW, D_ATTN), BF16),
            jax.ShapeDtypeStruct(side0.shape, BF16),
            jax.ShapeDtypeStruct(side1.shape, BF16),
        ],
        scratch_shapes=[pltpu.VMEM((tm, D_MODEL), BF16),
                        pltpu.VMEM((D_MODEL, tn), BF16),
                        pltpu.VMEM((2, N_HEADS * WINDOW, WINDOW), F32),
                        pltpu.VMEM((bps, N_HEADS * WINDOW, D_KV), BF16)],
        compiler_params=_params("arbitrary"),
        name="in_proj_sample_attn_prompt",
    )(x, g, w, side0, side1, rel_bias, sinks, zp, zp, zp, zp, zp)


SAMPLE_BATCH_TILE = 16


def _attn_sample_kernel(relb_ref, sink_ref, q_ref, kn_ref, vn_ref, ckt_ref, cvt_ref,
                        o_ref, nk_ref, nv_ref,
                        bias1_ref, bias2_ref, sinkv_ref, z_ref, s_ref, p_ref, oh_ref, *, steps):
    bt, wc, _ = nk_ref.shape
    tile_rows = bt * steps

    @pl.when(pl.program_id(0) == 0)
    def _():
        t = lax.broadcasted_iota(jnp.int32, (steps, wc), 0)
        j = lax.broadcasted_iota(jnp.int32, (steps, wc), 1)
        t2 = lax.broadcasted_iota(jnp.int32, (steps, tile_rows), 0)
        j2 = lax.broadcasted_iota(jnp.int32, (steps, tile_rows), 1)
        own = jnp.where(j2 < steps, t2 - j2, -1)
        for h in range(N_HEADS):
            sl = slice(h * steps, (h + 1) * steps)
            bias1_ref[sl, :] = _bias_from_distance(t + wc - j, h, relb_ref)
            bias2_ref[0, sl, :] = _bias_from_distance(own, h, relb_ref)
            sinkv_ref[sl, :] = jnp.full((steps, LANES), sink_ref[h], F32)
        for b in range(1, bt):
            bias2_ref[b] = pltpu.roll(bias2_ref[0], b * steps, 1)

    kn = kn_ref[...]
    vn = vn_ref[...]
    nk_ref[:, wc - steps:wc, :] = kn.astype(F32).reshape(bt, steps, D_KV)
    nv_ref[:, wc - steps:wc, :] = vn.astype(F32).reshape(bt, steps, D_KV)

    q = q_ref[...].astype(F32) * (HEAD_DIM ** -0.5)
    for h in range(N_HEADS):
        z_ref[:, h * steps:(h + 1) * steps, :] = _head_in_group_columns(q, h).reshape(bt, steps, D_KV)

    for b in range(bt):
        ck = ckt_ref[b].T
        nk_ref[b, 0:wc - steps, :] = ck[steps:, :]
        keys = jnp.concatenate([ck.astype(BF16), kn], axis=0)
        s_ref[b] = lax.dot_general(z_ref[b].astype(BF16), keys, (((1,), (1,)), ((), ())),
                                   preferred_element_type=F32)

    bias = jnp.concatenate([jnp.broadcast_to(bias1_ref[...], (bt,) + bias1_ref.shape), bias2_ref[...]], axis=2)
    s = _biased_scores(s_ref[...], bias)
    sink = sinkv_ref[...][None]
    m = jnp.maximum(jnp.max(s, axis=-1, keepdims=True), sink)
    p_ref[...] = jnp.exp(s - jnp.concatenate([m] * (s.shape[-1] // LANES), axis=-1)).astype(BF16)
    sink_term = jnp.exp(sink - m)

    vn_ones = jnp.concatenate([vn, jnp.ones((tile_rows, LANES), BF16)], axis=1)
    for b in range(bt):
        cv = cvt_ref[b].T
        nv_ref[b, 0:wc - steps, :] = cv[steps:, :]
        cv_ones = jnp.concatenate([cv.astype(BF16), jnp.ones((wc, LANES), BF16)], axis=1)
        oh_ref[b] = jnp.dot(p_ref[b], jnp.concatenate([cv_ones, vn_ones], axis=0), preferred_element_type=F32)

    inv = 1.0 / (oh_ref[:, :, D_KV:] + sink_term)
    outs = []
    for h in range(N_HEADS):
        rows = slice(h * steps, (h + 1) * steps)
        outs.append((oh_ref[:, rows, _group_columns(h)] * inv[:, rows, :HEAD_DIM]).reshape(tile_rows, HEAD_DIM))
    o_ref[...] = jnp.concatenate(outs, axis=1).astype(o_ref.dtype)


def _attn_sample(z, cache_kt, cache_vt, rel_bias, sinks, batch, steps):
    wc = cache_kt.shape[2]
    bt = SAMPLE_BATCH_TILE
    rows = bt * steps
    smem = pl.BlockSpec(memory_space=pltpu.SMEM)
    cache_t_spec = pl.BlockSpec((bt, D_KV, wc), lambda i: (i, 0, 0))
    cache_spec = pl.BlockSpec((bt, wc, D_KV), lambda i: (i, 0, 0))
    return pl.pallas_call(
        functools.partial(_attn_sample_kernel, steps=steps),
        grid=(batch // bt,),
        in_specs=[
            smem,
            smem,
            pl.BlockSpec((rows, D_ATTN), lambda i: (i, OFF_Q // D_ATTN)),
            pl.BlockSpec((rows, D_KV), lambda i: (i, OFF_K // D_KV)),
            pl.BlockSpec((rows, D_KV), lambda i: (i, OFF_V // D_KV)),
            cache_t_spec,
            cache_t_spec,
        ],
        out_specs=[
            pl.BlockSpec((rows, D_ATTN), lambda i: (i, 0)),
            cache_spec,
            cache_spec,
        ],
        out_shape=[
            jax.ShapeDtypeStruct((batch * steps, D_ATTN), BF16),
            jax.ShapeDtypeStruct((batch, wc, D_KV), F32),
            jax.ShapeDtypeStruct((batch, wc, D_KV), F32),
        ],
        scratch_shapes=[
            pltpu.VMEM((N_HEADS * steps, wc), F32),
            pltpu.VMEM((bt, N_HEADS * steps, rows), F32),
            pltpu.VMEM((N_HEADS * steps, LANES), F32),
            pltpu.VMEM((bt, N_HEADS * steps, D_KV), F32),
            pltpu.VMEM((bt, N_HEADS * steps, wc + rows), F32),
            pltpu.VMEM((bt, N_HEADS * steps, wc + rows), BF16),
            pltpu.VMEM((bt, N_HEADS * steps, D_KV + LANES), F32),
        ],
        compiler_params=_params("arbitrary"),
        name="attn_sample",
    )(rel_bias, sinks, z, z, z, cache_kt, cache_vt)


CARRY_ROWS = 8


def _gated_conv(b, wc_ref, u, u1, u2):
    conv = wc_ref[0:1, :] * u2 + wc_ref[1:2, :] * u1 + wc_ref[2:3, :] * u
    return (b.astype(F32) * conv).astype(BF16)


def _mix_tail(conv_branch_input, at_ref, gc_ref, ga_ref, x_ref, g_ref, wco_ref, wao_ref, wo_ref, h_ref, hn_ref):
    y_conv = jnp.dot(conv_branch_input(), wco_ref[...], preferred_element_type=F32)
    y_attn = jnp.dot(at_ref[...], wao_ref[...], preferred_element_type=F32)
    merged = (jax.nn.sigmoid(gc_ref[...].astype(F32)) * y_conv
              + jax.nn.sigmoid(ga_ref[...].astype(F32)) * y_attn)
    h = x_ref[...] + jnp.dot(merged.astype(BF16), wo_ref[...], preferred_element_type=F32)
    h_ref[...] = h
    hn_ref[...] = _rms_rows(h, g_ref[0:1, :]).astype(hn_ref.dtype)


def _delayed_from_scratch(us_ref, rows):
    def at(delay):
        return us_ref[CARRY_ROWS - delay + rows.start:CARRY_ROWS - delay + rows.stop, :]
    return at(0), at(1), at(2)


def _mix_prompt_kernel(b_ref, c_ref, hc_ref, wc_ref, at_ref, gc_ref, ga_ref, x_ref, g_ref,
                       wco_ref, wao_ref, wo_ref, side0_ref, side1_ref, side2_ref,
                       h_ref, hn_ref, st_ref, side0_bf16_ref, side1_bf16_ref, side2_bf16_ref,
                       us_ref, *, tiles_per_seq):
    tm = x_ref.shape[0]
    side0_bf16_ref[...] = side0_ref[...].astype(BF16)
    side1_bf16_ref[...] = side1_ref[...].astype(BF16)
    side2_bf16_ref[...] = side2_ref[...].astype(BF16)

    @pl.when(pl.program_id(0) % tiles_per_seq == 0)
    def _():
        us_ref[0:CARRY_ROWS, :] = jnp.zeros((CARRY_ROWS, D_CONV), F32)

    def conv_branch_input():
        u = c_ref[...].astype(F32) * hc_ref[...].astype(F32)
        us_ref[CARRY_ROWS:, :] = u
        st_ref[...] = u[tm - (CONV_WIDTH - 1):, :]
        yc = _gated_conv(b_ref[...], wc_ref, *_delayed_from_scratch(us_ref, slice(0, tm)))
        us_ref[0:CARRY_ROWS, :] = us_ref[tm:tm + CARRY_ROWS, :]
        return yc

    _mix_tail(conv_branch_input, at_ref, gc_ref, ga_ref, x_ref, g_ref, wco_ref, wao_ref, wo_ref, h_ref, hn_ref)


def _mix_sample_kernel(b_ref, c_ref, hc_ref, st_ref, wc_ref, at_ref, gc_ref, ga_ref, x_ref, g_ref,
                       wco_ref, wao_ref, wo_ref, h_ref, hn_ref, new_st_ref, us_ref, *, steps):
    tm = x_ref.shape[0]
    nseq = tm // steps
    keep = CONV_WIDTH - 1

    def conv_branch_input():
        u = c_ref[...].astype(F32) * hc_ref[...].astype(F32)
        us_ref[:, CARRY_ROWS - keep:CARRY_ROWS, :] = st_ref[...]
        us_ref[:, CARRY_ROWS:, :] = u.reshape(nseq, steps, D_CONV)
        new_st_ref[...] = us_ref[:, CARRY_ROWS + steps - keep:, :]

        def delayed(d):
            return us_ref[:, CARRY_ROWS - d:CARRY_ROWS - d + steps, :].reshape(tm, D_CONV)

        return _gated_conv(b_ref[...], wc_ref, u, delayed(1), delayed(2))

    _mix_tail(conv_branch_input, at_ref, gc_ref, ga_ref, x_ref, g_ref, wco_ref, wao_ref, wo_ref, h_ref, hn_ref)


def _mix_specs(tm):
    resident = functools.partial(pl.BlockSpec, index_map=lambda i: (0, 0), pipeline_mode=pl.Buffered(1))
    row_spec = pl.BlockSpec((tm, D_MODEL), lambda i: (i, 0))

    def zcol(off, width):
        return pl.BlockSpec((tm, width), lambda i: (i, off // width))

    conv_in = [zcol(OFF_B, D_CONV), zcol(OFF_C, D_CONV), zcol(OFF_H, D_CONV)]
    rest_in = [
        pl.BlockSpec((CONV_WIDTH, D_CONV), lambda i: (0, 0)),
        pl.BlockSpec((tm, D_ATTN), lambda i: (i, 0)),
        zcol(OFF_GC, D_MODEL),
        zcol(OFF_GA, D_MODEL),
        row_spec,
        pl.BlockSpec((1, D_MODEL),lambda i: (0, 0)),
        resident((D_CONV, D_MODEL)),
        resident((D_ATTN, D_MODEL)),
        resident((D_MODEL, D_MODEL)),
    ]
    return conv_in, rest_in, row_spec


def _mix_prompt(z, attn, x, w_conv, g_ffn, wco, wao, wo, sides, tm, batch, seq):
    m = x.shape[0]
    assert seq % tm == 0 and tm >= CARRY_ROWS
    tiles_per_seq = seq // tm
    nsteps = m // tm
    conv_in, rest_in, row_spec = _mix_specs(tm)

    def side_spec(side):
        rows = _side_cast_rows(side.shape[0], nsteps)
        return pl.BlockSpec((rows, side.shape[1]), lambda i: (jnp.minimum(i, side.shape[0] // rows - 1), 0))

    side_specs = [side_spec(s) for s in sides]
    return pl.pallas_call(
        functools.partial(_mix_prompt_kernel, tiles_per_seq=tiles_per_seq),
        grid=(nsteps,),
        in_specs=conv_in + rest_in + side_specs,
        out_specs=[row_spec, row_spec,
                   pl.BlockSpec((None, CONV_WIDTH - 1, D_CONV), lambda i: (i // tiles_per_seq, 0, 0))] + side_specs,
        out_shape=[jax.ShapeDtypeStruct((m, D_MODEL), F32), jax.ShapeDtypeStruct((m, D_MODEL), BF16),
                   jax.ShapeDtypeStruct((batch, CONV_WIDTH - 1, D_CONV), F32)]
        + [jax.ShapeDtypeStruct(s.shape, BF16) for s in sides],
        scratch_shapes=[pltpu.VMEM((tm + CARRY_ROWS, D_CONV), F32)],
        compiler_params=_params("arbitrary"),
        name="mix_prompt",
    )(z, z, z, w_conv, attn, z, z, x, g_ffn, wco, wao, wo, *sides)


def _mix_sample(z, state, attn, x, w_conv, g_ffn, wco, wao, wo, tm, steps):
    m = x.shape[0]
    assert tm % steps == 0 and CONV_WIDTH - 1 <= min(steps, CARRY_ROWS)
    conv_in, rest_in, row_spec = _mix_specs(tm)
    state_spec = pl.BlockSpec((tm // steps, CONV_WIDTH - 1, D_CONV), lambda i: (i, 0, 0))
    return pl.pallas_call(
        functools.partial(_mix_sample_kernel, steps=steps),
        grid=(m // tm,),
        in_specs=conv_in + [state_spec] + rest_in,
        out_specs=[row_spec, row_spec, state_spec],
        out_shape=[jax.ShapeDtypeStruct((m, D_MODEL), F32), jax.ShapeDtypeStruct((m, D_MODEL), BF16),
                   jax.ShapeDtypeStruct(state.shape, F32)],
        scratch_shapes=[pltpu.VMEM((tm // steps, CARRY_ROWS + steps, D_CONV), F32)],
        compiler_params=_params("arbitrary"),
        name="mix_sample",
    )(z, z, z, state, w_conv, attn, z, z, x, g_ffn, wco, wao, wo)


def _ffn_kernel(hn_ref, h_ref, gf_ref, wg_ref, wu_ref, wd_ref, y_ref, *, final_norm):
    j = pl.program_id(1)

    @pl.when(j == 0)
    def _():
        y_ref[...] = h_ref[...]

    hn = hn_ref[...]
    a = (jax.nn.silu(jnp.dot(hn, wg_ref[...], preferred_element_type=F32))
         * jnp.dot(hn, wu_ref[...], preferred_element_type=F32))
    y_ref[...] += jnp.dot(a.astype(BF16), wd_ref[...], preferred_element_type=F32)

    if final_norm:
        @pl.when(j == pl.num_programs(1) - 1)
        def _():
            def body(r, carry):
                rows = pl.ds(pl.multiple_of(r * NORM_ROWS, NORM_ROWS), NORM_ROWS)
                y_ref[rows, :] = _rms_rows(y_ref[rows, :], gf_ref[0:1, :])
                return carry

            lax.fori_loop(0, y_ref.shape[0] // NORM_ROWS, body, 0)


def _ffn(hn, h, g_final, wg, wu, wd, tm, tf, final_norm):
    m = h.shape[0]
    return pl.pallas_call(
        functools.partial(_ffn_kernel, final_norm=final_norm),
        grid=(m // tm, D_FF // tf),
        in_specs=[
            pl.BlockSpec((tm, D_MODEL), lambda i, j: (i, 0)),
            pl.BlockSpec((tm, D_MODEL), lambda i, j: (i, 0)),
            pl.BlockSpec((1, D_MODEL),lambda i, j: (0, 0)),
            pl.BlockSpec((D_MODEL, tf), lambda i, j: (0, j)),
            pl.BlockSpec((D_MODEL, tf), lambda i, j: (0, j)),
            pl.BlockSpec((tf, D_MODEL), lambda i, j: (j, 0)),
        ],
        out_specs=pl.BlockSpec((tm, D_MODEL), lambda i, j: (i, 0)),
        out_shape=jax.ShapeDtypeStruct((m, D_MODEL), F32),
        compiler_params=_params("arbitrary", "arbitrary"),
        name="ffn",
    )(hn, h, g_final, wg, wu, wd)


IN_PROJ_TM_PROMPT = 1024
IN_PROJ_TN = 512
MIX_TM = 256
FFN_TM = 1024
FFN_TF = 512


def kernel(x_prompt, x_sample, cache_k, cache_v, state_conv, rel_bias, w_in, w_conv, w_conv_out, sinks,
           w_attn_out, w_o, g_mix, g_ffn, w_gate, w_up, w_down, g_final):
    depth = w_in.shape[0]
    batch, seq, _ = x_prompt.shape
    dec_batch, steps, _ = x_sample.shape
    wc = cache_k.shape[2]
    assert seq % WINDOW == 0 and wc == WINDOW and dec_batch % SAMPLE_BATCH_TILE == 0

    hp = x_prompt.reshape(batch * seq, D_MODEL)
    hs = x_sample.reshape(dec_batch * steps, D_MODEL)
    g_final2 = g_final.reshape(1, D_MODEL)
    kp_l, vp_l, cp_l, ks_l, vs_l, cs_l = [], [], [], [], [], []
    for l in range(depth):
        final = l == depth - 1
        lw = {
            "w_in": w_in[l],
            "g_ffn": g_ffn[l].reshape(1, D_MODEL),
        }
        g_mix_l = g_mix[l].reshape(1, D_MODEL)

        zp, lw["wo"] = _in_proj(hp, g_mix_l, lw["w_in"], w_o[l], IN_PROJ_TM_PROMPT, IN_PROJ_TN)
        zs, attn_p, lw["wco"], lw["wao"] = _in_proj_sample_attn_prompt(
            hs, g_mix_l, lw["w_in"], w_conv_out[l], w_attn_out[l], zp, rel_bias, sinks[l], IN_PROJ_TN, batch, seq)
        hp, hnp, conv_p, lw["wg"], lw["wu"], lw["wd"] = _mix_prompt(
            zp, attn_p, hp, w_conv[l], lw["g_ffn"], lw["wco"], lw["wao"], lw["wo"],
            (w_gate[l], w_up[l], w_down[l]), MIX_TM, batch, seq)
        hp = _ffn(hnp, hp, g_final2, lw["wg"], lw["wu"], lw["wd"], FFN_TM, FFN_TF, final)
        kv_p = zp.reshape(batch, seq, D_IN_PROJ)[:, seq - WINDOW:, OFF_K:OFF_V + D_KV].astype(F32)
        kp_l.append(kv_p[..., :D_KV].reshape(batch, WINDOW, N_KV_HEADS, HEAD_DIM))
        vp_l.append(kv_p[..., D_KV:].reshape(batch, WINDOW, N_KV_HEADS, HEAD_DIM))
        cp_l.append(conv_p)

        cache_kt = jnp.transpose(cache_k[l], (0, 2, 3, 1)).reshape(dec_batch, D_KV, wc)
        cache_vt = jnp.transpose(cache_v[l], (0, 2, 3, 1)).reshape(dec_batch, D_KV, wc)
        attn_s, nk, nv = _attn_sample(zs, cache_kt, cache_vt, rel_bias, sinks[l], dec_batch, steps)
        hs, hns, conv_s = _mix_sample(zs, state_conv[l], attn_s, hs, w_conv[l], lw["g_ffn"], lw["wco"], lw["wao"],
                                      lw["wo"], MIX_TM, steps)
        hs = _ffn(hns, hs, g_final2, lw["wg"], lw["wu"], lw["wd"], FFN_TM, FFN_TF, final)
        ks_l.append(nk.reshape(dec_batch, wc, N_KV_HEADS, HEAD_DIM))
        vs_l.append(nv.reshape(dec_batch, wc, N_KV_HEADS, HEAD_DIM))
        cs_l.append(conv_s)

    return (hp.reshape(batch, seq, D_MODEL), hs.reshape(dec_batch, steps, D_MODEL),
            jnp.stack(kp_l), jnp.stack(vp_l), jnp.stack(cp_l),
            jnp.stack(ks_l), jnp.stack(vs_l), jnp.stack(cs_l))
```

```python
import functools
import math

import numpy as np
import jax
import jax.numpy as jnp
from jax import lax
from jax.experimental import pallas as pl
from jax.experimental.pallas import tpu as pltpu

D_MODEL = 2048
D_CONV = D_MODEL // 2
CONV_WIDTH = 3
HEAD_DIM = 64
N_HEADS = (D_MODEL // 2) // HEAD_DIM
N_KV_HEADS = N_HEADS // 4
GROUP = N_HEADS // N_KV_HEADS
D_ATTN = N_HEADS * HEAD_DIM
D_KV = N_KV_HEADS * HEAD_DIM
WINDOW = 128
NUM_BUCKETS = 32
MAX_DISTANCE = 128
D_FF = -(-8 * D_MODEL // (3 * 256)) * 256
EPS = 1e-6
D_IN_PROJ = 3 * D_CONV + D_ATTN + 2 * D_KV + 2 * D_MODEL

OFF_GC = 0
OFF_GA = D_MODEL
OFF_B = 2 * D_MODEL
OFF_C = OFF_B + D_CONV
OFF_H = OFF_C + D_CONV
OFF_Q = OFF_H + D_CONV
OFF_K = OFF_Q + D_ATTN
OFF_V = OFF_K + D_KV

LANES = 128
NEG = -1e30
F32 = jnp.float32
BF16 = jnp.bfloat16
VMEM_LIMIT = 60 * 1024 * 1024


def _bucket_thresholds():
    max_exact = NUM_BUCKETS // 2
    d = np.arange(MAX_DISTANCE)
    ratio = np.log(np.maximum(d, 1).astype(np.float32) / np.float32(max_exact)) / np.float32(
        math.log(MAX_DISTANCE / max_exact))
    large = max_exact + (ratio * np.float32(NUM_BUCKETS - max_exact)).astype(np.int32)
    large = np.minimum(large, NUM_BUCKETS - 1)
    return [int(np.min(d[(d >= max_exact) & (large >= b)])) for b in range(max_exact + 1, NUM_BUCKETS)]


BUCKET_THRESHOLDS = _bucket_thresholds()


def _params(*semantics):
    return pltpu.CompilerParams(dimension_semantics=semantics, vmem_limit_bytes=VMEM_LIMIT)


def _rms_rows(x, g):
    ms = jnp.mean(x * x, axis=-1, keepdims=True)
    return x * lax.rsqrt(ms + EPS) * g


NORM_ROWS = 256


def _inproj_kernel(x_ref, g_ref, w_ref, side_ref, z_ref, side_bf16_ref, xn_ref):
    _inproj_norm(pl.program_id(1) == 0, x_ref, g_ref, xn_ref)
    _inproj_columns(w_ref, z_ref, xn_ref)
    side_bf16_ref[...] = side_ref[...].astype(BF16)


BF16_ROWS = 16


def _side_cast_rows(rows, nsteps):
    per_step = -(-rows // nsteps)
    per_step = -(-per_step // BF16_ROWS) * BF16_ROWS
    assert rows % per_step == 0
    return per_step


def _inproj_norm(first_column_step, x_ref, g_ref, xn_ref):
    @pl.when(first_column_step)
    def _():
        def body(r, carry):
            rows = pl.ds(pl.multiple_of(r * NORM_ROWS, NORM_ROWS), NORM_ROWS)
            xn_ref[rows, :] = _rms_rows(x_ref[rows, :], g_ref[0:1, :]).astype(xn_ref.dtype)
            return carry

        lax.fori_loop(0, x_ref.shape[0] // NORM_ROWS, body, 0)


def _inproj_columns(w_ref, z_ref, xn_ref):
    z_ref[...] = jnp.dot(xn_ref[...], w_ref[...].astype(BF16), preferred_element_type=F32).astype(z_ref.dtype)


def _in_proj(x, g, w, side, tm, tn):
    m = x.shape[0]
    n = w.shape[1]
    gate_start = n - 2 * D_MODEL
    assert gate_start % tn == 0 and n % tn == 0
    gate_block, nblocks = gate_start // tn, n // tn
    side_rows = _side_cast_rows(side.shape[0], (m // tm) * nblocks)
    side_spec = pl.BlockSpec((side_rows, side.shape[1]),
                             lambda i, j: (jnp.minimum(i * nblocks + j, side.shape[0] // side_rows - 1), 0))
    return pl.pallas_call(
        _inproj_kernel,
        grid=(m // tm, nblocks),
        in_specs=[
            pl.BlockSpec((tm, D_MODEL), lambda i, j: (i, 0)),
            pl.BlockSpec((1, D_MODEL),lambda i, j: (0, 0)),
            pl.BlockSpec((D_MODEL, tn), lambda i, j: (0, (j + gate_block) % nblocks)),
            side_spec,
        ],
        out_specs=[pl.BlockSpec((tm, tn), lambda i, j: (i, j)), side_spec],
        out_shape=[jax.ShapeDtypeStruct((m, n), BF16), jax.ShapeDtypeStruct(side.shape, BF16)],
        scratch_shapes=[pltpu.VMEM((tm, D_MODEL), BF16)],
        compiler_params=_params("arbitrary", "arbitrary"),
        name="in_proj",
    )(x, g, w, side)


def _bias_from_distance(dist, head, relb_ref):
    valid = (dist >= 0) & (dist < WINDOW)
    d = jnp.clip(dist, 0, MAX_DISTANCE - 1)
    bucket = jnp.minimum(d, NUM_BUCKETS // 2)
    for thr in BUCKET_THRESHOLDS:
        bucket = bucket + (d >= thr).astype(jnp.int32)
    bias = jnp.zeros(dist.shape, F32)
    for b in range(NUM_BUCKETS):
        bias = jnp.where(bucket == b, relb_ref[b, head], bias)
    return jnp.where(valid, bias, NEG)


def _biased_scores(s, bias):
    return jnp.where(bias > 0.5 * NEG, s + bias, NEG)


def _head_in_group_columns(q, h):
    half = D_KV // 2
    assert 2 * HEAD_DIM == half and GROUP * HEAD_DIM == D_KV
    g = h // GROUP
    src = q[:, (h // 2) * half:(h // 2 + 1) * half]
    if h % 2 != g % 2:
        src = pltpu.roll(src, HEAD_DIM, 1)
    lane_half = lax.broadcasted_iota(jnp.int32, src.shape, 1) // HEAD_DIM
    placed = jnp.where(lane_half == g % 2, src, 0.0)
    zero = jnp.zeros_like(placed)
    return jnp.concatenate([placed, zero] if g // 2 == 0 else [zero, placed], axis=1)


def _group_columns(h):
    g = h // GROUP
    return slice(g * HEAD_DIM, (g + 1) * HEAD_DIM)


INPROJ_FILL_CHUNKS = 4
PROMPT_BLOCKS_PER_STEP = 4


def _inproj_attn_kernel(x_ref, g_ref, w_ref, side0_ref, side1_ref, relb_ref, sink_ref, q_ref, kp_ref, kc_ref,
                        vp_ref, vc_ref, z_ref, o_ref, side0_bf16_ref, side1_bf16_ref,
                        xn_ref, wb_ref, bias_ref, lhs_ref, *, steps_per_seq):
    j = pl.program_id(0)
    _attn_prompt_bias(j == 0, relb_ref, bias_ref)
    _inproj_norm(j == 0, x_ref, g_ref, xn_ref)
    wb_ref[...] = w_ref[...].astype(BF16)
    side0_bf16_ref[...] = side0_ref[...].astype(BF16)
    side1_bf16_ref[...] = side1_ref[...].astype(BF16)
    chunk = xn_ref.shape[0] // INPROJ_FILL_CHUNKS

    def column_chunk(c):
        rows = slice(c * chunk, (c + 1) * chunk)

        def emit():
            z_ref[rows, :] = jnp.dot(xn_ref[rows, :], wb_ref[...], preferred_element_type=F32).astype(z_ref.dtype)
        return emit

    _attn_prompt_blocks(j % steps_per_seq != 0, sink_ref, q_ref, kp_ref, kc_ref, vp_ref, vc_ref,
                        o_ref, bias_ref, lhs_ref, fillers=[column_chunk(c) for c in range(INPROJ_FILL_CHUNKS)])


def _from_prev_mask():
    qi = lax.broadcasted_iota(jnp.int32, (WINDOW, WINDOW), 0)
    kj = lax.broadcasted_iota(jnp.int32, (WINDOW, WINDOW), 1)
    return qi, kj, kj > qi


def _attn_prompt_bias(first_step, relb_ref, bias_ref):
    @pl.when(first_step)
    def _():
        qi, kj, from_prev = _from_prev_mask()
        dist = jnp.where(from_prev, qi + WINDOW - kj, qi - kj)
        for h in range(N_HEADS):
            rows = slice(h * WINDOW, (h + 1) * WINDOW)
            bias = _bias_from_distance(dist, h, relb_ref)
            bias_ref[1, rows, :] = bias
            bias_ref[0, rows, :] = jnp.where(from_prev, NEG, bias)


def _attn_prompt_blocks(has_prev_block, sink_ref, q_ref, kp_ref, kc_ref, vp_ref, vc_ref, o_ref, bias_ref, lhs_ref,
                        fillers=()):
    _, _, from_prev = _from_prev_mask()
    k_all = jnp.concatenate([kp_ref[...], kc_ref[...]], axis=0)
    v_all = jnp.concatenate([vp_ref[...], vc_ref[...]], axis=0)

    def keys_of(blk):
        return slice(blk * WINDOW, (blk + 2) * WINDOW)

    nblk = PROMPT_BLOCKS_PER_STEP
    fillers = list(fillers)
    per_stage = -(-len(fillers) // (nblk + 2))
    scores, probs = {}, {}
    for stage in range(nblk + 2):
        if stage < nblk:
            q = q_ref[stage * WINDOW:(stage + 1) * WINDOW, :]
            scores[stage] = _block_scores(q, k_all[keys_of(stage)], lhs_ref.at[stage])
        for _ in range(min(per_stage, len(fillers))):
            fillers.pop(0)()
        if 0 <= stage - 1 < nblk:
            blk = stage - 1
            has_prev = has_prev_block.astype(jnp.int32) if blk == 0 else 1
            probs[blk] = _block_softmax(scores.pop(blk), has_prev, from_prev, sink_ref, bias_ref)
        if 0 <= stage - 2 < nblk:
            blk = stage - 2
            o_ref[blk * WINDOW:(blk + 1) * WINDOW, :] = _block_output(
                *probs.pop(blk), v_all[keys_of(blk)]).astype(o_ref.dtype)


def _block_scores(q, kk, lhs_ref):
    q = q.astype(F32) * (HEAD_DIM ** -0.5)
    for h in range(N_HEADS):
        lhs_ref[h * WINDOW:(h + 1) * WINDOW, :] = _head_in_group_columns(q, h).astype(BF16)
    return lax.dot_general(lhs_ref[...], kk, (((1,), (1,)), ((), ())), preferred_element_type=F32)


def _block_softmax(s, has_prev, from_prev, sink_ref, bias_ref):
    probs, sink_terms = [], []
    for h in range(N_HEADS):
        rows = slice(h * WINDOW, (h + 1) * WINDOW)
        sh = _biased_scores(jnp.where(from_prev, s[rows, :WINDOW], s[rows, WINDOW:]), bias_ref[has_prev, rows, :])
        m = jnp.maximum(jnp.max(sh, axis=-1, keepdims=True), sink_ref[h])
        p = jnp.exp(sh - m)
        probs.append(jnp.concatenate([jnp.where(from_prev, p, 0.0), jnp.where(from_prev, 0.0, p)],
                                     axis=1).astype(BF16))
        sink_terms.append(jnp.exp(sink_ref[h] - m))
    return probs, sink_terms


def _block_output(probs, sink_terms, vv):
    assert 2 * HEAD_DIM == LANES and GROUP % 2 == 0
    ones = jnp.ones((vv.shape[0], HEAD_DIM), BF16)
    low_half = lax.broadcasted_iota(jnp.int32, (WINDOW, LANES), 1) < HEAD_DIM
    outs = []
    for g in range(N_KV_HEADS):
        vg = vv[:, g * HEAD_DIM:(g + 1) * HEAD_DIM]
        even_heads = [g * GROUP + r for r in range(0, GROUP, 2)]
        pe = jnp.concatenate([probs[h] for h in even_heads], axis=0)
        po = jnp.concatenate([probs[h + 1] for h in even_heads], axis=0)
        oe = jnp.dot(pe, jnp.concatenate([vg, ones], axis=1), preferred_element_type=F32)
        oo = jnp.dot(po, jnp.concatenate([ones, vg], axis=1), preferred_element_type=F32)
        for k, h in enumerate(even_heads):
            a = oe[k * WINDOW:(k + 1) * WINDOW]
            b = oo[k * WINDOW:(k + 1) * WINDOW]
            out_pair = jnp.where(low_half, a, b)
            sum_pair = pltpu.roll(jnp.where(low_half, b, a), HEAD_DIM, 1)
            sink_pair = jnp.where(low_half, sink_terms[h], sink_terms[h + 1])
            outs.append(out_pair * (1.0 / (sum_pair + sink_pair)))
    return jnp.concatenate(outs, axis=1)


def _in_proj_sample_attn_prompt(x, g, w, side0, side1, zp, rel_bias, sinks, tn, batch, seq):
    tm = x.shape[0]
    n = w.shape[1]
    gate_start = n - 2 * D_MODEL
    assert gate_start % tn == 0 and n % tn == 0
    gate_block, nsteps = gate_start // tn, n // tn
    bps = PROMPT_BLOCKS_PER_STEP
    assert seq % (bps * WINDOW) == 0
    steps_per_seq = seq // (bps * WINDOW)
    attn_steps = batch * steps_per_seq
    assert attn_steps <= nsteps

    def astep(j):
        return jnp.minimum(j, attn_steps - 1)

    def cur(off, width):
        return lambda j: (astep(j), off // width)

    def prev(off, width):
        return lambda j: (astep(j) * bps - jnp.minimum(astep(j) % steps_per_seq, 1), off // width)

    def side_spec(side):
        rows = _side_cast_rows(side.shape[0], nsteps)
        return pl.BlockSpec((rows, side.shape[1]), lambda j: (jnp.minimum(j, side.shape[0] // rows - 1), 0))

    smem = pl.BlockSpec(memory_space=pltpu.SMEM)
    return pl.pallas_call(
        functools.partial(_inproj_attn_kernel, steps_per_seq=steps_per_seq),
        grid=(nsteps,),
        in_specs=[
            pl.BlockSpec((tm, D_MODEL), lambda j: (0, 0)),
            pl.BlockSpec((1, D_MODEL),lambda j: (0, 0)),
            pl.BlockSpec((D_MODEL, tn), lambda j: (0, (j + gate_block) % nsteps)),
            side_spec(side0),
            side_spec(side1),
            smem,
            smem,
            pl.BlockSpec((bps * WINDOW, D_ATTN), cur(OFF_Q, D_ATTN)),
            pl.BlockSpec((WINDOW, D_KV), prev(OFF_K, D_KV)),
            pl.BlockSpec((bps * WINDOW, D_KV), cur(OFF_K, D_KV)),
            pl.BlockSpec((WINDOW, D_KV), prev(OFF_V, D_KV)),
            pl.BlockSpec((bps * WINDOW, D_KV), cur(OFF_V, D_KV)),
        ],
        out_specs=[
            pl.BlockSpec((tm, tn), lambda j: (0, j)),
            pl.BlockSpec((bps * WINDOW, D_ATTN), lambda j: (j, 0)),
            side_spec(side0),
            side_spec(side1),
        ],
        out_shape=[
            jax.ShapeDtypeStruct((tm, n), BF16),
            jax.ShapeDtypeStruct((nsteps * bps * WINDOW, D_ATTN), BF16),
            jax.ShapeDtypeStruct(side0.shape, BF16),
            jax.ShapeDtypeStruct(side1.shape, BF16),
        ],
        scratch_shapes=[pltpu.VMEM((tm, D_MODEL), BF16),
                        pltpu.VMEM((D_MODEL, tn), BF16),
                        pltpu.VMEM((2, N_HEADS * WINDOW, WINDOW), F32),
                        pltpu.VMEM((bps, N_HEADS * WINDOW, D_KV), BF16)],
        compiler_params=_params("arbitrary"),
        name="in_proj_sample_attn_prompt",
    )(x, g, w, side0, side1, rel_bias, sinks, zp, zp, zp, zp, zp)


SAMPLE_BATCH_TILE = 16


def _attn_sample_kernel(relb_ref, sink_ref, q_ref, kn_ref, vn_ref, ckt_ref, cvt_ref,
                        o_ref, nk_ref, nv_ref,
                        bias1_ref, bias2_ref, sinkv_ref, z_ref, s_ref, p_ref, oh_ref, *, steps):
    bt, wc, _ = nk_ref.shape
    tile_rows = bt * steps

    @pl.when(pl.program_id(0) == 0)
    def _():
        t = lax.broadcasted_iota(jnp.int32, (steps, wc), 0)
        j = lax.broadcasted_iota(jnp.int32, (steps, wc), 1)
        t2 = lax.broadcasted_iota(jnp.int32, (steps, tile_rows), 0)
        j2 = lax.broadcasted_iota(jnp.int32, (steps, tile_rows), 1)
        own = jnp.where(j2 < steps, t2 - j2, -1)
        for h in range(N_HEADS):
            sl = slice(h * steps, (h + 1) * steps)
            bias1_ref[sl, :] = _bias_from_distance(t + wc - j, h, relb_ref)
            bias2_ref[0, sl, :] = _bias_from_distance(own, h, relb_ref)
            sinkv_ref[sl, :] = jnp.full((steps, LANES), sink_ref[h], F32)
        for b in range(1, bt):
            bias2_ref[b] = pltpu.roll(bias2_ref[0], b * steps, 1)

    kn = kn_ref[...]
    vn = vn_ref[...]
    nk_ref[:, wc - steps:wc, :] = kn.astype(F32).reshape(bt, steps, D_KV)
    nv_ref[:, wc - steps:wc, :] = vn.astype(F32).reshape(bt, steps, D_KV)

    q = q_ref[...].astype(F32) * (HEAD_DIM ** -0.5)
    for h in range(N_HEADS):
        z_ref[:, h * steps:(h + 1) * steps, :] = _head_in_group_columns(q, h).reshape(bt, steps, D_KV)

    for b in range(bt):
        ck = ckt_ref[b].T
        nk_ref[b, 0:wc - steps, :] = ck[steps:, :]
        keys = jnp.concatenate([ck.astype(BF16), kn], axis=0)
        s_ref[b] = lax.dot_general(z_ref[b].astype(BF16), keys, (((1,), (1,)), ((), ())),
                                   preferred_element_type=F32)

    bias = jnp.concatenate([jnp.broadcast_to(bias1_ref[...], (bt,) + bias1_ref.shape), bias2_ref[...]], axis=2)
    s = _biased_scores(s_ref[...], bias)
    sink = sinkv_ref[...][None]
    m = jnp.maximum(jnp.max(s, axis=-1, keepdims=True), sink)
    p_ref[...] = jnp.exp(s - jnp.concatenate([m] * (s.shape[-1] // LANES), axis=-1)).astype(BF16)
    sink_term = jnp.exp(sink - m)

    vn_ones = jnp.concatenate([vn, jnp.ones((tile_rows, LANES), BF16)], axis=1)
    for b in range(bt):
        cv = cvt_ref[b].T
        nv_ref[b, 0:wc - steps, :] = cv[steps:, :]
        cv_ones = jnp.concatenate([cv.astype(BF16), jnp.ones((wc, LANES), BF16)], axis=1)
        oh_ref[b] = jnp.dot(p_ref[b], jnp.concatenate([cv_ones, vn_ones], axis=0), preferred_element_type=F32)

    inv = 1.0 / (oh_ref[:, :, D_KV:] + sink_term)
    outs = []
    for h in range(N_HEADS):
        rows = slice(h * steps, (h + 1) * steps)
        outs.append((oh_ref[:, rows, _group_columns(h)] * inv[:, rows, :HEAD_DIM]).reshape(tile_rows, HEAD_DIM))
    o_ref[...] = jnp.concatenate(outs, axis=1).astype(o_ref.dtype)


def _attn_sample(z, cache_kt, cache_vt, rel_bias, sinks, batch, steps):
    wc = cache_kt.shape[2]
    bt = SAMPLE_BATCH_TILE
    rows = bt * steps
    smem = pl.BlockSpec(memory_space=pltpu.SMEM)
    cache_t_spec = pl.BlockSpec((bt, D_KV, wc), lambda i: (i, 0, 0))
    cache_spec = pl.BlockSpec((bt, wc, D_KV), lambda i: (i, 0, 0))
    return pl.pallas_call(
        functools.partial(_attn_sample_kernel, steps=steps),
        grid=(batch // bt,),
        in_specs=[
            smem,
            smem,
            pl.BlockSpec((rows, D_ATTN), lambda i: (i, OFF_Q // D_ATTN)),
            pl.BlockSpec((rows, D_KV), lambda i: (i, OFF_K // D_KV)),
            pl.BlockSpec((rows, D_KV), lambda i: (i, OFF_V // D_KV)),
            cache_t_spec,
            cache_t_spec,
        ],
        out_specs=[
            pl.BlockSpec((rows, D_ATTN), lambda i: (i, 0)),
            cache_spec,
            cache_spec,
        ],
        out_shape=[
            jax.ShapeDtypeStruct((batch * steps, D_ATTN), BF16),
            jax.ShapeDtypeStruct((batch, wc, D_KV), F32),
            jax.ShapeDtypeStruct((batch, wc, D_KV), F32),
        ],
        scratch_shapes=[
            pltpu.VMEM((N_HEADS * steps, wc), F32),
            pltpu.VMEM((bt, N_HEADS * steps, rows), F32),
            pltpu.VMEM((N_HEADS * steps, LANES), F32),
            pltpu.VMEM((bt, N_HEADS * steps, D_KV), F32),
            pltpu.VMEM((bt, N_HEADS * steps, wc + rows), F32),
            pltpu.VMEM((bt, N_HEADS * steps, wc + rows), BF16),
            pltpu.VMEM((bt, N_HEADS * steps, D_KV + LANES), F32),
        ],
        compiler_params=_params("arbitrary"),
        name="attn_sample",
    )(rel_bias, sinks, z, z, z, cache_kt, cache_vt)


CARRY_ROWS = 8


def _gated_conv(b, wc_ref, u, u1, u2):
    conv = wc_ref[0:1, :] * u2 + wc_ref[1:2, :] * u1 + wc_ref[2:3, :] * u
    return (b.astype(F32) * conv).astype(BF16)


def _mix_tail(conv_branch_input, at_ref, gc_ref, ga_ref, x_ref, g_ref, wco_ref, wao_ref, wo_ref, h_ref, hn_ref):
    y_conv = jnp.dot(conv_branch_input(), wco_ref[...], preferred_element_type=F32)
    y_attn = jnp.dot(at_ref[...], wao_ref[...], preferred_element_type=F32)
    merged = (jax.nn.sigmoid(gc_ref[...].astype(F32)) * y_conv
              + jax.nn.sigmoid(ga_ref[...].astype(F32)) * y_attn)
    h = x_ref[...] + jnp.dot(merged.astype(BF16), wo_ref[...], preferred_element_type=F32)
    h_ref[...] = h
    hn_ref[...] = _rms_rows(h, g_ref[0:1, :]).astype(hn_ref.dtype)


def _delayed_from_scratch(us_ref, rows):
    def at(delay):
        return us_ref[CARRY_ROWS - delay + rows.start:CARRY_ROWS - delay + rows.stop, :]
    return at(0), at(1), at(2)


def _mix_prompt_kernel(b_ref, c_ref, hc_ref, wc_ref, at_ref, gc_ref, ga_ref, x_ref, g_ref,
                       wco_ref, wao_ref, wo_ref, side0_ref, side1_ref, side2_ref,
                       h_ref, hn_ref, st_ref, side0_bf16_ref, side1_bf16_ref, side2_bf16_ref,
                       us_ref, *, tiles_per_seq):
    tm = x_ref.shape[0]
    side0_bf16_ref[...] = side0_ref[...].astype(BF16)
    side1_bf16_ref[...] = side1_ref[...].astype(BF16)
    side2_bf16_ref[...] = side2_ref[...].astype(BF16)

    @pl.when(pl.program_id(0) % tiles_per_seq == 0)
    def _():
        us_ref[0:CARRY_ROWS, :] = jnp.zeros((CARRY_ROWS, D_CONV), F32)

    def conv_branch_input():
        u = c_ref[...].astype(F32) * hc_ref[...].astype(F32)
        us_ref[CARRY_ROWS:, :] = u
        st_ref[...] = u[tm - (CONV_WIDTH - 1):, :]
        yc = _gated_conv(b_ref[...], wc_ref, *_delayed_from_scratch(us_ref, slice(0, tm)))
        us_ref[0:CARRY_ROWS, :] = us_ref[tm:tm + CARRY_ROWS, :]
        return yc

    _mix_tail(conv_branch_input, at_ref, gc_ref, ga_ref, x_ref, g_ref, wco_ref, wao_ref, wo_ref, h_ref, hn_ref)


def _mix_sample_kernel(b_ref, c_ref, hc_ref, st_ref, wc_ref, at_ref, gc_ref, ga_ref, x_ref, g_ref,
                       wco_ref, wao_ref, wo_ref, h_ref, hn_ref, new_st_ref, us_ref, *, steps):
    tm = x_ref.shape[0]
    nseq = tm // steps
    keep = CONV_WIDTH - 1

    def conv_branch_input():
        u = c_ref[...].astype(F32) * hc_ref[...].astype(F32)
        us_ref[:, CARRY_ROWS - keep:CARRY_ROWS, :] = st_ref[...]
        us_ref[:, CARRY_ROWS:, :] = u.reshape(nseq, steps, D_CONV)
        new_st_ref[...] = us_ref[:, CARRY_ROWS + steps - keep:, :]

        def delayed(d):
            return us_ref[:, CARRY_ROWS - d:CARRY_ROWS - d + steps, :].reshape(tm, D_CONV)

        return _gated_conv(b_ref[...], wc_ref, u, delayed(1), delayed(2))

    _mix_tail(conv_branch_input, at_ref, gc_ref, ga_ref, x_ref, g_ref, wco_ref, wao_ref, wo_ref, h_ref, hn_ref)


def _mix_specs(tm):
    resident = functools.partial(pl.BlockSpec, index_map=lambda i: (0, 0), pipeline_mode=pl.Buffered(1))
    row_spec = pl.BlockSpec((tm, D_MODEL), lambda i: (i, 0))

    def zcol(off, width):
        return pl.BlockSpec((tm, width), lambda i: (i, off // width))

    conv_in = [zcol(OFF_B, D_CONV), zcol(OFF_C, D_CONV), zcol(OFF_H, D_CONV)]
    rest_in = [
        pl.BlockSpec((CONV_WIDTH, D_CONV), lambda i: (0, 0)),
        pl.BlockSpec((tm, D_ATTN), lambda i: (i, 0)),
        zcol(OFF_GC, D_MODEL),
        zcol(OFF_GA, D_MODEL),
        row_spec,
        pl.BlockSpec((1, D_MODEL),lambda i: (0, 0)),
        resident((D_CONV, D_MODEL)),
        resident((D_ATTN, D_MODEL)),
        resident((D_MODEL, D_MODEL)),
    ]
    return conv_in, rest_in, row_spec


def _mix_prompt(z, attn, x, w_conv, g_ffn, wco, wao, wo, sides, tm, batch, seq):
    m = x.shape[0]
    assert seq % tm == 0 and tm >= CARRY_ROWS
    tiles_per_seq = seq // tm
    nsteps = m // tm
    conv_in, rest_in, row_spec = _mix_specs(tm)

    def side_spec(side):
        rows = _side_cast_rows(side.shape[0], nsteps)
        return pl.BlockSpec((rows, side.shape[1]), lambda i: (jnp.minimum(i, side.shape[0] // rows - 1), 0))

    side_specs = [side_spec(s) for s in sides]
    return pl.pallas_call(
        functools.partial(_mix_prompt_kernel, tiles_per_seq=tiles_per_seq),
        grid=(nsteps,),
        in_specs=conv_in + rest_in + side_specs,
        out_specs=[row_spec, row_spec,
                   pl.BlockSpec((None, CONV_WIDTH - 1, D_CONV), lambda i: (i // tiles_per_seq, 0, 0))] + side_specs,
        out_shape=[jax.ShapeDtypeStruct((m, D_MODEL), F32), jax.ShapeDtypeStruct((m, D_MODEL), BF16),
                   jax.ShapeDtypeStruct((batch, CONV_WIDTH - 1, D_CONV), F32)]
        + [jax.ShapeDtypeStruct(s.shape, BF16) for s in sides],
        scratch_shapes=[pltpu.VMEM((tm + CARRY_ROWS, D_CONV), F32)],
        compiler_params=_params("arbitrary"),
        name="mix_prompt",
    )(z, z, z, w_conv, attn, z, z, x, g_ffn, wco, wao, wo, *sides)


def _mix_sample(z, state, attn, x, w_conv, g_ffn, wco, wao, wo, tm, steps):
    m = x.shape[0]
    assert tm % steps == 0 and CONV_WIDTH - 1 <= min(steps, CARRY_ROWS)
    conv_in, rest_in, row_spec = _mix_specs(tm)
    state_spec = pl.BlockSpec((tm // steps, CONV_WIDTH - 1, D_CONV), lambda i: (i, 0, 0))
    return pl.pallas_call(
        functools.partial(_mix_sample_kernel, steps=steps),
        grid=(m // tm,),
        in_specs=conv_in + [state_spec] + rest_in,
        out_specs=[row_spec, row_spec, state_spec],
        out_shape=[jax.ShapeDtypeStruct((m, D_MODEL), F32), jax.ShapeDtypeStruct((m, D_MODEL), BF16),
                   jax.ShapeDtypeStruct(state.shape, F32)],
        scratch_shapes=[pltpu.VMEM((tm // steps, CARRY_ROWS + steps, D_CONV), F32)],
        compiler_params=_params("arbitrary"),
        name="mix_sample",
    )(z, z, z, state, w_conv, attn, z, z, x, g_ffn, wco, wao, wo)


def _ffn_kernel(hn_ref, h_ref, gf_ref, wg_ref, wu_ref, wd_ref, y_ref, *, final_norm):
    j = pl.program_id(1)

    @pl.when(j == 0)
    def _():
        y_ref[...] = h_ref[...]

    hn = hn_ref[...]
    a = (jax.nn.silu(jnp.dot(hn, wg_ref[...], preferred_element_type=F32))
         * jnp.dot(hn, wu_ref[...], preferred_element_type=F32))
    y_ref[...] += jnp.dot(a.astype(BF16), wd_ref[...], preferred_element_type=F32)

    if final_norm:
        @pl.when(j == pl.num_programs(1) - 1)
        def _():
            def body(r, carry):
                rows = pl.ds(pl.multiple_of(r * NORM_ROWS, NORM_ROWS), NORM_ROWS)
                y_ref[rows, :] = _rms_rows(y_ref[rows, :], gf_ref[0:1, :])
                return carry

            lax.fori_loop(0, y_ref.shape[0] // NORM_ROWS, body, 0)


def _ffn(hn, h, g_final, wg, wu, wd, tm, tf, final_norm):
    m = h.shape[0]
    return pl.pallas_call(
        functools.partial(_ffn_kernel, final_norm=final_norm),
        grid=(m // tm, D_FF // tf),
        in_specs=[
            pl.BlockSpec((tm, D_MODEL), lambda i, j: (i, 0)),
            pl.BlockSpec((tm, D_MODEL), lambda i, j: (i, 0)),
            pl.BlockSpec((1, D_MODEL),lambda i, j: (0, 0)),
            pl.BlockSpec((D_MODEL, tf), lambda i, j: (0, j)),
            pl.BlockSpec((D_MODEL, tf), lambda i, j: (0, j)),
            pl.BlockSpec((tf, D_MODEL), lambda i, j: (j, 0)),
        ],
        out_specs=pl.BlockSpec((tm, D_MODEL), lambda i, j: (i, 0)),
        out_shape=jax.ShapeDtypeStruct((m, D_MODEL), F32),
        compiler_params=_params("arbitrary", "arbitrary"),
        name="ffn",
    )(hn, h, g_final, wg, wu, wd)


IN_PROJ_TM_PROMPT = 2048
IN_PROJ_TN = 512
MIX_TM = 256
FFN_TM = 1024
FFN_TF = 256


def kernel(x_prompt, x_sample, cache_k, cache_v, state_conv, rel_bias, w_in, w_conv, w_conv_out, sinks,
           w_attn_out, w_o, g_mix, g_ffn, w_gate, w_up, w_down, g_final):
    depth = w_in.shape[0]
    batch, seq, _ = x_prompt.shape
    dec_batch, steps, _ = x_sample.shape
    wc = cache_k.shape[2]
    assert seq % WINDOW == 0 and wc == WINDOW and dec_batch % SAMPLE_BATCH_TILE == 0

    hp = x_prompt.reshape(batch * seq, D_MODEL)
    hs = x_sample.reshape(dec_batch * steps, D_MODEL)
    g_final2 = g_final.reshape(1, D_MODEL)
    kp_l, vp_l, cp_l, ks_l, vs_l, cs_l = [], [], [], [], [], []
    for l in range(depth):
        final = l == depth - 1
        lw = {
            "w_in": w_in[l],
            "g_ffn": g_ffn[l].reshape(1, D_MODEL),
        }
        g_mix_l = g_mix[l].reshape(1, D_MODEL)

        zp, lw["wo"] = _in_proj(hp, g_mix_l, lw["w_in"], w_o[l], IN_PROJ_TM_PROMPT, IN_PROJ_TN)
        zs, attn_p, lw["wco"], lw["wao"] = _in_proj_sample_attn_prompt(
            hs, g_mix_l, lw["w_in"], w_conv_out[l], w_attn_out[l], zp, rel_bias, sinks[l], IN_PROJ_TN, batch, seq)
        hp, hnp, conv_p, lw["wg"], lw["wu"], lw["wd"] = _mix_prompt(
            zp, attn_p, hp, w_conv[l], lw["g_ffn"], lw["wco"], lw["wao"], lw["wo"],
            (w_gate[l], w_up[l], w_down[l]), MIX_TM, batch, seq)
        hp = _ffn(hnp, hp, g_final2, lw["wg"], lw["wu"], lw["wd"], FFN_TM, FFN_TF, final)
        kv_p = zp.reshape(batch, seq, D_IN_PROJ)[:, seq - WINDOW:, OFF_K:OFF_V + D_KV].astype(F32)
        kp_l.append(kv_p[..., :D_KV].reshape(batch, WINDOW, N_KV_HEADS, HEAD_DIM))
        vp_l.append(kv_p[..., D_KV:].reshape(batch, WINDOW, N_KV_HEADS, HEAD_DIM))
        cp_l.append(conv_p)

        cache_kt = jnp.transpose(cache_k[l], (0, 2, 3, 1)).reshape(dec_batch, D_KV, wc)
        cache_vt = jnp.transpose(cache_v[l], (0, 2, 3, 1)).reshape(dec_batch, D_KV, wc)
        attn_s, nk, nv = _attn_sample(zs, cache_kt, cache_vt, rel_bias, sinks[l], dec_batch, steps)
        hs, hns, conv_s = _mix_sample(zs, state_conv[l], attn_s, hs, w_conv[l], lw["g_ffn"], lw["wco"], lw["wao"],
                                      lw["wo"], MIX_TM, steps)
        hs = _ffn(hns, hs, g_final2, lw["wg"], lw["wu"], lw["wd"], FFN_TM, FFN_TF, final)
        ks_l.append(nk.reshape(dec_batch, wc, N_KV_HEADS, HEAD_DIM))
        vs_l.append(nv.reshape(dec_batch, wc, N_KV_HEADS, HEAD_DIM))
        cs_l.append(conv_s)

    return (hp.reshape(batch, seq, D_MODEL), hs.reshape(dec_batch, steps, D_MODEL),
            jnp.stack(kp_l), jnp.stack(vp_l), jnp.stack(cp_l),
            jnp.stack(ks_l), jnp.stack(vs_l), jnp.stack(cs_l))
```

```python
import functools
import math

import numpy as np
import jax
import jax.numpy as jnp
from jax import lax
from jax.experimental import pallas as pl
from jax.experimental.pallas import tpu as pltpu

D_MODEL = 2048
D_CONV = D_MODEL // 2
CONV_WIDTH = 3
HEAD_DIM = 64
N_HEADS = (D_MODEL // 2) // HEAD_DIM
N_KV_HEADS = N_HEADS // 4
GROUP = N_HEADS // N_KV_HEADS
D_ATTN = N_HEADS * HEAD_DIM
D_KV = N_KV_HEADS * HEAD_DIM
WINDOW = 128
NUM_BUCKETS = 32
MAX_DISTANCE = 128
D_FF = -(-8 * D_MODEL // (3 * 256)) * 256
EPS = 1e-6
D_IN_PROJ = 3 * D_CONV + D_ATTN + 2 * D_KV + 2 * D_MODEL

OFF_GC = 0
OFF_GA = D_MODEL
OFF_B = 2 * D_MODEL
OFF_C = OFF_B + D_CONV
OFF_H = OFF_C + D_CONV
OFF_Q = OFF_H + D_CONV
OFF_K = OFF_Q + D_ATTN
OFF_V = OFF_K + D_KV

LANES = 128
NEG = -1e30
F32 = jnp.float32
BF16 = jnp.bfloat16
VMEM_LIMIT = 60 * 1024 * 1024


def _bucket_thresholds():
    max_exact = NUM_BUCKETS // 2
    d = np.arange(MAX_DISTANCE)
    ratio = np.log(np.maximum(d, 1).astype(np.float32) / np.float32(max_exact)) / np.float32(
        math.log(MAX_DISTANCE / max_exact))
    large = max_exact + (ratio * np.float32(NUM_BUCKETS - max_exact)).astype(np.int32)
    large = np.minimum(large, NUM_BUCKETS - 1)
    return [int(np.min(d[(d >= max_exact) & (large >= b)])) for b in range(max_exact + 1, NUM_BUCKETS)]


BUCKET_THRESHOLDS = _bucket_thresholds()


def _params(*semantics):
    return pltpu.CompilerParams(dimension_semantics=semantics, vmem_limit_bytes=VMEM_LIMIT)


def _rms_rows(x, g):
    ms = jnp.mean(x * x, axis=-1, keepdims=True)
    return x * lax.rsqrt(ms + EPS) * g


NORM_ROWS = 256


def _inproj_kernel(x_ref, g_ref, w_ref, side_ref, z_ref, side_bf16_ref, xn_ref):
    _inproj_norm(pl.program_id(1) == 0, x_ref, g_ref, xn_ref)
    _inproj_columns(w_ref, z_ref, xn_ref)
    side_bf16_ref[...] = side_ref[...].astype(BF16)


BF16_ROWS = 16


def _side_cast_rows(rows, nsteps):
    per_step = -(-rows // nsteps)
    per_step = -(-per_step // BF16_ROWS) * BF16_ROWS
    assert rows % per_step == 0
    return per_step


def _inproj_norm(first_column_step, x_ref, g_ref, xn_ref):
    @pl.when(first_column_step)
    def _():
        def body(r, carry):
            rows = pl.ds(pl.multiple_of(r * NORM_ROWS, NORM_ROWS), NORM_ROWS)
            xn_ref[rows, :] = _rms_rows(x_ref[rows, :], g_ref[0:1, :]).astype(xn_ref.dtype)
            return carry

        lax.fori_loop(0, x_ref.shape[0] // NORM_ROWS, body, 0)


def _inproj_columns(w_ref, z_ref, xn_ref):
    z_ref[...] = jnp.dot(xn_ref[...], w_ref[...].astype(BF16), preferred_element_type=F32).astype(z_ref.dtype)


def _in_proj(x, g, w, side, tm, tn):
    m = x.shape[0]
    n = w.shape[1]
    gate_start = n - 2 * D_MODEL
    assert gate_start % tn == 0 and n % tn == 0
    gate_block, nblocks = gate_start // tn, n // tn
    side_rows = _side_cast_rows(side.shape[0], (m // tm) * nblocks)
    side_spec = pl.BlockSpec((side_rows, side.shape[1]),
                             lambda i, j: (jnp.minimum(i * nblocks + j, side.shape[0] // side_rows - 1), 0))
    return pl.pallas_call(
        _inproj_kernel,
        grid=(m // tm, nblocks),
        in_specs=[
            pl.BlockSpec((tm, D_MODEL), lambda i, j: (i, 0)),
            pl.BlockSpec((1, D_MODEL),lambda i, j: (0, 0)),
            pl.BlockSpec((D_MODEL, tn), lambda i, j: (0, (j + gate_block) % nblocks)),
            side_spec,
        ],
        out_specs=[pl.BlockSpec((tm, tn), lambda i, j: (i, j)), side_spec],
        out_shape=[jax.ShapeDtypeStruct((m, n), BF16), jax.ShapeDtypeStruct(side.shape, BF16)],
        scratch_shapes=[pltpu.VMEM((tm, D_MODEL), BF16)],
        compiler_params=_params("arbitrary", "arbitrary"),
        name="in_proj",
    )(x, g, w, side)


def _bias_from_distance(dist, head, relb_ref):
    valid = (dist >= 0) & (dist < WINDOW)
    d = jnp.clip(dist, 0, MAX_DISTANCE - 1)
    bucket = jnp.minimum(d, NUM_BUCKETS // 2)
    for thr in BUCKET_THRESHOLDS:
        bucket = bucket + (d >= thr).astype(jnp.int32)
    bias = jnp.zeros(dist.shape, F32)
    for b in range(NUM_BUCKETS):
        bias = jnp.where(bucket == b, relb_ref[b, head], bias)
    return jnp.where(valid, bias, NEG)


def _biased_scores(s, bias):
    return jnp.where(bias > 0.5 * NEG, s + bias, NEG)


def _head_in_group_columns(q, h):
    half = D_KV // 2
    assert 2 * HEAD_DIM == half and GROUP * HEAD_DIM == D_KV
    g = h // GROUP
    src = q[:, (h // 2) * half:(h // 2 + 1) * half]
    if h % 2 != g % 2:
        src = pltpu.roll(src, HEAD_DIM, 1)
    lane_half = lax.broadcasted_iota(jnp.int32, src.shape, 1) // HEAD_DIM
    placed = jnp.where(lane_half == g % 2, src, 0.0)
    zero = jnp.zeros_like(placed)
    return jnp.concatenate([placed, zero] if g // 2 == 0 else [zero, placed], axis=1)


def _group_columns(h):
    g = h // GROUP
    return slice(g * HEAD_DIM, (g + 1) * HEAD_DIM)


INPROJ_FILL_CHUNKS = 4
PROMPT_BLOCKS_PER_STEP = 4


def _inproj_attn_kernel(x_ref, g_ref, w_ref, side0_ref, side1_ref, relb_ref, sink_ref, q_ref, kp_ref, kc_ref,
                        vp_ref, vc_ref, z_ref, o_ref, side0_bf16_ref, side1_bf16_ref,
                        xn_ref, wb_ref, bias_ref, lhs_ref, *, steps_per_seq):
    j = pl.program_id(0)
    _attn_prompt_bias(j == 0, relb_ref, bias_ref)
    _inproj_norm(j == 0, x_ref, g_ref, xn_ref)
    wb_ref[...] = w_ref[...].astype(BF16)
    side0_bf16_ref[...] = side0_ref[...].astype(BF16)
    side1_bf16_ref[...] = side1_ref[...].astype(BF16)
    chunk = xn_ref.shape[0] // INPROJ_FILL_CHUNKS

    def column_chunk(c):
        rows = slice(c * chunk, (c + 1) * chunk)

        def emit():
            z_ref[rows, :] = jnp.dot(xn_ref[rows, :], wb_ref[...], preferred_element_type=F32).astype(z_ref.dtype)
        return emit

    _attn_prompt_blocks(j % steps_per_seq != 0, sink_ref, q_ref, kp_ref, kc_ref, vp_ref, vc_ref,
                        o_ref, bias_ref, lhs_ref, fillers=[column_chunk(c) for c in range(INPROJ_FILL_CHUNKS)])


def _from_prev_mask():
    qi = lax.broadcasted_iota(jnp.int32, (WINDOW, WINDOW), 0)
    kj = lax.broadcasted_iota(jnp.int32, (WINDOW, WINDOW), 1)
    return qi, kj, kj > qi


def _attn_prompt_bias(first_step, relb_ref, bias_ref):
    @pl.when(first_step)
    def _():
        qi, kj, from_prev = _from_prev_mask()
        dist = jnp.where(from_prev, qi + WINDOW - kj, qi - kj)
        for h in range(N_HEADS):
            rows = slice(h * WINDOW, (h + 1) * WINDOW)
            bias = _bias_from_distance(dist, h, relb_ref)
            bias_ref[1, rows, :] = bias
            bias_ref[0, rows, :] = jnp.where(from_prev, NEG, bias)


def _attn_prompt_blocks(has_prev_block, sink_ref, q_ref, kp_ref, kc_ref, vp_ref, vc_ref, o_ref, bias_ref, lhs_ref,
                        fillers=()):
    _, _, from_prev = _from_prev_mask()
    k_all = jnp.concatenate([kp_ref[...], kc_ref[...]], axis=0)
    v_all = jnp.concatenate([vp_ref[...], vc_ref[...]], axis=0)

    def keys_of(blk):
        return slice(blk * WINDOW, (blk + 2) * WINDOW)

    nblk = PROMPT_BLOCKS_PER_STEP
    fillers = list(fillers)
    per_stage = -(-len(fillers) // (nblk + 2))
    scores, probs = {}, {}
    for stage in range(nblk + 2):
        if stage < nblk:
            q = q_ref[stage * WINDOW:(stage + 1) * WINDOW, :]
            scores[stage] = _block_scores(q, k_all[keys_of(stage)], lhs_ref.at[stage])
        for _ in range(min(per_stage, len(fillers))):
            fillers.pop(0)()
        if 0 <= stage - 1 < nblk:
            blk = stage - 1
            has_prev = has_prev_block.astype(jnp.int32) if blk == 0 else 1
            probs[blk] = _block_softmax(scores.pop(blk), has_prev, from_prev, sink_ref, bias_ref)
        if 0 <= stage - 2 < nblk:
            blk = stage - 2
            o_ref[blk * WINDOW:(blk + 1) * WINDOW, :] = _block_output(
                *probs.pop(blk), v_all[keys_of(blk)]).astype(o_ref.dtype)


def _block_scores(q, kk, lhs_ref):
    q = q.astype(F32) * (HEAD_DIM ** -0.5)
    for h in range(N_HEADS):
        lhs_ref[h * WINDOW:(h + 1) * WINDOW, :] = _head_in_group_columns(q, h).astype(BF16)
    return lax.dot_general(lhs_ref[...], kk, (((1,), (1,)), ((), ())), preferred_element_type=F32)


def _block_softmax(s, has_prev, from_prev, sink_ref, bias_ref):
    probs, sink_terms = [], []
    for h in range(N_HEADS):
        rows = slice(h * WINDOW, (h + 1) * WINDOW)
        sh = _biased_scores(jnp.where(from_prev, s[rows, :WINDOW], s[rows, WINDOW:]), bias_ref[has_prev, rows, :])
        m = jnp.maximum(jnp.max(sh, axis=-1, keepdims=True), sink_ref[h])
        p = jnp.exp(sh - m)
        probs.append(jnp.concatenate([jnp.where(from_prev, p, 0.0), jnp.where(from_prev, 0.0, p)],
                                     axis=1).astype(BF16))
        sink_terms.append(jnp.exp(sink_ref[h] - m))
    return probs, sink_terms


def _block_output(probs, sink_terms, vv):
    assert 2 * HEAD_DIM == LANES and GROUP % 2 == 0
    ones = jnp.ones((vv.shape[0], HEAD_DIM), BF16)
    low_half = lax.broadcasted_iota(jnp.int32, (WINDOW, LANES), 1) < HEAD_DIM
    outs = []
    for g in range(N_KV_HEADS):
        vg = vv[:, g * HEAD_DIM:(g + 1) * HEAD_DIM]
        even_heads = [g * GROUP + r for r in range(0, GROUP, 2)]
        pe = jnp.concatenate([probs[h] for h in even_heads], axis=0)
        po = jnp.concatenate([probs[h + 1] for h in even_heads], axis=0)
        oe = jnp.dot(pe, jnp.concatenate([vg, ones], axis=1), preferred_element_type=F32)
        oo = jnp.dot(po, jnp.concatenate([ones, vg], axis=1), preferred_element_type=F32)
        for k, h in enumerate(even_heads):
            a = oe[k * WINDOW:(k + 1) * WINDOW]
            b = oo[k * WINDOW:(k + 1) * WINDOW]
            out_pair = jnp.where(low_half, a, b)
            sum_pair = pltpu.roll(jnp.where(low_half, b, a), HEAD_DIM, 1)
            sink_pair = jnp.where(low_half, sink_terms[h], sink_terms[h + 1])
            outs.append(out_pair * (1.0 / (sum_pair + sink_pair)))
    return jnp.concatenate(outs, axis=1)


def _in_proj_sample_attn_prompt(x, g, w, side0, side1, zp, rel_bias, sinks, tn, batch, seq):
    tm = x.shape[0]
    n = w.shape[1]
    gate_start = n - 2 * D_MODEL
    assert gate_start % tn == 0 and n % tn == 0
    gate_block, nsteps = gate_start // tn, n // tn
    bps = PROMPT_BLOCKS_PER_STEP
    assert seq % (bps * WINDOW) == 0
    steps_per_seq = seq // (bps * WINDOW)
    attn_steps = batch * steps_per_seq
    assert attn_steps <= nsteps

    def astep(j):
        return jnp.minimum(j, attn_steps - 1)

    def cur(off, width):
        return lambda j: (astep(j), off // width)

    def prev(off, width):
        return lambda j: (astep(j) * bps - jnp.minimum(astep(j) % steps_per_seq, 1), off // width)

    def side_spec(side):
        rows = _side_cast_rows(side.shape[0], nsteps)
        return pl.BlockSpec((rows, side.shape[1]), lambda j: (jnp.minimum(j, side.shape[0] // rows - 1), 0))

    smem = pl.BlockSpec(memory_space=pltpu.SMEM)
    return pl.pallas_call(
        functools.partial(_inproj_attn_kernel, steps_per_seq=steps_per_seq),
        grid=(nsteps,),
        in_specs=[
            pl.BlockSpec((tm, D_MODEL), lambda j: (0, 0)),
            pl.BlockSpec((1, D_MODEL),lambda j: (0, 0)),
            pl.BlockSpec((D_MODEL, tn), lambda j: (0, (j + gate_block) % nsteps)),
            side_spec(side0),
            side_spec(side1),
            smem,
            smem,
            pl.BlockSpec((bps * WINDOW, D_ATTN), cur(OFF_Q, D_ATTN)),
            pl.BlockSpec((WINDOW, D_KV), prev(OFF_K, D_KV)),
            pl.BlockSpec((bps * WINDOW, D_KV), cur(OFF_K, D_KV)),
            pl.BlockSpec((WINDOW, D_KV), prev(OFF_V, D_KV)),
            pl.BlockSpec((bps * WINDOW, D_KV), cur(OFF_V, D_KV)),
        ],
        out_specs=[
            pl.BlockSpec((tm, tn), lambda j: (0, j)),
            pl.BlockSpec((bps * WINDOW, D_ATTN), lambda j: (j, 0)),
            side_spec(side0),
            side_spec(side1),
        ],
        out_shape=[
            jax.ShapeDtypeStruct((tm, n), BF16),
            jax.ShapeDtypeStruct((nsteps * bps * WINDOW, D_ATTN), BF16),
            jax.ShapeDtypeStruct(side0.shape, BF16),
            jax.ShapeDtypeStruct(side1.shape, BF16),
        ],
        scratch_shapes=[pltpu.VMEM((tm, D_MODEL), BF16),
                        pltpu.VMEM((D_MODEL, tn), BF16),
                        pltpu.VMEM((2, N_HEADS * WINDOW, WINDOW), F32),
                        pltpu.VMEM((bps, N_HEADS * WINDOW, D_KV), BF16)],
        compiler_params=_params("arbitrary"),
        name="in_proj_sample_attn_prompt",
    )(x, g, w, side0, side1, rel_bias, sinks, zp, zp, zp, zp, zp)


SAMPLE_BATCH_TILE = 16


def _attn_sample_kernel(relb_ref, sink_ref, q_ref, kn_ref, vn_ref, ckt_ref, cvt_ref,
                        o_ref, nk_ref, nv_ref,
                        bias1_ref, bias2_ref, sinkv_ref, z_ref, s_ref, p_ref, oh_ref, *, steps):
    bt, wc, _ = nk_ref.shape
    tile_rows = bt * steps

    @pl.when(pl.program_id(0) == 0)
    def _():
        t = lax.broadcasted_iota(jnp.int32, (steps, wc), 0)
        j = lax.broadcasted_iota(jnp.int32, (steps, wc), 1)
        t2 = lax.broadcasted_iota(jnp.int32, (steps, tile_rows), 0)
        j2 = lax.broadcasted_iota(jnp.int32, (steps, tile_rows), 1)
        own = jnp.where(j2 < steps, t2 - j2, -1)
        for h in range(N_HEADS):
            sl = slice(h * steps, (h + 1) * steps)
            bias1_ref[sl, :] = _bias_from_distance(t + wc - j, h, relb_ref)
            bias2_ref[0, sl, :] = _bias_from_distance(own, h, relb_ref)
            sinkv_ref[sl, :] = jnp.full((steps, LANES), sink_ref[h], F32)
        for b in range(1, bt):
            bias2_ref[b] = pltpu.roll(bias2_ref[0], b * steps, 1)

    kn = kn_ref[...]
    vn = vn_ref[...]
    nk_ref[:, wc - steps:wc, :] = kn.astype(F32).reshape(bt, steps, D_KV)
    nv_ref[:, wc - steps:wc, :] = vn.astype(F32).reshape(bt, steps, D_KV)

    q = q_ref[...].astype(F32) * (HEAD_DIM ** -0.5)
    for h in range(N_HEADS):
        z_ref[:, h * steps:(h + 1) * steps, :] = _head_in_group_columns(q, h).reshape(bt, steps, D_KV)

    for b in range(bt):
        ck = ckt_ref[b].T
        nk_ref[b, 0:wc - steps, :] = ck[steps:, :]
        keys = jnp.concatenate([ck.astype(BF16), kn], axis=0)
        s_ref[b] = lax.dot_general(z_ref[b].astype(BF16), keys, (((1,), (1,)), ((), ())),
                                   preferred_element_type=F32)

    bias = jnp.concatenate([jnp.broadcast_to(bias1_ref[...], (bt,) + bias1_ref.shape), bias2_ref[...]], axis=2)
    s = _biased_scores(s_ref[...], bias)
    sink = sinkv_ref[...][None]
    m = jnp.maximum(jnp.max(s, axis=-1, keepdims=True), sink)
    p_ref[...] = jnp.exp(s - jnp.concatenate([m] * (s.shape[-1] // LANES), axis=-1)).astype(BF16)
    sink_term = jnp.exp(sink - m)

    vn_ones = jnp.concatenate([vn, jnp.ones((tile_rows, LANES), BF16)], axis=1)
    for b in range(bt):
        cv = cvt_ref[b].T
        nv_ref[b, 0:wc - steps, :] = cv[steps:, :]
        cv_ones = jnp.concatenate([cv.astype(BF16), jnp.ones((wc, LANES), BF16)], axis=1)
        oh_ref[b] = jnp.dot(p_ref[b], jnp.concatenate([cv_ones, vn_ones], axis=0), preferred_element_type=F32)

    inv = 1.0 / (oh_ref[:, :, D_KV:] + sink_term)
    outs = []
    for h in range(N_HEADS):
        rows = slice(h * steps, (h + 1) * steps)
        outs.append((oh_ref[:, rows, _group_columns(h)] * inv[:, rows, :HEAD_DIM]).reshape(tile_rows, HEAD_DIM))
    o_ref[...] = jnp.concatenate(outs, axis=1).astype(o_ref.dtype)


def _attn_sample(z, cache_kt, cache_vt, rel_bias, sinks, batch, steps):
    wc = cache_kt.shape[2]
    bt = SAMPLE_BATCH_TILE
    rows = bt * steps
    smem = pl.BlockSpec(memory_space=pltpu.SMEM)
    cache_t_spec = pl.BlockSpec((bt, D_KV, wc), lambda i: (i, 0, 0))
    cache_spec = pl.BlockSpec((bt, wc, D_KV), lambda i: (i, 0, 0))
    return pl.pallas_call(
        functools.partial(_attn_sample_kernel, steps=steps),
        grid=(batch // bt,),
        in_specs=[
            smem,
            smem,
            pl.BlockSpec((rows, D_ATTN), lambda i: (i, OFF_Q // D_ATTN)),
            pl.BlockSpec((rows, D_KV), lambda i: (i, OFF_K // D_KV)),
            pl.BlockSpec((rows, D_KV), lambda i: (i, OFF_V // D_KV)),
            cache_t_spec,
            cache_t_spec,
        ],
        out_specs=[
            pl.BlockSpec((rows, D_ATTN), lambda i: (i, 0)),
            cache_spec,
            cache_spec,
        ],
        out_shape=[
            jax.ShapeDtypeStruct((batch * steps, D_ATTN), BF16),
            jax.ShapeDtypeStruct((batch, wc, D_KV), F32),
            jax.ShapeDtypeStruct((batch, wc, D_KV), F32),
        ],
        scratch_shapes=[
            pltpu.VMEM((N_HEADS * steps, wc), F32),
            pltpu.VMEM((bt, N_HEADS * steps, rows), F32),
            pltpu.VMEM((N_HEADS * steps, LANES), F32),
            pltpu.VMEM((bt, N_HEADS * steps, D_KV), F32),
            pltpu.VMEM((bt, N_HEADS * steps, wc + rows), F32),
            pltpu.VMEM((bt, N_HEADS * steps, wc + rows), BF16),
            pltpu.VMEM((bt, N_HEADS * steps, D_KV + LANES), F32),
        ],
        compiler_params=_params("arbitrary"),
        name="attn_sample",
    )(rel_bias, sinks, z, z, z, cache_kt, cache_vt)


CARRY_ROWS = 8


def _gated_conv(b, wc_ref, u, u1, u2):
    conv = wc_ref[0:1, :] * u2 + wc_ref[1:2, :] * u1 + wc_ref[2:3, :] * u
    return (b.astype(F32) * conv).astype(BF16)


def _mix_tail(conv_branch_input, at_ref, gc_ref, ga_ref, x_ref, g_ref, wco_ref, wao_ref, wo_ref, h_ref, hn_ref):
    y_conv = jnp.dot(conv_branch_input(), wco_ref[...], preferred_element_type=F32)
    y_attn = jnp.dot(at_ref[...], wao_ref[...], preferred_element_type=F32)
    merged = (jax.nn.sigmoid(gc_ref[...].astype(F32)) * y_conv
              + jax.nn.sigmoid(ga_ref[...].astype(F32)) * y_attn)
    h = x_ref[...] + jnp.dot(merged.astype(BF16), wo_ref[...], preferred_element_type=F32)
    h_ref[...] = h
    hn_ref[...] = _rms_rows(h, g_ref[0:1, :]).astype(hn_ref.dtype)


def _delayed_from_scratch(us_ref, rows):
    def at(delay):
        return us_ref[CARRY_ROWS - delay + rows.start:CARRY_ROWS - delay + rows.stop, :]
    return at(0), at(1), at(2)


def _mix_prompt_kernel(b_ref, c_ref, hc_ref, wc_ref, at_ref, gc_ref, ga_ref, x_ref, g_ref,
                       wco_ref, wao_ref, wo_ref, side0_ref, side1_ref, side2_ref,
                       h_ref, hn_ref, st_ref, side0_bf16_ref, side1_bf16_ref, side2_bf16_ref,
                       us_ref, *, tiles_per_seq):
    tm = x_ref.shape[0]
    side0_bf16_ref[...] = side0_ref[...].astype(BF16)
    side1_bf16_ref[...] = side1_ref[...].astype(BF16)
    side2_bf16_ref[...] = side2_ref[...].astype(BF16)

    @pl.when(pl.program_id(0) % tiles_per_seq == 0)
    def _():
        us_ref[0:CARRY_ROWS, :] = jnp.zeros((CARRY_ROWS, D_CONV), F32)

    def conv_branch_input():
        u = c_ref[...].astype(F32) * hc_ref[...].astype(F32)
        us_ref[CARRY_ROWS:, :] = u
        st_ref[...] = u[tm - (CONV_WIDTH - 1):, :]
        yc = _gated_conv(b_ref[...], wc_ref, *_delayed_from_scratch(us_ref, slice(0, tm)))
        us_ref[0:CARRY_ROWS, :] = us_ref[tm:tm + CARRY_ROWS, :]
        return yc

    _mix_tail(conv_branch_input, at_ref, gc_ref, ga_ref, x_ref, g_ref, wco_ref, wao_ref, wo_ref, h_ref, hn_ref)


def _mix_sample_kernel(b_ref, c_ref, hc_ref, st_ref, wc_ref, at_ref, gc_ref, ga_ref, x_ref, g_ref,
                       wco_ref, wao_ref, wo_ref, h_ref, hn_ref, new_st_ref, us_ref, *, steps):
    tm = x_ref.shape[0]
    nseq = tm // steps
    keep = CONV_WIDTH - 1

    def conv_branch_input():
        u = c_ref[...].astype(F32) * hc_ref[...].astype(F32)
        us_ref[:, CARRY_ROWS - keep:CARRY_ROWS, :] = st_ref[...]
        us_ref[:, CARRY_ROWS:, :] = u.reshape(nseq, steps, D_CONV)
        new_st_ref[...] = us_ref[:, CARRY_ROWS + steps - keep:, :]

        def delayed(d):
            return us_ref[:, CARRY_ROWS - d:CARRY_ROWS - d + steps, :].reshape(tm, D_CONV)

        return _gated_conv(b_ref[...], wc_ref, u, delayed(1), delayed(2))

    _mix_tail(conv_branch_input, at_ref, gc_ref, ga_ref, x_ref, g_ref, wco_ref, wao_ref, wo_ref, h_ref, hn_ref)


def _mix_specs(tm):
    resident = functools.partial(pl.BlockSpec, index_map=lambda i: (0, 0), pipeline_mode=pl.Buffered(1))
    row_spec = pl.BlockSpec((tm, D_MODEL), lambda i: (i, 0))

    def zcol(off, width):
        return pl.BlockSpec((tm, width), lambda i: (i, off // width))

    conv_in = [zcol(OFF_B, D_CONV), zcol(OFF_C, D_CONV), zcol(OFF_H, D_CONV)]
    rest_in = [
        pl.BlockSpec((CONV_WIDTH, D_CONV), lambda i: (0, 0)),
        pl.BlockSpec((tm, D_ATTN), lambda i: (i, 0)),
        zcol(OFF_GC, D_MODEL),
        zcol(OFF_GA, D_MODEL),
        row_spec,
        pl.BlockSpec((1, D_MODEL),lambda i: (0, 0)),
        resident((D_CONV, D_MODEL)),
        resident((D_ATTN, D_MODEL)),
        resident((D_MODEL, D_MODEL)),
    ]
    return conv_in, rest_in, row_spec


def _mix_prompt(z, attn, x, w_conv, g_ffn, wco, wao, wo, sides, tm, batch, seq):
    m = x.shape[0]
    assert seq % tm == 0 and tm >= CARRY_ROWS
    tiles_per_seq = seq // tm
    nsteps = m // tm
    conv_in, rest_in, row_spec = _mix_specs(tm)

    def side_spec(side):
        rows = _side_cast_rows(side.shape[0], nsteps)
        return pl.BlockSpec((rows, side.shape[1]), lambda i: (jnp.minimum(i, side.shape[0] // rows - 1), 0))

    side_specs = [side_spec(s) for s in sides]
    return pl.pallas_call(
        functools.partial(_mix_prompt_kernel, tiles_per_seq=tiles_per_seq),
        grid=(nsteps,),
        in_specs=conv_in + rest_in + side_specs,
        out_specs=[row_spec, row_spec,
                   pl.BlockSpec((None, CONV_WIDTH - 1, D_CONV), lambda i: (i // tiles_per_seq, 0, 0))] + side_specs,
        out_shape=[jax.ShapeDtypeStruct((m, D_MODEL), F32), jax.ShapeDtypeStruct((m, D_MODEL), BF16),
                   jax.ShapeDtypeStruct((batch, CONV_WIDTH - 1, D_CONV), F32)]
        + [jax.ShapeDtypeStruct(s.shape, BF16) for s in sides],
        scratch_shapes=[pltpu.VMEM((tm + CARRY_ROWS, D_CONV), F32)],
        compiler_params=_params("arbitrary"),
        name="mix_prompt",
    )(z, z, z, w_conv, attn, z, z, x, g_ffn, wco, wao, wo, *sides)


def _mix_sample(z, state, attn, x, w_conv, g_ffn, wco, wao, wo, tm, steps):
    m = x.shape[0]
    assert tm % steps == 0 and CONV_WIDTH - 1 <= min(steps, CARRY_ROWS)
    conv_in, rest_in, row_spec = _mix_specs(tm)
    state_spec = pl.BlockSpec((tm // steps, CONV_WIDTH - 1, D_CONV), lambda i: (i, 0, 0))
    return pl.pallas_call(
        functools.partial(_mix_sample_kernel, steps=steps),
        grid=(m // tm,),
        in_specs=conv_in + [state_spec] + rest_in,
        out_specs=[row_spec, row_spec, state_spec],
        out_shape=[jax.ShapeDtypeStruct((m, D_MODEL), F32), jax.ShapeDtypeStruct((m, D_MODEL), BF16),
                   jax.ShapeDtypeStruct(state.shape, F32)],
        scratch_shapes=[pltpu.VMEM((tm // steps, CARRY_ROWS + steps, D_CONV), F32)],
        compiler_params=_params("arbitrary"),
        name="mix_sample",
    )(z, z, z, state, w_conv, attn, z, z, x, g_ffn, wco, wao, wo)


def _ffn_kernel(hn_ref, h_ref, gf_ref, wg_hbm_ref, wu_hbm_ref, wd_hbm_ref, y_ref,
                wg_buf, wu_buf, wd_buf, sem, *, final_norm, tf):
    i = pl.program_id(0)
    ntiles = pl.num_programs(0)
    nf = D_FF // tf

    def chunk_copies(f, slot):
        cols = pl.ds(pl.multiple_of(f * tf, tf), tf)
        return (pltpu.make_async_copy(wg_hbm_ref.at[:, cols], wg_buf.at[slot], sem.at[0, slot]),
                pltpu.make_async_copy(wu_hbm_ref.at[:, cols], wu_buf.at[slot], sem.at[1, slot]),
                pltpu.make_async_copy(wd_hbm_ref.at[cols, :], wd_buf.at[slot], sem.at[2, slot]))

    @pl.when(i == 0)
    def _():
        for copy in chunk_copies(0, 0):
            copy.start()

    y_ref[...] = h_ref[...]

    def chunk(f, carry):
        slot = (i * nf + f) % 2
        nxt = f + 1

        @pl.when((nxt < nf) | (i + 1 < ntiles))
        def _():
            for copy in chunk_copies(nxt % nf, 1 - slot):
                copy.start()

        for copy in chunk_copies(f, slot):
            copy.wait()
        hn = hn_ref[...]
        a = (jax.nn.silu(jnp.dot(hn, wg_buf[slot], preferred_element_type=F32))
             * jnp.dot(hn, wu_buf[slot], preferred_element_type=F32))
        y_ref[...] += jnp.dot(a.astype(BF16), wd_buf[slot], preferred_element_type=F32)
        return carry

    lax.fori_loop(0, nf, chunk, 0)

    if final_norm:
        def body(r, carry):
            rows = pl.ds(pl.multiple_of(r * NORM_ROWS, NORM_ROWS), NORM_ROWS)
            y_ref[rows, :] = _rms_rows(y_ref[rows, :], gf_ref[0:1, :])
            return carry

        lax.fori_loop(0, y_ref.shape[0] // NORM_ROWS, body, 0)


def _ffn(hn, h, g_final, wg, wu, wd, tm, tf, final_norm):
    m = h.shape[0]
    assert D_FF % tf == 0 and wg.dtype == BF16
    hbm = pl.BlockSpec(memory_space=pl.ANY)
    return pl.pallas_call(
        functools.partial(_ffn_kernel, final_norm=final_norm, tf=tf),
        grid=(m // tm,),
        in_specs=[
            pl.BlockSpec((tm, D_MODEL), lambda i: (i, 0)),
            pl.BlockSpec((tm, D_MODEL), lambda i: (i, 0)),
            pl.BlockSpec((1, D_MODEL), lambda i: (0, 0)),
            hbm,
            hbm,
            hbm,
        ],
        out_specs=pl.BlockSpec((tm, D_MODEL), lambda i: (i, 0)),
        out_shape=jax.ShapeDtypeStruct((m, D_MODEL), F32),
        scratch_shapes=[pltpu.VMEM((2, D_MODEL, tf), BF16), pltpu.VMEM((2, D_MODEL, tf), BF16),
                        pltpu.VMEM((2, tf, D_MODEL), BF16), pltpu.SemaphoreType.DMA((3, 2))],
        compiler_params=_params("arbitrary"),
        name="ffn",
    )(hn, h, g_final, wg, wu, wd)


IN_PROJ_TM_PROMPT = 2048
IN_PROJ_TN = 512
MIX_TM = 256
FFN_TM = 1024
FFN_TF = 512


def kernel(x_prompt, x_sample, cache_k, cache_v, state_conv, rel_bias, w_in, w_conv, w_conv_out, sinks,
           w_attn_out, w_o, g_mix, g_ffn, w_gate, w_up, w_down, g_final):
    depth = w_in.shape[0]
    batch, seq, _ = x_prompt.shape
    dec_batch, steps, _ = x_sample.shape
    wc = cache_k.shape[2]
    assert seq % WINDOW == 0 and wc == WINDOW and dec_batch % SAMPLE_BATCH_TILE == 0

    hp = x_prompt.reshape(batch * seq, D_MODEL)
    hs = x_sample.reshape(dec_batch * steps, D_MODEL)
    g_final2 = g_final.reshape(1, D_MODEL)
    kp_l, vp_l, cp_l, ks_l, vs_l, cs_l = [], [], [], [], [], []
    for l in range(depth):
        final = l == depth - 1
        lw = {
            "w_in": w_in[l],
            "g_ffn": g_ffn[l].reshape(1, D_MODEL),
        }
        g_mix_l = g_mix[l].reshape(1, D_MODEL)

        zp, lw["wo"] = _in_proj(hp, g_mix_l, lw["w_in"], w_o[l], IN_PROJ_TM_PROMPT, IN_PROJ_TN)
        zs, attn_p, lw["wco"], lw["wao"] = _in_proj_sample_attn_prompt(
            hs, g_mix_l, lw["w_in"], w_conv_out[l], w_attn_out[l], zp, rel_bias, sinks[l], IN_PROJ_TN, batch, seq)
        hp, hnp, conv_p, lw["wg"], lw["wu"], lw["wd"] = _mix_prompt(
            zp, attn_p, hp, w_conv[l], lw["g_ffn"], lw["wco"], lw["wao"], lw["wo"],
            (w_gate[l], w_up[l], w_down[l]), MIX_TM, batch, seq)
        hp = _ffn(hnp, hp, g_final2, lw["wg"], lw["wu"], lw["wd"], FFN_TM, FFN_TF, final)
        kv_p = zp.reshape(batch, seq, D_IN_PROJ)[:, seq - WINDOW:, OFF_K:OFF_V + D_KV].astype(F32)
        kp_l.append(kv_p[..., :D_KV].reshape(batch, WINDOW, N_KV_HEADS, HEAD_DIM))
        vp_l.append(kv_p[..., D_KV:].reshape(batch, WINDOW, N_KV_HEADS, HEAD_DIM))
        cp_l.append(conv_p)

        cache_kt = jnp.transpose(cache_k[l], (0, 2, 3, 1)).reshape(dec_batch, D_KV, wc)
        cache_vt = jnp.transpose(cache_v[l], (0, 2, 3, 1)).reshape(dec_batch, D_KV, wc)
        attn_s, nk, nv = _attn_sample(zs, cache_kt, cache_vt, rel_bias, sinks[l], dec_batch, steps)
        hs, hns, conv_s = _mix_sample(zs, state_conv[l], attn_s, hs, w_conv[l], lw["g_ffn"], lw["wco"], lw["wao"],
                                      lw["wo"], MIX_TM, steps)
        hs = _ffn(hns, hs, g_final2, lw["wg"], lw["wu"], lw["wd"], FFN_TM, FFN_TF, final)
        ks_l.append(nk.reshape(dec_batch, wc, N_KV_HEADS, HEAD_DIM))
        vs_l.append(nv.reshape(dec_batch, wc, N_KV_HEADS, HEAD_DIM))
        cs_l.append(conv_s)

    return (hp.reshape(batch, seq, D_MODEL), hs.reshape(dec_batch, steps, D_MODEL),
            jnp.stack(kp_l), jnp.stack(vp_l), jnp.stack(cp_l),
            jnp.stack(ks_l), jnp.stack(vs_l), jnp.stack(cs_l))
```

```python
import functools
import math

import numpy as np
import jax
import jax.numpy as jnp
from jax import lax
from jax.experimental import pallas as pl
from jax.experimental.pallas import tpu as pltpu

D_MODEL = 2048
D_CONV = D_MODEL // 2
CONV_WIDTH = 3
HEAD_DIM = 64
N_HEADS = (D_MODEL // 2) // HEAD_DIM
N_KV_HEADS = N_HEADS // 4
GROUP = N_HEADS // N_KV_HEADS
D_ATTN = N_HEADS * HEAD_DIM
D_KV = N_KV_HEADS * HEAD_DIM
WINDOW = 128
NUM_BUCKETS = 32
MAX_DISTANCE = 128
D_FF = -(-8 * D_MODEL // (3 * 256)) * 256
EPS = 1e-6
D_IN_PROJ = 3 * D_CONV + D_ATTN + 2 * D_KV + 2 * D_MODEL

OFF_GC = 0
OFF_GA = D_MODEL
OFF_B = 2 * D_MODEL
OFF_C = OFF_B + D_CONV
OFF_H = OFF_C + D_CONV
OFF_Q = OFF_H + D_CONV
OFF_K = OFF_Q + D_ATTN
OFF_V = OFF_K + D_KV

LANES = 128
NEG = -1e30
F32 = jnp.float32
BF16 = jnp.bfloat16
VMEM_LIMIT = 60 * 1024 * 1024


def _bucket_thresholds():
    max_exact = NUM_BUCKETS // 2
    d = np.arange(MAX_DISTANCE)
    ratio = np.log(np.maximum(d, 1).astype(np.float32) / np.float32(max_exact)) / np.float32(
        math.log(MAX_DISTANCE / max_exact))
    large = max_exact + (ratio * np.float32(NUM_BUCKETS - max_exact)).astype(np.int32)
    large = np.minimum(large, NUM_BUCKETS - 1)
    return [int(np.min(d[(d >= max_exact) & (large >= b)])) for b in range(max_exact + 1, NUM_BUCKETS)]


BUCKET_THRESHOLDS = _bucket_thresholds()


def _params(*semantics):
    return pltpu.CompilerParams(dimension_semantics=semantics, vmem_limit_bytes=VMEM_LIMIT)


def _rms_rows(x, g):
    ms = jnp.mean(x * x, axis=-1, keepdims=True)
    return x * lax.rsqrt(ms + EPS) * g


NORM_ROWS = 256


def _inproj_kernel(x_ref, g_ref, w_ref, side_ref, z_ref, side_bf16_ref, xn_ref):
    _inproj_norm(pl.program_id(1) == 0, x_ref, g_ref, xn_ref)
    _inproj_columns(w_ref, z_ref, xn_ref)
    side_bf16_ref[...] = side_ref[...].astype(BF16)


BF16_ROWS = 16


def _side_cast_rows(rows, nsteps):
    per_step = -(-rows // nsteps)
    per_step = -(-per_step // BF16_ROWS) * BF16_ROWS
    assert rows % per_step == 0
    return per_step


def _inproj_norm(first_column_step, x_ref, g_ref, xn_ref):
    @pl.when(first_column_step)
    def _():
        def body(r, carry):
            rows = pl.ds(pl.multiple_of(r * NORM_ROWS, NORM_ROWS), NORM_ROWS)
            xn_ref[rows, :] = _rms_rows(x_ref[rows, :], g_ref[0:1, :]).astype(xn_ref.dtype)
            return carry

        lax.fori_loop(0, x_ref.shape[0] // NORM_ROWS, body, 0)


def _inproj_columns(w_ref, z_ref, xn_ref):
    z_ref[...] = jnp.dot(xn_ref[...], w_ref[...].astype(BF16), preferred_element_type=F32).astype(z_ref.dtype)


def _in_proj(x, g, w, side, tm, tn):
    m = x.shape[0]
    n = w.shape[1]
    gate_start = n - 2 * D_MODEL
    assert gate_start % tn == 0 and n % tn == 0
    gate_block, nblocks = gate_start // tn, n // tn
    side_rows = _side_cast_rows(side.shape[0], (m // tm) * nblocks)
    side_spec = pl.BlockSpec((side_rows, side.shape[1]),
                             lambda i, j: (jnp.minimum(i * nblocks + j, side.shape[0] // side_rows - 1), 0))
    return pl.pallas_call(
        _inproj_kernel,
        grid=(m // tm, nblocks),
        in_specs=[
            pl.BlockSpec((tm, D_MODEL), lambda i, j: (i, 0)),
            pl.BlockSpec((1, D_MODEL),lambda i, j: (0, 0)),
            pl.BlockSpec((D_MODEL, tn), lambda i, j: (0, (j + gate_block) % nblocks)),
            side_spec,
        ],
        out_specs=[pl.BlockSpec((tm, tn), lambda i, j: (i, j)), side_spec],
        out_shape=[jax.ShapeDtypeStruct((m, n), BF16), jax.ShapeDtypeStruct(side.shape, BF16)],
        scratch_shapes=[pltpu.VMEM((tm, D_MODEL), BF16)],
        compiler_params=_params("arbitrary", "arbitrary"),
        name="in_proj",
    )(x, g, w, side)


def _bias_from_distance(dist, head, relb_ref):
    valid = (dist >= 0) & (dist < WINDOW)
    d = jnp.clip(dist, 0, MAX_DISTANCE - 1)
    bucket = jnp.minimum(d, NUM_BUCKETS // 2)
    for thr in BUCKET_THRESHOLDS:
        bucket = bucket + (d >= thr).astype(jnp.int32)
    bias = jnp.zeros(dist.shape, F32)
    for b in range(NUM_BUCKETS):
        bias = jnp.where(bucket == b, relb_ref[b, head], bias)
    return jnp.where(valid, bias, NEG)


def _biased_scores(s, bias):
    return jnp.where(bias > 0.5 * NEG, s + bias, NEG)


def _head_in_group_columns(q, h):
    half = D_KV // 2
    assert 2 * HEAD_DIM == half and GROUP * HEAD_DIM == D_KV
    g = h // GROUP
    src = q[:, (h // 2) * half:(h // 2 + 1) * half]
    if h % 2 != g % 2:
        src = pltpu.roll(src, HEAD_DIM, 1)
    lane_half = lax.broadcasted_iota(jnp.int32, src.shape, 1) // HEAD_DIM
    placed = jnp.where(lane_half == g % 2, src, 0.0)
    zero = jnp.zeros_like(placed)
    return jnp.concatenate([placed, zero] if g // 2 == 0 else [zero, placed], axis=1)


def _group_columns(h):
    g = h // GROUP
    return slice(g * HEAD_DIM, (g + 1) * HEAD_DIM)


INPROJ_FILL_CHUNKS = 4
PROMPT_BLOCKS_PER_STEP = 4


def _inproj_attn_kernel(x_ref, g_ref, w_ref, side0_ref, side1_ref, relb_ref, sink_ref, q_ref, kp_ref, kc_ref,
                        vp_ref, vc_ref, z_ref, o_ref, side0_bf16_ref, side1_bf16_ref,
                        xn_ref, wb_ref, bias_ref, lhs_ref, *, steps_per_seq):
    j = pl.program_id(0)
    _attn_prompt_bias(j == 0, relb_ref, bias_ref)
    _inproj_norm(j == 0, x_ref, g_ref, xn_ref)
    wb_ref[...] = w_ref[...].astype(BF16)
    side0_bf16_ref[...] = side0_ref[...].astype(BF16)
    side1_bf16_ref[...] = side1_ref[...].astype(BF16)
    chunk = xn_ref.shape[0] // INPROJ_FILL_CHUNKS

    def column_chunk(c):
        rows = slice(c * chunk, (c + 1) * chunk)

        def emit():
            z_ref[rows, :] = jnp.dot(xn_ref[rows, :], wb_ref[...], preferred_element_type=F32).astype(z_ref.dtype)
        return emit

    _attn_prompt_blocks(j % steps_per_seq != 0, sink_ref, q_ref, kp_ref, kc_ref, vp_ref, vc_ref,
                        o_ref, bias_ref, lhs_ref, fillers=[column_chunk(c) for c in range(INPROJ_FILL_CHUNKS)])


def _from_prev_mask():
    qi = lax.broadcasted_iota(jnp.int32, (WINDOW, WINDOW), 0)
    kj = lax.broadcasted_iota(jnp.int32, (WINDOW, WINDOW), 1)
    return qi, kj, kj > qi


def _attn_prompt_bias(first_step, relb_ref, bias_ref):
    @pl.when(first_step)
    def _():
        qi, kj, from_prev = _from_prev_mask()
        dist = jnp.where(from_prev, qi + WINDOW - kj, qi - kj)
        for h in range(N_HEADS):
            rows = slice(h * WINDOW, (h + 1) * WINDOW)
            bias = _bias_from_distance(dist, h, relb_ref)
            bias_ref[1, rows, :] = bias
            bias_ref[0, rows, :] = jnp.where(from_prev, NEG, bias)


def _attn_prompt_blocks(has_prev_block, sink_ref, q_ref, kp_ref, kc_ref, vp_ref, vc_ref, o_ref, bias_ref, lhs_ref,
                        fillers=()):
    _, _, from_prev = _from_prev_mask()
    k_all = jnp.concatenate([kp_ref[...], kc_ref[...]], axis=0)
    v_all = jnp.concatenate([vp_ref[...], vc_ref[...]], axis=0)

    def keys_of(blk):
        return slice(blk * WINDOW, (blk + 2) * WINDOW)

    nblk = PROMPT_BLOCKS_PER_STEP
    fillers = list(fillers)
    per_stage = -(-len(fillers) // (nblk + 2))
    scores, probs = {}, {}
    for stage in range(nblk + 2):
        if stage < nblk:
            q = q_ref[stage * WINDOW:(stage + 1) * WINDOW, :]
            scores[stage] = _block_scores(q, k_all[keys_of(stage)], lhs_ref.at[stage])
        for _ in range(min(per_stage, len(fillers))):
            fillers.pop(0)()
        if 0 <= stage - 1 < nblk:
            blk = stage - 1
            has_prev = has_prev_block.astype(jnp.int32) if blk == 0 else 1
            probs[blk] = _block_softmax(scores.pop(blk), has_prev, from_prev, sink_ref, bias_ref)
        if 0 <= stage - 2 < nblk:
            blk = stage - 2
            o_ref[blk * WINDOW:(blk + 1) * WINDOW, :] = _block_output(
                *probs.pop(blk), v_all[keys_of(blk)]).astype(o_ref.dtype)


def _block_scores(q, kk, lhs_ref):
    q = q.astype(F32) * (HEAD_DIM ** -0.5)
    for h in range(N_HEADS):
        lhs_ref[h * WINDOW:(h + 1) * WINDOW, :] = _head_in_group_columns(q, h).astype(BF16)
    return lax.dot_general(lhs_ref[...], kk, (((1,), (1,)), ((), ())), preferred_element_type=F32)


def _block_softmax(s, has_prev, from_prev, sink_ref, bias_ref):
    probs, sink_terms = [], []
    for h in range(N_HEADS):
        rows = slice(h * WINDOW, (h + 1) * WINDOW)
        sh = _biased_scores(jnp.where(from_prev, s[rows, :WINDOW], s[rows, WINDOW:]), bias_ref[has_prev, rows, :])
        m = jnp.maximum(jnp.max(sh, axis=-1, keepdims=True), sink_ref[h])
        p = jnp.exp(sh - m)
        probs.append(jnp.concatenate([jnp.where(from_prev, p, 0.0), jnp.where(from_prev, 0.0, p)],
                                     axis=1).astype(BF16))
        sink_terms.append(jnp.exp(sink_ref[h] - m))
    return probs, sink_terms


def _block_output(probs, sink_terms, vv):
    assert 2 * HEAD_DIM == LANES and GROUP % 2 == 0
    ones = jnp.ones((vv.shape[0], HEAD_DIM), BF16)
    low_half = lax.broadcasted_iota(jnp.int32, (WINDOW, LANES), 1) < HEAD_DIM
    outs = []
    for g in range(N_KV_HEADS):
        vg = vv[:, g * HEAD_DIM:(g + 1) * HEAD_DIM]
        even_heads = [g * GROUP + r for r in range(0, GROUP, 2)]
        pe = jnp.concatenate([probs[h] for h in even_heads], axis=0)
        po = jnp.concatenate([probs[h + 1] for h in even_heads], axis=0)
        oe = jnp.dot(pe, jnp.concatenate([vg, ones], axis=1), preferred_element_type=F32)
        oo = jnp.dot(po, jnp.concatenate([ones, vg], axis=1), preferred_element_type=F32)
        for k, h in enumerate(even_heads):
            a = oe[k * WINDOW:(k + 1) * WINDOW]
            b = oo[k * WINDOW:(k + 1) * WINDOW]
            out_pair = jnp.where(low_half, a, b)
            sum_pair = pltpu.roll(jnp.where(low_half, b, a), HEAD_DIM, 1)
            sink_pair = jnp.where(low_half, sink_terms[h], sink_terms[h + 1])
            outs.append(out_pair * (1.0 / (sum_pair + sink_pair)))
    return jnp.concatenate(outs, axis=1)


def _in_proj_sample_attn_prompt(x, g, w, side0, side1, zp, rel_bias, sinks, tn, batch, seq):
    tm = x.shape[0]
    n = w.shape[1]
    gate_start = n - 2 * D_MODEL
    assert gate_start % tn == 0 and n % tn == 0
    gate_block, nsteps = gate_start // tn, n // tn
    bps = PROMPT_BLOCKS_PER_STEP
    assert seq % (bps * WINDOW) == 0
    steps_per_seq = seq // (bps * WINDOW)
    attn_steps = batch * steps_per_seq
    assert attn_steps <= nsteps

    def astep(j):
        return jnp.minimum(j, attn_steps - 1)

    def cur(off, width):
        return lambda j: (astep(j), off // width)

    def prev(off, width):
        return lambda j: (astep(j) * bps - jnp.minimum(astep(j) % steps_per_seq, 1), off // width)

    def side_spec(side):
        rows = _side_cast_rows(side.shape[0], nsteps)
        return pl.BlockSpec((rows, side.shape[1]), lambda j: (jnp.minimum(j, side.shape[0] // rows - 1), 0))

    smem = pl.BlockSpec(memory_space=pltpu.SMEM)
    return pl.pallas_call(
        functools.partial(_inproj_attn_kernel, steps_per_seq=steps_per_seq),
        grid=(nsteps,),
        in_specs=[
            pl.BlockSpec((tm, D_MODEL), lambda j: (0, 0)),
            pl.BlockSpec((1, D_MODEL),lambda j: (0, 0)),
            pl.BlockSpec((D_MODEL, tn), lambda j: (0, (j + gate_block) % nsteps)),
            side_spec(side0),
            side_spec(side1),
            smem,
            smem,
            pl.BlockSpec((bps * WINDOW, D_ATTN), cur(OFF_Q, D_ATTN)),
            pl.BlockSpec((WINDOW, D_KV), prev(OFF_K, D_KV)),
            pl.BlockSpec((bps * WINDOW, D_KV), cur(OFF_K, D_KV)),
            pl.BlockSpec((WINDOW, D_KV), prev(OFF_V, D_KV)),
            pl.BlockSpec((bps * WINDOW, D_KV), cur(OFF_V, D_KV)),
        ],
        out_specs=[
            pl.BlockSpec((tm, tn), lambda j: (0, j)),
            pl.BlockSpec((bps * WINDOW, D_ATTN), lambda j: (j, 0)),
            side_spec(side0),
            side_spec(side1),
        ],
        out_shape=[
            jax.ShapeDtypeStruct((tm, n), BF16),
            jax.ShapeDtypeStruct((nsteps * bps * WINDOW, D_ATTN), BF16),
            jax.ShapeDtypeStruct(side0.shape, BF16),
            jax.ShapeDtypeStruct(side1.shape, BF16),
        ],
        scratch_shapes=[pltpu.VMEM((tm, D_MODEL), BF16),
                        pltpu.VMEM((D_MODEL, tn), BF16),
                        pltpu.VMEM((2, N_HEADS * WINDOW, WINDOW), F32),
                        pltpu.VMEM((bps, N_HEADS * WINDOW, D_KV), BF16)],
        compiler_params=_params("arbitrary"),
        name="in_proj_sample_attn_prompt",
    )(x, g, w, side0, side1, rel_bias, sinks, zp, zp, zp, zp, zp)


SAMPLE_BATCH_TILE = 16


def _attn_sample_kernel(relb_ref, sink_ref, q_ref, kn_ref, vn_ref, ckt_ref, cvt_ref,
                        o_ref, nk_ref, nv_ref,
                        bias1_ref, bias2_ref, sinkv_ref, z_ref, s_ref, p_ref, oh_ref, *, steps):
    bt, _, wc = nk_ref.shape
    tile_rows = bt * steps

    @pl.when(pl.program_id(0) == 0)
    def _():
        t = lax.broadcasted_iota(jnp.int32, (steps, wc), 0)
        j = lax.broadcasted_iota(jnp.int32, (steps, wc), 1)
        t2 = lax.broadcasted_iota(jnp.int32, (steps, tile_rows), 0)
        j2 = lax.broadcasted_iota(jnp.int32, (steps, tile_rows), 1)
        own = jnp.where(j2 < steps, t2 - j2, -1)
        for h in range(N_HEADS):
            sl = slice(h * steps, (h + 1) * steps)
            bias1_ref[sl, :] = _bias_from_distance(t + wc - j, h, relb_ref)
            bias2_ref[0, sl, :] = _bias_from_distance(own, h, relb_ref)
            sinkv_ref[sl, :] = jnp.full((steps, LANES), sink_ref[h], F32)
        for b in range(1, bt):
            bias2_ref[b] = pltpu.roll(bias2_ref[0], b * steps, 1)

    kn = kn_ref[...]
    vn = vn_ref[...]
    knt = kn.astype(F32).T
    vnt = vn.astype(F32).T
    new_lanes = lax.broadcasted_iota(jnp.int32, (D_KV, wc), 1) >= wc - steps

    def slid(cache_t, new_t, b):
        new = pltpu.roll(new_t, (wc - (b + 1) * steps) % tile_rows, 1)
        return jnp.where(new_lanes, new, pltpu.roll(cache_t, wc - steps, 1))

    q = q_ref[...].astype(F32) * (HEAD_DIM ** -0.5)
    for h in range(N_HEADS):
        z_ref[:, h * steps:(h + 1) * steps, :] = _head_in_group_columns(q, h).reshape(bt, steps, D_KV)

    for b in range(bt):
        ckt = ckt_ref[b]
        nk_ref[b] = slid(ckt, knt, b)
        ck = ckt.T
        keys = jnp.concatenate([ck.astype(BF16), kn], axis=0)
        s_ref[b] = lax.dot_general(z_ref[b].astype(BF16), keys, (((1,), (1,)), ((), ())),
                                   preferred_element_type=F32)

    bias = jnp.concatenate([jnp.broadcast_to(bias1_ref[...], (bt,) + bias1_ref.shape), bias2_ref[...]], axis=2)
    s = _biased_scores(s_ref[...], bias)
    sink = sinkv_ref[...][None]
    m = jnp.maximum(jnp.max(s, axis=-1, keepdims=True), sink)
    p_ref[...] = jnp.exp(s - jnp.concatenate([m] * (s.shape[-1] // LANES), axis=-1)).astype(BF16)
    sink_term = jnp.exp(sink - m)

    vn_ones = jnp.concatenate([vn, jnp.ones((tile_rows, LANES), BF16)], axis=1)
    for b in range(bt):
        cvt = cvt_ref[b]
        nv_ref[b] = slid(cvt, vnt, b)
        cv = cvt.T
        cv_ones = jnp.concatenate([cv.astype(BF16), jnp.ones((wc, LANES), BF16)], axis=1)
        oh_ref[b] = jnp.dot(p_ref[b], jnp.concatenate([cv_ones, vn_ones], axis=0), preferred_element_type=F32)

    inv = 1.0 / (oh_ref[:, :, D_KV:] + sink_term)
    outs = []
    for h in range(N_HEADS):
        rows = slice(h * steps, (h + 1) * steps)
        outs.append((oh_ref[:, rows, _group_columns(h)] * inv[:, rows, :HEAD_DIM]).reshape(tile_rows, HEAD_DIM))
    o_ref[...] = jnp.concatenate(outs, axis=1).astype(o_ref.dtype)


def _attn_sample(z, cache_kt, cache_vt, rel_bias, sinks, batch, steps):
    wc = cache_kt.shape[2]
    bt = SAMPLE_BATCH_TILE
    rows = bt * steps
    smem = pl.BlockSpec(memory_space=pltpu.SMEM)
    assert rows == wc, "the slid window is assembled with lane rotations of a (D_KV, window) tile"
    cache_t_spec = pl.BlockSpec((bt, D_KV, wc), lambda i: (i, 0, 0))
    return pl.pallas_call(
        functools.partial(_attn_sample_kernel, steps=steps),
        grid=(batch // bt,),
        in_specs=[
            smem,
            smem,
            pl.BlockSpec((rows, D_ATTN), lambda i: (i, OFF_Q // D_ATTN)),
            pl.BlockSpec((rows, D_KV), lambda i: (i, OFF_K // D_KV)),
            pl.BlockSpec((rows, D_KV), lambda i: (i, OFF_V // D_KV)),
            cache_t_spec,
            cache_t_spec,
        ],
        out_specs=[
            pl.BlockSpec((rows, D_ATTN), lambda i: (i, 0)),
            cache_t_spec,
            cache_t_spec,
        ],
        out_shape=[
            jax.ShapeDtypeStruct((batch * steps, D_ATTN), BF16),
            jax.ShapeDtypeStruct((batch, D_KV, wc), F32),
            jax.ShapeDtypeStruct((batch, D_KV, wc), F32),
        ],
        scratch_shapes=[
            pltpu.VMEM((N_HEADS * steps, wc), F32),
            pltpu.VMEM((bt, N_HEADS * steps, rows), F32),
            pltpu.VMEM((N_HEADS * steps, LANES), F32),
            pltpu.VMEM((bt, N_HEADS * steps, D_KV), F32),
            pltpu.VMEM((bt, N_HEADS * steps, wc + rows), F32),
            pltpu.VMEM((bt, N_HEADS * steps, wc + rows), BF16),
            pltpu.VMEM((bt, N_HEADS * steps, D_KV + LANES), F32),
        ],
        compiler_params=_params("arbitrary"),
        name="attn_sample",
    )(rel_bias, sinks, z, z, z, cache_kt, cache_vt)


CARRY_ROWS = 8


def _gated_conv(b, wc_ref, u, u1, u2):
    conv = wc_ref[0:1, :] * u2 + wc_ref[1:2, :] * u1 + wc_ref[2:3, :] * u
    return (b.astype(F32) * conv).astype(BF16)


def _mix_tail(conv_branch_input, at_ref, gc_ref, ga_ref, x_ref, g_ref, wco_ref, wao_ref, wo_ref, h_ref, hn_ref):
    y_conv = jnp.dot(conv_branch_input(), wco_ref[...], preferred_element_type=F32)
    y_attn = jnp.dot(at_ref[...], wao_ref[...], preferred_element_type=F32)
    merged = (jax.nn.sigmoid(gc_ref[...].astype(F32)) * y_conv
              + jax.nn.sigmoid(ga_ref[...].astype(F32)) * y_attn)
    h = x_ref[...] + jnp.dot(merged.astype(BF16), wo_ref[...], preferred_element_type=F32)
    h_ref[...] = h
    hn_ref[...] = _rms_rows(h, g_ref[0:1, :]).astype(hn_ref.dtype)


def _delayed_from_scratch(us_ref, rows):
    def at(delay):
        return us_ref[CARRY_ROWS - delay + rows.start:CARRY_ROWS - delay + rows.stop, :]
    return at(0), at(1), at(2)


def _mix_prompt_kernel(b_ref, c_ref, hc_ref, wc_ref, at_ref, gc_ref, ga_ref, x_ref, g_ref,
                       wco_ref, wao_ref, wo_ref, side0_ref, side1_ref, side2_ref,
                       h_ref, hn_ref, st_ref, side0_bf16_ref, side1_bf16_ref, side2_bf16_ref,
                       us_ref, *, tiles_per_seq):
    tm = x_ref.shape[0]
    side0_bf16_ref[...] = side0_ref[...].astype(BF16)
    side1_bf16_ref[...] = side1_ref[...].astype(BF16)
    side2_bf16_ref[...] = side2_ref[...].astype(BF16)

    @pl.when(pl.program_id(0) % tiles_per_seq == 0)
    def _():
        us_ref[0:CARRY_ROWS, :] = jnp.zeros((CARRY_ROWS, D_CONV), F32)

    def conv_branch_input():
        u = c_ref[...].astype(F32) * hc_ref[...].astype(F32)
        us_ref[CARRY_ROWS:, :] = u
        st_ref[...] = u[tm - (CONV_WIDTH - 1):, :]
        yc = _gated_conv(b_ref[...], wc_ref, *_delayed_from_scratch(us_ref, slice(0, tm)))
        us_ref[0:CARRY_ROWS, :] = us_ref[tm:tm + CARRY_ROWS, :]
        return yc

    _mix_tail(conv_branch_input, at_ref, gc_ref, ga_ref, x_ref, g_ref, wco_ref, wao_ref, wo_ref, h_ref, hn_ref)


def _mix_sample_kernel(b_ref, c_ref, hc_ref, st_ref, wc_ref, at_ref, gc_ref, ga_ref, x_ref, g_ref,
                       wco_ref, wao_ref, wo_ref, h_ref, hn_ref, new_st_ref, us_ref, *, steps):
    tm = x_ref.shape[0]
    nseq = tm // steps
    keep = CONV_WIDTH - 1

    def conv_branch_input():
        u = c_ref[...].astype(F32) * hc_ref[...].astype(F32)
        us_ref[:, CARRY_ROWS - keep:CARRY_ROWS, :] = st_ref[...]
        us_ref[:, CARRY_ROWS:, :] = u.reshape(nseq, steps, D_CONV)
        new_st_ref[...] = us_ref[:, CARRY_ROWS + steps - keep:, :]

        def delayed(d):
            return us_ref[:, CARRY_ROWS - d:CARRY_ROWS - d + steps, :].reshape(tm, D_CONV)

        return _gated_conv(b_ref[...], wc_ref, u, delayed(1), delayed(2))

    _mix_tail(conv_branch_input, at_ref, gc_ref, ga_ref, x_ref, g_ref, wco_ref, wao_ref, wo_ref, h_ref, hn_ref)


def _mix_specs(tm):
    resident = functools.partial(pl.BlockSpec, index_map=lambda i: (0, 0), pipeline_mode=pl.Buffered(1))
    row_spec = pl.BlockSpec((tm, D_MODEL), lambda i: (i, 0))

    def zcol(off, width):
        return pl.BlockSpec((tm, width), lambda i: (i, off // width))

    conv_in = [zcol(OFF_B, D_CONV), zcol(OFF_C, D_CONV), zcol(OFF_H, D_CONV)]
    rest_in = [
        pl.BlockSpec((CONV_WIDTH, D_CONV), lambda i: (0, 0)),
        pl.BlockSpec((tm, D_ATTN), lambda i: (i, 0)),
        zcol(OFF_GC, D_MODEL),
        zcol(OFF_GA, D_MODEL),
        row_spec,
        pl.BlockSpec((1, D_MODEL),lambda i: (0, 0)),
        resident((D_CONV, D_MODEL)),
        resident((D_ATTN, D_MODEL)),
        resident((D_MODEL, D_MODEL)),
    ]
    return conv_in, rest_in, row_spec


def _mix_prompt(z, attn, x, w_conv, g_ffn, wco, wao, wo, sides, tm, batch, seq):
    m = x.shape[0]
    assert seq % tm == 0 and tm >= CARRY_ROWS
    tiles_per_seq = seq // tm
    nsteps = m // tm
    conv_in, rest_in, row_spec = _mix_specs(tm)

    def side_spec(side):
        rows = _side_cast_rows(side.shape[0], nsteps)
        return pl.BlockSpec((rows, side.shape[1]), lambda i: (jnp.minimum(i, side.shape[0] // rows - 1), 0))

    side_specs = [side_spec(s) for s in sides]
    return pl.pallas_call(
        functools.partial(_mix_prompt_kernel, tiles_per_seq=tiles_per_seq),
        grid=(nsteps,),
        in_specs=conv_in + rest_in + side_specs,
        out_specs=[row_spec, row_spec,
                   pl.BlockSpec((None, CONV_WIDTH - 1, D_CONV), lambda i: (i // tiles_per_seq, 0, 0))] + side_specs,
        out_shape=[jax.ShapeDtypeStruct((m, D_MODEL), F32), jax.ShapeDtypeStruct((m, D_MODEL), BF16),
                   jax.ShapeDtypeStruct((batch, CONV_WIDTH - 1, D_CONV), F32)]
        + [jax.ShapeDtypeStruct(s.shape, BF16) for s in sides],
        scratch_shapes=[pltpu.VMEM((tm + CARRY_ROWS, D_CONV), F32)],
        compiler_params=_params("arbitrary"),
        name="mix_prompt",
    )(z, z, z, w_conv, attn, z, z, x, g_ffn, wco, wao, wo, *sides)


def _mix_sample(z, state, attn, x, w_conv, g_ffn, wco, wao, wo, tm, steps):
    m = x.shape[0]
    assert tm % steps == 0 and CONV_WIDTH - 1 <= min(steps, CARRY_ROWS)
    conv_in, rest_in, row_spec = _mix_specs(tm)
    state_spec = pl.BlockSpec((tm // steps, CONV_WIDTH - 1, D_CONV), lambda i: (i, 0, 0))
    return pl.pallas_call(
        functools.partial(_mix_sample_kernel, steps=steps),
        grid=(m // tm,),
        in_specs=conv_in + [state_spec] + rest_in,
        out_specs=[row_spec, row_spec, state_spec],
        out_shape=[jax.ShapeDtypeStruct((m, D_MODEL), F32), jax.ShapeDtypeStruct((m, D_MODEL), BF16),
                   jax.ShapeDtypeStruct(state.shape, F32)],
        scratch_shapes=[pltpu.VMEM((tm // steps, CARRY_ROWS + steps, D_CONV), F32)],
        compiler_params=_params("arbitrary"),
        name="mix_sample",
    )(z, z, z, state, w_conv, attn, z, z, x, g_ffn, wco, wao, wo)


def _ffn_kernel(hn_ref, h_ref, gf_ref, wg_ref, wu_ref, wd_ref, y_ref, *, final_norm):
    j = pl.program_id(1)

    @pl.when(j == 0)
    def _():
        y_ref[...] = h_ref[...]

    hn = hn_ref[...]
    a = (jax.nn.silu(jnp.dot(hn, wg_ref[...], preferred_element_type=F32))
         * jnp.dot(hn, wu_ref[...], preferred_element_type=F32))
    y_ref[...] += jnp.dot(a.astype(BF16), wd_ref[...], preferred_element_type=F32)

    if final_norm:
        @pl.when(j == pl.num_programs(1) - 1)
        def _():
            def body(r, carry):
                rows = pl.ds(pl.multiple_of(r * NORM_ROWS, NORM_ROWS), NORM_ROWS)
                y_ref[rows, :] = _rms_rows(y_ref[rows, :], gf_ref[0:1, :])
                return carry

            lax.fori_loop(0, y_ref.shape[0] // NORM_ROWS, body, 0)


def _ffn(hn, h, g_final, wg, wu, wd, tm, tf, final_norm):
    m = h.shape[0]
    return pl.pallas_call(
        functools.partial(_ffn_kernel, final_norm=final_norm),
        grid=(m // tm, D_FF // tf),
        in_specs=[
            pl.BlockSpec((tm, D_MODEL), lambda i, j: (i, 0)),
            pl.BlockSpec((tm, D_MODEL), lambda i, j: (i, 0)),
            pl.BlockSpec((1, D_MODEL),lambda i, j: (0, 0)),
            pl.BlockSpec((D_MODEL, tf), lambda i, j: (0, j)),
            pl.BlockSpec((D_MODEL, tf), lambda i, j: (0, j)),
            pl.BlockSpec((tf, D_MODEL), lambda i, j: (j, 0)),
        ],
        out_specs=pl.BlockSpec((tm, D_MODEL), lambda i, j: (i, 0)),
        out_shape=jax.ShapeDtypeStruct((m, D_MODEL), F32),
        compiler_params=_params("arbitrary", "arbitrary"),
        name="ffn",
    )(hn, h, g_final, wg, wu, wd)


IN_PROJ_TM_PROMPT = 2048
IN_PROJ_TN = 512
MIX_TM = 256
FFN_TM = 1024
FFN_TF = 512


def kernel(x_prompt, x_sample, cache_k, cache_v, state_conv, rel_bias, w_in, w_conv, w_conv_out, sinks,
           w_attn_out, w_o, g_mix, g_ffn, w_gate, w_up, w_down, g_final):
    depth = w_in.shape[0]
    batch, seq, _ = x_prompt.shape
    dec_batch, steps, _ = x_sample.shape
    wc = cache_k.shape[2]
    assert seq % WINDOW == 0 and wc == WINDOW and dec_batch % SAMPLE_BATCH_TILE == 0

    hp = x_prompt.reshape(batch * seq, D_MODEL)
    hs = x_sample.reshape(dec_batch * steps, D_MODEL)
    g_final2 = g_final.reshape(1, D_MODEL)
    kp_l, vp_l, cp_l, ks_l, vs_l, cs_l = [], [], [], [], [], []
    for l in range(depth):
        final = l == depth - 1
        lw = {
            "w_in": w_in[l],
            "g_ffn": g_ffn[l].reshape(1, D_MODEL),
        }
        g_mix_l = g_mix[l].reshape(1, D_MODEL)

        zp, lw["wo"] = _in_proj(hp, g_mix_l, lw["w_in"], w_o[l], IN_PROJ_TM_PROMPT, IN_PROJ_TN)
        zs, attn_p, lw["wco"], lw["wao"] = _in_proj_sample_attn_prompt(
            hs, g_mix_l, lw["w_in"], w_conv_out[l], w_attn_out[l], zp, rel_bias, sinks[l], IN_PROJ_TN, batch, seq)
        hp, hnp, conv_p, lw["wg"], lw["wu"], lw["wd"] = _mix_prompt(
            zp, attn_p, hp, w_conv[l], lw["g_ffn"], lw["wco"], lw["wao"], lw["wo"],
            (w_gate[l], w_up[l], w_down[l]), MIX_TM, batch, seq)
        hp = _ffn(hnp, hp, g_final2, lw["wg"], lw["wu"], lw["wd"], FFN_TM, FFN_TF, final)
        kv_p = zp.reshape(batch, seq, D_IN_PROJ)[:, seq - WINDOW:, OFF_K:OFF_V + D_KV].astype(F32)
        kp_l.append(kv_p[..., :D_KV].reshape(batch, WINDOW, N_KV_HEADS, HEAD_DIM))
        vp_l.append(kv_p[..., D_KV:].reshape(batch, WINDOW, N_KV_HEADS, HEAD_DIM))
        cp_l.append(conv_p)

        cache_kt = jnp.transpose(cache_k[l], (0, 2, 3, 1)).reshape(dec_batch, D_KV, wc)
        cache_vt = jnp.transpose(cache_v[l], (0, 2, 3, 1)).reshape(dec_batch, D_KV, wc)
        attn_s, nk, nv = _attn_sample(zs, cache_kt, cache_vt, rel_bias, sinks[l], dec_batch, steps)
        hs, hns, conv_s = _mix_sample(zs, state_conv[l], attn_s, hs, w_conv[l], lw["g_ffn"], lw["wco"], lw["wao"],
                                      lw["wo"], MIX_TM, steps)
        hs = _ffn(hns, hs, g_final2, lw["wg"], lw["wu"], lw["wd"], FFN_TM, FFN_TF, final)
        ks_l.append(jnp.transpose(nk.reshape(dec_batch, N_KV_HEADS, HEAD_DIM, wc), (0, 3, 1, 2)))
        vs_l.append(jnp.transpose(nv.reshape(dec_batch, N_KV_HEADS, HEAD_DIM, wc), (0, 3, 1, 2)))
        cs_l.append(conv_s)

    return (hp.reshape(batch, seq, D_MODEL), hs.reshape(dec_batch, steps, D_MODEL),
            jnp.stack(kp_l), jnp.stack(vp_l), jnp.stack(cp_l),
            jnp.stack(ks_l), jnp.stack(vs_l), jnp.stack(cs_l))
```

```python
import functools
import math

import numpy as np
import jax
import jax.numpy as jnp
from jax import lax
from jax.experimental import pallas as pl
from jax.experimental.pallas import tpu as pltpu

D_MODEL = 2048
D_CONV = D_MODEL // 2
CONV_WIDTH = 3
HEAD_DIM = 64
N_HEADS = (D_MODEL // 2) // HEAD_DIM
N_KV_HEADS = N_HEADS // 4
GROUP = N_HEADS // N_KV_HEADS
D_ATTN = N_HEADS * HEAD_DIM
D_KV = N_KV_HEADS * HEAD_DIM
WINDOW = 128
NUM_BUCKETS = 32
MAX_DISTANCE = 128
D_FF = -(-8 * D_MODEL // (3 * 256)) * 256
EPS = 1e-6
D_IN_PROJ = 3 * D_CONV + D_ATTN + 2 * D_KV + 2 * D_MODEL

OFF_GC = 0
OFF_GA = D_MODEL
OFF_B = 2 * D_MODEL
OFF_C = OFF_B + D_CONV
OFF_H = OFF_C + D_CONV
OFF_Q = OFF_H + D_CONV
OFF_K = OFF_Q + D_ATTN
OFF_V = OFF_K + D_KV

LANES = 128
NEG = -1e30
F32 = jnp.float32
BF16 = jnp.bfloat16
VMEM_LIMIT = 60 * 1024 * 1024


def _bucket_thresholds():
    max_exact = NUM_BUCKETS // 2
    d = np.arange(MAX_DISTANCE)
    ratio = np.log(np.maximum(d, 1).astype(np.float32) / np.float32(max_exact)) / np.float32(
        math.log(MAX_DISTANCE / max_exact))
    large = max_exact + (ratio * np.float32(NUM_BUCKETS - max_exact)).astype(np.int32)
    large = np.minimum(large, NUM_BUCKETS - 1)
    return [int(np.min(d[(d >= max_exact) & (large >= b)])) for b in range(max_exact + 1, NUM_BUCKETS)]


BUCKET_THRESHOLDS = _bucket_thresholds()


def _params(*semantics):
    return pltpu.CompilerParams(dimension_semantics=semantics, vmem_limit_bytes=VMEM_LIMIT)


def _rms_rows(x, g):
    ms = jnp.mean(x * x, axis=-1, keepdims=True)
    return x * lax.rsqrt(ms + EPS) * g


NORM_ROWS = 256


def _inproj_kernel(x_ref, g_ref, w_ref, side_ref, z_ref, side_bf16_ref, xn_ref):
    _inproj_norm(pl.program_id(1) == 0, x_ref, g_ref, xn_ref)
    _inproj_columns(w_ref, z_ref, xn_ref)
    side_bf16_ref[...] = side_ref[...].astype(BF16)


BF16_ROWS = 16


def _side_cast_rows(rows, nsteps):
    per_step = -(-rows // nsteps)
    per_step = -(-per_step // BF16_ROWS) * BF16_ROWS
    assert rows % per_step == 0
    return per_step


def _inproj_norm(first_column_step, x_ref, g_ref, xn_ref):
    @pl.when(first_column_step)
    def _():
        def body(r, carry):
            rows = pl.ds(pl.multiple_of(r * NORM_ROWS, NORM_ROWS), NORM_ROWS)
            xn_ref[rows, :] = _rms_rows(x_ref[rows, :], g_ref[0:1, :]).astype(xn_ref.dtype)
            return carry

        lax.fori_loop(0, x_ref.shape[0] // NORM_ROWS, body, 0)


def _inproj_columns(w_ref, z_ref, xn_ref):
    z_ref[...] = jnp.dot(xn_ref[...], w_ref[...].astype(BF16), preferred_element_type=F32).astype(z_ref.dtype)


def _in_proj(x, g, w, side, tm, tn):
    m = x.shape[0]
    n = w.shape[1]
    gate_start = n - 2 * D_MODEL
    assert gate_start % tn == 0 and n % tn == 0
    gate_block, nblocks = gate_start // tn, n // tn
    side_rows = _side_cast_rows(side.shape[0], (m // tm) * nblocks)
    side_spec = pl.BlockSpec((side_rows, side.shape[1]),
                             lambda i, j: (jnp.minimum(i * nblocks + j, side.shape[0] // side_rows - 1), 0))
    return pl.pallas_call(
        _inproj_kernel,
        grid=(m // tm, nblocks),
        in_specs=[
            pl.BlockSpec((tm, D_MODEL), lambda i, j: (i, 0)),
            pl.BlockSpec((1, D_MODEL),lambda i, j: (0, 0)),
            pl.BlockSpec((D_MODEL, tn), lambda i, j: (0, (j + gate_block) % nblocks)),
            side_spec,
        ],
        out_specs=[pl.BlockSpec((tm, tn), lambda i, j: (i, j)), side_spec],
        out_shape=[jax.ShapeDtypeStruct((m, n), BF16), jax.ShapeDtypeStruct(side.shape, BF16)],
        scratch_shapes=[pltpu.VMEM((tm, D_MODEL), BF16)],
        compiler_params=_params("arbitrary", "arbitrary"),
        name="in_proj",
    )(x, g, w, side)


def _bias_from_distance(dist, head, relb_ref):
    valid = (dist >= 0) & (dist < WINDOW)
    d = jnp.clip(dist, 0, MAX_DISTANCE - 1)
    bucket = jnp.minimum(d, NUM_BUCKETS // 2)
    for thr in BUCKET_THRESHOLDS:
        bucket = bucket + (d >= thr).astype(jnp.int32)
    bias = jnp.zeros(dist.shape, F32)
    for b in range(NUM_BUCKETS):
        bias = jnp.where(bucket == b, relb_ref[b, head], bias)
    return jnp.where(valid, bias, NEG)


def _biased_scores(s, bias):
    return jnp.where(bias > 0.5 * NEG, s + bias, NEG)


def _head_in_group_columns(q, h):
    half = D_KV // 2
    assert 2 * HEAD_DIM == half and GROUP * HEAD_DIM == D_KV
    g = h // GROUP
    src = q[:, (h // 2) * half:(h // 2 + 1) * half]
    if h % 2 != g % 2:
        src = pltpu.roll(src, HEAD_DIM, 1)
    lane_half = lax.broadcasted_iota(jnp.int32, src.shape, 1) // HEAD_DIM
    placed = jnp.where(lane_half == g % 2, src, 0.0)
    zero = jnp.zeros_like(placed)
    return jnp.concatenate([placed, zero] if g // 2 == 0 else [zero, placed], axis=1)


def _group_columns(h):
    g = h // GROUP
    return slice(g * HEAD_DIM, (g + 1) * HEAD_DIM)


INPROJ_FILL_CHUNKS = 4
PROMPT_BLOCKS_PER_STEP = 4


def _inproj_attn_kernel(x_ref, g_ref, w_ref, side0_ref, side1_ref, relb_ref, sink_ref, q_ref, kp_ref, kc_ref,
                        vp_ref, vc_ref, z_ref, o_ref, side0_bf16_ref, side1_bf16_ref,
                        xn_ref, wb_ref, bias_ref, lhs_ref, *, steps_per_seq):
    j = pl.program_id(0)
    _attn_prompt_bias(j == 0, relb_ref, bias_ref)
    _inproj_norm(j == 0, x_ref, g_ref, xn_ref)
    wb_ref[...] = w_ref[...].astype(BF16)
    side0_bf16_ref[...] = side0_ref[...].astype(BF16)
    side1_bf16_ref[...] = side1_ref[...].astype(BF16)
    chunk = xn_ref.shape[0] // INPROJ_FILL_CHUNKS

    def column_chunk(c):
        rows = slice(c * chunk, (c + 1) * chunk)

        def emit():
            z_ref[rows, :] = jnp.dot(xn_ref[rows, :], wb_ref[...], preferred_element_type=F32).astype(z_ref.dtype)
        return emit

    _attn_prompt_blocks(j % steps_per_seq != 0, sink_ref, q_ref, kp_ref, kc_ref, vp_ref, vc_ref,
                        o_ref, bias_ref, lhs_ref, fillers=[column_chunk(c) for c in range(INPROJ_FILL_CHUNKS)])


def _from_prev_mask():
    qi = lax.broadcasted_iota(jnp.int32, (WINDOW, WINDOW), 0)
    kj = lax.broadcasted_iota(jnp.int32, (WINDOW, WINDOW), 1)
    return qi, kj, kj > qi


def _attn_prompt_bias(first_step, relb_ref, bias_ref):
    @pl.when(first_step)
    def _():
        qi, kj, from_prev = _from_prev_mask()
        dist = jnp.where(from_prev, qi + WINDOW - kj, qi - kj)
        for h in range(N_HEADS):
            rows = slice(h * WINDOW, (h + 1) * WINDOW)
            bias = _bias_from_distance(dist, h, relb_ref)
            bias_ref[1, rows, :] = bias
            bias_ref[0, rows, :] = jnp.where(from_prev, NEG, bias)


def _attn_prompt_blocks(has_prev_block, sink_ref, q_ref, kp_ref, kc_ref, vp_ref, vc_ref, o_ref, bias_ref, lhs_ref,
                        fillers=()):
    _, _, from_prev = _from_prev_mask()
    k_all = jnp.concatenate([kp_ref[...], kc_ref[...]], axis=0)
    v_all = jnp.concatenate([vp_ref[...], vc_ref[...]], axis=0)

    def keys_of(blk):
        return slice(blk * WINDOW, (blk + 2) * WINDOW)

    nblk = PROMPT_BLOCKS_PER_STEP
    fillers = list(fillers)
    per_stage = -(-len(fillers) // (nblk + 2))
    scores, probs = {}, {}
    for stage in range(nblk + 2):
        if stage < nblk:
            q = q_ref[stage * WINDOW:(stage + 1) * WINDOW, :]
            scores[stage] = _block_scores(q, k_all[keys_of(stage)], lhs_ref.at[stage])
        for _ in range(min(per_stage, len(fillers))):
            fillers.pop(0)()
        if 0 <= stage - 1 < nblk:
            blk = stage - 1
            has_prev = has_prev_block.astype(jnp.int32) if blk == 0 else 1
            probs[blk] = _block_softmax(scores.pop(blk), has_prev, from_prev, sink_ref, bias_ref)
        if 0 <= stage - 2 < nblk:
            blk = stage - 2
            o_ref[blk * WINDOW:(blk + 1) * WINDOW, :] = _block_output(
                *probs.pop(blk), v_all[keys_of(blk)]).astype(o_ref.dtype)


def _block_scores(q, kk, lhs_ref):
    q = q.astype(F32) * (HEAD_DIM ** -0.5)
    for h in range(N_HEADS):
        lhs_ref[h * WINDOW:(h + 1) * WINDOW, :] = _head_in_group_columns(q, h).astype(BF16)
    return lax.dot_general(lhs_ref[...], kk, (((1,), (1,)), ((), ())), preferred_element_type=F32)


def _block_softmax(s, has_prev, from_prev, sink_ref, bias_ref):
    probs, sink_terms = [], []
    for h in range(N_HEADS):
        rows = slice(h * WINDOW, (h + 1) * WINDOW)
        sh = _biased_scores(jnp.where(from_prev, s[rows, :WINDOW], s[rows, WINDOW:]), bias_ref[has_prev, rows, :])
        m = jnp.maximum(jnp.max(sh, axis=-1, keepdims=True), sink_ref[h])
        p = jnp.exp(sh - m)
        probs.append(jnp.concatenate([jnp.where(from_prev, p, 0.0), jnp.where(from_prev, 0.0, p)],
                                     axis=1).astype(BF16))
        sink_terms.append(jnp.exp(sink_ref[h] - m))
    return probs, sink_terms


def _block_output(probs, sink_terms, vv):
    assert 2 * HEAD_DIM == LANES and GROUP % 2 == 0
    ones = jnp.ones((vv.shape[0], HEAD_DIM), BF16)
    low_half = lax.broadcasted_iota(jnp.int32, (WINDOW, LANES), 1) < HEAD_DIM
    outs = []
    for g in range(N_KV_HEADS):
        vg = vv[:, g * HEAD_DIM:(g + 1) * HEAD_DIM]
        even_heads = [g * GROUP + r for r in range(0, GROUP, 2)]
        pe = jnp.concatenate([probs[h] for h in even_heads], axis=0)
        po = jnp.concatenate([probs[h + 1] for h in even_heads], axis=0)
        oe = jnp.dot(pe, jnp.concatenate([vg, ones], axis=1), preferred_element_type=F32)
        oo = jnp.dot(po, jnp.concatenate([ones, vg], axis=1), preferred_element_type=F32)
        for k, h in enumerate(even_heads):
            a = oe[k * WINDOW:(k + 1) * WINDOW]
            b = oo[k * WINDOW:(k + 1) * WINDOW]
            out_pair = jnp.where(low_half, a, b)
            sum_pair = pltpu.roll(jnp.where(low_half, b, a), HEAD_DIM, 1)
            sink_pair = jnp.where(low_half, sink_terms[h], sink_terms[h + 1])
            outs.append(out_pair * (1.0 / (sum_pair + sink_pair)))
    return jnp.concatenate(outs, axis=1)


def _in_proj_sample_attn_prompt(x, g, w, side0, side1, zp, rel_bias, sinks, tn, batch, seq):
    tm = x.shape[0]
    n = w.shape[1]
    gate_start = n - 2 * D_MODEL
    assert gate_start % tn == 0 and n % tn == 0
    gate_block, nsteps = gate_start // tn, n // tn
    bps = PROMPT_BLOCKS_PER_STEP
    assert seq % (bps * WINDOW) == 0
    steps_per_seq = seq // (bps * WINDOW)
    attn_steps = batch * steps_per_seq
    assert attn_steps <= nsteps

    def astep(j):
        return jnp.minimum(j, attn_steps - 1)

    def cur(off, width):
        return lambda j: (astep(j), off // width)

    def prev(off, width):
        return lambda j: (astep(j) * bps - jnp.minimum(astep(j) % steps_per_seq, 1), off // width)

    def side_spec(side):
        rows = _side_cast_rows(side.shape[0], nsteps)
        return pl.BlockSpec((rows, side.shape[1]), lambda j: (jnp.minimum(j, side.shape[0] // rows - 1), 0))

    smem = pl.BlockSpec(memory_space=pltpu.SMEM)
    return pl.pallas_call(
        functools.partial(_inproj_attn_kernel, steps_per_seq=steps_per_seq),
        grid=(nsteps,),
        in_specs=[
            pl.BlockSpec((tm, D_MODEL), lambda j: (0, 0)),
            pl.BlockSpec((1, D_MODEL),lambda j: (0, 0)),
            pl.BlockSpec((D_MODEL, tn), lambda j: (0, (j + gate_block) % nsteps)),
            side_spec(side0),
            side_spec(side1),
            smem,
            smem,
            pl.BlockSpec((bps * WINDOW, D_ATTN), cur(OFF_Q, D_ATTN)),
            pl.BlockSpec((WINDOW, D_KV), prev(OFF_K, D_KV)),
            pl.BlockSpec((bps * WINDOW, D_KV), cur(OFF_K, D_KV)),
            pl.BlockSpec((WINDOW, D_KV), prev(OFF_V, D_KV)),
            pl.BlockSpec((bps * WINDOW, D_KV), cur(OFF_V, D_KV)),
        ],
        out_specs=[
            pl.BlockSpec((tm, tn), lambda j: (0, j)),
            pl.BlockSpec((bps * WINDOW, D_ATTN), lambda j: (j, 0)),
            side_spec(side0),
            side_spec(side1),
        ],
        out_shape=[
            jax.ShapeDtypeStruct((tm, n), BF16),
            jax.ShapeDtypeStruct((nsteps * bps * WINDOW, D_ATTN), BF16),
            jax.ShapeDtypeStruct(side0.shape, BF16),
            jax.ShapeDtypeStruct(side1.shape, BF16),
        ],
        scratch_shapes=[pltpu.VMEM((tm, D_MODEL), BF16),
                        pltpu.VMEM((D_MODEL, tn), BF16),
                        pltpu.VMEM((2, N_HEADS * WINDOW, WINDOW), F32),
                        pltpu.VMEM((bps, N_HEADS * WINDOW, D_KV), BF16)],
        compiler_params=_params("arbitrary"),
        name="in_proj_sample_attn_prompt",
    )(x, g, w, side0, side1, rel_bias, sinks, zp, zp, zp, zp, zp)


SAMPLE_BATCH_TILE = 16


def _attn_sample_kernel(relb_ref, sink_ref, q_ref, kn_ref, vn_ref, ckt_ref, cvt_ref,
                        o_ref, nk_ref, nv_ref,
                        bias1_ref, bias2_ref, sinkv_ref, z_ref, s_ref, p_ref, oh_ref, *, steps):
    bt, _, wc = nk_ref.shape
    tile_rows = bt * steps

    @pl.when(pl.program_id(0) == 0)
    def _():
        t = lax.broadcasted_iota(jnp.int32, (steps, wc), 0)
        j = lax.broadcasted_iota(jnp.int32, (steps, wc), 1)
        t2 = lax.broadcasted_iota(jnp.int32, (steps, tile_rows), 0)
        j2 = lax.broadcasted_iota(jnp.int32, (steps, tile_rows), 1)
        own = jnp.where(j2 < steps, t2 - j2, -1)
        for h in range(N_HEADS):
            sl = slice(h * steps, (h + 1) * steps)
            bias1_ref[sl, :] = _bias_from_distance(t + wc - j, h, relb_ref)
            bias2_ref[0, sl, :] = _bias_from_distance(own, h, relb_ref)
            sinkv_ref[sl, :] = jnp.full((steps, LANES), sink_ref[h], F32)
        for b in range(1, bt):
            bias2_ref[b] = pltpu.roll(bias2_ref[0], b * steps, 1)

    kn = kn_ref[...]
    vn = vn_ref[...]
    knt = kn.astype(F32).T.astype(BF16)
    vnt = vn.astype(F32).T.astype(BF16)
    new_lanes = lax.broadcasted_iota(jnp.int32, (D_KV, wc), 1) >= wc - steps
    row = lax.broadcasted_iota(jnp.int32, (tile_rows, wc), 0)
    lane = lax.broadcasted_iota(jnp.int32, (tile_rows, wc), 1)
    first_row = jnp.where(lane >= wc - steps, row - (lane - (wc - steps)), -1)

    def slid(cache_t, new_t, b):
        place = jnp.where(first_row == b * steps, 1.0, 0.0).astype(BF16)
        new = jnp.dot(new_t, place, preferred_element_type=F32)
        return jnp.where(new_lanes, new, pltpu.roll(cache_t, wc - steps, 1))

    q = q_ref[...].astype(F32) * (HEAD_DIM ** -0.5)
    for h in range(N_HEADS):
        z_ref[:, h * steps:(h + 1) * steps, :] = _head_in_group_columns(q, h).reshape(bt, steps, D_KV)

    for b in range(bt):
        ckt = ckt_ref[b]
        nk_ref[b] = slid(ckt, knt, b)
        keys_t = jnp.concatenate([ckt.astype(BF16), knt], axis=1)
        s_ref[b] = jnp.dot(z_ref[b].astype(BF16), keys_t, preferred_element_type=F32)

    bias = jnp.concatenate([jnp.broadcast_to(bias1_ref[...], (bt,) + bias1_ref.shape), bias2_ref[...]], axis=2)
    s = _biased_scores(s_ref[...], bias)
    sink = sinkv_ref[...][None]
    m = jnp.maximum(jnp.max(s, axis=-1, keepdims=True), sink)
    p_ref[...] = jnp.exp(s - jnp.concatenate([m] * (s.shape[-1] // LANES), axis=-1)).astype(BF16)
    sink_term = jnp.exp(sink - m)

    vnt_ones = jnp.concatenate([vnt, jnp.ones((LANES, tile_rows), BF16)], axis=0)
    for b in range(bt):
        cvt = cvt_ref[b]
        nv_ref[b] = slid(cvt, vnt, b)
        cvt_ones = jnp.concatenate([cvt.astype(BF16), jnp.ones((LANES, wc), BF16)], axis=0)
        oh_ref[b] = lax.dot_general(p_ref[b], jnp.concatenate([cvt_ones, vnt_ones], axis=1),
                                    (((1,), (1,)), ((), ())), preferred_element_type=F32)

    inv = 1.0 / (oh_ref[:, :, D_KV:] + sink_term)
    outs = []
    for h in range(N_HEADS):
        rows = slice(h * steps, (h + 1) * steps)
        outs.append((oh_ref[:, rows, _group_columns(h)] * inv[:, rows, :HEAD_DIM]).reshape(tile_rows, HEAD_DIM))
    o_ref[...] = jnp.concatenate(outs, axis=1).astype(o_ref.dtype)


def _attn_sample(z, cache_kt, cache_vt, rel_bias, sinks, batch, steps):
    wc = cache_kt.shape[2]
    bt = SAMPLE_BATCH_TILE
    rows = bt * steps
    smem = pl.BlockSpec(memory_space=pltpu.SMEM)
    cache_t_spec = pl.BlockSpec((bt, D_KV, wc), lambda i: (i, 0, 0))
    return pl.pallas_call(
        functools.partial(_attn_sample_kernel, steps=steps),
        grid=(batch // bt,),
        in_specs=[
            smem,
            smem,
            pl.BlockSpec((rows, D_ATTN), lambda i: (i, OFF_Q // D_ATTN)),
            pl.BlockSpec((rows, D_KV), lambda i: (i, OFF_K // D_KV)),
            pl.BlockSpec((rows, D_KV), lambda i: (i, OFF_V // D_KV)),
            cache_t_spec,
            cache_t_spec,
        ],
        out_specs=[
            pl.BlockSpec((rows, D_ATTN), lambda i: (i, 0)),
            cache_t_spec,
            cache_t_spec,
        ],
        out_shape=[
            jax.ShapeDtypeStruct((batch * steps, D_ATTN), BF16),
            jax.ShapeDtypeStruct((batch, D_KV, wc), F32),
            jax.ShapeDtypeStruct((batch, D_KV, wc), F32),
        ],
        scratch_shapes=[
            pltpu.VMEM((N_HEADS * steps, wc), F32),
            pltpu.VMEM((bt, N_HEADS * steps, rows), F32),
            pltpu.VMEM((N_HEADS * steps, LANES), F32),
            pltpu.VMEM((bt, N_HEADS * steps, D_KV), F32),
            pltpu.VMEM((bt, N_HEADS * steps, wc + rows), F32),
            pltpu.VMEM((bt, N_HEADS * steps, wc + rows), BF16),
            pltpu.VMEM((bt, N_HEADS * steps, D_KV + LANES), F32),
        ],
        compiler_params=_params("arbitrary"),
        name="attn_sample",
    )(rel_bias, sinks, z, z, z, cache_kt, cache_vt)


CARRY_ROWS = 8


def _gated_conv(b, wc_ref, u, u1, u2):
    conv = wc_ref[0:1, :] * u2 + wc_ref[1:2, :] * u1 + wc_ref[2:3, :] * u
    return (b.astype(F32) * conv).astype(BF16)


def _mix_tail(conv_branch_input, at_ref, gc_ref, ga_ref, x_ref, g_ref, wco_ref, wao_ref, wo_ref, h_ref, hn_ref):
    y_conv = jnp.dot(conv_branch_input(), wco_ref[...], preferred_element_type=F32)
    y_attn = jnp.dot(at_ref[...], wao_ref[...], preferred_element_type=F32)
    merged = (jax.nn.sigmoid(gc_ref[...].astype(F32)) * y_conv
              + jax.nn.sigmoid(ga_ref[...].astype(F32)) * y_attn)
    h = x_ref[...] + jnp.dot(merged.astype(BF16), wo_ref[...], preferred_element_type=F32)
    h_ref[...] = h
    hn_ref[...] = _rms_rows(h, g_ref[0:1, :]).astype(hn_ref.dtype)


def _delayed_from_scratch(us_ref, rows):
    def at(delay):
        return us_ref[CARRY_ROWS - delay + rows.start:CARRY_ROWS - delay + rows.stop, :]
    return at(0), at(1), at(2)


def _mix_prompt_kernel(b_ref, c_ref, hc_ref, wc_ref, at_ref, gc_ref, ga_ref, x_ref, g_ref,
                       wco_ref, wao_ref, wo_ref, side0_ref, side1_ref, side2_ref,
                       h_ref, hn_ref, st_ref, side0_bf16_ref, side1_bf16_ref, side2_bf16_ref,
                       us_ref, *, tiles_per_seq):
    tm = x_ref.shape[0]
    side0_bf16_ref[...] = side0_ref[...].astype(BF16)
    side1_bf16_ref[...] = side1_ref[...].astype(BF16)
    side2_bf16_ref[...] = side2_ref[...].astype(BF16)

    @pl.when(pl.program_id(0) % tiles_per_seq == 0)
    def _():
        us_ref[0:CARRY_ROWS, :] = jnp.zeros((CARRY_ROWS, D_CONV), F32)

    def conv_branch_input():
        u = c_ref[...].astype(F32) * hc_ref[...].astype(F32)
        us_ref[CARRY_ROWS:, :] = u
        st_ref[...] = u[tm - (CONV_WIDTH - 1):, :]
        yc = _gated_conv(b_ref[...], wc_ref, *_delayed_from_scratch(us_ref, slice(0, tm)))
        us_ref[0:CARRY_ROWS, :] = us_ref[tm:tm + CARRY_ROWS, :]
        return yc

    _mix_tail(conv_branch_input, at_ref, gc_ref, ga_ref, x_ref, g_ref, wco_ref, wao_ref, wo_ref, h_ref, hn_ref)


def _mix_sample_kernel(b_ref, c_ref, hc_ref, st_ref, wc_ref, at_ref, gc_ref, ga_ref, x_ref, g_ref,
                       wco_ref, wao_ref, wo_ref, h_ref, hn_ref, new_st_ref, us_ref, *, steps):
    tm = x_ref.shape[0]
    nseq = tm // steps
    keep = CONV_WIDTH - 1

    def conv_branch_input():
        u = c_ref[...].astype(F32) * hc_ref[...].astype(F32)
        us_ref[:, CARRY_ROWS - keep:CARRY_ROWS, :] = st_ref[...]
        us_ref[:, CARRY_ROWS:, :] = u.reshape(nseq, steps, D_CONV)
        new_st_ref[...] = us_ref[:, CARRY_ROWS + steps - keep:, :]

        def delayed(d):
            return us_ref[:, CARRY_ROWS - d:CARRY_ROWS - d + steps, :].reshape(tm, D_CONV)

        return _gated_conv(b_ref[...], wc_ref, u, delayed(1), delayed(2))

    _mix_tail(conv_branch_input, at_ref, gc_ref, ga_ref, x_ref, g_ref, wco_ref, wao_ref, wo_ref, h_ref, hn_ref)


def _mix_specs(tm):
    resident = functools.partial(pl.BlockSpec, index_map=lambda i: (0, 0), pipeline_mode=pl.Buffered(1))
    row_spec = pl.BlockSpec((tm, D_MODEL), lambda i: (i, 0))

    def zcol(off, width):
        return pl.BlockSpec((tm, width), lambda i: (i, off // width))

    conv_in = [zcol(OFF_B, D_CONV), zcol(OFF_C, D_CONV), zcol(OFF_H, D_CONV)]
    rest_in = [
        pl.BlockSpec((CONV_WIDTH, D_CONV), lambda i: (0, 0)),
        pl.BlockSpec((tm, D_ATTN), lambda i: (i, 0)),
        zcol(OFF_GC, D_MODEL),
        zcol(OFF_GA, D_MODEL),
        row_spec,
        pl.BlockSpec((1, D_MODEL),lambda i: (0, 0)),
        resident((D_CONV, D_MODEL)),
        resident((D_ATTN, D_MODEL)),
        resident((D_MODEL, D_MODEL)),
    ]
    return conv_in, rest_in, row_spec


def _mix_prompt(z, attn, x, w_conv, g_ffn, wco, wao, wo, sides, tm, batch, seq):
    m = x.shape[0]
    assert seq % tm == 0 and tm >= CARRY_ROWS
    tiles_per_seq = seq // tm
    nsteps = m // tm
    conv_in, rest_in, row_spec = _mix_specs(tm)

    def side_spec(side):
        rows = _side_cast_rows(side.shape[0], nsteps)
        return pl.BlockSpec((rows, side.shape[1]), lambda i: (jnp.minimum(i, side.shape[0] // rows - 1), 0))

    side_specs = [side_spec(s) for s in sides]
    return pl.pallas_call(
        functools.partial(_mix_prompt_kernel, tiles_per_seq=tiles_per_seq),
        grid=(nsteps,),
        in_specs=conv_in + rest_in + side_specs,
        out_specs=[row_spec, row_spec,
                   pl.BlockSpec((None, CONV_WIDTH - 1, D_CONV), lambda i: (i // tiles_per_seq, 0, 0))] + side_specs,
        out_shape=[jax.ShapeDtypeStruct((m, D_MODEL), F32), jax.ShapeDtypeStruct((m, D_MODEL), BF16),
                   jax.ShapeDtypeStruct((batch, CONV_WIDTH - 1, D_CONV), F32)]
        + [jax.ShapeDtypeStruct(s.shape, BF16) for s in sides],
        scratch_shapes=[pltpu.VMEM((tm + CARRY_ROWS, D_CONV), F32)],
        compiler_params=_params("arbitrary"),
        name="mix_prompt",
    )(z, z, z, w_conv, attn, z, z, x, g_ffn, wco, wao, wo, *sides)


def _mix_sample(z, state, attn, x, w_conv, g_ffn, wco, wao, wo, tm, steps):
    m = x.shape[0]
    assert tm % steps == 0 and CONV_WIDTH - 1 <= min(steps, CARRY_ROWS)
    conv_in, rest_in, row_spec = _mix_specs(tm)
    state_spec = pl.BlockSpec((tm // steps, CONV_WIDTH - 1, D_CONV), lambda i: (i, 0, 0))
    return pl.pallas_call(
        functools.partial(_mix_sample_kernel, steps=steps),
        grid=(m // tm,),
        in_specs=conv_in + [state_spec] + rest_in,
        out_specs=[row_spec, row_spec, state_spec],
        out_shape=[jax.ShapeDtypeStruct((m, D_MODEL), F32), jax.ShapeDtypeStruct((m, D_MODEL), BF16),
                   jax.ShapeDtypeStruct(state.shape, F32)],
        scratch_shapes=[pltpu.VMEM((tm // steps, CARRY_ROWS + steps, D_CONV), F32)],
        compiler_params=_params("arbitrary"),
        name="mix_sample",
    )(z, z, z, state, w_conv, attn, z, z, x, g_ffn, wco, wao, wo)


def _ffn_kernel(hn_ref, h_ref, gf_ref, wg_ref, wu_ref, wd_ref, y_ref, *, final_norm):
    j = pl.program_id(1)

    @pl.when(j == 0)
    def _():
        y_ref[...] = h_ref[...]

    hn = hn_ref[...]
    a = (jax.nn.silu(jnp.dot(hn, wg_ref[...], preferred_element_type=F32))
         * jnp.dot(hn, wu_ref[...], preferred_element_type=F32))
    y_ref[...] += jnp.dot(a.astype(BF16), wd_ref[...], preferred_element_type=F32)

    if final_norm:
        @pl.when(j == pl.num_programs(1) - 1)
        def _():
            def body(r, carry):
                rows = pl.ds(pl.multiple_of(r * NORM_ROWS, NORM_ROWS), NORM_ROWS)
                y_ref[rows, :] = _rms_rows(y_ref[rows, :], gf_ref[0:1, :])
                return carry

            lax.fori_loop(0, y_ref.shape[0] // NORM_ROWS, body, 0)


def _ffn(hn, h, g_final, wg, wu, wd, tm, tf, final_norm):
    m = h.shape[0]
    return pl.pallas_call(
        functools.partial(_ffn_kernel, final_norm=final_norm),
        grid=(m // tm, D_FF // tf),
        in_specs=[
            pl.BlockSpec((tm, D_MODEL), lambda i, j: (i, 0)),
            pl.BlockSpec((tm, D_MODEL), lambda i, j: (i, 0)),
            pl.BlockSpec((1, D_MODEL),lambda i, j: (0, 0)),
            pl.BlockSpec((D_MODEL, tf), lambda i, j: (0, j)),
            pl.BlockSpec((D_MODEL, tf), lambda i, j: (0, j)),
            pl.BlockSpec((tf, D_MODEL), lambda i, j: (j, 0)),
        ],
        out_specs=pl.BlockSpec((tm, D_MODEL), lambda i, j: (i, 0)),
        out_shape=jax.ShapeDtypeStruct((m, D_MODEL), F32),
        compiler_params=_params("arbitrary", "arbitrary"),
        name="ffn",
    )(hn, h, g_final, wg, wu, wd)


IN_PROJ_TM_PROMPT = 2048
IN_PROJ_TN = 512
MIX_TM = 256
FFN_TM = 1024
FFN_TF = 512


def kernel(x_prompt, x_sample, cache_k, cache_v, state_conv, rel_bias, w_in, w_conv, w_conv_out, sinks,
           w_attn_out, w_o, g_mix, g_ffn, w_gate, w_up, w_down, g_final):
    depth = w_in.shape[0]
    batch, seq, _ = x_prompt.shape
    dec_batch, steps, _ = x_sample.shape
    wc = cache_k.shape[2]
    assert seq % WINDOW == 0 and wc == WINDOW and dec_batch % SAMPLE_BATCH_TILE == 0

    hp = x_prompt.reshape(batch * seq, D_MODEL)
    hs = x_sample.reshape(dec_batch * steps, D_MODEL)
    g_final2 = g_final.reshape(1, D_MODEL)
    kp_l, vp_l, cp_l, ks_l, vs_l, cs_l = [], [], [], [], [], []
    for l in range(depth):
        final = l == depth - 1
        lw = {
            "w_in": w_in[l],
            "g_ffn": g_ffn[l].reshape(1, D_MODEL),
        }
        g_mix_l = g_mix[l].reshape(1, D_MODEL)

        zp, lw["wo"] = _in_proj(hp, g_mix_l, lw["w_in"], w_o[l], IN_PROJ_TM_PROMPT, IN_PROJ_TN)
        zs, attn_p, lw["wco"], lw["wao"] = _in_proj_sample_attn_prompt(
            hs, g_mix_l, lw["w_in"], w_conv_out[l], w_attn_out[l], zp, rel_bias, sinks[l], IN_PROJ_TN, batch, seq)
        hp, hnp, conv_p, lw["wg"], lw["wu"], lw["wd"] = _mix_prompt(
            zp, attn_p, hp, w_conv[l], lw["g_ffn"], lw["wco"], lw["wao"], lw["wo"],
            (w_gate[l], w_up[l], w_down[l]), MIX_TM, batch, seq)
        hp = _ffn(hnp, hp, g_final2, lw["wg"], lw["wu"], lw["wd"], FFN_TM, FFN_TF, final)
        kv_p = zp.reshape(batch, seq, D_IN_PROJ)[:, seq - WINDOW:, OFF_K:OFF_V + D_KV].astype(F32)
        kp_l.append(kv_p[..., :D_KV].reshape(batch, WINDOW, N_KV_HEADS, HEAD_DIM))
        vp_l.append(kv_p[..., D_KV:].reshape(batch, WINDOW, N_KV_HEADS, HEAD_DIM))
        cp_l.append(conv_p)

        cache_kt = jnp.transpose(cache_k[l], (0, 2, 3, 1)).reshape(dec_batch, D_KV, wc)
        cache_vt = jnp.transpose(cache_v[l], (0, 2, 3, 1)).reshape(dec_batch, D_KV, wc)
        attn_s, nk, nv = _attn_sample(zs, cache_kt, cache_vt, rel_bias, sinks[l], dec_batch, steps)
        hs, hns, conv_s = _mix_sample(zs, state_conv[l], attn_s, hs, w_conv[l], lw["g_ffn"], lw["wco"], lw["wao"],
                                      lw["wo"], MIX_TM, steps)
        hs = _ffn(hns, hs, g_final2, lw["wg"], lw["wu"], lw["wd"], FFN_TM, FFN_TF, final)
        ks_l.append(jnp.transpose(nk.reshape(dec_batch, N_KV_HEADS, HEAD_DIM, wc), (0, 3, 1, 2)))
        vs_l.append(jnp.transpose(nv.reshape(dec_batch, N_KV_HEADS, HEAD_DIM, wc), (0, 3, 1, 2)))
        cs_l.append(conv_s)

    return (hp.reshape(batch, seq, D_MODEL), hs.reshape(dec_batch, steps, D_MODEL),
            jnp.stack(kp_l), jnp.stack(vp_l), jnp.stack(cp_l),
            jnp.stack(ks_l), jnp.stack(vs_l), jnp.stack(cs_l))
```

```python
import functools
import math

import numpy as np
import jax
import jax.numpy as jnp
from jax import lax
from jax.experimental import pallas as pl
from jax.experimental.pallas import tpu as pltpu

D_MODEL = 2048
D_CONV = D_MODEL // 2
CONV_WIDTH = 3
HEAD_DIM = 64
N_HEADS = (D_MODEL // 2) // HEAD_DIM
N_KV_HEADS = N_HEADS // 4
GROUP = N_HEADS // N_KV_HEADS
D_ATTN = N_HEADS * HEAD_DIM
D_KV = N_KV_HEADS * HEAD_DIM
WINDOW = 128
NUM_BUCKETS = 32
MAX_DISTANCE = 128
D_FF = -(-8 * D_MODEL // (3 * 256)) * 256
EPS = 1e-6
D_IN_PROJ = 3 * D_CONV + D_ATTN + 2 * D_KV + 2 * D_MODEL

OFF_GC = 0
OFF_GA = D_MODEL
OFF_B = 2 * D_MODEL
OFF_C = OFF_B + D_CONV
OFF_H = OFF_C + D_CONV
OFF_Q = OFF_H + D_CONV
OFF_K = OFF_Q + D_ATTN
OFF_V = OFF_K + D_KV

LANES = 128
NEG = -1e30
F32 = jnp.float32
BF16 = jnp.bfloat16
VMEM_LIMIT = 60 * 1024 * 1024


def _bucket_thresholds():
    max_exact = NUM_BUCKETS // 2
    d = np.arange(MAX_DISTANCE)
    ratio = np.log(np.maximum(d, 1).astype(np.float32) / np.float32(max_exact)) / np.float32(
        math.log(MAX_DISTANCE / max_exact))
    large = max_exact + (ratio * np.float32(NUM_BUCKETS - max_exact)).astype(np.int32)
    large = np.minimum(large, NUM_BUCKETS - 1)
    return [int(np.min(d[(d >= max_exact) & (large >= b)])) for b in range(max_exact + 1, NUM_BUCKETS)]


BUCKET_THRESHOLDS = _bucket_thresholds()


def _params(*semantics):
    return pltpu.CompilerParams(dimension_semantics=semantics, vmem_limit_bytes=VMEM_LIMIT)


def _rms_rows(x, g):
    ms = jnp.mean(x * x, axis=-1, keepdims=True)
    return x * lax.rsqrt(ms + EPS) * g


NORM_ROWS = 256


def _inproj_kernel(x_ref, g_ref, w_ref, side_ref, z_ref, side_bf16_ref, xn_ref):
    _inproj_norm(pl.program_id(1) == 0, x_ref, g_ref, xn_ref)
    _inproj_columns(w_ref, z_ref, xn_ref)
    side_bf16_ref[...] = side_ref[...].astype(BF16)


BF16_ROWS = 16


def _side_cast_rows(rows, nsteps):
    per_step = -(-rows // nsteps)
    per_step = -(-per_step // BF16_ROWS) * BF16_ROWS
    assert rows % per_step == 0
    return per_step


def _inproj_norm(first_column_step, x_ref, g_ref, xn_ref):
    @pl.when(first_column_step)
    def _():
        def body(r, carry):
            rows = pl.ds(pl.multiple_of(r * NORM_ROWS, NORM_ROWS), NORM_ROWS)
            xn_ref[rows, :] = _rms_rows(x_ref[rows, :], g_ref[0:1, :]).astype(xn_ref.dtype)
            return carry

        lax.fori_loop(0, x_ref.shape[0] // NORM_ROWS, body, 0)


def _inproj_columns(w_ref, z_ref, xn_ref):
    z_ref[...] = jnp.dot(xn_ref[...], w_ref[...].astype(BF16), preferred_element_type=F32).astype(z_ref.dtype)


def _in_proj(x, g, w, side, tm, tn):
    m = x.shape[0]
    n = w.shape[1]
    gate_start = n - 2 * D_MODEL
    assert gate_start % tn == 0 and n % tn == 0
    gate_block, nblocks = gate_start // tn, n // tn
    side_rows = _side_cast_rows(side.shape[0], (m // tm) * nblocks)
    side_spec = pl.BlockSpec((side_rows, side.shape[1]),
                             lambda i, j: (jnp.minimum(i * nblocks + j, side.shape[0] // side_rows - 1), 0))
    return pl.pallas_call(
        _inproj_kernel,
        grid=(m // tm, nblocks),
        in_specs=[
            pl.BlockSpec((tm, D_MODEL), lambda i, j: (i, 0)),
            pl.BlockSpec((1, D_MODEL),lambda i, j: (0, 0)),
            pl.BlockSpec((D_MODEL, tn), lambda i, j: (0, (j + gate_block) % nblocks)),
            side_spec,
        ],
        out_specs=[pl.BlockSpec((tm, tn), lambda i, j: (i, j)), side_spec],
        out_shape=[jax.ShapeDtypeStruct((m, n), BF16), jax.ShapeDtypeStruct(side.shape, BF16)],
        scratch_shapes=[pltpu.VMEM((tm, D_MODEL), BF16)],
        compiler_params=_params("arbitrary", "arbitrary"),
        name="in_proj",
    )(x, g, w, side)


def _bias_from_distance(dist, head, relb_ref):
    valid = (dist >= 0) & (dist < WINDOW)
    d = jnp.clip(dist, 0, MAX_DISTANCE - 1)
    bucket = jnp.minimum(d, NUM_BUCKETS // 2)
    for thr in BUCKET_THRESHOLDS:
        bucket = bucket + (d >= thr).astype(jnp.int32)
    bias = jnp.zeros(dist.shape, F32)
    for b in range(NUM_BUCKETS):
        bias = jnp.where(bucket == b, relb_ref[b, head], bias)
    return jnp.where(valid, bias, NEG)


def _biased_scores(s, bias):
    return jnp.where(bias > 0.5 * NEG, s + bias, NEG)


def _head_in_group_columns(q, h):
    half = D_KV // 2
    assert 2 * HEAD_DIM == half and GROUP * HEAD_DIM == D_KV
    g = h // GROUP
    src = q[:, (h // 2) * half:(h // 2 + 1) * half]
    if h % 2 != g % 2:
        src = pltpu.roll(src, HEAD_DIM, 1)
    lane_half = lax.broadcasted_iota(jnp.int32, src.shape, 1) // HEAD_DIM
    placed = jnp.where(lane_half == g % 2, src, 0.0)
    zero = jnp.zeros_like(placed)
    return jnp.concatenate([placed, zero] if g // 2 == 0 else [zero, placed], axis=1)


def _group_columns(h):
    g = h // GROUP
    return slice(g * HEAD_DIM, (g + 1) * HEAD_DIM)


INPROJ_FILL_CHUNKS = 4
PROMPT_BLOCKS_PER_STEP = 4


def _inproj_attn_kernel(x_ref, g_ref, w_ref, side0_ref, side1_ref, relb_ref, sink_ref, q_ref, kp_ref, kc_ref,
                        vp_ref, vc_ref, z_ref, o_ref, side0_bf16_ref, side1_bf16_ref, kst_ref, vst_ref,
                        xn_ref, wb_ref, bias_ref, lhs_ref, *, steps_per_seq):
    j = pl.program_id(0)
    _attn_prompt_bias(j == 0, relb_ref, bias_ref)
    _inproj_norm(j == 0, x_ref, g_ref, xn_ref)
    wb_ref[...] = w_ref[...].astype(BF16)
    side0_bf16_ref[...] = side0_ref[...].astype(BF16)
    side1_bf16_ref[...] = side1_ref[...].astype(BF16)
    chunk = xn_ref.shape[0] // INPROJ_FILL_CHUNKS

    def column_chunk(c):
        rows = slice(c * chunk, (c + 1) * chunk)

        def emit():
            z_ref[rows, :] = jnp.dot(xn_ref[rows, :], wb_ref[...], preferred_element_type=F32).astype(z_ref.dtype)
        return emit

    _attn_prompt_blocks(j % steps_per_seq != 0, sink_ref, q_ref, kp_ref, kc_ref, vp_ref, vc_ref,
                        o_ref, bias_ref, lhs_ref, fillers=[column_chunk(c) for c in range(INPROJ_FILL_CHUNKS)])
    @pl.when(j % steps_per_seq == steps_per_seq - 1)
    def _():
        last = kc_ref.shape[0] - WINDOW
        kst_ref[0] = kc_ref[last:, :].astype(F32).T
        vst_ref[0] = vc_ref[last:, :].astype(F32).T


def _from_prev_mask():
    qi = lax.broadcasted_iota(jnp.int32, (WINDOW, WINDOW), 0)
    kj = lax.broadcasted_iota(jnp.int32, (WINDOW, WINDOW), 1)
    return qi, kj, kj > qi


def _attn_prompt_bias(first_step, relb_ref, bias_ref):
    @pl.when(first_step)
    def _():
        qi, kj, from_prev = _from_prev_mask()
        dist = jnp.where(from_prev, qi + WINDOW - kj, qi - kj)
        for h in range(N_HEADS):
            rows = slice(h * WINDOW, (h + 1) * WINDOW)
            bias = _bias_from_distance(dist, h, relb_ref)
            bias_ref[1, rows, :] = bias
            bias_ref[0, rows, :] = jnp.where(from_prev, NEG, bias)


def _attn_prompt_blocks(has_prev_block, sink_ref, q_ref, kp_ref, kc_ref, vp_ref, vc_ref, o_ref, bias_ref, lhs_ref,
                        fillers=()):
    _, _, from_prev = _from_prev_mask()
    k_all = jnp.concatenate([kp_ref[...], kc_ref[...]], axis=0)
    v_all = jnp.concatenate([vp_ref[...], vc_ref[...]], axis=0)

    def keys_of(blk):
        return slice(blk * WINDOW, (blk + 2) * WINDOW)

    nblk = PROMPT_BLOCKS_PER_STEP
    fillers = list(fillers)
    per_stage = -(-len(fillers) // (nblk + 2))
    scores, probs = {}, {}
    for stage in range(nblk + 2):
        if stage < nblk:
            q = q_ref[stage * WINDOW:(stage + 1) * WINDOW, :]
            scores[stage] = _block_scores(q, k_all[keys_of(stage)], lhs_ref.at[stage])
        for _ in range(min(per_stage, len(fillers))):
            fillers.pop(0)()
        if 0 <= stage - 1 < nblk:
            blk = stage - 1
            has_prev = has_prev_block.astype(jnp.int32) if blk == 0 else 1
            probs[blk] = _block_softmax(scores.pop(blk), has_prev, from_prev, sink_ref, bias_ref)
        if 0 <= stage - 2 < nblk:
            blk = stage - 2
            o_ref[blk * WINDOW:(blk + 1) * WINDOW, :] = _block_output(
                *probs.pop(blk), v_all[keys_of(blk)]).astype(o_ref.dtype)


def _block_scores(q, kk, lhs_ref):
    q = q.astype(F32) * (HEAD_DIM ** -0.5)
    for h in range(N_HEADS):
        lhs_ref[h * WINDOW:(h + 1) * WINDOW, :] = _head_in_group_columns(q, h).astype(BF16)
    return lax.dot_general(lhs_ref[...], kk, (((1,), (1,)), ((), ())), preferred_element_type=F32)


def _block_softmax(s, has_prev, from_prev, sink_ref, bias_ref):
    probs, sink_terms = [], []
    for h in range(N_HEADS):
        rows = slice(h * WINDOW, (h + 1) * WINDOW)
        sh = _biased_scores(jnp.where(from_prev, s[rows, :WINDOW], s[rows, WINDOW:]), bias_ref[has_prev, rows, :])
        m = jnp.maximum(jnp.max(sh, axis=-1, keepdims=True), sink_ref[h])
        p = jnp.exp(sh - m)
        probs.append(jnp.concatenate([jnp.where(from_prev, p, 0.0), jnp.where(from_prev, 0.0, p)],
                                     axis=1).astype(BF16))
        sink_terms.append(jnp.exp(sink_ref[h] - m))
    return probs, sink_terms


def _block_output(probs, sink_terms, vv):
    assert 2 * HEAD_DIM == LANES and GROUP % 2 == 0
    ones = jnp.ones((vv.shape[0], HEAD_DIM), BF16)
    low_half = lax.broadcasted_iota(jnp.int32, (WINDOW, LANES), 1) < HEAD_DIM
    outs = []
    for g in range(N_KV_HEADS):
        vg = vv[:, g * HEAD_DIM:(g + 1) * HEAD_DIM]
        even_heads = [g * GROUP + r for r in range(0, GROUP, 2)]
        pe = jnp.concatenate([probs[h] for h in even_heads], axis=0)
        po = jnp.concatenate([probs[h + 1] for h in even_heads], axis=0)
        oe = jnp.dot(pe, jnp.concatenate([vg, ones], axis=1), preferred_element_type=F32)
        oo = jnp.dot(po, jnp.concatenate([ones, vg], axis=1), preferred_element_type=F32)
        for k, h in enumerate(even_heads):
            a = oe[k * WINDOW:(k + 1) * WINDOW]
            b = oo[k * WINDOW:(k + 1) * WINDOW]
            out_pair = jnp.where(low_half, a, b)
            sum_pair = pltpu.roll(jnp.where(low_half, b, a), HEAD_DIM, 1)
            sink_pair = jnp.where(low_half, sink_terms[h], sink_terms[h + 1])
            outs.append(out_pair * (1.0 / (sum_pair + sink_pair)))
    return jnp.concatenate(outs, axis=1)


def _in_proj_sample_attn_prompt(x, g, w, side0, side1, zp, rel_bias, sinks, tn, batch, seq):
    tm = x.shape[0]
    n = w.shape[1]
    gate_start = n - 2 * D_MODEL
    assert gate_start % tn == 0 and n % tn == 0
    gate_block, nsteps = gate_start // tn, n // tn
    bps = PROMPT_BLOCKS_PER_STEP
    assert seq % (bps * WINDOW) == 0
    steps_per_seq = seq // (bps * WINDOW)
    attn_steps = batch * steps_per_seq
    assert attn_steps <= nsteps

    def astep(j):
        return jnp.minimum(j, attn_steps - 1)

    def cur(off, width):
        return lambda j: (astep(j), off // width)

    def prev(off, width):
        return lambda j: (astep(j) * bps - jnp.minimum(astep(j) % steps_per_seq, 1), off // width)

    def side_spec(side):
        rows = _side_cast_rows(side.shape[0], nsteps)
        return pl.BlockSpec((rows, side.shape[1]), lambda j: (jnp.minimum(j, side.shape[0] // rows - 1), 0))

    smem = pl.BlockSpec(memory_space=pltpu.SMEM)
    state_spec = pl.BlockSpec((1, D_KV, WINDOW), lambda j: (astep(j) // steps_per_seq, 0, 0))
    return pl.pallas_call(
        functools.partial(_inproj_attn_kernel, steps_per_seq=steps_per_seq),
        grid=(nsteps,),
        in_specs=[
            pl.BlockSpec((tm, D_MODEL), lambda j: (0, 0)),
            pl.BlockSpec((1, D_MODEL),lambda j: (0, 0)),
            pl.BlockSpec((D_MODEL, tn), lambda j: (0, (j + gate_block) % nsteps)),
            side_spec(side0),
            side_spec(side1),
            smem,
            smem,
            pl.BlockSpec((bps * WINDOW, D_ATTN), cur(OFF_Q, D_ATTN)),
            pl.BlockSpec((WINDOW, D_KV), prev(OFF_K, D_KV)),
            pl.BlockSpec((bps * WINDOW, D_KV), cur(OFF_K, D_KV)),
            pl.BlockSpec((WINDOW, D_KV), prev(OFF_V, D_KV)),
            pl.BlockSpec((bps * WINDOW, D_KV), cur(OFF_V, D_KV)),
        ],
        out_specs=[
            pl.BlockSpec((tm, tn), lambda j: (0, j)),
            pl.BlockSpec((bps * WINDOW, D_ATTN), lambda j: (j, 0)),
            side_spec(side0),
            side_spec(side1),
            state_spec,
            state_spec,
        ],
        out_shape=[
            jax.ShapeDtypeStruct((tm, n), BF16),
            jax.ShapeDtypeStruct((nsteps * bps * WINDOW, D_ATTN), BF16),
            jax.ShapeDtypeStruct(side0.shape, BF16),
            jax.ShapeDtypeStruct(side1.shape, BF16),
            jax.ShapeDtypeStruct((batch, D_KV, WINDOW), F32),
            jax.ShapeDtypeStruct((batch, D_KV, WINDOW), F32),
        ],
        scratch_shapes=[pltpu.VMEM((tm, D_MODEL), BF16),
                        pltpu.VMEM((D_MODEL, tn), BF16),
                        pltpu.VMEM((2, N_HEADS * WINDOW, WINDOW), F32),
                        pltpu.VMEM((bps, N_HEADS * WINDOW, D_KV), BF16)],
        compiler_params=_params("arbitrary"),
        name="in_proj_sample_attn_prompt",
    )(x, g, w, side0, side1, rel_bias, sinks, zp, zp, zp, zp, zp)


SAMPLE_BATCH_TILE = 16


def _attn_sample_kernel(relb_ref, sink_ref, q_ref, kn_ref, vn_ref, ckt_ref, cvt_ref,
                        o_ref, nk_ref, nv_ref,
                        bias1_ref, bias2_ref, sinkv_ref, z_ref, s_ref, p_ref, oh_ref, *, steps):
    bt, _, wc = nk_ref.shape
    tile_rows = bt * steps

    @pl.when(pl.program_id(0) == 0)
    def _():
        t = lax.broadcasted_iota(jnp.int32, (steps, wc), 0)
        j = lax.broadcasted_iota(jnp.int32, (steps, wc), 1)
        t2 = lax.broadcasted_iota(jnp.int32, (steps, tile_rows), 0)
        j2 = lax.broadcasted_iota(jnp.int32, (steps, tile_rows), 1)
        own = jnp.where(j2 < steps, t2 - j2, -1)
        for h in range(N_HEADS):
            sl = slice(h * steps, (h + 1) * steps)
            bias1_ref[sl, :] = _bias_from_distance(t + wc - j, h, relb_ref)
            bias2_ref[0, sl, :] = _bias_from_distance(own, h, relb_ref)
            sinkv_ref[sl, :] = jnp.full((steps, LANES), sink_ref[h], F32)
        for b in range(1, bt):
            bias2_ref[b] = pltpu.roll(bias2_ref[0], b * steps, 1)

    kn = kn_ref[...]
    vn = vn_ref[...]
    knt = kn.astype(F32).T.astype(BF16)
    vnt = vn.astype(F32).T.astype(BF16)
    new_lanes = lax.broadcasted_iota(jnp.int32, (D_KV, wc), 1) >= wc - steps
    row = lax.broadcasted_iota(jnp.int32, (tile_rows, wc), 0)
    lane = lax.broadcasted_iota(jnp.int32, (tile_rows, wc), 1)
    first_row = jnp.where(lane >= wc - steps, row - (lane - (wc - steps)), -1)

    def slid(cache_t, new_t, b):
        place = jnp.where(first_row == b * steps, 1.0, 0.0).astype(BF16)
        new = jnp.dot(new_t, place, preferred_element_type=F32)
        return jnp.where(new_lanes, new, pltpu.roll(cache_t, wc - steps, 1))

    q = q_ref[...].astype(F32) * (HEAD_DIM ** -0.5)
    for h in range(N_HEADS):
        z_ref[:, h * steps:(h + 1) * steps, :] = _head_in_group_columns(q, h).reshape(bt, steps, D_KV)

    for b in range(bt):
        ckt = ckt_ref[b]
        nk_ref[b] = slid(ckt, knt, b)
        keys_t = jnp.concatenate([ckt.astype(BF16), knt], axis=1)
        s_ref[b] = jnp.dot(z_ref[b].astype(BF16), keys_t, preferred_element_type=F32)

    bias = jnp.concatenate([jnp.broadcast_to(bias1_ref[...], (bt,) + bias1_ref.shape), bias2_ref[...]], axis=2)
    s = _biased_scores(s_ref[...], bias)
    sink = sinkv_ref[...][None]
    m = jnp.maximum(jnp.max(s, axis=-1, keepdims=True), sink)
    p_ref[...] = jnp.exp(s - jnp.concatenate([m] * (s.shape[-1] // LANES), axis=-1)).astype(BF16)
    sink_term = jnp.exp(sink - m)

    vnt_ones = jnp.concatenate([vnt, jnp.ones((LANES, tile_rows), BF16)], axis=0)
    for b in range(bt):
        cvt = cvt_ref[b]
        nv_ref[b] = slid(cvt, vnt, b)
        cvt_ones = jnp.concatenate([cvt.astype(BF16), jnp.ones((LANES, wc), BF16)], axis=0)
        oh_ref[b] = lax.dot_general(p_ref[b], jnp.concatenate([cvt_ones, vnt_ones], axis=1),
                                    (((1,), (1,)), ((), ())), preferred_element_type=F32)

    inv = 1.0 / (oh_ref[:, :, D_KV:] + sink_term)
    outs = []
    for h in range(N_HEADS):
        rows = slice(h * steps, (h + 1) * steps)
        outs.append((oh_ref[:, rows, _group_columns(h)] * inv[:, rows, :HEAD_DIM]).reshape(tile_rows, HEAD_DIM))
    o_ref[...] = jnp.concatenate(outs, axis=1).astype(o_ref.dtype)


def _attn_sample(z, cache_kt, cache_vt, rel_bias, sinks, batch, steps):
    wc = cache_kt.shape[2]
    bt = SAMPLE_BATCH_TILE
    rows = bt * steps
    smem = pl.BlockSpec(memory_space=pltpu.SMEM)
    cache_t_spec = pl.BlockSpec((bt, D_KV, wc), lambda i: (i, 0, 0))
    return pl.pallas_call(
        functools.partial(_attn_sample_kernel, steps=steps),
        grid=(batch // bt,),
        in_specs=[
            smem,
            smem,
            pl.BlockSpec((rows, D_ATTN), lambda i: (i, OFF_Q // D_ATTN)),
            pl.BlockSpec((rows, D_KV), lambda i: (i, OFF_K // D_KV)),
            pl.BlockSpec((rows, D_KV), lambda i: (i, OFF_V // D_KV)),
            cache_t_spec,
            cache_t_spec,
        ],
        out_specs=[
            pl.BlockSpec((rows, D_ATTN), lambda i: (i, 0)),
            cache_t_spec,
            cache_t_spec,
        ],
        out_shape=[
            jax.ShapeDtypeStruct((batch * steps, D_ATTN), BF16),
            jax.ShapeDtypeStruct((batch, D_KV, wc), F32),
            jax.ShapeDtypeStruct((batch, D_KV, wc), F32),
        ],
        scratch_shapes=[
            pltpu.VMEM((N_HEADS * steps, wc), F32),
            pltpu.VMEM((bt, N_HEADS * steps, rows), F32),
            pltpu.VMEM((N_HEADS * steps, LANES), F32),
            pltpu.VMEM((bt, N_HEADS * steps, D_KV), F32),
            pltpu.VMEM((bt, N_HEADS * steps, wc + rows), F32),
            pltpu.VMEM((bt, N_HEADS * steps, wc + rows), BF16),
            pltpu.VMEM((bt, N_HEADS * steps, D_KV + LANES), F32),
        ],
        compiler_params=_params("arbitrary"),
        name="attn_sample",
    )(rel_bias, sinks, z, z, z, cache_kt, cache_vt)


CARRY_ROWS = 8


def _gated_conv(b, wc_ref, u, u1, u2):
    conv = wc_ref[0:1, :] * u2 + wc_ref[1:2, :] * u1 + wc_ref[2:3, :] * u
    return (b.astype(F32) * conv).astype(BF16)


def _mix_tail(conv_branch_input, at_ref, gc_ref, ga_ref, x_ref, g_ref, wco_ref, wao_ref, wo_ref, h_ref, hn_ref):
    y_conv = jnp.dot(conv_branch_input(), wco_ref[...], preferred_element_type=F32)
    y_attn = jnp.dot(at_ref[...], wao_ref[...], preferred_element_type=F32)
    merged = (jax.nn.sigmoid(gc_ref[...].astype(F32)) * y_conv
              + jax.nn.sigmoid(ga_ref[...].astype(F32)) * y_attn)
    h = x_ref[...] + jnp.dot(merged.astype(BF16), wo_ref[...], preferred_element_type=F32)
    h_ref[...] = h
    hn_ref[...] = _rms_rows(h, g_ref[0:1, :]).astype(hn_ref.dtype)


def _mix_prompt_kernel(b_ref, c_ref, hc_ref, wc_ref, at_ref, gc_ref, ga_ref, x_ref, g_ref,
                       wco_ref, wao_ref, wo_ref, side0_ref, side1_ref, side2_ref,
                       h_ref, hn_ref, st_ref, side0_bf16_ref, side1_bf16_ref, side2_bf16_ref,
                       us_ref, *, tiles_per_seq):
    tm = x_ref.shape[0]
    side0_bf16_ref[...] = side0_ref[...].astype(BF16)
    side1_bf16_ref[...] = side1_ref[...].astype(BF16)
    side2_bf16_ref[...] = side2_ref[...].astype(BF16)

    @pl.when(pl.program_id(0) % tiles_per_seq == 0)
    def _():
        us_ref[...] = jnp.zeros((CARRY_ROWS, D_CONV), F32)

    def conv_branch_input():
        u = c_ref[...].astype(F32) * hc_ref[...].astype(F32)
        st_ref[...] = u[tm - (CONV_WIDTH - 1):, :]
        u_ext = jnp.concatenate([us_ref[...], u], axis=0)
        us_ref[...] = u[tm - CARRY_ROWS:, :]
        return _gated_conv(b_ref[...], wc_ref, u,
                           pltpu.roll(u_ext, 1, 0)[CARRY_ROWS:, :], pltpu.roll(u_ext, 2, 0)[CARRY_ROWS:, :])

    _mix_tail(conv_branch_input, at_ref, gc_ref, ga_ref, x_ref, g_ref, wco_ref, wao_ref, wo_ref, h_ref, hn_ref)


def _mix_sample_kernel(b_ref, c_ref, hc_ref, st_ref, wc_ref, at_ref, gc_ref, ga_ref, x_ref, g_ref,
                       wco_ref, wao_ref, wo_ref, h_ref, hn_ref, new_st_ref, us_ref, *, steps):
    tm = x_ref.shape[0]
    nseq = tm // steps
    keep = CONV_WIDTH - 1

    def conv_branch_input():
        u = c_ref[...].astype(F32) * hc_ref[...].astype(F32)
        us_ref[:, CARRY_ROWS - keep:CARRY_ROWS, :] = st_ref[...]
        us_ref[:, CARRY_ROWS:, :] = u.reshape(nseq, steps, D_CONV)
        new_st_ref[...] = us_ref[:, CARRY_ROWS + steps - keep:, :]

        def delayed(d):
            return us_ref[:, CARRY_ROWS - d:CARRY_ROWS - d + steps, :].reshape(tm, D_CONV)

        return _gated_conv(b_ref[...], wc_ref, u, delayed(1), delayed(2))

    _mix_tail(conv_branch_input, at_ref, gc_ref, ga_ref, x_ref, g_ref, wco_ref, wao_ref, wo_ref, h_ref, hn_ref)


def _mix_specs(tm):
    resident = functools.partial(pl.BlockSpec, index_map=lambda i: (0, 0), pipeline_mode=pl.Buffered(1))
    row_spec = pl.BlockSpec((tm, D_MODEL), lambda i: (i, 0))

    def zcol(off, width):
        return pl.BlockSpec((tm, width), lambda i: (i, off // width))

    conv_in = [zcol(OFF_B, D_CONV), zcol(OFF_C, D_CONV), zcol(OFF_H, D_CONV)]
    rest_in = [
        pl.BlockSpec((CONV_WIDTH, D_CONV), lambda i: (0, 0)),
        pl.BlockSpec((tm, D_ATTN), lambda i: (i, 0)),
        zcol(OFF_GC, D_MODEL),
        zcol(OFF_GA, D_MODEL),
        row_spec,
        pl.BlockSpec((1, D_MODEL),lambda i: (0, 0)),
        resident((D_CONV, D_MODEL)),
        resident((D_ATTN, D_MODEL)),
        resident((D_MODEL, D_MODEL)),
    ]
    return conv_in, rest_in, row_spec


def _mix_prompt(z, attn, x, w_conv, g_ffn, wco, wao, wo, sides, tm, batch, seq):
    m = x.shape[0]
    assert seq % tm == 0 and tm >= CARRY_ROWS
    tiles_per_seq = seq // tm
    nsteps = m // tm
    conv_in, rest_in, row_spec = _mix_specs(tm)

    def side_spec(side):
        rows = _side_cast_rows(side.shape[0], nsteps)
        return pl.BlockSpec((rows, side.shape[1]), lambda i: (jnp.minimum(i, side.shape[0] // rows - 1), 0))

    side_specs = [side_spec(s) for s in sides]
    return pl.pallas_call(
        functools.partial(_mix_prompt_kernel, tiles_per_seq=tiles_per_seq),
        grid=(nsteps,),
        in_specs=conv_in + rest_in + side_specs,
        out_specs=[row_spec, row_spec,
                   pl.BlockSpec((None, CONV_WIDTH - 1, D_CONV), lambda i: (i // tiles_per_seq, 0, 0))] + side_specs,
        out_shape=[jax.ShapeDtypeStruct((m, D_MODEL), F32), jax.ShapeDtypeStruct((m, D_MODEL), BF16),
                   jax.ShapeDtypeStruct((batch, CONV_WIDTH - 1, D_CONV), F32)]
        + [jax.ShapeDtypeStruct(s.shape, BF16) for s in sides],
        scratch_shapes=[pltpu.VMEM((CARRY_ROWS, D_CONV), F32)],
        compiler_params=_params("arbitrary"),
        name="mix_prompt",
    )(z, z, z, w_conv, attn, z, z, x, g_ffn, wco, wao, wo, *sides)


def _mix_sample(z, state, attn, x, w_conv, g_ffn, wco, wao, wo, tm, steps):
    m = x.shape[0]
    assert tm % steps == 0 and CONV_WIDTH - 1 <= min(steps, CARRY_ROWS)
    conv_in, rest_in, row_spec = _mix_specs(tm)
    state_spec = pl.BlockSpec((tm // steps, CONV_WIDTH - 1, D_CONV), lambda i: (i, 0, 0))
    return pl.pallas_call(
        functools.partial(_mix_sample_kernel, steps=steps),
        grid=(m // tm,),
        in_specs=conv_in + [state_spec] + rest_in,
        out_specs=[row_spec, row_spec, state_spec],
        out_shape=[jax.ShapeDtypeStruct((m, D_MODEL), F32), jax.ShapeDtypeStruct((m, D_MODEL), BF16),
                   jax.ShapeDtypeStruct(state.shape, F32)],
        scratch_shapes=[pltpu.VMEM((tm // steps, CARRY_ROWS + steps, D_CONV), F32)],
        compiler_params=_params("arbitrary"),
        name="mix_sample",
    )(z, z, z, state, w_conv, attn, z, z, x, g_ffn, wco, wao, wo)


def _ffn_kernel(hn_ref, h_ref, gf_ref, wg_ref, wu_ref, wd_ref, y_ref, *, final_norm):
    j = pl.program_id(1)

    @pl.when(j == 0)
    def _():
        y_ref[...] = h_ref[...]

    hn = hn_ref[...]
    a = (jax.nn.silu(jnp.dot(hn, wg_ref[...], preferred_element_type=F32))
         * jnp.dot(hn, wu_ref[...], preferred_element_type=F32))
    y_ref[...] += jnp.dot(a.astype(BF16), wd_ref[...], preferred_element_type=F32)

    if final_norm:
        @pl.when(j == pl.num_programs(1) - 1)
        def _():
            def body(r, carry):
                rows = pl.ds(pl.multiple_of(r * NORM_ROWS, NORM_ROWS), NORM_ROWS)
                y_ref[rows, :] = _rms_rows(y_ref[rows, :], gf_ref[0:1, :])
                return carry

            lax.fori_loop(0, y_ref.shape[0] // NORM_ROWS, body, 0)


def _ffn(hn, h, g_final, wg, wu, wd, tm, tf, final_norm):
    m = h.shape[0]
    return pl.pallas_call(
        functools.partial(_ffn_kernel, final_norm=final_norm),
        grid=(m // tm, D_FF // tf),
        in_specs=[
            pl.BlockSpec((tm, D_MODEL), lambda i, j: (i, 0)),
            pl.BlockSpec((tm, D_MODEL), lambda i, j: (i, 0)),
            pl.BlockSpec((1, D_MODEL),lambda i, j: (0, 0)),
            pl.BlockSpec((D_MODEL, tf), lambda i, j: (0, j)),
            pl.BlockSpec((D_MODEL, tf), lambda i, j: (0, j)),
            pl.BlockSpec((tf, D_MODEL), lambda i, j: (j, 0)),
        ],
        out_specs=pl.BlockSpec((tm, D_MODEL), lambda i, j: (i, 0)),
        out_shape=jax.ShapeDtypeStruct((m, D_MODEL), F32),
        compiler_params=_params("arbitrary", "arbitrary"),
        name="ffn",
    )(hn, h, g_final, wg, wu, wd)


IN_PROJ_TM_PROMPT = 2048
IN_PROJ_TN = 512
MIX_TM = 256
FFN_TM = 1024
FFN_TF = 512


def kernel(x_prompt, x_sample, cache_k, cache_v, state_conv, rel_bias, w_in, w_conv, w_conv_out, sinks,
           w_attn_out, w_o, g_mix, g_ffn, w_gate, w_up, w_down, g_final):
    depth = w_in.shape[0]
    batch, seq, _ = x_prompt.shape
    dec_batch, steps, _ = x_sample.shape
    wc = cache_k.shape[2]
    assert seq % WINDOW == 0 and wc == WINDOW and dec_batch % SAMPLE_BATCH_TILE == 0

    hp = x_prompt.reshape(batch * seq, D_MODEL)
    hs = x_sample.reshape(dec_batch * steps, D_MODEL)
    g_final2 = g_final.reshape(1, D_MODEL)
    kp_l, vp_l, cp_l, ks_l, vs_l, cs_l = [], [], [], [], [], []
    for l in range(depth):
        final = l == depth - 1
        lw = {
            "w_in": w_in[l],
            "g_ffn": g_ffn[l].reshape(1, D_MODEL),
        }
        g_mix_l = g_mix[l].reshape(1, D_MODEL)

        zp, lw["wo"] = _in_proj(hp, g_mix_l, lw["w_in"], w_o[l], IN_PROJ_TM_PROMPT, IN_PROJ_TN)
        zs, attn_p, lw["wco"], lw["wao"], kp_t, vp_t = _in_proj_sample_attn_prompt(
            hs, g_mix_l, lw["w_in"], w_conv_out[l], w_attn_out[l], zp, rel_bias, sinks[l], IN_PROJ_TN, batch, seq)
        hp, hnp, conv_p, lw["wg"], lw["wu"], lw["wd"] = _mix_prompt(
            zp, attn_p, hp, w_conv[l], lw["g_ffn"], lw["wco"], lw["wao"], lw["wo"],
            (w_gate[l], w_up[l], w_down[l]), MIX_TM, batch, seq)
        hp = _ffn(hnp, hp, g_final2, lw["wg"], lw["wu"], lw["wd"], FFN_TM, FFN_TF, final)
        kp_l.append(jnp.transpose(kp_t.reshape(batch, N_KV_HEADS, HEAD_DIM, WINDOW), (0, 3, 1, 2)))
        vp_l.append(jnp.transpose(vp_t.reshape(batch, N_KV_HEADS, HEAD_DIM, WINDOW), (0, 3, 1, 2)))
        cp_l.append(conv_p)

        cache_kt = jnp.transpose(cache_k[l], (0, 2, 3, 1)).reshape(dec_batch, D_KV, wc)
        cache_vt = jnp.transpose(cache_v[l], (0, 2, 3, 1)).reshape(dec_batch, D_KV, wc)
        attn_s, nk, nv = _attn_sample(zs, cache_kt, cache_vt, rel_bias, sinks[l], dec_batch, steps)
        hs, hns, conv_s = _mix_sample(zs, state_conv[l], attn_s, hs, w_conv[l], lw["g_ffn"], lw["wco"], lw["wao"],
                                      lw["wo"], MIX_TM, steps)
        hs = _ffn(hns, hs, g_final2, lw["wg"], lw["wu"], lw["wd"], FFN_TM, FFN_TF, final)
        ks_l.append(jnp.transpose(nk.reshape(dec_batch, N_KV_HEADS, HEAD_DIM, wc), (0, 3, 1, 2)))
        vs_l.append(jnp.transpose(nv.reshape(dec_batch, N_KV_HEADS, HEAD_DIM, wc), (0, 3, 1, 2)))
        cs_l.append(conv_s)

    return (hp.reshape(batch, seq, D_MODEL), hs.reshape(dec_batch, steps, D_MODEL),
            jnp.stack(kp_l), jnp.stack(vp_l), jnp.stack(cp_l),
            jnp.stack(ks_l), jnp.stack(vs_l), jnp.stack(cs_l))
```

```python
import functools
import math

import numpy as np
import jax
import jax.numpy as jnp
from jax import lax
from jax.experimental import pallas as pl
from jax.experimental.pallas import tpu as pltpu

D_MODEL = 2048
D_CONV = D_MODEL // 2
CONV_WIDTH = 3
HEAD_DIM = 64
N_HEADS = (D_MODEL // 2) // HEAD_DIM
N_KV_HEADS = N_HEADS // 4
GROUP = N_HEADS // N_KV_HEADS
D_ATTN = N_HEADS * HEAD_DIM
D_KV = N_KV_HEADS * HEAD_DIM
WINDOW = 128
NUM_BUCKETS = 32
MAX_DISTANCE = 128
D_FF = -(-8 * D_MODEL // (3 * 256)) * 256
EPS = 1e-6
D_IN_PROJ = 3 * D_CONV + D_ATTN + 2 * D_KV + 2 * D_MODEL

OFF_GC = 0
OFF_GA = D_MODEL
OFF_B = 2 * D_MODEL
OFF_C = OFF_B + D_CONV
OFF_H = OFF_C + D_CONV
OFF_Q = OFF_H + D_CONV
OFF_K = OFF_Q + D_ATTN
OFF_V = OFF_K + D_KV

LANES = 128
NEG = -1e30
F32 = jnp.float32
BF16 = jnp.bfloat16
VMEM_LIMIT = 60 * 1024 * 1024


def _bucket_thresholds():
    max_exact = NUM_BUCKETS // 2
    d = np.arange(MAX_DISTANCE)
    ratio = np.log(np.maximum(d, 1).astype(np.float32) / np.float32(max_exact)) / np.float32(
        math.log(MAX_DISTANCE / max_exact))
    large = max_exact + (ratio * np.float32(NUM_BUCKETS - max_exact)).astype(np.int32)
    large = np.minimum(large, NUM_BUCKETS - 1)
    return [int(np.min(d[(d >= max_exact) & (large >= b)])) for b in range(max_exact + 1, NUM_BUCKETS)]


BUCKET_THRESHOLDS = _bucket_thresholds()


def _params(*semantics):
    return pltpu.CompilerParams(dimension_semantics=semantics, vmem_limit_bytes=VMEM_LIMIT)


def _rms_rows(x, g):
    ms = jnp.mean(x * x, axis=-1, keepdims=True)
    return x * lax.rsqrt(ms + EPS) * g


NORM_ROWS = 256


def _inproj_kernel(x_ref, g_ref, w_ref, side_ref, z_ref, side_bf16_ref, kst_ref, vst_ref, xn_ref):
    _inproj_norm(pl.program_id(1) == 0, x_ref, g_ref, xn_ref)
    _inproj_columns(w_ref, z_ref, xn_ref)
    side_bf16_ref[...] = side_ref[...].astype(BF16)

    @pl.when(pl.program_id(1) == pl.num_programs(1) - 1)
    def _():
        last = z_ref.shape[0] - WINDOW
        kst_ref[0] = z_ref[last:, 0:D_KV].astype(F32).T
        vst_ref[0] = z_ref[last:, D_KV:2 * D_KV].astype(F32).T


BF16_ROWS = 16


def _side_cast_rows(rows, nsteps):
    per_step = -(-rows // nsteps)
    per_step = -(-per_step // BF16_ROWS) * BF16_ROWS
    assert rows % per_step == 0
    return per_step


def _inproj_norm(first_column_step, x_ref, g_ref, xn_ref):
    @pl.when(first_column_step)
    def _():
        def body(r, carry):
            rows = pl.ds(pl.multiple_of(r * NORM_ROWS, NORM_ROWS), NORM_ROWS)
            xn_ref[rows, :] = _rms_rows(x_ref[rows, :], g_ref[0:1, :]).astype(xn_ref.dtype)
            return carry

        lax.fori_loop(0, x_ref.shape[0] // NORM_ROWS, body, 0)


def _inproj_columns(w_ref, z_ref, xn_ref):
    z_ref[...] = jnp.dot(xn_ref[...], w_ref[...].astype(BF16), preferred_element_type=F32).astype(z_ref.dtype)


def _in_proj(x, g, w, side, tm, tn, seq):
    m = x.shape[0]
    n = w.shape[1]
    gate_start = n - 2 * D_MODEL
    assert gate_start % tn == 0 and n % tn == 0
    gate_block, nblocks = gate_start // tn, n // tn
    assert OFF_K == (nblocks - 1) * tn and tn == 2 * D_KV and seq % tm == 0 and tm >= WINDOW
    state_spec = pl.BlockSpec((1, D_KV, WINDOW), lambda i, j: ((i * tm) // seq, 0, 0))
    state_shape = jax.ShapeDtypeStruct((m // seq, D_KV, WINDOW), F32)
    side_rows = _side_cast_rows(side.shape[0], (m // tm) * nblocks)
    side_spec = pl.BlockSpec((side_rows, side.shape[1]),
                             lambda i, j: (jnp.minimum(i * nblocks + j, side.shape[0] // side_rows - 1), 0))
    return pl.pallas_call(
        _inproj_kernel,
        grid=(m // tm, nblocks),
        in_specs=[
            pl.BlockSpec((tm, D_MODEL), lambda i, j: (i, 0)),
            pl.BlockSpec((1, D_MODEL),lambda i, j: (0, 0)),
            pl.BlockSpec((D_MODEL, tn), lambda i, j: (0, (j + gate_block) % nblocks)),
            side_spec,
        ],
        out_specs=[pl.BlockSpec((tm, tn), lambda i, j: (i, j)), side_spec, state_spec, state_spec],
        out_shape=[jax.ShapeDtypeStruct((m, n), BF16), jax.ShapeDtypeStruct(side.shape, BF16),
                   state_shape, state_shape],
        scratch_shapes=[pltpu.VMEM((tm, D_MODEL), BF16)],
        compiler_params=_params("arbitrary", "arbitrary"),
        name="in_proj",
    )(x, g, w, side)


def _bias_from_distance(dist, head, relb_ref):
    valid = (dist >= 0) & (dist < WINDOW)
    d = jnp.clip(dist, 0, MAX_DISTANCE - 1)
    bucket = jnp.minimum(d, NUM_BUCKETS // 2)
    for thr in BUCKET_THRESHOLDS:
        bucket = bucket + (d >= thr).astype(jnp.int32)
    bias = jnp.zeros(dist.shape, F32)
    for b in range(NUM_BUCKETS):
        bias = jnp.where(bucket == b, relb_ref[b, head], bias)
    return jnp.where(valid, bias, NEG)


def _biased_scores(s, bias):
    return jnp.where(bias > 0.5 * NEG, s + bias, NEG)


def _head_in_group_columns(q, h):
    half = D_KV // 2
    assert 2 * HEAD_DIM == half and GROUP * HEAD_DIM == D_KV
    g = h // GROUP
    src = q[:, (h // 2) * half:(h // 2 + 1) * half]
    if h % 2 != g % 2:
        src = pltpu.roll(src, HEAD_DIM, 1)
    lane_half = lax.broadcasted_iota(jnp.int32, src.shape, 1) // HEAD_DIM
    placed = jnp.where(lane_half == g % 2, src, 0.0)
    zero = jnp.zeros_like(placed)
    return jnp.concatenate([placed, zero] if g // 2 == 0 else [zero, placed], axis=1)


def _group_columns(h):
    g = h // GROUP
    return slice(g * HEAD_DIM, (g + 1) * HEAD_DIM)


INPROJ_FILL_CHUNKS = 4
PROMPT_BLOCKS_PER_STEP = 4


def _inproj_attn_kernel(x_ref, g_ref, w_ref, side0_ref, side1_ref, relb_ref, sink_ref, q_ref, kp_ref, kc_ref,
                        vp_ref, vc_ref, z_ref, o_ref, side0_bf16_ref, side1_bf16_ref,
                        xn_ref, wb_ref, bias_ref, lhs_ref, *, steps_per_seq):
    j = pl.program_id(0)
    _attn_prompt_bias(j == 0, relb_ref, bias_ref)
    _inproj_norm(j == 0, x_ref, g_ref, xn_ref)
    wb_ref[...] = w_ref[...].astype(BF16)
    side0_bf16_ref[...] = side0_ref[...].astype(BF16)
    side1_bf16_ref[...] = side1_ref[...].astype(BF16)
    chunk = xn_ref.shape[0] // INPROJ_FILL_CHUNKS

    def column_chunk(c):
        rows = slice(c * chunk, (c + 1) * chunk)

        def emit():
            z_ref[rows, :] = jnp.dot(xn_ref[rows, :], wb_ref[...], preferred_element_type=F32).astype(z_ref.dtype)
        return emit

    _attn_prompt_blocks(j % steps_per_seq != 0, sink_ref, q_ref, kp_ref, kc_ref, vp_ref, vc_ref,
                        o_ref, bias_ref, lhs_ref, fillers=[column_chunk(c) for c in range(INPROJ_FILL_CHUNKS)])


def _from_prev_mask():
    qi = lax.broadcasted_iota(jnp.int32, (WINDOW, WINDOW), 0)
    kj = lax.broadcasted_iota(jnp.int32, (WINDOW, WINDOW), 1)
    return qi, kj, kj > qi


def _attn_prompt_bias(first_step, relb_ref, bias_ref):
    @pl.when(first_step)
    def _():
        qi, kj, from_prev = _from_prev_mask()
        dist = jnp.where(from_prev, qi + WINDOW - kj, qi - kj)
        for h in range(N_HEADS):
            rows = slice(h * WINDOW, (h + 1) * WINDOW)
            bias = _bias_from_distance(dist, h, relb_ref)
            bias_ref[1, rows, :] = bias
            bias_ref[0, rows, :] = jnp.where(from_prev, NEG, bias)


def _attn_prompt_blocks(has_prev_block, sink_ref, q_ref, kp_ref, kc_ref, vp_ref, vc_ref, o_ref, bias_ref, lhs_ref,
                        fillers=()):
    _, _, from_prev = _from_prev_mask()
    k_all = jnp.concatenate([kp_ref[...], kc_ref[...]], axis=0)
    v_all = jnp.concatenate([vp_ref[...], vc_ref[...]], axis=0)

    def keys_of(blk):
        return slice(blk * WINDOW, (blk + 2) * WINDOW)

    nblk = PROMPT_BLOCKS_PER_STEP
    fillers = list(fillers)
    per_stage = -(-len(fillers) // (nblk + 2))
    scores, probs = {}, {}
    for stage in range(nblk + 2):
        if stage < nblk:
            q = q_ref[stage * WINDOW:(stage + 1) * WINDOW, :]
            scores[stage] = _block_scores(q, k_all[keys_of(stage)], lhs_ref.at[stage])
        for _ in range(min(per_stage, len(fillers))):
            fillers.pop(0)()
        if 0 <= stage - 1 < nblk:
            blk = stage - 1
            has_prev = has_prev_block.astype(jnp.int32) if blk == 0 else 1
            probs[blk] = _block_softmax(scores.pop(blk), has_prev, from_prev, sink_ref, bias_ref)
        if 0 <= stage - 2 < nblk:
            blk = stage - 2
            o_ref[blk * WINDOW:(blk + 1) * WINDOW, :] = _block_output(
                *probs.pop(blk), v_all[keys_of(blk)]).astype(o_ref.dtype)


def _block_scores(q, kk, lhs_ref):
    q = q.astype(F32) * (HEAD_DIM ** -0.5)
    for h in range(N_HEADS):
        lhs_ref[h * WINDOW:(h + 1) * WINDOW, :] = _head_in_group_columns(q, h).astype(BF16)
    return lax.dot_general(lhs_ref[...], kk, (((1,), (1,)), ((), ())), preferred_element_type=F32)


def _block_softmax(s, has_prev, from_prev, sink_ref, bias_ref):
    probs, sink_terms = [], []
    for h in range(N_HEADS):
        rows = slice(h * WINDOW, (h + 1) * WINDOW)
        sh = _biased_scores(jnp.where(from_prev, s[rows, :WINDOW], s[rows, WINDOW:]), bias_ref[has_prev, rows, :])
        m = jnp.maximum(jnp.max(sh, axis=-1, keepdims=True), sink_ref[h])
        p = jnp.exp(sh - m)
        probs.append(jnp.concatenate([jnp.where(from_prev, p, 0.0), jnp.where(from_prev, 0.0, p)],
                                     axis=1).astype(BF16))
        sink_terms.append(jnp.exp(sink_ref[h] - m))
    return probs, sink_terms


def _block_output(probs, sink_terms, vv):
    assert 2 * HEAD_DIM == LANES and GROUP % 2 == 0
    ones = jnp.ones((vv.shape[0], HEAD_DIM), BF16)
    low_half = lax.broadcasted_iota(jnp.int32, (WINDOW, LANES), 1) < HEAD_DIM
    outs = []
    for g in range(N_KV_HEADS):
        vg = vv[:, g * HEAD_DIM:(g + 1) * HEAD_DIM]
        even_heads = [g * GROUP + r for r in range(0, GROUP, 2)]
        pe = jnp.concatenate([probs[h] for h in even_heads], axis=0)
        po = jnp.concatenate([probs[h + 1] for h in even_heads], axis=0)
        oe = jnp.dot(pe, jnp.concatenate([vg, ones], axis=1), preferred_element_type=F32)
        oo = jnp.dot(po, jnp.concatenate([ones, vg], axis=1), preferred_element_type=F32)
        for k, h in enumerate(even_heads):
            a = oe[k * WINDOW:(k + 1) * WINDOW]
            b = oo[k * WINDOW:(k + 1) * WINDOW]
            out_pair = jnp.where(low_half, a, b)
            sum_pair = pltpu.roll(jnp.where(low_half, b, a), HEAD_DIM, 1)
            sink_pair = jnp.where(low_half, sink_terms[h], sink_terms[h + 1])
            outs.append(out_pair * (1.0 / (sum_pair + sink_pair)))
    return jnp.concatenate(outs, axis=1)


def _in_proj_sample_attn_prompt(x, g, w, side0, side1, zp, rel_bias, sinks, tn, batch, seq):
    tm = x.shape[0]
    n = w.shape[1]
    gate_start = n - 2 * D_MODEL
    assert gate_start % tn == 0 and n % tn == 0
    gate_block, nsteps = gate_start // tn, n // tn
    bps = PROMPT_BLOCKS_PER_STEP
    assert seq % (bps * WINDOW) == 0
    steps_per_seq = seq // (bps * WINDOW)
    attn_steps = batch * steps_per_seq
    assert attn_steps <= nsteps

    def astep(j):
        return jnp.minimum(j, attn_steps - 1)

    def cur(off, width):
        return lambda j: (astep(j), off // width)

    def prev(off, width):
        return lambda j: (astep(j) * bps - jnp.minimum(astep(j) % steps_per_seq, 1), off // width)

    def side_spec(side):
        rows = _side_cast_rows(side.shape[0], nsteps)
        return pl.BlockSpec((rows, side.shape[1]), lambda j: (jnp.minimum(j, side.shape[0] // rows - 1), 0))

    smem = pl.BlockSpec(memory_space=pltpu.SMEM)
    return pl.pallas_call(
        functools.partial(_inproj_attn_kernel, steps_per_seq=steps_per_seq),
        grid=(nsteps,),
        in_specs=[
            pl.BlockSpec((tm, D_MODEL), lambda j: (0, 0)),
            pl.BlockSpec((1, D_MODEL),lambda j: (0, 0)),
            pl.BlockSpec((D_MODEL, tn), lambda j: (0, (j + gate_block) % nsteps)),
            side_spec(side0),
            side_spec(side1),
            smem,
            smem,
            pl.BlockSpec((bps * WINDOW, D_ATTN), cur(OFF_Q, D_ATTN)),
            pl.BlockSpec((WINDOW, D_KV), prev(OFF_K, D_KV)),
            pl.BlockSpec((bps * WINDOW, D_KV), cur(OFF_K, D_KV)),
            pl.BlockSpec((WINDOW, D_KV), prev(OFF_V, D_KV)),
            pl.BlockSpec((bps * WINDOW, D_KV), cur(OFF_V, D_KV)),
        ],
        out_specs=[
            pl.BlockSpec((tm, tn), lambda j: (0, j)),
            pl.BlockSpec((bps * WINDOW, D_ATTN), lambda j: (j, 0)),
            side_spec(side0),
            side_spec(side1),
        ],
        out_shape=[
            jax.ShapeDtypeStruct((tm, n), BF16),
            jax.ShapeDtypeStruct((nsteps * bps * WINDOW, D_ATTN), BF16),
            jax.ShapeDtypeStruct(side0.shape, BF16),
            jax.ShapeDtypeStruct(side1.shape, BF16),
        ],
        scratch_shapes=[pltpu.VMEM((tm, D_MODEL), BF16),
                        pltpu.VMEM((D_MODEL, tn), BF16),
                        pltpu.VMEM((2, N_HEADS * WINDOW, WINDOW), F32),
                        pltpu.VMEM((bps, N_HEADS * WINDOW, D_KV), BF16)],
        compiler_params=_params("arbitrary"),
        name="in_proj_sample_attn_prompt",
    )(x, g, w, side0, side1, rel_bias, sinks, zp, zp, zp, zp, zp)


SAMPLE_BATCH_TILE = 16


def _attn_sample_kernel(relb_ref, sink_ref, q_ref, kn_ref, vn_ref, ckt_ref, cvt_ref,
                        o_ref, nk_ref, nv_ref,
                        bias1_ref, bias2_ref, sinkv_ref, z_ref, s_ref, p_ref, oh_ref, *, steps):
    bt, _, wc = nk_ref.shape
    tile_rows = bt * steps

    @pl.when(pl.program_id(0) == 0)
    def _():
        t = lax.broadcasted_iota(jnp.int32, (steps, wc), 0)
        j = lax.broadcasted_iota(jnp.int32, (steps, wc), 1)
        t2 = lax.broadcasted_iota(jnp.int32, (steps, tile_rows), 0)
        j2 = lax.broadcasted_iota(jnp.int32, (steps, tile_rows), 1)
        own = jnp.where(j2 < steps, t2 - j2, -1)
        for h in range(N_HEADS):
            sl = slice(h * steps, (h + 1) * steps)
            bias1_ref[sl, :] = _bias_from_distance(t + wc - j, h, relb_ref)
            bias2_ref[0, sl, :] = _bias_from_distance(own, h, relb_ref)
            sinkv_ref[sl, :] = jnp.full((steps, LANES), sink_ref[h], F32)
        for b in range(1, bt):
            bias2_ref[b] = pltpu.roll(bias2_ref[0], b * steps, 1)

    kn = kn_ref[...]
    vn = vn_ref[...]
    knt = kn.astype(F32).T.astype(BF16)
    vnt = vn.astype(F32).T.astype(BF16)
    new_lanes = lax.broadcasted_iota(jnp.int32, (D_KV, wc), 1) >= wc - steps
    row = lax.broadcasted_iota(jnp.int32, (tile_rows, wc), 0)
    lane = lax.broadcasted_iota(jnp.int32, (tile_rows, wc), 1)
    first_row = jnp.where(lane >= wc - steps, row - (lane - (wc - steps)), -1)

    def slid(cache_t, new_t, b):
        place = jnp.where(first_row == b * steps, 1.0, 0.0).astype(BF16)
        new = jnp.dot(new_t, place, preferred_element_type=F32)
        return jnp.where(new_lanes, new, pltpu.roll(cache_t, wc - steps, 1))

    q = q_ref[...].astype(F32) * (HEAD_DIM ** -0.5)
    for h in range(N_HEADS):
        z_ref[:, h * steps:(h + 1) * steps, :] = _head_in_group_columns(q, h).reshape(bt, steps, D_KV)

    for b in range(bt):
        ckt = ckt_ref[b]
        nk_ref[b] = slid(ckt, knt, b)
        keys_t = jnp.concatenate([ckt.astype(BF16), knt], axis=1)
        s_ref[b] = jnp.dot(z_ref[b].astype(BF16), keys_t, preferred_element_type=F32)

    bias = jnp.concatenate([jnp.broadcast_to(bias1_ref[...], (bt,) + bias1_ref.shape), bias2_ref[...]], axis=2)
    s = _biased_scores(s_ref[...], bias)
    sink = sinkv_ref[...][None]
    m = jnp.maximum(jnp.max(s, axis=-1, keepdims=True), sink)
    p_ref[...] = jnp.exp(s - jnp.concatenate([m] * (s.shape[-1] // LANES), axis=-1)).astype(BF16)
    sink_term = jnp.exp(sink - m)

    vnt_ones = jnp.concatenate([vnt, jnp.ones((LANES, tile_rows), BF16)], axis=0)
    for b in range(bt):
        cvt = cvt_ref[b]
        nv_ref[b] = slid(cvt, vnt, b)
        cvt_ones = jnp.concatenate([cvt.astype(BF16), jnp.ones((LANES, wc), BF16)], axis=0)
        oh_ref[b] = lax.dot_general(p_ref[b], jnp.concatenate([cvt_ones, vnt_ones], axis=1),
                                    (((1,), (1,)), ((), ())), preferred_element_type=F32)

    inv = 1.0 / (oh_ref[:, :, D_KV:] + sink_term)
    outs = []
    for h in range(N_HEADS):
        rows = slice(h * steps, (h + 1) * steps)
        outs.append((oh_ref[:, rows, _group_columns(h)] * inv[:, rows, :HEAD_DIM]).reshape(tile_rows, HEAD_DIM))
    o_ref[...] = jnp.concatenate(outs, axis=1).astype(o_ref.dtype)


def _attn_sample(z, cache_kt, cache_vt, rel_bias, sinks, batch, steps):
    wc = cache_kt.shape[2]
    bt = SAMPLE_BATCH_TILE
    rows = bt * steps
    smem = pl.BlockSpec(memory_space=pltpu.SMEM)
    cache_t_spec = pl.BlockSpec((bt, D_KV, wc), lambda i: (i, 0, 0))
    return pl.pallas_call(
        functools.partial(_attn_sample_kernel, steps=steps),
        grid=(batch // bt,),
        in_specs=[
            smem,
            smem,
            pl.BlockSpec((rows, D_ATTN), lambda i: (i, OFF_Q // D_ATTN)),
            pl.BlockSpec((rows, D_KV), lambda i: (i, OFF_K // D_KV)),
            pl.BlockSpec((rows, D_KV), lambda i: (i, OFF_V // D_KV)),
            cache_t_spec,
            cache_t_spec,
        ],
        out_specs=[
            pl.BlockSpec((rows, D_ATTN), lambda i: (i, 0)),
            cache_t_spec,
            cache_t_spec,
        ],
        out_shape=[
            jax.ShapeDtypeStruct((batch * steps, D_ATTN), BF16),
            jax.ShapeDtypeStruct((batch, D_KV, wc), F32),
            jax.ShapeDtypeStruct((batch, D_KV, wc), F32),
        ],
        scratch_shapes=[
            pltpu.VMEM((N_HEADS * steps, wc), F32),
            pltpu.VMEM((bt, N_HEADS * steps, rows), F32),
            pltpu.VMEM((N_HEADS * steps, LANES), F32),
            pltpu.VMEM((bt, N_HEADS * steps, D_KV), F32),
            pltpu.VMEM((bt, N_HEADS * steps, wc + rows), F32),
            pltpu.VMEM((bt, N_HEADS * steps, wc + rows), BF16),
            pltpu.VMEM((bt, N_HEADS * steps, D_KV + LANES), F32),
        ],
        compiler_params=_params("arbitrary"),
        name="attn_sample",
    )(rel_bias, sinks, z, z, z, cache_kt, cache_vt)


CARRY_ROWS = 8


def _gated_conv(b, wc_ref, u, u1, u2):
    conv = wc_ref[0:1, :] * u2 + wc_ref[1:2, :] * u1 + wc_ref[2:3, :] * u
    return (b.astype(F32) * conv).astype(BF16)


def _mix_tail(conv_branch_input, at_ref, gc_ref, ga_ref, x_ref, g_ref, wco_ref, wao_ref, wo_ref, h_ref, hn_ref):
    y_conv = jnp.dot(conv_branch_input(), wco_ref[...], preferred_element_type=F32)
    y_attn = jnp.dot(at_ref[...], wao_ref[...], preferred_element_type=F32)
    merged = (jax.nn.sigmoid(gc_ref[...].astype(F32)) * y_conv
              + jax.nn.sigmoid(ga_ref[...].astype(F32)) * y_attn)
    h = x_ref[...] + jnp.dot(merged.astype(BF16), wo_ref[...], preferred_element_type=F32)
    h_ref[...] = h
    hn_ref[...] = _rms_rows(h, g_ref[0:1, :]).astype(hn_ref.dtype)


def _mix_prompt_kernel(b_ref, c_ref, hc_ref, wc_ref, at_ref, gc_ref, ga_ref, x_ref, g_ref,
                       wco_ref, wao_ref, wo_ref, side0_ref, side1_ref, side2_ref,
                       h_ref, hn_ref, st_ref, side0_bf16_ref, side1_bf16_ref, side2_bf16_ref,
                       us_ref, *, tiles_per_seq):
    tm = x_ref.shape[0]
    side0_bf16_ref[...] = side0_ref[...].astype(BF16)
    side1_bf16_ref[...] = side1_ref[...].astype(BF16)
    side2_bf16_ref[...] = side2_ref[...].astype(BF16)

    @pl.when(pl.program_id(0) % tiles_per_seq == 0)
    def _():
        us_ref[...] = jnp.zeros((CARRY_ROWS, D_CONV), F32)

    def conv_branch_input():
        u = c_ref[...].astype(F32) * hc_ref[...].astype(F32)
        st_ref[...] = u[tm - (CONV_WIDTH - 1):, :]
        u_ext = jnp.concatenate([us_ref[...], u], axis=0)
        us_ref[...] = u[tm - CARRY_ROWS:, :]
        return _gated_conv(b_ref[...], wc_ref, u,
                           pltpu.roll(u_ext, 1, 0)[CARRY_ROWS:, :], pltpu.roll(u_ext, 2, 0)[CARRY_ROWS:, :])

    _mix_tail(conv_branch_input, at_ref, gc_ref, ga_ref, x_ref, g_ref, wco_ref, wao_ref, wo_ref, h_ref, hn_ref)


def _mix_sample_kernel(b_ref, c_ref, hc_ref, st_ref, wc_ref, at_ref, gc_ref, ga_ref, x_ref, g_ref,
                       wco_ref, wao_ref, wo_ref, h_ref, hn_ref, new_st_ref, us_ref, *, steps):
    tm = x_ref.shape[0]
    nseq = tm // steps
    keep = CONV_WIDTH - 1

    def conv_branch_input():
        u = c_ref[...].astype(F32) * hc_ref[...].astype(F32)
        us_ref[:, CARRY_ROWS - keep:CARRY_ROWS, :] = st_ref[...]
        us_ref[:, CARRY_ROWS:, :] = u.reshape(nseq, steps, D_CONV)
        new_st_ref[...] = us_ref[:, CARRY_ROWS + steps - keep:, :]

        def delayed(d):
            return us_ref[:, CARRY_ROWS - d:CARRY_ROWS - d + steps, :].reshape(tm, D_CONV)

        return _gated_conv(b_ref[...], wc_ref, u, delayed(1), delayed(2))

    _mix_tail(conv_branch_input, at_ref, gc_ref, ga_ref, x_ref, g_ref, wco_ref, wao_ref, wo_ref, h_ref, hn_ref)


def _mix_specs(tm):
    resident = functools.partial(pl.BlockSpec, index_map=lambda i: (0, 0), pipeline_mode=pl.Buffered(1))
    row_spec = pl.BlockSpec((tm, D_MODEL), lambda i: (i, 0))

    def zcol(off, width):
        return pl.BlockSpec((tm, width), lambda i: (i, off // width))

    conv_in = [zcol(OFF_B, D_CONV), zcol(OFF_C, D_CONV), zcol(OFF_H, D_CONV)]
    rest_in = [
        pl.BlockSpec((CONV_WIDTH, D_CONV), lambda i: (0, 0)),
        pl.BlockSpec((tm, D_ATTN), lambda i: (i, 0)),
        zcol(OFF_GC, D_MODEL),
        zcol(OFF_GA, D_MODEL),
        row_spec,
        pl.BlockSpec((1, D_MODEL),lambda i: (0, 0)),
        resident((D_CONV, D_MODEL)),
        resident((D_ATTN, D_MODEL)),
        resident((D_MODEL, D_MODEL)),
    ]
    return conv_in, rest_in, row_spec


def _mix_prompt(z, attn, x, w_conv, g_ffn, wco, wao, wo, sides, tm, batch, seq):
    m = x.shape[0]
    assert seq % tm == 0 and tm >= CARRY_ROWS
    tiles_per_seq = seq // tm
    nsteps = m // tm
    conv_in, rest_in, row_spec = _mix_specs(tm)

    def side_spec(side):
        rows = _side_cast_rows(side.shape[0], nsteps)
        return pl.BlockSpec((rows, side.shape[1]), lambda i: (jnp.minimum(i, side.shape[0] // rows - 1), 0))

    side_specs = [side_spec(s) for s in sides]
    return pl.pallas_call(
        functools.partial(_mix_prompt_kernel, tiles_per_seq=tiles_per_seq),
        grid=(nsteps,),
        in_specs=conv_in + rest_in + side_specs,
        out_specs=[row_spec, row_spec,
                   pl.BlockSpec((None, CONV_WIDTH - 1, D_CONV), lambda i: (i // tiles_per_seq, 0, 0))] + side_specs,
        out_shape=[jax.ShapeDtypeStruct((m, D_MODEL), F32), jax.ShapeDtypeStruct((m, D_MODEL), BF16),
                   jax.ShapeDtypeStruct((batch, CONV_WIDTH - 1, D_CONV), F32)]
        + [jax.ShapeDtypeStruct(s.shape, BF16) for s in sides],
        scratch_shapes=[pltpu.VMEM((CARRY_ROWS, D_CONV), F32)],
        compiler_params=_params("arbitrary"),
        name="mix_prompt",
    )(z, z, z, w_conv, attn, z, z, x, g_ffn, wco, wao, wo, *sides)


def _mix_sample(z, state, attn, x, w_conv, g_ffn, wco, wao, wo, tm, steps):
    m = x.shape[0]
    assert tm % steps == 0 and CONV_WIDTH - 1 <= min(steps, CARRY_ROWS)
    conv_in, rest_in, row_spec = _mix_specs(tm)
    state_spec = pl.BlockSpec((tm // steps, CONV_WIDTH - 1, D_CONV), lambda i: (i, 0, 0))
    return pl.pallas_call(
        functools.partial(_mix_sample_kernel, steps=steps),
        grid=(m // tm,),
        in_specs=conv_in + [state_spec] + rest_in,
        out_specs=[row_spec, row_spec, state_spec],
        out_shape=[jax.ShapeDtypeStruct((m, D_MODEL), F32), jax.ShapeDtypeStruct((m, D_MODEL), BF16),
                   jax.ShapeDtypeStruct(state.shape, F32)],
        scratch_shapes=[pltpu.VMEM((tm // steps, CARRY_ROWS + steps, D_CONV), F32)],
        compiler_params=_params("arbitrary"),
        name="mix_sample",
    )(z, z, z, state, w_conv, attn, z, z, x, g_ffn, wco, wao, wo)


def _ffn_kernel(hn_ref, h_ref, gf_ref, wg_ref, wu_ref, wd_ref, y_ref, *, final_norm):
    j = pl.program_id(1)

    @pl.when(j == 0)
    def _():
        y_ref[...] = h_ref[...]

    hn = hn_ref[...]
    a = (jax.nn.silu(jnp.dot(hn, wg_ref[...], preferred_element_type=F32))
         * jnp.dot(hn, wu_ref[...], preferred_element_type=F32))
    y_ref[...] += jnp.dot(a.astype(BF16), wd_ref[...], preferred_element_type=F32)

    if final_norm:
        @pl.when(j == pl.num_programs(1) - 1)
        def _():
            def body(r, carry):
                rows = pl.ds(pl.multiple_of(r * NORM_ROWS, NORM_ROWS), NORM_ROWS)
                y_ref[rows, :] = _rms_rows(y_ref[rows, :], gf_ref[0:1, :])
                return carry

            lax.fori_loop(0, y_ref.shape[0] // NORM_ROWS, body, 0)


def _ffn(hn, h, g_final, wg, wu, wd, tm, tf, final_norm):
    m = h.shape[0]
    return pl.pallas_call(
        functools.partial(_ffn_kernel, final_norm=final_norm),
        grid=(m // tm, D_FF // tf),
        in_specs=[
            pl.BlockSpec((tm, D_MODEL), lambda i, j: (i, 0)),
            pl.BlockSpec((tm, D_MODEL), lambda i, j: (i, 0)),
            pl.BlockSpec((1, D_MODEL),lambda i, j: (0, 0)),
            pl.BlockSpec((D_MODEL, tf), lambda i, j: (0, j)),
            pl.BlockSpec((D_MODEL, tf), lambda i, j: (0, j)),
            pl.BlockSpec((tf, D_MODEL), lambda i, j: (j, 0)),
        ],
        out_specs=pl.BlockSpec((tm, D_MODEL), lambda i, j: (i, 0)),
        out_shape=jax.ShapeDtypeStruct((m, D_MODEL), F32),
        compiler_params=_params("arbitrary", "arbitrary"),
        name="ffn",
    )(hn, h, g_final, wg, wu, wd)


IN_PROJ_TM_PROMPT = 2048
IN_PROJ_TN = 512
MIX_TM = 256
FFN_TM = 1024
FFN_TF = 512


def kernel(x_prompt, x_sample, cache_k, cache_v, state_conv, rel_bias, w_in, w_conv, w_conv_out, sinks,
           w_attn_out, w_o, g_mix, g_ffn, w_gate, w_up, w_down, g_final):
    depth = w_in.shape[0]
    batch, seq, _ = x_prompt.shape
    dec_batch, steps, _ = x_sample.shape
    wc = cache_k.shape[2]
    assert seq % WINDOW == 0 and wc == WINDOW and dec_batch % SAMPLE_BATCH_TILE == 0

    hp = x_prompt.reshape(batch * seq, D_MODEL)
    hs = x_sample.reshape(dec_batch * steps, D_MODEL)
    g_final2 = g_final.reshape(1, D_MODEL)
    kp_l, vp_l, cp_l, ks_l, vs_l, cs_l = [], [], [], [], [], []
    for l in range(depth):
        final = l == depth - 1
        lw = {
            "w_in": w_in[l],
            "g_ffn": g_ffn[l].reshape(1, D_MODEL),
        }
        g_mix_l = g_mix[l].reshape(1, D_MODEL)

        zp, lw["wo"], kp_t, vp_t = _in_proj(hp, g_mix_l, lw["w_in"], w_o[l], IN_PROJ_TM_PROMPT, IN_PROJ_TN, seq)
        zs, attn_p, lw["wco"], lw["wao"] = _in_proj_sample_attn_prompt(
            hs, g_mix_l, lw["w_in"], w_conv_out[l], w_attn_out[l], zp, rel_bias, sinks[l], IN_PROJ_TN, batch, seq)
        hp, hnp, conv_p, lw["wg"], lw["wu"], lw["wd"] = _mix_prompt(
            zp, attn_p, hp, w_conv[l], lw["g_ffn"], lw["wco"], lw["wao"], lw["wo"],
            (w_gate[l], w_up[l], w_down[l]), MIX_TM, batch, seq)
        hp = _ffn(hnp, hp, g_final2, lw["wg"], lw["wu"], lw["wd"], FFN_TM, FFN_TF, final)
        kp_l.append(jnp.transpose(kp_t.reshape(batch, N_KV_HEADS, HEAD_DIM, WINDOW), (0, 3, 1, 2)))
        vp_l.append(jnp.transpose(vp_t.reshape(batch, N_KV_HEADS, HEAD_DIM, WINDOW), (0, 3, 1, 2)))
        cp_l.append(conv_p)

        cache_kt = jnp.transpose(cache_k[l], (0, 2, 3, 1)).reshape(dec_batch, D_KV, wc)
        cache_vt = jnp.transpose(cache_v[l], (0, 2, 3, 1)).reshape(dec_batch, D_KV, wc)
        attn_s, nk, nv = _attn_sample(zs, cache_kt, cache_vt, rel_bias, sinks[l], dec_batch, steps)
        hs, hns, conv_s = _mix_sample(zs, state_conv[l], attn_s, hs, w_conv[l], lw["g_ffn"], lw["wco"], lw["wao"],
                                      lw["wo"], MIX_TM, steps)
        hs = _ffn(hns, hs, g_final2, lw["wg"], lw["wu"], lw["wd"], FFN_TM, FFN_TF, final)
        ks_l.append(jnp.transpose(nk.reshape(dec_batch, N_KV_HEADS, HEAD_DIM, wc), (0, 3, 1, 2)))
        vs_l.append(jnp.transpose(nv.reshape(dec_batch, N_KV_HEADS, HEAD_DIM, wc), (0, 3, 1, 2)))
        cs_l.append(conv_s)

    return (hp.reshape(batch, seq, D_MODEL), hs.reshape(dec_batch, steps, D_MODEL),
            jnp.stack(kp_l), jnp.stack(vp_l), jnp.stack(cp_l),
            jnp.stack(ks_l), jnp.stack(vs_l), jnp.stack(cs_l))
```

```python
import functools
import math

import numpy as np
import jax
import jax.numpy as jnp
from jax import lax
from jax.experimental import pallas as pl
from jax.experimental.pallas import tpu as pltpu

D_MODEL = 2048
D_CONV = D_MODEL // 2
CONV_WIDTH = 3
HEAD_DIM = 64
N_HEADS = (D_MODEL // 2) // HEAD_DIM
N_KV_HEADS = N_HEADS // 4
GROUP = N_HEADS // N_KV_HEADS
D_ATTN = N_HEADS * HEAD_DIM
D_KV = N_KV_HEADS * HEAD_DIM
WINDOW = 128
NUM_BUCKETS = 32
MAX_DISTANCE = 128
D_FF = -(-8 * D_MODEL // (3 * 256)) * 256
EPS = 1e-6
D_IN_PROJ = 3 * D_CONV + D_ATTN + 2 * D_KV + 2 * D_MODEL

OFF_GC = 0
OFF_GA = D_MODEL
OFF_B = 2 * D_MODEL
OFF_C = OFF_B + D_CONV
OFF_H = OFF_C + D_CONV
OFF_Q = OFF_H + D_CONV
OFF_K = OFF_Q + D_ATTN
OFF_V = OFF_K + D_KV

LANES = 128
NEG = -1e30
F32 = jnp.float32
BF16 = jnp.bfloat16
VMEM_LIMIT = 60 * 1024 * 1024


def _bucket_thresholds():
    max_exact = NUM_BUCKETS // 2
    d = np.arange(MAX_DISTANCE)
    ratio = np.log(np.maximum(d, 1).astype(np.float32) / np.float32(max_exact)) / np.float32(
        math.log(MAX_DISTANCE / max_exact))
    large = max_exact + (ratio * np.float32(NUM_BUCKETS - max_exact)).astype(np.int32)
    large = np.minimum(large, NUM_BUCKETS - 1)
    return [int(np.min(d[(d >= max_exact) & (large >= b)])) for b in range(max_exact + 1, NUM_BUCKETS)]


BUCKET_THRESHOLDS = _bucket_thresholds()


def _params(*semantics):
    return pltpu.CompilerParams(dimension_semantics=semantics, vmem_limit_bytes=VMEM_LIMIT)


def _rms_rows(x, g):
    ms = jnp.mean(x * x, axis=-1, keepdims=True)
    return x * lax.rsqrt(ms + EPS) * g


NORM_ROWS = 256


def _inproj_kernel(x_ref, g_ref, w_ref, side_ref, z_ref, side_bf16_ref, xn_ref):
    _inproj_norm(pl.program_id(1) == 0, x_ref, g_ref, xn_ref)
    _inproj_columns(w_ref, z_ref, xn_ref)
    side_bf16_ref[...] = side_ref[...].astype(BF16)


BF16_ROWS = 16


def _side_cast_rows(rows, nsteps):
    per_step = -(-rows // nsteps)
    per_step = -(-per_step // BF16_ROWS) * BF16_ROWS
    assert rows % per_step == 0
    return per_step


def _inproj_norm(first_column_step, x_ref, g_ref, xn_ref):
    @pl.when(first_column_step)
    def _():
        def body(r, carry):
            rows = pl.ds(pl.multiple_of(r * NORM_ROWS, NORM_ROWS), NORM_ROWS)
            xn_ref[rows, :] = _rms_rows(x_ref[rows, :], g_ref[0:1, :]).astype(xn_ref.dtype)
            return carry

        lax.fori_loop(0, x_ref.shape[0] // NORM_ROWS, body, 0)


def _inproj_columns(w_ref, z_ref, xn_ref):
    z_ref[...] = jnp.dot(xn_ref[...], w_ref[...].astype(BF16), preferred_element_type=F32).astype(z_ref.dtype)


def _in_proj(x, g, w, side, tm, tn):
    m = x.shape[0]
    n = w.shape[1]
    gate_start = n - 2 * D_MODEL
    assert gate_start % tn == 0 and n % tn == 0
    gate_block, nblocks = gate_start // tn, n // tn
    side_rows = _side_cast_rows(side.shape[0], (m // tm) * nblocks)
    side_spec = pl.BlockSpec((side_rows, side.shape[1]),
                             lambda i, j: (jnp.minimum(i * nblocks + j, side.shape[0] // side_rows - 1), 0))
    return pl.pallas_call(
        _inproj_kernel,
        grid=(m // tm, nblocks),
        in_specs=[
            pl.BlockSpec((tm, D_MODEL), lambda i, j: (i, 0)),
            pl.BlockSpec((1, D_MODEL),lambda i, j: (0, 0)),
            pl.BlockSpec((D_MODEL, tn), lambda i, j: (0, (j + gate_block) % nblocks)),
            side_spec,
        ],
        out_specs=[pl.BlockSpec((tm, tn), lambda i, j: (i, j)), side_spec],
        out_shape=[jax.ShapeDtypeStruct((m, n), BF16), jax.ShapeDtypeStruct(side.shape, BF16)],
        scratch_shapes=[pltpu.VMEM((tm, D_MODEL), BF16)],
        compiler_params=_params("arbitrary", "arbitrary"),
        name="in_proj",
    )(x, g, w, side)


def _bias_from_distance(dist, head, relb_ref):
    valid = (dist >= 0) & (dist < WINDOW)
    d = jnp.clip(dist, 0, MAX_DISTANCE - 1)
    bucket = jnp.minimum(d, NUM_BUCKETS // 2)
    for thr in BUCKET_THRESHOLDS:
        bucket = bucket + (d >= thr).astype(jnp.int32)
    bias = jnp.zeros(dist.shape, F32)
    for b in range(NUM_BUCKETS):
        bias = jnp.where(bucket == b, relb_ref[b, head], bias)
    return jnp.where(valid, bias, NEG)


def _biased_scores(s, bias):
    return jnp.where(bias > 0.5 * NEG, s + bias, NEG)


def _head_in_group_columns(q, h):
    half = D_KV // 2
    assert 2 * HEAD_DIM == half and GROUP * HEAD_DIM == D_KV
    g = h // GROUP
    src = q[:, (h // 2) * half:(h // 2 + 1) * half]
    if h % 2 != g % 2:
        src = pltpu.roll(src, HEAD_DIM, 1)
    lane_half = lax.broadcasted_iota(jnp.int32, src.shape, 1) // HEAD_DIM
    placed = jnp.where(lane_half == g % 2, src, 0.0)
    zero = jnp.zeros_like(placed)
    return jnp.concatenate([placed, zero] if g // 2 == 0 else [zero, placed], axis=1)


def _group_columns(h):
    g = h // GROUP
    return slice(g * HEAD_DIM, (g + 1) * HEAD_DIM)


INPROJ_FILL_CHUNKS = 4
PROMPT_BLOCKS_PER_STEP = 4


def _inproj_attn_kernel(x_ref, g_ref, w_ref, side0_ref, side1_ref, relb_ref, sink_ref, q_ref, kp_ref, kc_ref,
                        vp_ref, vc_ref, z_ref, o_ref, side0_bf16_ref, side1_bf16_ref, kst_ref, vst_ref,
                        xn_ref, wb_ref, bias_ref, lhs_ref, *, steps_per_seq):
    j = pl.program_id(0)
    _attn_prompt_bias(j == 0, relb_ref, bias_ref)
    _inproj_norm(j == 0, x_ref, g_ref, xn_ref)
    wb_ref[...] = w_ref[...].astype(BF16)
    side0_bf16_ref[...] = side0_ref[...].astype(BF16)
    side1_bf16_ref[...] = side1_ref[...].astype(BF16)
    chunk = xn_ref.shape[0] // INPROJ_FILL_CHUNKS

    def column_chunk(c):
        rows = slice(c * chunk, (c + 1) * chunk)

        def emit():
            z_ref[rows, :] = jnp.dot(xn_ref[rows, :], wb_ref[...], preferred_element_type=F32).astype(z_ref.dtype)
        return emit

    _attn_prompt_blocks(j % steps_per_seq != 0, sink_ref, q_ref, kp_ref, kc_ref, vp_ref, vc_ref,
                        o_ref, bias_ref, lhs_ref, fillers=[column_chunk(c) for c in range(INPROJ_FILL_CHUNKS)])
    @pl.when(j % steps_per_seq == steps_per_seq - 1)
    def _():
        last = kc_ref.shape[0] - WINDOW
        kst_ref[0] = kc_ref[last:, :].astype(F32).T
        vst_ref[0] = vc_ref[last:, :].astype(F32).T


def _from_prev_mask():
    qi = lax.broadcasted_iota(jnp.int32, (WINDOW, WINDOW), 0)
    kj = lax.broadcasted_iota(jnp.int32, (WINDOW, WINDOW), 1)
    return qi, kj, kj > qi


def _attn_prompt_bias(first_step, relb_ref, bias_ref):
    @pl.when(first_step)
    def _():
        qi, kj, from_prev = _from_prev_mask()
        dist = jnp.where(from_prev, qi + WINDOW - kj, qi - kj)
        for h in range(N_HEADS):
            rows = slice(h * WINDOW, (h + 1) * WINDOW)
            bias = _bias_from_distance(dist, h, relb_ref)
            bias_ref[1, rows, :] = bias
            bias_ref[0, rows, :] = jnp.where(from_prev, NEG, bias)


def _attn_prompt_blocks(has_prev_block, sink_ref, q_ref, kp_ref, kc_ref, vp_ref, vc_ref, o_ref, bias_ref, lhs_ref,
                        fillers=()):
    _, _, from_prev = _from_prev_mask()
    k_all = jnp.concatenate([kp_ref[...], kc_ref[...]], axis=0)
    v_all = jnp.concatenate([vp_ref[...], vc_ref[...]], axis=0)

    def keys_of(blk):
        return slice(blk * WINDOW, (blk + 2) * WINDOW)

    nblk = PROMPT_BLOCKS_PER_STEP
    fillers = list(fillers)
    per_stage = -(-len(fillers) // (nblk + 2))
    scores, probs = {}, {}
    for stage in range(nblk + 2):
        if stage < nblk:
            q = q_ref[stage * WINDOW:(stage + 1) * WINDOW, :]
            scores[stage] = _block_scores(q, k_all[keys_of(stage)], lhs_ref.at[stage])
        for _ in range(min(per_stage, len(fillers))):
            fillers.pop(0)()
        if 0 <= stage - 1 < nblk:
            blk = stage - 1
            has_prev = has_prev_block.astype(jnp.int32) if blk == 0 else 1
            probs[blk] = _block_softmax(scores.pop(blk), has_prev, from_prev, sink_ref, bias_ref)
        if 0 <= stage - 2 < nblk:
            blk = stage - 2
            o_ref[blk * WINDOW:(blk + 1) * WINDOW, :] = _block_output(
                *probs.pop(blk), v_all[keys_of(blk)]).astype(o_ref.dtype)


def _block_scores(q, kk, lhs_ref):
    q = q.astype(F32) * (HEAD_DIM ** -0.5)
    for h in range(N_HEADS):
        lhs_ref[h * WINDOW:(h + 1) * WINDOW, :] = _head_in_group_columns(q, h).astype(BF16)
    return lax.dot_general(lhs_ref[...], kk, (((1,), (1,)), ((), ())), preferred_element_type=F32)


def _block_softmax(s, has_prev, from_prev, sink_ref, bias_ref):
    probs, sink_terms = [], []
    for h in range(N_HEADS):
        rows = slice(h * WINDOW, (h + 1) * WINDOW)
        sh = _biased_scores(jnp.where(from_prev, s[rows, :WINDOW], s[rows, WINDOW:]), bias_ref[has_prev, rows, :])
        m = jnp.maximum(jnp.max(sh, axis=-1, keepdims=True), sink_ref[h])
        p = jnp.exp(sh - m)
        probs.append(jnp.concatenate([jnp.where(from_prev, p, 0.0), jnp.where(from_prev, 0.0, p)],
                                     axis=1).astype(BF16))
        sink_terms.append(jnp.exp(sink_ref[h] - m))
    return probs, sink_terms


def _block_output(probs, sink_terms, vv):
    assert 2 * HEAD_DIM == LANES and GROUP % 2 == 0
    ones = jnp.ones((vv.shape[0], HEAD_DIM), BF16)
    low_half = lax.broadcasted_iota(jnp.int32, (WINDOW, LANES), 1) < HEAD_DIM
    outs = []
    for g in range(N_KV_HEADS):
        vg = vv[:, g * HEAD_DIM:(g + 1) * HEAD_DIM]
        even_heads = [g * GROUP + r for r in range(0, GROUP, 2)]
        pe = jnp.concatenate([probs[h] for h in even_heads], axis=0)
        po = jnp.concatenate([probs[h + 1] for h in even_heads], axis=0)
        oe = jnp.dot(pe, jnp.concatenate([vg, ones], axis=1), preferred_element_type=F32)
        oo = jnp.dot(po, jnp.concatenate([ones, vg], axis=1), preferred_element_type=F32)
        for k, h in enumerate(even_heads):
            a = oe[k * WINDOW:(k + 1) * WINDOW]
            b = oo[k * WINDOW:(k + 1) * WINDOW]
            out_pair = jnp.where(low_half, a, b)
            sum_pair = pltpu.roll(jnp.where(low_half, b, a), HEAD_DIM, 1)
            sink_pair = jnp.where(low_half, sink_terms[h], sink_terms[h + 1])
            outs.append(out_pair * (1.0 / (sum_pair + sink_pair)))
    return jnp.concatenate(outs, axis=1)


def _in_proj_sample_attn_prompt(x, g, w, side0, side1, zp, rel_bias, sinks, tn, batch, seq):
    tm = x.shape[0]
    n = w.shape[1]
    gate_start = n - 2 * D_MODEL
    assert gate_start % tn == 0 and n % tn == 0
    gate_block, nsteps = gate_start // tn, n // tn
    bps = PROMPT_BLOCKS_PER_STEP
    assert seq % (bps * WINDOW) == 0
    steps_per_seq = seq // (bps * WINDOW)
    attn_steps = batch * steps_per_seq
    assert attn_steps <= nsteps

    def astep(j):
        return jnp.minimum(j, attn_steps - 1)

    def cur(off, width):
        return lambda j: (astep(j), off // width)

    def prev(off, width):
        return lambda j: (astep(j) * bps - jnp.minimum(astep(j) % steps_per_seq, 1), off // width)

    def side_spec(side):
        rows = _side_cast_rows(side.shape[0], nsteps)
        return pl.BlockSpec((rows, side.shape[1]), lambda j: (jnp.minimum(j, side.shape[0] // rows - 1), 0))

    smem = pl.BlockSpec(memory_space=pltpu.SMEM)
    state_spec = pl.BlockSpec((1, D_KV, WINDOW), lambda j: (astep(j) // steps_per_seq, 0, 0))
    return pl.pallas_call(
        functools.partial(_inproj_attn_kernel, steps_per_seq=steps_per_seq),
        grid=(nsteps,),
        in_specs=[
            pl.BlockSpec((tm, D_MODEL), lambda j: (0, 0)),
            pl.BlockSpec((1, D_MODEL),lambda j: (0, 0)),
            pl.BlockSpec((D_MODEL, tn), lambda j: (0, (j + gate_block) % nsteps)),
            side_spec(side0),
            side_spec(side1),
            smem,
            smem,
            pl.BlockSpec((bps * WINDOW, D_ATTN), cur(OFF_Q, D_ATTN)),
            pl.BlockSpec((WINDOW, D_KV), prev(OFF_K, D_KV)),
            pl.BlockSpec((bps * WINDOW, D_KV), cur(OFF_K, D_KV)),
            pl.BlockSpec((WINDOW, D_KV), prev(OFF_V, D_KV)),
            pl.BlockSpec((bps * WINDOW, D_KV), cur(OFF_V, D_KV)),
        ],
        out_specs=[
            pl.BlockSpec((tm, tn), lambda j: (0, j)),
            pl.BlockSpec((bps * WINDOW, D_ATTN), lambda j: (j, 0)),
            side_spec(side0),
            side_spec(side1),
            state_spec,
            state_spec,
        ],
        out_shape=[
            jax.ShapeDtypeStruct((tm, n), BF16),
            jax.ShapeDtypeStruct((nsteps * bps * WINDOW, D_ATTN), BF16),
            jax.ShapeDtypeStruct(side0.shape, BF16),
            jax.ShapeDtypeStruct(side1.shape, BF16),
            jax.ShapeDtypeStruct((batch, D_KV, WINDOW), F32),
            jax.ShapeDtypeStruct((batch, D_KV, WINDOW), F32),
        ],
        scratch_shapes=[pltpu.VMEM((tm, D_MODEL), BF16),
                        pltpu.VMEM((D_MODEL, tn), BF16),
                        pltpu.VMEM((2, N_HEADS * WINDOW, WINDOW), F32),
                        pltpu.VMEM((bps, N_HEADS * WINDOW, D_KV), BF16)],
        compiler_params=_params("arbitrary"),
        name="in_proj_sample_attn_prompt",
    )(x, g, w, side0, side1, rel_bias, sinks, zp, zp, zp, zp, zp)


SAMPLE_BATCH_TILE = 16


def _attn_sample_kernel(relb_ref, sink_ref, q_ref, kn_ref, vn_ref, ckt_ref, cvt_ref,
                        o_ref, nk_ref, nv_ref,
                        bias1_ref, bias2_ref, sinkv_ref, z_ref, s_ref, p_ref, oh_ref, *, steps):
    bt, _, wc = nk_ref.shape
    tile_rows = bt * steps

    @pl.when(pl.program_id(0) == 0)
    def _():
        t = lax.broadcasted_iota(jnp.int32, (steps, wc), 0)
        j = lax.broadcasted_iota(jnp.int32, (steps, wc), 1)
        t2 = lax.broadcasted_iota(jnp.int32, (steps, tile_rows), 0)
        j2 = lax.broadcasted_iota(jnp.int32, (steps, tile_rows), 1)
        own = jnp.where(j2 < steps, t2 - j2, -1)
        for h in range(N_HEADS):
            sl = slice(h * steps, (h + 1) * steps)
            bias1_ref[sl, :] = _bias_from_distance(t + wc - j, h, relb_ref)
            bias2_ref[0, sl, :] = _bias_from_distance(own, h, relb_ref)
            sinkv_ref[sl, :] = jnp.full((steps, LANES), sink_ref[h], F32)
        for b in range(1, bt):
            bias2_ref[b] = pltpu.roll(bias2_ref[0], b * steps, 1)

    kn = kn_ref[...]
    vn = vn_ref[...]
    knt = kn.astype(F32).T.astype(BF16)
    vnt = vn.astype(F32).T.astype(BF16)
    new_lanes = lax.broadcasted_iota(jnp.int32, (D_KV, wc), 1) >= wc - steps
    row = lax.broadcasted_iota(jnp.int32, (tile_rows, wc), 0)
    lane = lax.broadcasted_iota(jnp.int32, (tile_rows, wc), 1)
    first_row = jnp.where(lane >= wc - steps, row - (lane - (wc - steps)), -1)

    def slid(cache_t, new_t, b):
        place = jnp.where(first_row == b * steps, 1.0, 0.0).astype(BF16)
        new = jnp.dot(new_t, place, preferred_element_type=F32)
        return jnp.where(new_lanes, new, pltpu.roll(cache_t, wc - steps, 1))

    q = q_ref[...].astype(F32) * (HEAD_DIM ** -0.5)
    for h in range(N_HEADS):
        z_ref[:, h * steps:(h + 1) * steps, :] = _head_in_group_columns(q, h).reshape(bt, steps, D_KV)

    for b in range(bt):
        ckt = ckt_ref[b]
        nk_ref[b] = slid(ckt, knt, b)
        keys_t = jnp.concatenate([ckt.astype(BF16), knt], axis=1)
        s_ref[b] = jnp.dot(z_ref[b].astype(BF16), keys_t, preferred_element_type=F32)

    bias = jnp.concatenate([jnp.broadcast_to(bias1_ref[...], (bt,) + bias1_ref.shape), bias2_ref[...]], axis=2)
    s = _biased_scores(s_ref[...], bias)
    sink = sinkv_ref[...][None]
    m = jnp.maximum(jnp.max(s, axis=-1, keepdims=True), sink)
    p_ref[...] = jnp.exp(s - jnp.concatenate([m] * (s.shape[-1] // LANES), axis=-1)).astype(BF16)
    sink_term = jnp.exp(sink - m)

    vnt_ones = jnp.concatenate([vnt, jnp.ones((LANES, tile_rows), BF16)], axis=0)
    for b in range(bt):
        cvt = cvt_ref[b]
        nv_ref[b] = slid(cvt, vnt, b)
        cvt_ones = jnp.concatenate([cvt.astype(BF16), jnp.ones((LANES, wc), BF16)], axis=0)
        oh_ref[b] = lax.dot_general(p_ref[b], jnp.concatenate([cvt_ones, vnt_ones], axis=1),
                                    (((1,), (1,)), ((), ())), preferred_element_type=F32)

    inv = 1.0 / (oh_ref[:, :, D_KV:] + sink_term)
    outs = []
    for h in range(N_HEADS):
        rows = slice(h * steps, (h + 1) * steps)
        outs.append((oh_ref[:, rows, _group_columns(h)] * inv[:, rows, :HEAD_DIM]).reshape(tile_rows, HEAD_DIM))
    o_ref[...] = jnp.concatenate(outs, axis=1).astype(o_ref.dtype)


def _attn_sample(z, cache_kt, cache_vt, rel_bias, sinks, batch, steps):
    wc = cache_kt.shape[2]
    bt = SAMPLE_BATCH_TILE
    rows = bt * steps
    smem = pl.BlockSpec(memory_space=pltpu.SMEM)
    cache_t_spec = pl.BlockSpec((bt, D_KV, wc), lambda i: (i, 0, 0))
    return pl.pallas_call(
        functools.partial(_attn_sample_kernel, steps=steps),
        grid=(batch // bt,),
        in_specs=[
            smem,
            smem,
            pl.BlockSpec((rows, D_ATTN), lambda i: (i, OFF_Q // D_ATTN)),
            pl.BlockSpec((rows, D_KV), lambda i: (i, OFF_K // D_KV)),
            pl.BlockSpec((rows, D_KV), lambda i: (i, OFF_V // D_KV)),
            cache_t_spec,
            cache_t_spec,
        ],
        out_specs=[
            pl.BlockSpec((rows, D_ATTN), lambda i: (i, 0)),
            cache_t_spec,
            cache_t_spec,
        ],
        out_shape=[
            jax.ShapeDtypeStruct((batch * steps, D_ATTN), BF16),
            jax.ShapeDtypeStruct((batch, D_KV, wc), F32),
            jax.ShapeDtypeStruct((batch, D_KV, wc), F32),
        ],
        scratch_shapes=[
            pltpu.VMEM((N_HEADS * steps, wc), F32),
            pltpu.VMEM((bt, N_HEADS * steps, rows), F32),
            pltpu.VMEM((N_HEADS * steps, LANES), F32),
            pltpu.VMEM((bt, N_HEADS * steps, D_KV), F32),
            pltpu.VMEM((bt, N_HEADS * steps, wc + rows), F32),
            pltpu.VMEM((bt, N_HEADS * steps, wc + rows), BF16),
            pltpu.VMEM((bt, N_HEADS * steps, D_KV + LANES), F32),
        ],
        compiler_params=_params("arbitrary"),
        name="attn_sample",
    )(rel_bias, sinks, z, z, z, cache_kt, cache_vt)


CARRY_ROWS = 8


def _gated_conv(b, wc_ref, u, u1, u2):
    conv = wc_ref[0:1, :] * u2 + wc_ref[1:2, :] * u1 + wc_ref[2:3, :] * u
    return (b.astype(F32) * conv).astype(BF16)


def _mix_tail(conv_branch_input, at_ref, gc_ref, ga_ref, x_ref, g_ref, wco_ref, wao_ref, wo_ref, h_ref, hn_ref):
    y_conv = jnp.dot(conv_branch_input(), wco_ref[...], preferred_element_type=F32)
    y_attn = jnp.dot(at_ref[...], wao_ref[...], preferred_element_type=F32)
    merged = (jax.nn.sigmoid(gc_ref[...].astype(F32)) * y_conv
              + jax.nn.sigmoid(ga_ref[...].astype(F32)) * y_attn)
    h = x_ref[...] + jnp.dot(merged.astype(BF16), wo_ref[...], preferred_element_type=F32)
    h_ref[...] = h
    hn_ref[...] = _rms_rows(h, g_ref[0:1, :]).astype(hn_ref.dtype)


def _mix_prompt_kernel(b_ref, c_ref, hc_ref, wc_ref, at_ref, gc_ref, ga_ref, x_ref, g_ref,
                       wco_ref, wao_ref, wo_ref, side0_ref, side1_ref, side2_ref,
                       h_ref, hn_ref, st_ref, side0_bf16_ref, side1_bf16_ref, side2_bf16_ref,
                       us_ref, *, tiles_per_seq):
    tm = x_ref.shape[0]
    side0_bf16_ref[...] = side0_ref[...].astype(BF16)
    side1_bf16_ref[...] = side1_ref[...].astype(BF16)
    side2_bf16_ref[...] = side2_ref[...].astype(BF16)

    def conv_branch_input():
        u = c_ref[...].astype(F32) * hc_ref[...].astype(F32)
        st_ref[...] = u[tm - (CONV_WIDTH - 1):, :]
        carry = jnp.where(pl.program_id(0) % tiles_per_seq == 0, 0.0, us_ref[...])
        u_ext = jnp.concatenate([carry, u], axis=0)
        us_ref[...] = u[tm - CARRY_ROWS:, :]
        return _gated_conv(b_ref[...], wc_ref, u,
                           pltpu.roll(u_ext, 1, 0)[CARRY_ROWS:, :], pltpu.roll(u_ext, 2, 0)[CARRY_ROWS:, :])

    _mix_tail(conv_branch_input, at_ref, gc_ref, ga_ref, x_ref, g_ref, wco_ref, wao_ref, wo_ref, h_ref, hn_ref)


def _mix_sample_kernel(b_ref, c_ref, hc_ref, st_ref, wc_ref, at_ref, gc_ref, ga_ref, x_ref, g_ref,
                       wco_ref, wao_ref, wo_ref, h_ref, hn_ref, new_st_ref, us_ref, *, steps):
    tm = x_ref.shape[0]
    nseq = tm // steps
    keep = CONV_WIDTH - 1

    def conv_branch_input():
        u = c_ref[...].astype(F32) * hc_ref[...].astype(F32)
        us_ref[:, CARRY_ROWS - keep:CARRY_ROWS, :] = st_ref[...]
        us_ref[:, CARRY_ROWS:, :] = u.reshape(nseq, steps, D_CONV)
        new_st_ref[...] = us_ref[:, CARRY_ROWS + steps - keep:, :]

        def delayed(d):
            return us_ref[:, CARRY_ROWS - d:CARRY_ROWS - d + steps, :].reshape(tm, D_CONV)

        return _gated_conv(b_ref[...], wc_ref, u, delayed(1), delayed(2))

    _mix_tail(conv_branch_input, at_ref, gc_ref, ga_ref, x_ref, g_ref, wco_ref, wao_ref, wo_ref, h_ref, hn_ref)


def _mix_specs(tm):
    resident = functools.partial(pl.BlockSpec, index_map=lambda i: (0, 0), pipeline_mode=pl.Buffered(1))
    row_spec = pl.BlockSpec((tm, D_MODEL), lambda i: (i, 0))

    def zcol(off, width):
        return pl.BlockSpec((tm, width), lambda i: (i, off // width))

    conv_in = [zcol(OFF_B, D_CONV), zcol(OFF_C, D_CONV), zcol(OFF_H, D_CONV)]
    rest_in = [
        pl.BlockSpec((CONV_WIDTH, D_CONV), lambda i: (0, 0)),
        pl.BlockSpec((tm, D_ATTN), lambda i: (i, 0)),
        zcol(OFF_GC, D_MODEL),
        zcol(OFF_GA, D_MODEL),
        row_spec,
        pl.BlockSpec((1, D_MODEL),lambda i: (0, 0)),
        resident((D_CONV, D_MODEL)),
        resident((D_ATTN, D_MODEL)),
        resident((D_MODEL, D_MODEL)),
    ]
    return conv_in, rest_in, row_spec


def _mix_prompt(z, attn, x, w_conv, g_ffn, wco, wao, wo, sides, tm, batch, seq):
    m = x.shape[0]
    assert seq % tm == 0 and tm >= CARRY_ROWS
    tiles_per_seq = seq // tm
    nsteps = m // tm
    conv_in, rest_in, row_spec = _mix_specs(tm)

    def side_spec(side):
        rows = _side_cast_rows(side.shape[0], nsteps)
        return pl.BlockSpec((rows, side.shape[1]), lambda i: (jnp.minimum(i, side.shape[0] // rows - 1), 0))

    side_specs = [side_spec(s) for s in sides]
    return pl.pallas_call(
        functools.partial(_mix_prompt_kernel, tiles_per_seq=tiles_per_seq),
        grid=(nsteps,),
        in_specs=conv_in + rest_in + side_specs,
        out_specs=[row_spec, row_spec,
                   pl.BlockSpec((None, CONV_WIDTH - 1, D_CONV), lambda i: (i // tiles_per_seq, 0, 0))] + side_specs,
        out_shape=[jax.ShapeDtypeStruct((m, D_MODEL), F32), jax.ShapeDtypeStruct((m, D_MODEL), BF16),
                   jax.ShapeDtypeStruct((batch, CONV_WIDTH - 1, D_CONV), F32)]
        + [jax.ShapeDtypeStruct(s.shape, BF16) for s in sides],
        scratch_shapes=[pltpu.VMEM((CARRY_ROWS, D_CONV), F32)],
        compiler_params=_params("arbitrary"),
        name="mix_prompt",
    )(z, z, z, w_conv, attn, z, z, x, g_ffn, wco, wao, wo, *sides)


def _mix_sample(z, state, attn, x, w_conv, g_ffn, wco, wao, wo, tm, steps):
    m = x.shape[0]
    assert tm % steps == 0 and CONV_WIDTH - 1 <= min(steps, CARRY_ROWS)
    conv_in, rest_in, row_spec = _mix_specs(tm)
    state_spec = pl.BlockSpec((tm // steps, CONV_WIDTH - 1, D_CONV), lambda i: (i, 0, 0))
    return pl.pallas_call(
        functools.partial(_mix_sample_kernel, steps=steps),
        grid=(m // tm,),
        in_specs=conv_in + [state_spec] + rest_in,
        out_specs=[row_spec, row_spec, state_spec],
        out_shape=[jax.ShapeDtypeStruct((m, D_MODEL), F32), jax.ShapeDtypeStruct((m, D_MODEL), BF16),
                   jax.ShapeDtypeStruct(state.shape, F32)],
        scratch_shapes=[pltpu.VMEM((tm // steps, CARRY_ROWS + steps, D_CONV), F32)],
        compiler_params=_params("arbitrary"),
        name="mix_sample",
    )(z, z, z, state, w_conv, attn, z, z, x, g_ffn, wco, wao, wo)


def _ffn_kernel(hn_ref, h_ref, gf_ref, wg_ref, wu_ref, wd_ref, y_ref, *, final_norm):
    j = pl.program_id(1)

    hn = hn_ref[...]
    a = (jax.nn.silu(jnp.dot(hn, wg_ref[...], preferred_element_type=F32))
         * jnp.dot(hn, wu_ref[...], preferred_element_type=F32))
    acc = jnp.where(j == 0, h_ref[...], y_ref[...])
    y_ref[...] = acc + jnp.dot(a.astype(BF16), wd_ref[...], preferred_element_type=F32)

    if final_norm:
        @pl.when(j == pl.num_programs(1) - 1)
        def _():
            def body(r, carry):
                rows = pl.ds(pl.multiple_of(r * NORM_ROWS, NORM_ROWS), NORM_ROWS)
                y_ref[rows, :] = _rms_rows(y_ref[rows, :], gf_ref[0:1, :])
                return carry

            lax.fori_loop(0, y_ref.shape[0] // NORM_ROWS, body, 0)


def _ffn(hn, h, g_final, wg, wu, wd, tm, tf, final_norm):
    m = h.shape[0]
    return pl.pallas_call(
        functools.partial(_ffn_kernel, final_norm=final_norm),
        grid=(m // tm, D_FF // tf),
        in_specs=[
            pl.BlockSpec((tm, D_MODEL), lambda i, j: (i, 0)),
            pl.BlockSpec((tm, D_MODEL), lambda i, j: (i, 0)),
            pl.BlockSpec((1, D_MODEL),lambda i, j: (0, 0)),
            pl.BlockSpec((D_MODEL, tf), lambda i, j: (0, j)),
            pl.BlockSpec((D_MODEL, tf), lambda i, j: (0, j)),
            pl.BlockSpec((tf, D_MODEL), lambda i, j: (j, 0)),
        ],
        out_specs=pl.BlockSpec((tm, D_MODEL), lambda i, j: (i, 0)),
        out_shape=jax.ShapeDtypeStruct((m, D_MODEL), F32),
        compiler_params=_params("arbitrary", "arbitrary"),
        name="ffn",
    )(hn, h, g_final, wg, wu, wd)


IN_PROJ_TM_PROMPT = 2048
IN_PROJ_TN = 512
MIX_TM = 256
FFN_TM = 1024
FFN_TF = 512


def kernel(x_prompt, x_sample, cache_k, cache_v, state_conv, rel_bias, w_in, w_conv, w_conv_out, sinks,
           w_attn_out, w_o, g_mix, g_ffn, w_gate, w_up, w_down, g_final):
    depth = w_in.shape[0]
    batch, seq, _ = x_prompt.shape
    dec_batch, steps, _ = x_sample.shape
    wc = cache_k.shape[2]
    assert seq % WINDOW == 0 and wc == WINDOW and dec_batch % SAMPLE_BATCH_TILE == 0

    hp = x_prompt.reshape(batch * seq, D_MODEL)
    hs = x_sample.reshape(dec_batch * steps, D_MODEL)
    g_final2 = g_final.reshape(1, D_MODEL)
    kp_l, vp_l, cp_l, ks_l, vs_l, cs_l = [], [], [], [], [], []
    for l in range(depth):
        final = l == depth - 1
        lw = {
            "w_in": w_in[l],
            "g_ffn": g_ffn[l].reshape(1, D_MODEL),
        }
        g_mix_l = g_mix[l].reshape(1, D_MODEL)

        zp, lw["wo"] = _in_proj(hp, g_mix_l, lw["w_in"], w_o[l], IN_PROJ_TM_PROMPT, IN_PROJ_TN)
        zs, attn_p, lw["wco"], lw["wao"], kp_t, vp_t = _in_proj_sample_attn_prompt(
            hs, g_mix_l, lw["w_in"], w_conv_out[l], w_attn_out[l], zp, rel_bias, sinks[l], IN_PROJ_TN, batch, seq)
        hp, hnp, conv_p, lw["wg"], lw["wu"], lw["wd"] = _mix_prompt(
            zp, attn_p, hp, w_conv[l], lw["g_ffn"], lw["wco"], lw["wao"], lw["wo"],
            (w_gate[l], w_up[l], w_down[l]), MIX_TM, batch, seq)
        hp = _ffn(hnp, hp, g_final2, lw["wg"], lw["wu"], lw["wd"], FFN_TM, FFN_TF, final)
        kp_l.append(jnp.transpose(kp_t.reshape(batch, N_KV_HEADS, HEAD_DIM, WINDOW), (0, 3, 1, 2)))
        vp_l.append(jnp.transpose(vp_t.reshape(batch, N_KV_HEADS, HEAD_DIM, WINDOW), (0, 3, 1, 2)))
        cp_l.append(conv_p)

        cache_kt = jnp.transpose(cache_k[l], (0, 2, 3, 1)).reshape(dec_batch, D_KV, wc)
        cache_vt = jnp.transpose(cache_v[l], (0, 2, 3, 1)).reshape(dec_batch, D_KV, wc)
        attn_s, nk, nv = _attn_sample(zs, cache_kt, cache_vt, rel_bias, sinks[l], dec_batch, steps)
        hs, hns, conv_s = _mix_sample(zs, state_conv[l], attn_s, hs, w_conv[l], lw["g_ffn"], lw["wco"], lw["wao"],
                                      lw["wo"], MIX_TM, steps)
        hs = _ffn(hns, hs, g_final2, lw["wg"], lw["wu"], lw["wd"], FFN_TM, FFN_TF, final)
        ks_l.append(jnp.transpose(nk.reshape(dec_batch, N_KV_HEADS, HEAD_DIM, wc), (0, 3, 1, 2)))
        vs_l.append(jnp.transpose(nv.reshape(dec_batch, N_KV_HEADS, HEAD_DIM, wc), (0, 3, 1, 2)))
        cs_l.append(conv_s)

    return (hp.reshape(batch, seq, D_MODEL), hs.reshape(dec_batch, steps, D_MODEL),
            jnp.stack(kp_l), jnp.stack(vp_l), jnp.stack(cp_l),
            jnp.stack(ks_l), jnp.stack(vs_l), jnp.stack(cs_l))
```

```python
import functools
import math

import numpy as np
import jax
import jax.numpy as jnp
from jax import lax
from jax.experimental import pallas as pl
from jax.experimental.pallas import tpu as pltpu

D_MODEL = 2048
D_CONV = D_MODEL // 2
CONV_WIDTH = 3
HEAD_DIM = 64
N_HEADS = (D_MODEL // 2) // HEAD_DIM
N_KV_HEADS = N_HEADS // 4
GROUP = N_HEADS // N_KV_HEADS
D_ATTN = N_HEADS * HEAD_DIM
D_KV = N_KV_HEADS * HEAD_DIM
WINDOW = 128
NUM_BUCKETS = 32
MAX_DISTANCE = 128
D_FF = -(-8 * D_MODEL // (3 * 256)) * 256
EPS = 1e-6
D_IN_PROJ = 3 * D_CONV + D_ATTN + 2 * D_KV + 2 * D_MODEL

OFF_GC = 0
OFF_GA = D_MODEL
OFF_B = 2 * D_MODEL
OFF_C = OFF_B + D_CONV
OFF_H = OFF_C + D_CONV
OFF_Q = OFF_H + D_CONV
OFF_K = OFF_Q + D_ATTN
OFF_V = OFF_K + D_KV

LANES = 128
NEG = -1e30
F32 = jnp.float32
BF16 = jnp.bfloat16
VMEM_LIMIT = 60 * 1024 * 1024


def _bucket_thresholds():
    max_exact = NUM_BUCKETS // 2
    d = np.arange(MAX_DISTANCE)
    ratio = np.log(np.maximum(d, 1).astype(np.float32) / np.float32(max_exact)) / np.float32(
        math.log(MAX_DISTANCE / max_exact))
    large = max_exact + (ratio * np.float32(NUM_BUCKETS - max_exact)).astype(np.int32)
    large = np.minimum(large, NUM_BUCKETS - 1)
    return [int(np.min(d[(d >= max_exact) & (large >= b)])) for b in range(max_exact + 1, NUM_BUCKETS)]


BUCKET_THRESHOLDS = _bucket_thresholds()


def _params(*semantics):
    return pltpu.CompilerParams(dimension_semantics=semantics, vmem_limit_bytes=VMEM_LIMIT)


def _rms_rows(x, g):
    ms = jnp.mean(x * x, axis=-1, keepdims=True)
    return x * lax.rsqrt(ms + EPS) * g


NORM_ROWS = 256


def _inproj_kernel(x_ref, g_ref, w_ref, side_ref, z_ref, side_bf16_ref, xn_ref):
    @pl.when(pl.program_id(1) == 0)
    def _():
        _inproj_norm(x_ref, g_ref, xn_ref)

    _inproj_columns(w_ref, z_ref, xn_ref)
    side_bf16_ref[...] = side_ref[...].astype(BF16)


BF16_ROWS = 16


def _side_cast_rows(rows, nsteps):
    per_step = -(-rows // nsteps)
    per_step = -(-per_step // BF16_ROWS) * BF16_ROWS
    assert rows % per_step == 0
    return per_step


def _inproj_norm(x_ref, g_ref, xn_ref):
    def body(r, carry):
        rows = pl.ds(pl.multiple_of(r * NORM_ROWS, NORM_ROWS), NORM_ROWS)
        xn_ref[rows, :] = _rms_rows(x_ref[rows, :], g_ref[0:1, :]).astype(xn_ref.dtype)
        return carry

    lax.fori_loop(0, x_ref.shape[0] // NORM_ROWS, body, 0)


def _inproj_columns(w_ref, z_ref, xn_ref):
    z_ref[...] = jnp.dot(xn_ref[...], w_ref[...].astype(BF16), preferred_element_type=F32).astype(z_ref.dtype)


def _in_proj(x, g, w, side, tm, tn):
    m = x.shape[0]
    n = w.shape[1]
    gate_start = n - 2 * D_MODEL
    assert gate_start % tn == 0 and n % tn == 0
    gate_block, nblocks = gate_start // tn, n // tn
    side_rows = _side_cast_rows(side.shape[0], (m // tm) * nblocks)
    side_spec = pl.BlockSpec((side_rows, side.shape[1]),
                             lambda i, j: (jnp.minimum(i * nblocks + j, side.shape[0] // side_rows - 1), 0))
    return pl.pallas_call(
        _inproj_kernel,
        grid=(m // tm, nblocks),
        in_specs=[
            pl.BlockSpec((tm, D_MODEL), lambda i, j: (i, 0)),
            pl.BlockSpec((1, D_MODEL),lambda i, j: (0, 0)),
            pl.BlockSpec((D_MODEL, tn), lambda i, j: (0, (j + gate_block) % nblocks)),
            side_spec,
        ],
        out_specs=[pl.BlockSpec((tm, tn), lambda i, j: (i, j)), side_spec],
        out_shape=[jax.ShapeDtypeStruct((m, n), BF16), jax.ShapeDtypeStruct(side.shape, BF16)],
        scratch_shapes=[pltpu.VMEM((tm, D_MODEL), BF16)],
        compiler_params=_params("arbitrary", "arbitrary"),
        name="in_proj",
    )(x, g, w, side)


def _bias_from_distance(dist, head, relb_ref):
    valid = (dist >= 0) & (dist < WINDOW)
    d = jnp.clip(dist, 0, MAX_DISTANCE - 1)
    bucket = jnp.minimum(d, NUM_BUCKETS // 2)
    for thr in BUCKET_THRESHOLDS:
        bucket = bucket + (d >= thr).astype(jnp.int32)
    bias = jnp.zeros(dist.shape, F32)
    for b in range(NUM_BUCKETS):
        bias = jnp.where(bucket == b, relb_ref[b, head], bias)
    return jnp.where(valid, bias, NEG)


def _biased_scores(s, bias):
    return jnp.where(bias > 0.5 * NEG, s + bias, NEG)


def _head_in_group_columns(q, h):
    half = D_KV // 2
    assert 2 * HEAD_DIM == half and GROUP * HEAD_DIM == D_KV
    g = h // GROUP
    src = q[:, (h // 2) * half:(h // 2 + 1) * half]
    if h % 2 != g % 2:
        src = pltpu.roll(src, HEAD_DIM, 1)
    lane_half = lax.broadcasted_iota(jnp.int32, src.shape, 1) // HEAD_DIM
    placed = jnp.where(lane_half == g % 2, src, 0.0)
    zero = jnp.zeros_like(placed)
    return jnp.concatenate([placed, zero] if g // 2 == 0 else [zero, placed], axis=1)


def _group_columns(h):
    g = h // GROUP
    return slice(g * HEAD_DIM, (g + 1) * HEAD_DIM)


INPROJ_FILL_CHUNKS = 4
PROMPT_BLOCKS_PER_STEP = 4


def _inproj_attn_kernel(x_ref, g_ref, w_ref, side0_ref, side1_ref, relb_ref, sink_ref, q_ref, kp_ref, kc_ref,
                        vp_ref, vc_ref, z_ref, o_ref, side0_bf16_ref, side1_bf16_ref, kst_ref, vst_ref,
                        xn_ref, wb_ref, bias_ref, lhs_ref, *, steps_per_seq):
    j = pl.program_id(0)
    @pl.when(j == 0)
    def _():
        _attn_prompt_bias(relb_ref, bias_ref)
        _inproj_norm(x_ref, g_ref, xn_ref)

    wb_ref[...] = w_ref[...].astype(BF16)
    side0_bf16_ref[...] = side0_ref[...].astype(BF16)
    side1_bf16_ref[...] = side1_ref[...].astype(BF16)
    chunk = xn_ref.shape[0] // INPROJ_FILL_CHUNKS

    def column_chunk(c):
        rows = slice(c * chunk, (c + 1) * chunk)

        def emit():
            z_ref[rows, :] = jnp.dot(xn_ref[rows, :], wb_ref[...], preferred_element_type=F32).astype(z_ref.dtype)
        return emit

    _attn_prompt_blocks(j % steps_per_seq != 0, sink_ref, q_ref, kp_ref, kc_ref, vp_ref, vc_ref,
                        o_ref, bias_ref, lhs_ref, fillers=[column_chunk(c) for c in range(INPROJ_FILL_CHUNKS)])
    last = kc_ref.shape[0] - WINDOW
    kst_ref[0] = kc_ref[last:, :].astype(F32).T
    vst_ref[0] = vc_ref[last:, :].astype(F32).T


def _from_prev_mask():
    qi = lax.broadcasted_iota(jnp.int32, (WINDOW, WINDOW), 0)
    kj = lax.broadcasted_iota(jnp.int32, (WINDOW, WINDOW), 1)
    return qi, kj, kj > qi


def _attn_prompt_bias(relb_ref, bias_ref):
    qi, kj, from_prev = _from_prev_mask()
    dist = jnp.where(from_prev, qi + WINDOW - kj, qi - kj)
    for h in range(N_HEADS):
        rows = slice(h * WINDOW, (h + 1) * WINDOW)
        bias = _bias_from_distance(dist, h, relb_ref)
        bias_ref[1, rows, :] = bias
        bias_ref[0, rows, :] = jnp.where(from_prev, NEG, bias)


def _attn_prompt_blocks(has_prev_block, sink_ref, q_ref, kp_ref, kc_ref, vp_ref, vc_ref, o_ref, bias_ref, lhs_ref,
                        fillers=()):
    _, _, from_prev = _from_prev_mask()
    k_all = jnp.concatenate([kp_ref[...], kc_ref[...]], axis=0)
    v_all = jnp.concatenate([vp_ref[...], vc_ref[...]], axis=0)

    def keys_of(blk):
        return slice(blk * WINDOW, (blk + 2) * WINDOW)

    nblk = PROMPT_BLOCKS_PER_STEP
    fillers = list(fillers)
    per_stage = -(-len(fillers) // (nblk + 2))
    scores, probs = {}, {}
    for stage in range(nblk + 2):
        if stage < nblk:
            q = q_ref[stage * WINDOW:(stage + 1) * WINDOW, :]
            scores[stage] = _block_scores(q, k_all[keys_of(stage)], lhs_ref.at[stage])
        for _ in range(min(per_stage, len(fillers))):
            fillers.pop(0)()
        if 0 <= stage - 1 < nblk:
            blk = stage - 1
            has_prev = has_prev_block.astype(jnp.int32) if blk == 0 else 1
            probs[blk] = _block_softmax(scores.pop(blk), has_prev, from_prev, sink_ref, bias_ref)
        if 0 <= stage - 2 < nblk:
            blk = stage - 2
            o_ref[blk * WINDOW:(blk + 1) * WINDOW, :] = _block_output(
                *probs.pop(blk), v_all[keys_of(blk)]).astype(o_ref.dtype)


def _block_scores(q, kk, lhs_ref):
    q = q.astype(F32) * (HEAD_DIM ** -0.5)
    for h in range(N_HEADS):
        lhs_ref[h * WINDOW:(h + 1) * WINDOW, :] = _head_in_group_columns(q, h).astype(BF16)
    return lax.dot_general(lhs_ref[...], kk, (((1,), (1,)), ((), ())), preferred_element_type=F32)


def _block_softmax(s, has_prev, from_prev, sink_ref, bias_ref):
    probs, sink_terms = [], []
    for h in range(N_HEADS):
        rows = slice(h * WINDOW, (h + 1) * WINDOW)
        sh = _biased_scores(jnp.where(from_prev, s[rows, :WINDOW], s[rows, WINDOW:]), bias_ref[has_prev, rows, :])
        m = jnp.maximum(jnp.max(sh, axis=-1, keepdims=True), sink_ref[h])
        p = jnp.exp(sh - m)
        probs.append(jnp.concatenate([jnp.where(from_prev, p, 0.0), jnp.where(from_prev, 0.0, p)],
                                     axis=1).astype(BF16))
        sink_terms.append(jnp.exp(sink_ref[h] - m))
    return probs, sink_terms


def _block_output(probs, sink_terms, vv):
    assert 2 * HEAD_DIM == LANES and GROUP % 2 == 0
    ones = jnp.ones((vv.shape[0], HEAD_DIM), BF16)
    low_half = lax.broadcasted_iota(jnp.int32, (WINDOW, LANES), 1) < HEAD_DIM
    outs = []
    for g in range(N_KV_HEADS):
        vg = vv[:, g * HEAD_DIM:(g + 1) * HEAD_DIM]
        even_heads = [g * GROUP + r for r in range(0, GROUP, 2)]
        pe = jnp.concatenate([probs[h] for h in even_heads], axis=0)
        po = jnp.concatenate([probs[h + 1] for h in even_heads], axis=0)
        oe = jnp.dot(pe, jnp.concatenate([vg, ones], axis=1), preferred_element_type=F32)
        oo = jnp.dot(po, jnp.concatenate([ones, vg], axis=1), preferred_element_type=F32)
        for k, h in enumerate(even_heads):
            a = oe[k * WINDOW:(k + 1) * WINDOW]
            b = oo[k * WINDOW:(k + 1) * WINDOW]
            out_pair = jnp.where(low_half, a, b)
            sum_pair = pltpu.roll(jnp.where(low_half, b, a), HEAD_DIM, 1)
            sink_pair = jnp.where(low_half, sink_terms[h], sink_terms[h + 1])
            outs.append(out_pair * (1.0 / (sum_pair + sink_pair)))
    return jnp.concatenate(outs, axis=1)


def _in_proj_sample_attn_prompt(x, g, w, side0, side1, zp, rel_bias, sinks, tn, batch, seq):
    tm = x.shape[0]
    n = w.shape[1]
    gate_start = n - 2 * D_MODEL
    assert gate_start % tn == 0 and n % tn == 0
    gate_block, nsteps = gate_start // tn, n // tn
    bps = PROMPT_BLOCKS_PER_STEP
    assert seq % (bps * WINDOW) == 0
    steps_per_seq = seq // (bps * WINDOW)
    attn_steps = batch * steps_per_seq
    assert attn_steps <= nsteps

    def astep(j):
        return jnp.minimum(j, attn_steps - 1)

    def cur(off, width):
        return lambda j: (astep(j), off // width)

    def prev(off, width):
        return lambda j: (astep(j) * bps - jnp.minimum(astep(j) % steps_per_seq, 1), off // width)

    def side_spec(side):
        rows = _side_cast_rows(side.shape[0], nsteps)
        return pl.BlockSpec((rows, side.shape[1]), lambda j: (jnp.minimum(j, side.shape[0] // rows - 1), 0))

    smem = pl.BlockSpec(memory_space=pltpu.SMEM)
    state_spec = pl.BlockSpec((1, D_KV, WINDOW), lambda j: (astep(j) // steps_per_seq, 0, 0))
    return pl.pallas_call(
        functools.partial(_inproj_attn_kernel, steps_per_seq=steps_per_seq),
        grid=(nsteps,),
        in_specs=[
            pl.BlockSpec((tm, D_MODEL), lambda j: (0, 0)),
            pl.BlockSpec((1, D_MODEL),lambda j: (0, 0)),
            pl.BlockSpec((D_MODEL, tn), lambda j: (0, (j + gate_block) % nsteps)),
            side_spec(side0),
            side_spec(side1),
            smem,
            smem,
            pl.BlockSpec((bps * WINDOW, D_ATTN), cur(OFF_Q, D_ATTN)),
            pl.BlockSpec((WINDOW, D_KV), prev(OFF_K, D_KV)),
            pl.BlockSpec((bps * WINDOW, D_KV), cur(OFF_K, D_KV)),
            pl.BlockSpec((WINDOW, D_KV), prev(OFF_V, D_KV)),
            pl.BlockSpec((bps * WINDOW, D_KV), cur(OFF_V, D_KV)),
        ],
        out_specs=[
            pl.BlockSpec((tm, tn), lambda j: (0, j)),
            pl.BlockSpec((bps * WINDOW, D_ATTN), lambda j: (j, 0)),
            side_spec(side0),
            side_spec(side1),
            state_spec,
            state_spec,
        ],
        out_shape=[
            jax.ShapeDtypeStruct((tm, n), BF16),
            jax.ShapeDtypeStruct((nsteps * bps * WINDOW, D_ATTN), BF16),
            jax.ShapeDtypeStruct(side0.shape, BF16),
            jax.ShapeDtypeStruct(side1.shape, BF16),
            jax.ShapeDtypeStruct((batch, D_KV, WINDOW), F32),
            jax.ShapeDtypeStruct((batch, D_KV, WINDOW), F32),
        ],
        scratch_shapes=[pltpu.VMEM((tm, D_MODEL), BF16),
                        pltpu.VMEM((D_MODEL, tn), BF16),
                        pltpu.VMEM((2, N_HEADS * WINDOW, WINDOW), F32),
                        pltpu.VMEM((bps, N_HEADS * WINDOW, D_KV), BF16)],
        compiler_params=_params("arbitrary"),
        name="in_proj_sample_attn_prompt",
    )(x, g, w, side0, side1, rel_bias, sinks, zp, zp, zp, zp, zp)


SAMPLE_BATCH_TILE = 16


def _attn_sample_kernel(relb_ref, sink_ref, q_ref, kn_ref, vn_ref, ckt_ref, cvt_ref,
                        o_ref, nk_ref, nv_ref,
                        bias1_ref, bias2_ref, sinkv_ref, z_ref, s_ref, p_ref, oh_ref, *, steps):
    bt, _, wc = nk_ref.shape
    tile_rows = bt * steps

    @pl.when(pl.program_id(0) == 0)
    def _():
        t = lax.broadcasted_iota(jnp.int32, (steps, wc), 0)
        j = lax.broadcasted_iota(jnp.int32, (steps, wc), 1)
        t2 = lax.broadcasted_iota(jnp.int32, (steps, tile_rows), 0)
        j2 = lax.broadcasted_iota(jnp.int32, (steps, tile_rows), 1)
        own = jnp.where(j2 < steps, t2 - j2, -1)
        for h in range(N_HEADS):
            sl = slice(h * steps, (h + 1) * steps)
            bias1_ref[sl, :] = _bias_from_distance(t + wc - j, h, relb_ref)
            bias2_ref[0, sl, :] = _bias_from_distance(own, h, relb_ref)
            sinkv_ref[sl, :] = jnp.full((steps, LANES), sink_ref[h], F32)
        for b in range(1, bt):
            bias2_ref[b] = pltpu.roll(bias2_ref[0], b * steps, 1)

    kn = kn_ref[...]
    vn = vn_ref[...]
    knt = kn.astype(F32).T.astype(BF16)
    vnt = vn.astype(F32).T.astype(BF16)
    new_lanes = lax.broadcasted_iota(jnp.int32, (D_KV, wc), 1) >= wc - steps
    row = lax.broadcasted_iota(jnp.int32, (tile_rows, wc), 0)
    lane = lax.broadcasted_iota(jnp.int32, (tile_rows, wc), 1)
    first_row = jnp.where(lane >= wc - steps, row - (lane - (wc - steps)), -1)

    def slid(cache_t, new_t, b):
        place = jnp.where(first_row == b * steps, 1.0, 0.0).astype(BF16)
        new = jnp.dot(new_t, place, preferred_element_type=F32)
        return jnp.where(new_lanes, new, pltpu.roll(cache_t, wc - steps, 1))

    q = q_ref[...].astype(F32) * (HEAD_DIM ** -0.5)
    for h in range(N_HEADS):
        z_ref[:, h * steps:(h + 1) * steps, :] = _head_in_group_columns(q, h).reshape(bt, steps, D_KV)

    for b in range(bt):
        ckt = ckt_ref[b]
        nk_ref[b] = slid(ckt, knt, b)
        keys_t = jnp.concatenate([ckt.astype(BF16), knt], axis=1)
        s_ref[b] = jnp.dot(z_ref[b].astype(BF16), keys_t, preferred_element_type=F32)

    bias = jnp.concatenate([jnp.broadcast_to(bias1_ref[...], (bt,) + bias1_ref.shape), bias2_ref[...]], axis=2)
    s = _biased_scores(s_ref[...], bias)
    sink = sinkv_ref[...][None]
    m = jnp.maximum(jnp.max(s, axis=-1, keepdims=True), sink)
    p_ref[...] = jnp.exp(s - jnp.concatenate([m] * (s.shape[-1] // LANES), axis=-1)).astype(BF16)
    sink_term = jnp.exp(sink - m)

    vnt_ones = jnp.concatenate([vnt, jnp.ones((LANES, tile_rows), BF16)], axis=0)
    for b in range(bt):
        cvt = cvt_ref[b]
        nv_ref[b] = slid(cvt, vnt, b)
        cvt_ones = jnp.concatenate([cvt.astype(BF16), jnp.ones((LANES, wc), BF16)], axis=0)
        oh_ref[b] = lax.dot_general(p_ref[b], jnp.concatenate([cvt_ones, vnt_ones], axis=1),
                                    (((1,), (1,)), ((), ())), preferred_element_type=F32)

    inv = 1.0 / (oh_ref[:, :, D_KV:] + sink_term)
    outs = []
    for h in range(N_HEADS):
        rows = slice(h * steps, (h + 1) * steps)
        outs.append((oh_ref[:, rows, _group_columns(h)] * inv[:, rows, :HEAD_DIM]).reshape(tile_rows, HEAD_DIM))
    o_ref[...] = jnp.concatenate(outs, axis=1).astype(o_ref.dtype)


def _attn_sample(z, cache_kt, cache_vt, rel_bias, sinks, batch, steps):
    wc = cache_kt.shape[2]
    bt = SAMPLE_BATCH_TILE
    rows = bt * steps
    smem = pl.BlockSpec(memory_space=pltpu.SMEM)
    cache_t_spec = pl.BlockSpec((bt, D_KV, wc), lambda i: (i, 0, 0))
    return pl.pallas_call(
        functools.partial(_attn_sample_kernel, steps=steps),
        grid=(batch // bt,),
        in_specs=[
            smem,
            smem,
            pl.BlockSpec((rows, D_ATTN), lambda i: (i, OFF_Q // D_ATTN)),
            pl.BlockSpec((rows, D_KV), lambda i: (i, OFF_K // D_KV)),
            pl.BlockSpec((rows, D_KV), lambda i: (i, OFF_V // D_KV)),
            cache_t_spec,
            cache_t_spec,
        ],
        out_specs=[
            pl.BlockSpec((rows, D_ATTN), lambda i: (i, 0)),
            cache_t_spec,
            cache_t_spec,
        ],
        out_shape=[
            jax.ShapeDtypeStruct((batch * steps, D_ATTN), BF16),
            jax.ShapeDtypeStruct((batch, D_KV, wc), F32),
            jax.ShapeDtypeStruct((batch, D_KV, wc), F32),
        ],
        scratch_shapes=[
            pltpu.VMEM((N_HEADS * steps, wc), F32),
            pltpu.VMEM((bt, N_HEADS * steps, rows), F32),
            pltpu.VMEM((N_HEADS * steps, LANES), F32),
            pltpu.VMEM((bt, N_HEADS * steps, D_KV), F32),
            pltpu.VMEM((bt, N_HEADS * steps, wc + rows), F32),
            pltpu.VMEM((bt, N_HEADS * steps, wc + rows), BF16),
            pltpu.VMEM((bt, N_HEADS * steps, D_KV + LANES), F32),
        ],
        compiler_params=_params("arbitrary"),
        name="attn_sample",
    )(rel_bias, sinks, z, z, z, cache_kt, cache_vt)


CARRY_ROWS = 8


def _gated_conv(b, wc_ref, u, u1, u2):
    conv = wc_ref[0:1, :] * u2 + wc_ref[1:2, :] * u1 + wc_ref[2:3, :] * u
    return (b.astype(F32) * conv).astype(BF16)


def _mix_tail(conv_branch_input, at_ref, gc_ref, ga_ref, x_ref, g_ref, wco_ref, wao_ref, wo_ref, h_ref, hn_ref):
    y_conv = jnp.dot(conv_branch_input(), wco_ref[...], preferred_element_type=F32)
    y_attn = jnp.dot(at_ref[...], wao_ref[...], preferred_element_type=F32)
    merged = (jax.nn.sigmoid(gc_ref[...].astype(F32)) * y_conv
              + jax.nn.sigmoid(ga_ref[...].astype(F32)) * y_attn)
    h = x_ref[...] + jnp.dot(merged.astype(BF16), wo_ref[...], preferred_element_type=F32)
    h_ref[...] = h
    hn_ref[...] = _rms_rows(h, g_ref[0:1, :]).astype(hn_ref.dtype)


def _mix_prompt_kernel(b_ref, c_ref, hc_ref, wc_ref, at_ref, gc_ref, ga_ref, x_ref, g_ref,
                       wco_ref, wao_ref, wo_ref, side0_ref, side1_ref, side2_ref,
                       h_ref, hn_ref, st_ref, side0_bf16_ref, side1_bf16_ref, side2_bf16_ref,
                       us_ref, *, tiles_per_seq):
    tm = x_ref.shape[0]
    side0_bf16_ref[...] = side0_ref[...].astype(BF16)
    side1_bf16_ref[...] = side1_ref[...].astype(BF16)
    side2_bf16_ref[...] = side2_ref[...].astype(BF16)

    def conv_branch_input():
        u = c_ref[...].astype(F32) * hc_ref[...].astype(F32)
        st_ref[...] = u[tm - (CONV_WIDTH - 1):, :]
        carry = jnp.where(pl.program_id(0) % tiles_per_seq == 0, 0.0, us_ref[...])
        u_ext = jnp.concatenate([carry, u], axis=0)
        us_ref[...] = u[tm - CARRY_ROWS:, :]
        return _gated_conv(b_ref[...], wc_ref, u,
                           pltpu.roll(u_ext, 1, 0)[CARRY_ROWS:, :], pltpu.roll(u_ext, 2, 0)[CARRY_ROWS:, :])

    _mix_tail(conv_branch_input, at_ref, gc_ref, ga_ref, x_ref, g_ref, wco_ref, wao_ref, wo_ref, h_ref, hn_ref)


def _mix_sample_kernel(b_ref, c_ref, hc_ref, st_ref, wc_ref, at_ref, gc_ref, ga_ref, x_ref, g_ref,
                       wco_ref, wao_ref, wo_ref, h_ref, hn_ref, new_st_ref, us_ref, *, steps):
    tm = x_ref.shape[0]
    nseq = tm // steps
    keep = CONV_WIDTH - 1

    def conv_branch_input():
        u = c_ref[...].astype(F32) * hc_ref[...].astype(F32)
        us_ref[:, CARRY_ROWS - keep:CARRY_ROWS, :] = st_ref[...]
        us_ref[:, CARRY_ROWS:, :] = u.reshape(nseq, steps, D_CONV)
        new_st_ref[...] = us_ref[:, CARRY_ROWS + steps - keep:, :]

        def delayed(d):
            return us_ref[:, CARRY_ROWS - d:CARRY_ROWS - d + steps, :].reshape(tm, D_CONV)

        return _gated_conv(b_ref[...], wc_ref, u, delayed(1), delayed(2))

    _mix_tail(conv_branch_input, at_ref, gc_ref, ga_ref, x_ref, g_ref, wco_ref, wao_ref, wo_ref, h_ref, hn_ref)


def _mix_specs(tm):
    resident = functools.partial(pl.BlockSpec, index_map=lambda i: (0, 0), pipeline_mode=pl.Buffered(1))
    row_spec = pl.BlockSpec((tm, D_MODEL), lambda i: (i, 0))

    def zcol(off, width):
        return pl.BlockSpec((tm, width), lambda i: (i, off // width))

    conv_in = [zcol(OFF_B, D_CONV), zcol(OFF_C, D_CONV), zcol(OFF_H, D_CONV)]
    rest_in = [
        pl.BlockSpec((CONV_WIDTH, D_CONV), lambda i: (0, 0)),
        pl.BlockSpec((tm, D_ATTN), lambda i: (i, 0)),
        zcol(OFF_GC, D_MODEL),
        zcol(OFF_GA, D_MODEL),
        row_spec,
        pl.BlockSpec((1, D_MODEL),lambda i: (0, 0)),
        resident((D_CONV, D_MODEL)),
        resident((D_ATTN, D_MODEL)),
        resident((D_MODEL, D_MODEL)),
    ]
    return conv_in, rest_in, row_spec


def _mix_prompt(z, attn, x, w_conv, g_ffn, wco, wao, wo, sides, tm, batch, seq):
    m = x.shape[0]
    assert seq % tm == 0 and tm >= CARRY_ROWS
    tiles_per_seq = seq // tm
    nsteps = m // tm
    conv_in, rest_in, row_spec = _mix_specs(tm)

    def side_spec(side):
        rows = _side_cast_rows(side.shape[0], nsteps)
        return pl.BlockSpec((rows, side.shape[1]), lambda i: (jnp.minimum(i, side.shape[0] // rows - 1), 0))

    side_specs = [side_spec(s) for s in sides]
    return pl.pallas_call(
        functools.partial(_mix_prompt_kernel, tiles_per_seq=tiles_per_seq),
        grid=(nsteps,),
        in_specs=conv_in + rest_in + side_specs,
        out_specs=[row_spec, row_spec,
                   pl.BlockSpec((None, CONV_WIDTH - 1, D_CONV), lambda i: (i // tiles_per_seq, 0, 0))] + side_specs,
        out_shape=[jax.ShapeDtypeStruct((m, D_MODEL), F32), jax.ShapeDtypeStruct((m, D_MODEL), BF16),
                   jax.ShapeDtypeStruct((batch, CONV_WIDTH - 1, D_CONV), F32)]
        + [jax.ShapeDtypeStruct(s.shape, BF16) for s in sides],
        scratch_shapes=[pltpu.VMEM((CARRY_ROWS, D_CONV), F32)],
        compiler_params=_params("arbitrary"),
        name="mix_prompt",
    )(z, z, z, w_conv, attn, z, z, x, g_ffn, wco, wao, wo, *sides)


def _mix_sample(z, state, attn, x, w_conv, g_ffn, wco, wao, wo, tm, steps):
    m = x.shape[0]
    assert tm % steps == 0 and CONV_WIDTH - 1 <= min(steps, CARRY_ROWS)
    conv_in, rest_in, row_spec = _mix_specs(tm)
    state_spec = pl.BlockSpec((tm // steps, CONV_WIDTH - 1, D_CONV), lambda i: (i, 0, 0))
    return pl.pallas_call(
        functools.partial(_mix_sample_kernel, steps=steps),
        grid=(m // tm,),
        in_specs=conv_in + [state_spec] + rest_in,
        out_specs=[row_spec, row_spec, state_spec],
        out_shape=[jax.ShapeDtypeStruct((m, D_MODEL), F32), jax.ShapeDtypeStruct((m, D_MODEL), BF16),
                   jax.ShapeDtypeStruct(state.shape, F32)],
        scratch_shapes=[pltpu.VMEM((tm // steps, CARRY_ROWS + steps, D_CONV), F32)],
        compiler_params=_params("arbitrary"),
        name="mix_sample",
    )(z, z, z, state, w_conv, attn, z, z, x, g_ffn, wco, wao, wo)


def _ffn_kernel(hn_ref, h_ref, gf_ref, wg_ref, wu_ref, wd_ref, y_ref, *, final_norm):
    j = pl.program_id(1)

    hn = hn_ref[...]
    a = (jax.nn.silu(jnp.dot(hn, wg_ref[...], preferred_element_type=F32))
         * jnp.dot(hn, wu_ref[...], preferred_element_type=F32))
    acc = jnp.where(j == 0, h_ref[...], y_ref[...])
    y_ref[...] = acc + jnp.dot(a.astype(BF16), wd_ref[...], preferred_element_type=F32)

    if final_norm:
        @pl.when(j == pl.num_programs(1) - 1)
        def _():
            def body(r, carry):
                rows = pl.ds(pl.multiple_of(r * NORM_ROWS, NORM_ROWS), NORM_ROWS)
                y_ref[rows, :] = _rms_rows(y_ref[rows, :], gf_ref[0:1, :])
                return carry

            lax.fori_loop(0, y_ref.shape[0] // NORM_ROWS, body, 0)


def _ffn(hn, h, g_final, wg, wu, wd, tm, tf, final_norm):
    m = h.shape[0]
    return pl.pallas_call(
        functools.partial(_ffn_kernel, final_norm=final_norm),
        grid=(m // tm, D_FF // tf),
        in_specs=[
            pl.BlockSpec((tm, D_MODEL), lambda i, j: (i, 0)),
            pl.BlockSpec((tm, D_MODEL), lambda i, j: (i, 0)),
            pl.BlockSpec((1, D_MODEL),lambda i, j: (0, 0)),
            pl.BlockSpec((D_MODEL, tf), lambda i, j: (0, j)),
            pl.BlockSpec((D_MODEL, tf), lambda i, j: (0, j)),
            pl.BlockSpec((tf, D_MODEL), lambda i, j: (j, 0)),
        ],
        out_specs=pl.BlockSpec((tm, D_MODEL), lambda i, j: (i, 0)),
        out_shape=jax.ShapeDtypeStruct((m, D_MODEL), F32),
        compiler_params=_params("arbitrary", "arbitrary"),
        name="ffn",
    )(hn, h, g_final, wg, wu, wd)


IN_PROJ_TM_PROMPT = 2048
IN_PROJ_TN = 512
MIX_TM = 256
FFN_TM = 1024
FFN_TF = 512


def kernel(x_prompt, x_sample, cache_k, cache_v, state_conv, rel_bias, w_in, w_conv, w_conv_out, sinks,
           w_attn_out, w_o, g_mix, g_ffn, w_gate, w_up, w_down, g_final):
    depth = w_in.shape[0]
    batch, seq, _ = x_prompt.shape
    dec_batch, steps, _ = x_sample.shape
    wc = cache_k.shape[2]
    assert seq % WINDOW == 0 and wc == WINDOW and dec_batch % SAMPLE_BATCH_TILE == 0

    hp = x_prompt.reshape(batch * seq, D_MODEL)
    hs = x_sample.reshape(dec_batch * steps, D_MODEL)
    g_final2 = g_final.reshape(1, D_MODEL)
    kp_l, vp_l, cp_l, ks_l, vs_l, cs_l = [], [], [], [], [], []
    for l in range(depth):
        final = l == depth - 1
        lw = {
            "w_in": w_in[l],
            "g_ffn": g_ffn[l].reshape(1, D_MODEL),
        }
        g_mix_l = g_mix[l].reshape(1, D_MODEL)

        zp, lw["wo"] = _in_proj(hp, g_mix_l, lw["w_in"], w_o[l], IN_PROJ_TM_PROMPT, IN_PROJ_TN)
        zs, attn_p, lw["wco"], lw["wao"], kp_t, vp_t = _in_proj_sample_attn_prompt(
            hs, g_mix_l, lw["w_in"], w_conv_out[l], w_attn_out[l], zp, rel_bias, sinks[l], IN_PROJ_TN, batch, seq)
        hp, hnp, conv_p, lw["wg"], lw["wu"], lw["wd"] = _mix_prompt(
            zp, attn_p, hp, w_conv[l], lw["g_ffn"], lw["wco"], lw["wao"], lw["wo"],
            (w_gate[l], w_up[l], w_down[l]), MIX_TM, batch, seq)
        hp = _ffn(hnp, hp, g_final2, lw["wg"], lw["wu"], lw["wd"], FFN_TM, FFN_TF, final)
        kp_l.append(jnp.transpose(kp_t.reshape(batch, N_KV_HEADS, HEAD_DIM, WINDOW), (0, 3, 1, 2)))
        vp_l.append(jnp.transpose(vp_t.reshape(batch, N_KV_HEADS, HEAD_DIM, WINDOW), (0, 3, 1, 2)))
        cp_l.append(conv_p)

        cache_kt = jnp.transpose(cache_k[l], (0, 2, 3, 1)).reshape(dec_batch, D_KV, wc)
        cache_vt = jnp.transpose(cache_v[l], (0, 2, 3, 1)).reshape(dec_batch, D_KV, wc)
        attn_s, nk, nv = _attn_sample(zs, cache_kt, cache_vt, rel_bias, sinks[l], dec_batch, steps)
        hs, hns, conv_s = _mix_sample(zs, state_conv[l], attn_s, hs, w_conv[l], lw["g_ffn"], lw["wco"], lw["wao"],
                                      lw["wo"], MIX_TM, steps)
        hs = _ffn(hns, hs, g_final2, lw["wg"], lw["wu"], lw["wd"], FFN_TM, FFN_TF, final)
        ks_l.append(jnp.transpose(nk.reshape(dec_batch, N_KV_HEADS, HEAD_DIM, wc), (0, 3, 1, 2)))
        vs_l.append(jnp.transpose(nv.reshape(dec_batch, N_KV_HEADS, HEAD_DIM, wc), (0, 3, 1, 2)))
        cs_l.append(conv_s)

    return (hp.reshape(batch, seq, D_MODEL), hs.reshape(dec_batch, steps, D_MODEL),
            jnp.stack(kp_l), jnp.stack(vp_l), jnp.stack(cp_l),
            jnp.stack(ks_l), jnp.stack(vs_l), jnp.stack(cs_l))
```

```python
import functools
import math

import numpy as np
import jax
import jax.numpy as jnp
from jax import lax
from jax.experimental import pallas as pl
from jax.experimental.pallas import tpu as pltpu

D_MODEL = 2048
D_CONV = D_MODEL // 2
CONV_WIDTH = 3
HEAD_DIM = 64
N_HEADS = (D_MODEL // 2) // HEAD_DIM
N_KV_HEADS = N_HEADS // 4
GROUP = N_HEADS // N_KV_HEADS
D_ATTN = N_HEADS * HEAD_DIM
D_KV = N_KV_HEADS * HEAD_DIM
WINDOW = 128
NUM_BUCKETS = 32
MAX_DISTANCE = 128
D_FF = -(-8 * D_MODEL // (3 * 256)) * 256
EPS = 1e-6
D_IN_PROJ = 3 * D_CONV + D_ATTN + 2 * D_KV + 2 * D_MODEL

OFF_GC = 0
OFF_GA = D_MODEL
OFF_B = 2 * D_MODEL
OFF_C = OFF_B + D_CONV
OFF_H = OFF_C + D_CONV
OFF_Q = OFF_H + D_CONV
OFF_K = OFF_Q + D_ATTN
OFF_V = OFF_K + D_KV

LANES = 128
NEG = -1e30
F32 = jnp.float32
BF16 = jnp.bfloat16
VMEM_LIMIT = 60 * 1024 * 1024


def _bucket_thresholds():
    max_exact = NUM_BUCKETS // 2
    d = np.arange(MAX_DISTANCE)
    ratio = np.log(np.maximum(d, 1).astype(np.float32) / np.float32(max_exact)) / np.float32(
        math.log(MAX_DISTANCE / max_exact))
    large = max_exact + (ratio * np.float32(NUM_BUCKETS - max_exact)).astype(np.int32)
    large = np.minimum(large, NUM_BUCKETS - 1)
    return [int(np.min(d[(d >= max_exact) & (large >= b)])) for b in range(max_exact + 1, NUM_BUCKETS)]


BUCKET_THRESHOLDS = _bucket_thresholds()


def _params(*semantics):
    return pltpu.CompilerParams(dimension_semantics=semantics, vmem_limit_bytes=VMEM_LIMIT)


def _rms_rows(x, g):
    ms = jnp.mean(x * x, axis=-1, keepdims=True)
    return x * lax.rsqrt(ms + EPS) * g


NORM_ROWS = 256


def _inproj_kernel(x_ref, g_ref, w_ref, side_ref, z_ref, side_bf16_ref, xn_ref):
    _inproj_norm(pl.program_id(1) == 0, x_ref, g_ref, xn_ref)
    _inproj_columns(w_ref, z_ref, xn_ref)
    side_bf16_ref[...] = side_ref[...].astype(BF16)


BF16_ROWS = 16


def _side_cast_rows(rows, nsteps):
    per_step = -(-rows // nsteps)
    per_step = -(-per_step // BF16_ROWS) * BF16_ROWS
    assert rows % per_step == 0
    return per_step


def _inproj_norm(first_column_step, x_ref, g_ref, xn_ref):
    @pl.when(first_column_step)
    def _():
        def body(r, carry):
            rows = pl.ds(pl.multiple_of(r * NORM_ROWS, NORM_ROWS), NORM_ROWS)
            xn_ref[rows, :] = _rms_rows(x_ref[rows, :], g_ref[0:1, :]).astype(xn_ref.dtype)
            return carry

        lax.fori_loop(0, x_ref.shape[0] // NORM_ROWS, body, 0)


def _inproj_columns(w_ref, z_ref, xn_ref):
    z_ref[...] = jnp.dot(xn_ref[...], w_ref[...].astype(BF16), preferred_element_type=F32).astype(z_ref.dtype)


def _in_proj(x, g, w, side, tm, tn):
    m = x.shape[0]
    n = w.shape[1]
    gate_start = n - 2 * D_MODEL
    assert gate_start % tn == 0 and n % tn == 0
    gate_block, nblocks = gate_start // tn, n // tn
    side_rows = _side_cast_rows(side.shape[0], (m // tm) * nblocks)
    side_spec = pl.BlockSpec((side_rows, side.shape[1]),
                             lambda i, j: (jnp.minimum(i * nblocks + j, side.shape[0] // side_rows - 1), 0))
    return pl.pallas_call(
        _inproj_kernel,
        grid=(m // tm, nblocks),
        in_specs=[
            pl.BlockSpec((tm, D_MODEL), lambda i, j: (i, 0)),
            pl.BlockSpec((1, D_MODEL),lambda i, j: (0, 0)),
            pl.BlockSpec((D_MODEL, tn), lambda i, j: (0, (j + gate_block) % nblocks)),
            side_spec,
        ],
        out_specs=[pl.BlockSpec((tm, tn), lambda i, j: (i, j)), side_spec],
        out_shape=[jax.ShapeDtypeStruct((m, n), BF16), jax.ShapeDtypeStruct(side.shape, BF16)],
        scratch_shapes=[pltpu.VMEM((tm, D_MODEL), BF16)],
        compiler_params=_params("arbitrary", "arbitrary"),
        name="in_proj",
    )(x, g, w, side)


def _bias_from_distance(dist, head, relb_ref):
    valid = (dist >= 0) & (dist < WINDOW)
    d = jnp.clip(dist, 0, MAX_DISTANCE - 1)
    bucket = jnp.minimum(d, NUM_BUCKETS // 2)
    for thr in BUCKET_THRESHOLDS:
        bucket = bucket + (d >= thr).astype(jnp.int32)
    bias = jnp.zeros(dist.shape, F32)
    for b in range(NUM_BUCKETS):
        bias = jnp.where(bucket == b, relb_ref[b, head], bias)
    return jnp.where(valid, bias, NEG)


def _biased_scores(s, bias):
    return jnp.where(bias > 0.5 * NEG, s + bias, NEG)


def _head_in_group_columns(q, h):
    half = D_KV // 2
    assert 2 * HEAD_DIM == half and GROUP * HEAD_DIM == D_KV
    g = h // GROUP
    src = q[:, (h // 2) * half:(h // 2 + 1) * half]
    if h % 2 != g % 2:
        src = pltpu.roll(src, HEAD_DIM, 1)
    lane_half = lax.broadcasted_iota(jnp.int32, src.shape, 1) // HEAD_DIM
    placed = jnp.where(lane_half == g % 2, src, 0.0)
    zero = jnp.zeros_like(placed)
    return jnp.concatenate([placed, zero] if g // 2 == 0 else [zero, placed], axis=1)


def _group_columns(h):
    g = h // GROUP
    return slice(g * HEAD_DIM, (g + 1) * HEAD_DIM)


INPROJ_FILL_CHUNKS = 4
PROMPT_BLOCKS_PER_STEP = 4


def _inproj_attn_kernel(x_ref, g_ref, w_ref, side0_ref, side1_ref, relb_ref, sink_ref, q_ref, kp_ref, kc_ref,
                        vp_ref, vc_ref, z_ref, o_ref, side0_bf16_ref, side1_bf16_ref, kst_ref, vst_ref,
                        xn_ref, wb_ref, bias_ref, lhs_ref, *, steps_per_seq):
    j = pl.program_id(0)
    _attn_prompt_bias(j == 0, relb_ref, bias_ref)
    _inproj_norm(j == 0, x_ref, g_ref, xn_ref)
    wb_ref[...] = w_ref[...].astype(BF16)
    side0_bf16_ref[...] = side0_ref[...].astype(BF16)
    side1_bf16_ref[...] = side1_ref[...].astype(BF16)
    chunk = xn_ref.shape[0] // INPROJ_FILL_CHUNKS

    def column_chunk(c):
        rows = slice(c * chunk, (c + 1) * chunk)

        def emit():
            z_ref[rows, :] = jnp.dot(xn_ref[rows, :], wb_ref[...], preferred_element_type=F32).astype(z_ref.dtype)
        return emit

    _attn_prompt_blocks(j % steps_per_seq != 0, sink_ref, q_ref, kp_ref, kc_ref, vp_ref, vc_ref,
                        o_ref, bias_ref, lhs_ref, fillers=[column_chunk(c) for c in range(INPROJ_FILL_CHUNKS)])
    @pl.when(j % steps_per_seq == steps_per_seq - 1)
    def _():
        last = kc_ref.shape[0] - WINDOW
        kst_ref[0] = kc_ref[last:, :].astype(F32).T
        vst_ref[0] = vc_ref[last:, :].astype(F32).T


def _from_prev_mask():
    qi = lax.broadcasted_iota(jnp.int32, (WINDOW, WINDOW), 0)
    kj = lax.broadcasted_iota(jnp.int32, (WINDOW, WINDOW), 1)
    return qi, kj, kj > qi


def _attn_prompt_bias(first_step, relb_ref, bias_ref):
    @pl.when(first_step)
    def _():
        qi, kj, from_prev = _from_prev_mask()
        dist = jnp.where(from_prev, qi + WINDOW - kj, qi - kj)
        for h in range(N_HEADS):
            rows = slice(h * WINDOW, (h + 1) * WINDOW)
            bias = _bias_from_distance(dist, h, relb_ref)
            bias_ref[1, rows, :] = bias
            bias_ref[0, rows, :] = jnp.where(from_prev, NEG, bias)


def _attn_prompt_blocks(has_prev_block, sink_ref, q_ref, kp_ref, kc_ref, vp_ref, vc_ref, o_ref, bias_ref, lhs_ref,
                        fillers=()):
    _, _, from_prev = _from_prev_mask()
    k_all = jnp.concatenate([kp_ref[...], kc_ref[...]], axis=0)
    v_all = jnp.concatenate([vp_ref[...], vc_ref[...]], axis=0)

    def keys_of(blk):
        return slice(blk * WINDOW, (blk + 2) * WINDOW)

    nblk = PROMPT_BLOCKS_PER_STEP
    fillers = list(fillers)
    per_stage = -(-len(fillers) // (nblk + 2))
    scores, probs = {}, {}
    for stage in range(nblk + 2):
        if stage < nblk:
            q = q_ref[stage * WINDOW:(stage + 1) * WINDOW, :]
            scores[stage] = _block_scores(q, k_all[keys_of(stage)], lhs_ref.at[stage])
        for _ in range(min(per_stage, len(fillers))):
            fillers.pop(0)()
        if 0 <= stage - 1 < nblk:
            blk = stage - 1
            has_prev = has_prev_block.astype(jnp.int32) if blk == 0 else 1
            probs[blk] = _block_softmax(scores.pop(blk), has_prev, from_prev, sink_ref, bias_ref)
        if 0 <= stage - 2 < nblk:
            blk = stage - 2
            o_ref[blk * WINDOW:(blk + 1) * WINDOW, :] = _block_output(
                *probs.pop(blk), v_all[keys_of(blk)]).astype(o_ref.dtype)


def _block_scores(q, kk, lhs_ref):
    q = q.astype(F32) * (HEAD_DIM ** -0.5)
    for h in range(N_HEADS):
        lhs_ref[h * WINDOW:(h + 1) * WINDOW, :] = _head_in_group_columns(q, h).astype(BF16)
    return lax.dot_general(lhs_ref[...], kk, (((1,), (1,)), ((), ())), preferred_element_type=F32)


def _block_softmax(s, has_prev, from_prev, sink_ref, bias_ref):
    probs, sink_terms = [], []
    for h in range(N_HEADS):
        rows = slice(h * WINDOW, (h + 1) * WINDOW)
        sh = _biased_scores(jnp.where(from_prev, s[rows, :WINDOW], s[rows, WINDOW:]), bias_ref[has_prev, rows, :])
        m = jnp.maximum(jnp.max(sh, axis=-1, keepdims=True), sink_ref[h])
        p = jnp.exp(sh - m)
        probs.append(jnp.concatenate([jnp.where(from_prev, p, 0.0), jnp.where(from_prev, 0.0, p)],
                                     axis=1).astype(BF16))
        sink_terms.append(jnp.exp(sink_ref[h] - m))
    return probs, sink_terms


def _block_output(probs, sink_terms, vv):
    assert 2 * HEAD_DIM == LANES and GROUP % 2 == 0
    ones = jnp.ones((vv.shape[0], HEAD_DIM), BF16)
    low_half = lax.broadcasted_iota(jnp.int32, (WINDOW, LANES), 1) < HEAD_DIM
    outs = []
    for g in range(N_KV_HEADS):
        vg = vv[:, g * HEAD_DIM:(g + 1) * HEAD_DIM]
        even_heads = [g * GROUP + r for r in range(0, GROUP, 2)]
        pe = jnp.concatenate([probs[h] for h in even_heads], axis=0)
        po = jnp.concatenate([probs[h + 1] for h in even_heads], axis=0)
        oe = jnp.dot(pe, jnp.concatenate([vg, ones], axis=1), preferred_element_type=F32)
        oo = jnp.dot(po, jnp.concatenate([ones, vg], axis=1), preferred_element_type=F32)
        for k, h in enumerate(even_heads):
            a = oe[k * WINDOW:(k + 1) * WINDOW]
            b = oo[k * WINDOW:(k + 1) * WINDOW]
            out_pair = jnp.where(low_half, a, b)
            sum_pair = pltpu.roll(jnp.where(low_half, b, a), HEAD_DIM, 1)
            sink_pair = jnp.where(low_half, sink_terms[h], sink_terms[h + 1])
            outs.append(out_pair * (1.0 / (sum_pair + sink_pair)))
    return jnp.concatenate(outs, axis=1)


def _in_proj_sample_attn_prompt(x, g, w, side0, side1, zp, rel_bias, sinks, tn, batch, seq):
    tm = x.shape[0]
    n = w.shape[1]
    gate_start = n - 2 * D_MODEL
    assert gate_start % tn == 0 and n % tn == 0
    gate_block, nsteps = gate_start // tn, n // tn
    bps = PROMPT_BLOCKS_PER_STEP
    assert seq % (bps * WINDOW) == 0
    steps_per_seq = seq // (bps * WINDOW)
    attn_steps = batch * steps_per_seq
    assert attn_steps <= nsteps

    def astep(j):
        return jnp.minimum(j, attn_steps - 1)

    def cur(off, width):
        return lambda j: (astep(j), off // width)

    def prev(off, width):
        return lambda j: (astep(j) * bps - jnp.minimum(astep(j) % steps_per_seq, 1), off // width)

    def side_spec(side):
        rows = _side_cast_rows(side.shape[0], nsteps)
        return pl.BlockSpec((rows, side.shape[1]), lambda j: (jnp.minimum(j, side.shape[0] // rows - 1), 0))

    smem = pl.BlockSpec(memory_space=pltpu.SMEM)
    state_spec = pl.BlockSpec((1, D_KV, WINDOW), lambda j: (astep(j) // steps_per_seq, 0, 0))
    return pl.pallas_call(
        functools.partial(_inproj_attn_kernel, steps_per_seq=steps_per_seq),
        grid=(nsteps,),
        in_specs=[
            pl.BlockSpec((tm, D_MODEL), lambda j: (0, 0)),
            pl.BlockSpec((1, D_MODEL),lambda j: (0, 0)),
            pl.BlockSpec((D_MODEL, tn), lambda j: (0, (j + gate_block) % nsteps)),
            side_spec(side0),
            side_spec(side1),
            smem,
            smem,
            pl.BlockSpec((bps * WINDOW, D_ATTN), cur(OFF_Q, D_ATTN)),
            pl.BlockSpec((WINDOW, D_KV), prev(OFF_K, D_KV)),
            pl.BlockSpec((bps * WINDOW, D_KV), cur(OFF_K, D_KV)),
            pl.BlockSpec((WINDOW, D_KV), prev(OFF_V, D_KV)),
            pl.BlockSpec((bps * WINDOW, D_KV), cur(OFF_V, D_KV)),
        ],
        out_specs=[
            pl.BlockSpec((tm, tn), lambda j: (0, j)),
            pl.BlockSpec((bps * WINDOW, D_ATTN), lambda j: (j, 0)),
            side_spec(side0),
            side_spec(side1),
            state_spec,
            state_spec,
        ],
        out_shape=[
            jax.ShapeDtypeStruct((tm, n), BF16),
            jax.ShapeDtypeStruct((nsteps * bps * WINDOW, D_ATTN), BF16),
            jax.ShapeDtypeStruct(side0.shape, BF16),
            jax.ShapeDtypeStruct(side1.shape, BF16),
            jax.ShapeDtypeStruct((batch, D_KV, WINDOW), F32),
            jax.ShapeDtypeStruct((batch, D_KV, WINDOW), F32),
        ],
        scratch_shapes=[pltpu.VMEM((tm, D_MODEL), BF16),
                        pltpu.VMEM((D_MODEL, tn), BF16),
                        pltpu.VMEM((2, N_HEADS * WINDOW, WINDOW), F32),
                        pltpu.VMEM((bps, N_HEADS * WINDOW, D_KV), BF16)],
        compiler_params=_params("arbitrary"),
        name="in_proj_sample_attn_prompt",
    )(x, g, w, side0, side1, rel_bias, sinks, zp, zp, zp, zp, zp)


SAMPLE_BATCH_TILE = 16


def _attn_sample_kernel(relb_ref, sink_ref, q_ref, kn_ref, vn_ref, ckt_ref, cvt_ref,
                        o_ref, nk_ref, nv_ref,
                        bias1_ref, bias2_ref, sinkv_ref, z_ref, s_ref, p_ref, oh_ref, *, steps):
    bt, _, wc = nk_ref.shape
    tile_rows = bt * steps

    @pl.when(pl.program_id(0) == 0)
    def _():
        t = lax.broadcasted_iota(jnp.int32, (steps, wc), 0)
        j = lax.broadcasted_iota(jnp.int32, (steps, wc), 1)
        t2 = lax.broadcasted_iota(jnp.int32, (steps, tile_rows), 0)
        j2 = lax.broadcasted_iota(jnp.int32, (steps, tile_rows), 1)
        own = jnp.where(j2 < steps, t2 - j2, -1)
        for h in range(N_HEADS):
            sl = slice(h * steps, (h + 1) * steps)
            bias1_ref[sl, :] = _bias_from_distance(t + wc - j, h, relb_ref)
            bias2_ref[0, sl, :] = _bias_from_distance(own, h, relb_ref)
            sinkv_ref[sl, :] = jnp.full((steps, LANES), sink_ref[h], F32)
        for b in range(1, bt):
            bias2_ref[b] = pltpu.roll(bias2_ref[0], b * steps, 1)

    kn = kn_ref[...]
    vn = vn_ref[...]
    knt = kn.astype(F32).T.astype(BF16)
    vnt = vn.astype(F32).T.astype(BF16)
    new_lanes = lax.broadcasted_iota(jnp.int32, (D_KV, wc), 1) >= wc - steps
    row = lax.broadcasted_iota(jnp.int32, (tile_rows, wc), 0)
    lane = lax.broadcasted_iota(jnp.int32, (tile_rows, wc), 1)
    first_row = jnp.where(lane >= wc - steps, row - (lane - (wc - steps)), -1)

    def slid(cache_t, new_t, b):
        place = jnp.where(first_row == b * steps, 1.0, 0.0).astype(BF16)
        new = jnp.dot(new_t, place, preferred_element_type=F32)
        return jnp.where(new_lanes, new, pltpu.roll(cache_t, wc - steps, 1))

    q = q_ref[...].astype(F32) * (HEAD_DIM ** -0.5)
    for h in range(N_HEADS):
        z_ref[:, h * steps:(h + 1) * steps, :] = _head_in_group_columns(q, h).reshape(bt, steps, D_KV)

    for b in range(bt):
        ckt = ckt_ref[b]
        nk_ref[b] = slid(ckt, knt, b)
        keys_t = jnp.concatenate([ckt.astype(BF16), knt], axis=1)
        s_ref[b] = jnp.dot(z_ref[b].astype(BF16), keys_t, preferred_element_type=F32)

    bias = jnp.concatenate([jnp.broadcast_to(bias1_ref[...], (bt,) + bias1_ref.shape), bias2_ref[...]], axis=2)
    s = _biased_scores(s_ref[...], bias)
    sink = sinkv_ref[...][None]
    m = jnp.maximum(jnp.max(s, axis=-1, keepdims=True), sink)
    p_ref[...] = jnp.exp(s - jnp.concatenate([m] * (s.shape[-1] // LANES), axis=-1)).astype(BF16)
    sink_term = jnp.exp(sink - m)

    vnt_ones = jnp.concatenate([vnt, jnp.ones((LANES, tile_rows), BF16)], axis=0)
    for b in range(bt):
        cvt = cvt_ref[b]
        nv_ref[b] = slid(cvt, vnt, b)
        cvt_ones = jnp.concatenate([cvt.astype(BF16), jnp.ones((LANES, wc), BF16)], axis=0)
        oh_ref[b] = lax.dot_general(p_ref[b], jnp.concatenate([cvt_ones, vnt_ones], axis=1),
                                    (((1,), (1,)), ((), ())), preferred_element_type=F32)

    inv = 1.0 / (oh_ref[:, :, D_KV:] + sink_term)
    outs = []
    for h in range(N_HEADS):
        rows = slice(h * steps, (h + 1) * steps)
        outs.append((oh_ref[:, rows, _group_columns(h)] * inv[:, rows, :HEAD_DIM]).reshape(tile_rows, HEAD_DIM))
    o_ref[...] = jnp.concatenate(outs, axis=1).astype(o_ref.dtype)


def _attn_sample(z, cache_kt, cache_vt, rel_bias, sinks, batch, steps):
    wc = cache_kt.shape[2]
    bt = SAMPLE_BATCH_TILE
    rows = bt * steps
    smem = pl.BlockSpec(memory_space=pltpu.SMEM)
    cache_t_spec = pl.BlockSpec((bt, D_KV, wc), lambda i: (i, 0, 0))
    return pl.pallas_call(
        functools.partial(_attn_sample_kernel, steps=steps),
        grid=(batch // bt,),
        in_specs=[
            smem,
            smem,
            pl.BlockSpec((rows, D_ATTN), lambda i: (i, OFF_Q // D_ATTN)),
            pl.BlockSpec((rows, D_KV), lambda i: (i, OFF_K // D_KV)),
            pl.BlockSpec((rows, D_KV), lambda i: (i, OFF_V // D_KV)),
            cache_t_spec,
            cache_t_spec,
        ],
        out_specs=[
            pl.BlockSpec((rows, D_ATTN), lambda i: (i, 0)),
            cache_t_spec,
            cache_t_spec,
        ],
        out_shape=[
            jax.ShapeDtypeStruct((batch * steps, D_ATTN), BF16),
            jax.ShapeDtypeStruct((batch, D_KV, wc), F32),
            jax.ShapeDtypeStruct((batch, D_KV, wc), F32),
        ],
        scratch_shapes=[
            pltpu.VMEM((N_HEADS * steps, wc), F32),
            pltpu.VMEM((bt, N_HEADS * steps, rows), F32),
            pltpu.VMEM((N_HEADS * steps, LANES), F32),
            pltpu.VMEM((bt, N_HEADS * steps, D_KV), F32),
            pltpu.VMEM((bt, N_HEADS * steps, wc + rows), F32),
            pltpu.VMEM((bt, N_HEADS * steps, wc + rows), BF16),
            pltpu.VMEM((bt, N_HEADS * steps, D_KV + LANES), F32),
        ],
        compiler_params=_params("arbitrary"),
        name="attn_sample",
    )(rel_bias, sinks, z, z, z, cache_kt, cache_vt)


CARRY_ROWS = 8


def _gated_conv(b, wc_ref, u, u1, u2):
    conv = wc_ref[0:1, :] * u2 + wc_ref[1:2, :] * u1 + wc_ref[2:3, :] * u
    return (b.astype(F32) * conv).astype(BF16)


def _mix_tail(conv_branch_input, at_ref, gc_ref, ga_ref, x_ref, g_ref, wco_ref, wao_ref, wo_ref, h_ref, hn_ref):
    y_conv = jnp.dot(conv_branch_input(), wco_ref[...], preferred_element_type=F32)
    y_attn = jnp.dot(at_ref[...], wao_ref[...], preferred_element_type=F32)
    merged = (jax.nn.sigmoid(gc_ref[...].astype(F32)) * y_conv
              + jax.nn.sigmoid(ga_ref[...].astype(F32)) * y_attn)
    h = x_ref[...] + jnp.dot(merged.astype(BF16), wo_ref[...], preferred_element_type=F32)
    h_ref[...] = h
    hn_ref[...] = _rms_rows(h, g_ref[0:1, :]).astype(hn_ref.dtype)


def _mix_prompt_kernel(b_ref, c_ref, hc_ref, wc_ref, at_ref, gc_ref, ga_ref, x_ref, g_ref,
                       wco_ref, wao_ref, wo_ref, side0_ref, side1_ref, side2_ref,
                       h_ref, hn_ref, st_ref, side0_bf16_ref, side1_bf16_ref, side2_bf16_ref,
                       us_ref, *, tiles_per_seq):
    tm = x_ref.shape[0]
    side0_bf16_ref[...] = side0_ref[...].astype(BF16)
    side1_bf16_ref[...] = side1_ref[...].astype(BF16)
    side2_bf16_ref[...] = side2_ref[...].astype(BF16)

    def conv_branch_input():
        u = c_ref[...].astype(F32) * hc_ref[...].astype(F32)
        st_ref[...] = u[tm - (CONV_WIDTH - 1):, :]
        carry = jnp.where(pl.program_id(0) % tiles_per_seq == 0, 0.0, us_ref[...])
        u_ext = jnp.concatenate([carry, u], axis=0)
        us_ref[...] = u[tm - CARRY_ROWS:, :]
        return _gated_conv(b_ref[...], wc_ref, u,
                           pltpu.roll(u_ext, 1, 0)[CARRY_ROWS:, :], pltpu.roll(u_ext, 2, 0)[CARRY_ROWS:, :])

    _mix_tail(conv_branch_input, at_ref, gc_ref, ga_ref, x_ref, g_ref, wco_ref, wao_ref, wo_ref, h_ref, hn_ref)


def _mix_sample_kernel(b_ref, c_ref, hc_ref, st_ref, wc_ref, at_ref, gc_ref, ga_ref, x_ref, g_ref,
                       wco_ref, wao_ref, wo_ref, h_ref, hn_ref, new_st_ref, us_ref, *, steps):
    tm = x_ref.shape[0]
    nseq = tm // steps
    keep = CONV_WIDTH - 1

    def conv_branch_input():
        u = c_ref[...].astype(F32) * hc_ref[...].astype(F32)
        us_ref[:, CARRY_ROWS - keep:CARRY_ROWS, :] = st_ref[...]
        us_ref[:, CARRY_ROWS:, :] = u.reshape(nseq, steps, D_CONV)
        new_st_ref[...] = us_ref[:, CARRY_ROWS + steps - keep:, :]

        def delayed(d):
            return us_ref[:, CARRY_ROWS - d:CARRY_ROWS - d + steps, :].reshape(tm, D_CONV)

        return _gated_conv(b_ref[...], wc_ref, u, delayed(1), delayed(2))

    _mix_tail(conv_branch_input, at_ref, gc_ref, ga_ref, x_ref, g_ref, wco_ref, wao_ref, wo_ref, h_ref, hn_ref)


def _mix_specs(tm):
    resident = functools.partial(pl.BlockSpec, index_map=lambda i: (0, 0), pipeline_mode=pl.Buffered(1))
    row_spec = pl.BlockSpec((tm, D_MODEL), lambda i: (i, 0))

    def zcol(off, width):
        return pl.BlockSpec((tm, width), lambda i: (i, off // width))

    conv_in = [zcol(OFF_B, D_CONV), zcol(OFF_C, D_CONV), zcol(OFF_H, D_CONV)]
    rest_in = [
        pl.BlockSpec((CONV_WIDTH, D_CONV), lambda i: (0, 0)),
        pl.BlockSpec((tm, D_ATTN), lambda i: (i, 0)),
        zcol(OFF_GC, D_MODEL),
        zcol(OFF_GA, D_MODEL),
        row_spec,
        pl.BlockSpec((1, D_MODEL),lambda i: (0, 0)),
        resident((D_CONV, D_MODEL)),
        resident((D_ATTN, D_MODEL)),
        resident((D_MODEL, D_MODEL)),
    ]
    return conv_in, rest_in, row_spec


def _mix_prompt(z, attn, x, w_conv, g_ffn, wco, wao, wo, sides, tm, batch, seq):
    m = x.shape[0]
    assert seq % tm == 0 and tm >= CARRY_ROWS
    tiles_per_seq = seq // tm
    nsteps = m // tm
    conv_in, rest_in, row_spec = _mix_specs(tm)

    def side_spec(side):
        rows = _side_cast_rows(side.shape[0], nsteps)
        return pl.BlockSpec((rows, side.shape[1]), lambda i: (jnp.minimum(i, side.shape[0] // rows - 1), 0))

    side_specs = [side_spec(s) for s in sides]
    return pl.pallas_call(
        functools.partial(_mix_prompt_kernel, tiles_per_seq=tiles_per_seq),
        grid=(nsteps,),
        in_specs=conv_in + rest_in + side_specs,
        out_specs=[row_spec, row_spec,
                   pl.BlockSpec((None, CONV_WIDTH - 1, D_CONV), lambda i: (i // tiles_per_seq, 0, 0))] + side_specs,
        out_shape=[jax.ShapeDtypeStruct((m, D_MODEL), F32), jax.ShapeDtypeStruct((m, D_MODEL), BF16),
                   jax.ShapeDtypeStruct((batch, CONV_WIDTH - 1, D_CONV), F32)]
        + [jax.ShapeDtypeStruct(s.shape, BF16) for s in sides],
        scratch_shapes=[pltpu.VMEM((CARRY_ROWS, D_CONV), F32)],
        compiler_params=_params("arbitrary"),
        name="mix_prompt",
    )(z, z, z, w_conv, attn, z, z, x, g_ffn, wco, wao, wo, *sides)


def _mix_sample(z, state, attn, x, w_conv, g_ffn, wco, wao, wo, tm, steps):
    m = x.shape[0]
    assert tm % steps == 0 and CONV_WIDTH - 1 <= min(steps, CARRY_ROWS)
    conv_in, rest_in, row_spec = _mix_specs(tm)
    state_spec = pl.BlockSpec((tm // steps, CONV_WIDTH - 1, D_CONV), lambda i: (i, 0, 0))
    return pl.pallas_call(
        functools.partial(_mix_sample_kernel, steps=steps),
        grid=(m // tm,),
        in_specs=conv_in + [state_spec] + rest_in,
        out_specs=[row_spec, row_spec, state_spec],
        out_shape=[jax.ShapeDtypeStruct((m, D_MODEL), F32), jax.ShapeDtypeStruct((m, D_MODEL), BF16),
                   jax.ShapeDtypeStruct(state.shape, F32)],
        scratch_shapes=[pltpu.VMEM((tm // steps, CARRY_ROWS + steps, D_CONV), F32)],
        compiler_params=_params("arbitrary"),
        name="mix_sample",
    )(z, z, z, state, w_conv, attn, z, z, x, g_ffn, wco, wao, wo)


def _ffn_kernel(hn_ref, h_ref, gf_ref, wg_ref, wu_ref, wd_ref, y_ref, *, final_norm):
    j = pl.program_id(1)

    hn = hn_ref[...]
    a = (jax.nn.silu(jnp.dot(hn, wg_ref[...], preferred_element_type=F32))
         * jnp.dot(hn, wu_ref[...], preferred_element_type=F32))
    acc = jnp.where(j == 0, h_ref[...], y_ref[...])
    y_ref[...] = acc + jnp.dot(a.astype(BF16), wd_ref[...], preferred_element_type=F32)

    if final_norm:
        @pl.when(j == pl.num_programs(1) - 1)
        def _():
            def body(r, carry):
                rows = pl.ds(pl.multiple_of(r * NORM_ROWS, NORM_ROWS), NORM_ROWS)
                y_ref[rows, :] = _rms_rows(y_ref[rows, :], gf_ref[0:1, :])
                return carry

            lax.fori_loop(0, y_ref.shape[0] // NORM_ROWS, body, 0)


def _ffn(hn, h, g_final, wg, wu, wd, tm, tf, final_norm):
    m = h.shape[0]
    return pl.pallas_call(
        functools.partial(_ffn_kernel, final_norm=final_norm),
        grid=(m // tm, D_FF // tf),
        in_specs=[
            pl.BlockSpec((tm, D_MODEL), lambda i, j: (i, 0)),
            pl.BlockSpec((tm, D_MODEL), lambda i, j: (i, 0)),
            pl.BlockSpec((1, D_MODEL),lambda i, j: (0, 0)),
            pl.BlockSpec((D_MODEL, tf), lambda i, j: (0, j)),
            pl.BlockSpec((D_MODEL, tf), lambda i, j: (0, j)),
            pl.BlockSpec((tf, D_MODEL), lambda i, j: (j, 0)),
        ],
        out_specs=pl.BlockSpec((tm, D_MODEL), lambda i, j: (i, 0)),
        out_shape=jax.ShapeDtypeStruct((m, D_MODEL), F32),
        compiler_params=_params("arbitrary", "arbitrary"),
        name="ffn",
    )(hn, h, g_final, wg, wu, wd)


IN_PROJ_TM_PROMPT = 2048
IN_PROJ_TN = 512
MIX_TM = 256
FFN_TM = 1024
FFN_TM_SAMPLE = 512
FFN_TF = 512


def kernel(x_prompt, x_sample, cache_k, cache_v, state_conv, rel_bias, w_in, w_conv, w_conv_out, sinks,
           w_attn_out, w_o, g_mix, g_ffn, w_gate, w_up, w_down, g_final):
    depth = w_in.shape[0]
    batch, seq, _ = x_prompt.shape
    dec_batch, steps, _ = x_sample.shape
    wc = cache_k.shape[2]
    assert seq % WINDOW == 0 and wc == WINDOW and dec_batch % SAMPLE_BATCH_TILE == 0

    hp = x_prompt.reshape(batch * seq, D_MODEL)
    hs = x_sample.reshape(dec_batch * steps, D_MODEL)
    g_final2 = g_final.reshape(1, D_MODEL)
    kp_l, vp_l, cp_l, ks_l, vs_l, cs_l = [], [], [], [], [], []
    for l in range(depth):
        final = l == depth - 1
        lw = {
            "w_in": w_in[l],
            "g_ffn": g_ffn[l].reshape(1, D_MODEL),
        }
        g_mix_l = g_mix[l].reshape(1, D_MODEL)

        zp, lw["wo"] = _in_proj(hp, g_mix_l, lw["w_in"], w_o[l], IN_PROJ_TM_PROMPT, IN_PROJ_TN)
        zs, attn_p, lw["wco"], lw["wao"], kp_t, vp_t = _in_proj_sample_attn_prompt(
            hs, g_mix_l, lw["w_in"], w_conv_out[l], w_attn_out[l], zp, rel_bias, sinks[l], IN_PROJ_TN, batch, seq)
        hp, hnp, conv_p, lw["wg"], lw["wu"], lw["wd"] = _mix_prompt(
            zp, attn_p, hp, w_conv[l], lw["g_ffn"], lw["wco"], lw["wao"], lw["wo"],
            (w_gate[l], w_up[l], w_down[l]), MIX_TM, batch, seq)
        hp = _ffn(hnp, hp, g_final2, lw["wg"], lw["wu"], lw["wd"], FFN_TM, FFN_TF, final)
        kp_l.append(jnp.transpose(kp_t.reshape(batch, N_KV_HEADS, HEAD_DIM, WINDOW), (0, 3, 1, 2)))
        vp_l.append(jnp.transpose(vp_t.reshape(batch, N_KV_HEADS, HEAD_DIM, WINDOW), (0, 3, 1, 2)))
        cp_l.append(conv_p)

        cache_kt = jnp.transpose(cache_k[l], (0, 2, 3, 1)).reshape(dec_batch, D_KV, wc)
        cache_vt = jnp.transpose(cache_v[l], (0, 2, 3, 1)).reshape(dec_batch, D_KV, wc)
        attn_s, nk, nv = _attn_sample(zs, cache_kt, cache_vt, rel_bias, sinks[l], dec_batch, steps)
        hs, hns, conv_s = _mix_sample(zs, state_conv[l], attn_s, hs, w_conv[l], lw["g_ffn"], lw["wco"], lw["wao"],
                                      lw["wo"], MIX_TM, steps)
        hs = _ffn(hns, hs, g_final2, lw["wg"], lw["wu"], lw["wd"], FFN_TM_SAMPLE, FFN_TF, final)
        ks_l.append(jnp.transpose(nk.reshape(dec_batch, N_KV_HEADS, HEAD_DIM, wc), (0, 3, 1, 2)))
        vs_l.append(jnp.transpose(nv.reshape(dec_batch, N_KV_HEADS, HEAD_DIM, wc), (0, 3, 1, 2)))
        cs_l.append(conv_s)

    return (hp.reshape(batch, seq, D_MODEL), hs.reshape(dec_batch, steps, D_MODEL),
            jnp.stack(kp_l), jnp.stack(vp_l), jnp.stack(cp_l),
            jnp.stack(ks_l), jnp.stack(vs_l), jnp.stack(cs_l))
```

```python
import functools
import math

import numpy as np
import jax
import jax.numpy as jnp
from jax import lax
from jax.experimental import pallas as pl
from jax.experimental.pallas import tpu as pltpu

D_MODEL = 2048
D_CONV = D_MODEL // 2
CONV_WIDTH = 3
HEAD_DIM = 64
N_HEADS = (D_MODEL // 2) // HEAD_DIM
N_KV_HEADS = N_HEADS // 4
GROUP = N_HEADS // N_KV_HEADS
D_ATTN = N_HEADS * HEAD_DIM
D_KV = N_KV_HEADS * HEAD_DIM
WINDOW = 128
NUM_BUCKETS = 32
MAX_DISTANCE = 128
D_FF = -(-8 * D_MODEL // (3 * 256)) * 256
EPS = 1e-6
D_IN_PROJ = 3 * D_CONV + D_ATTN + 2 * D_KV + 2 * D_MODEL

OFF_GC = 0
OFF_GA = D_MODEL
OFF_B = 2 * D_MODEL
OFF_C = OFF_B + D_CONV
OFF_H = OFF_C + D_CONV
OFF_Q = OFF_H + D_CONV
OFF_K = OFF_Q + D_ATTN
OFF_V = OFF_K + D_KV

LANES = 128
NEG = -1e30
F32 = jnp.float32
BF16 = jnp.bfloat16
VMEM_LIMIT = 60 * 1024 * 1024


def _bucket_thresholds():
    max_exact = NUM_BUCKETS // 2
    d = np.arange(MAX_DISTANCE)
    ratio = np.log(np.maximum(d, 1).astype(np.float32) / np.float32(max_exact)) / np.float32(
        math.log(MAX_DISTANCE / max_exact))
    large = max_exact + (ratio * np.float32(NUM_BUCKETS - max_exact)).astype(np.int32)
    large = np.minimum(large, NUM_BUCKETS - 1)
    return [int(np.min(d[(d >= max_exact) & (large >= b)])) for b in range(max_exact + 1, NUM_BUCKETS)]


BUCKET_THRESHOLDS = _bucket_thresholds()


def _params(*semantics):
    return pltpu.CompilerParams(dimension_semantics=semantics, vmem_limit_bytes=VMEM_LIMIT)


def _rms_rows(x, g):
    ms = jnp.mean(x * x, axis=-1, keepdims=True)
    return x * lax.rsqrt(ms + EPS) * g


NORM_ROWS = 256


def _inproj_kernel(x_ref, g_ref, w_ref, side_ref, z_ref, side_bf16_ref, xn_ref):
    _inproj_norm(pl.program_id(1) == 0, x_ref, g_ref, xn_ref)
    _inproj_columns(w_ref, z_ref, xn_ref)
    side_bf16_ref[...] = side_ref[...].astype(BF16)


BF16_ROWS = 16


def _side_cast_rows(rows, nsteps):
    per_step = -(-rows // nsteps)
    per_step = -(-per_step // BF16_ROWS) * BF16_ROWS
    assert rows % per_step == 0
    return per_step


def _inproj_norm(first_column_step, x_ref, g_ref, xn_ref):
    @pl.when(first_column_step)
    def _():
        def body(r, carry):
            rows = pl.ds(pl.multiple_of(r * NORM_ROWS, NORM_ROWS), NORM_ROWS)
            xn_ref[rows, :] = _rms_rows(x_ref[rows, :], g_ref[0:1, :]).astype(xn_ref.dtype)
            return carry

        lax.fori_loop(0, x_ref.shape[0] // NORM_ROWS, body, 0)


def _inproj_columns(w_ref, z_ref, xn_ref):
    z_ref[...] = jnp.dot(xn_ref[...], w_ref[...].astype(BF16), preferred_element_type=F32).astype(z_ref.dtype)


def _in_proj(x, g, w, side, tm, tn):
    m = x.shape[0]
    n = w.shape[1]
    gate_start = n - 2 * D_MODEL
    assert gate_start % tn == 0 and n % tn == 0
    gate_block, nblocks = gate_start // tn, n // tn
    side_rows = _side_cast_rows(side.shape[0], (m // tm) * nblocks)
    side_spec = pl.BlockSpec((side_rows, side.shape[1]),
                             lambda i, j: (jnp.minimum(i * nblocks + j, side.shape[0] // side_rows - 1), 0))
    return pl.pallas_call(
        _inproj_kernel,
        grid=(m // tm, nblocks),
        in_specs=[
            pl.BlockSpec((tm, D_MODEL), lambda i, j: (i, 0)),
            pl.BlockSpec((1, D_MODEL),lambda i, j: (0, 0)),
            pl.BlockSpec((D_MODEL, tn), lambda i, j: (0, (j + gate_block) % nblocks)),
            side_spec,
        ],
        out_specs=[pl.BlockSpec((tm, tn), lambda i, j: (i, j)), side_spec],
        out_shape=[jax.ShapeDtypeStruct((m, n), BF16), jax.ShapeDtypeStruct(side.shape, BF16)],
        scratch_shapes=[pltpu.VMEM((tm, D_MODEL), BF16)],
        compiler_params=_params("arbitrary", "arbitrary"),
        name="in_proj",
    )(x, g, w, side)


def _bias_from_distance(dist, head, relb_ref):
    valid = (dist >= 0) & (dist < WINDOW)
    d = jnp.clip(dist, 0, MAX_DISTANCE - 1)
    bucket = jnp.minimum(d, NUM_BUCKETS // 2)
    for thr in BUCKET_THRESHOLDS:
        bucket = bucket + (d >= thr).astype(jnp.int32)
    bias = jnp.zeros(dist.shape, F32)
    for b in range(NUM_BUCKETS):
        bias = jnp.where(bucket == b, relb_ref[b, head], bias)
    return jnp.where(valid, bias, NEG)


def _biased_scores(s, bias):
    return jnp.where(bias > 0.5 * NEG, s + bias, NEG)


def _head_in_group_columns(q, h):
    half = D_KV // 2
    assert 2 * HEAD_DIM == half and GROUP * HEAD_DIM == D_KV
    g = h // GROUP
    src = q[:, (h // 2) * half:(h // 2 + 1) * half]
    if h % 2 != g % 2:
        src = pltpu.roll(src, HEAD_DIM, 1)
    lane_half = lax.broadcasted_iota(jnp.int32, src.shape, 1) // HEAD_DIM
    placed = jnp.where(lane_half == g % 2, src, 0.0)
    zero = jnp.zeros_like(placed)
    return jnp.concatenate([placed, zero] if g // 2 == 0 else [zero, placed], axis=1)


def _group_columns(h):
    g = h // GROUP
    return slice(g * HEAD_DIM, (g + 1) * HEAD_DIM)


INPROJ_FILL_CHUNKS = 4
PROMPT_BLOCKS_PER_STEP = 4


def _inproj_attn_kernel(x_ref, g_ref, w_ref, side0_ref, side1_ref, relb_ref, sink_ref, q_ref, kp_ref, kc_ref,
                        vp_ref, vc_ref, z_ref, o_ref, side0_bf16_ref, side1_bf16_ref, kst_ref, vst_ref,
                        xn_ref, wb_ref, bias_ref, lhs_ref, *, steps_per_seq):
    j = pl.program_id(0)
    _attn_prompt_bias(j == 0, relb_ref, bias_ref)
    _inproj_norm(j == 0, x_ref, g_ref, xn_ref)
    wb_ref[...] = w_ref[...].astype(BF16)
    side0_bf16_ref[...] = side0_ref[...].astype(BF16)
    side1_bf16_ref[...] = side1_ref[...].astype(BF16)
    chunk = xn_ref.shape[0] // INPROJ_FILL_CHUNKS

    def column_chunk(c):
        rows = slice(c * chunk, (c + 1) * chunk)

        def emit():
            z_ref[rows, :] = jnp.dot(xn_ref[rows, :], wb_ref[...], preferred_element_type=F32).astype(z_ref.dtype)
        return emit

    _attn_prompt_blocks(j % steps_per_seq != 0, sink_ref, q_ref, kp_ref, kc_ref, vp_ref, vc_ref,
                        o_ref, bias_ref, lhs_ref, fillers=[column_chunk(c) for c in range(INPROJ_FILL_CHUNKS)])
    @pl.when(j % steps_per_seq == steps_per_seq - 1)
    def _():
        last = kc_ref.shape[0] - WINDOW
        kst_ref[0] = kc_ref[last:, :].astype(F32).T
        vst_ref[0] = vc_ref[last:, :].astype(F32).T


def _from_prev_mask():
    qi = lax.broadcasted_iota(jnp.int32, (WINDOW, WINDOW), 0)
    kj = lax.broadcasted_iota(jnp.int32, (WINDOW, WINDOW), 1)
    return qi, kj, kj > qi


def _attn_prompt_bias(first_step, relb_ref, bias_ref):
    @pl.when(first_step)
    def _():
        qi, kj, from_prev = _from_prev_mask()
        dist = jnp.where(from_prev, qi + WINDOW - kj, qi - kj)
        for h in range(N_HEADS):
            rows = slice(h * WINDOW, (h + 1) * WINDOW)
            bias = _bias_from_distance(dist, h, relb_ref)
            bias_ref[1, rows, :] = bias
            bias_ref[0, rows, :] = jnp.where(from_prev, NEG, bias)


def _attn_prompt_blocks(has_prev_block, sink_ref, q_ref, kp_ref, kc_ref, vp_ref, vc_ref, o_ref, bias_ref, lhs_ref,
                        fillers=()):
    _, _, from_prev = _from_prev_mask()
    k_all = jnp.concatenate([kp_ref[...], kc_ref[...]], axis=0)
    v_all = jnp.concatenate([vp_ref[...], vc_ref[...]], axis=0)

    def keys_of(blk):
        return slice(blk * WINDOW, (blk + 2) * WINDOW)

    nblk = PROMPT_BLOCKS_PER_STEP
    fillers = list(fillers)
    per_stage = -(-len(fillers) // (nblk + 2))
    scores, probs = {}, {}
    for stage in range(nblk + 2):
        if stage < nblk:
            q = q_ref[stage * WINDOW:(stage + 1) * WINDOW, :]
            scores[stage] = _block_scores(q, k_all[keys_of(stage)], lhs_ref.at[stage])
        for _ in range(min(per_stage, len(fillers))):
            fillers.pop(0)()
        if 0 <= stage - 1 < nblk:
            blk = stage - 1
            has_prev = has_prev_block.astype(jnp.int32) if blk == 0 else 1
            probs[blk] = _block_softmax(scores.pop(blk), has_prev, from_prev, sink_ref, bias_ref)
        if 0 <= stage - 2 < nblk:
            blk = stage - 2
            o_ref[blk * WINDOW:(blk + 1) * WINDOW, :] = _block_output(
                *probs.pop(blk), v_all[keys_of(blk)]).astype(o_ref.dtype)


def _block_scores(q, kk, lhs_ref):
    q = q.astype(F32) * (HEAD_DIM ** -0.5)
    for h in range(N_HEADS):
        lhs_ref[h * WINDOW:(h + 1) * WINDOW, :] = _head_in_group_columns(q, h).astype(BF16)
    return lax.dot_general(lhs_ref[...], kk, (((1,), (1,)), ((), ())), preferred_element_type=F32)


def _block_softmax(s, has_prev, from_prev, sink_ref, bias_ref):
    probs, sink_terms = [], []
    for h in range(N_HEADS):
        rows = slice(h * WINDOW, (h + 1) * WINDOW)
        sh = _biased_scores(jnp.where(from_prev, s[rows, :WINDOW], s[rows, WINDOW:]), bias_ref[has_prev, rows, :])
        m = jnp.maximum(jnp.max(sh, axis=-1, keepdims=True), sink_ref[h])
        p = jnp.exp(sh - m)
        probs.append(jnp.concatenate([jnp.where(from_prev, p, 0.0), jnp.where(from_prev, 0.0, p)],
                                     axis=1).astype(BF16))
        sink_terms.append(jnp.exp(sink_ref[h] - m))
    return probs, sink_terms


def _block_output(probs, sink_terms, vv):
    assert 2 * HEAD_DIM == LANES and GROUP % 2 == 0
    ones = jnp.ones((vv.shape[0], HEAD_DIM), BF16)
    low_half = lax.broadcasted_iota(jnp.int32, (WINDOW, LANES), 1) < HEAD_DIM
    outs = []
    for g in range(N_KV_HEADS):
        vg = vv[:, g * HEAD_DIM:(g + 1) * HEAD_DIM]
        even_heads = [g * GROUP + r for r in range(0, GROUP, 2)]
        pe = jnp.concatenate([probs[h] for h in even_heads], axis=0)
        po = jnp.concatenate([probs[h + 1] for h in even_heads], axis=0)
        oe = jnp.dot(pe, jnp.concatenate([vg, ones], axis=1), preferred_element_type=F32)
        oo = jnp.dot(po, jnp.concatenate([ones, vg], axis=1), preferred_element_type=F32)
        for k, h in enumerate(even_heads):
            a = oe[k * WINDOW:(k + 1) * WINDOW]
            b = oo[k * WINDOW:(k + 1) * WINDOW]
            out_pair = jnp.where(low_half, a, b)
            sum_pair = pltpu.roll(jnp.where(low_half, b, a), HEAD_DIM, 1)
            sink_pair = jnp.where(low_half, sink_terms[h], sink_terms[h + 1])
            outs.append(out_pair * (1.0 / (sum_pair + sink_pair)))
    return jnp.concatenate(outs, axis=1)


def _in_proj_sample_attn_prompt(x, g, w, side0, side1, zp, rel_bias, sinks, tn, batch, seq):
    tm = x.shape[0]
    n = w.shape[1]
    gate_start = n - 2 * D_MODEL
    assert gate_start % tn == 0 and n % tn == 0
    gate_block, nsteps = gate_start // tn, n // tn
    bps = PROMPT_BLOCKS_PER_STEP
    assert seq % (bps * WINDOW) == 0
    steps_per_seq = seq // (bps * WINDOW)
    attn_steps = batch * steps_per_seq
    assert attn_steps <= nsteps

    def astep(j):
        return jnp.minimum(j, attn_steps - 1)

    def cur(off, width):
        return lambda j: (astep(j), off // width)

    def prev(off, width):
        return lambda j: (astep(j) * bps - jnp.minimum(astep(j) % steps_per_seq, 1), off // width)

    def side_spec(side):
        rows = _side_cast_rows(side.shape[0], nsteps)
        return pl.BlockSpec((rows, side.shape[1]), lambda j: (jnp.minimum(j, side.shape[0] // rows - 1), 0))

    smem = pl.BlockSpec(memory_space=pltpu.SMEM)
    state_spec = pl.BlockSpec((1, D_KV, WINDOW), lambda j: (astep(j) // steps_per_seq, 0, 0))
    return pl.pallas_call(
        functools.partial(_inproj_attn_kernel, steps_per_seq=steps_per_seq),
        grid=(nsteps,),
        in_specs=[
            pl.BlockSpec((tm, D_MODEL), lambda j: (0, 0)),
            pl.BlockSpec((1, D_MODEL),lambda j: (0, 0)),
            pl.BlockSpec((D_MODEL, tn), lambda j: (0, (j + gate_block) % nsteps)),
            side_spec(side0),
            side_spec(side1),
            smem,
            smem,
            pl.BlockSpec((bps * WINDOW, D_ATTN), cur(OFF_Q, D_ATTN)),
            pl.BlockSpec((WINDOW, D_KV), prev(OFF_K, D_KV)),
            pl.BlockSpec((bps * WINDOW, D_KV), cur(OFF_K, D_KV)),
            pl.BlockSpec((WINDOW, D_KV), prev(OFF_V, D_KV)),
            pl.BlockSpec((bps * WINDOW, D_KV), cur(OFF_V, D_KV)),
        ],
        out_specs=[
            pl.BlockSpec((tm, tn), lambda j: (0, j)),
            pl.BlockSpec((bps * WINDOW, D_ATTN), lambda j: (j, 0)),
            side_spec(side0),
            side_spec(side1),
            state_spec,
            state_spec,
        ],
        out_shape=[
            jax.ShapeDtypeStruct((tm, n), BF16),
            jax.ShapeDtypeStruct((nsteps * bps * WINDOW, D_ATTN), BF16),
            jax.ShapeDtypeStruct(side0.shape, BF16),
            jax.ShapeDtypeStruct(side1.shape, BF16),
            jax.ShapeDtypeStruct((batch, D_KV, WINDOW), F32),
            jax.ShapeDtypeStruct((batch, D_KV, WINDOW), F32),
        ],
        scratch_shapes=[pltpu.VMEM((tm, D_MODEL), BF16),
                        pltpu.VMEM((D_MODEL, tn), BF16),
                        pltpu.VMEM((2, N_HEADS * WINDOW, WINDOW), F32),
                        pltpu.VMEM((bps, N_HEADS * WINDOW, D_KV), BF16)],
        compiler_params=_params("arbitrary"),
        name="in_proj_sample_attn_prompt",
    )(x, g, w, side0, side1, rel_bias, sinks, zp, zp, zp, zp, zp)


SAMPLE_BATCH_TILE = 16


def _attn_sample_kernel(relb_ref, sink_ref, q_ref, kn_ref, vn_ref, ckt_ref, cvt_ref,
                        o_ref, nk_ref, nv_ref,
                        bias1_ref, bias2_ref, sinkv_ref, z_ref, s_ref, p_ref, oh_ref, *, steps):
    bt, _, wc = nk_ref.shape
    tile_rows = bt * steps

    @pl.when(pl.program_id(0) == 0)
    def _():
        t = lax.broadcasted_iota(jnp.int32, (steps, wc), 0)
        j = lax.broadcasted_iota(jnp.int32, (steps, wc), 1)
        t2 = lax.broadcasted_iota(jnp.int32, (steps, tile_rows), 0)
        j2 = lax.broadcasted_iota(jnp.int32, (steps, tile_rows), 1)
        own = jnp.where(j2 < steps, t2 - j2, -1)
        for h in range(N_HEADS):
            sl = slice(h * steps, (h + 1) * steps)
            bias1_ref[sl, :] = _bias_from_distance(t + wc - j, h, relb_ref)
            bias2_ref[0, sl, :] = _bias_from_distance(own, h, relb_ref)
            sinkv_ref[sl, :] = jnp.full((steps, LANES), sink_ref[h], F32)
        for b in range(1, bt):
            bias2_ref[b] = pltpu.roll(bias2_ref[0], b * steps, 1)

    kn = kn_ref[...]
    vn = vn_ref[...]
    knt = kn.astype(F32).T.astype(BF16)
    vnt = vn.astype(F32).T.astype(BF16)
    new_lanes = lax.broadcasted_iota(jnp.int32, (D_KV, wc), 1) >= wc - steps
    row = lax.broadcasted_iota(jnp.int32, (tile_rows, wc), 0)
    lane = lax.broadcasted_iota(jnp.int32, (tile_rows, wc), 1)
    first_row = jnp.where(lane >= wc - steps, row - (lane - (wc - steps)), -1)

    def slid(cache_t, new_t, b):
        place = jnp.where(first_row == b * steps, 1.0, 0.0).astype(BF16)
        new = jnp.dot(new_t, place, preferred_element_type=F32)
        return jnp.where(new_lanes, new, pltpu.roll(cache_t, wc - steps, 1))

    q = q_ref[...].astype(F32) * (HEAD_DIM ** -0.5)
    for h in range(N_HEADS):
        z_ref[:, h * steps:(h + 1) * steps, :] = _head_in_group_columns(q, h).reshape(bt, steps, D_KV)

    for b in range(bt):
        ckt = ckt_ref[b]
        nk_ref[b] = slid(ckt, knt, b)
        keys_t = jnp.concatenate([ckt.astype(BF16), knt], axis=1)
        s_ref[b] = jnp.dot(z_ref[b].astype(BF16), keys_t, preferred_element_type=F32)

    bias = jnp.concatenate([jnp.broadcast_to(bias1_ref[...], (bt,) + bias1_ref.shape), bias2_ref[...]], axis=2)
    s = _biased_scores(s_ref[...], bias)
    sink = sinkv_ref[...][None]
    m = jnp.maximum(jnp.max(s, axis=-1, keepdims=True), sink)
    p_ref[...] = jnp.exp(s - jnp.concatenate([m] * (s.shape[-1] // LANES), axis=-1)).astype(BF16)
    sink_term = jnp.exp(sink - m)

    vnt_ones = jnp.concatenate([vnt, jnp.ones((LANES, tile_rows), BF16)], axis=0)
    for b in range(bt):
        cvt = cvt_ref[b]
        nv_ref[b] = slid(cvt, vnt, b)
        cvt_ones = jnp.concatenate([cvt.astype(BF16), jnp.ones((LANES, wc), BF16)], axis=0)
        oh_ref[b] = lax.dot_general(p_ref[b], jnp.concatenate([cvt_ones, vnt_ones], axis=1),
                                    (((1,), (1,)), ((), ())), preferred_element_type=F32)

    inv = 1.0 / (oh_ref[:, :, D_KV:] + sink_term)
    outs = []
    for h in range(N_HEADS):
        rows = slice(h * steps, (h + 1) * steps)
        outs.append((oh_ref[:, rows, _group_columns(h)] * inv[:, rows, :HEAD_DIM]).reshape(tile_rows, HEAD_DIM))
    o_ref[...] = jnp.concatenate(outs, axis=1).astype(o_ref.dtype)


def _attn_sample(z, cache_kt, cache_vt, rel_bias, sinks, batch, steps):
    wc = cache_kt.shape[2]
    bt = SAMPLE_BATCH_TILE
    rows = bt * steps
    smem = pl.BlockSpec(memory_space=pltpu.SMEM)
    cache_t_spec = pl.BlockSpec((bt, D_KV, wc), lambda i: (i, 0, 0))
    return pl.pallas_call(
        functools.partial(_attn_sample_kernel, steps=steps),
        grid=(batch // bt,),
        in_specs=[
            smem,
            smem,
            pl.BlockSpec((rows, D_ATTN), lambda i: (i, OFF_Q // D_ATTN)),
            pl.BlockSpec((rows, D_KV), lambda i: (i, OFF_K // D_KV)),
            pl.BlockSpec((rows, D_KV), lambda i: (i, OFF_V // D_KV)),
            cache_t_spec,
            cache_t_spec,
        ],
        out_specs=[
            pl.BlockSpec((rows, D_ATTN), lambda i: (i, 0)),
            cache_t_spec,
            cache_t_spec,
        ],
        out_shape=[
            jax.ShapeDtypeStruct((batch * steps, D_ATTN), BF16),
            jax.ShapeDtypeStruct((batch, D_KV, wc), F32),
            jax.ShapeDtypeStruct((batch, D_KV, wc), F32),
        ],
        scratch_shapes=[
            pltpu.VMEM((N_HEADS * steps, wc), F32),
            pltpu.VMEM((bt, N_HEADS * steps, rows), F32),
            pltpu.VMEM((N_HEADS * steps, LANES), F32),
            pltpu.VMEM((bt, N_HEADS * steps, D_KV), F32),
            pltpu.VMEM((bt, N_HEADS * steps, wc + rows), F32),
            pltpu.VMEM((bt, N_HEADS * steps, wc + rows), BF16),
            pltpu.VMEM((bt, N_HEADS * steps, D_KV + LANES), F32),
        ],
        compiler_params=_params("arbitrary"),
        name="attn_sample",
    )(rel_bias, sinks, z, z, z, cache_kt, cache_vt)


CARRY_ROWS = 8


def _gated_conv(b, wc_ref, u, u1, u2):
    conv = wc_ref[0:1, :] * u2 + wc_ref[1:2, :] * u1 + wc_ref[2:3, :] * u
    return (b.astype(F32) * conv).astype(BF16)


def _mix_tail(conv_branch_input, at_ref, gc_ref, ga_ref, x_ref, g_ref, wco_ref, wao_ref, wo_ref, h_ref, hn_ref):
    y_conv = jnp.dot(conv_branch_input(), wco_ref[...], preferred_element_type=F32)
    y_attn = jnp.dot(at_ref[...], wao_ref[...], preferred_element_type=F32)
    merged = (jax.nn.sigmoid(gc_ref[...].astype(F32)) * y_conv
              + jax.nn.sigmoid(ga_ref[...].astype(F32)) * y_attn)
    h = x_ref[...] + jnp.dot(merged.astype(BF16), wo_ref[...], preferred_element_type=F32)
    h_ref[...] = h
    hn_ref[...] = _rms_rows(h, g_ref[0:1, :]).astype(hn_ref.dtype)


def _mix_prompt_kernel(b_ref, c_ref, hc_ref, wc_ref, at_ref, gc_ref, ga_ref, x_ref, g_ref,
                       wco_ref, wao_ref, wo_ref, side0_ref, side1_ref, side2_ref,
                       h_ref, hn_ref, st_ref, side0_bf16_ref, side1_bf16_ref, side2_bf16_ref,
                       us_ref, *, tiles_per_seq):
    tm = x_ref.shape[0]
    side0_bf16_ref[...] = side0_ref[...].astype(BF16)
    side1_bf16_ref[...] = side1_ref[...].astype(BF16)
    side2_bf16_ref[...] = side2_ref[...].astype(BF16)

    def conv_branch_input():
        u = c_ref[...].astype(F32) * hc_ref[...].astype(F32)
        st_ref[...] = u[tm - (CONV_WIDTH - 1):, :]
        carry = jnp.where(pl.program_id(0) % tiles_per_seq == 0, 0.0, us_ref[...])
        u_ext = jnp.concatenate([carry, u], axis=0)
        us_ref[...] = u[tm - CARRY_ROWS:, :]
        return _gated_conv(b_ref[...], wc_ref, u,
                           pltpu.roll(u_ext, 1, 0)[CARRY_ROWS:, :], pltpu.roll(u_ext, 2, 0)[CARRY_ROWS:, :])

    _mix_tail(conv_branch_input, at_ref, gc_ref, ga_ref, x_ref, g_ref, wco_ref, wao_ref, wo_ref, h_ref, hn_ref)


def _mix_sample_kernel(b_ref, c_ref, hc_ref, st_ref, wc_ref, at_ref, gc_ref, ga_ref, x_ref, g_ref,
                       wco_ref, wao_ref, wo_ref, h_ref, hn_ref, new_st_ref, us_ref, *, steps):
    tm = x_ref.shape[0]
    nseq = tm // steps
    keep = CONV_WIDTH - 1

    def conv_branch_input():
        u = c_ref[...].astype(F32) * hc_ref[...].astype(F32)
        us_ref[:, CARRY_ROWS - keep:CARRY_ROWS, :] = st_ref[...]
        us_ref[:, CARRY_ROWS:, :] = u.reshape(nseq, steps, D_CONV)
        new_st_ref[...] = us_ref[:, CARRY_ROWS + steps - keep:, :]

        def delayed(d):
            return us_ref[:, CARRY_ROWS - d:CARRY_ROWS - d + steps, :].reshape(tm, D_CONV)

        return _gated_conv(b_ref[...], wc_ref, u, delayed(1), delayed(2))

    _mix_tail(conv_branch_input, at_ref, gc_ref, ga_ref, x_ref, g_ref, wco_ref, wao_ref, wo_ref, h_ref, hn_ref)


def _mix_specs(tm):
    resident = functools.partial(pl.BlockSpec, index_map=lambda i: (0, 0), pipeline_mode=pl.Buffered(1))
    row_spec = pl.BlockSpec((tm, D_MODEL), lambda i: (i, 0))

    def zcol(off, width):
        return pl.BlockSpec((tm, width), lambda i: (i, off // width))

    conv_in = [zcol(OFF_B, D_CONV), zcol(OFF_C, D_CONV), zcol(OFF_H, D_CONV)]
    rest_in = [
        pl.BlockSpec((CONV_WIDTH, D_CONV), lambda i: (0, 0)),
        pl.BlockSpec((tm, D_ATTN), lambda i: (i, 0)),
        zcol(OFF_GC, D_MODEL),
        zcol(OFF_GA, D_MODEL),
        row_spec,
        pl.BlockSpec((1, D_MODEL),lambda i: (0, 0)),
        resident((D_CONV, D_MODEL)),
        resident((D_ATTN, D_MODEL)),
        resident((D_MODEL, D_MODEL)),
    ]
    return conv_in, rest_in, row_spec


def _mix_prompt(z, attn, x, w_conv, g_ffn, wco, wao, wo, sides, tm, batch, seq):
    m = x.shape[0]
    assert seq % tm == 0 and tm >= CARRY_ROWS
    tiles_per_seq = seq // tm
    nsteps = m // tm
    conv_in, rest_in, row_spec = _mix_specs(tm)

    def side_spec(side):
        rows = _side_cast_rows(side.shape[0], nsteps)
        return pl.BlockSpec((rows, side.shape[1]), lambda i: (jnp.minimum(i, side.shape[0] // rows - 1), 0))

    side_specs = [side_spec(s) for s in sides]
    return pl.pallas_call(
        functools.partial(_mix_prompt_kernel, tiles_per_seq=tiles_per_seq),
        grid=(nsteps,),
        in_specs=conv_in + rest_in + side_specs,
        out_specs=[row_spec, row_spec,
                   pl.BlockSpec((None, CONV_WIDTH - 1, D_CONV), lambda i: (i // tiles_per_seq, 0, 0))] + side_specs,
        out_shape=[jax.ShapeDtypeStruct((m, D_MODEL), F32), jax.ShapeDtypeStruct((m, D_MODEL), BF16),
                   jax.ShapeDtypeStruct((batch, CONV_WIDTH - 1, D_CONV), F32)]
        + [jax.ShapeDtypeStruct(s.shape, BF16) for s in sides],
        scratch_shapes=[pltpu.VMEM((CARRY_ROWS, D_CONV), F32)],
        compiler_params=_params("arbitrary"),
        name="mix_prompt",
    )(z, z, z, w_conv, attn, z, z, x, g_ffn, wco, wao, wo, *sides)


def _mix_sample(z, state, attn, x, w_conv, g_ffn, wco, wao, wo, tm, steps):
    m = x.shape[0]
    assert tm % steps == 0 and CONV_WIDTH - 1 <= min(steps, CARRY_ROWS)
    conv_in, rest_in, row_spec = _mix_specs(tm)
    state_spec = pl.BlockSpec((tm // steps, CONV_WIDTH - 1, D_CONV), lambda i: (i, 0, 0))
    return pl.pallas_call(
        functools.partial(_mix_sample_kernel, steps=steps),
        grid=(m // tm,),
        in_specs=conv_in + [state_spec] + rest_in,
        out_specs=[row_spec, row_spec, state_spec],
        out_shape=[jax.ShapeDtypeStruct((m, D_MODEL), F32), jax.ShapeDtypeStruct((m, D_MODEL), BF16),
                   jax.ShapeDtypeStruct(state.shape, F32)],
        scratch_shapes=[pltpu.VMEM((tm // steps, CARRY_ROWS + steps, D_CONV), F32)],
        compiler_params=_params("arbitrary"),
        name="mix_sample",
    )(z, z, z, state, w_conv, attn, z, z, x, g_ffn, wco, wao, wo)


def _ffn_kernel(hn_ref, h_ref, gf_ref, wg_ref, wu_ref, wd_ref, y_ref, *, final_norm):
    j = pl.program_id(1)

    hn = hn_ref[...]
    a = (jax.nn.silu(jnp.dot(hn, wg_ref[...], preferred_element_type=F32))
         * jnp.dot(hn, wu_ref[...], preferred_element_type=F32))
    acc = jnp.where(j == 0, h_ref[...], y_ref[...])
    y_ref[...] = acc + jnp.dot(a.astype(BF16), wd_ref[...], preferred_element_type=F32)

    if final_norm:
        @pl.when(j == pl.num_programs(1) - 1)
        def _():
            def body(r, carry):
                rows = pl.ds(pl.multiple_of(r * NORM_ROWS, NORM_ROWS), NORM_ROWS)
                y_ref[rows, :] = _rms_rows(y_ref[rows, :], gf_ref[0:1, :])
                return carry

            lax.fori_loop(0, y_ref.shape[0] // NORM_ROWS, body, 0)


def _ffn(hn, h, g_final, wg, wu, wd, tm, tf, final_norm):
    m = h.shape[0]
    ahead = pl.Buffered(2, use_lookahead=True)
    pipeline = pltpu.emit_pipeline(
        functools.partial(_ffn_kernel, final_norm=final_norm),
        grid=(m // tm, D_FF // tf),
        in_specs=[
            pl.BlockSpec((tm, D_MODEL), lambda i, j: (i, 0), pipeline_mode=ahead),
            pl.BlockSpec((tm, D_MODEL), lambda i, j: (i, 0), pipeline_mode=ahead),
            pl.BlockSpec((1, D_MODEL),lambda i, j: (0, 0)),
            pl.BlockSpec((D_MODEL, tf), lambda i, j: (0, j)),
            pl.BlockSpec((D_MODEL, tf), lambda i, j: (0, j)),
            pl.BlockSpec((tf, D_MODEL), lambda i, j: (j, 0)),
        ],
        out_specs=[pl.BlockSpec((tm, D_MODEL), lambda i, j: (i, 0))],
    )
    in_hbm = pl.BlockSpec(memory_space=pl.ANY)
    return pl.pallas_call(
        lambda *refs: pipeline(*refs),
        in_specs=[in_hbm] * 6,
        out_specs=in_hbm,
        out_shape=jax.ShapeDtypeStruct((m, D_MODEL), F32),
        compiler_params=pltpu.CompilerParams(vmem_limit_bytes=VMEM_LIMIT),
        name="ffn",
    )(hn, h, g_final, wg, wu, wd)


IN_PROJ_TM_PROMPT = 2048
IN_PROJ_TN = 512
MIX_TM = 256
FFN_TM = 1024
FFN_TF = 512


def kernel(x_prompt, x_sample, cache_k, cache_v, state_conv, rel_bias, w_in, w_conv, w_conv_out, sinks,
           w_attn_out, w_o, g_mix, g_ffn, w_gate, w_up, w_down, g_final):
    depth = w_in.shape[0]
    batch, seq, _ = x_prompt.shape
    dec_batch, steps, _ = x_sample.shape
    wc = cache_k.shape[2]
    assert seq % WINDOW == 0 and wc == WINDOW and dec_batch % SAMPLE_BATCH_TILE == 0

    hp = x_prompt.reshape(batch * seq, D_MODEL)
    hs = x_sample.reshape(dec_batch * steps, D_MODEL)
    g_final2 = g_final.reshape(1, D_MODEL)
    kp_l, vp_l, cp_l, ks_l, vs_l, cs_l = [], [], [], [], [], []
    for l in range(depth):
        final = l == depth - 1
        lw = {
            "w_in": w_in[l],
            "g_ffn": g_ffn[l].reshape(1, D_MODEL),
        }
        g_mix_l = g_mix[l].reshape(1, D_MODEL)

        zp, lw["wo"] = _in_proj(hp, g_mix_l, lw["w_in"], w_o[l], IN_PROJ_TM_PROMPT, IN_PROJ_TN)
        zs, attn_p, lw["wco"], lw["wao"], kp_t, vp_t = _in_proj_sample_attn_prompt(
            hs, g_mix_l, lw["w_in"], w_conv_out[l], w_attn_out[l], zp, rel_bias, sinks[l], IN_PROJ_TN, batch, seq)
        hp, hnp, conv_p, lw["wg"], lw["wu"], lw["wd"] = _mix_prompt(
            zp, attn_p, hp, w_conv[l], lw["g_ffn"], lw["wco"], lw["wao"], lw["wo"],
            (w_gate[l], w_up[l], w_down[l]), MIX_TM, batch, seq)
        hp = _ffn(hnp, hp, g_final2, lw["wg"], lw["wu"], lw["wd"], FFN_TM, FFN_TF, final)
        kp_l.append(jnp.transpose(kp_t.reshape(batch, N_KV_HEADS, HEAD_DIM, WINDOW), (0, 3, 1, 2)))
        vp_l.append(jnp.transpose(vp_t.reshape(batch, N_KV_HEADS, HEAD_DIM, WINDOW), (0, 3, 1, 2)))
        cp_l.append(conv_p)

        cache_kt = jnp.transpose(cache_k[l], (0, 2, 3, 1)).reshape(dec_batch, D_KV, wc)
        cache_vt = jnp.transpose(cache_v[l], (0, 2, 3, 1)).reshape(dec_batch, D_KV, wc)
        attn_s, nk, nv = _attn_sample(zs, cache_kt, cache_vt, rel_bias, sinks[l], dec_batch, steps)
        hs, hns, conv_s = _mix_sample(zs, state_conv[l], attn_s, hs, w_conv[l], lw["g_ffn"], lw["wco"], lw["wao"],
                                      lw["wo"], MIX_TM, steps)
        hs = _ffn(hns, hs, g_final2, lw["wg"], lw["wu"], lw["wd"], FFN_TM, FFN_TF, final)
        ks_l.append(jnp.transpose(nk.reshape(dec_batch, N_KV_HEADS, HEAD_DIM, wc), (0, 3, 1, 2)))
        vs_l.append(jnp.transpose(nv.reshape(dec_batch, N_KV_HEADS, HEAD_DIM, wc), (0, 3, 1, 2)))
        cs_l.append(conv_s)

    return (hp.reshape(batch, seq, D_MODEL), hs.reshape(dec_batch, steps, D_MODEL),
            jnp.stack(kp_l), jnp.stack(vp_l), jnp.stack(cp_l),
            jnp.stack(ks_l), jnp.stack(vs_l), jnp.stack(cs_l))
```
